```python
import jax, jax.numpy as jnp
from jax import lax
import numpy as np

D_MODEL = 1024
BATCH = 8
SEQ = 8192
DEPTH = 2

CHUNK = 64
PLE_DIM = 256
D_FF = 2816
CONV_W = 4
RET_HEADS = 4
RET_DK = 64
RET_DV = 64
RET_W = RET_HEADS * RET_DV
LRU_W = 384
LRU_BLOCKS = 6
LRU_BLOCK = LRU_W // LRU_BLOCKS
LRU_C = 8.0
GDN_HEADS = 6
GDN_DK = 64
GDN_DV = 64
GDN_W = GDN_HEADS * GDN_DV
D_MIX = RET_W + LRU_W + GDN_W
IN_WIDTHS = (RET_W, RET_W, RET_W, RET_W, LRU_W, LRU_W, GDN_W, GDN_W, GDN_W, GDN_W, GDN_HEADS, GDN_HEADS)
D_IN = sum(IN_WIDTHS)
ROPE_THETA = 10000.0
ALPHA = (2 * DEPTH) ** 0.25
BETA_INIT = (8 * DEPTH) ** -0.25
LN_EPS = 1e-5

kernel_name = 'hybrid_retention_rglru_gdn_macaron_deepnorm'


def layer_norm(x, g, b):
    xf = x.astype(jnp.float32)
    mu = jnp.mean(xf, -1, keepdims=True)
    var = jnp.mean(jnp.square(xf - mu), -1, keepdims=True)
    return ((xf - mu) * lax.rsqrt(var + LN_EPS) * g + b).astype(x.dtype)


def head_norm(t, eps=1e-5):
    mu = jnp.mean(t, -1, keepdims=True)
    return (t - mu) * lax.rsqrt(jnp.var(t, -1, keepdims=True) + eps)


def rms_norm(t, g, eps=1e-6):
    return t * lax.rsqrt(jnp.mean(t * t, -1, keepdims=True) + eps) * g


def l2_normalize(t, eps=1e-6):
    return t * lax.rsqrt(jnp.sum(t * t, -1, keepdims=True) + eps)


def swiglu(x, w_gate, w_up, w_down):
    return (jax.nn.silu(x @ w_gate) * (x @ w_up)) @ w_down


def causal_depthwise_conv(x, w, b=None):
    K = w.shape[0]
    S = x.shape[1]
    xp = jnp.pad(x, ((0, 0), (K - 1, 0), (0, 0)))
    y = sum(w[k] * xp[:, k:k + S] for k in range(K))
    return y if b is None else y + b


def rotary(t, positions):
    half = t.shape[-1] // 2
    inv_freq = ROPE_THETA ** (-jnp.arange(half, dtype=jnp.float32) / half)
    ang = positions.astype(jnp.float32)[..., None] * inv_freq
    cos = jnp.cos(ang)[:, :, None, :]
    sin = jnp.sin(ang)[:, :, None, :]
    t1, t2 = t[..., :half], t[..., half:]
    return jnp.concatenate([t1 * cos - t2 * sin, t2 * cos + t1 * sin], -1)


def retention_chunked(q, k, v, positions):
    B, S, H, _ = q.shape
    N = S // CHUNK
    q = rotary(q, positions) * RET_DK ** -0.5
    k = rotary(k, positions)
    log_gamma = jnp.log1p(-jnp.exp2(-5.0 - jnp.arange(H, dtype=jnp.float32)))
    idx = jnp.arange(CHUNK, dtype=jnp.float32)
    intra = jnp.exp(jnp.abs(idx[:, None] - idx[None, :])[None] * log_gamma[:, None, None])
    cross = jnp.exp((idx + 1.0)[None] * log_gamma[:, None])
    tail = jnp.exp((CHUNK - 1.0 - idx)[None] * log_gamma[:, None])
    chunk_decay = jnp.exp(CHUNK * log_gamma)
    qc = q.reshape(B, N, CHUNK, H, RET_DK)
    kc = k.reshape(B, N, CHUNK, H, RET_DK)
    vc = v.reshape(B, N, CHUNK, H, RET_DV)
    scores = jnp.einsum('bnihd,bnjhd->bnhij', qc, kc) * intra
    o_intra = jnp.einsum('bnhij,bnjhe->bnihe', scores, vc)
    kv = jnp.einsum('bnjhd,hj,bnjhe->nbhde', kc, tail, vc)

    def step(state, kv_n):
        return state * chunk_decay[None, :, None, None] + kv_n, state

    _, states = lax.scan(step, jnp.zeros((B, H, RET_DK, RET_DV), jnp.float32), kv)
    o_cross = jnp.einsum('bnihd,hi,nbhde->bnihe', qc, cross, states)
    return (o_intra + o_cross).reshape(B, S, H, RET_DV)


def rg_lru(x, conv_w, conv_b, w_a, b_a, w_x, b_x, lam):
    B, S, _ = x.shape
    x = causal_depthwise_conv(x, conv_w, conv_b)
    xb = x.reshape(B, S, LRU_BLOCKS, LRU_BLOCK)
    r = jax.nn.sigmoid(jnp.einsum('bsgi,gij->bsgj', xb, w_a).reshape(B, S, LRU_W) + b_a)
    i = jax.nn.sigmoid(jnp.einsum('bsgi,gij->bsgj', xb, w_x).reshape(B, S, LRU_W) + b_x)
    log_a = LRU_C * r * jax.nn.log_sigmoid(lam)
    a = jnp.exp(log_a)
    b = jnp.sqrt(-jnp.expm1(2.0 * log_a)) * (i * x)

    def combine(left, right):
        a1, b1 = left
        a2, b2 = right
        return a1 * a2, a2 * b1 + b2

    _, h = lax.associative_scan(combine, (a, b), axis=1)
    return h


def gated_delta_chunked(q, k, v, g, beta):
    B, S, H, DK = q.shape
    DV = v.shape[-1]
    N = S // CHUNK

    def to_chunks(t):
        return jnp.moveaxis(t.reshape(B, N, CHUNK, H, *t.shape[3:]), 3, 2)

    q = to_chunks(q) * DK ** -0.5
    k = to_chunks(k)
    v = to_chunks(v)
    g = to_chunks(g)
    beta = to_chunks(beta)
    gc = jnp.cumsum(g, axis=-1)
    incl = jnp.tril(jnp.ones((CHUNK, CHUNK), bool))
    strict = jnp.tril(jnp.ones((CHUNK, CHUNK), bool), -1)
    decay = jnp.where(incl, jnp.exp(jnp.minimum(gc[..., :, None] - gc[..., None, :], 0.0)), 0.0)
    kk = jnp.einsum('bnhid,bnhjd->bnhij', k, k)
    a_mat = jnp.where(strict, beta[..., :, None] * kk * decay, 0.0) + jnp.eye(CHUNK, dtype=jnp.float32)
    rhs = jnp.concatenate([v * beta[..., None], k * (beta * jnp.exp(gc))[..., None]], -1)
    sol = lax.linalg.triangular_solve(a_mat, rhs, left_side=True, lower=True, unit_diagonal=True)
    u, w = sol[..., :DV], sol[..., DV:]
    qk = jnp.einsum('bnhid,bnhjd->bnhij', q, k) * decay
    q_dec = q * jnp.exp(gc)[..., None]
    g_last = gc[..., -1]
    k_tail = k * jnp.exp(g_last[..., None] - gc)[..., None]
    xs = (jnp.moveaxis(u, 1, 0), jnp.moveaxis(w, 1, 0), jnp.moveaxis(qk, 1, 0),
          jnp.moveaxis(q_dec, 1, 0), jnp.moveaxis(k_tail, 1, 0), jnp.moveaxis(g_last, 1, 0))

    def step(state, inp):
        u_n, w_n, qk_n, qd_n, kt_n, gl_n = inp
        v_new = u_n - jnp.einsum('bhcd,bhde->bhce', w_n, state)
        o = jnp.einsum('bhcd,bhde->bhce', qd_n, state) + jnp.einsum('bhij,bhje->bhie', qk_n, v_new)
        state = state * jnp.exp(gl_n)[..., None, None] + jnp.einsum('bhcd,bhce->bhde', kt_n, v_new)
        return state, o

    _, o = lax.scan(step, jnp.zeros((B, H, DK, DV), jnp.float32), xs)
    return jnp.transpose(o, (1, 0, 3, 2, 4)).reshape(B, S, H, DV)


def hybrid_mixer(x, positions, w_in, ret_norm_g, lru_conv_w, lru_conv_b, lru_w_a, lru_b_a,
                 lru_w_x, lru_b_x, lru_lambda, gdn_conv_w, gdn_a_log, gdn_dt_bias, gdn_norm_g, w_out):
    B, S, _ = x.shape
    h = (x @ w_in).astype(jnp.float32)
    splits = np.cumsum(IN_WIDTHS)[:-1].tolist()
    q_r, k_r, v_r, g_r, x_l, gate_l, q_g, k_g, v_g, z_g, a_g, b_g = jnp.split(h, splits, axis=-1)

    o_r = retention_chunked(q_r.reshape(B, S, RET_HEADS, RET_DK), k_r.reshape(B, S, RET_HEADS, RET_DK),
                            v_r.reshape(B, S, RET_HEADS, RET_DV), positions)
    o_r = head_norm(o_r).reshape(B, S, RET_W) * ret_norm_g * jax.nn.silu(g_r)

    o_l = rg_lru(x_l, lru_conv_w, lru_conv_b, lru_w_a, lru_b_a, lru_w_x, lru_b_x, lru_lambda) * jax.nn.gelu(gate_l)

    qkv = jax.nn.silu(causal_depthwise_conv(jnp.concatenate([q_g, k_g, v_g], -1), gdn_conv_w))
    q_g, k_g, v_g = jnp.split(qkv, 3, axis=-1)
    q_g = l2_normalize(q_g.reshape(B, S, GDN_HEADS, GDN_DK))
    k_g = l2_normalize(k_g.reshape(B, S, GDN_HEADS, GDN_DK))
    v_g = v_g.reshape(B, S, GDN_HEADS, GDN_DV)
    beta = jax.nn.sigmoid(b_g)
    g = -jnp.exp(gdn_a_log) * jax.nn.softplus(a_g + gdn_dt_bias)
    o_g = gated_delta_chunked(q_g, k_g, v_g, g, beta)
    o_g = (rms_norm(o_g, gdn_norm_g) * jax.nn.silu(z_g.reshape(B, S, GDN_HEADS, GDN_DV))).reshape(B, S, GDN_W)

    o = jnp.concatenate([o_r, o_l, o_g], -1).astype(x.dtype)
    return o @ w_out


def _fwd_setup_inputs(seed: int = 0) -> dict:
    key = jax.random.key(seed)
    counter = [0]

    def nk():
        counter[0] += 1
        return jax.random.fold_in(key, counter[0])

    f32 = jnp.float32
    L = DEPTH

    def nrm(shape, fan_in, scale=1.0):
        return jax.random.normal(nk(), shape, f32) * (scale * fan_in ** -0.5)

    def gain(shape):
        return 1.0 + 0.02 * jax.random.normal(nk(), shape, f32)

    def bias(shape):
        return 0.02 * jax.random.normal(nk(), shape, f32)

    x = jax.random.normal(nk(), (BATCH, SEQ, D_MODEL), f32)
    p = jax.random.normal(nk(), (DEPTH, BATCH, SEQ, PLE_DIM), f32)
    start = jax.random.randint(nk(), (BATCH, 1), 0, 4096, jnp.int32)
    positions = start + jnp.arange(SEQ, dtype=jnp.int32)[None, :]
    a0 = jax.random.uniform(nk(), (L, LRU_W), f32, 0.9, 0.999)
    lru_lambda = jnp.log(a0) - jnp.log1p(-a0)
    gdn_a_log = jnp.log(jax.random.uniform(nk(), (L, GDN_HEADS), f32, 1.0, 16.0))
    dt = jnp.exp(jax.random.uniform(nk(), (L, GDN_HEADS), f32, np.log(1e-3), np.log(1e-1)))
    gdn_dt_bias = dt + jnp.log(-jnp.expm1(-dt))
    return {
        'x': x,
        'p': p,
        'positions': positions,
        'ln_ffn1_g': gain((L, D_MODEL)),
        'ln_ffn1_b': bias((L, D_MODEL)),
        'ffn1_w_gate': nrm((L, D_MODEL, D_FF), D_MODEL),
        'ffn1_w_up': nrm((L, D_MODEL, D_FF), D_MODEL),
        'ffn1_w_down': nrm((L, D_FF, D_MODEL), D_FF, BETA_INIT),
        'w_in': nrm((L, D_MODEL, D_IN), D_MODEL),
        'ret_norm_g': gain((L, RET_W)),
        'lru_conv_w': nrm((L, CONV_W, LRU_W), CONV_W),
        'lru_conv_b': bias((L, LRU_W)),
        'lru_w_a': nrm((L, LRU_BLOCKS, LRU_BLOCK, LRU_BLOCK), LRU_BLOCK),
        'lru_b_a': bias((L, LRU_W)),
        'lru_w_x': nrm((L, LRU_BLOCKS, LRU_BLOCK, LRU_BLOCK), LRU_BLOCK),
        'lru_b_x': bias((L, LRU_W)),
        'lru_lambda': lru_lambda,
        'gdn_conv_w': nrm((L, CONV_W, 3 * GDN_W), CONV_W),
        'gdn_a_log': gdn_a_log,
        'gdn_dt_bias': gdn_dt_bias,
        'gdn_norm_g': gain((L, GDN_DV)),
        'w_out': nrm((L, D_MIX, D_MODEL), D_MIX, BETA_INIT),
        'ln_mix_g': gain((L, D_MODEL)),
        'ln_mix_b': bias((L, D_MODEL)),
        'ffn2_w_gate': nrm((L, D_MODEL, D_FF), D_MODEL),
        'ffn2_w_up': nrm((L, D_MODEL, D_FF), D_MODEL),
        'ffn2_w_down': nrm((L, D_FF, D_MODEL), D_FF, BETA_INIT),
        'ple_w_gate': nrm((L, D_MODEL, D_MODEL), D_MODEL),
        'ple_w_proj': nrm((L, PLE_DIM, D_MODEL), PLE_DIM, BETA_INIT),
        'ln_ffn2_g': gain((L, D_MODEL)),
        'ln_ffn2_b': bias((L, D_MODEL)),
    }


def _fwd_reference(x, p, positions, ln_ffn1_g, ln_ffn1_b, ffn1_w_gate, ffn1_w_up, ffn1_w_down,
              w_in, ret_norm_g, lru_conv_w, lru_conv_b, lru_w_a, lru_b_a, lru_w_x, lru_b_x, lru_lambda,
              gdn_conv_w, gdn_a_log, gdn_dt_bias, gdn_norm_g, w_out, ln_mix_g, ln_mix_b,
              ffn2_w_gate, ffn2_w_up, ffn2_w_down, ple_w_gate, ple_w_proj, ln_ffn2_g, ln_ffn2_b):
    for i in range(DEPTH):
        x = layer_norm(ALPHA * x + 0.5 * swiglu(x, ffn1_w_gate[i], ffn1_w_up[i], ffn1_w_down[i]),
                       ln_ffn1_g[i], ln_ffn1_b[i])
        mix = hybrid_mixer(x, positions, w_in[i], ret_norm_g[i], lru_conv_w[i], lru_conv_b[i],
                           lru_w_a[i], lru_b_a[i], lru_w_x[i], lru_b_x[i], lru_lambda[i],
                           gdn_conv_w[i], gdn_a_log[i], gdn_dt_bias[i], gdn_norm_g[i], w_out[i])
        x = layer_norm(ALPHA * x + mix, ln_mix_g[i], ln_mix_b[i])
        ffn = 0.5 * swiglu(x, ffn2_w_gate[i], ffn2_w_up[i], ffn2_w_down[i])
        ple = jax.nn.sigmoid(x @ ple_w_gate[i]) * (p[i] @ ple_w_proj[i])
        x = layer_norm(ALPHA * x + ffn + ple, ln_ffn2_g[i], ln_ffn2_b[i])
    return x


import jax as _jax
import jax.numpy as _jnp

TWIN_FORMAT = 'train_step'
FWD_PARAMS = ['x', 'p', 'positions', 'ln_ffn1_g', 'ln_ffn1_b', 'ffn1_w_gate', 'ffn1_w_up', 'ffn1_w_down', 'w_in', 'ret_norm_g', 'lru_conv_w', 'lru_conv_b', 'lru_w_a', 'lru_b_a', 'lru_w_x', 'lru_b_x', 'lru_lambda', 'gdn_conv_w', 'gdn_a_log', 'gdn_dt_bias', 'gdn_norm_g', 'w_out', 'ln_mix_g', 'ln_mix_b', 'ffn2_w_gate', 'ffn2_w_up', 'ffn2_w_down', 'ple_w_gate', 'ple_w_proj', 'ln_ffn2_g', 'ln_ffn2_b']
TWIN_WEIGHTS = ['ln_ffn1_g', 'ln_ffn1_b', 'ffn1_w_gate', 'ffn1_w_up', 'ffn1_w_down', 'w_in', 'ret_norm_g', 'lru_conv_w', 'lru_conv_b', 'lru_w_a', 'lru_b_a', 'lru_w_x', 'lru_b_x', 'lru_lambda', 'gdn_conv_w', 'gdn_a_log', 'gdn_dt_bias', 'gdn_norm_g', 'w_out', 'ln_mix_g', 'ln_mix_b', 'ffn2_w_gate', 'ffn2_w_up', 'ffn2_w_down', 'ple_w_gate', 'ple_w_proj', 'ln_ffn2_g', 'ln_ffn2_b']
TWIN_DIFF_INPUT = 'x'
TWIN_INPUTS = ['x', 'p', 'positions', 'ln_ffn1_g', 'ln_ffn1_b', 'ffn1_w_gate', 'ffn1_w_up', 'ffn1_w_down', 'w_in', 'ret_norm_g', 'lru_conv_w', 'lru_conv_b', 'lru_w_a', 'lru_b_a', 'lru_w_x', 'lru_b_x', 'lru_lambda', 'gdn_conv_w', 'gdn_a_log', 'gdn_dt_bias', 'gdn_norm_g', 'w_out', 'ln_mix_g', 'ln_mix_b', 'ffn2_w_gate', 'ffn2_w_up', 'ffn2_w_down', 'ple_w_gate', 'ple_w_proj', 'ln_ffn2_g', 'ln_ffn2_b', 'loss_target', 'm_ln_ffn1_g', 'm_ln_ffn1_b', 'm_ffn1_w_gate', 'm_ffn1_w_up', 'm_ffn1_w_down', 'm_w_in', 'm_ret_norm_g', 'm_lru_conv_w', 'm_lru_conv_b', 'm_lru_w_a', 'm_lru_b_a', 'm_lru_w_x', 'm_lru_b_x', 'm_lru_lambda', 'm_gdn_conv_w', 'm_gdn_a_log', 'm_gdn_dt_bias', 'm_gdn_norm_g', 'm_w_out', 'm_ln_mix_g', 'm_ln_mix_b', 'm_ffn2_w_gate', 'm_ffn2_w_up', 'm_ffn2_w_down', 'm_ple_w_gate', 'm_ple_w_proj', 'm_ln_ffn2_g', 'm_ln_ffn2_b', 'v_ln_ffn1_g', 'v_ln_ffn1_b', 'v_ffn1_w_gate', 'v_ffn1_w_up', 'v_ffn1_w_down', 'v_w_in', 'v_ret_norm_g', 'v_lru_conv_w', 'v_lru_conv_b', 'v_lru_w_a', 'v_lru_b_a', 'v_lru_w_x', 'v_lru_b_x', 'v_lru_lambda', 'v_gdn_conv_w', 'v_gdn_a_log', 'v_gdn_dt_bias', 'v_gdn_norm_g', 'v_w_out', 'v_ln_mix_g', 'v_ln_mix_b', 'v_ffn2_w_gate', 'v_ffn2_w_up', 'v_ffn2_w_down', 'v_ple_w_gate', 'v_ple_w_proj', 'v_ln_ffn2_g', 'v_ln_ffn2_b']
TWIN_OUTPUTS = ['loss', 'grad_x', 'grad_ln_ffn1_g', 'grad_ln_ffn1_b', 'grad_ffn1_w_gate', 'grad_ffn1_w_up', 'grad_ffn1_w_down', 'grad_w_in', 'grad_ret_norm_g', 'grad_lru_conv_w', 'grad_lru_conv_b', 'grad_lru_w_a', 'grad_lru_b_a', 'grad_lru_w_x', 'grad_lru_b_x', 'grad_lru_lambda', 'grad_gdn_conv_w', 'grad_gdn_a_log', 'grad_gdn_dt_bias', 'grad_gdn_norm_g', 'grad_w_out', 'grad_ln_mix_g', 'grad_ln_mix_b', 'grad_ffn2_w_gate', 'grad_ffn2_w_up', 'grad_ffn2_w_down', 'grad_ple_w_gate', 'grad_ple_w_proj', 'grad_ln_ffn2_g', 'grad_ln_ffn2_b', 'delta_ln_ffn1_g', 'delta_ln_ffn1_b', 'delta_ffn1_w_gate', 'delta_ffn1_w_up', 'delta_ffn1_w_down', 'delta_w_in', 'delta_ret_norm_g', 'delta_lru_conv_w', 'delta_lru_conv_b', 'delta_lru_w_a', 'delta_lru_b_a', 'delta_lru_w_x', 'delta_lru_b_x', 'delta_lru_lambda', 'delta_gdn_conv_w', 'delta_gdn_a_log', 'delta_gdn_dt_bias', 'delta_gdn_norm_g', 'delta_w_out', 'delta_ln_mix_g', 'delta_ln_mix_b', 'delta_ffn2_w_gate', 'delta_ffn2_w_up', 'delta_ffn2_w_down', 'delta_ple_w_gate', 'delta_ple_w_proj', 'delta_ln_ffn2_g', 'delta_ln_ffn2_b', 'new_m_ln_ffn1_g', 'new_m_ln_ffn1_b', 'new_m_ffn1_w_gate', 'new_m_ffn1_w_up', 'new_m_ffn1_w_down', 'new_m_w_in', 'new_m_ret_norm_g', 'new_m_lru_conv_w', 'new_m_lru_conv_b', 'new_m_lru_w_a', 'new_m_lru_b_a', 'new_m_lru_w_x', 'new_m_lru_b_x', 'new_m_lru_lambda', 'new_m_gdn_conv_w', 'new_m_gdn_a_log', 'new_m_gdn_dt_bias', 'new_m_gdn_norm_g', 'new_m_w_out', 'new_m_ln_mix_g', 'new_m_ln_mix_b', 'new_m_ffn2_w_gate', 'new_m_ffn2_w_up', 'new_m_ffn2_w_down', 'new_m_ple_w_gate', 'new_m_ple_w_proj', 'new_m_ln_ffn2_g', 'new_m_ln_ffn2_b', 'new_v_ln_ffn1_g', 'new_v_ln_ffn1_b', 'new_v_ffn1_w_gate', 'new_v_ffn1_w_up', 'new_v_ffn1_w_down', 'new_v_w_in', 'new_v_ret_norm_g', 'new_v_lru_conv_w', 'new_v_lru_conv_b', 'new_v_lru_w_a', 'new_v_lru_b_a', 'new_v_lru_w_x', 'new_v_lru_b_x', 'new_v_lru_lambda', 'new_v_gdn_conv_w', 'new_v_gdn_a_log', 'new_v_gdn_dt_bias', 'new_v_gdn_norm_g', 'new_v_w_out', 'new_v_ln_mix_g', 'new_v_ln_mix_b', 'new_v_ffn2_w_gate', 'new_v_ffn2_w_up', 'new_v_ffn2_w_down', 'new_v_ple_w_gate', 'new_v_ple_w_proj', 'new_v_ln_ffn2_g', 'new_v_ln_ffn2_b']
TWIN_LEAF_KINDS = {'loss': 'loss', 'grad_x': 'grad_x', 'grad_ln_ffn1_g': 'grad_w', 'grad_ln_ffn1_b': 'grad_w', 'grad_ffn1_w_gate': 'grad_w', 'grad_ffn1_w_up': 'grad_w', 'grad_ffn1_w_down': 'grad_w', 'grad_w_in': 'grad_w', 'grad_ret_norm_g': 'grad_w', 'grad_lru_conv_w': 'grad_w', 'grad_lru_conv_b': 'grad_w', 'grad_lru_w_a': 'grad_w', 'grad_lru_b_a': 'grad_w', 'grad_lru_w_x': 'grad_w', 'grad_lru_b_x': 'grad_w', 'grad_lru_lambda': 'grad_w', 'grad_gdn_conv_w': 'grad_w', 'grad_gdn_a_log': 'grad_w', 'grad_gdn_dt_bias': 'grad_w', 'grad_gdn_norm_g': 'grad_w', 'grad_w_out': 'grad_w', 'grad_ln_mix_g': 'grad_w', 'grad_ln_mix_b': 'grad_w', 'grad_ffn2_w_gate': 'grad_w', 'grad_ffn2_w_up': 'grad_w', 'grad_ffn2_w_down': 'grad_w', 'grad_ple_w_gate': 'grad_w', 'grad_ple_w_proj': 'grad_w', 'grad_ln_ffn2_g': 'grad_w', 'grad_ln_ffn2_b': 'grad_w', 'delta_ln_ffn1_g': 'delta_w', 'delta_ln_ffn1_b': 'delta_w', 'delta_ffn1_w_gate': 'delta_w', 'delta_ffn1_w_up': 'delta_w', 'delta_ffn1_w_down': 'delta_w', 'delta_w_in': 'delta_w', 'delta_ret_norm_g': 'delta_w', 'delta_lru_conv_w': 'delta_w', 'delta_lru_conv_b': 'delta_w', 'delta_lru_w_a': 'delta_w', 'delta_lru_b_a': 'delta_w', 'delta_lru_w_x': 'delta_w', 'delta_lru_b_x': 'delta_w', 'delta_lru_lambda': 'delta_w', 'delta_gdn_conv_w': 'delta_w', 'delta_gdn_a_log': 'delta_w', 'delta_gdn_dt_bias': 'delta_w', 'delta_gdn_norm_g': 'delta_w', 'delta_w_out': 'delta_w', 'delta_ln_mix_g': 'delta_w', 'delta_ln_mix_b': 'delta_w', 'delta_ffn2_w_gate': 'delta_w', 'delta_ffn2_w_up': 'delta_w', 'delta_ffn2_w_down': 'delta_w', 'delta_ple_w_gate': 'delta_w', 'delta_ple_w_proj': 'delta_w', 'delta_ln_ffn2_g': 'delta_w', 'delta_ln_ffn2_b': 'delta_w', 'new_m_ln_ffn1_g': 'new_m', 'new_m_ln_ffn1_b': 'new_m', 'new_m_ffn1_w_gate': 'new_m', 'new_m_ffn1_w_up': 'new_m', 'new_m_ffn1_w_down': 'new_m', 'new_m_w_in': 'new_m', 'new_m_ret_norm_g': 'new_m', 'new_m_lru_conv_w': 'new_m', 'new_m_lru_conv_b': 'new_m', 'new_m_lru_w_a': 'new_m', 'new_m_lru_b_a': 'new_m', 'new_m_lru_w_x': 'new_m', 'new_m_lru_b_x': 'new_m', 'new_m_lru_lambda': 'new_m', 'new_m_gdn_conv_w': 'new_m', 'new_m_gdn_a_log': 'new_m', 'new_m_gdn_dt_bias': 'new_m', 'new_m_gdn_norm_g': 'new_m', 'new_m_w_out': 'new_m', 'new_m_ln_mix_g': 'new_m', 'new_m_ln_mix_b': 'new_m', 'new_m_ffn2_w_gate': 'new_m', 'new_m_ffn2_w_up': 'new_m', 'new_m_ffn2_w_down': 'new_m', 'new_m_ple_w_gate': 'new_m', 'new_m_ple_w_proj': 'new_m', 'new_m_ln_ffn2_g': 'new_m', 'new_m_ln_ffn2_b': 'new_m', 'new_v_ln_ffn1_g': 'new_v', 'new_v_ln_ffn1_b': 'new_v', 'new_v_ffn1_w_gate': 'new_v', 'new_v_ffn1_w_up': 'new_v', 'new_v_ffn1_w_down': 'new_v', 'new_v_w_in': 'new_v', 'new_v_ret_norm_g': 'new_v', 'new_v_lru_conv_w': 'new_v', 'new_v_lru_conv_b': 'new_v', 'new_v_lru_w_a': 'new_v', 'new_v_lru_b_a': 'new_v', 'new_v_lru_w_x': 'new_v', 'new_v_lru_b_x': 'new_v', 'new_v_lru_lambda': 'new_v', 'new_v_gdn_conv_w': 'new_v', 'new_v_gdn_a_log': 'new_v', 'new_v_gdn_dt_bias': 'new_v', 'new_v_gdn_norm_g': 'new_v', 'new_v_w_out': 'new_v', 'new_v_ln_mix_g': 'new_v', 'new_v_ln_mix_b': 'new_v', 'new_v_ffn2_w_gate': 'new_v', 'new_v_ffn2_w_up': 'new_v', 'new_v_ffn2_w_down': 'new_v', 'new_v_ple_w_gate': 'new_v', 'new_v_ple_w_proj': 'new_v', 'new_v_ln_ffn2_g': 'new_v', 'new_v_ln_ffn2_b': 'new_v'}


def _forward(args):
    return _fwd_reference(*[args[k] for k in FWD_PARAMS])


def _output_shape():
    out = _jax.eval_shape(lambda: _forward(_fwd_setup_inputs(0)))
    return out.shape, out.dtype

N_MICROBATCH = 1
ADAM_LR = 0.001
ADAM_B1 = 0.9
ADAM_B2 = 0.999
ADAM_EPS = 1e-08
ADAM_WD = 0.01
ADAM_STEP = 10
PER_EXAMPLE_BATCH_AXIS = {'x': 0, 'p': 1, 'positions': 0, 'loss_target': 0}
SHARED_INPUTS = []
_WEIGHT_DTYPES = {'ln_ffn1_g': _jnp.float32, 'ln_ffn1_b': _jnp.float32, 'ffn1_w_gate': _jnp.float32, 'ffn1_w_up': _jnp.float32, 'ffn1_w_down': _jnp.float32, 'w_in': _jnp.float32, 'ret_norm_g': _jnp.float32, 'lru_conv_w': _jnp.float32, 'lru_conv_b': _jnp.float32, 'lru_w_a': _jnp.float32, 'lru_b_a': _jnp.float32, 'lru_w_x': _jnp.float32, 'lru_b_x': _jnp.float32, 'lru_lambda': _jnp.float32, 'gdn_conv_w': _jnp.float32, 'gdn_a_log': _jnp.float32, 'gdn_dt_bias': _jnp.float32, 'gdn_norm_g': _jnp.float32, 'w_out': _jnp.float32, 'ln_mix_g': _jnp.float32, 'ln_mix_b': _jnp.float32, 'ffn2_w_gate': _jnp.float32, 'ffn2_w_up': _jnp.float32, 'ffn2_w_down': _jnp.float32, 'ple_w_gate': _jnp.float32, 'ple_w_proj': _jnp.float32, 'ln_ffn2_g': _jnp.float32, 'ln_ffn2_b': _jnp.float32}
MOMENT_SCALE = {'ln_ffn1_g': 1.648656e+00, 'ln_ffn1_b': 7.227539e-01, 'ffn1_w_gate': 1.729982e-02, 'ffn1_w_up': 1.672198e-02, 'ffn1_w_down': 5.537944e-02, 'w_in': 4.937481e-02, 'ret_norm_g': 5.715962e-02, 'lru_conv_w': 3.523877e-02, 'lru_conv_b': 1.921516e-01, 'lru_w_a': 8.290471e-03, 'lru_b_a': 8.093973e-03, 'lru_w_x': 1.470073e-02, 'lru_b_x': 1.356320e-02, 'lru_lambda': 1.682982e-02, 'gdn_conv_w': 5.141812e-02, 'gdn_a_log': 3.341751e-01, 'gdn_dt_bias': 3.156368e-01, 'gdn_norm_g': 1.250204e-01, 'w_out': 9.594440e-02, 'ln_mix_g': 1.735535e+00, 'ln_mix_b': 7.384552e-01, 'ffn2_w_gate': 1.640147e-02, 'ffn2_w_up': 1.592111e-02, 'ffn2_w_down': 5.271708e-02, 'ple_w_gate': 1.875629e-02, 'ple_w_proj': 9.583924e-02, 'ln_ffn2_g': 4.540607e+01, 'ln_ffn2_b': 1.941277e+00}


def _to_microbatches(a, axis):
    t = _jnp.moveaxis(a, axis, 0)
    t = t.reshape((N_MICROBATCH, t.shape[0] // N_MICROBATCH) + t.shape[1:])
    return _jnp.moveaxis(t, 1, axis + 1)


def setup_inputs(seed: int = 0) -> dict:
    inp = _fwd_setup_inputs(seed)
    key = _jax.random.fold_in(_jax.random.key(seed), 7919)
    shape, _ = _output_shape()
    out = dict(inp)
    out["loss_target"] = _jax.random.normal(_jax.random.fold_in(key, 0), shape, _jnp.float32)
    for i, name in enumerate(TWIN_WEIGHTS):
        w = inp[name].astype(_jnp.float32)
        if MOMENT_SCALE is None:
            s = _jnp.sqrt(_jnp.mean(_jnp.square(w)) + 1e-30)
        else:
            s = MOMENT_SCALE[name]
        km, kv = _jax.random.split(_jax.random.fold_in(key, i + 1))
        out[name] = w
        out["m_" + name] = s * _jax.random.normal(km, w.shape, _jnp.float32)
        out["v_" + name] = (s * s) * _jax.random.uniform(kv, w.shape, _jnp.float32, 0.5, 1.5)
    if N_MICROBATCH > 1:
        for name, axis in PER_EXAMPLE_BATCH_AXIS.items():
            out[name] = _to_microbatches(out[name], axis)
    return {'x': out['x'], 'p': out['p'], 'positions': out['positions'], 'ln_ffn1_g': out['ln_ffn1_g'], 'ln_ffn1_b': out['ln_ffn1_b'], 'ffn1_w_gate': out['ffn1_w_gate'], 'ffn1_w_up': out['ffn1_w_up'], 'ffn1_w_down': out['ffn1_w_down'], 'w_in': out['w_in'], 'ret_norm_g': out['ret_norm_g'], 'lru_conv_w': out['lru_conv_w'], 'lru_conv_b': out['lru_conv_b'], 'lru_w_a': out['lru_w_a'], 'lru_b_a': out['lru_b_a'], 'lru_w_x': out['lru_w_x'], 'lru_b_x': out['lru_b_x'], 'lru_lambda': out['lru_lambda'], 'gdn_conv_w': out['gdn_conv_w'], 'gdn_a_log': out['gdn_a_log'], 'gdn_dt_bias': out['gdn_dt_bias'], 'gdn_norm_g': out['gdn_norm_g'], 'w_out': out['w_out'], 'ln_mix_g': out['ln_mix_g'], 'ln_mix_b': out['ln_mix_b'], 'ffn2_w_gate': out['ffn2_w_gate'], 'ffn2_w_up': out['ffn2_w_up'], 'ffn2_w_down': out['ffn2_w_down'], 'ple_w_gate': out['ple_w_gate'], 'ple_w_proj': out['ple_w_proj'], 'ln_ffn2_g': out['ln_ffn2_g'], 'ln_ffn2_b': out['ln_ffn2_b'], 'loss_target': out['loss_target'], 'm_ln_ffn1_g': out['m_ln_ffn1_g'], 'm_ln_ffn1_b': out['m_ln_ffn1_b'], 'm_ffn1_w_gate': out['m_ffn1_w_gate'], 'm_ffn1_w_up': out['m_ffn1_w_up'], 'm_ffn1_w_down': out['m_ffn1_w_down'], 'm_w_in': out['m_w_in'], 'm_ret_norm_g': out['m_ret_norm_g'], 'm_lru_conv_w': out['m_lru_conv_w'], 'm_lru_conv_b': out['m_lru_conv_b'], 'm_lru_w_a': out['m_lru_w_a'], 'm_lru_b_a': out['m_lru_b_a'], 'm_lru_w_x': out['m_lru_w_x'], 'm_lru_b_x': out['m_lru_b_x'], 'm_lru_lambda': out['m_lru_lambda'], 'm_gdn_conv_w': out['m_gdn_conv_w'], 'm_gdn_a_log': out['m_gdn_a_log'], 'm_gdn_dt_bias': out['m_gdn_dt_bias'], 'm_gdn_norm_g': out['m_gdn_norm_g'], 'm_w_out': out['m_w_out'], 'm_ln_mix_g': out['m_ln_mix_g'], 'm_ln_mix_b': out['m_ln_mix_b'], 'm_ffn2_w_gate': out['m_ffn2_w_gate'], 'm_ffn2_w_up': out['m_ffn2_w_up'], 'm_ffn2_w_down': out['m_ffn2_w_down'], 'm_ple_w_gate': out['m_ple_w_gate'], 'm_ple_w_proj': out['m_ple_w_proj'], 'm_ln_ffn2_g': out['m_ln_ffn2_g'], 'm_ln_ffn2_b': out['m_ln_ffn2_b'], 'v_ln_ffn1_g': out['v_ln_ffn1_g'], 'v_ln_ffn1_b': out['v_ln_ffn1_b'], 'v_ffn1_w_gate': out['v_ffn1_w_gate'], 'v_ffn1_w_up': out['v_ffn1_w_up'], 'v_ffn1_w_down': out['v_ffn1_w_down'], 'v_w_in': out['v_w_in'], 'v_ret_norm_g': out['v_ret_norm_g'], 'v_lru_conv_w': out['v_lru_conv_w'], 'v_lru_conv_b': out['v_lru_conv_b'], 'v_lru_w_a': out['v_lru_w_a'], 'v_lru_b_a': out['v_lru_b_a'], 'v_lru_w_x': out['v_lru_w_x'], 'v_lru_b_x': out['v_lru_b_x'], 'v_lru_lambda': out['v_lru_lambda'], 'v_gdn_conv_w': out['v_gdn_conv_w'], 'v_gdn_a_log': out['v_gdn_a_log'], 'v_gdn_dt_bias': out['v_gdn_dt_bias'], 'v_gdn_norm_g': out['v_gdn_norm_g'], 'v_w_out': out['v_w_out'], 'v_ln_mix_g': out['v_ln_mix_g'], 'v_ln_mix_b': out['v_ln_mix_b'], 'v_ffn2_w_gate': out['v_ffn2_w_gate'], 'v_ffn2_w_up': out['v_ffn2_w_up'], 'v_ffn2_w_down': out['v_ffn2_w_down'], 'v_ple_w_gate': out['v_ple_w_gate'], 'v_ple_w_proj': out['v_ple_w_proj'], 'v_ln_ffn2_g': out['v_ln_ffn2_g'], 'v_ln_ffn2_b': out['v_ln_ffn2_b']}


def _loss(weights, diff, rest, loss_target):
    with _jax.named_scope("forward"):
        args = {**rest, TWIN_DIFF_INPUT: diff, **{k: w.astype(_WEIGHT_DTYPES[k]) for k, w in weights.items()}}
        y = _forward(args)
    with _jax.named_scope("loss_head"):
        err = _jnp.square(y.astype(_jnp.float32) - loss_target)
        return 0.5 * _jnp.sum(_jnp.mean(err, axis=-1)) if err.ndim else 0.5 * err


def _adamw(w, g, m, v):
    m = ADAM_B1 * m + (1.0 - ADAM_B1) * g
    v = ADAM_B2 * v + (1.0 - ADAM_B2) * _jnp.square(g)
    m_hat = m / (1.0 - ADAM_B1 ** ADAM_STEP)
    v_hat = v / (1.0 - ADAM_B2 ** ADAM_STEP)
    delta = -ADAM_LR * (m_hat / (_jnp.sqrt(v_hat) + ADAM_EPS) + ADAM_WD * w)
    return delta, m, v


def reference(x, p, positions, ln_ffn1_g, ln_ffn1_b, ffn1_w_gate, ffn1_w_up, ffn1_w_down, w_in, ret_norm_g, lru_conv_w, lru_conv_b, lru_w_a, lru_b_a, lru_w_x, lru_b_x, lru_lambda, gdn_conv_w, gdn_a_log, gdn_dt_bias, gdn_norm_g, w_out, ln_mix_g, ln_mix_b, ffn2_w_gate, ffn2_w_up, ffn2_w_down, ple_w_gate, ple_w_proj, ln_ffn2_g, ln_ffn2_b, loss_target, m_ln_ffn1_g, m_ln_ffn1_b, m_ffn1_w_gate, m_ffn1_w_up, m_ffn1_w_down, m_w_in, m_ret_norm_g, m_lru_conv_w, m_lru_conv_b, m_lru_w_a, m_lru_b_a, m_lru_w_x, m_lru_b_x, m_lru_lambda, m_gdn_conv_w, m_gdn_a_log, m_gdn_dt_bias, m_gdn_norm_g, m_w_out, m_ln_mix_g, m_ln_mix_b, m_ffn2_w_gate, m_ffn2_w_up, m_ffn2_w_down, m_ple_w_gate, m_ple_w_proj, m_ln_ffn2_g, m_ln_ffn2_b, v_ln_ffn1_g, v_ln_ffn1_b, v_ffn1_w_gate, v_ffn1_w_up, v_ffn1_w_down, v_w_in, v_ret_norm_g, v_lru_conv_w, v_lru_conv_b, v_lru_w_a, v_lru_b_a, v_lru_w_x, v_lru_b_x, v_lru_lambda, v_gdn_conv_w, v_gdn_a_log, v_gdn_dt_bias, v_gdn_norm_g, v_w_out, v_ln_mix_g, v_ln_mix_b, v_ffn2_w_gate, v_ffn2_w_up, v_ffn2_w_down, v_ple_w_gate, v_ple_w_proj, v_ln_ffn2_g, v_ln_ffn2_b):
    given = dict(x=x, p=p, positions=positions, ln_ffn1_g=ln_ffn1_g, ln_ffn1_b=ln_ffn1_b, ffn1_w_gate=ffn1_w_gate, ffn1_w_up=ffn1_w_up, ffn1_w_down=ffn1_w_down, w_in=w_in, ret_norm_g=ret_norm_g, lru_conv_w=lru_conv_w, lru_conv_b=lru_conv_b, lru_w_a=lru_w_a, lru_b_a=lru_b_a, lru_w_x=lru_w_x, lru_b_x=lru_b_x, lru_lambda=lru_lambda, gdn_conv_w=gdn_conv_w, gdn_a_log=gdn_a_log, gdn_dt_bias=gdn_dt_bias, gdn_norm_g=gdn_norm_g, w_out=w_out, ln_mix_g=ln_mix_g, ln_mix_b=ln_mix_b, ffn2_w_gate=ffn2_w_gate, ffn2_w_up=ffn2_w_up, ffn2_w_down=ffn2_w_down, ple_w_gate=ple_w_gate, ple_w_proj=ple_w_proj, ln_ffn2_g=ln_ffn2_g, ln_ffn2_b=ln_ffn2_b, loss_target=loss_target, m_ln_ffn1_g=m_ln_ffn1_g, m_ln_ffn1_b=m_ln_ffn1_b, m_ffn1_w_gate=m_ffn1_w_gate, m_ffn1_w_up=m_ffn1_w_up, m_ffn1_w_down=m_ffn1_w_down, m_w_in=m_w_in, m_ret_norm_g=m_ret_norm_g, m_lru_conv_w=m_lru_conv_w, m_lru_conv_b=m_lru_conv_b, m_lru_w_a=m_lru_w_a, m_lru_b_a=m_lru_b_a, m_lru_w_x=m_lru_w_x, m_lru_b_x=m_lru_b_x, m_lru_lambda=m_lru_lambda, m_gdn_conv_w=m_gdn_conv_w, m_gdn_a_log=m_gdn_a_log, m_gdn_dt_bias=m_gdn_dt_bias, m_gdn_norm_g=m_gdn_norm_g, m_w_out=m_w_out, m_ln_mix_g=m_ln_mix_g, m_ln_mix_b=m_ln_mix_b, m_ffn2_w_gate=m_ffn2_w_gate, m_ffn2_w_up=m_ffn2_w_up, m_ffn2_w_down=m_ffn2_w_down, m_ple_w_gate=m_ple_w_gate, m_ple_w_proj=m_ple_w_proj, m_ln_ffn2_g=m_ln_ffn2_g, m_ln_ffn2_b=m_ln_ffn2_b, v_ln_ffn1_g=v_ln_ffn1_g, v_ln_ffn1_b=v_ln_ffn1_b, v_ffn1_w_gate=v_ffn1_w_gate, v_ffn1_w_up=v_ffn1_w_up, v_ffn1_w_down=v_ffn1_w_down, v_w_in=v_w_in, v_ret_norm_g=v_ret_norm_g, v_lru_conv_w=v_lru_conv_w, v_lru_conv_b=v_lru_conv_b, v_lru_w_a=v_lru_w_a, v_lru_b_a=v_lru_b_a, v_lru_w_x=v_lru_w_x, v_lru_b_x=v_lru_b_x, v_lru_lambda=v_lru_lambda, v_gdn_conv_w=v_gdn_conv_w, v_gdn_a_log=v_gdn_a_log, v_gdn_dt_bias=v_gdn_dt_bias, v_gdn_norm_g=v_gdn_norm_g, v_w_out=v_w_out, v_ln_mix_g=v_ln_mix_g, v_ln_mix_b=v_ln_mix_b, v_ffn2_w_gate=v_ffn2_w_gate, v_ffn2_w_up=v_ffn2_w_up, v_ffn2_w_down=v_ffn2_w_down, v_ple_w_gate=v_ple_w_gate, v_ple_w_proj=v_ple_w_proj, v_ln_ffn2_g=v_ln_ffn2_g, v_ln_ffn2_b=v_ln_ffn2_b)
    weights = {n: given[n] for n in TWIN_WEIGHTS}
    shared = {n: given[n] for n in SHARED_INPUTS}
    per_example = {n: given[n] for n in ['x', 'p', 'positions']}
    grad_fn = _jax.value_and_grad(_loss, argnums=(0, 1))

    def one_microbatch(ex, loss_target):
        ex = dict(ex)
        diff = ex.pop(TWIN_DIFF_INPUT)
        return grad_fn(weights, diff, {**shared, **ex}, loss_target)

    if N_MICROBATCH == 1:
        loss, (grad_w, grad_x) = one_microbatch(per_example, given["loss_target"])
    else:
        def body(carry, xs):
            loss_sum, grad_sum = carry
            l_k, (gw_k, gx_k) = one_microbatch(xs[0], xs[1])
            with _jax.named_scope("update"):
                return (loss_sum + l_k, _jax.tree.map(_jnp.add, grad_sum, gw_k)), gx_k

        init = (_jnp.zeros((), _jnp.float32), _jax.tree.map(_jnp.zeros_like, weights))
        (loss, grad_w), grad_x = _jax.lax.scan(body, init, (per_example, given["loss_target"]))
    with _jax.named_scope("update"):
        delta_w, new_m, new_v = {}, {}, {}
        for n in TWIN_WEIGHTS:
            delta_w[n], new_m[n], new_v[n] = _adamw(weights[n], grad_w[n], given["m_" + n], given["v_" + n])
    return (loss, grad_x, *[grad_w[n] for n in TWIN_WEIGHTS], *[delta_w[n] for n in TWIN_WEIGHTS],
            *[new_m[n] for n in TWIN_WEIGHTS], *[new_v[n] for n in TWIN_WEIGHTS])
```

```python
import functools
import math

import numpy as np
import jax
import jax.numpy as jnp
from jax import lax
from jax.experimental import pallas as pl
from jax.experimental.pallas import tpu as pltpu

F32 = jnp.float32
BF16 = jnp.bfloat16

D_MODEL = 1024
D_FF = 2816
PLE_DIM = 256
DEPTH = 2
CHUNK = 64
RET_HEADS = 4
RET_W = 256
LRU_W = 384
LRU_BLOCKS = 6
GDN_HEADS = 6
GDN_W = 384
HEAD = 64
D_IN = 3340
RET_IN = 4 * RET_W
LRU_IN = 2 * LRU_W
GDN_IN = 6 * GDN_W
ROPE_THETA = 10000.0
ALPHA = (2 * DEPTH) ** 0.25
LN_EPS = 1e-5
LRU_C = 8.0
N_CHIPS = 4
N_DEV = 8

ADAM_LR = 0.001
ADAM_B1 = 0.9
ADAM_B2 = 0.999
ADAM_EPS = 1e-08
ADAM_WD = 0.01
ADAM_STEP = 10

LANES = 128
VMEM_LIMIT = 56 * 1024 * 1024
ROW_TILE = 256
ROW_TILE_BWD = 128
SCAN_TILE = 256


def _params(*sem):
    return pltpu.CompilerParams(dimension_semantics=sem, vmem_limit_bytes=VMEM_LIMIT)


def _mm(a, b):
    return jnp.dot(a, b, preferred_element_type=F32)


def _mm_nt(a, b):
    return lax.dot_general(a, b, (((1,), (1,)), ((), ())), preferred_element_type=F32)


def _mm_tn(a, b):
    return lax.dot_general(a, b, (((0,), (0,)), ((), ())), preferred_element_type=F32)


def _split(a):
    hi = a.astype(BF16)
    lo = (a - hi.astype(F32)).astype(BF16)
    return hi, lo


def _mm3(a, b):
    ah, al = _split(a)
    bh, bl = _split(b)
    return _mm(ah, bh) + (_mm(ah, bl) + _mm(al, bh))


def _sigmoid(x):
    return jax.nn.sigmoid(x)


def _log1p(u):
    w = 1.0 + u
    return jnp.where(w == 1.0, u, jnp.log(w) * (u / jnp.where(w == 1.0, 1.0, w - 1.0)))


def _expm1(y):
    u = jnp.exp(y)
    um1 = u - 1.0
    safe = jnp.where((u == 1.0) | (um1 == -1.0), 1.0, jnp.log(jnp.where(u == 0.0, 1.0, u)))
    return jnp.where(u == 1.0, y, jnp.where(um1 == -1.0, -1.0, um1 * (y / safe)))


def _softplus(x):
    return jnp.maximum(x, 0.0) + _log1p(jnp.exp(-jnp.abs(x)))


_GELU_C = math.sqrt(2.0 / math.pi)


def _gelu(x):
    return 0.5 * x * (1.0 + jnp.tanh(_GELU_C * (x + 0.044715 * (x * x * x))))


def _gelu_grad(x):
    t = jnp.tanh(_GELU_C * (x + 0.044715 * (x * x * x)))
    return 0.5 * (1.0 + t) + 0.5 * x * (1.0 - t * t) * (_GELU_C * (1.0 + 3.0 * 0.044715 * (x * x)))


def _silu_and_grad(x):
    s = _sigmoid(x)
    return x * s, s * (1.0 + x * (1.0 - s))


def _group_sum_slab(x):
    lane = lax.broadcasted_iota(jnp.int32, x.shape, 1)
    s = x
    for d in (1, 2, 4, 8, 16, 32):
        s = s + jnp.where((lane & d) == 0, pltpu.roll(s, LANES - d, 1), pltpu.roll(s, d, 1))
    return s


def _group_sum(x):
    n = x.shape[1] // LANES
    if n == 1:
        return _group_sum_slab(x)
    return jnp.concatenate([_group_sum_slab(x[:, LANES * i:LANES * (i + 1)]) for i in range(n)], axis=1)


def _rows_prefix_sum(x):
    n = x.shape[0]
    row = lax.broadcasted_iota(jnp.int32, x.shape, 0)
    d = 1
    while d < n:
        x = x + jnp.where(row >= d, pltpu.roll(x, d, 0), 0.0)
        d *= 2
    return x


def _rows_suffix_sum(x):
    n = x.shape[0]
    row = lax.broadcasted_iota(jnp.int32, x.shape, 0)
    d = 1
    while d < n:
        x = x + jnp.where(row < n - d, pltpu.roll(x, n - d, 0), 0.0)
        d *= 2
    return x


def _shift_rows(cur, prev, j):
    row = lax.broadcasted_iota(jnp.int32, cur.shape, 0)
    return jnp.where(row < j, pltpu.roll(prev, j, 0), pltpu.roll(cur, j, 0))


def _shift_rows_up(cur, nxt, j):
    n = cur.shape[0]
    row = lax.broadcasted_iota(jnp.int32, cur.shape, 0)
    return jnp.where(row < n - j, pltpu.roll(cur, n - j, 0), pltpu.roll(nxt, n - j, 0))


def _layer_norm_stats(r):
    mu = jnp.mean(r, axis=-1, keepdims=True)
    d = r - mu
    var = jnp.mean(d * d, axis=-1, keepdims=True)
    rstd = lax.rsqrt(var + LN_EPS)
    return d * rstd, rstd


def _load_resident(step, pairs, sems):
    @pl.when(step == 0)
    def _():
        cps = [pltpu.make_async_copy(h, v, sems.at[i]) for i, (h, v) in enumerate(pairs)]
        for c in cps:
            c.start()
        for c in cps:
            c.wait()


def _row_spec(tile, width):
    return pl.BlockSpec((tile, width), lambda i: (i, 0))


def _full_spec(shape):
    nd = len(shape)
    return pl.BlockSpec(shape, lambda i: (0,) * nd)


_ANY = pl.BlockSpec(memory_space=pl.ANY)


def ffn_fwd(x, ln_g, ln_b, w_gate, w_up, w_down, ple=None):
    s = x.shape[0]
    tm = ROW_TILE
    with_ple = ple is not None
    weights = [w_gate, w_up, w_down] + ([ple[1], ple[2]] if with_ple else [])

    def body(*refs):
        it = iter(refs)
        x_ref, g_ref, b_ref = next(it), next(it), next(it)
        p_ref = next(it) if with_ple else None
        w_hbm = [next(it) for _ in weights]
        hg_ref, hu_ref, r_ref, xn_ref = next(it), next(it), next(it), next(it)
        pg_ref, pp_ref = (next(it), next(it)) if with_ple else (None, None)
        w_vm = [next(it) for _ in weights]
        sems = next(it)
        _load_resident(pl.program_id(0), list(zip(w_hbm, w_vm)), sems)
        xv = x_ref[...]
        xb = xv.astype(BF16)
        hg = _mm(xb, w_vm[0][...])
        hu = _mm(xb, w_vm[1][...])
        hg_ref[...] = hg
        hu_ref[...] = hu
        act = (hg * _sigmoid(hg)) * hu
        r = ALPHA * xv + 0.5 * _mm(act.astype(BF16), w_vm[2][...])
        if with_ple:
            pg = _mm(xb, w_vm[3][...])
            pp = _mm(p_ref[...].astype(BF16), w_vm[4][...])
            pg_ref[...] = pg
            pp_ref[...] = pp
            r = r + _sigmoid(pg) * pp
        r_ref[...] = r
        xhat, _ = _layer_norm_stats(r)
        xn_ref[...] = xhat * g_ref[...] + b_ref[...]

    d, f = D_MODEL, D_FF
    in_specs = [_row_spec(tm, d), _full_spec((1, d)), _full_spec((1, d))]
    args = [x, ln_g, ln_b]
    if with_ple:
        in_specs.append(_row_spec(tm, PLE_DIM))
        args.append(ple[0])
    in_specs += [_ANY] * len(weights)
    args += weights
    out_shape = [jax.ShapeDtypeStruct((s, f), F32), jax.ShapeDtypeStruct((s, f), F32),
                 jax.ShapeDtypeStruct((s, d), F32), jax.ShapeDtypeStruct((s, d), F32)]
    out_specs = [_row_spec(tm, f), _row_spec(tm, f), _row_spec(tm, d), _row_spec(tm, d)]
    if with_ple:
        out_shape += [jax.ShapeDtypeStruct((s, d), F32)] * 2
        out_specs += [_row_spec(tm, d)] * 2
    scratch = [pltpu.VMEM(w.shape, w.dtype) for w in weights] + [pltpu.SemaphoreType.DMA((len(weights),))]
    return pl.pallas_call(
        body, name="ffn_fwd_ple" if with_ple else "ffn_fwd", grid=(s // tm,), in_specs=in_specs, out_specs=out_specs,
        out_shape=out_shape, scratch_shapes=scratch, compiler_params=_params("arbitrary"),
    )(*args)


def ffn_bwd(dxn, r, x, hg, hu, ln_g, w_gate, w_up, w_down, ple=None):
    s = x.shape[0]
    tm = ROW_TILE_BWD
    with_ple = ple is not None
    weights = [w_gate, w_up, w_down] + ([ple[2]] if with_ple else [])

    def body(*refs):
        it = iter(refs)
        dxn_ref, r_ref, x_ref, hg_ref, hu_ref, g_ref = (next(it) for _ in range(6))
        pg_ref, pp_ref = (next(it), next(it)) if with_ple else (None, None)
        w_hbm = [next(it) for _ in weights]
        dx_ref, act_ref, dhg_ref, dhu_ref, dy_ref, dg_ref, db_ref = (next(it) for _ in range(7))
        dpg_ref, dpp_ref = (next(it), next(it)) if with_ple else (None, None)
        w_vm = [next(it) for _ in weights]
        sems = next(it)
        step = pl.program_id(0)
        _load_resident(step, list(zip(w_hbm, w_vm)), sems)

        @pl.when(step == 0)
        def _():
            dg_ref[...] = jnp.zeros_like(dg_ref)
            db_ref[...] = jnp.zeros_like(db_ref)

        dxn_v = dxn_ref[...]
        xhat, rstd = _layer_norm_stats(r_ref[...])
        dg_ref[...] += jnp.sum(dxn_v * xhat, axis=0, keepdims=True)
        db_ref[...] += jnp.sum(dxn_v, axis=0, keepdims=True)
        dyh = dxn_v * g_ref[...]
        dr = rstd * (dyh - jnp.mean(dyh, axis=-1, keepdims=True) - xhat * jnp.mean(dyh * xhat, axis=-1, keepdims=True))
        dy = (0.5 * dr).astype(BF16)
        dy_ref[...] = dy
        da = _mm_nt(dy, w_vm[2][...])
        hg_v = hg_ref[...]
        hu_v = hu_ref[...]
        sil, dsil = _silu_and_grad(hg_v)
        act_ref[...] = (sil * hu_v).astype(BF16)
        dhu = (da * sil).astype(BF16)
        dhg = (da * hu_v * dsil).astype(BF16)
        dhu_ref[...] = dhu
        dhg_ref[...] = dhg
        dx = ALPHA * dr + _mm_nt(dhg, w_vm[0][...]) + _mm_nt(dhu, w_vm[1][...])
        if with_ple:
            sp = _sigmoid(pg_ref[...])
            pp = pp_ref[...]
            dpp_ref[...] = (dr * sp).astype(BF16)
            dpg = (dr * pp * sp * (1.0 - sp)).astype(BF16)
            dpg_ref[...] = dpg
            dx = dx + _mm_nt(dpg, w_vm[3][...])
        dx_ref[...] = dx

    d, f = D_MODEL, D_FF
    in_specs = [_row_spec(tm, d), _row_spec(tm, d), _row_spec(tm, d), _row_spec(tm, f), _row_spec(tm, f), _full_spec((1, d))]
    args = [dxn, r, x, hg, hu, ln_g]
    if with_ple:
        in_specs += [_row_spec(tm, d), _row_spec(tm, d)]
        args += [ple[0], ple[1]]
    in_specs += [_ANY] * len(weights)
    args += weights
    out_shape = [jax.ShapeDtypeStruct((s, d), F32), jax.ShapeDtypeStruct((s, f), BF16), jax.ShapeDtypeStruct((s, f), BF16),
                 jax.ShapeDtypeStruct((s, f), BF16), jax.ShapeDtypeStruct((s, d), BF16),
                 jax.ShapeDtypeStruct((1, d), F32), jax.ShapeDtypeStruct((1, d), F32)]
    out_specs = [_row_spec(tm, d), _row_spec(tm, f), _row_spec(tm, f), _row_spec(tm, f), _row_spec(tm, d),
                 _full_spec((1, d)), _full_spec((1, d))]
    if with_ple:
        out_shape += [jax.ShapeDtypeStruct((s, d), BF16)] * 2
        out_specs += [_row_spec(tm, d)] * 2
    scratch = [pltpu.VMEM(w.shape, w.dtype) for w in weights] + [pltpu.SemaphoreType.DMA((len(weights),))]
    return pl.pallas_call(
        body, name="ffn_bwd_ple" if with_ple else "ffn_bwd", grid=(s // tm,), in_specs=in_specs, out_specs=out_specs,
        out_shape=out_shape, scratch_shapes=scratch, compiler_params=_params("arbitrary"),
    )(*args)


def win_fwd(x1, w_r, w_l, w_g):
    s = x1.shape[0]
    tm = ROW_TILE
    weights = [w_r, w_l, w_g]

    def body(x_ref, wr_h, wl_h, wg_h, hr_ref, hl_ref, hgd_ref, wr_v, wl_v, wg_v, sems):
        _load_resident(pl.program_id(0), [(wr_h, wr_v), (wl_h, wl_v), (wg_h, wg_v)], sems)
        xb = x_ref[...].astype(BF16)
        hr_ref[...] = _mm(xb, wr_v[...])
        hl_ref[...] = _mm(xb, wl_v[...])
        hgd_ref[...] = _mm(xb, wg_v[...])

    return pl.pallas_call(
        body, name="win_fwd", grid=(s // tm,),
        in_specs=[_row_spec(tm, D_MODEL), _ANY, _ANY, _ANY],
        out_specs=[_row_spec(tm, RET_IN), _row_spec(tm, LRU_IN), _row_spec(tm, GDN_IN)],
        out_shape=[jax.ShapeDtypeStruct((s, RET_IN), F32), jax.ShapeDtypeStruct((s, LRU_IN), F32),
                   jax.ShapeDtypeStruct((s, GDN_IN), F32)],
        scratch_shapes=[pltpu.VMEM(w.shape, w.dtype) for w in weights] + [pltpu.SemaphoreType.DMA((3,))],
        compiler_params=_params("arbitrary"),
    )(x1, *weights)


def win_bwd(dr2, dhr, dhl, dhgd, w_r, w_l, w_g):
    s = dr2.shape[0]
    tm = ROW_TILE
    weights = [w_r, w_l, w_g]

    def body(dr_ref, dhr_ref, dhl_ref, dhg_ref, wr_h, wl_h, wg_h, dx_ref, wr_v, wl_v, wg_v, sems):
        _load_resident(pl.program_id(0), [(wr_h, wr_v), (wl_h, wl_v), (wg_h, wg_v)], sems)
        dx_ref[...] = (ALPHA * dr_ref[...] + _mm_nt(dhr_ref[...], wr_v[...]) + _mm_nt(dhl_ref[...], wl_v[...])
                       + _mm_nt(dhg_ref[...], wg_v[...]))

    return pl.pallas_call(
        body, name="win_bwd", grid=(s // tm,),
        in_specs=[_row_spec(tm, D_MODEL), _row_spec(tm, RET_IN), _row_spec(tm, LRU_IN), _row_spec(tm, GDN_IN), _ANY, _ANY, _ANY],
        out_specs=_row_spec(tm, D_MODEL),
        out_shape=jax.ShapeDtypeStruct((s, D_MODEL), F32),
        scratch_shapes=[pltpu.VMEM(w.shape, w.dtype) for w in weights] + [pltpu.SemaphoreType.DMA((3,))],
        compiler_params=_params("arbitrary"),
    )(dr2, dhr, dhl, dhgd, *weights)


def out_fwd(o_r, o_l, o_g, x1, w_out, ln_g, ln_b):
    s = x1.shape[0]
    tm = ROW_TILE

    def body(or_ref, ol_ref, og_ref, x_ref, g_ref, b_ref, w_h, r_ref, xn_ref, w_v, sems):
        _load_resident(pl.program_id(0), [(w_h, w_v)], sems)
        mix = (_mm(or_ref[...].astype(BF16), w_v[0:RET_W, :]) + _mm(ol_ref[...].astype(BF16), w_v[RET_W:RET_W + LRU_W, :])
               + _mm(og_ref[...].astype(BF16), w_v[RET_W + LRU_W:, :]))
        r = ALPHA * x_ref[...] + mix
        r_ref[...] = r
        xhat, _ = _layer_norm_stats(r)
        xn_ref[...] = xhat * g_ref[...] + b_ref[...]

    d = D_MODEL
    return pl.pallas_call(
        body, name="out_fwd", grid=(s // tm,),
        in_specs=[_row_spec(tm, RET_W), _row_spec(tm, LRU_W), _row_spec(tm, GDN_W), _row_spec(tm, d),
                  _full_spec((1, d)), _full_spec((1, d)), _ANY],
        out_specs=[_row_spec(tm, d), _row_spec(tm, d)],
        out_shape=[jax.ShapeDtypeStruct((s, d), F32)] * 2,
        scratch_shapes=[pltpu.VMEM(w_out.shape, w_out.dtype), pltpu.SemaphoreType.DMA((1,))],
        compiler_params=_params("arbitrary"),
    )(o_r, o_l, o_g, x1, ln_g, ln_b, w_out)


def out_bwd(dxn, r2, ln_g, w_out):
    s = dxn.shape[0]
    tm = ROW_TILE

    def body(dxn_ref, r_ref, g_ref, w_h, dr_ref, drb_ref, dor_ref, dol_ref, dog_ref, dg_ref, db_ref, w_v, sems):
        step = pl.program_id(0)
        _load_resident(step, [(w_h, w_v)], sems)

        @pl.when(step == 0)
        def _():
            dg_ref[...] = jnp.zeros_like(dg_ref)
            db_ref[...] = jnp.zeros_like(db_ref)

        dxn_v = dxn_ref[...]
        xhat, rstd = _layer_norm_stats(r_ref[...])
        dg_ref[...] += jnp.sum(dxn_v * xhat, axis=0, keepdims=True)
        db_ref[...] += jnp.sum(dxn_v, axis=0, keepdims=True)
        dyh = dxn_v * g_ref[...]
        dr = rstd * (dyh - jnp.mean(dyh, axis=-1, keepdims=True) - xhat * jnp.mean(dyh * xhat, axis=-1, keepdims=True))
        dr_ref[...] = dr
        drb = dr.astype(BF16)
        drb_ref[...] = drb
        dor_ref[...] = _mm_nt(drb, w_v[0:RET_W, :])
        dol_ref[...] = _mm_nt(drb, w_v[RET_W:RET_W + LRU_W, :])
        dog_ref[...] = _mm_nt(drb, w_v[RET_W + LRU_W:, :])

    d = D_MODEL
    return pl.pallas_call(
        body, name="out_bwd", grid=(s // tm,),
        in_specs=[_row_spec(tm, d), _row_spec(tm, d), _full_spec((1, d)), _ANY],
        out_specs=[_row_spec(tm, d), _row_spec(tm, d), _row_spec(tm, RET_W), _row_spec(tm, LRU_W), _row_spec(tm, GDN_W),
                   _full_spec((1, d)), _full_spec((1, d))],
        out_shape=[jax.ShapeDtypeStruct((s, d), F32), jax.ShapeDtypeStruct((s, d), BF16),
                   jax.ShapeDtypeStruct((s, RET_W), F32), jax.ShapeDtypeStruct((s, LRU_W), F32),
                   jax.ShapeDtypeStruct((s, GDN_W), F32), jax.ShapeDtypeStruct((1, d), F32), jax.ShapeDtypeStruct((1, d), F32)],
        scratch_shapes=[pltpu.VMEM(w_out.shape, w_out.dtype), pltpu.SemaphoreType.DMA((1,))],
        compiler_params=_params("arbitrary"),
    )(dxn, r2, ln_g, w_out)


def wgrad(a, b, name):
    s, m = a.shape
    n = b.shape[1]
    tk = 1024 if s % 1024 == 0 else s
    tm = next((c for c in (512, 384, 256) if m % c == 0), m)
    tn = next((c for c in (1408, 1152, 1024, 768, 512) if n % c == 0), n)
    nk = s // tk

    def body(a_ref, b_ref, o_ref, acc_ref):
        k = pl.program_id(2)

        @pl.when(k == 0)
        def _():
            acc_ref[...] = jnp.zeros_like(acc_ref)

        acc_ref[...] += _mm_tn(a_ref[...].astype(BF16), b_ref[...].astype(BF16))

        @pl.when(k == nk - 1)
        def _():
            o_ref[...] = acc_ref[...]

    return pl.pallas_call(
        body, name=name, grid=(m // tm, n // tn, nk),
        in_specs=[pl.BlockSpec((tk, tm), lambda i, j, k: (k, i)), pl.BlockSpec((tk, tn), lambda i, j, k: (k, j))],
        out_specs=pl.BlockSpec((tm, tn), lambda i, j, k: (i, j)),
        out_shape=jax.ShapeDtypeStruct((m, n), F32),
        scratch_shapes=[pltpu.VMEM((tm, tn), F32)],
        compiler_params=_params("arbitrary", "arbitrary", "arbitrary"),
    )(a, b)


def loss_and_grad(y, target):
    s, d = y.shape
    tm = ROW_TILE

    def body(y_ref, t_ref, dy_ref, l_ref):
        @pl.when(pl.program_id(0) == 0)
        def _():
            l_ref[...] = jnp.zeros_like(l_ref)

        err = y_ref[...] - t_ref[...]
        dy_ref[...] = err / d
        l_ref[...] += 0.5 * jnp.sum(jnp.mean(err * err, axis=-1, keepdims=True), axis=0, keepdims=True)

    return pl.pallas_call(
        body, name="loss_and_grad", grid=(s // tm,),
        in_specs=[_row_spec(tm, d), _row_spec(tm, d)],
        out_specs=[_row_spec(tm, d), _full_spec((1, 1))],
        out_shape=[jax.ShapeDtypeStruct((s, d), F32), jax.ShapeDtypeStruct((1, 1), F32)],
        compiler_params=_params("arbitrary"),
    )(y, target)


def _ret_consts():
    lg = np.log1p(-np.exp2(-5.0 - np.arange(RET_HEADS, dtype=np.float64)))
    idx = np.arange(CHUNK, dtype=np.float64)
    intra = np.exp(np.abs(idx[:, None] - idx[None, :])[None] * lg[:, None, None])
    cross = np.repeat(np.exp((idx + 1.0)[:, None] * lg[None, :]), HEAD, axis=1)
    tail = np.repeat(np.exp((CHUNK - 1.0 - idx)[:, None] * lg[None, :]), HEAD, axis=1)
    dec = np.repeat(np.exp(CHUNK * lg)[None, :], HEAD, axis=1)
    half = HEAD // 2
    inv_freq = (ROPE_THETA ** (-jnp.arange(half, dtype=F32) / half))
    invf = jnp.tile(inv_freq, 2 * LANES // HEAD)[None, :]
    sgn = np.tile(np.concatenate([-np.ones(half), np.ones(half)]), LANES // HEAD)[None, :]
    f = lambda a: jnp.asarray(a, F32)
    return dict(intra=f(intra), cross=f(cross), tail=f(tail), dec=f(dec), invf=invf, sgn=f(sgn))


def _swap_halves(t):
    lane = lax.broadcasted_iota(jnp.int32, t.shape, 1)
    return jnp.where((lane & 32) == 0, pltpu.roll(t, LANES - 32, 1), pltpu.roll(t, 32, 1))


def _rope(t, c, s):
    return t * c + _swap_halves(t) * s


def _rope_transposed(g, c, s):
    return g * c + _swap_halves(g * s)


def _head_mask(hd):
    lane = lax.broadcasted_iota(jnp.int32, (1, LANES), 1)
    return ((lane >= HEAD * hd) & (lane < HEAD * (hd + 1))).astype(F32)


def _block_diag_mask():
    r = lax.broadcasted_iota(jnp.int32, (LANES, LANES), 0)
    c = lax.broadcasted_iota(jnp.int32, (LANES, LANES), 1)
    return ((r >= HEAD) == (c >= HEAD)).astype(F32)


def _ret_specs(n_of):
    cst = lambda shape: pl.BlockSpec(shape, lambda i: (0,) * len(shape))
    return [pl.BlockSpec((CHUNK, RET_IN), lambda i: (n_of(i), 0)), pl.BlockSpec((CHUNK, 1), lambda i: (n_of(i), 0)),
            cst((1, LANES)), cst((1, LANES)), cst((RET_HEADS, CHUNK, CHUNK)), cst((CHUNK, RET_W)), cst((CHUNK, RET_W)),
            cst((1, RET_W)), cst((1, RET_W))]


def ret_fwd(hr, pos, norm_g):
    s = hr.shape[0]
    n_chunks = s // CHUNK
    cs = _ret_consts()
    n_slab = RET_W // LANES

    def body(hr_ref, pos_ref, invf_ref, sgn_ref, intra_ref, cross_ref, tail_ref, dec_ref, g_ref, o_ref, opre_ref, st_ref, state):
        @pl.when(pl.program_id(0) == 0)
        def _():
            state[...] = jnp.zeros_like(state)

        ang = pos_ref[...].astype(F32) * invf_ref[...]
        cosv = jnp.cos(ang)
        sinv = jnp.sin(ang) * sgn_ref[...]
        bd = _block_diag_mask()
        for sl in range(n_slab):
            lanes = slice(LANES * sl, LANES * (sl + 1))
            rows = slice(LANES * sl, LANES * (sl + 1))
            q = hr_ref[:, LANES * sl:LANES * (sl + 1)]
            k = hr_ref[:, RET_W + LANES * sl:RET_W + LANES * (sl + 1)]
            v = hr_ref[:, 2 * RET_W + LANES * sl:2 * RET_W + LANES * (sl + 1)]
            gate = hr_ref[:, 3 * RET_W + LANES * sl:3 * RET_W + LANES * (sl + 1)]
            qt = _rope(q, cosv, sinv) * (HEAD ** -0.5)
            kt = _rope(k, cosv, sinv)
            st = state[rows, :]
            st_ref[rows, :] = st
            o = _mm(qt * cross_ref[:, lanes], st)
            for hd in range(2):
                m = _head_mask(hd)
                sc = _mm_nt(qt * m, kt) * intra_ref[2 * sl + hd]
                o = o + _mm(sc, v) * m
            state[rows, :] = st * dec_ref[:, lanes] + _mm_tn(kt, v * tail_ref[:, lanes]) * bd
            opre_ref[:, lanes] = o
            mu = _group_sum_slab(o) * (1.0 / HEAD)
            dlt = o - mu
            var = _group_sum_slab(dlt * dlt) * (1.0 / HEAD)
            on = dlt * lax.rsqrt(var + 1e-5)
            o_ref[:, lanes] = on * g_ref[:, lanes] * (gate * _sigmoid(gate))

    out_row = lambda w: pl.BlockSpec((CHUNK, w), lambda i: (i, 0))
    return pl.pallas_call(
        body, name="ret_fwd", grid=(n_chunks,),
        in_specs=_ret_specs(lambda i: i),
        out_specs=[out_row(RET_W), out_row(RET_W), pl.BlockSpec((RET_W, LANES), lambda i: (i, 0))],
        out_shape=[jax.ShapeDtypeStruct((s, RET_W), F32), jax.ShapeDtypeStruct((s, RET_W), F32),
                   jax.ShapeDtypeStruct((n_chunks * RET_W, LANES), F32)],
        scratch_shapes=[pltpu.VMEM((RET_W, LANES), F32)],
        compiler_params=_params("arbitrary"),
    )(hr, pos, cs["invf"], cs["sgn"], cs["intra"], cs["cross"], cs["tail"], cs["dec"], norm_g)


def ret_bwd(hr, pos, norm_g, opre, states, dout):
    s = hr.shape[0]
    n_chunks = s // CHUNK
    cs = _ret_consts()
    n_slab = RET_W // LANES
    rev = lambda i: n_chunks - 1 - i

    def body(hr_ref, pos_ref, invf_ref, sgn_ref, intra_ref, cross_ref, tail_ref, dec_ref, g_ref, opre_ref, st_ref, do_ref,
             dh_ref, dg_ref, gstate):
        @pl.when(pl.program_id(0) == 0)
        def _():
            gstate[...] = jnp.zeros_like(gstate)
            dg_ref[...] = jnp.zeros_like(dg_ref)

        ang = pos_ref[...].astype(F32) * invf_ref[...]
        cosv = jnp.cos(ang)
        sinv = jnp.sin(ang) * sgn_ref[...]
        bd = _block_diag_mask()
        for sl in range(n_slab):
            lanes = slice(LANES * sl, LANES * (sl + 1))
            rows = slice(LANES * sl, LANES * (sl + 1))
            q = hr_ref[:, LANES * sl:LANES * (sl + 1)]
            k = hr_ref[:, RET_W + LANES * sl:RET_W + LANES * (sl + 1)]
            v = hr_ref[:, 2 * RET_W + LANES * sl:2 * RET_W + LANES * (sl + 1)]
            gate = hr_ref[:, 3 * RET_W + LANES * sl:3 * RET_W + LANES * (sl + 1)]
            qt = _rope(q, cosv, sinv) * (HEAD ** -0.5)
            kt = _rope(k, cosv, sinv)
            o = opre_ref[:, lanes]
            mu = _group_sum_slab(o) * (1.0 / HEAD)
            dlt = o - mu
            var = _group_sum_slab(dlt * dlt) * (1.0 / HEAD)
            rstd = lax.rsqrt(var + 1e-5)
            on = dlt * rstd
            sil, dsil = _silu_and_grad(gate)
            dout_v = do_ref[:, lanes]
            gn = g_ref[:, lanes]
            dg_ref[:, lanes] += jnp.sum(dout_v * on * sil, axis=0, keepdims=True)
            d_on = dout_v * gn * sil
            dgate = dout_v * on * gn * dsil
            d_o = rstd * (d_on - _group_sum_slab(d_on) * (1.0 / HEAD) - on * (_group_sum_slab(d_on * on) * (1.0 / HEAD)))
            st = st_ref[rows, :]
            gs = gstate[rows, :]
            cross = cross_ref[:, lanes]
            tail = tail_ref[:, lanes]
            dqt = _mm_nt(d_o, st) * cross
            ds_here = _mm_tn(qt * cross, d_o) * bd
            vt = v * tail
            dkt = _mm_nt(vt, gs)
            dv = _mm(kt, gs) * tail
            for hd in range(2):
                m = _head_mask(hd)
                qm = qt * m
                dom = d_o * m
                intra = intra_ref[2 * sl + hd]
                sc = _mm_nt(qm, kt) * intra
                dsc = _mm_nt(dom, v) * intra
                dqt = dqt + _mm(dsc, kt) * m
                dkt = dkt + _mm_tn(dsc, qm)
                dv = dv + _mm_tn(sc, dom)
            gstate[rows, :] = gs * dec_ref[:, lanes] + ds_here
            dh_ref[:, LANES * sl:LANES * (sl + 1)] = _rope_transposed(dqt * (HEAD ** -0.5), cosv, sinv).astype(BF16)
            dh_ref[:, RET_W + LANES * sl:RET_W + LANES * (sl + 1)] = _rope_transposed(dkt, cosv, sinv).astype(BF16)
            dh_ref[:, 2 * RET_W + LANES * sl:2 * RET_W + LANES * (sl + 1)] = dv.astype(BF16)
            dh_ref[:, 3 * RET_W + LANES * sl:3 * RET_W + LANES * (sl + 1)] = dgate.astype(BF16)

    row = lambda w: pl.BlockSpec((CHUNK, w), lambda i: (rev(i), 0))
    return pl.pallas_call(
        body, name="ret_bwd", grid=(n_chunks,),
        in_specs=_ret_specs(rev) + [row(RET_W), pl.BlockSpec((RET_W, LANES), lambda i: (rev(i), 0)), row(RET_W)],
        out_specs=[row(RET_IN), pl.BlockSpec((1, RET_W), lambda i: (0, 0))],
        out_shape=[jax.ShapeDtypeStruct((s, RET_IN), BF16), jax.ShapeDtypeStruct((1, RET_W), F32)],
        scratch_shapes=[pltpu.VMEM((RET_W, LANES), F32)],
        compiler_params=_params("arbitrary"),
    )(hr, pos, cs["invf"], cs["sgn"], cs["intra"], cs["cross"], cs["tail"], cs["dec"], norm_g, opre, states, dout)


def _lru_gates(xc, wa_ref, ba_ref, wx_ref, bx_ref, lam_ref):
    xcb = xc.astype(BF16)
    r = _sigmoid(_mm(xcb, wa_ref[...].astype(BF16)) + ba_ref[...])
    ig = _sigmoid(_mm(xcb, wx_ref[...].astype(BF16)) + bx_ref[...])
    lam = lam_ref[...]
    ls = jnp.minimum(lam, 0.0) - _log1p(jnp.exp(-jnp.abs(lam)))
    la = (LRU_C * r) * ls
    a = jnp.exp(la)
    mult = jnp.sqrt(-_expm1(2.0 * la))
    return r, ig, ls, a, mult


def _lru_conv(x, xprev, w_ref, b_ref):
    xc = b_ref[...] + w_ref[3:4, :] * x
    for j in (1, 2, 3):
        xc = xc + w_ref[3 - j:4 - j, :] * _shift_rows(x, xprev, j)
    return xc


def lru_fwd(hl, conv_w, conv_b, w_a, b_a, w_x, b_x, lam):
    s = hl.shape[0]
    ts = SCAN_TILE
    w = LRU_W

    def body(hl_ref, hp_ref, cw_ref, cb_ref, wa_ref, ba_ref, wx_ref, bx_ref, lam_ref, o_ref, xc_ref, h_ref, carry):
        i = pl.program_id(0)

        @pl.when(i == 0)
        def _():
            carry[...] = jnp.zeros_like(carry)

        x = hl_ref[:, 0:w]
        gate = hl_ref[:, w:2 * w]
        xprev = hp_ref[...] * (i > 0).astype(F32)
        xc = _lru_conv(x, xprev, cw_ref, cb_ref)
        xc_ref[...] = xc
        _, ig, _, a, mult = _lru_gates(xc, wa_ref, ba_ref, wx_ref, bx_ref, lam_ref)
        b = mult * (ig * xc)
        row = lax.broadcasted_iota(jnp.int32, (ts, w), 0)
        d = 1
        while d < ts:
            ap = jnp.where(row >= d, pltpu.roll(a, d, 0), 1.0)
            bp = jnp.where(row >= d, pltpu.roll(b, d, 0), 0.0)
            b = a * bp + b
            a = a * ap
            d *= 2
        h = b + a * carry[0:1, :]
        h_ref[...] = h
        carry[0:1, :] = h[ts - 1:ts, :]
        o_ref[...] = h * _gelu(gate)

    cst = lambda shape: pl.BlockSpec(shape, lambda i: (0, 0))
    return pl.pallas_call(
        body, name="lru_fwd", grid=(s // ts,),
        in_specs=[_row_spec(ts, 2 * w), pl.BlockSpec((ts, w), lambda i: (jnp.maximum(i - 1, 0), 0)),
                  cst((4, w)), cst((1, w)), cst((w, w)), cst((1, w)), cst((w, w)), cst((1, w)), cst((1, w))],
        out_specs=[_row_spec(ts, w)] * 3,
        out_shape=[jax.ShapeDtypeStruct((s, w), F32)] * 3,
        scratch_shapes=[pltpu.VMEM((8, w), F32)],
        compiler_params=_params("arbitrary"),
    )(hl, hl, conv_w, conv_b, w_a, b_a, w_x, b_x, lam)


def lru_bwd(hl, conv_w, conv_b, w_a, b_a, w_x, b_x, lam, xc_saved, h_saved, dout):
    s = hl.shape[0]
    ts = SCAN_TILE
    w = LRU_W
    nb = s // ts
    rev = lambda i: nb - 1 - i

    def body(hl_ref, hp_ref, cw_ref, cb_ref, wa_ref, ba_ref, wx_ref, bx_ref, lam_ref, xc_ref, h_ref, hprev_ref, do_ref,
             dhl_ref, dcw_ref, dcb_ref, dwa_ref, dba_ref, dwx_ref, dbx_ref, dlam_ref, carry, dxc_next):
        i = pl.program_id(0)
        blk = nb - 1 - i

        @pl.when(i == 0)
        def _():
            carry[...] = jnp.zeros_like(carry)
            dxc_next[...] = jnp.zeros_like(dxc_next)
            for ref in (dcw_ref, dcb_ref, dwa_ref, dba_ref, dwx_ref, dbx_ref, dlam_ref):
                ref[...] = jnp.zeros_like(ref)

        first = (blk > 0).astype(F32)
        x = hl_ref[:, 0:w]
        gate = hl_ref[:, w:2 * w]
        xprev = hp_ref[...] * first
        xc = xc_ref[...]
        h = h_ref[...]
        hprev = hprev_ref[...] * first
        r, ig, ls, a, mult = _lru_gates(xc, wa_ref, ba_ref, wx_ref, bx_ref, lam_ref)
        do = do_ref[...]
        dh = do * _gelu(gate)
        dgate = do * h * _gelu_grad(gate)
        row = lax.broadcasted_iota(jnp.int32, (ts, w), 0)
        ca = jnp.where(row < ts - 1, pltpu.roll(a, ts - 1, 0), 1.0)
        cb = dh
        d = 1
        while d < ts:
            an = jnp.where(row < ts - d, pltpu.roll(ca, ts - d, 0), 1.0)
            bn = jnp.where(row < ts - d, pltpu.roll(cb, ts - d, 0), 0.0)
            cb = cb + ca * bn
            ca = ca * an
            d *= 2
        lamb = cb + ca * carry[0:1, :]
        carry[0:1, :] = a[0:1, :] * lamb[0:1, :]
        h_before = _shift_rows(h, hprev, 1)
        da = lamb * h_before
        ix = ig * xc
        dmult = lamb * ix
        dig = lamb * mult * xc
        dxc = lamb * mult * ig
        dla = (da - dmult * a / mult) * a
        dr = dla * LRU_C * ls
        dlam_ref[...] += jnp.sum(dla * LRU_C * r, axis=0, keepdims=True) * _sigmoid(-lam_ref[...])
        dpa = dr * r * (1.0 - r)
        dpx = dig * ig * (1.0 - ig)
        dba_ref[...] += jnp.sum(dpa, axis=0, keepdims=True)
        dbx_ref[...] += jnp.sum(dpx, axis=0, keepdims=True)
        dpab = dpa.astype(BF16)
        dpxb = dpx.astype(BF16)
        xcb = xc.astype(BF16)
        dxc = dxc + _mm_nt(dpab, wa_ref[...].astype(BF16)) + _mm_nt(dpxb, wx_ref[...].astype(BF16))
        dwa_ref[...] += _mm_tn(xcb, dpab)
        dwx_ref[...] += _mm_tn(xcb, dpxb)
        dcb_ref[...] += jnp.sum(dxc, axis=0, keepdims=True)
        nxt = dxc_next[...]
        dx = cw_ref[3:4, :] * dxc
        dcw_ref[3:4, :] += jnp.sum(dxc * x, axis=0, keepdims=True)
        for j in (1, 2, 3):
            dx = dx + cw_ref[3 - j:4 - j, :] * _shift_rows_up(dxc, nxt, j)
            dcw_ref[3 - j:4 - j, :] += jnp.sum(dxc * _shift_rows(x, xprev, j), axis=0, keepdims=True)
        dxc_next[...] = dxc
        dhl_ref[:, 0:w] = dx.astype(BF16)
        dhl_ref[:, w:2 * w] = dgate.astype(BF16)

    cst = lambda shape: pl.BlockSpec(shape, lambda i: (0, 0))
    rowr = lambda width: pl.BlockSpec((ts, width), lambda i: (rev(i), 0))
    prevr = lambda width: pl.BlockSpec((ts, width), lambda i: (jnp.maximum(rev(i) - 1, 0), 0))
    return pl.pallas_call(
        body, name="lru_bwd", grid=(nb,),
        in_specs=[rowr(2 * w), prevr(w), cst((4, w)), cst((1, w)), cst((w, w)), cst((1, w)), cst((w, w)), cst((1, w)), cst((1, w)),
                  rowr(w), rowr(w), prevr(w), rowr(w)],
        out_specs=[rowr(2 * w), cst((4, w)), cst((1, w)), cst((w, w)), cst((1, w)), cst((w, w)), cst((1, w)), cst((1, w))],
        out_shape=[jax.ShapeDtypeStruct((s, 2 * w), BF16), jax.ShapeDtypeStruct((4, w), F32), jax.ShapeDtypeStruct((1, w), F32),
                   jax.ShapeDtypeStruct((w, w), F32), jax.ShapeDtypeStruct((1, w), F32), jax.ShapeDtypeStruct((w, w), F32),
                   jax.ShapeDtypeStruct((1, w), F32), jax.ShapeDtypeStruct((1, w), F32)],
        scratch_shapes=[pltpu.VMEM((8, w), F32), pltpu.VMEM((ts, w), F32)],
        compiler_params=_params("arbitrary"),
    )(hl, hl, conv_w, conv_b, w_a, b_a, w_x, b_x, lam, xc_saved, h_saved, h_saved, dout)


GDN_QKV = 3 * GDN_W


def _tri_inverse(nm):
    r = lax.broadcasted_iota(jnp.int32, nm.shape, 0)
    c = lax.broadcasted_iota(jnp.int32, nm.shape, 1)
    t = (r == c).astype(F32) - nm
    p = nm
    for _ in range(5):
        p = _mm3(p, p)
        t = t + _mm3(t, p)
    return t


def _gdn_front(hx_ref, hprev, cw_ref, al_ref, dt_ref):
    w = GDN_W
    x = hx_ref[:, 0:GDN_QKV]
    y = cw_ref[3:4, :] * x
    for j in (1, 2, 3):
        y = y + cw_ref[3 - j:4 - j, :] * _shift_rows(x, hprev, j)
    qkv, dsil = _silu_and_grad(y)
    q, k, v = qkv[:, 0:w], qkv[:, w:2 * w], qkv[:, 2 * w:3 * w]
    rq = lax.rsqrt(_group_sum(q * q) + 1e-6)
    rk = lax.rsqrt(_group_sum(k * k) + 1e-6)
    beta = _sigmoid(hx_ref[:, 5 * w:6 * w])
    sp_in = hx_ref[:, 4 * w:5 * w] + dt_ref[...]
    neg_a = -jnp.exp(al_ref[...])
    g = neg_a * _softplus(sp_in)
    gc = _rows_prefix_sum(g)
    return dict(x=x, dsil=dsil, qn=q * rq, kn=k * rk, v=v, rq=rq, rk=rk, beta=beta, sp_in=sp_in, neg_a=neg_a, g=g, gc=gc)


def _gdn_head(fr, hd, tri):
    lower, strict = tri
    hs = lambda a: a[:, HEAD * hd:HEAD * (hd + 1)]
    k = hs(fr["kn"])
    q = hs(fr["qn"]) * (HEAD ** -0.5)
    v = hs(fr["v"])
    beta = hs(fr["beta"])
    gc = hs(fr["gc"])
    e = jnp.exp(gc)
    gl = gc[CHUNK - 1:CHUNK, :]
    xt = jnp.exp(gl - gc)
    dec = jnp.where(lower, jnp.exp(jnp.minimum(gc - gc.T, 0.0)), 0.0)
    kk = _mm_nt(k, k)
    qkr = _mm_nt(q, k)
    return dict(k=k, q=q, v=v, beta=beta, e=e, egl=jnp.exp(gl), xt=xt, dec=dec, kk=kk, qkr=qkr,
                nm=jnp.where(strict, beta * kk * dec, 0.0))


def _tri_masks():
    r = lax.broadcasted_iota(jnp.int32, (CHUNK, CHUNK), 0)
    c = lax.broadcasted_iota(jnp.int32, (CHUNK, CHUNK), 1)
    return r >= c, r > c


def gdn_fwd(hx, conv_w, a_log_e, dt_bias_e, norm_g_e):
    s = hx.shape[0]
    n_chunks = s // CHUNK
    w = GDN_W

    def body(hx_ref, hp_ref, cw_ref, al_ref, dt_ref, ng_ref, o_ref, opre_ref, t_ref, st_ref, state):
        n = pl.program_id(0)

        @pl.when(n == 0)
        def _():
            state[...] = jnp.zeros_like(state)

        fr = _gdn_front(hx_ref, hp_ref[...] * (n > 0).astype(F32), cw_ref, al_ref, dt_ref)
        tri = _tri_masks()
        st_ref[...] = state[...]
        for hd in range(GDN_HEADS):
            win = slice(HEAD * hd, HEAD * (hd + 1))
            hq = _gdn_head(fr, hd, tri)
            t = _tri_inverse(hq["nm"])
            t_ref[:, win] = t
            u = _mm(t, hq["v"] * hq["beta"])
            wk = _mm(t, hq["k"] * (hq["beta"] * hq["e"]))
            st = state[:, win]
            vnew = u - _mm(wk, st)
            opre_ref[:, win] = _mm(hq["q"] * hq["e"], st) + _mm(hq["qkr"] * hq["dec"], vnew)
            state[:, win] = st * hq["egl"] + _mm_tn(hq["k"] * hq["xt"], vnew)
        o = opre_ref[...]
        rinv = lax.rsqrt(_group_sum(o * o) * (1.0 / HEAD) + 1e-6)
        z = hx_ref[:, 3 * w:4 * w]
        o_ref[...] = (o * rinv) * ng_ref[...] * (z * _sigmoid(z))

    cst = lambda shape: pl.BlockSpec(shape, lambda i: (0, 0))
    row = lambda width: pl.BlockSpec((CHUNK, width), lambda i: (i, 0))
    return pl.pallas_call(
        body, name="gdn_fwd", grid=(n_chunks,),
        in_specs=[row(GDN_IN), pl.BlockSpec((CHUNK, GDN_QKV), lambda i: (jnp.maximum(i - 1, 0), 0)),
                  cst((4, GDN_QKV)), cst((1, w)), cst((1, w)), cst((1, w))],
        out_specs=[row(w)] * 4,
        out_shape=[jax.ShapeDtypeStruct((s, w), F32)] * 4,
        scratch_shapes=[pltpu.VMEM((CHUNK, w), F32)],
        compiler_params=_params("arbitrary"),
    )(hx, hx, conv_w, a_log_e, dt_bias_e, norm_g_e)


def gdn_bwd(hx, conv_w, a_log_e, dt_bias_e, norm_g_e, opre, tmat, states, dout):
    s = hx.shape[0]
    n_chunks = s // CHUNK
    w = GDN_W
    rev = lambda i: n_chunks - 1 - i

    def body(hx_ref, hp_ref, cw_ref, al_ref, dt_ref, ng_ref, opre_ref, t_ref, st_ref, do_ref,
             dhx_ref, dcw_ref, dal_ref, ddt_ref, dng_ref, dstate, dy_next, dq_buf, dk_buf, dv_buf, dbeta_buf, dgc_buf):
        i = pl.program_id(0)
        n = n_chunks - 1 - i

        @pl.when(i == 0)
        def _():
            dstate[...] = jnp.zeros_like(dstate)
            dy_next[...] = jnp.zeros_like(dy_next)
            for ref in (dcw_ref, dal_ref, ddt_ref, dng_ref):
                ref[...] = jnp.zeros_like(ref)

        hprev = hp_ref[...] * (n > 0).astype(F32)
        fr = _gdn_front(hx_ref, hprev, cw_ref, al_ref, dt_ref)
        tri = _tri_masks()
        lower, strict = tri
        o = opre_ref[...]
        rinv = lax.rsqrt(_group_sum(o * o) * (1.0 / HEAD) + 1e-6)
        yn = o * rinv
        z = hx_ref[:, 3 * w:4 * w]
        sil, dsil_z = _silu_and_grad(z)
        dout_v = do_ref[...]
        ng = ng_ref[...]
        dng_ref[...] += jnp.sum(dout_v * yn * sil, axis=0, keepdims=True)
        dz = dout_v * yn * ng * dsil_z
        dyn = dout_v * ng * sil
        d_o = rinv * (dyn - yn * (_group_sum(dyn * yn) * (1.0 / HEAD)))
        last_row = (lax.broadcasted_iota(jnp.int32, (CHUNK, HEAD), 0) == CHUNK - 1).astype(F32)
        rowsum = lambda m: jnp.sum(m, axis=1, keepdims=True)
        for hd in range(GDN_HEADS):
            win = slice(HEAD * hd, HEAD * (hd + 1))
            hq = _gdn_head(fr, hd, tri)
            k, q, v, beta, e, xt, dec, kk, qkr = (hq[n_] for n_ in ("k", "q", "v", "beta", "e", "xt", "dec", "kk", "qkr"))
            t = t_ref[:, win]
            st = st_ref[:, win]
            dsn = dstate[:, win]
            do_h = d_o[:, win]
            u = _mm(t, v * beta)
            wk = _mm(t, k * (beta * e))
            vnew = u - _mm(wk, st)
            qk = qkr * dec
            kt = k * xt
            dqd = _mm_nt(do_h, st)
            ds = _mm_tn(q * e, do_h)
            dqk = _mm_nt(do_h, vnew)
            dvnew = _mm_tn(qk, do_h) + _mm(kt, dsn)
            dkt = _mm_nt(vnew, dsn)
            ds = ds + hq["egl"] * dsn
            dgl = jnp.sum(rowsum(dsn * st), axis=0, keepdims=True) * hq["egl"]
            dwk = -_mm_nt(dvnew, st)
            ds = ds - _mm_tn(wk, dvnew)
            drv = _mm_tn(t, dvnew)
            drk = _mm_tn(t, dwk)
            dnm = jnp.where(strict, -(_mm_nt(drv, u) + _mm_nt(drk, wk)), 0.0)
            dbeta = rowsum(dnm * kk * dec)
            dkk = dnm * beta * dec
            ddec = dnm * beta * kk + dqk * qkr
            mq = dqk * dec
            dq = _mm(mq, k) + dqd * e
            dk = _mm_tn(mq, q) + _mm(dkk, k) + _mm_tn(dkk, k) + drk * (beta * e) + dkt * xt
            dv_buf[:, win] = drv * beta
            rks = rowsum(drk * k)
            dbeta = dbeta + rowsum(drv * v) + rks * e
            de = rks * beta + rowsum(dqd * q)
            dxt = rowsum(dkt * k) * xt
            dgl = dgl + jnp.sum(dxt, axis=0, keepdims=True)
            dd = ddec * dec
            dgc = de * e - dxt + rowsum(dd) - rowsum(dd.T) + last_row * dgl
            dstate[:, win] = ds
            dq_buf[:, win] = dq * (HEAD ** -0.5)
            dk_buf[:, win] = dk
            dbeta_buf[:, win] = dbeta + jnp.zeros((CHUNK, HEAD), F32)
            dgc_buf[:, win] = dgc
        dg = _rows_suffix_sum(dgc_buf[...])
        dal_ref[...] += jnp.sum(dg * fr["g"], axis=0, keepdims=True)
        da = dg * fr["neg_a"] * _sigmoid(fr["sp_in"])
        ddt_ref[...] += jnp.sum(da, axis=0, keepdims=True)
        beta_all = fr["beta"]
        db = dbeta_buf[...] * beta_all * (1.0 - beta_all)
        dqn = dq_buf[...]
        dkn = dk_buf[...]
        dq_raw = fr["rq"] * (dqn - fr["qn"] * _group_sum(dqn * fr["qn"]))
        dk_raw = fr["rk"] * (dkn - fr["kn"] * _group_sum(dkn * fr["kn"]))
        dy = jnp.concatenate([dq_raw, dk_raw, dv_buf[...]], axis=1) * fr["dsil"]
        nxt = dy_next[...]
        x = fr["x"]
        dx = cw_ref[3:4, :] * dy
        dcw_ref[3:4, :] += jnp.sum(dy * x, axis=0, keepdims=True)
        for j in (1, 2, 3):
            dx = dx + cw_ref[3 - j:4 - j, :] * _shift_rows_up(dy, nxt, j)
            dcw_ref[3 - j:4 - j, :] += jnp.sum(dy * _shift_rows(x, hprev, j), axis=0, keepdims=True)
        dy_next[...] = dy
        dhx_ref[:, 0:GDN_QKV] = dx.astype(BF16)
        dhx_ref[:, 3 * w:4 * w] = dz.astype(BF16)
        dhx_ref[:, 4 * w:5 * w] = (da * (1.0 / HEAD)).astype(BF16)
        dhx_ref[:, 5 * w:6 * w] = (db * (1.0 / HEAD)).astype(BF16)

    cst = lambda shape: pl.BlockSpec(shape, lambda i: (0, 0))
    row = lambda width: pl.BlockSpec((CHUNK, width), lambda i: (rev(i), 0))
    buf = lambda width: pltpu.VMEM((CHUNK, width), F32)
    return pl.pallas_call(
        body, name="gdn_bwd", grid=(n_chunks,),
        in_specs=[row(GDN_IN), pl.BlockSpec((CHUNK, GDN_QKV), lambda i: (jnp.maximum(rev(i) - 1, 0), 0)),
                  cst((4, GDN_QKV)), cst((1, w)), cst((1, w)), cst((1, w)), row(w), row(w), row(w), row(w)],
        out_specs=[row(GDN_IN), cst((4, GDN_QKV)), cst((1, w)), cst((1, w)), cst((1, w))],
        out_shape=[jax.ShapeDtypeStruct((s, GDN_IN), BF16), jax.ShapeDtypeStruct((4, GDN_QKV), F32),
                   jax.ShapeDtypeStruct((1, w), F32), jax.ShapeDtypeStruct((1, w), F32), jax.ShapeDtypeStruct((1, w), F32)],
        scratch_shapes=[buf(w), buf(GDN_QKV), buf(w), buf(w), buf(w), buf(w), buf(w)],
        compiler_params=_params("arbitrary"),
    )(hx, hx, conv_w, a_log_e, dt_bias_e, norm_g_e, opre, tmat, states, dout)


_MESH = pl.DeviceIdType.MESH


def all_gather8(x, name):
    m, n = x.shape

    def body(x_ref, out_ref, send_sems, recv_sems, local_sem):
        px, py, pc = lax.axis_index("x"), lax.axis_index("y"), lax.axis_index("c")
        me, sibling = (px, py, pc), (px, py, 1 - pc)
        chips = [(1 - px, py), (px, 1 - py), (1 - px, 1 - py)]

        def slot(dx, dy, dc):
            return out_ref.at[4 * dx + 2 * dy + dc]

        def copy(k, block, to, src=None):
            return pltpu.make_async_remote_copy(
                src_ref=slot(*block) if src is None else src, dst_ref=slot(*block),
                send_sem=send_sems.at[k], recv_sem=recv_sems.at[k], device_id=to, device_id_type=_MESH)

        mine = pltpu.make_async_copy(x_ref, slot(*me), local_sem)
        mine.start()
        first = [copy(0, me, sibling, src=x_ref)]
        first += [copy(1 + j, me, (*chip, pc), src=x_ref) for j, chip in enumerate(chips)]
        for cp in first:
            cp.start()
        passed = [copy(4 + j, (*chip, pc), sibling) for j, chip in enumerate(chips)]
        for j, chip in enumerate(chips):
            copy(1 + j, (*chip, pc), me).wait_recv()
            passed[j].start()
        copy(0, sibling, me).wait_recv()
        for j, chip in enumerate(chips):
            copy(4 + j, (*chip, 1 - pc), me).wait_recv()
        for cp in first + passed:
            cp.wait_send()
        mine.wait()

    return pl.pallas_call(
        body, name=name, out_shape=jax.ShapeDtypeStruct((N_DEV, m, n), x.dtype),
        in_specs=[_ANY], out_specs=_ANY,
        scratch_shapes=[pltpu.SemaphoreType.DMA((7,)), pltpu.SemaphoreType.DMA((7,)), pltpu.SemaphoreType.DMA],
    )(x)


def sibling_swap_half(g, name):
    shape = g.shape[1:]

    def body(g_ref, out_ref, send_sem, recv_sem):
        px, py, pc = lax.axis_index("x"), lax.axis_index("y"), lax.axis_index("c")
        cp = pltpu.make_async_remote_copy(
            src_ref=g_ref.at[1 - pc], dst_ref=out_ref, send_sem=send_sem, recv_sem=recv_sem,
            device_id=(px, py, 1 - pc), device_id_type=_MESH)
        cp.start()
        cp.wait()

    return pl.pallas_call(
        body, name=name, out_shape=jax.ShapeDtypeStruct(shape, g.dtype), in_specs=[_ANY], out_specs=_ANY,
        scratch_shapes=[pltpu.SemaphoreType.DMA, pltpu.SemaphoreType.DMA],
    )(g)


def chip_exchange(p, name):
    def body(p_ref, q_ref, send_sems, recv_sems, local_sem):
        px, py, pc = lax.axis_index("x"), lax.axis_index("y"), lax.axis_index("c")
        mine = 2 * px + py
        chips = [(1 - px, py), (px, 1 - py), (1 - px, 1 - py)]
        local = pltpu.make_async_copy(p_ref.at[mine], q_ref.at[mine], local_sem)
        local.start()
        cps = []
        for j, (cx, cy) in enumerate(chips):
            cps.append(pltpu.make_async_remote_copy(
                src_ref=p_ref.at[2 * cx + cy], dst_ref=q_ref.at[mine], send_sem=send_sems.at[j], recv_sem=recv_sems.at[j],
                device_id=(cx, cy, pc), device_id_type=_MESH))
        for cp in cps:
            cp.start()
        for j, (cx, cy) in enumerate(chips):
            pltpu.make_async_remote_copy(
                src_ref=p_ref.at[mine], dst_ref=q_ref.at[2 * cx + cy], send_sem=send_sems.at[j], recv_sem=recv_sems.at[j],
                device_id=(cx, cy, pc), device_id_type=_MESH).wait_recv()
        for cp in cps:
            cp.wait_send()
        local.wait()

    return pl.pallas_call(
        body, name=name, out_shape=jax.ShapeDtypeStruct(p.shape, p.dtype), in_specs=[_ANY], out_specs=_ANY,
        scratch_shapes=[pltpu.SemaphoreType.DMA((3,)), pltpu.SemaphoreType.DMA((3,)), pltpu.SemaphoreType.DMA],
    )(p)


def sibling_gather(x, name):
    m, n = x.shape

    def body(x_ref, out_ref, send_sem, recv_sem, local_sem):
        px, py, pc = lax.axis_index("x"), lax.axis_index("y"), lax.axis_index("c")
        local = pltpu.make_async_copy(x_ref, out_ref.at[pc], local_sem)
        local.start()
        cp = pltpu.make_async_remote_copy(
            src_ref=x_ref, dst_ref=out_ref.at[pc], send_sem=send_sem, recv_sem=recv_sem,
            device_id=(px, py, 1 - pc), device_id_type=_MESH)
        cp.start()
        pltpu.make_async_remote_copy(
            src_ref=x_ref, dst_ref=out_ref.at[1 - pc], send_sem=send_sem, recv_sem=recv_sem,
            device_id=(px, py, 1 - pc), device_id_type=_MESH).wait_recv()
        cp.wait_send()
        local.wait()

    return pl.pallas_call(
        body, name=name, out_shape=jax.ShapeDtypeStruct((2, m, n), x.dtype), in_specs=[_ANY], out_specs=_ANY,
        scratch_shapes=[pltpu.SemaphoreType.DMA, pltpu.SemaphoreType.DMA, pltpu.SemaphoreType.DMA],
    )(x)


ELT_TILE = 128


def _elt_rows(m):
    for t in (512, 256, ELT_TILE, 16, 8):
        if m % t == 0:
            return t
    return m


def pair_add(a, b, name):
    m, n = a.shape
    tm = _elt_rows(m)

    def body(a_ref, b_ref, o_ref):
        o_ref[...] = (a_ref[...].astype(F32) + b_ref[...].astype(F32)).astype(o_ref.dtype)

    return pl.pallas_call(
        body, name=name, grid=(m // tm,), in_specs=[_row_spec(tm, n)] * 2, out_specs=_row_spec(tm, n),
        out_shape=jax.ShapeDtypeStruct((m, n), a.dtype), compiler_params=_params("arbitrary"),
    )(a, b)


def sum_leading(q, name):
    kk, m, n = q.shape
    tm = _elt_rows(m)

    def body(q_ref, o_ref):
        acc = q_ref[0].astype(F32)
        for i in range(1, kk):
            acc = acc + q_ref[i].astype(F32)
        o_ref[...] = acc

    return pl.pallas_call(
        body, name=name, grid=(m // tm,), in_specs=[pl.BlockSpec((kk, tm, n), lambda i: (0, i, 0))],
        out_specs=_row_spec(tm, n), out_shape=jax.ShapeDtypeStruct((m, n), F32), compiler_params=_params("arbitrary"),
    )(q)


def adamw(w, g, m, v, name):
    rows, cols = w.shape
    tm = _elt_rows(rows)

    def body(w_ref, g_ref, m_ref, v_ref, d_ref, nm_ref, nv_ref):
        gv = g_ref[...]
        nm = ADAM_B1 * m_ref[...] + (1.0 - ADAM_B1) * gv
        nv = ADAM_B2 * v_ref[...] + (1.0 - ADAM_B2) * jnp.square(gv)
        nm_ref[...] = nm
        nv_ref[...] = nv
        m_hat = nm / (1.0 - ADAM_B1 ** ADAM_STEP)
        v_hat = nv / (1.0 - ADAM_B2 ** ADAM_STEP)
        d_ref[...] = -ADAM_LR * (m_hat / (jnp.sqrt(v_hat) + ADAM_EPS) + ADAM_WD * w_ref[...])

    spec = _row_spec(tm, cols)
    return pl.pallas_call(
        body, name=name, grid=(rows // tm,), in_specs=[spec] * 4, out_specs=[spec] * 3,
        out_shape=[jax.ShapeDtypeStruct((rows, cols), F32)] * 3, compiler_params=_params("arbitrary"),
    )(w, g, m, v)


def _block_diag_dense(w):
    g = w.shape[0]
    return jnp.einsum("gij,gh->gihj", w, jnp.eye(g, dtype=w.dtype)).reshape(g * w.shape[1], g * w.shape[2])


def _diag_blocks(m):
    return jnp.stack([m[HEAD * i:HEAD * (i + 1), HEAD * i:HEAD * (i + 1)] for i in range(LRU_BLOCKS)])


def _rep(v):
    return jnp.repeat(v, HEAD, axis=-1)


def _split_w_in(w_in):
    gdn0 = RET_IN + LRU_IN
    gdn1 = gdn0 + 4 * GDN_W
    w_r = w_in[:, 0:RET_IN]
    w_l = w_in[:, RET_IN:gdn0]
    w_g = jnp.concatenate([w_in[:, gdn0:gdn1], _rep(w_in[:, gdn1:gdn1 + GDN_HEADS]), _rep(w_in[:, gdn1 + GDN_HEADS:])], axis=1)
    return w_r, w_l, w_g


def _merge_w_in_grad(d_r, d_l, d_g):
    rows = d_g.shape[0]
    q = 4 * GDN_W
    d_a = d_g[:, q:q + GDN_W].reshape(rows, GDN_HEADS, HEAD).sum(-1)
    d_b = d_g[:, q + GDN_W:].reshape(rows, GDN_HEADS, HEAD).sum(-1)
    return jnp.concatenate([d_r, d_l, d_g[:, 0:q], d_a, d_b], axis=1)


def _local_step(x, p, pos, target, wt):
    row = lambda v: v[None, :]
    saved = []
    for i in range(DEPTH):
        w_r, w_l, w_g = _split_w_in(wt["w_in"][i])
        lw = dict(
            wg1=wt["ffn1_w_gate"][i], wu1=wt["ffn1_w_up"][i], wd1=wt["ffn1_w_down"][i], w_r=w_r, w_l=w_l, w_g=w_g,
            w_out=wt["w_out"][i], wg2=wt["ffn2_w_gate"][i], wu2=wt["ffn2_w_up"][i], wd2=wt["ffn2_w_down"][i],
            wpg=wt["ple_w_gate"][i], wpp=wt["ple_w_proj"][i],
            wa=_block_diag_dense(wt["lru_w_a"][i]), wx=_block_diag_dense(wt["lru_w_x"][i]),
            al=row(_rep(wt["gdn_a_log"][i])), dt=row(_rep(wt["gdn_dt_bias"][i])), ng=row(jnp.tile(wt["gdn_norm_g"][i], GDN_HEADS)))
        hg1, hu1, r1, x1 = ffn_fwd(x, row(wt["ln_ffn1_g"][i]), row(wt["ln_ffn1_b"][i]), lw["wg1"], lw["wu1"], lw["wd1"])
        hr, hl, hgd = win_fwd(x1, w_r, w_l, w_g)
        o_r, opre_r, st_r = ret_fwd(hr, pos, row(wt["ret_norm_g"][i]))
        o_l, xc, hs = lru_fwd(hl, wt["lru_conv_w"][i], row(wt["lru_conv_b"][i]), lw["wa"], row(wt["lru_b_a"][i]), lw["wx"],
                              row(wt["lru_b_x"][i]), row(wt["lru_lambda"][i]))
        o_g, opre_g, tmat, st_g = gdn_fwd(hgd, wt["gdn_conv_w"][i], lw["al"], lw["dt"], lw["ng"])
        r2, x2 = out_fwd(o_r, o_l, o_g, x1, lw["w_out"], row(wt["ln_mix_g"][i]), row(wt["ln_mix_b"][i]))
        hg2, hu2, r3, x3, pg, pp = ffn_fwd(x2, row(wt["ln_ffn2_g"][i]), row(wt["ln_ffn2_b"][i]), lw["wg2"], lw["wu2"], lw["wd2"],
                                           ple=(p[i], lw["wpg"], lw["wpp"]))
        saved.append(dict(lw=lw, x0=x, hg1=hg1, hu1=hu1, r1=r1, x1=x1, hr=hr, hl=hl, hgd=hgd, o_r=o_r, opre_r=opre_r, st_r=st_r,
                          o_l=o_l, xc=xc, hs=hs, o_g=o_g, opre_g=opre_g, tmat=tmat, st_g=st_g, r2=r2, x2=x2, hg2=hg2, hu2=hu2,
                          r3=r3, pg=pg, pp=pp))
        x = x3

    dx, loss = loss_and_grad(x, target)
    grads = [None] * DEPTH
    for i in reversed(range(DEPTH)):
        sv = saved[i]
        lw = sv["lw"]
        tag = f"_l{i}"
        dx2, act2, dhg2, dhu2, dy2, dg3, db3, dpg, dpp = ffn_bwd(
            dx, sv["r3"], sv["x2"], sv["hg2"], sv["hu2"], row(wt["ln_ffn2_g"][i]), lw["wg2"], lw["wu2"], lw["wd2"],
            ple=(sv["pg"], sv["pp"], lw["wpg"]))
        g = {}
        g["ffn2_w_gate"] = wgrad(sv["x2"], dhg2, "wgrad_gate2" + tag)
        g["ffn2_w_up"] = wgrad(sv["x2"], dhu2, "wgrad_up2" + tag)
        g["ffn2_w_down"] = wgrad(act2, dy2, "wgrad_down2" + tag)
        g["ple_w_gate"] = wgrad(sv["x2"], dpg, "wgrad_pgate" + tag)
        g["ple_w_proj"] = wgrad(p[i], dpp, "wgrad_pproj" + tag)
        g["ln_ffn2_g"], g["ln_ffn2_b"] = dg3[0], db3[0]
        dr2, dr2b, do_r, do_l, do_g, dg2, db2 = out_bwd(dx2, sv["r2"], row(wt["ln_mix_g"][i]), lw["w_out"])
        g["ln_mix_g"], g["ln_mix_b"] = dg2[0], db2[0]
        g["w_out"] = jnp.concatenate([wgrad(sv["o_r"], dr2b, "wgrad_out_r" + tag), wgrad(sv["o_l"], dr2b, "wgrad_out_l" + tag),
                                      wgrad(sv["o_g"], dr2b, "wgrad_out_g" + tag)], axis=0)
        dhr, dgn = ret_bwd(sv["hr"], pos, row(wt["ret_norm_g"][i]), sv["opre_r"], sv["st_r"], do_r)
        g["ret_norm_g"] = dgn[0]
        dhl, dcw, dcb, dwa, dba, dwx, dbx, dlam = lru_bwd(
            sv["hl"], wt["lru_conv_w"][i], row(wt["lru_conv_b"][i]), lw["wa"], row(wt["lru_b_a"][i]), lw["wx"],
            row(wt["lru_b_x"][i]), row(wt["lru_lambda"][i]), sv["xc"], sv["hs"], do_l)
        g["lru_conv_w"], g["lru_conv_b"] = dcw, dcb[0]
        g["lru_w_a"], g["lru_b_a"], g["lru_w_x"], g["lru_b_x"], g["lru_lambda"] = _diag_blocks(dwa), dba[0], _diag_blocks(dwx), dbx[0], dlam[0]
        dhgd, dgcw, dal, ddt, dng = gdn_bwd(sv["hgd"], wt["gdn_conv_w"][i], lw["al"], lw["dt"], lw["ng"], sv["opre_g"], sv["tmat"],
                                            sv["st_g"], do_g)
        g["gdn_conv_w"] = dgcw
        g["gdn_a_log"], g["gdn_dt_bias"] = dal[0, ::HEAD], ddt[0, ::HEAD]
        g["gdn_norm_g"] = dng[0].reshape(GDN_HEADS, HEAD).sum(0)
        dx1 = win_bwd(dr2, dhr, dhl, dhgd, lw["w_r"], lw["w_l"], lw["w_g"])
        g["w_in"] = _merge_w_in_grad(wgrad(sv["x1"], dhr, "wgrad_in_r" + tag), wgrad(sv["x1"], dhl, "wgrad_in_l" + tag),
                                     wgrad(sv["x1"], dhgd, "wgrad_in_g" + tag))
        dx, act1, dhg1, dhu1, dy1, dg1, db1 = ffn_bwd(dx1, sv["r1"], sv["x0"], sv["hg1"], sv["hu1"], row(wt["ln_ffn1_g"][i]),
                                                      lw["wg1"], lw["wu1"], lw["wd1"])
        g["ffn1_w_gate"] = wgrad(sv["x0"], dhg1, "wgrad_gate1" + tag)
        g["ffn1_w_up"] = wgrad(sv["x0"], dhu1, "wgrad_up1" + tag)
        g["ffn1_w_down"] = wgrad(act1, dy1, "wgrad_down1" + tag)
        g["ln_ffn1_g"], g["ln_ffn1_b"] = dg1[0], db1[0]
        grads[i] = g
    return loss, dx, {k: jnp.stack([grads[i][k] for i in range(DEPTH)]) for k in grads[0]}


_SPLIT = dict(ffn1_w_gate=2, ffn1_w_up=2, ffn1_w_down=1, w_in=2, w_out=1, ffn2_w_gate=2, ffn2_w_up=2, ffn2_w_down=1,
              ple_w_gate=1, ple_w_proj=2)
_CONV = ("lru_conv_w", "gdn_conv_w")
_WHOLE = ("ln_ffn1_g", "ln_ffn1_b", "ret_norm_g", "lru_conv_b", "lru_w_a", "lru_b_a", "lru_w_x", "lru_b_x", "lru_lambda",
          "gdn_a_log", "gdn_dt_bias", "gdn_norm_g", "ln_mix_g", "ln_mix_b", "ln_ffn2_g", "ln_ffn2_b")
_WEIGHTS = ("ln_ffn1_g", "ln_ffn1_b", "ffn1_w_gate", "ffn1_w_up", "ffn1_w_down", "w_in", "ret_norm_g", "lru_conv_w", "lru_conv_b",
            "lru_w_a", "lru_b_a", "lru_w_x", "lru_b_x", "lru_lambda", "gdn_conv_w", "gdn_a_log", "gdn_dt_bias", "gdn_norm_g",
            "w_out", "ln_mix_g", "ln_mix_b", "ffn2_w_gate", "ffn2_w_up", "ffn2_w_down", "ple_w_gate", "ple_w_proj",
            "ln_ffn2_g", "ln_ffn2_b")
_INPUTS = ("x", "p", "positions") + _WEIGHTS + ("loss_target",) + tuple("m_" + n for n in _WEIGHTS) + tuple("v_" + n for n in _WEIGHTS)

BIG_COLS = 1024
BIG_HALF_ROWS_MULT = 128
SMALL_COLS = LANES
SMALL_ROWS_MULT = 8


def _pack(arrays, dtype, cols, rows_mult):
    flat = jnp.concatenate([a.reshape(-1).astype(dtype) for a in arrays])
    rows = -(-flat.shape[0] // cols)
    rows = -(-rows // rows_mult) * rows_mult
    return jnp.pad(flat, (0, rows * cols - flat.shape[0])).reshape(rows, cols)


def _unpack(packed, shapes):
    flat = packed.reshape(-1)
    out, off = [], 0
    for shp in shapes:
        size = int(np.prod(shp))
        out.append(flat[off:off + size].reshape(shp))
        off += size
    return out


def _as2d(a):
    return a.reshape(-1, a.shape[-1])


def kernel(x, p, positions, ln_ffn1_g, ln_ffn1_b, ffn1_w_gate, ffn1_w_up, ffn1_w_down, w_in, ret_norm_g, lru_conv_w, lru_conv_b, lru_w_a, lru_b_a, lru_w_x, lru_b_x, lru_lambda, gdn_conv_w, gdn_a_log, gdn_dt_bias, gdn_norm_g, w_out, ln_mix_g, ln_mix_b, ffn2_w_gate, ffn2_w_up, ffn2_w_down, ple_w_gate, ple_w_proj, ln_ffn2_g, ln_ffn2_b, loss_target, m_ln_ffn1_g, m_ln_ffn1_b, m_ffn1_w_gate, m_ffn1_w_up, m_ffn1_w_down, m_w_in, m_ret_norm_g, m_lru_conv_w, m_lru_conv_b, m_lru_w_a, m_lru_b_a, m_lru_w_x, m_lru_b_x, m_lru_lambda, m_gdn_conv_w, m_gdn_a_log, m_gdn_dt_bias, m_gdn_norm_g, m_w_out, m_ln_mix_g, m_ln_mix_b, m_ffn2_w_gate, m_ffn2_w_up, m_ffn2_w_down, m_ple_w_gate, m_ple_w_proj, m_ln_ffn2_g, m_ln_ffn2_b, v_ln_ffn1_g, v_ln_ffn1_b, v_ffn1_w_gate, v_ffn1_w_up, v_ffn1_w_down, v_w_in, v_ret_norm_g, v_lru_conv_w, v_lru_conv_b, v_lru_w_a, v_lru_b_a, v_lru_w_x, v_lru_b_x, v_lru_lambda, v_gdn_conv_w, v_gdn_a_log, v_gdn_dt_bias, v_gdn_norm_g, v_w_out, v_ln_mix_g, v_ln_mix_b, v_ffn2_w_gate, v_ffn2_w_up, v_ffn2_w_down, v_ple_w_gate, v_ple_w_proj, v_ln_ffn2_g, v_ln_ffn2_b):
    a = dict(zip(_INPUTS, (x, p, positions, ln_ffn1_g, ln_ffn1_b, ffn1_w_gate, ffn1_w_up, ffn1_w_down, w_in, ret_norm_g, lru_conv_w, lru_conv_b, lru_w_a, lru_b_a, lru_w_x, lru_b_x, lru_lambda, gdn_conv_w, gdn_a_log, gdn_dt_bias, gdn_norm_g, w_out, ln_mix_g, ln_mix_b, ffn2_w_gate, ffn2_w_up, ffn2_w_down, ple_w_gate, ple_w_proj, ln_ffn2_g, ln_ffn2_b, loss_target, m_ln_ffn1_g, m_ln_ffn1_b, m_ffn1_w_gate, m_ffn1_w_up, m_ffn1_w_down, m_w_in, m_ret_norm_g, m_lru_conv_w, m_lru_conv_b, m_lru_w_a, m_lru_b_a, m_lru_w_x, m_lru_b_x, m_lru_lambda, m_gdn_conv_w, m_gdn_a_log, m_gdn_dt_bias, m_gdn_norm_g, m_w_out, m_ln_mix_g, m_ln_mix_b, m_ffn2_w_gate, m_ffn2_w_up, m_ffn2_w_down, m_ple_w_gate, m_ple_w_proj, m_ln_ffn2_g, m_ln_ffn2_b, v_ln_ffn1_g, v_ln_ffn1_b, v_ffn1_w_gate, v_ffn1_w_up, v_ffn1_w_down, v_w_in, v_ret_norm_g, v_lru_conv_w, v_lru_conv_b, v_lru_w_a, v_lru_b_a, v_lru_w_x, v_lru_b_x, v_lru_lambda, v_gdn_conv_w, v_gdn_a_log, v_gdn_dt_bias, v_gdn_norm_g, v_w_out, v_ln_mix_g, v_ln_mix_b, v_ffn2_w_gate, v_ffn2_w_up, v_ffn2_w_down, v_ple_w_gate, v_ple_w_proj, v_ln_ffn2_g, v_ln_ffn2_b)))
    assert len(a) == len(_INPUTS)
    core = lax.axis_index("c")
    chip = 2 * lax.axis_index("x") + lax.axis_index("y")
    big = list(_SPLIT)

    packed_w = _pack([a[n] for n in big], BF16, BIG_COLS, 2 * BIG_HALF_ROWS_MULT)
    half_rows = packed_w.shape[0] // 2
    my_half = lax.dynamic_slice_in_dim(packed_w, core * half_rows, half_rows, axis=0)
    gathered = all_gather8(my_half, "gather_weights").reshape(N_CHIPS, 2 * half_rows * BIG_COLS)
    wt = {}
    off = 0
    for n in big:
        shp = a[n].shape
        size = int(np.prod(shp))
        parts = gathered[:, off:off + size].reshape((N_CHIPS,) + shp)
        wt[n] = jnp.concatenate([parts[k] for k in range(N_CHIPS)], axis=_SPLIT[n])
        off += size
    conv_g = all_gather8(_pack([a[n] for n in _CONV], F32, SMALL_COLS, SMALL_ROWS_MULT), "gather_conv_weights")[0::2]
    conv_g = conv_g.reshape(N_CHIPS, -1)
    off = 0
    for n in _CONV:
        shp = a[n].shape
        size = int(np.prod(shp))
        parts = conv_g[:, off:off + size].reshape((N_CHIPS,) + shp)
        wt[n] = jnp.concatenate([parts[k] for k in range(N_CHIPS)], axis=2)
        off += size
    for n in _WHOLE:
        wt[n] = a[n]

    seq = a["x"].shape[1]
    loss_part, dx, grads = _local_step(a["x"][0], a["p"][:, 0], a["positions"].reshape(seq, 1), a["loss_target"][0], wt)
    loss = lax.psum(loss_part[0, 0], ("x", "y", "c"))

    def shard_of(g, n, k):
        width = a[n].shape[_SPLIT[n]]
        return lax.slice_in_dim(g, k * width, (k + 1) * width, axis=_SPLIT[n])

    per_chip = [_pack([shard_of(grads[n], n, k) for n in big], BF16, BIG_COLS, 2 * BIG_HALF_ROWS_MULT) for k in range(N_CHIPS)]
    halves = jnp.stack(per_chip).reshape(N_CHIPS, 2, half_rows, BIG_COLS).transpose(1, 0, 2, 3)
    from_sibling = sibling_swap_half(halves, "reduce_pair_swap")
    kept = lax.dynamic_index_in_dim(halves, core, axis=0, keepdims=False)
    pair = pair_add(kept.reshape(N_CHIPS * half_rows, BIG_COLS), from_sibling.reshape(N_CHIPS * half_rows, BIG_COLS),
                    "reduce_pair_add").reshape(N_CHIPS, half_rows, BIG_COLS)
    arrived = chip_exchange(pair, "reduce_chip_exchange")
    my_half_sum = sum_leading(arrived, "reduce_chip_sum")
    reduced = sibling_gather(my_half_sum, "reduce_pair_gather").reshape(2 * half_rows, BIG_COLS)
    big_grads = dict(zip(big, _unpack(reduced, [a[n].shape for n in big])))

    small_names = list(_WHOLE) + list(_CONV)
    small_local = _pack([grads[n] for n in small_names], F32, SMALL_COLS, SMALL_ROWS_MULT)
    small_sum = sum_leading(all_gather8(small_local, "gather_small_grads"), "sum_small_grads")
    small_grads = dict(zip(small_names, _unpack(small_sum, [grads[n].shape for n in small_names])))
    for n in _CONV:
        width = a[n].shape[2]
        small_grads[n] = lax.dynamic_slice_in_dim(small_grads[n], chip * width, width, axis=2)

    new = {}
    for n in big:
        d, nm, nv = adamw(_as2d(a[n]), _as2d(big_grads[n]), _as2d(a["m_" + n]), _as2d(a["v_" + n]), "adamw_" + n)
        new[n] = tuple(t.reshape(a[n].shape) for t in (d, nm, nv))
    pk = lambda prefix: _pack([a[prefix + n] for n in small_names], F32, SMALL_COLS, SMALL_ROWS_MULT)
    pg = _pack([small_grads[n] for n in small_names], F32, SMALL_COLS, SMALL_ROWS_MULT)
    outs = adamw(pk(""), pg, pk("m_"), pk("v_"), "adamw_small")
    shapes = [a[n].shape for n in small_names]
    for n, d, nm, nv in zip(small_names, *[_unpack(o, shapes) for o in outs]):
        new[n] = (d, nm, nv)
    all_grads = {**big_grads, **small_grads}
    return (loss, dx[None], *[all_grads[n] for n in _WEIGHTS], *[new[n][0] for n in _WEIGHTS],
            *[new[n][1] for n in _WEIGHTS], *[new[n][2] for n in _WEIGHTS])
```

```python
import functools
import math

import numpy as np
import jax
import jax.numpy as jnp
from jax import lax
from jax.experimental import pallas as pl
from jax.experimental.pallas import tpu as pltpu

F32 = jnp.float32
BF16 = jnp.bfloat16

D_MODEL = 1024
D_FF = 2816
PLE_DIM = 256
DEPTH = 2
CHUNK = 64
RET_HEADS = 4
RET_W = 256
LRU_W = 384
LRU_BLOCKS = 6
GDN_HEADS = 6
GDN_W = 384
HEAD = 64
D_IN = 3340
RET_IN = 4 * RET_W
LRU_IN = 2 * LRU_W
GDN_IN = 6 * GDN_W
ROPE_THETA = 10000.0
ALPHA = (2 * DEPTH) ** 0.25
LN_EPS = 1e-5
LRU_C = 8.0
N_CHIPS = 4
N_DEV = 8

ADAM_LR = 0.001
ADAM_B1 = 0.9
ADAM_B2 = 0.999
ADAM_EPS = 1e-08
ADAM_WD = 0.01
ADAM_STEP = 10

LANES = 128
VMEM_LIMIT = 56 * 1024 * 1024
ROW_TILE = 256
ROW_TILE_BWD = 512
SCAN_TILE = 256


def _params(*sem):
    return pltpu.CompilerParams(dimension_semantics=sem, vmem_limit_bytes=VMEM_LIMIT)


def _mm(a, b):
    return jnp.dot(a, b, preferred_element_type=F32)


def _mm_nt(a, b):
    return lax.dot_general(a, b, (((1,), (1,)), ((), ())), preferred_element_type=F32)


def _mm_tn(a, b):
    return lax.dot_general(a, b, (((0,), (0,)), ((), ())), preferred_element_type=F32)


def _split(a):
    hi = a.astype(BF16)
    lo = (a - hi.astype(F32)).astype(BF16)
    return hi, lo


def _mm3(a, b):
    ah, al = _split(a)
    bh, bl = _split(b)
    return _mm(ah, bh) + (_mm(ah, bl) + _mm(al, bh))


def _sigmoid(x):
    return jax.nn.sigmoid(x)


def _log1p(u):
    w = 1.0 + u
    return jnp.where(w == 1.0, u, jnp.log(w) * (u / jnp.where(w == 1.0, 1.0, w - 1.0)))


def _expm1(y):
    u = jnp.exp(y)
    um1 = u - 1.0
    safe = jnp.where((u == 1.0) | (um1 == -1.0), 1.0, jnp.log(jnp.where(u == 0.0, 1.0, u)))
    return jnp.where(u == 1.0, y, jnp.where(um1 == -1.0, -1.0, um1 * (y / safe)))


def _softplus(x):
    return jnp.maximum(x, 0.0) + _log1p(jnp.exp(-jnp.abs(x)))


_GELU_C = math.sqrt(2.0 / math.pi)


def _gelu(x):
    return 0.5 * x * (1.0 + jnp.tanh(_GELU_C * (x + 0.044715 * (x * x * x))))


def _gelu_grad(x):
    t = jnp.tanh(_GELU_C * (x + 0.044715 * (x * x * x)))
    return 0.5 * (1.0 + t) + 0.5 * x * (1.0 - t * t) * (_GELU_C * (1.0 + 3.0 * 0.044715 * (x * x)))


def _silu_and_grad(x):
    s = _sigmoid(x)
    return x * s, s * (1.0 + x * (1.0 - s))


def _group_sum_slab(x):
    lane = lax.broadcasted_iota(jnp.int32, x.shape, 1)
    s = x
    for d in (1, 2, 4, 8, 16, 32):
        s = s + jnp.where((lane & d) == 0, pltpu.roll(s, LANES - d, 1), pltpu.roll(s, d, 1))
    return s


def _group_sum(x):
    n = x.shape[1] // LANES
    if n == 1:
        return _group_sum_slab(x)
    return jnp.concatenate([_group_sum_slab(x[:, LANES * i:LANES * (i + 1)]) for i in range(n)], axis=1)


def _rows_prefix_sum(x):
    n = x.shape[0]
    row = lax.broadcasted_iota(jnp.int32, x.shape, 0)
    d = 1
    while d < n:
        x = x + jnp.where(row >= d, pltpu.roll(x, d, 0), 0.0)
        d *= 2
    return x


def _rows_suffix_sum(x):
    n = x.shape[0]
    row = lax.broadcasted_iota(jnp.int32, x.shape, 0)
    d = 1
    while d < n:
        x = x + jnp.where(row < n - d, pltpu.roll(x, n - d, 0), 0.0)
        d *= 2
    return x


def _shift_rows(cur, prev, j):
    row = lax.broadcasted_iota(jnp.int32, cur.shape, 0)
    return jnp.where(row < j, pltpu.roll(prev, j, 0), pltpu.roll(cur, j, 0))


def _shift_rows_up(cur, nxt, j):
    n = cur.shape[0]
    row = lax.broadcasted_iota(jnp.int32, cur.shape, 0)
    return jnp.where(row < n - j, pltpu.roll(cur, n - j, 0), pltpu.roll(nxt, n - j, 0))


def _layer_norm_stats(r):
    mu = jnp.mean(r, axis=-1, keepdims=True)
    d = r - mu
    var = jnp.mean(d * d, axis=-1, keepdims=True)
    rstd = lax.rsqrt(var + LN_EPS)
    return d * rstd, rstd


def _load_resident(step, pairs, sems):
    @pl.when(step == 0)
    def _():
        cps = [pltpu.make_async_copy(h, v, sems.at[i]) for i, (h, v) in enumerate(pairs)]
        for c in cps:
            c.start()
        for c in cps:
            c.wait()


def _row_spec(tile, width):
    return pl.BlockSpec((tile, width), lambda i: (i, 0))


def _full_spec(shape):
    nd = len(shape)
    return pl.BlockSpec(shape, lambda i: (0,) * nd)


_ANY = pl.BlockSpec(memory_space=pl.ANY)


def ffn_fwd(x, ln_g, ln_b, w_gate, w_up, w_down, ple=None):
    s = x.shape[0]
    tm = ROW_TILE
    with_ple = ple is not None
    weights = [w_gate, w_up, w_down] + ([ple[1], ple[2]] if with_ple else [])

    def body(*refs):
        it = iter(refs)
        x_ref, g_ref, b_ref = next(it), next(it), next(it)
        p_ref = next(it) if with_ple else None
        w_hbm = [next(it) for _ in weights]
        hg_ref, hu_ref, r_ref, xn_ref = next(it), next(it), next(it), next(it)
        pg_ref, pp_ref = (next(it), next(it)) if with_ple else (None, None)
        w_vm = [next(it) for _ in weights]
        sems = next(it)
        _load_resident(pl.program_id(0), list(zip(w_hbm, w_vm)), sems)
        xv = x_ref[...]
        xb = xv.astype(BF16)
        hg = _mm(xb, w_vm[0][...])
        hu = _mm(xb, w_vm[1][...])
        hg_ref[...] = hg
        hu_ref[...] = hu
        act = (hg * _sigmoid(hg)) * hu
        r = ALPHA * xv + 0.5 * _mm(act.astype(BF16), w_vm[2][...])
        if with_ple:
            pg = _mm(xb, w_vm[3][...])
            pp = _mm(p_ref[...].astype(BF16), w_vm[4][...])
            pg_ref[...] = pg
            pp_ref[...] = pp
            r = r + _sigmoid(pg) * pp
        r_ref[...] = r
        xhat, _ = _layer_norm_stats(r)
        xn_ref[...] = xhat * g_ref[...] + b_ref[...]

    d, f = D_MODEL, D_FF
    in_specs = [_row_spec(tm, d), _full_spec((1, d)), _full_spec((1, d))]
    args = [x, ln_g, ln_b]
    if with_ple:
        in_specs.append(_row_spec(tm, PLE_DIM))
        args.append(ple[0])
    in_specs += [_ANY] * len(weights)
    args += weights
    out_shape = [jax.ShapeDtypeStruct((s, f), F32), jax.ShapeDtypeStruct((s, f), F32),
                 jax.ShapeDtypeStruct((s, d), F32), jax.ShapeDtypeStruct((s, d), F32)]
    out_specs = [_row_spec(tm, f), _row_spec(tm, f), _row_spec(tm, d), _row_spec(tm, d)]
    if with_ple:
        out_shape += [jax.ShapeDtypeStruct((s, d), F32)] * 2
        out_specs += [_row_spec(tm, d)] * 2
    scratch = [pltpu.VMEM(w.shape, w.dtype) for w in weights] + [pltpu.SemaphoreType.DMA((len(weights),))]
    return pl.pallas_call(
        body, name="ffn_fwd_ple" if with_ple else "ffn_fwd", grid=(s // tm,), in_specs=in_specs, out_specs=out_specs,
        out_shape=out_shape, scratch_shapes=scratch, compiler_params=_params("arbitrary"),
    )(*args)


def ffn_bwd(dxn, r, x, hg, hu, ln_g, w_gate, w_up, w_down, ple=None):
    s = x.shape[0]
    with_ple = ple is not None
    d, f = D_MODEL, D_FF
    suffix = "_ple" if with_ple else ""

    tm = ROW_TILE

    def body_a(*refs):
        it = iter(refs)
        dxn_ref, r_ref, hg_ref, hu_ref, g_ref = (next(it) for _ in range(5))
        pg_ref, pp_ref = (next(it), next(it)) if with_ple else (None, None)
        wd_hbm = next(it)
        dr_ref, act_ref, dhg_ref, dhu_ref, dy_ref, dg_ref, db_ref = (next(it) for _ in range(7))
        dpg_ref, dpp_ref = (next(it), next(it)) if with_ple else (None, None)
        wd_vm, sems = next(it), next(it)
        step = pl.program_id(0)
        _load_resident(step, [(wd_hbm, wd_vm)], sems)

        @pl.when(step == 0)
        def _():
            dg_ref[...] = jnp.zeros_like(dg_ref)
            db_ref[...] = jnp.zeros_like(db_ref)

        dxn_v = dxn_ref[...]
        xhat, rstd = _layer_norm_stats(r_ref[...])
        dg_ref[...] += jnp.sum(dxn_v * xhat, axis=0, keepdims=True)
        db_ref[...] += jnp.sum(dxn_v, axis=0, keepdims=True)
        dyh = dxn_v * g_ref[...]
        dr = rstd * (dyh - jnp.mean(dyh, axis=-1, keepdims=True) - xhat * jnp.mean(dyh * xhat, axis=-1, keepdims=True))
        dr_ref[...] = dr
        dy = (0.5 * dr).astype(BF16)
        dy_ref[...] = dy
        da = _mm_nt(dy, wd_vm[...])
        hg_v = hg_ref[...]
        hu_v = hu_ref[...]
        sil, dsil = _silu_and_grad(hg_v)
        act_ref[...] = (sil * hu_v).astype(BF16)
        dhu_ref[...] = (da * sil).astype(BF16)
        dhg_ref[...] = (da * hu_v * dsil).astype(BF16)
        if with_ple:
            sp = _sigmoid(pg_ref[...])
            dpp_ref[...] = (dr * sp).astype(BF16)
            dpg_ref[...] = (dr * pp_ref[...] * sp * (1.0 - sp)).astype(BF16)

    in_specs = [_row_spec(tm, d), _row_spec(tm, d), _row_spec(tm, f), _row_spec(tm, f), _full_spec((1, d))]
    args = [dxn, r, hg, hu, ln_g]
    if with_ple:
        in_specs += [_row_spec(tm, d), _row_spec(tm, d)]
        args += [ple[0], ple[1]]
    out_shape = [jax.ShapeDtypeStruct((s, d), F32), jax.ShapeDtypeStruct((s, f), BF16), jax.ShapeDtypeStruct((s, f), BF16),
                 jax.ShapeDtypeStruct((s, f), BF16), jax.ShapeDtypeStruct((s, d), BF16),
                 jax.ShapeDtypeStruct((1, d), F32), jax.ShapeDtypeStruct((1, d), F32)]
    out_specs = [_row_spec(tm, d), _row_spec(tm, f), _row_spec(tm, f), _row_spec(tm, f), _row_spec(tm, d),
                 _full_spec((1, d)), _full_spec((1, d))]
    if with_ple:
        out_shape += [jax.ShapeDtypeStruct((s, d), BF16)] * 2
        out_specs += [_row_spec(tm, d)] * 2
    first = pl.pallas_call(
        body_a, name="ffn_bwd_hidden" + suffix, grid=(s // tm,), in_specs=in_specs + [_ANY], out_specs=out_specs,
        out_shape=out_shape, scratch_shapes=[pltpu.VMEM(w_down.shape, w_down.dtype), pltpu.SemaphoreType.DMA((1,))],
        compiler_params=_params("arbitrary"),
    )(*args, w_down)
    dr, act, dhg, dhu, dy, dg, db = first[:7]

    tb = min(ROW_TILE_BWD, s)
    weights = [w_gate, w_up] + ([ple[2]] if with_ple else [])

    def body_b(*refs):
        it = iter(refs)
        dr_ref, dhg_ref, dhu_ref = next(it), next(it), next(it)
        dpg_ref = next(it) if with_ple else None
        w_hbm = [next(it) for _ in weights]
        dx_ref = next(it)
        w_vm = [next(it) for _ in weights]
        sems = next(it)
        _load_resident(pl.program_id(0), list(zip(w_hbm, w_vm)), sems)
        dx = ALPHA * dr_ref[...] + _mm_nt(dhg_ref[...], w_vm[0][...]) + _mm_nt(dhu_ref[...], w_vm[1][...])
        if with_ple:
            dx = dx + _mm_nt(dpg_ref[...], w_vm[2][...])
        dx_ref[...] = dx

    in_specs = [_row_spec(tb, d), _row_spec(tb, f), _row_spec(tb, f)] + ([_row_spec(tb, d)] if with_ple else [])
    args = [dr, dhg, dhu] + ([first[7]] if with_ple else [])
    dx = pl.pallas_call(
        body_b, name="ffn_bwd_input" + suffix, grid=(s // tb,), in_specs=in_specs + [_ANY] * len(weights),
        out_specs=_row_spec(tb, d), out_shape=jax.ShapeDtypeStruct((s, d), F32),
        scratch_shapes=[pltpu.VMEM(w.shape, w.dtype) for w in weights] + [pltpu.SemaphoreType.DMA((len(weights),))],
        compiler_params=_params("arbitrary"),
    )(*args, *weights)
    return (dx, act, dhg, dhu, dy, dg, db) + tuple(first[7:])


def win_fwd(x1, w_r, w_l, w_g):
    s = x1.shape[0]
    tm = min(ROW_TILE_BWD, s)
    weights = [w_r, w_l, w_g]

    def body(x_ref, wr_h, wl_h, wg_h, hr_ref, hl_ref, hgd_ref, wr_v, wl_v, wg_v, sems):
        _load_resident(pl.program_id(0), [(wr_h, wr_v), (wl_h, wl_v), (wg_h, wg_v)], sems)
        xb = x_ref[...].astype(BF16)
        hr_ref[...] = _mm(xb, wr_v[...])
        hl_ref[...] = _mm(xb, wl_v[...])
        hgd_ref[...] = _mm(xb, wg_v[...])

    return pl.pallas_call(
        body, name="win_fwd", grid=(s // tm,),
        in_specs=[_row_spec(tm, D_MODEL), _ANY, _ANY, _ANY],
        out_specs=[_row_spec(tm, RET_IN), _row_spec(tm, LRU_IN), _row_spec(tm, GDN_IN)],
        out_shape=[jax.ShapeDtypeStruct((s, RET_IN), F32), jax.ShapeDtypeStruct((s, LRU_IN), F32),
                   jax.ShapeDtypeStruct((s, GDN_IN), F32)],
        scratch_shapes=[pltpu.VMEM(w.shape, w.dtype) for w in weights] + [pltpu.SemaphoreType.DMA((3,))],
        compiler_params=_params("arbitrary"),
    )(x1, *weights)


def win_bwd(dr2, dhr, dhl, dhq, dab, w_r, w_l, w_g, w_ab):
    s = dr2.shape[0]
    tm = min(ROW_TILE_BWD, s)
    weights = [w_r, w_l, w_g, w_ab]
    nq = 4 * GDN_W

    def body(dr_ref, dhr_ref, dhl_ref, dhq_ref, dab_ref, wr_h, wl_h, wg_h, wab_h, dx_ref, wr_v, wl_v, wg_v, wab_v, sems):
        _load_resident(pl.program_id(0), [(wr_h, wr_v), (wl_h, wl_v), (wg_h, wg_v), (wab_h, wab_v)], sems)
        dx_ref[...] = (ALPHA * dr_ref[...] + _mm_nt(dhr_ref[...], wr_v[...]) + _mm_nt(dhl_ref[...], wl_v[...])
                       + _mm_nt(dhq_ref[...], wg_v[:, 0:nq]) + _mm_nt(dab_ref[...], wab_v[...]))

    return pl.pallas_call(
        body, name="win_bwd", grid=(s // tm,),
        in_specs=[_row_spec(tm, D_MODEL), _row_spec(tm, RET_IN), _row_spec(tm, LRU_IN), _row_spec(tm, nq), _row_spec(tm, LANES),
                  _ANY, _ANY, _ANY, _ANY],
        out_specs=_row_spec(tm, D_MODEL),
        out_shape=jax.ShapeDtypeStruct((s, D_MODEL), F32),
        scratch_shapes=[pltpu.VMEM(w.shape, w.dtype) for w in weights] + [pltpu.SemaphoreType.DMA((4,))],
        compiler_params=_params("arbitrary"),
    )(dr2, dhr, dhl, dhq, dab, *weights)


def out_fwd(o_r, o_l, o_g, x1, w_out, ln_g, ln_b):
    s = x1.shape[0]
    tm = ROW_TILE

    def body(or_ref, ol_ref, og_ref, x_ref, g_ref, b_ref, w_h, r_ref, xn_ref, ocat_ref, w_v, sems):
        _load_resident(pl.program_id(0), [(w_h, w_v)], sems)
        ocat = jnp.concatenate([or_ref[...], ol_ref[...], og_ref[...]], axis=1).astype(BF16)
        ocat_ref[...] = ocat
        r = ALPHA * x_ref[...] + _mm(ocat, w_v[...])
        r_ref[...] = r
        xhat, _ = _layer_norm_stats(r)
        xn_ref[...] = xhat * g_ref[...] + b_ref[...]

    d = D_MODEL
    return pl.pallas_call(
        body, name="out_fwd", grid=(s // tm,),
        in_specs=[_row_spec(tm, RET_W), _row_spec(tm, LRU_W), _row_spec(tm, GDN_W), _row_spec(tm, d),
                  _full_spec((1, d)), _full_spec((1, d)), _ANY],
        out_specs=[_row_spec(tm, d), _row_spec(tm, d), _row_spec(tm, d)],
        out_shape=[jax.ShapeDtypeStruct((s, d), F32)] * 2 + [jax.ShapeDtypeStruct((s, d), BF16)],
        scratch_shapes=[pltpu.VMEM(w_out.shape, w_out.dtype), pltpu.SemaphoreType.DMA((1,))],
        compiler_params=_params("arbitrary"),
    )(o_r, o_l, o_g, x1, ln_g, ln_b, w_out)


def out_bwd(dxn, r2, ln_g, w_out):
    s = dxn.shape[0]
    tm = ROW_TILE

    def body(dxn_ref, r_ref, g_ref, w_h, dr_ref, drb_ref, dor_ref, dol_ref, dog_ref, dg_ref, db_ref, w_v, sems):
        step = pl.program_id(0)
        _load_resident(step, [(w_h, w_v)], sems)

        @pl.when(step == 0)
        def _():
            dg_ref[...] = jnp.zeros_like(dg_ref)
            db_ref[...] = jnp.zeros_like(db_ref)

        dxn_v = dxn_ref[...]
        xhat, rstd = _layer_norm_stats(r_ref[...])
        dg_ref[...] += jnp.sum(dxn_v * xhat, axis=0, keepdims=True)
        db_ref[...] += jnp.sum(dxn_v, axis=0, keepdims=True)
        dyh = dxn_v * g_ref[...]
        dr = rstd * (dyh - jnp.mean(dyh, axis=-1, keepdims=True) - xhat * jnp.mean(dyh * xhat, axis=-1, keepdims=True))
        dr_ref[...] = dr
        drb = dr.astype(BF16)
        drb_ref[...] = drb
        dor_ref[...] = _mm_nt(drb, w_v[0:RET_W, :])
        dol_ref[...] = _mm_nt(drb, w_v[RET_W:RET_W + LRU_W, :])
        dog_ref[...] = _mm_nt(drb, w_v[RET_W + LRU_W:, :])

    d = D_MODEL
    return pl.pallas_call(
        body, name="out_bwd", grid=(s // tm,),
        in_specs=[_row_spec(tm, d), _row_spec(tm, d), _full_spec((1, d)), _ANY],
        out_specs=[_row_spec(tm, d), _row_spec(tm, d), _row_spec(tm, RET_W), _row_spec(tm, LRU_W), _row_spec(tm, GDN_W),
                   _full_spec((1, d)), _full_spec((1, d))],
        out_shape=[jax.ShapeDtypeStruct((s, d), F32), jax.ShapeDtypeStruct((s, d), BF16),
                   jax.ShapeDtypeStruct((s, RET_W), F32), jax.ShapeDtypeStruct((s, LRU_W), F32),
                   jax.ShapeDtypeStruct((s, GDN_W), F32), jax.ShapeDtypeStruct((1, d), F32), jax.ShapeDtypeStruct((1, d), F32)],
        scratch_shapes=[pltpu.VMEM(w_out.shape, w_out.dtype), pltpu.SemaphoreType.DMA((1,))],
        compiler_params=_params("arbitrary"),
    )(dxn, r2, ln_g, w_out)


def wgrad(a, b, name, out_dtype=BF16):
    s, m = a.shape
    n = b.shape[1]
    tk = 1024 if s % 1024 == 0 else s
    tm = next((c for c in (512, 384, 256) if m % c == 0), m)
    tn = next((c for c in (1408, 1152, 1024, 768, 512) if n % c == 0), n)
    nk = s // tk

    def body(a_ref, b_ref, o_ref, acc_ref):
        k = pl.program_id(2)

        @pl.when(k == 0)
        def _():
            acc_ref[...] = jnp.zeros_like(acc_ref)

        acc_ref[...] += _mm_tn(a_ref[...].astype(BF16), b_ref[...].astype(BF16))

        @pl.when(k == nk - 1)
        def _():
            o_ref[...] = acc_ref[...].astype(o_ref.dtype)

    return pl.pallas_call(
        body, name=name, grid=(m // tm, n // tn, nk),
        in_specs=[pl.BlockSpec((tk, tm), lambda i, j, k: (k, i)), pl.BlockSpec((tk, tn), lambda i, j, k: (k, j))],
        out_specs=pl.BlockSpec((tm, tn), lambda i, j, k: (i, j)),
        out_shape=jax.ShapeDtypeStruct((m, n), out_dtype),
        scratch_shapes=[pltpu.VMEM((tm, tn), F32)],
        compiler_params=_params("arbitrary", "arbitrary", "arbitrary"),
    )(a, b)


def loss_and_grad(y, target):
    s, d = y.shape
    tm = ROW_TILE

    def body(y_ref, t_ref, dy_ref, l_ref):
        @pl.when(pl.program_id(0) == 0)
        def _():
            l_ref[...] = jnp.zeros_like(l_ref)

        err = y_ref[...] - t_ref[...]
        dy_ref[...] = err / d
        l_ref[...] += 0.5 * jnp.sum(jnp.mean(err * err, axis=-1, keepdims=True), axis=0, keepdims=True)

    return pl.pallas_call(
        body, name="loss_and_grad", grid=(s // tm,),
        in_specs=[_row_spec(tm, d), _row_spec(tm, d)],
        out_specs=[_row_spec(tm, d), _full_spec((1, 1))],
        out_shape=[jax.ShapeDtypeStruct((s, d), F32), jax.ShapeDtypeStruct((1, 1), F32)],
        compiler_params=_params("arbitrary"),
    )(y, target)


def _ret_consts():
    lg = np.log1p(-np.exp2(-5.0 - np.arange(RET_HEADS, dtype=np.float64)))
    idx = np.arange(CHUNK, dtype=np.float64)
    intra = np.exp(np.abs(idx[:, None] - idx[None, :])[None] * lg[:, None, None])
    cross = np.repeat(np.exp((idx + 1.0)[:, None] * lg[None, :]), HEAD, axis=1)
    tail = np.repeat(np.exp((CHUNK - 1.0 - idx)[:, None] * lg[None, :]), HEAD, axis=1)
    dec = np.repeat(np.exp(CHUNK * lg)[None, :], HEAD, axis=1)
    half = HEAD // 2
    inv_freq = (ROPE_THETA ** (-jnp.arange(half, dtype=F32) / half))
    invf = jnp.tile(inv_freq, 2 * LANES // HEAD)[None, :]
    sgn = np.tile(np.concatenate([-np.ones(half), np.ones(half)]), LANES // HEAD)[None, :]
    f = lambda a: jnp.asarray(a, F32)
    return dict(intra=f(intra), cross=f(cross), tail=f(tail), dec=f(dec), invf=invf, sgn=f(sgn))


def _swap_halves(t):
    lane = lax.broadcasted_iota(jnp.int32, t.shape, 1)
    return jnp.where((lane & 32) == 0, pltpu.roll(t, LANES - 32, 1), pltpu.roll(t, 32, 1))


def _rope(t, c, s):
    return t * c + _swap_halves(t) * s


def _rope_transposed(g, c, s):
    return g * c + _swap_halves(g * s)


def _head_mask(hd):
    lane = lax.broadcasted_iota(jnp.int32, (1, LANES), 1)
    return ((lane >= HEAD * hd) & (lane < HEAD * (hd + 1))).astype(F32)


def _block_diag_mask():
    r = lax.broadcasted_iota(jnp.int32, (LANES, LANES), 0)
    c = lax.broadcasted_iota(jnp.int32, (LANES, LANES), 1)
    return ((r >= HEAD) == (c >= HEAD)).astype(F32)


def _ret_specs(n_of):
    cst = lambda shape: pl.BlockSpec(shape, lambda i: (0,) * len(shape))
    return [pl.BlockSpec((CHUNK, RET_IN), lambda i: (n_of(i), 0)), pl.BlockSpec((CHUNK, 1), lambda i: (n_of(i), 0)),
            cst((1, LANES)), cst((1, LANES)), cst((RET_HEADS, CHUNK, CHUNK)), cst((CHUNK, RET_W)), cst((CHUNK, RET_W)),
            cst((1, RET_W)), cst((1, RET_W))]


def ret_fwd(hr, pos, norm_g):
    s = hr.shape[0]
    n_chunks = s // CHUNK
    cs = _ret_consts()
    n_slab = RET_W // LANES

    def body(hr_ref, pos_ref, invf_ref, sgn_ref, intra_ref, cross_ref, tail_ref, dec_ref, g_ref, o_ref, opre_ref, st_ref, state):
        @pl.when(pl.program_id(0) == 0)
        def _():
            state[...] = jnp.zeros_like(state)

        ang = pos_ref[...].astype(F32) * invf_ref[...]
        cosv = jnp.cos(ang)
        sinv = jnp.sin(ang) * sgn_ref[...]
        bd = _block_diag_mask()
        for sl in range(n_slab):
            lanes = slice(LANES * sl, LANES * (sl + 1))
            rows = slice(LANES * sl, LANES * (sl + 1))
            q = hr_ref[:, LANES * sl:LANES * (sl + 1)]
            k = hr_ref[:, RET_W + LANES * sl:RET_W + LANES * (sl + 1)]
            v = hr_ref[:, 2 * RET_W + LANES * sl:2 * RET_W + LANES * (sl + 1)]
            gate = hr_ref[:, 3 * RET_W + LANES * sl:3 * RET_W + LANES * (sl + 1)]
            qt = _rope(q, cosv, sinv) * (HEAD ** -0.5)
            kt = _rope(k, cosv, sinv)
            st = state[rows, :]
            st_ref[rows, :] = st
            o = _mm(qt * cross_ref[:, lanes], st)
            for hd in range(2):
                m = _head_mask(hd)
                sc = _mm_nt(qt * m, kt) * intra_ref[2 * sl + hd]
                o = o + _mm(sc, v) * m
            state[rows, :] = st * dec_ref[:, lanes] + _mm_tn(kt, v * tail_ref[:, lanes]) * bd
            opre_ref[:, lanes] = o
            mu = _group_sum_slab(o) * (1.0 / HEAD)
            dlt = o - mu
            var = _group_sum_slab(dlt * dlt) * (1.0 / HEAD)
            on = dlt * lax.rsqrt(var + 1e-5)
            o_ref[:, lanes] = on * g_ref[:, lanes] * (gate * _sigmoid(gate))

    out_row = lambda w: pl.BlockSpec((CHUNK, w), lambda i: (i, 0))
    return pl.pallas_call(
        body, name="ret_fwd", grid=(n_chunks,),
        in_specs=_ret_specs(lambda i: i),
        out_specs=[out_row(RET_W), out_row(RET_W), pl.BlockSpec((RET_W, LANES), lambda i: (i, 0))],
        out_shape=[jax.ShapeDtypeStruct((s, RET_W), F32), jax.ShapeDtypeStruct((s, RET_W), F32),
                   jax.ShapeDtypeStruct((n_chunks * RET_W, LANES), F32)],
        scratch_shapes=[pltpu.VMEM((RET_W, LANES), F32)],
        compiler_params=_params("arbitrary"),
    )(hr, pos, cs["invf"], cs["sgn"], cs["intra"], cs["cross"], cs["tail"], cs["dec"], norm_g)


def ret_bwd(hr, pos, norm_g, opre, states, dout):
    s = hr.shape[0]
    n_chunks = s // CHUNK
    cs = _ret_consts()
    n_slab = RET_W // LANES
    rev = lambda i: n_chunks - 1 - i

    def body(hr_ref, pos_ref, invf_ref, sgn_ref, intra_ref, cross_ref, tail_ref, dec_ref, g_ref, opre_ref, st_ref, do_ref,
             dh_ref, dg_ref, gstate):
        @pl.when(pl.program_id(0) == 0)
        def _():
            gstate[...] = jnp.zeros_like(gstate)
            dg_ref[...] = jnp.zeros_like(dg_ref)

        ang = pos_ref[...].astype(F32) * invf_ref[...]
        cosv = jnp.cos(ang)
        sinv = jnp.sin(ang) * sgn_ref[...]
        bd = _block_diag_mask()
        for sl in range(n_slab):
            lanes = slice(LANES * sl, LANES * (sl + 1))
            rows = slice(LANES * sl, LANES * (sl + 1))
            q = hr_ref[:, LANES * sl:LANES * (sl + 1)]
            k = hr_ref[:, RET_W + LANES * sl:RET_W + LANES * (sl + 1)]
            v = hr_ref[:, 2 * RET_W + LANES * sl:2 * RET_W + LANES * (sl + 1)]
            gate = hr_ref[:, 3 * RET_W + LANES * sl:3 * RET_W + LANES * (sl + 1)]
            qt = _rope(q, cosv, sinv) * (HEAD ** -0.5)
            kt = _rope(k, cosv, sinv)
            o = opre_ref[:, lanes]
            mu = _group_sum_slab(o) * (1.0 / HEAD)
            dlt = o - mu
            var = _group_sum_slab(dlt * dlt) * (1.0 / HEAD)
            rstd = lax.rsqrt(var + 1e-5)
            on = dlt * rstd
            sil, dsil = _silu_and_grad(gate)
            dout_v = do_ref[:, lanes]
            gn = g_ref[:, lanes]
            dg_ref[:, lanes] += jnp.sum(dout_v * on * sil, axis=0, keepdims=True)
            d_on = dout_v * gn * sil
            dgate = dout_v * on * gn * dsil
            d_o = rstd * (d_on - _group_sum_slab(d_on) * (1.0 / HEAD) - on * (_group_sum_slab(d_on * on) * (1.0 / HEAD)))
            st = st_ref[rows, :]
            gs = gstate[rows, :]
            cross = cross_ref[:, lanes]
            tail = tail_ref[:, lanes]
            dqt = _mm_nt(d_o, st) * cross
            ds_here = _mm_tn(qt * cross, d_o) * bd
            vt = v * tail
            dkt = _mm_nt(vt, gs)
            dv = _mm(kt, gs) * tail
            for hd in range(2):
                m = _head_mask(hd)
                qm = qt * m
                dom = d_o * m
                intra = intra_ref[2 * sl + hd]
                sc = _mm_nt(qm, kt) * intra
                dsc = _mm_nt(dom, v) * intra
                dqt = dqt + _mm(dsc, kt) * m
                dkt = dkt + _mm_tn(dsc, qm)
                dv = dv + _mm_tn(sc, dom)
            gstate[rows, :] = gs * dec_ref[:, lanes] + ds_here
            dh_ref[:, LANES * sl:LANES * (sl + 1)] = _rope_transposed(dqt * (HEAD ** -0.5), cosv, sinv).astype(BF16)
            dh_ref[:, RET_W + LANES * sl:RET_W + LANES * (sl + 1)] = _rope_transposed(dkt, cosv, sinv).astype(BF16)
            dh_ref[:, 2 * RET_W + LANES * sl:2 * RET_W + LANES * (sl + 1)] = dv.astype(BF16)
            dh_ref[:, 3 * RET_W + LANES * sl:3 * RET_W + LANES * (sl + 1)] = dgate.astype(BF16)

    row = lambda w: pl.BlockSpec((CHUNK, w), lambda i: (rev(i), 0))
    return pl.pallas_call(
        body, name="ret_bwd", grid=(n_chunks,),
        in_specs=_ret_specs(rev) + [row(RET_W), pl.BlockSpec((RET_W, LANES), lambda i: (rev(i), 0)), row(RET_W)],
        out_specs=[row(RET_IN), pl.BlockSpec((1, RET_W), lambda i: (0, 0))],
        out_shape=[jax.ShapeDtypeStruct((s, RET_IN), BF16), jax.ShapeDtypeStruct((1, RET_W), F32)],
        scratch_shapes=[pltpu.VMEM((RET_W, LANES), F32)],
        compiler_params=_params("arbitrary"),
    )(hr, pos, cs["invf"], cs["sgn"], cs["intra"], cs["cross"], cs["tail"], cs["dec"], norm_g, opre, states, dout)


def _lru_gates(xc, wa_ref, ba_ref, wx_ref, bx_ref, lam_ref):
    xcb = xc.astype(BF16)
    r = _sigmoid(_mm(xcb, wa_ref[...].astype(BF16)) + ba_ref[...])
    ig = _sigmoid(_mm(xcb, wx_ref[...].astype(BF16)) + bx_ref[...])
    lam = lam_ref[...]
    ls = jnp.minimum(lam, 0.0) - _log1p(jnp.exp(-jnp.abs(lam)))
    la = (LRU_C * r) * ls
    a = jnp.exp(la)
    mult = jnp.sqrt(-_expm1(2.0 * la))
    return r, ig, ls, a, mult


def _lru_conv(x, xprev, w_ref, b_ref):
    xc = b_ref[...] + w_ref[3:4, :] * x
    for j in (1, 2, 3):
        xc = xc + w_ref[3 - j:4 - j, :] * _shift_rows(x, xprev, j)
    return xc


def lru_fwd(hl, conv_w, conv_b, w_a, b_a, w_x, b_x, lam):
    s = hl.shape[0]
    ts = SCAN_TILE
    w = LRU_W

    def body(hl_ref, hp_ref, cw_ref, cb_ref, wa_ref, ba_ref, wx_ref, bx_ref, lam_ref, o_ref, xc_ref, h_ref, carry):
        i = pl.program_id(0)

        @pl.when(i == 0)
        def _():
            carry[...] = jnp.zeros_like(carry)

        x = hl_ref[:, 0:w]
        gate = hl_ref[:, w:2 * w]
        xprev = hp_ref[...] * (i > 0).astype(F32)
        xc = _lru_conv(x, xprev, cw_ref, cb_ref)
        xc_ref[...] = xc
        _, ig, _, a, mult = _lru_gates(xc, wa_ref, ba_ref, wx_ref, bx_ref, lam_ref)
        b = mult * (ig * xc)
        row = lax.broadcasted_iota(jnp.int32, (ts, w), 0)
        d = 1
        while d < ts:
            ap = jnp.where(row >= d, pltpu.roll(a, d, 0), 1.0)
            bp = jnp.where(row >= d, pltpu.roll(b, d, 0), 0.0)
            b = a * bp + b
            a = a * ap
            d *= 2
        h = b + a * carry[0:1, :]
        h_ref[...] = h
        carry[0:1, :] = h[ts - 1:ts, :]
        o_ref[...] = h * _gelu(gate)

    cst = lambda shape: pl.BlockSpec(shape, lambda i: (0, 0))
    return pl.pallas_call(
        body, name="lru_fwd", grid=(s // ts,),
        in_specs=[_row_spec(ts, 2 * w), pl.BlockSpec((ts, w), lambda i: (jnp.maximum(i - 1, 0), 0)),
                  cst((4, w)), cst((1, w)), cst((w, w)), cst((1, w)), cst((w, w)), cst((1, w)), cst((1, w))],
        out_specs=[_row_spec(ts, w)] * 3,
        out_shape=[jax.ShapeDtypeStruct((s, w), F32)] * 3,
        scratch_shapes=[pltpu.VMEM((8, w), F32)],
        compiler_params=_params("arbitrary"),
    )(hl, hl, conv_w, conv_b, w_a, b_a, w_x, b_x, lam)


def lru_bwd(hl, conv_w, conv_b, w_a, b_a, w_x, b_x, lam, xc_saved, h_saved, dout):
    s = hl.shape[0]
    ts = SCAN_TILE
    w = LRU_W
    nb = s // ts
    rev = lambda i: nb - 1 - i

    def body(hl_ref, hp_ref, cw_ref, cb_ref, wa_ref, ba_ref, wx_ref, bx_ref, lam_ref, xc_ref, h_ref, hprev_ref, do_ref,
             dhl_ref, dcw_ref, dcb_ref, dwa_ref, dba_ref, dwx_ref, dbx_ref, dlam_ref, carry, dxc_next):
        i = pl.program_id(0)
        blk = nb - 1 - i

        @pl.when(i == 0)
        def _():
            carry[...] = jnp.zeros_like(carry)
            dxc_next[...] = jnp.zeros_like(dxc_next)
            for ref in (dcw_ref, dcb_ref, dwa_ref, dba_ref, dwx_ref, dbx_ref, dlam_ref):
                ref[...] = jnp.zeros_like(ref)

        first = (blk > 0).astype(F32)
        x = hl_ref[:, 0:w]
        gate = hl_ref[:, w:2 * w]
        xprev = hp_ref[...] * first
        xc = xc_ref[...]
        h = h_ref[...]
        hprev = hprev_ref[...] * first
        r, ig, ls, a, mult = _lru_gates(xc, wa_ref, ba_ref, wx_ref, bx_ref, lam_ref)
        do = do_ref[...]
        dh = do * _gelu(gate)
        dgate = do * h * _gelu_grad(gate)
        row = lax.broadcasted_iota(jnp.int32, (ts, w), 0)
        ca = jnp.where(row < ts - 1, pltpu.roll(a, ts - 1, 0), 1.0)
        cb = dh
        d = 1
        while d < ts:
            an = jnp.where(row < ts - d, pltpu.roll(ca, ts - d, 0), 1.0)
            bn = jnp.where(row < ts - d, pltpu.roll(cb, ts - d, 0), 0.0)
            cb = cb + ca * bn
            ca = ca * an
            d *= 2
        lamb = cb + ca * carry[0:1, :]
        carry[0:1, :] = a[0:1, :] * lamb[0:1, :]
        h_before = _shift_rows(h, hprev, 1)
        da = lamb * h_before
        ix = ig * xc
        dmult = lamb * ix
        dig = lamb * mult * xc
        dxc = lamb * mult * ig
        dla = (da - dmult * a / mult) * a
        dr = dla * LRU_C * ls
        dlam_ref[...] += jnp.sum(dla * LRU_C * r, axis=0, keepdims=True) * _sigmoid(-lam_ref[...])
        dpa = dr * r * (1.0 - r)
        dpx = dig * ig * (1.0 - ig)
        dba_ref[...] += jnp.sum(dpa, axis=0, keepdims=True)
        dbx_ref[...] += jnp.sum(dpx, axis=0, keepdims=True)
        dpab = dpa.astype(BF16)
        dpxb = dpx.astype(BF16)
        xcb = xc.astype(BF16)
        dxc = dxc + _mm_nt(dpab, wa_ref[...].astype(BF16)) + _mm_nt(dpxb, wx_ref[...].astype(BF16))
        dwa_ref[...] += _mm_tn(xcb, dpab)
        dwx_ref[...] += _mm_tn(xcb, dpxb)
        dcb_ref[...] += jnp.sum(dxc, axis=0, keepdims=True)
        nxt = dxc_next[...]
        dx = cw_ref[3:4, :] * dxc
        dcw_ref[3:4, :] += jnp.sum(dxc * x, axis=0, keepdims=True)
        for j in (1, 2, 3):
            dx = dx + cw_ref[3 - j:4 - j, :] * _shift_rows_up(dxc, nxt, j)
            dcw_ref[3 - j:4 - j, :] += jnp.sum(dxc * _shift_rows(x, xprev, j), axis=0, keepdims=True)
        dxc_next[...] = dxc
        dhl_ref[:, 0:w] = dx.astype(BF16)
        dhl_ref[:, w:2 * w] = dgate.astype(BF16)

    cst = lambda shape: pl.BlockSpec(shape, lambda i: (0, 0))
    rowr = lambda width: pl.BlockSpec((ts, width), lambda i: (rev(i), 0))
    prevr = lambda width: pl.BlockSpec((ts, width), lambda i: (jnp.maximum(rev(i) - 1, 0), 0))
    return pl.pallas_call(
        body, name="lru_bwd", grid=(nb,),
        in_specs=[rowr(2 * w), prevr(w), cst((4, w)), cst((1, w)), cst((w, w)), cst((1, w)), cst((w, w)), cst((1, w)), cst((1, w)),
                  rowr(w), rowr(w), prevr(w), rowr(w)],
        out_specs=[rowr(2 * w), cst((4, w)), cst((1, w)), cst((w, w)), cst((1, w)), cst((w, w)), cst((1, w)), cst((1, w))],
        out_shape=[jax.ShapeDtypeStruct((s, 2 * w), BF16), jax.ShapeDtypeStruct((4, w), F32), jax.ShapeDtypeStruct((1, w), F32),
                   jax.ShapeDtypeStruct((w, w), F32), jax.ShapeDtypeStruct((1, w), F32), jax.ShapeDtypeStruct((w, w), F32),
                   jax.ShapeDtypeStruct((1, w), F32), jax.ShapeDtypeStruct((1, w), F32)],
        scratch_shapes=[pltpu.VMEM((8, w), F32), pltpu.VMEM((ts, w), F32)],
        compiler_params=_params("arbitrary"),
    )(hl, hl, conv_w, conv_b, w_a, b_a, w_x, b_x, lam, xc_saved, h_saved, h_saved, dout)


GDN_QKV = 3 * GDN_W


def _tri_inverse(nm):
    r = lax.broadcasted_iota(jnp.int32, nm.shape, 0)
    c = lax.broadcasted_iota(jnp.int32, nm.shape, 1)
    t = (r == c).astype(F32) - nm
    p = nm
    for _ in range(5):
        p = _mm3(p, p)
        t = t + _mm3(t, p)
    return t


def _tri_inverse_many(nms):
    r = lax.broadcasted_iota(jnp.int32, nms[0].shape, 0)
    c = lax.broadcasted_iota(jnp.int32, nms[0].shape, 1)
    eye = (r == c).astype(F32)
    ts = [eye - nm for nm in nms]
    ps = list(nms)
    for _ in range(5):
        ps = [_mm3(p, p) for p in ps]
        ts = [t + _mm3(t, p) for t, p in zip(ts, ps)]
    return ts


def _gdn_front(hx_ref, hprev, cw_ref, al_ref, dt_ref):
    w = GDN_W
    x = hx_ref[:, 0:GDN_QKV]
    y = cw_ref[3:4, :] * x
    for j in (1, 2, 3):
        y = y + cw_ref[3 - j:4 - j, :] * _shift_rows(x, hprev, j)
    qkv, dsil = _silu_and_grad(y)
    q, k, v = qkv[:, 0:w], qkv[:, w:2 * w], qkv[:, 2 * w:3 * w]
    rq = lax.rsqrt(_group_sum(q * q) + 1e-6)
    rk = lax.rsqrt(_group_sum(k * k) + 1e-6)
    beta = _sigmoid(hx_ref[:, 5 * w:6 * w])
    sp_in = hx_ref[:, 4 * w:5 * w] + dt_ref[...]
    neg_a = -jnp.exp(al_ref[...])
    g = neg_a * _softplus(sp_in)
    gc = _rows_prefix_sum(g)
    return dict(x=x, dsil=dsil, qn=q * rq, kn=k * rk, v=v, rq=rq, rk=rk, beta=beta, sp_in=sp_in, neg_a=neg_a, g=g, gc=gc)


def _gdn_head(fr, hd, tri):
    lower, strict = tri
    hs = lambda a: a[:, HEAD * hd:HEAD * (hd + 1)]
    k = hs(fr["kn"])
    q = hs(fr["qn"]) * (HEAD ** -0.5)
    v = hs(fr["v"])
    beta = hs(fr["beta"])
    gc = hs(fr["gc"])
    e = jnp.exp(gc)
    gl = gc[CHUNK - 1:CHUNK, :]
    xt = jnp.exp(gl - gc)
    dec = jnp.where(lower, jnp.exp(jnp.minimum(gc - gc.T, 0.0)), 0.0)
    kk = _mm_nt(k, k)
    qkr = _mm_nt(q, k)
    return dict(k=k, q=q, v=v, beta=beta, e=e, egl=jnp.exp(gl), xt=xt, dec=dec, kk=kk, qkr=qkr,
                nm=jnp.where(strict, beta * kk * dec, 0.0))


def _tri_masks():
    r = lax.broadcasted_iota(jnp.int32, (CHUNK, CHUNK), 0)
    c = lax.broadcasted_iota(jnp.int32, (CHUNK, CHUNK), 1)
    return r >= c, r > c


def gdn_fwd(hx, conv_w, a_log_e, dt_bias_e, norm_g_e):
    s = hx.shape[0]
    n_chunks = s // CHUNK
    w = GDN_W

    def body(hx_ref, hp_ref, cw_ref, al_ref, dt_ref, ng_ref, o_ref, opre_ref, t_ref, st_ref, state):
        n = pl.program_id(0)

        @pl.when(n == 0)
        def _():
            state[...] = jnp.zeros_like(state)

        fr = _gdn_front(hx_ref, hp_ref[...] * (n > 0).astype(F32), cw_ref, al_ref, dt_ref)
        tri = _tri_masks()
        st_all = state[...]
        st_ref[...] = st_all
        heads = [_gdn_head(fr, hd, tri) for hd in range(GDN_HEADS)]
        ts = _tri_inverse_many([hq["nm"] for hq in heads])
        outs, new_states = [], []
        for hd, (hq, t) in enumerate(zip(heads, ts)):
            u = _mm(t, hq["v"] * hq["beta"])
            wk = _mm(t, hq["k"] * (hq["beta"] * hq["e"]))
            st = st_all[:, HEAD * hd:HEAD * (hd + 1)]
            vnew = u - _mm(wk, st)
            outs.append(_mm(hq["q"] * hq["e"], st) + _mm(hq["qkr"] * hq["dec"], vnew))
            new_states.append(st * hq["egl"] + _mm_tn(hq["k"] * hq["xt"], vnew))
        t_ref[...] = jnp.concatenate(ts, axis=1)
        state[...] = jnp.concatenate(new_states, axis=1)
        o = jnp.concatenate(outs, axis=1)
        opre_ref[...] = o
        rinv = lax.rsqrt(_group_sum(o * o) * (1.0 / HEAD) + 1e-6)
        z = hx_ref[:, 3 * w:4 * w]
        o_ref[...] = (o * rinv) * ng_ref[...] * (z * _sigmoid(z))

    cst = lambda shape: pl.BlockSpec(shape, lambda i: (0, 0))
    row = lambda width: pl.BlockSpec((CHUNK, width), lambda i: (i, 0))
    return pl.pallas_call(
        body, name="gdn_fwd", grid=(n_chunks,),
        in_specs=[row(GDN_IN), pl.BlockSpec((CHUNK, GDN_QKV), lambda i: (jnp.maximum(i - 1, 0), 0)),
                  cst((4, GDN_QKV)), cst((1, w)), cst((1, w)), cst((1, w))],
        out_specs=[row(w)] * 4,
        out_shape=[jax.ShapeDtypeStruct((s, w), F32)] * 4,
        scratch_shapes=[pltpu.VMEM((CHUNK, w), F32)],
        compiler_params=_params("arbitrary"),
    )(hx, hx, conv_w, a_log_e, dt_bias_e, norm_g_e)


def gdn_bwd(hx, conv_w, a_log_e, dt_bias_e, norm_g_e, opre, tmat, states, dout):
    s = hx.shape[0]
    n_chunks = s // CHUNK
    w = GDN_W
    rev = lambda i: n_chunks - 1 - i

    def body(hx_ref, hp_ref, cw_ref, al_ref, dt_ref, ng_ref, opre_ref, t_ref, st_ref, do_ref,
             dhx_ref, dab_ref, dcw_ref, dal_ref, ddt_ref, dng_ref, dstate, dy_next):
        i = pl.program_id(0)
        n = n_chunks - 1 - i

        @pl.when(i == 0)
        def _():
            dstate[...] = jnp.zeros_like(dstate)
            dy_next[...] = jnp.zeros_like(dy_next)
            for ref in (dcw_ref, dal_ref, ddt_ref, dng_ref):
                ref[...] = jnp.zeros_like(ref)

        hprev = hp_ref[...] * (n > 0).astype(F32)
        fr = _gdn_front(hx_ref, hprev, cw_ref, al_ref, dt_ref)
        tri = _tri_masks()
        lower, strict = tri
        o = opre_ref[...]
        rinv = lax.rsqrt(_group_sum(o * o) * (1.0 / HEAD) + 1e-6)
        yn = o * rinv
        z = hx_ref[:, 3 * w:4 * w]
        sil, dsil_z = _silu_and_grad(z)
        dout_v = do_ref[...]
        ng = ng_ref[...]
        dng_ref[...] += jnp.sum(dout_v * yn * sil, axis=0, keepdims=True)
        dz = dout_v * yn * ng * dsil_z
        dyn = dout_v * ng * sil
        d_o = rinv * (dyn - yn * (_group_sum(dyn * yn) * (1.0 / HEAD)))
        last_row = (lax.broadcasted_iota(jnp.int32, (CHUNK, HEAD), 0) == CHUNK - 1).astype(F32)
        rowsum = lambda m: jnp.sum(m, axis=1, keepdims=True)
        t_all, st_all, dsn_all = t_ref[...], st_ref[...], dstate[...]
        new_ds, dq_l, dk_l, dv_l, dbeta_l, dgc_l = [], [], [], [], [], []
        for hd in range(GDN_HEADS):
            win = slice(HEAD * hd, HEAD * (hd + 1))
            hq = _gdn_head(fr, hd, tri)
            k, q, v, beta, e, xt, dec, kk, qkr = (hq[n_] for n_ in ("k", "q", "v", "beta", "e", "xt", "dec", "kk", "qkr"))
            t = t_all[:, win]
            st = st_all[:, win]
            dsn = dsn_all[:, win]
            do_h = d_o[:, win]
            u = _mm(t, v * beta)
            wk = _mm(t, k * (beta * e))
            vnew = u - _mm(wk, st)
            qk = qkr * dec
            kt = k * xt
            dqd = _mm_nt(do_h, st)
            ds = _mm_tn(q * e, do_h)
            dqk = _mm_nt(do_h, vnew)
            dvnew = _mm_tn(qk, do_h) + _mm(kt, dsn)
            dkt = _mm_nt(vnew, dsn)
            ds = ds + hq["egl"] * dsn
            dgl = jnp.sum(rowsum(dsn * st), axis=0, keepdims=True) * hq["egl"]
            dwk = -_mm_nt(dvnew, st)
            ds = ds - _mm_tn(wk, dvnew)
            drv = _mm_tn(t, dvnew)
            drk = _mm_tn(t, dwk)
            dnm = jnp.where(strict, -(_mm_nt(drv, u) + _mm_nt(drk, wk)), 0.0)
            dbeta = rowsum(dnm * kk * dec)
            dkk = dnm * beta * dec
            ddec = dnm * beta * kk + dqk * qkr
            mq = dqk * dec
            dq = _mm(mq, k) + dqd * e
            dk = _mm_tn(mq, q) + _mm(dkk, k) + _mm_tn(dkk, k) + drk * (beta * e) + dkt * xt
            dv_l.append(drv * beta)
            rks = rowsum(drk * k)
            dbeta = dbeta + rowsum(drv * v) + rks * e
            de = rks * beta + rowsum(dqd * q)
            dxt = rowsum(dkt * k) * xt
            dgl = dgl + jnp.sum(dxt, axis=0, keepdims=True)
            dd = ddec * dec
            dgc = de * e - dxt + rowsum(dd) - rowsum(dd.T) + last_row * dgl
            new_ds.append(ds)
            dq_l.append(dq * (HEAD ** -0.5))
            dk_l.append(dk)
            dbeta_l.append(dbeta + jnp.zeros((CHUNK, HEAD), F32))
            dgc_l.append(dgc)
        dstate[...] = jnp.concatenate(new_ds, axis=1)
        dg = _rows_suffix_sum(jnp.concatenate(dgc_l, axis=1))
        dal_ref[...] += jnp.sum(dg * fr["g"], axis=0, keepdims=True)
        da = dg * fr["neg_a"] * _sigmoid(fr["sp_in"])
        ddt_ref[...] += jnp.sum(da, axis=0, keepdims=True)
        beta_all = fr["beta"]
        db = jnp.concatenate(dbeta_l, axis=1) * beta_all * (1.0 - beta_all)
        lane = lax.broadcasted_iota(jnp.int32, (CHUNK, LANES), 1)
        dab = jnp.zeros((CHUNK, LANES), F32)
        for hd in range(GDN_HEADS):
            dab = jnp.where(lane == hd, da[:, HEAD * hd:HEAD * hd + 1], dab)
            dab = jnp.where(lane == GDN_HEADS + hd, db[:, HEAD * hd:HEAD * hd + 1], dab)
        dab_ref[...] = dab.astype(BF16)
        dqn = jnp.concatenate(dq_l, axis=1)
        dkn = jnp.concatenate(dk_l, axis=1)
        dq_raw = fr["rq"] * (dqn - fr["qn"] * _group_sum(dqn * fr["qn"]))
        dk_raw = fr["rk"] * (dkn - fr["kn"] * _group_sum(dkn * fr["kn"]))
        dy = jnp.concatenate([dq_raw, dk_raw] + dv_l, axis=1) * fr["dsil"]
        nxt = dy_next[...]
        x = fr["x"]
        dx = cw_ref[3:4, :] * dy
        dcw_ref[3:4, :] += jnp.sum(dy * x, axis=0, keepdims=True)
        for j in (1, 2, 3):
            dx = dx + cw_ref[3 - j:4 - j, :] * _shift_rows_up(dy, nxt, j)
            dcw_ref[3 - j:4 - j, :] += jnp.sum(dy * _shift_rows(x, hprev, j), axis=0, keepdims=True)
        dy_next[...] = dy
        dhx_ref[:, 0:GDN_QKV] = dx.astype(BF16)
        dhx_ref[:, 3 * w:4 * w] = dz.astype(BF16)

    cst = lambda shape: pl.BlockSpec(shape, lambda i: (0, 0))
    row = lambda width: pl.BlockSpec((CHUNK, width), lambda i: (rev(i), 0))
    buf = lambda width: pltpu.VMEM((CHUNK, width), F32)
    return pl.pallas_call(
        body, name="gdn_bwd", grid=(n_chunks,),
        in_specs=[row(GDN_IN), pl.BlockSpec((CHUNK, GDN_QKV), lambda i: (jnp.maximum(rev(i) - 1, 0), 0)),
                  cst((4, GDN_QKV)), cst((1, w)), cst((1, w)), cst((1, w)), row(w), row(w), row(w), row(w)],
        out_specs=[row(4 * w), row(LANES), cst((4, GDN_QKV)), cst((1, w)), cst((1, w)), cst((1, w))],
        out_shape=[jax.ShapeDtypeStruct((s, 4 * w), BF16), jax.ShapeDtypeStruct((s, LANES), BF16),
                   jax.ShapeDtypeStruct((4, GDN_QKV), F32),
                   jax.ShapeDtypeStruct((1, w), F32), jax.ShapeDtypeStruct((1, w), F32), jax.ShapeDtypeStruct((1, w), F32)],
        scratch_shapes=[buf(w), buf(GDN_QKV)],
        compiler_params=_params("arbitrary"),
    )(hx, hx, conv_w, a_log_e, dt_bias_e, norm_g_e, opre, tmat, states, dout)


_MESH = pl.DeviceIdType.MESH


def all_gather8(x, name):
    m, n = x.shape

    def body(x_ref, out_ref, send_sems, recv_sems, local_sem):
        px, py, pc = lax.axis_index("x"), lax.axis_index("y"), lax.axis_index("c")
        me, sibling = (px, py, pc), (px, py, 1 - pc)
        chips = [(1 - px, py), (px, 1 - py), (1 - px, 1 - py)]

        def slot(dx, dy, dc):
            return out_ref.at[4 * dx + 2 * dy + dc]

        def copy(k, block, to, src=None):
            return pltpu.make_async_remote_copy(
                src_ref=slot(*block) if src is None else src, dst_ref=slot(*block),
                send_sem=send_sems.at[k], recv_sem=recv_sems.at[k], device_id=to, device_id_type=_MESH)

        mine = pltpu.make_async_copy(x_ref, slot(*me), local_sem)
        mine.start()
        first = [copy(0, me, sibling, src=x_ref)]
        first += [copy(1 + j, me, (*chip, pc), src=x_ref) for j, chip in enumerate(chips)]
        for cp in first:
            cp.start()
        passed = [copy(4 + j, (*chip, pc), sibling) for j, chip in enumerate(chips)]
        for j, chip in enumerate(chips):
            copy(1 + j, (*chip, pc), me).wait_recv()
            passed[j].start()
        copy(0, sibling, me).wait_recv()
        for j, chip in enumerate(chips):
            copy(4 + j, (*chip, 1 - pc), me).wait_recv()
        for cp in first + passed:
            cp.wait_send()
        mine.wait()

    return pl.pallas_call(
        body, name=name, out_shape=jax.ShapeDtypeStruct((N_DEV, m, n), x.dtype),
        in_specs=[_ANY], out_specs=_ANY,
        scratch_shapes=[pltpu.SemaphoreType.DMA((7,)), pltpu.SemaphoreType.DMA((7,)), pltpu.SemaphoreType.DMA],
    )(x)


def sibling_swap_layers(arrays0, arrays1, name):
    n = len(arrays0)

    def body(*refs):
        a0, a1, outs = refs[0:n], refs[n:2 * n], refs[2 * n:3 * n]
        send_sems, recv_sems = refs[3 * n:]
        px, py, pc = lax.axis_index("x"), lax.axis_index("y"), lax.axis_index("c")

        def run(send):
            cps = [pltpu.make_async_remote_copy(
                src_ref=send[i], dst_ref=outs[i], send_sem=send_sems.at[i], recv_sem=recv_sems.at[i],
                device_id=(px, py, 1 - pc), device_id_type=_MESH) for i in range(n)]
            for cp in cps:
                cp.start()
            for cp in cps:
                cp.wait()

        @pl.when(pc == 0)
        def _():
            run(a1)

        @pl.when(pc == 1)
        def _():
            run(a0)

    return pl.pallas_call(
        body, name=name, out_shape=[jax.ShapeDtypeStruct(v.shape, v.dtype) for v in arrays0],
        in_specs=[_ANY] * (2 * n), out_specs=[_ANY] * n,
        scratch_shapes=[pltpu.SemaphoreType.DMA((n,)), pltpu.SemaphoreType.DMA((n,))],
    )(*arrays0, *arrays1)


def chip_exchange(arrays, pieces, name):
    n = len(pieces)
    offs = [0]
    for r, _ in pieces:
        offs.append(offs[-1] + r)

    def body(*refs):
        srcs = refs[0:n]
        q_ref, send_sems, recv_sems, local_sems = refs[n:]
        px, py, pc = lax.axis_index("x"), lax.axis_index("y"), lax.axis_index("c")
        mine = 2 * px + py
        chips = [(1 - px, py), (px, 1 - py), (1 - px, 1 - py)]
        locals_, sends = [], []
        for i, (r, stride) in enumerate(pieces):
            dst = pl.ds(offs[i], r)
            lc = pltpu.make_async_copy(srcs[i].at[pl.ds(mine * stride, r)], q_ref.at[mine, dst], local_sems.at[i])
            lc.start()
            locals_.append(lc)
            for j, (cx, cy) in enumerate(chips):
                cp = pltpu.make_async_remote_copy(
                    src_ref=srcs[i].at[pl.ds((2 * cx + cy) * stride, r)], dst_ref=q_ref.at[mine, dst],
                    send_sem=send_sems.at[3 * i + j], recv_sem=recv_sems.at[3 * i + j], device_id=(cx, cy, pc), device_id_type=_MESH)
                cp.start()
                sends.append(cp)
        for i, (r, stride) in enumerate(pieces):
            dst = pl.ds(offs[i], r)
            for j, (cx, cy) in enumerate(chips):
                pltpu.make_async_remote_copy(
                    src_ref=srcs[i].at[pl.ds(mine * stride, r)], dst_ref=q_ref.at[2 * cx + cy, dst],
                    send_sem=send_sems.at[3 * i + j], recv_sem=recv_sems.at[3 * i + j], device_id=(cx, cy, pc),
                    device_id_type=_MESH).wait_recv()
        for cp in sends:
            cp.wait_send()
        for lc in locals_:
            lc.wait()

    return pl.pallas_call(
        body, name=name, out_shape=jax.ShapeDtypeStruct((N_CHIPS, offs[-1], arrays[0].shape[1]), arrays[0].dtype),
        in_specs=[_ANY] * n, out_specs=_ANY,
        scratch_shapes=[pltpu.SemaphoreType.DMA((3 * n,)), pltpu.SemaphoreType.DMA((3 * n,)), pltpu.SemaphoreType.DMA((n,))],
    )(*arrays)


def sibling_swap(x, name):
    def body(x_ref, out_ref, send_sem, recv_sem):
        px, py, pc = lax.axis_index("x"), lax.axis_index("y"), lax.axis_index("c")
        cp = pltpu.make_async_remote_copy(
            src_ref=x_ref, dst_ref=out_ref, send_sem=send_sem, recv_sem=recv_sem,
            device_id=(px, py, 1 - pc), device_id_type=_MESH)
        cp.start()
        cp.wait()

    return pl.pallas_call(
        body, name=name, out_shape=jax.ShapeDtypeStruct(x.shape, x.dtype), in_specs=[_ANY], out_specs=_ANY,
        scratch_shapes=[pltpu.SemaphoreType.DMA, pltpu.SemaphoreType.DMA],
    )(x)


ELT_TILE = 128


def _elt_rows(m):
    for t in (512, 256, ELT_TILE, 16, 8):
        if m % t == 0:
            return t
    return m


def pair_add(a0, a1, b, name):
    m, n = b.shape
    tm = _elt_rows(m)

    def body(a0_ref, a1_ref, b_ref, o_ref):
        mine = jnp.where(lax.axis_index("c") == 0, a0_ref[...], a1_ref[...])
        o_ref[...] = (mine.astype(F32) + b_ref[...].astype(F32)).astype(o_ref.dtype)

    return pl.pallas_call(
        body, name=name, grid=(m // tm,), in_specs=[_row_spec(tm, n)] * 3, out_specs=_row_spec(tm, n),
        out_shape=jax.ShapeDtypeStruct((m, n), b.dtype), compiler_params=_params("arbitrary"),
    )(a0, a1, b)


def sum_leading(q, name):
    kk, m, n = q.shape
    tm = _elt_rows(m)

    def body(q_ref, o_ref):
        acc = q_ref[0].astype(F32)
        for i in range(1, kk):
            acc = acc + q_ref[i].astype(F32)
        o_ref[...] = acc

    return pl.pallas_call(
        body, name=name, grid=(m // tm,), in_specs=[pl.BlockSpec((kk, tm, n), lambda i: (0, i, 0))],
        out_specs=_row_spec(tm, n), out_shape=jax.ShapeDtypeStruct((m, n), F32), compiler_params=_params("arbitrary"),
    )(q)


def adamw(w, g, m, v, name):
    rows, cols = w.shape
    tm = _elt_rows(rows)

    def body(w_ref, g_ref, m_ref, v_ref, d_ref, nm_ref, nv_ref):
        gv = g_ref[...]
        nm = ADAM_B1 * m_ref[...] + (1.0 - ADAM_B1) * gv
        nv = ADAM_B2 * v_ref[...] + (1.0 - ADAM_B2) * jnp.square(gv)
        nm_ref[...] = nm
        nv_ref[...] = nv
        m_hat = nm / (1.0 - ADAM_B1 ** ADAM_STEP)
        v_hat = nv / (1.0 - ADAM_B2 ** ADAM_STEP)
        d_ref[...] = -ADAM_LR * (m_hat / (jnp.sqrt(v_hat) + ADAM_EPS) + ADAM_WD * w_ref[...])

    spec = _row_spec(tm, cols)
    return pl.pallas_call(
        body, name=name, grid=(rows // tm,), in_specs=[spec] * 4, out_specs=[spec] * 3,
        out_shape=[jax.ShapeDtypeStruct((rows, cols), F32)] * 3, compiler_params=_params("arbitrary"),
    )(w, g, m, v)


def _block_diag_dense(w):
    g = w.shape[0]
    return jnp.einsum("gij,gh->gihj", w, jnp.eye(g, dtype=w.dtype)).reshape(g * w.shape[1], g * w.shape[2])


def _diag_blocks(m):
    return jnp.stack([m[HEAD * i:HEAD * (i + 1), HEAD * i:HEAD * (i + 1)] for i in range(LRU_BLOCKS)])


def _rep(v):
    return jnp.repeat(v, HEAD, axis=-1)


def _split_w_in(w_in):
    gdn0 = RET_IN + LRU_IN
    gdn1 = gdn0 + 4 * GDN_W
    w_r = w_in[:, 0:RET_IN]
    w_l = w_in[:, RET_IN:gdn0]
    w_g = jnp.concatenate([w_in[:, gdn0:gdn1], _rep(w_in[:, gdn1:gdn1 + GDN_HEADS]), _rep(w_in[:, gdn1 + GDN_HEADS:])], axis=1)
    w_ab = jnp.pad(w_in[:, gdn1:], ((0, 0), (0, LANES - 2 * GDN_HEADS)))
    return w_r, w_l, w_g, w_ab


WIN_SHARD = D_IN // N_CHIPS
WIN_STRIDE = 832
WIN_ROWS = 960
WIN_T_ROWS = WIN_STRIDE * (N_CHIPS - 1) + WIN_ROWS
AB_ROWS = 16

_GRAD_PIECES = (("ffn1_w_gate", 704, 704), ("ffn1_w_up", 704, 704), ("ffn1_w_down", 704, 704), ("w_in", WIN_ROWS, WIN_STRIDE),
                ("w_out", 256, 256), ("ffn2_w_gate", 704, 704), ("ffn2_w_up", 704, 704), ("ffn2_w_down", 704, 704),
                ("ple_w_gate", 256, 256), ("ple_w_proj", 64, 64))
_TRANSPOSED = ("ffn1_w_gate", "ffn1_w_up", "w_in", "ffn2_w_gate", "ffn2_w_up", "ple_w_proj")


def _local_step(x, p, pos, target, wt):
    row = lambda v: v[None, :]
    saved = []
    for i in range(DEPTH):
        w_r, w_l, w_g, w_ab = _split_w_in(wt["w_in"][i])
        lw = dict(
            wg1=wt["ffn1_w_gate"][i], wu1=wt["ffn1_w_up"][i], wd1=wt["ffn1_w_down"][i], w_r=w_r, w_l=w_l, w_g=w_g, w_ab=w_ab,
            w_out=wt["w_out"][i], wg2=wt["ffn2_w_gate"][i], wu2=wt["ffn2_w_up"][i], wd2=wt["ffn2_w_down"][i],
            wpg=wt["ple_w_gate"][i], wpp=wt["ple_w_proj"][i],
            wa=_block_diag_dense(wt["lru_w_a"][i]), wx=_block_diag_dense(wt["lru_w_x"][i]),
            al=row(_rep(wt["gdn_a_log"][i])), dt=row(_rep(wt["gdn_dt_bias"][i])), ng=row(jnp.tile(wt["gdn_norm_g"][i], GDN_HEADS)))
        hg1, hu1, r1, x1 = ffn_fwd(x, row(wt["ln_ffn1_g"][i]), row(wt["ln_ffn1_b"][i]), lw["wg1"], lw["wu1"], lw["wd1"])
        hr, hl, hgd = win_fwd(x1, w_r, w_l, w_g)
        o_r, opre_r, st_r = ret_fwd(hr, pos, row(wt["ret_norm_g"][i]))
        o_l, xc, hs = lru_fwd(hl, wt["lru_conv_w"][i], row(wt["lru_conv_b"][i]), lw["wa"], row(wt["lru_b_a"][i]), lw["wx"],
                              row(wt["lru_b_x"][i]), row(wt["lru_lambda"][i]))
        o_g, opre_g, tmat, st_g = gdn_fwd(hgd, wt["gdn_conv_w"][i], lw["al"], lw["dt"], lw["ng"])
        r2, x2, ocat = out_fwd(o_r, o_l, o_g, x1, lw["w_out"], row(wt["ln_mix_g"][i]), row(wt["ln_mix_b"][i]))
        hg2, hu2, r3, x3, pg, pp = ffn_fwd(x2, row(wt["ln_ffn2_g"][i]), row(wt["ln_ffn2_b"][i]), lw["wg2"], lw["wu2"], lw["wd2"],
                                           ple=(p[i], lw["wpg"], lw["wpp"]))
        saved.append(dict(lw=lw, x0=x, hg1=hg1, hu1=hu1, r1=r1, x1=x1, hr=hr, hl=hl, hgd=hgd, ocat=ocat, opre_r=opre_r, st_r=st_r,
                          xc=xc, hs=hs, opre_g=opre_g, tmat=tmat, st_g=st_g, r2=r2, x2=x2, hg2=hg2, hu2=hu2,
                          r3=r3, pg=pg, pp=pp))
        x = x3

    dx, loss = loss_and_grad(x, target)
    grads = [None] * DEPTH
    big = [None] * DEPTH
    for i in reversed(range(DEPTH)):
        sv = saved[i]
        lw = sv["lw"]
        tag = f"_l{i}"
        dx2, act2, dhg2, dhu2, dy2, dg3, db3, dpg, dpp = ffn_bwd(
            dx, sv["r3"], sv["x2"], sv["hg2"], sv["hu2"], row(wt["ln_ffn2_g"][i]), lw["wg2"], lw["wu2"], lw["wd2"],
            ple=(sv["pg"], sv["pp"], lw["wpg"]))
        g, bg = {}, {}
        bg["ffn2_w_gate"] = wgrad(dhg2, sv["x2"], "wgrad_gate2" + tag)
        bg["ffn2_w_up"] = wgrad(dhu2, sv["x2"], "wgrad_up2" + tag)
        bg["ffn2_w_down"] = wgrad(act2, dy2, "wgrad_down2" + tag)
        bg["ple_w_gate"] = wgrad(sv["x2"], dpg, "wgrad_pgate" + tag)
        bg["ple_w_proj"] = wgrad(dpp, p[i], "wgrad_pproj" + tag).reshape(PLE_DIM, D_MODEL)
        g["ln_ffn2_g"], g["ln_ffn2_b"] = dg3[0], db3[0]
        dr2, dr2b, do_r, do_l, do_g, dg2, db2 = out_bwd(dx2, sv["r2"], row(wt["ln_mix_g"][i]), lw["w_out"])
        g["ln_mix_g"], g["ln_mix_b"] = dg2[0], db2[0]
        bg["w_out"] = wgrad(sv["ocat"], dr2b, "wgrad_out" + tag)
        dhr, dgn = ret_bwd(sv["hr"], pos, row(wt["ret_norm_g"][i]), sv["opre_r"], sv["st_r"], do_r)
        g["ret_norm_g"] = dgn[0]
        dhl, dcw, dcb, dwa, dba, dwx, dbx, dlam = lru_bwd(
            sv["hl"], wt["lru_conv_w"][i], row(wt["lru_conv_b"][i]), lw["wa"], row(wt["lru_b_a"][i]), lw["wx"],
            row(wt["lru_b_x"][i]), row(wt["lru_lambda"][i]), sv["xc"], sv["hs"], do_l)
        g["lru_conv_w"], g["lru_conv_b"] = dcw, dcb[0]
        g["lru_w_a"], g["lru_b_a"], g["lru_w_x"], g["lru_b_x"], g["lru_lambda"] = _diag_blocks(dwa), dba[0], _diag_blocks(dwx), dbx[0], dlam[0]
        dhq, dab, dgcw, dal, ddt, dng = gdn_bwd(sv["hgd"], wt["gdn_conv_w"][i], lw["al"], lw["dt"], lw["ng"], sv["opre_g"],
                                                sv["tmat"], sv["st_g"], do_g)
        g["gdn_conv_w"] = dgcw
        g["gdn_a_log"], g["gdn_dt_bias"] = dal[0, ::HEAD], ddt[0, ::HEAD]
        g["gdn_norm_g"] = dng[0].reshape(GDN_HEADS, HEAD).sum(0)
        dx1 = win_bwd(dr2, dhr, dhl, dhq, dab, lw["w_r"], lw["w_l"], lw["w_g"], lw["w_ab"])
        used = RET_IN + LRU_IN + 4 * GDN_W + AB_ROWS
        bg["w_in"] = jnp.concatenate(
            [wgrad(dhr, sv["x1"], "wgrad_in_r" + tag), wgrad(dhl, sv["x1"], "wgrad_in_l" + tag),
             wgrad(dhq, sv["x1"], "wgrad_in_q" + tag), wgrad(dab, sv["x1"], "wgrad_in_ab" + tag)[0:AB_ROWS],
             jnp.zeros((WIN_T_ROWS - used, D_MODEL), BF16)], axis=0)
        dx, act1, dhg1, dhu1, dy1, dg1, db1 = ffn_bwd(dx1, sv["r1"], sv["x0"], sv["hg1"], sv["hu1"], row(wt["ln_ffn1_g"][i]),
                                                      lw["wg1"], lw["wu1"], lw["wd1"])
        bg["ffn1_w_gate"] = wgrad(dhg1, sv["x0"], "wgrad_gate1" + tag)
        bg["ffn1_w_up"] = wgrad(dhu1, sv["x0"], "wgrad_up1" + tag)
        bg["ffn1_w_down"] = wgrad(act1, dy1, "wgrad_down1" + tag)
        g["ln_ffn1_g"], g["ln_ffn1_b"] = dg1[0], db1[0]
        grads[i] = g
        big[i] = bg
    return loss, dx, {k: jnp.stack([grads[i][k] for i in range(DEPTH)]) for k in grads[0]}, big


def _natural_grad(name, rows):
    if name == "ple_w_proj":
        return rows.reshape(-1, PLE_DIM).T
    return rows.T if name in _TRANSPOSED else rows


_SPLIT = dict(ffn1_w_gate=2, ffn1_w_up=2, ffn1_w_down=1, w_in=2, w_out=1, ffn2_w_gate=2, ffn2_w_up=2, ffn2_w_down=1,
              ple_w_gate=1, ple_w_proj=2)
_CONV = ("lru_conv_w", "gdn_conv_w")
_WHOLE = ("ln_ffn1_g", "ln_ffn1_b", "ret_norm_g", "lru_conv_b", "lru_w_a", "lru_b_a", "lru_w_x", "lru_b_x", "lru_lambda",
          "gdn_a_log", "gdn_dt_bias", "gdn_norm_g", "ln_mix_g", "ln_mix_b", "ln_ffn2_g", "ln_ffn2_b")
_WEIGHTS = ("ln_ffn1_g", "ln_ffn1_b", "ffn1_w_gate", "ffn1_w_up", "ffn1_w_down", "w_in", "ret_norm_g", "lru_conv_w", "lru_conv_b",
            "lru_w_a", "lru_b_a", "lru_w_x", "lru_b_x", "lru_lambda", "gdn_conv_w", "gdn_a_log", "gdn_dt_bias", "gdn_norm_g",
            "w_out", "ln_mix_g", "ln_mix_b", "ffn2_w_gate", "ffn2_w_up", "ffn2_w_down", "ple_w_gate", "ple_w_proj",
            "ln_ffn2_g", "ln_ffn2_b")
_INPUTS = ("x", "p", "positions") + _WEIGHTS + ("loss_target",) + tuple("m_" + n for n in _WEIGHTS) + tuple("v_" + n for n in _WEIGHTS)

BIG_COLS = 1024
BIG_HALF_ROWS_MULT = 128
SMALL_COLS = LANES
SMALL_ROWS_MULT = 8


def _pack(arrays, dtype, cols, rows_mult):
    flat = jnp.concatenate([a.reshape(-1).astype(dtype) for a in arrays])
    rows = -(-flat.shape[0] // cols)
    rows = -(-rows // rows_mult) * rows_mult
    return jnp.pad(flat, (0, rows * cols - flat.shape[0])).reshape(rows, cols)


def _unpack(packed, shapes):
    flat = packed.reshape(-1)
    out, off = [], 0
    for shp in shapes:
        size = int(np.prod(shp))
        out.append(flat[off:off + size].reshape(shp))
        off += size
    return out


def _as2d(a):
    return a.reshape(-1, a.shape[-1])


def kernel(x, p, positions, ln_ffn1_g, ln_ffn1_b, ffn1_w_gate, ffn1_w_up, ffn1_w_down, w_in, ret_norm_g, lru_conv_w, lru_conv_b, lru_w_a, lru_b_a, lru_w_x, lru_b_x, lru_lambda, gdn_conv_w, gdn_a_log, gdn_dt_bias, gdn_norm_g, w_out, ln_mix_g, ln_mix_b, ffn2_w_gate, ffn2_w_up, ffn2_w_down, ple_w_gate, ple_w_proj, ln_ffn2_g, ln_ffn2_b, loss_target, m_ln_ffn1_g, m_ln_ffn1_b, m_ffn1_w_gate, m_ffn1_w_up, m_ffn1_w_down, m_w_in, m_ret_norm_g, m_lru_conv_w, m_lru_conv_b, m_lru_w_a, m_lru_b_a, m_lru_w_x, m_lru_b_x, m_lru_lambda, m_gdn_conv_w, m_gdn_a_log, m_gdn_dt_bias, m_gdn_norm_g, m_w_out, m_ln_mix_g, m_ln_mix_b, m_ffn2_w_gate, m_ffn2_w_up, m_ffn2_w_down, m_ple_w_gate, m_ple_w_proj, m_ln_ffn2_g, m_ln_ffn2_b, v_ln_ffn1_g, v_ln_ffn1_b, v_ffn1_w_gate, v_ffn1_w_up, v_ffn1_w_down, v_w_in, v_ret_norm_g, v_lru_conv_w, v_lru_conv_b, v_lru_w_a, v_lru_b_a, v_lru_w_x, v_lru_b_x, v_lru_lambda, v_gdn_conv_w, v_gdn_a_log, v_gdn_dt_bias, v_gdn_norm_g, v_w_out, v_ln_mix_g, v_ln_mix_b, v_ffn2_w_gate, v_ffn2_w_up, v_ffn2_w_down, v_ple_w_gate, v_ple_w_proj, v_ln_ffn2_g, v_ln_ffn2_b):
    a = dict(zip(_INPUTS, (x, p, positions, ln_ffn1_g, ln_ffn1_b, ffn1_w_gate, ffn1_w_up, ffn1_w_down, w_in, ret_norm_g, lru_conv_w, lru_conv_b, lru_w_a, lru_b_a, lru_w_x, lru_b_x, lru_lambda, gdn_conv_w, gdn_a_log, gdn_dt_bias, gdn_norm_g, w_out, ln_mix_g, ln_mix_b, ffn2_w_gate, ffn2_w_up, ffn2_w_down, ple_w_gate, ple_w_proj, ln_ffn2_g, ln_ffn2_b, loss_target, m_ln_ffn1_g, m_ln_ffn1_b, m_ffn1_w_gate, m_ffn1_w_up, m_ffn1_w_down, m_w_in, m_ret_norm_g, m_lru_conv_w, m_lru_conv_b, m_lru_w_a, m_lru_b_a, m_lru_w_x, m_lru_b_x, m_lru_lambda, m_gdn_conv_w, m_gdn_a_log, m_gdn_dt_bias, m_gdn_norm_g, m_w_out, m_ln_mix_g, m_ln_mix_b, m_ffn2_w_gate, m_ffn2_w_up, m_ffn2_w_down, m_ple_w_gate, m_ple_w_proj, m_ln_ffn2_g, m_ln_ffn2_b, v_ln_ffn1_g, v_ln_ffn1_b, v_ffn1_w_gate, v_ffn1_w_up, v_ffn1_w_down, v_w_in, v_ret_norm_g, v_lru_conv_w, v_lru_conv_b, v_lru_w_a, v_lru_b_a, v_lru_w_x, v_lru_b_x, v_lru_lambda, v_gdn_conv_w, v_gdn_a_log, v_gdn_dt_bias, v_gdn_norm_g, v_w_out, v_ln_mix_g, v_ln_mix_b, v_ffn2_w_gate, v_ffn2_w_up, v_ffn2_w_down, v_ple_w_gate, v_ple_w_proj, v_ln_ffn2_g, v_ln_ffn2_b)))
    assert len(a) == len(_INPUTS)
    core = lax.axis_index("c")
    chip = 2 * lax.axis_index("x") + lax.axis_index("y")
    big = list(_SPLIT)

    packed_w = _pack([a[n] for n in big], BF16, BIG_COLS, 2 * BIG_HALF_ROWS_MULT)
    half_rows = packed_w.shape[0] // 2
    my_half = lax.dynamic_slice_in_dim(packed_w, core * half_rows, half_rows, axis=0)
    gathered = all_gather8(my_half, "gather_weights").reshape(N_CHIPS, 2 * half_rows * BIG_COLS)
    wt = {}
    off = 0
    for n in big:
        shp = a[n].shape
        size = int(np.prod(shp))
        parts = gathered[:, off:off + size].reshape((N_CHIPS,) + shp)
        wt[n] = jnp.concatenate([parts[k] for k in range(N_CHIPS)], axis=_SPLIT[n])
        off += size
    conv_g = all_gather8(_pack([a[n] for n in _CONV], F32, SMALL_COLS, SMALL_ROWS_MULT), "gather_conv_weights")[0::2]
    conv_g = conv_g.reshape(N_CHIPS, -1)
    off = 0
    for n in _CONV:
        shp = a[n].shape
        size = int(np.prod(shp))
        parts = conv_g[:, off:off + size].reshape((N_CHIPS,) + shp)
        wt[n] = jnp.concatenate([parts[k] for k in range(N_CHIPS)], axis=2)
        off += size
    for n in _WHOLE:
        wt[n] = a[n]

    seq = a["x"].shape[1]
    loss_part, dx, grads, big_src = _local_step(a["x"][0], a["p"][:, 0], a["positions"].reshape(seq, 1), a["loss_target"][0], wt)
    loss = lax.psum(loss_part[0, 0], ("x", "y", "c"))

    pieces = [(r, stride) for _, r, stride in _GRAD_PIECES]
    names = [n for n, _, _ in _GRAD_PIECES]
    layer0 = [big_src[0][n] for n in names]
    layer1 = [big_src[1][n] for n in names]
    from_sibling = sibling_swap_layers(layer0, layer1, "reduce_pair_swap")
    pair = [pair_add(u, v, w, "reduce_pair_add_" + n) for n, u, v, w in zip(names, layer0, layer1, from_sibling)]
    arrived = chip_exchange(pair, pieces, "reduce_chip_exchange")
    my_layer_sum = sum_leading(arrived, "reduce_chip_sum")
    other_layer_sum = sibling_swap(my_layer_sum, "reduce_pair_share")
    reduced = [jnp.where(core == layer, my_layer_sum, other_layer_sum) for layer in range(DEPTH)]
    big_grads = {}
    off = 0
    for n, r, _ in _GRAD_PIECES:
        per_layer = []
        for layer in range(DEPTH):
            rows = reduced[layer][off:off + r]
            if n == "w_in":
                rows = lax.dynamic_slice_in_dim(rows, chip * (WIN_SHARD - WIN_STRIDE), WIN_SHARD, axis=0)
            per_layer.append(_natural_grad(n, rows))
        big_grads[n] = jnp.stack(per_layer)
        off += r

    small_names = list(_WHOLE) + list(_CONV)
    small_local = _pack([grads[n] for n in small_names], F32, SMALL_COLS, SMALL_ROWS_MULT)
    small_sum = sum_leading(all_gather8(small_local, "gather_small_grads"), "sum_small_grads")
    small_grads = dict(zip(small_names, _unpack(small_sum, [grads[n].shape for n in small_names])))
    for n in _CONV:
        width = a[n].shape[2]
        small_grads[n] = lax.dynamic_slice_in_dim(small_grads[n], chip * width, width, axis=2)

    new = {}
    for n in big:
        d, nm, nv = adamw(_as2d(a[n]), _as2d(big_grads[n]), _as2d(a["m_" + n]), _as2d(a["v_" + n]), "adamw_" + n)
        new[n] = tuple(t.reshape(a[n].shape) for t in (d, nm, nv))
    pk = lambda prefix: _pack([a[prefix + n] for n in small_names], F32, SMALL_COLS, SMALL_ROWS_MULT)
    pg = _pack([small_grads[n] for n in small_names], F32, SMALL_COLS, SMALL_ROWS_MULT)
    outs = adamw(pk(""), pg, pk("m_"), pk("v_"), "adamw_small")
    shapes = [a[n].shape for n in small_names]
    for n, d, nm, nv in zip(small_names, *[_unpack(o, shapes) for o in outs]):
        new[n] = (d, nm, nv)
    all_grads = {**big_grads, **small_grads}
    return (loss, dx[None], *[all_grads[n] for n in _WEIGHTS], *[new[n][0] for n in _WEIGHTS],
            *[new[n][1] for n in _WEIGHTS], *[new[n][2] for n in _WEIGHTS])
```

```python
import functools
import math

import numpy as np
import jax
import jax.numpy as jnp
from jax import lax
from jax.experimental import pallas as pl
from jax.experimental.pallas import tpu as pltpu

F32 = jnp.float32
BF16 = jnp.bfloat16

D_MODEL = 1024
D_FF = 2816
PLE_DIM = 256
DEPTH = 2
CHUNK = 64
RET_HEADS = 4
RET_W = 256
LRU_W = 384
LRU_BLOCKS = 6
GDN_HEADS = 6
GDN_W = 384
HEAD = 64
D_IN = 3340
RET_IN = 4 * RET_W
LRU_IN = 2 * LRU_W
GDN_IN = 6 * GDN_W
ROPE_THETA = 10000.0
ALPHA = (2 * DEPTH) ** 0.25
LN_EPS = 1e-5
LRU_C = 8.0
N_CHIPS = 4
N_DEV = 8

ADAM_LR = 0.001
ADAM_B1 = 0.9
ADAM_B2 = 0.999
ADAM_EPS = 1e-08
ADAM_WD = 0.01
ADAM_STEP = 10

LANES = 128
VMEM_LIMIT = 56 * 1024 * 1024
ROW_TILE = 256
ROW_TILE_BWD = 512
SCAN_TILE = 256


def _params(*sem):
    return pltpu.CompilerParams(dimension_semantics=sem, vmem_limit_bytes=VMEM_LIMIT)


def _mm(a, b):
    return jnp.dot(a, b, preferred_element_type=F32)


def _mm_nt(a, b):
    return lax.dot_general(a, b, (((1,), (1,)), ((), ())), preferred_element_type=F32)


def _mm_tn(a, b):
    return lax.dot_general(a, b, (((0,), (0,)), ((), ())), preferred_element_type=F32)


def _split(a):
    hi = a.astype(BF16)
    lo = (a - hi.astype(F32)).astype(BF16)
    return hi, lo


def _mm3(a, b):
    ah, al = _split(a)
    bh, bl = _split(b)
    return _mm(ah, bh) + (_mm(ah, bl) + _mm(al, bh))


def _sigmoid(x):
    return jax.nn.sigmoid(x)


def _log1p(u):
    w = 1.0 + u
    return jnp.where(w == 1.0, u, jnp.log(w) * (u / jnp.where(w == 1.0, 1.0, w - 1.0)))


def _expm1(y):
    u = jnp.exp(y)
    um1 = u - 1.0
    safe = jnp.where((u == 1.0) | (um1 == -1.0), 1.0, jnp.log(jnp.where(u == 0.0, 1.0, u)))
    return jnp.where(u == 1.0, y, jnp.where(um1 == -1.0, -1.0, um1 * (y / safe)))


def _softplus(x):
    return jnp.maximum(x, 0.0) + _log1p(jnp.exp(-jnp.abs(x)))


_GELU_C = math.sqrt(2.0 / math.pi)


def _gelu(x):
    return 0.5 * x * (1.0 + jnp.tanh(_GELU_C * (x + 0.044715 * (x * x * x))))


def _gelu_grad(x):
    t = jnp.tanh(_GELU_C * (x + 0.044715 * (x * x * x)))
    return 0.5 * (1.0 + t) + 0.5 * x * (1.0 - t * t) * (_GELU_C * (1.0 + 3.0 * 0.044715 * (x * x)))


def _silu_and_grad(x):
    s = _sigmoid(x)
    return x * s, s * (1.0 + x * (1.0 - s))


def _group_sum_slab(x):
    lane = lax.broadcasted_iota(jnp.int32, x.shape, 1)
    s = x
    for d in (1, 2, 4, 8, 16, 32):
        s = s + jnp.where((lane & d) == 0, pltpu.roll(s, LANES - d, 1), pltpu.roll(s, d, 1))
    return s


def _group_sum(x):
    n = x.shape[1] // LANES
    if n == 1:
        return _group_sum_slab(x)
    return jnp.concatenate([_group_sum_slab(x[:, LANES * i:LANES * (i + 1)]) for i in range(n)], axis=1)


def _rows_prefix_sum(x):
    n = x.shape[0]
    row = lax.broadcasted_iota(jnp.int32, x.shape, 0)
    d = 1
    while d < n:
        x = x + jnp.where(row >= d, pltpu.roll(x, d, 0), 0.0)
        d *= 2
    return x


def _rows_suffix_sum(x):
    n = x.shape[0]
    row = lax.broadcasted_iota(jnp.int32, x.shape, 0)
    d = 1
    while d < n:
        x = x + jnp.where(row < n - d, pltpu.roll(x, n - d, 0), 0.0)
        d *= 2
    return x


def _shift_rows(cur, prev, j):
    row = lax.broadcasted_iota(jnp.int32, cur.shape, 0)
    return jnp.where(row < j, pltpu.roll(prev, j, 0), pltpu.roll(cur, j, 0))


def _shift_rows_up(cur, nxt, j):
    n = cur.shape[0]
    row = lax.broadcasted_iota(jnp.int32, cur.shape, 0)
    return jnp.where(row < n - j, pltpu.roll(cur, n - j, 0), pltpu.roll(nxt, n - j, 0))


def _layer_norm_stats(r):
    mu = jnp.mean(r, axis=-1, keepdims=True)
    d = r - mu
    var = jnp.mean(d * d, axis=-1, keepdims=True)
    rstd = lax.rsqrt(var + LN_EPS)
    return d * rstd, rstd


def _load_resident(step, pairs, sems):
    @pl.when(step == 0)
    def _():
        cps = [pltpu.make_async_copy(h, v, sems.at[i]) for i, (h, v) in enumerate(pairs)]
        for c in cps:
            c.start()
        for c in cps:
            c.wait()


def _row_spec(tile, width):
    return pl.BlockSpec((tile, width), lambda i: (i, 0))


def _full_spec(shape):
    nd = len(shape)
    return pl.BlockSpec(shape, lambda i: (0,) * nd)


_ANY = pl.BlockSpec(memory_space=pl.ANY)


def ffn_fwd(x, ln_g, ln_b, w_gate, w_up, w_down, ple=None):
    s = x.shape[0]
    tm = ROW_TILE
    with_ple = ple is not None
    weights = [w_gate, w_up, w_down] + ([ple[1], ple[2]] if with_ple else [])

    def body(*refs):
        it = iter(refs)
        x_ref, g_ref, b_ref = next(it), next(it), next(it)
        p_ref = next(it) if with_ple else None
        w_hbm = [next(it) for _ in weights]
        hg_ref, hu_ref, r_ref, xn_ref = next(it), next(it), next(it), next(it)
        pg_ref, pp_ref = (next(it), next(it)) if with_ple else (None, None)
        w_vm = [next(it) for _ in weights]
        sems = next(it)
        _load_resident(pl.program_id(0), list(zip(w_hbm, w_vm)), sems)
        xv = x_ref[...]
        xb = xv.astype(BF16)
        hg = _mm(xb, w_vm[0][...])
        hu = _mm(xb, w_vm[1][...])
        hg_ref[...] = hg
        hu_ref[...] = hu
        act = (hg * _sigmoid(hg)) * hu
        r = ALPHA * xv + 0.5 * _mm(act.astype(BF16), w_vm[2][...])
        if with_ple:
            pg = _mm(xb, w_vm[3][...])
            pp = _mm(p_ref[...].astype(BF16), w_vm[4][...])
            pg_ref[...] = pg
            pp_ref[...] = pp
            r = r + _sigmoid(pg) * pp
        r_ref[...] = r
        xhat, _ = _layer_norm_stats(r)
        xn_ref[...] = xhat * g_ref[...] + b_ref[...]

    d, f = D_MODEL, D_FF
    in_specs = [_row_spec(tm, d), _full_spec((1, d)), _full_spec((1, d))]
    args = [x, ln_g, ln_b]
    if with_ple:
        in_specs.append(_row_spec(tm, PLE_DIM))
        args.append(ple[0])
    in_specs += [_ANY] * len(weights)
    args += weights
    out_shape = [jax.ShapeDtypeStruct((s, f), F32), jax.ShapeDtypeStruct((s, f), F32),
                 jax.ShapeDtypeStruct((s, d), F32), jax.ShapeDtypeStruct((s, d), F32)]
    out_specs = [_row_spec(tm, f), _row_spec(tm, f), _row_spec(tm, d), _row_spec(tm, d)]
    if with_ple:
        out_shape += [jax.ShapeDtypeStruct((s, d), F32)] * 2
        out_specs += [_row_spec(tm, d)] * 2
    scratch = [pltpu.VMEM(w.shape, w.dtype) for w in weights] + [pltpu.SemaphoreType.DMA((len(weights),))]
    return pl.pallas_call(
        body, name="ffn_fwd_ple" if with_ple else "ffn_fwd", grid=(s // tm,), in_specs=in_specs, out_specs=out_specs,
        out_shape=out_shape, scratch_shapes=scratch, compiler_params=_params("arbitrary"),
    )(*args)


def ffn_bwd(dxn, r, x, hg, hu, ln_g, w_gate, w_up, w_down, ple=None):
    s = x.shape[0]
    with_ple = ple is not None
    d, f = D_MODEL, D_FF
    suffix = "_ple" if with_ple else ""

    tm = ROW_TILE

    def body_a(*refs):
        it = iter(refs)
        dxn_ref, r_ref, hg_ref, hu_ref, g_ref = (next(it) for _ in range(5))
        pg_ref, pp_ref = (next(it), next(it)) if with_ple else (None, None)
        wd_hbm = next(it)
        dr_ref, act_ref, dhg_ref, dhu_ref, dy_ref, dg_ref, db_ref = (next(it) for _ in range(7))
        dpg_ref, dpp_ref = (next(it), next(it)) if with_ple else (None, None)
        wd_vm, sems = next(it), next(it)
        step = pl.program_id(0)
        _load_resident(step, [(wd_hbm, wd_vm)], sems)

        @pl.when(step == 0)
        def _():
            dg_ref[...] = jnp.zeros_like(dg_ref)
            db_ref[...] = jnp.zeros_like(db_ref)

        dxn_v = dxn_ref[...]
        xhat, rstd = _layer_norm_stats(r_ref[...])
        dg_ref[...] += jnp.sum(dxn_v * xhat, axis=0, keepdims=True)
        db_ref[...] += jnp.sum(dxn_v, axis=0, keepdims=True)
        dyh = dxn_v * g_ref[...]
        dr = rstd * (dyh - jnp.mean(dyh, axis=-1, keepdims=True) - xhat * jnp.mean(dyh * xhat, axis=-1, keepdims=True))
        dr_ref[...] = dr
        dy = (0.5 * dr).astype(BF16)
        dy_ref[...] = dy
        da = _mm_nt(dy, wd_vm[...])
        hg_v = hg_ref[...]
        hu_v = hu_ref[...]
        sil, dsil = _silu_and_grad(hg_v)
        act_ref[...] = (sil * hu_v).astype(BF16)
        dhu_ref[...] = (da * sil).astype(BF16)
        dhg_ref[...] = (da * hu_v * dsil).astype(BF16)
        if with_ple:
            sp = _sigmoid(pg_ref[...])
            dpp_ref[...] = (dr * sp).astype(BF16)
            dpg_ref[...] = (dr * pp_ref[...] * sp * (1.0 - sp)).astype(BF16)

    in_specs = [_row_spec(tm, d), _row_spec(tm, d), _row_spec(tm, f), _row_spec(tm, f), _full_spec((1, d))]
    args = [dxn, r, hg, hu, ln_g]
    if with_ple:
        in_specs += [_row_spec(tm, d), _row_spec(tm, d)]
        args += [ple[0], ple[1]]
    out_shape = [jax.ShapeDtypeStruct((s, d), F32), jax.ShapeDtypeStruct((s, f), BF16), jax.ShapeDtypeStruct((s, f), BF16),
                 jax.ShapeDtypeStruct((s, f), BF16), jax.ShapeDtypeStruct((s, d), BF16),
                 jax.ShapeDtypeStruct((1, d), F32), jax.ShapeDtypeStruct((1, d), F32)]
    out_specs = [_row_spec(tm, d), _row_spec(tm, f), _row_spec(tm, f), _row_spec(tm, f), _row_spec(tm, d),
                 _full_spec((1, d)), _full_spec((1, d))]
    if with_ple:
        out_shape += [jax.ShapeDtypeStruct((s, d), BF16)] * 2
        out_specs += [_row_spec(tm, d)] * 2
    first = pl.pallas_call(
        body_a, name="ffn_bwd_hidden" + suffix, grid=(s // tm,), in_specs=in_specs + [_ANY], out_specs=out_specs,
        out_shape=out_shape, scratch_shapes=[pltpu.VMEM(w_down.shape, w_down.dtype), pltpu.SemaphoreType.DMA((1,))],
        compiler_params=_params("arbitrary"),
    )(*args, w_down)
    dr, act, dhg, dhu, dy, dg, db = first[:7]

    tb = min(ROW_TILE_BWD, s)
    weights = [w_gate, w_up] + ([ple[2]] if with_ple else [])

    def body_b(*refs):
        it = iter(refs)
        dr_ref, dhg_ref, dhu_ref = next(it), next(it), next(it)
        dpg_ref = next(it) if with_ple else None
        w_hbm = [next(it) for _ in weights]
        dx_ref = next(it)
        w_vm = [next(it) for _ in weights]
        sems = next(it)
        _load_resident(pl.program_id(0), list(zip(w_hbm, w_vm)), sems)
        dx = ALPHA * dr_ref[...] + _mm_nt(dhg_ref[...], w_vm[0][...]) + _mm_nt(dhu_ref[...], w_vm[1][...])
        if with_ple:
            dx = dx + _mm_nt(dpg_ref[...], w_vm[2][...])
        dx_ref[...] = dx

    in_specs = [_row_spec(tb, d), _row_spec(tb, f), _row_spec(tb, f)] + ([_row_spec(tb, d)] if with_ple else [])
    args = [dr, dhg, dhu] + ([first[7]] if with_ple else [])
    dx = pl.pallas_call(
        body_b, name="ffn_bwd_input" + suffix, grid=(s // tb,), in_specs=in_specs + [_ANY] * len(weights),
        out_specs=_row_spec(tb, d), out_shape=jax.ShapeDtypeStruct((s, d), F32),
        scratch_shapes=[pltpu.VMEM(w.shape, w.dtype) for w in weights] + [pltpu.SemaphoreType.DMA((len(weights),))],
        compiler_params=_params("arbitrary"),
    )(*args, *weights)
    return (dx, act, dhg, dhu, dy, dg, db) + tuple(first[7:])


def win_fwd(x1, w_r, w_l, w_g):
    s = x1.shape[0]
    tm = min(ROW_TILE_BWD, s)
    weights = [w_r, w_l, w_g]

    def body(x_ref, wr_h, wl_h, wg_h, hr_ref, hl_ref, hgd_ref, wr_v, wl_v, wg_v, sems):
        _load_resident(pl.program_id(0), [(wr_h, wr_v), (wl_h, wl_v), (wg_h, wg_v)], sems)
        xb = x_ref[...].astype(BF16)
        hr_ref[...] = _mm(xb, wr_v[...])
        hl_ref[...] = _mm(xb, wl_v[...])
        hgd_ref[...] = _mm(xb, wg_v[...])

    return pl.pallas_call(
        body, name="win_fwd", grid=(s // tm,),
        in_specs=[_row_spec(tm, D_MODEL), _ANY, _ANY, _ANY],
        out_specs=[_row_spec(tm, RET_IN), _row_spec(tm, LRU_IN), _row_spec(tm, GDN_IN)],
        out_shape=[jax.ShapeDtypeStruct((s, RET_IN), F32), jax.ShapeDtypeStruct((s, LRU_IN), F32),
                   jax.ShapeDtypeStruct((s, GDN_IN), F32)],
        scratch_shapes=[pltpu.VMEM(w.shape, w.dtype) for w in weights] + [pltpu.SemaphoreType.DMA((3,))],
        compiler_params=_params("arbitrary"),
    )(x1, *weights)


def win_bwd(dr2, dhr, dhl, dhq, dab, w_r, w_l, w_g, w_ab):
    s = dr2.shape[0]
    tm = min(ROW_TILE_BWD, s)
    weights = [w_r, w_l, w_g, w_ab]
    nq = 4 * GDN_W

    def body(dr_ref, dhr_ref, dhl_ref, dhq_ref, dab_ref, wr_h, wl_h, wg_h, wab_h, dx_ref, wr_v, wl_v, wg_v, wab_v, sems):
        _load_resident(pl.program_id(0), [(wr_h, wr_v), (wl_h, wl_v), (wg_h, wg_v), (wab_h, wab_v)], sems)
        dx_ref[...] = (ALPHA * dr_ref[...] + _mm_nt(dhr_ref[...], wr_v[...]) + _mm_nt(dhl_ref[...], wl_v[...])
                       + _mm_nt(dhq_ref[...], wg_v[:, 0:nq]) + _mm_nt(dab_ref[...], wab_v[...]))

    return pl.pallas_call(
        body, name="win_bwd", grid=(s // tm,),
        in_specs=[_row_spec(tm, D_MODEL), _row_spec(tm, RET_IN), _row_spec(tm, LRU_IN), _row_spec(tm, nq), _row_spec(tm, LANES),
                  _ANY, _ANY, _ANY, _ANY],
        out_specs=_row_spec(tm, D_MODEL),
        out_shape=jax.ShapeDtypeStruct((s, D_MODEL), F32),
        scratch_shapes=[pltpu.VMEM(w.shape, w.dtype) for w in weights] + [pltpu.SemaphoreType.DMA((4,))],
        compiler_params=_params("arbitrary"),
    )(dr2, dhr, dhl, dhq, dab, *weights)


def out_fwd(o_r, o_l, o_g, x1, w_out, ln_g, ln_b):
    s = x1.shape[0]
    tm = ROW_TILE

    def body(or_ref, ol_ref, og_ref, x_ref, g_ref, b_ref, w_h, r_ref, xn_ref, ocat_ref, w_v, sems):
        _load_resident(pl.program_id(0), [(w_h, w_v)], sems)
        ocat = jnp.concatenate([or_ref[...], ol_ref[...], og_ref[...]], axis=1).astype(BF16)
        ocat_ref[...] = ocat
        r = ALPHA * x_ref[...] + _mm(ocat, w_v[...])
        r_ref[...] = r
        xhat, _ = _layer_norm_stats(r)
        xn_ref[...] = xhat * g_ref[...] + b_ref[...]

    d = D_MODEL
    return pl.pallas_call(
        body, name="out_fwd", grid=(s // tm,),
        in_specs=[_row_spec(tm, RET_W), _row_spec(tm, LRU_W), _row_spec(tm, GDN_W), _row_spec(tm, d),
                  _full_spec((1, d)), _full_spec((1, d)), _ANY],
        out_specs=[_row_spec(tm, d), _row_spec(tm, d), _row_spec(tm, d)],
        out_shape=[jax.ShapeDtypeStruct((s, d), F32)] * 2 + [jax.ShapeDtypeStruct((s, d), BF16)],
        scratch_shapes=[pltpu.VMEM(w_out.shape, w_out.dtype), pltpu.SemaphoreType.DMA((1,))],
        compiler_params=_params("arbitrary"),
    )(o_r, o_l, o_g, x1, ln_g, ln_b, w_out)


def out_bwd(dxn, r2, ln_g, w_out):
    s = dxn.shape[0]
    tm = ROW_TILE

    def body(dxn_ref, r_ref, g_ref, w_h, dr_ref, drb_ref, dor_ref, dol_ref, dog_ref, dg_ref, db_ref, w_v, sems):
        step = pl.program_id(0)
        _load_resident(step, [(w_h, w_v)], sems)

        @pl.when(step == 0)
        def _():
            dg_ref[...] = jnp.zeros_like(dg_ref)
            db_ref[...] = jnp.zeros_like(db_ref)

        dxn_v = dxn_ref[...]
        xhat, rstd = _layer_norm_stats(r_ref[...])
        dg_ref[...] += jnp.sum(dxn_v * xhat, axis=0, keepdims=True)
        db_ref[...] += jnp.sum(dxn_v, axis=0, keepdims=True)
        dyh = dxn_v * g_ref[...]
        dr = rstd * (dyh - jnp.mean(dyh, axis=-1, keepdims=True) - xhat * jnp.mean(dyh * xhat, axis=-1, keepdims=True))
        dr_ref[...] = dr
        drb = dr.astype(BF16)
        drb_ref[...] = drb
        dor_ref[...] = _mm_nt(drb, w_v[0:RET_W, :])
        dol_ref[...] = _mm_nt(drb, w_v[RET_W:RET_W + LRU_W, :])
        dog_ref[...] = _mm_nt(drb, w_v[RET_W + LRU_W:, :])

    d = D_MODEL
    return pl.pallas_call(
        body, name="out_bwd", grid=(s // tm,),
        in_specs=[_row_spec(tm, d), _row_spec(tm, d), _full_spec((1, d)), _ANY],
        out_specs=[_row_spec(tm, d), _row_spec(tm, d), _row_spec(tm, RET_W), _row_spec(tm, LRU_W), _row_spec(tm, GDN_W),
                   _full_spec((1, d)), _full_spec((1, d))],
        out_shape=[jax.ShapeDtypeStruct((s, d), F32), jax.ShapeDtypeStruct((s, d), BF16),
                   jax.ShapeDtypeStruct((s, RET_W), F32), jax.ShapeDtypeStruct((s, LRU_W), F32),
                   jax.ShapeDtypeStruct((s, GDN_W), F32), jax.ShapeDtypeStruct((1, d), F32), jax.ShapeDtypeStruct((1, d), F32)],
        scratch_shapes=[pltpu.VMEM(w_out.shape, w_out.dtype), pltpu.SemaphoreType.DMA((1,))],
        compiler_params=_params("arbitrary"),
    )(dxn, r2, ln_g, w_out)


def wgrad(a, b, name, out_dtype=BF16):
    s, m = a.shape
    n = b.shape[1]
    tk = 1024 if s % 1024 == 0 else s
    tm = next((c for c in (512, 384, 256) if m % c == 0), m)
    tn = next((c for c in (1408, 1152, 1024, 768, 512) if n % c == 0), n)
    nk = s // tk

    def body(a_ref, b_ref, o_ref, acc_ref):
        k = pl.program_id(2)

        @pl.when(k == 0)
        def _():
            acc_ref[...] = jnp.zeros_like(acc_ref)

        acc_ref[...] += _mm_tn(a_ref[...].astype(BF16), b_ref[...].astype(BF16))

        @pl.when(k == nk - 1)
        def _():
            o_ref[...] = acc_ref[...].astype(o_ref.dtype)

    return pl.pallas_call(
        body, name=name, grid=(m // tm, n // tn, nk),
        in_specs=[pl.BlockSpec((tk, tm), lambda i, j, k: (k, i)), pl.BlockSpec((tk, tn), lambda i, j, k: (k, j))],
        out_specs=pl.BlockSpec((tm, tn), lambda i, j, k: (i, j)),
        out_shape=jax.ShapeDtypeStruct((m, n), out_dtype),
        scratch_shapes=[pltpu.VMEM((tm, tn), F32)],
        compiler_params=_params("arbitrary", "arbitrary", "arbitrary"),
    )(a, b)


def loss_and_grad(y, target):
    s, d = y.shape
    tm = ROW_TILE

    def body(y_ref, t_ref, dy_ref, l_ref):
        @pl.when(pl.program_id(0) == 0)
        def _():
            l_ref[...] = jnp.zeros_like(l_ref)

        err = y_ref[...] - t_ref[...]
        dy_ref[...] = err / d
        l_ref[...] += 0.5 * jnp.sum(jnp.mean(err * err, axis=-1, keepdims=True), axis=0, keepdims=True)

    return pl.pallas_call(
        body, name="loss_and_grad", grid=(s // tm,),
        in_specs=[_row_spec(tm, d), _row_spec(tm, d)],
        out_specs=[_row_spec(tm, d), _full_spec((1, 1))],
        out_shape=[jax.ShapeDtypeStruct((s, d), F32), jax.ShapeDtypeStruct((1, 1), F32)],
        compiler_params=_params("arbitrary"),
    )(y, target)


def _ret_consts():
    lg = np.log1p(-np.exp2(-5.0 - np.arange(RET_HEADS, dtype=np.float64)))
    idx = np.arange(CHUNK, dtype=np.float64)
    intra = np.exp(np.abs(idx[:, None] - idx[None, :])[None] * lg[:, None, None])
    cross = np.repeat(np.exp((idx + 1.0)[:, None] * lg[None, :]), HEAD, axis=1)
    tail = np.repeat(np.exp((CHUNK - 1.0 - idx)[:, None] * lg[None, :]), HEAD, axis=1)
    dec = np.repeat(np.exp(CHUNK * lg)[None, :], HEAD, axis=1)
    half = HEAD // 2
    inv_freq = (ROPE_THETA ** (-jnp.arange(half, dtype=F32) / half))
    invf = jnp.tile(inv_freq, 2 * LANES // HEAD)[None, :]
    sgn = np.tile(np.concatenate([-np.ones(half), np.ones(half)]), LANES // HEAD)[None, :]
    f = lambda a: jnp.asarray(a, F32)
    return dict(intra=f(intra), cross=f(cross), tail=f(tail), dec=f(dec), invf=invf, sgn=f(sgn))


def _swap_halves(t):
    lane = lax.broadcasted_iota(jnp.int32, t.shape, 1)
    return jnp.where((lane & 32) == 0, pltpu.roll(t, LANES - 32, 1), pltpu.roll(t, 32, 1))


def _rope(t, c, s):
    return t * c + _swap_halves(t) * s


def _rope_transposed(g, c, s):
    return g * c + _swap_halves(g * s)


def _head_mask(hd):
    lane = lax.broadcasted_iota(jnp.int32, (1, LANES), 1)
    return ((lane >= HEAD * hd) & (lane < HEAD * (hd + 1))).astype(F32)


def _block_diag_mask():
    r = lax.broadcasted_iota(jnp.int32, (LANES, LANES), 0)
    c = lax.broadcasted_iota(jnp.int32, (LANES, LANES), 1)
    return ((r >= HEAD) == (c >= HEAD)).astype(F32)


def _ret_specs(n_of):
    cst = lambda shape: pl.BlockSpec(shape, lambda i: (0,) * len(shape))
    return [pl.BlockSpec((CHUNK, RET_IN), lambda i: (n_of(i), 0)), pl.BlockSpec((CHUNK, 1), lambda i: (n_of(i), 0)),
            cst((1, LANES)), cst((1, LANES)), cst((RET_HEADS, CHUNK, CHUNK)), cst((CHUNK, RET_W)), cst((CHUNK, RET_W)),
            cst((1, RET_W)), cst((1, RET_W))]


def ret_fwd(hr, pos, norm_g):
    s = hr.shape[0]
    n_chunks = s // CHUNK
    cs = _ret_consts()
    n_slab = RET_W // LANES

    def body(hr_ref, pos_ref, invf_ref, sgn_ref, intra_ref, cross_ref, tail_ref, dec_ref, g_ref, o_ref, opre_ref, st_ref, state):
        @pl.when(pl.program_id(0) == 0)
        def _():
            state[...] = jnp.zeros_like(state)

        ang = pos_ref[...].astype(F32) * invf_ref[...]
        cosv = jnp.cos(ang)
        sinv = jnp.sin(ang) * sgn_ref[...]
        bd = _block_diag_mask()
        for sl in range(n_slab):
            lanes = slice(LANES * sl, LANES * (sl + 1))
            rows = slice(LANES * sl, LANES * (sl + 1))
            q = hr_ref[:, LANES * sl:LANES * (sl + 1)]
            k = hr_ref[:, RET_W + LANES * sl:RET_W + LANES * (sl + 1)]
            v = hr_ref[:, 2 * RET_W + LANES * sl:2 * RET_W + LANES * (sl + 1)]
            gate = hr_ref[:, 3 * RET_W + LANES * sl:3 * RET_W + LANES * (sl + 1)]
            qt = _rope(q, cosv, sinv) * (HEAD ** -0.5)
            kt = _rope(k, cosv, sinv)
            st = state[rows, :]
            st_ref[rows, :] = st
            o = _mm(qt * cross_ref[:, lanes], st)
            for hd in range(2):
                m = _head_mask(hd)
                sc = _mm_nt(qt * m, kt) * intra_ref[2 * sl + hd]
                o = o + _mm(sc, v) * m
            state[rows, :] = st * dec_ref[:, lanes] + _mm_tn(kt, v * tail_ref[:, lanes]) * bd
            opre_ref[:, lanes] = o
            mu = _group_sum_slab(o) * (1.0 / HEAD)
            dlt = o - mu
            var = _group_sum_slab(dlt * dlt) * (1.0 / HEAD)
            on = dlt * lax.rsqrt(var + 1e-5)
            o_ref[:, lanes] = on * g_ref[:, lanes] * (gate * _sigmoid(gate))

    out_row = lambda w: pl.BlockSpec((CHUNK, w), lambda i: (i, 0))
    return pl.pallas_call(
        body, name="ret_fwd", grid=(n_chunks,),
        in_specs=_ret_specs(lambda i: i),
        out_specs=[out_row(RET_W), out_row(RET_W), pl.BlockSpec((RET_W, LANES), lambda i: (i, 0))],
        out_shape=[jax.ShapeDtypeStruct((s, RET_W), F32), jax.ShapeDtypeStruct((s, RET_W), F32),
                   jax.ShapeDtypeStruct((n_chunks * RET_W, LANES), F32)],
        scratch_shapes=[pltpu.VMEM((RET_W, LANES), F32)],
        compiler_params=_params("arbitrary"),
    )(hr, pos, cs["invf"], cs["sgn"], cs["intra"], cs["cross"], cs["tail"], cs["dec"], norm_g)


def ret_bwd(hr, pos, norm_g, opre, states, dout):
    s = hr.shape[0]
    n_chunks = s // CHUNK
    cs = _ret_consts()
    n_slab = RET_W // LANES
    rev = lambda i: n_chunks - 1 - i

    def body(hr_ref, pos_ref, invf_ref, sgn_ref, intra_ref, cross_ref, tail_ref, dec_ref, g_ref, opre_ref, st_ref, do_ref,
             dh_ref, dg_ref, gstate):
        @pl.when(pl.program_id(0) == 0)
        def _():
            gstate[...] = jnp.zeros_like(gstate)
            dg_ref[...] = jnp.zeros_like(dg_ref)

        ang = pos_ref[...].astype(F32) * invf_ref[...]
        cosv = jnp.cos(ang)
        sinv = jnp.sin(ang) * sgn_ref[...]
        bd = _block_diag_mask()
        for sl in range(n_slab):
            lanes = slice(LANES * sl, LANES * (sl + 1))
            rows = slice(LANES * sl, LANES * (sl + 1))
            q = hr_ref[:, LANES * sl:LANES * (sl + 1)]
            k = hr_ref[:, RET_W + LANES * sl:RET_W + LANES * (sl + 1)]
            v = hr_ref[:, 2 * RET_W + LANES * sl:2 * RET_W + LANES * (sl + 1)]
            gate = hr_ref[:, 3 * RET_W + LANES * sl:3 * RET_W + LANES * (sl + 1)]
            qt = _rope(q, cosv, sinv) * (HEAD ** -0.5)
            kt = _rope(k, cosv, sinv)
            o = opre_ref[:, lanes]
            mu = _group_sum_slab(o) * (1.0 / HEAD)
            dlt = o - mu
            var = _group_sum_slab(dlt * dlt) * (1.0 / HEAD)
            rstd = lax.rsqrt(var + 1e-5)
            on = dlt * rstd
            sil, dsil = _silu_and_grad(gate)
            dout_v = do_ref[:, lanes]
            gn = g_ref[:, lanes]
            dg_ref[:, lanes] += jnp.sum(dout_v * on * sil, axis=0, keepdims=True)
            d_on = dout_v * gn * sil
            dgate = dout_v * on * gn * dsil
            d_o = rstd * (d_on - _group_sum_slab(d_on) * (1.0 / HEAD) - on * (_group_sum_slab(d_on * on) * (1.0 / HEAD)))
            st = st_ref[rows, :]
            gs = gstate[rows, :]
            cross = cross_ref[:, lanes]
            tail = tail_ref[:, lanes]
            dqt = _mm_nt(d_o, st) * cross
            ds_here = _mm_tn(qt * cross, d_o) * bd
            vt = v * tail
            dkt = _mm_nt(vt, gs)
            dv = _mm(kt, gs) * tail
            for hd in range(2):
                m = _head_mask(hd)
                qm = qt * m
                dom = d_o * m
                intra = intra_ref[2 * sl + hd]
                sc = _mm_nt(qm, kt) * intra
                dsc = _mm_nt(dom, v) * intra
                dqt = dqt + _mm(dsc, kt) * m
                dkt = dkt + _mm_tn(dsc, qm)
                dv = dv + _mm_tn(sc, dom)
            gstate[rows, :] = gs * dec_ref[:, lanes] + ds_here
            dh_ref[:, LANES * sl:LANES * (sl + 1)] = _rope_transposed(dqt * (HEAD ** -0.5), cosv, sinv).astype(BF16)
            dh_ref[:, RET_W + LANES * sl:RET_W + LANES * (sl + 1)] = _rope_transposed(dkt, cosv, sinv).astype(BF16)
            dh_ref[:, 2 * RET_W + LANES * sl:2 * RET_W + LANES * (sl + 1)] = dv.astype(BF16)
            dh_ref[:, 3 * RET_W + LANES * sl:3 * RET_W + LANES * (sl + 1)] = dgate.astype(BF16)

    row = lambda w: pl.BlockSpec((CHUNK, w), lambda i: (rev(i), 0))
    return pl.pallas_call(
        body, name="ret_bwd", grid=(n_chunks,),
        in_specs=_ret_specs(rev) + [row(RET_W), pl.BlockSpec((RET_W, LANES), lambda i: (rev(i), 0)), row(RET_W)],
        out_specs=[row(RET_IN), pl.BlockSpec((1, RET_W), lambda i: (0, 0))],
        out_shape=[jax.ShapeDtypeStruct((s, RET_IN), BF16), jax.ShapeDtypeStruct((1, RET_W), F32)],
        scratch_shapes=[pltpu.VMEM((RET_W, LANES), F32)],
        compiler_params=_params("arbitrary"),
    )(hr, pos, cs["invf"], cs["sgn"], cs["intra"], cs["cross"], cs["tail"], cs["dec"], norm_g, opre, states, dout)


def _lru_gates(xc, wa_ref, ba_ref, wx_ref, bx_ref, lam_ref):
    xcb = xc.astype(BF16)
    r = _sigmoid(_mm(xcb, wa_ref[...].astype(BF16)) + ba_ref[...])
    ig = _sigmoid(_mm(xcb, wx_ref[...].astype(BF16)) + bx_ref[...])
    lam = lam_ref[...]
    ls = jnp.minimum(lam, 0.0) - _log1p(jnp.exp(-jnp.abs(lam)))
    la = (LRU_C * r) * ls
    a = jnp.exp(la)
    mult = jnp.sqrt(-_expm1(2.0 * la))
    return r, ig, ls, a, mult


def _lru_conv(x, xprev, w_ref, b_ref):
    xc = b_ref[...] + w_ref[3:4, :] * x
    for j in (1, 2, 3):
        xc = xc + w_ref[3 - j:4 - j, :] * _shift_rows(x, xprev, j)
    return xc


def lru_fwd(hl, conv_w, conv_b, w_a, b_a, w_x, b_x, lam):
    s = hl.shape[0]
    ts = SCAN_TILE
    w = LRU_W

    def body(hl_ref, hp_ref, cw_ref, cb_ref, wa_ref, ba_ref, wx_ref, bx_ref, lam_ref, o_ref, xc_ref, h_ref, carry):
        i = pl.program_id(0)

        @pl.when(i == 0)
        def _():
            carry[...] = jnp.zeros_like(carry)

        x = hl_ref[:, 0:w]
        gate = hl_ref[:, w:2 * w]
        xprev = hp_ref[...] * (i > 0).astype(F32)
        xc = _lru_conv(x, xprev, cw_ref, cb_ref)
        xc_ref[...] = xc
        _, ig, _, a, mult = _lru_gates(xc, wa_ref, ba_ref, wx_ref, bx_ref, lam_ref)
        b = mult * (ig * xc)
        row = lax.broadcasted_iota(jnp.int32, (ts, w), 0)
        d = 1
        while d < ts:
            ap = jnp.where(row >= d, pltpu.roll(a, d, 0), 1.0)
            bp = jnp.where(row >= d, pltpu.roll(b, d, 0), 0.0)
            b = a * bp + b
            a = a * ap
            d *= 2
        h = b + a * carry[0:1, :]
        h_ref[...] = h
        carry[0:1, :] = h[ts - 1:ts, :]
        o_ref[...] = h * _gelu(gate)

    cst = lambda shape: pl.BlockSpec(shape, lambda i: (0, 0))
    return pl.pallas_call(
        body, name="lru_fwd", grid=(s // ts,),
        in_specs=[_row_spec(ts, 2 * w), pl.BlockSpec((ts, w), lambda i: (jnp.maximum(i - 1, 0), 0)),
                  cst((4, w)), cst((1, w)), cst((w, w)), cst((1, w)), cst((w, w)), cst((1, w)), cst((1, w))],
        out_specs=[_row_spec(ts, w)] * 3,
        out_shape=[jax.ShapeDtypeStruct((s, w), F32)] * 3,
        scratch_shapes=[pltpu.VMEM((8, w), F32)],
        compiler_params=_params("arbitrary"),
    )(hl, hl, conv_w, conv_b, w_a, b_a, w_x, b_x, lam)


def lru_bwd(hl, conv_w, conv_b, w_a, b_a, w_x, b_x, lam, xc_saved, h_saved, dout):
    s = hl.shape[0]
    ts = SCAN_TILE
    w = LRU_W
    nb = s // ts
    rev = lambda i: nb - 1 - i

    def body(hl_ref, hp_ref, cw_ref, cb_ref, wa_ref, ba_ref, wx_ref, bx_ref, lam_ref, xc_ref, h_ref, hprev_ref, do_ref,
             dhl_ref, dcw_ref, dcb_ref, dwa_ref, dba_ref, dwx_ref, dbx_ref, dlam_ref, carry, dxc_next):
        i = pl.program_id(0)
        blk = nb - 1 - i

        @pl.when(i == 0)
        def _():
            carry[...] = jnp.zeros_like(carry)
            dxc_next[...] = jnp.zeros_like(dxc_next)
            for ref in (dcw_ref, dcb_ref, dwa_ref, dba_ref, dwx_ref, dbx_ref, dlam_ref):
                ref[...] = jnp.zeros_like(ref)

        first = (blk > 0).astype(F32)
        x = hl_ref[:, 0:w]
        gate = hl_ref[:, w:2 * w]
        xprev = hp_ref[...] * first
        xc = xc_ref[...]
        h = h_ref[...]
        hprev = hprev_ref[...] * first
        r, ig, ls, a, mult = _lru_gates(xc, wa_ref, ba_ref, wx_ref, bx_ref, lam_ref)
        do = do_ref[...]
        dh = do * _gelu(gate)
        dgate = do * h * _gelu_grad(gate)
        row = lax.broadcasted_iota(jnp.int32, (ts, w), 0)
        ca = jnp.where(row < ts - 1, pltpu.roll(a, ts - 1, 0), 1.0)
        cb = dh
        d = 1
        while d < ts:
            an = jnp.where(row < ts - d, pltpu.roll(ca, ts - d, 0), 1.0)
            bn = jnp.where(row < ts - d, pltpu.roll(cb, ts - d, 0), 0.0)
            cb = cb + ca * bn
            ca = ca * an
            d *= 2
        lamb = cb + ca * carry[0:1, :]
        carry[0:1, :] = a[0:1, :] * lamb[0:1, :]
        h_before = _shift_rows(h, hprev, 1)
        da = lamb * h_before
        ix = ig * xc
        dmult = lamb * ix
        dig = lamb * mult * xc
        dxc = lamb * mult * ig
        dla = (da - dmult * a / mult) * a
        dr = dla * LRU_C * ls
        dlam_ref[...] += jnp.sum(dla * LRU_C * r, axis=0, keepdims=True) * _sigmoid(-lam_ref[...])
        dpa = dr * r * (1.0 - r)
        dpx = dig * ig * (1.0 - ig)
        dba_ref[...] += jnp.sum(dpa, axis=0, keepdims=True)
        dbx_ref[...] += jnp.sum(dpx, axis=0, keepdims=True)
        dpab = dpa.astype(BF16)
        dpxb = dpx.astype(BF16)
        xcb = xc.astype(BF16)
        dxc = dxc + _mm_nt(dpab, wa_ref[...].astype(BF16)) + _mm_nt(dpxb, wx_ref[...].astype(BF16))
        dwa_ref[...] += _mm_tn(xcb, dpab)
        dwx_ref[...] += _mm_tn(xcb, dpxb)
        dcb_ref[...] += jnp.sum(dxc, axis=0, keepdims=True)
        nxt = dxc_next[...]
        dx = cw_ref[3:4, :] * dxc
        dcw_ref[3:4, :] += jnp.sum(dxc * x, axis=0, keepdims=True)
        for j in (1, 2, 3):
            dx = dx + cw_ref[3 - j:4 - j, :] * _shift_rows_up(dxc, nxt, j)
            dcw_ref[3 - j:4 - j, :] += jnp.sum(dxc * _shift_rows(x, xprev, j), axis=0, keepdims=True)
        dxc_next[...] = dxc
        dhl_ref[:, 0:w] = dx.astype(BF16)
        dhl_ref[:, w:2 * w] = dgate.astype(BF16)

    cst = lambda shape: pl.BlockSpec(shape, lambda i: (0, 0))
    rowr = lambda width: pl.BlockSpec((ts, width), lambda i: (rev(i), 0))
    prevr = lambda width: pl.BlockSpec((ts, width), lambda i: (jnp.maximum(rev(i) - 1, 0), 0))
    return pl.pallas_call(
        body, name="lru_bwd", grid=(nb,),
        in_specs=[rowr(2 * w), prevr(w), cst((4, w)), cst((1, w)), cst((w, w)), cst((1, w)), cst((w, w)), cst((1, w)), cst((1, w)),
                  rowr(w), rowr(w), prevr(w), rowr(w)],
        out_specs=[rowr(2 * w), cst((4, w)), cst((1, w)), cst((w, w)), cst((1, w)), cst((w, w)), cst((1, w)), cst((1, w))],
        out_shape=[jax.ShapeDtypeStruct((s, 2 * w), BF16), jax.ShapeDtypeStruct((4, w), F32), jax.ShapeDtypeStruct((1, w), F32),
                   jax.ShapeDtypeStruct((w, w), F32), jax.ShapeDtypeStruct((1, w), F32), jax.ShapeDtypeStruct((w, w), F32),
                   jax.ShapeDtypeStruct((1, w), F32), jax.ShapeDtypeStruct((1, w), F32)],
        scratch_shapes=[pltpu.VMEM((8, w), F32), pltpu.VMEM((ts, w), F32)],
        compiler_params=_params("arbitrary"),
    )(hl, hl, conv_w, conv_b, w_a, b_a, w_x, b_x, lam, xc_saved, h_saved, h_saved, dout)


GDN_QKV = 3 * GDN_W


def _tri_inverse(nm):
    r = lax.broadcasted_iota(jnp.int32, nm.shape, 0)
    c = lax.broadcasted_iota(jnp.int32, nm.shape, 1)
    t = (r == c).astype(F32) - nm
    p = nm
    for _ in range(5):
        p = _mm3(p, p)
        t = t + _mm3(t, p)
    return t


def _tri_inverse_many(nms):
    r = lax.broadcasted_iota(jnp.int32, nms[0].shape, 0)
    c = lax.broadcasted_iota(jnp.int32, nms[0].shape, 1)
    eye = (r == c).astype(F32)
    ts = [eye - nm for nm in nms]
    ps = list(nms)
    for _ in range(5):
        ps = [_mm3(p, p) for p in ps]
        ts = [t + _mm3(t, p) for t, p in zip(ts, ps)]
    return ts


def _gdn_front(hx_ref, hprev, cw_ref, al_ref, dt_ref):
    w = GDN_W
    x = hx_ref[:, 0:GDN_QKV]
    y = cw_ref[3:4, :] * x
    for j in (1, 2, 3):
        y = y + cw_ref[3 - j:4 - j, :] * _shift_rows(x, hprev, j)
    qkv, dsil = _silu_and_grad(y)
    q, k, v = qkv[:, 0:w], qkv[:, w:2 * w], qkv[:, 2 * w:3 * w]
    rq = lax.rsqrt(_group_sum(q * q) + 1e-6)
    rk = lax.rsqrt(_group_sum(k * k) + 1e-6)
    beta = _sigmoid(hx_ref[:, 5 * w:6 * w])
    sp_in = hx_ref[:, 4 * w:5 * w] + dt_ref[...]
    neg_a = -jnp.exp(al_ref[...])
    g = neg_a * _softplus(sp_in)
    gc = _rows_prefix_sum(g)
    return dict(x=x, dsil=dsil, qn=q * rq, kn=k * rk, v=v, rq=rq, rk=rk, beta=beta, sp_in=sp_in, neg_a=neg_a, g=g, gc=gc)


def _gdn_head(fr, hd, tri):
    lower, strict = tri
    hs = lambda a: a[:, HEAD * hd:HEAD * (hd + 1)]
    k = hs(fr["kn"])
    q = hs(fr["qn"]) * (HEAD ** -0.5)
    v = hs(fr["v"])
    beta = hs(fr["beta"])
    gc = hs(fr["gc"])
    e = jnp.exp(gc)
    gl = gc[CHUNK - 1:CHUNK, :]
    xt = jnp.exp(gl - gc)
    dec = jnp.where(lower, jnp.exp(jnp.minimum(gc - gc.T, 0.0)), 0.0)
    kk = _mm_nt(k, k)
    qkr = _mm_nt(q, k)
    return dict(k=k, q=q, v=v, beta=beta, e=e, egl=jnp.exp(gl), xt=xt, dec=dec, kk=kk, qkr=qkr,
                nm=jnp.where(strict, beta * kk * dec, 0.0))


def _tri_masks():
    r = lax.broadcasted_iota(jnp.int32, (CHUNK, CHUNK), 0)
    c = lax.broadcasted_iota(jnp.int32, (CHUNK, CHUNK), 1)
    return r >= c, r > c


def gdn_fwd(hx, conv_w, a_log_e, dt_bias_e, norm_g_e):
    s = hx.shape[0]
    n_chunks = s // CHUNK
    w = GDN_W

    def body(hx_ref, hp_ref, cw_ref, al_ref, dt_ref, ng_ref, o_ref, opre_ref, t_ref, st_ref, state):
        n = pl.program_id(0)

        @pl.when(n == 0)
        def _():
            state[...] = jnp.zeros_like(state)

        fr = _gdn_front(hx_ref, hp_ref[...] * (n > 0).astype(F32), cw_ref, al_ref, dt_ref)
        tri = _tri_masks()
        st_all = state[...]
        st_ref[...] = st_all
        heads = [_gdn_head(fr, hd, tri) for hd in range(GDN_HEADS)]
        ts = _tri_inverse_many([hq["nm"] for hq in heads])
        outs, new_states = [], []
        for hd, (hq, t) in enumerate(zip(heads, ts)):
            u = _mm(t, hq["v"] * hq["beta"])
            wk = _mm(t, hq["k"] * (hq["beta"] * hq["e"]))
            st = st_all[:, HEAD * hd:HEAD * (hd + 1)]
            vnew = u - _mm(wk, st)
            outs.append(_mm(hq["q"] * hq["e"], st) + _mm(hq["qkr"] * hq["dec"], vnew))
            new_states.append(st * hq["egl"] + _mm_tn(hq["k"] * hq["xt"], vnew))
        t_ref[...] = jnp.concatenate(ts, axis=1)
        state[...] = jnp.concatenate(new_states, axis=1)
        o = jnp.concatenate(outs, axis=1)
        opre_ref[...] = o
        rinv = lax.rsqrt(_group_sum(o * o) * (1.0 / HEAD) + 1e-6)
        z = hx_ref[:, 3 * w:4 * w]
        o_ref[...] = (o * rinv) * ng_ref[...] * (z * _sigmoid(z))

    cst = lambda shape: pl.BlockSpec(shape, lambda i: (0, 0))
    row = lambda width: pl.BlockSpec((CHUNK, width), lambda i: (i, 0))
    return pl.pallas_call(
        body, name="gdn_fwd", grid=(n_chunks,),
        in_specs=[row(GDN_IN), pl.BlockSpec((CHUNK, GDN_QKV), lambda i: (jnp.maximum(i - 1, 0), 0)),
                  cst((4, GDN_QKV)), cst((1, w)), cst((1, w)), cst((1, w))],
        out_specs=[row(w)] * 4,
        out_shape=[jax.ShapeDtypeStruct((s, w), F32)] * 4,
        scratch_shapes=[pltpu.VMEM((CHUNK, w), F32)],
        compiler_params=_params("arbitrary"),
    )(hx, hx, conv_w, a_log_e, dt_bias_e, norm_g_e)


def gdn_bwd(hx, conv_w, a_log_e, dt_bias_e, norm_g_e, opre, tmat, states, dout):
    s = hx.shape[0]
    n_chunks = s // CHUNK
    w = GDN_W
    rev = lambda i: n_chunks - 1 - i

    def body(hx_ref, hp_ref, cw_ref, al_ref, dt_ref, ng_ref, opre_ref, t_ref, st_ref, do_ref,
             dhx_ref, dab_ref, dcw_ref, dal_ref, ddt_ref, dng_ref, dstate, dy_next):
        i = pl.program_id(0)
        n = n_chunks - 1 - i

        @pl.when(i == 0)
        def _():
            dstate[...] = jnp.zeros_like(dstate)
            dy_next[...] = jnp.zeros_like(dy_next)
            for ref in (dcw_ref, dal_ref, ddt_ref, dng_ref):
                ref[...] = jnp.zeros_like(ref)

        hprev = hp_ref[...] * (n > 0).astype(F32)
        fr = _gdn_front(hx_ref, hprev, cw_ref, al_ref, dt_ref)
        tri = _tri_masks()
        lower, strict = tri
        o = opre_ref[...]
        rinv = lax.rsqrt(_group_sum(o * o) * (1.0 / HEAD) + 1e-6)
        yn = o * rinv
        z = hx_ref[:, 3 * w:4 * w]
        sil, dsil_z = _silu_and_grad(z)
        dout_v = do_ref[...]
        ng = ng_ref[...]
        dng_ref[...] += jnp.sum(dout_v * yn * sil, axis=0, keepdims=True)
        dz = dout_v * yn * ng * dsil_z
        dyn = dout_v * ng * sil
        d_o = rinv * (dyn - yn * (_group_sum(dyn * yn) * (1.0 / HEAD)))
        last_row = (lax.broadcasted_iota(jnp.int32, (CHUNK, HEAD), 0) == CHUNK - 1).astype(F32)
        rowsum = lambda m: jnp.sum(m, axis=1, keepdims=True)
        t_all, st_all, dsn_all = t_ref[...], st_ref[...], dstate[...]
        new_ds, dq_l, dk_l, dv_l, dbeta_l, dgc_l = [], [], [], [], [], []
        for hd in range(GDN_HEADS):
            win = slice(HEAD * hd, HEAD * (hd + 1))
            hq = _gdn_head(fr, hd, tri)
            k, q, v, beta, e, xt, dec, kk, qkr = (hq[n_] for n_ in ("k", "q", "v", "beta", "e", "xt", "dec", "kk", "qkr"))
            t = t_all[:, win]
            st = st_all[:, win]
            dsn = dsn_all[:, win]
            do_h = d_o[:, win]
            u = _mm(t, v * beta)
            wk = _mm(t, k * (beta * e))
            vnew = u - _mm(wk, st)
            qk = qkr * dec
            kt = k * xt
            dqd = _mm_nt(do_h, st)
            ds = _mm_tn(q * e, do_h)
            dqk = _mm_nt(do_h, vnew)
            dvnew = _mm_tn(qk, do_h) + _mm(kt, dsn)
            dkt = _mm_nt(vnew, dsn)
            ds = ds + hq["egl"] * dsn
            dgl = jnp.sum(rowsum(dsn * st), axis=0, keepdims=True) * hq["egl"]
            dwk = -_mm_nt(dvnew, st)
            ds = ds - _mm_tn(wk, dvnew)
            drv = _mm_tn(t, dvnew)
            drk = _mm_tn(t, dwk)
            dnm = jnp.where(strict, -(_mm_nt(drv, u) + _mm_nt(drk, wk)), 0.0)
            dbeta = rowsum(dnm * kk * dec)
            dkk = dnm * beta * dec
            ddec = dnm * beta * kk + dqk * qkr
            mq = dqk * dec
            dq = _mm(mq, k) + dqd * e
            dk = _mm_tn(mq, q) + _mm(dkk, k) + _mm_tn(dkk, k) + drk * (beta * e) + dkt * xt
            dv_l.append(drv * beta)
            rks = rowsum(drk * k)
            dbeta = dbeta + rowsum(drv * v) + rks * e
            de = rks * beta + rowsum(dqd * q)
            dxt = rowsum(dkt * k) * xt
            dgl = dgl + jnp.sum(dxt, axis=0, keepdims=True)
            dd = ddec * dec
            dgc = de * e - dxt + rowsum(dd) - rowsum(dd.T) + last_row * dgl
            new_ds.append(ds)
            dq_l.append(dq * (HEAD ** -0.5))
            dk_l.append(dk)
            dbeta_l.append(dbeta + jnp.zeros((CHUNK, HEAD), F32))
            dgc_l.append(dgc)
        dstate[...] = jnp.concatenate(new_ds, axis=1)
        dg = _rows_suffix_sum(jnp.concatenate(dgc_l, axis=1))
        dal_ref[...] += jnp.sum(dg * fr["g"], axis=0, keepdims=True)
        da = dg * fr["neg_a"] * _sigmoid(fr["sp_in"])
        ddt_ref[...] += jnp.sum(da, axis=0, keepdims=True)
        beta_all = fr["beta"]
        db = jnp.concatenate(dbeta_l, axis=1) * beta_all * (1.0 - beta_all)
        lane = lax.broadcasted_iota(jnp.int32, (CHUNK, LANES), 1)
        dab = jnp.zeros((CHUNK, LANES), F32)
        for hd in range(GDN_HEADS):
            dab = jnp.where(lane == hd, da[:, HEAD * hd:HEAD * hd + 1], dab)
            dab = jnp.where(lane == GDN_HEADS + hd, db[:, HEAD * hd:HEAD * hd + 1], dab)
        dab_ref[...] = dab.astype(BF16)
        dqn = jnp.concatenate(dq_l, axis=1)
        dkn = jnp.concatenate(dk_l, axis=1)
        dq_raw = fr["rq"] * (dqn - fr["qn"] * _group_sum(dqn * fr["qn"]))
        dk_raw = fr["rk"] * (dkn - fr["kn"] * _group_sum(dkn * fr["kn"]))
        dy = jnp.concatenate([dq_raw, dk_raw] + dv_l, axis=1) * fr["dsil"]
        nxt = dy_next[...]
        x = fr["x"]
        dx = cw_ref[3:4, :] * dy
        dcw_ref[3:4, :] += jnp.sum(dy * x, axis=0, keepdims=True)
        for j in (1, 2, 3):
            dx = dx + cw_ref[3 - j:4 - j, :] * _shift_rows_up(dy, nxt, j)
            dcw_ref[3 - j:4 - j, :] += jnp.sum(dy * _shift_rows(x, hprev, j), axis=0, keepdims=True)
        dy_next[...] = dy
        dhx_ref[:, 0:GDN_QKV] = dx.astype(BF16)
        dhx_ref[:, 3 * w:4 * w] = dz.astype(BF16)

    cst = lambda shape: pl.BlockSpec(shape, lambda i: (0, 0))
    row = lambda width: pl.BlockSpec((CHUNK, width), lambda i: (rev(i), 0))
    buf = lambda width: pltpu.VMEM((CHUNK, width), F32)
    return pl.pallas_call(
        body, name="gdn_bwd", grid=(n_chunks,),
        in_specs=[row(GDN_IN), pl.BlockSpec((CHUNK, GDN_QKV), lambda i: (jnp.maximum(rev(i) - 1, 0), 0)),
                  cst((4, GDN_QKV)), cst((1, w)), cst((1, w)), cst((1, w)), row(w), row(w), row(w), row(w)],
        out_specs=[row(4 * w), row(LANES), cst((4, GDN_QKV)), cst((1, w)), cst((1, w)), cst((1, w))],
        out_shape=[jax.ShapeDtypeStruct((s, 4 * w), BF16), jax.ShapeDtypeStruct((s, LANES), BF16),
                   jax.ShapeDtypeStruct((4, GDN_QKV), F32),
                   jax.ShapeDtypeStruct((1, w), F32), jax.ShapeDtypeStruct((1, w), F32), jax.ShapeDtypeStruct((1, w), F32)],
        scratch_shapes=[buf(w), buf(GDN_QKV)],
        compiler_params=_params("arbitrary"),
    )(hx, hx, conv_w, a_log_e, dt_bias_e, norm_g_e, opre, tmat, states, dout)


_MESH = pl.DeviceIdType.MESH


def all_gather8(x, name):
    m, n = x.shape

    def body(x_ref, out_ref, send_sems, recv_sems, local_sem):
        px, py, pc = lax.axis_index("x"), lax.axis_index("y"), lax.axis_index("c")
        me, sibling = (px, py, pc), (px, py, 1 - pc)
        chips = [(1 - px, py), (px, 1 - py), (1 - px, 1 - py)]

        def slot(dx, dy, dc):
            return out_ref.at[4 * dx + 2 * dy + dc]

        def copy(k, block, to, src=None):
            return pltpu.make_async_remote_copy(
                src_ref=slot(*block) if src is None else src, dst_ref=slot(*block),
                send_sem=send_sems.at[k], recv_sem=recv_sems.at[k], device_id=to, device_id_type=_MESH)

        mine = pltpu.make_async_copy(x_ref, slot(*me), local_sem)
        mine.start()
        first = [copy(0, me, sibling, src=x_ref)]
        first += [copy(1 + j, me, (*chip, pc), src=x_ref) for j, chip in enumerate(chips)]
        for cp in first:
            cp.start()
        passed = [copy(4 + j, (*chip, pc), sibling) for j, chip in enumerate(chips)]
        for j, chip in enumerate(chips):
            copy(1 + j, (*chip, pc), me).wait_recv()
            passed[j].start()
        copy(0, sibling, me).wait_recv()
        for j, chip in enumerate(chips):
            copy(4 + j, (*chip, 1 - pc), me).wait_recv()
        for cp in first + passed:
            cp.wait_send()
        mine.wait()

    return pl.pallas_call(
        body, name=name, out_shape=jax.ShapeDtypeStruct((N_DEV, m, n), x.dtype),
        in_specs=[_ANY], out_specs=_ANY,
        scratch_shapes=[pltpu.SemaphoreType.DMA((7,)), pltpu.SemaphoreType.DMA((7,)), pltpu.SemaphoreType.DMA],
    )(x)


def gather_layer_weights(shards0, shards1, name):
    n = len(shards0)

    def body(*refs):
        s0, s1 = refs[0:n], refs[n:2 * n]
        f0, f1 = refs[2 * n:3 * n], refs[3 * n:4 * n]
        own_send, own_recv, ici_send, ici_recv, fwd_send, fwd_recv = refs[4 * n:]
        px, py, pc = lax.axis_index("x"), lax.axis_index("y"), lax.axis_index("c")
        mine = 2 * px + py
        sibling = (px, py, 1 - pc)
        chips = [(1 - px, py), (px, 1 - py), (1 - px, 1 - py)]

        def copy(src, dst, sems_s, sems_r, k, to):
            return pltpu.make_async_remote_copy(src_ref=src, dst_ref=dst, send_sem=sems_s.at[k], recv_sem=sems_r.at[k],
                                                device_id=to, device_id_type=_MESH)

        def run(my_shards, my_full, other_full):
            own = []
            for li, (shards, full) in enumerate(((s0, f0), (s1, f1))):
                for i in range(n):
                    own.append(copy(shards[i], full[i].at[mine], own_send, own_recv, li * n + i, sibling))
            ici = []
            for i in range(n):
                for j, (cx, cy) in enumerate(chips):
                    ici.append(copy(my_shards[i], my_full[i].at[mine], ici_send, ici_recv, 3 * i + j, (cx, cy, pc)))
            for cp in own + ici:
                cp.start()
            fwd = []
            for i in range(n):
                for j, (cx, cy) in enumerate(chips):
                    slot = my_full[i].at[2 * cx + cy]
                    copy(my_shards[i], slot, ici_send, ici_recv, 3 * i + j, (cx, cy, pc)).wait_recv()
                    cp = copy(slot, slot, fwd_send, fwd_recv, 3 * i + j, sibling)
                    cp.start()
                    fwd.append(cp)
            for li, full in enumerate((f0, f1)):
                for i in range(n):
                    copy(s0[i], full[i].at[mine], own_send, own_recv, li * n + i, sibling).wait_recv()
            for i in range(n):
                for j, (cx, cy) in enumerate(chips):
                    slot = other_full[i].at[2 * cx + cy]
                    copy(slot, slot, fwd_send, fwd_recv, 3 * i + j, sibling).wait_recv()
            for cp in own + ici + fwd:
                cp.wait_send()

        @pl.when(pc == 0)
        def _():
            run(s0, f0, f1)

        @pl.when(pc == 1)
        def _():
            run(s1, f1, f0)

    full_shapes = [jax.ShapeDtypeStruct((N_CHIPS,) + v.shape, v.dtype) for v in shards0]
    dma = pltpu.SemaphoreType.DMA
    outs = pl.pallas_call(
        body, name=name, out_shape=full_shapes + full_shapes, in_specs=[_ANY] * (2 * n), out_specs=[_ANY] * (2 * n),
        scratch_shapes=[dma((2 * n,)), dma((2 * n,)), dma((3 * n,)), dma((3 * n,)), dma((3 * n,)), dma((3 * n,))],
    )(*shards0, *shards1)
    return outs[0:n], outs[n:2 * n]


def sibling_swap_layers(arrays0, arrays1, name):
    n = len(arrays0)

    def body(*refs):
        a0, a1, outs = refs[0:n], refs[n:2 * n], refs[2 * n:3 * n]
        send_sems, recv_sems = refs[3 * n:]
        px, py, pc = lax.axis_index("x"), lax.axis_index("y"), lax.axis_index("c")

        def run(send):
            cps = [pltpu.make_async_remote_copy(
                src_ref=send[i], dst_ref=outs[i], send_sem=send_sems.at[i], recv_sem=recv_sems.at[i],
                device_id=(px, py, 1 - pc), device_id_type=_MESH) for i in range(n)]
            for cp in cps:
                cp.start()
            for cp in cps:
                cp.wait()

        @pl.when(pc == 0)
        def _():
            run(a1)

        @pl.when(pc == 1)
        def _():
            run(a0)

    return pl.pallas_call(
        body, name=name, out_shape=[jax.ShapeDtypeStruct(v.shape, v.dtype) for v in arrays0],
        in_specs=[_ANY] * (2 * n), out_specs=[_ANY] * n,
        scratch_shapes=[pltpu.SemaphoreType.DMA((n,)), pltpu.SemaphoreType.DMA((n,))],
    )(*arrays0, *arrays1)


def chip_exchange(arrays, pieces, name):
    n = len(pieces)
    offs = [0]
    for r, _ in pieces:
        offs.append(offs[-1] + r)

    def body(*refs):
        srcs = refs[0:n]
        q_ref, send_sems, recv_sems, local_sems = refs[n:]
        px, py, pc = lax.axis_index("x"), lax.axis_index("y"), lax.axis_index("c")
        mine = 2 * px + py
        chips = [(1 - px, py), (px, 1 - py), (1 - px, 1 - py)]
        locals_, sends = [], []
        for i, (r, stride) in enumerate(pieces):
            dst = pl.ds(offs[i], r)
            lc = pltpu.make_async_copy(srcs[i].at[pl.ds(mine * stride, r)], q_ref.at[mine, dst], local_sems.at[i])
            lc.start()
            locals_.append(lc)
            for j, (cx, cy) in enumerate(chips):
                cp = pltpu.make_async_remote_copy(
                    src_ref=srcs[i].at[pl.ds((2 * cx + cy) * stride, r)], dst_ref=q_ref.at[mine, dst],
                    send_sem=send_sems.at[3 * i + j], recv_sem=recv_sems.at[3 * i + j], device_id=(cx, cy, pc), device_id_type=_MESH)
                cp.start()
                sends.append(cp)
        for i, (r, stride) in enumerate(pieces):
            dst = pl.ds(offs[i], r)
            for j, (cx, cy) in enumerate(chips):
                pltpu.make_async_remote_copy(
                    src_ref=srcs[i].at[pl.ds(mine * stride, r)], dst_ref=q_ref.at[2 * cx + cy, dst],
                    send_sem=send_sems.at[3 * i + j], recv_sem=recv_sems.at[3 * i + j], device_id=(cx, cy, pc),
                    device_id_type=_MESH).wait_recv()
        for cp in sends:
            cp.wait_send()
        for lc in locals_:
            lc.wait()

    return pl.pallas_call(
        body, name=name, out_shape=jax.ShapeDtypeStruct((N_CHIPS, offs[-1], arrays[0].shape[1]), arrays[0].dtype),
        in_specs=[_ANY] * n, out_specs=_ANY,
        scratch_shapes=[pltpu.SemaphoreType.DMA((3 * n,)), pltpu.SemaphoreType.DMA((3 * n,)), pltpu.SemaphoreType.DMA((n,))],
    )(*arrays)


def sibling_swap(x, name):
    def body(x_ref, out_ref, send_sem, recv_sem):
        px, py, pc = lax.axis_index("x"), lax.axis_index("y"), lax.axis_index("c")
        cp = pltpu.make_async_remote_copy(
            src_ref=x_ref, dst_ref=out_ref, send_sem=send_sem, recv_sem=recv_sem,
            device_id=(px, py, 1 - pc), device_id_type=_MESH)
        cp.start()
        cp.wait()

    return pl.pallas_call(
        body, name=name, out_shape=jax.ShapeDtypeStruct(x.shape, x.dtype), in_specs=[_ANY], out_specs=_ANY,
        scratch_shapes=[pltpu.SemaphoreType.DMA, pltpu.SemaphoreType.DMA],
    )(x)


ELT_TILE = 128


def _elt_rows(m):
    for t in (512, 256, ELT_TILE, 16, 8):
        if m % t == 0:
            return t
    return m


def pair_add(a0, a1, b, name):
    m, n = b.shape
    tm = _elt_rows(m)

    def body(a0_ref, a1_ref, b_ref, o_ref):
        mine = jnp.where(lax.axis_index("c") == 0, a0_ref[...], a1_ref[...])
        o_ref[...] = (mine.astype(F32) + b_ref[...].astype(F32)).astype(o_ref.dtype)

    return pl.pallas_call(
        body, name=name, grid=(m // tm,), in_specs=[_row_spec(tm, n)] * 3, out_specs=_row_spec(tm, n),
        out_shape=jax.ShapeDtypeStruct((m, n), b.dtype), compiler_params=_params("arbitrary"),
    )(a0, a1, b)


def sum_leading(q, name):
    kk, m, n = q.shape
    tm = _elt_rows(m)

    def body(q_ref, o_ref):
        acc = q_ref[0].astype(F32)
        for i in range(1, kk):
            acc = acc + q_ref[i].astype(F32)
        o_ref[...] = acc

    return pl.pallas_call(
        body, name=name, grid=(m // tm,), in_specs=[pl.BlockSpec((kk, tm, n), lambda i: (0, i, 0))],
        out_specs=_row_spec(tm, n), out_shape=jax.ShapeDtypeStruct((m, n), F32), compiler_params=_params("arbitrary"),
    )(q)


def adamw(w, g, m, v, name):
    rows, cols = w.shape
    tm = _elt_rows(rows)

    def body(w_ref, g_ref, m_ref, v_ref, d_ref, nm_ref, nv_ref):
        gv = g_ref[...]
        nm = ADAM_B1 * m_ref[...] + (1.0 - ADAM_B1) * gv
        nv = ADAM_B2 * v_ref[...] + (1.0 - ADAM_B2) * jnp.square(gv)
        nm_ref[...] = nm
        nv_ref[...] = nv
        m_hat = nm / (1.0 - ADAM_B1 ** ADAM_STEP)
        v_hat = nv / (1.0 - ADAM_B2 ** ADAM_STEP)
        d_ref[...] = -ADAM_LR * (m_hat / (jnp.sqrt(v_hat) + ADAM_EPS) + ADAM_WD * w_ref[...])

    spec = _row_spec(tm, cols)
    return pl.pallas_call(
        body, name=name, grid=(rows // tm,), in_specs=[spec] * 4, out_specs=[spec] * 3,
        out_shape=[jax.ShapeDtypeStruct((rows, cols), F32)] * 3, compiler_params=_params("arbitrary"),
    )(w, g, m, v)


def _block_diag_dense(w):
    g = w.shape[0]
    return jnp.einsum("gij,gh->gihj", w, jnp.eye(g, dtype=w.dtype)).reshape(g * w.shape[1], g * w.shape[2])


def _diag_blocks(m):
    return jnp.stack([m[HEAD * i:HEAD * (i + 1), HEAD * i:HEAD * (i + 1)] for i in range(LRU_BLOCKS)])


def _rep(v):
    return jnp.repeat(v, HEAD, axis=-1)


def _split_w_in(w_in):
    gdn0 = RET_IN + LRU_IN
    gdn1 = gdn0 + 4 * GDN_W
    w_r = w_in[:, 0:RET_IN]
    w_l = w_in[:, RET_IN:gdn0]
    w_g = jnp.concatenate([w_in[:, gdn0:gdn1], _rep(w_in[:, gdn1:gdn1 + GDN_HEADS]), _rep(w_in[:, gdn1 + GDN_HEADS:])], axis=1)
    w_ab = jnp.pad(w_in[:, gdn1:], ((0, 0), (0, LANES - 2 * GDN_HEADS)))
    return w_r, w_l, w_g, w_ab


WIN_SHARD = D_IN // N_CHIPS
WIN_STRIDE = 832
WIN_ROWS = 960
WIN_T_ROWS = WIN_STRIDE * (N_CHIPS - 1) + WIN_ROWS
AB_ROWS = 16

_GRAD_PIECES = (("ffn1_w_gate", 704, 704), ("ffn1_w_up", 704, 704), ("ffn1_w_down", 704, 704), ("w_in", WIN_ROWS, WIN_STRIDE),
                ("w_out", 256, 256), ("ffn2_w_gate", 704, 704), ("ffn2_w_up", 704, 704), ("ffn2_w_down", 704, 704),
                ("ple_w_gate", 256, 256), ("ple_w_proj", 64, 64))
_TRANSPOSED = ("ffn1_w_gate", "ffn1_w_up", "w_in", "ffn2_w_gate", "ffn2_w_up", "ple_w_proj")


def _local_step(x, p, pos, target, wt):
    row = lambda v: v[None, :]
    saved = []
    for i in range(DEPTH):
        w_r, w_l, w_g, w_ab = _split_w_in(wt["w_in"][i])
        lw = dict(
            wg1=wt["ffn1_w_gate"][i], wu1=wt["ffn1_w_up"][i], wd1=wt["ffn1_w_down"][i], w_r=w_r, w_l=w_l, w_g=w_g, w_ab=w_ab,
            w_out=wt["w_out"][i], wg2=wt["ffn2_w_gate"][i], wu2=wt["ffn2_w_up"][i], wd2=wt["ffn2_w_down"][i],
            wpg=wt["ple_w_gate"][i], wpp=wt["ple_w_proj"][i],
            wa=_block_diag_dense(wt["lru_w_a"][i]), wx=_block_diag_dense(wt["lru_w_x"][i]),
            al=row(_rep(wt["gdn_a_log"][i])), dt=row(_rep(wt["gdn_dt_bias"][i])), ng=row(jnp.tile(wt["gdn_norm_g"][i], GDN_HEADS)))
        hg1, hu1, r1, x1 = ffn_fwd(x, row(wt["ln_ffn1_g"][i]), row(wt["ln_ffn1_b"][i]), lw["wg1"], lw["wu1"], lw["wd1"])
        hr, hl, hgd = win_fwd(x1, w_r, w_l, w_g)
        o_r, opre_r, st_r = ret_fwd(hr, pos, row(wt["ret_norm_g"][i]))
        o_l, xc, hs = lru_fwd(hl, wt["lru_conv_w"][i], row(wt["lru_conv_b"][i]), lw["wa"], row(wt["lru_b_a"][i]), lw["wx"],
                              row(wt["lru_b_x"][i]), row(wt["lru_lambda"][i]))
        o_g, opre_g, tmat, st_g = gdn_fwd(hgd, wt["gdn_conv_w"][i], lw["al"], lw["dt"], lw["ng"])
        r2, x2, ocat = out_fwd(o_r, o_l, o_g, x1, lw["w_out"], row(wt["ln_mix_g"][i]), row(wt["ln_mix_b"][i]))
        hg2, hu2, r3, x3, pg, pp = ffn_fwd(x2, row(wt["ln_ffn2_g"][i]), row(wt["ln_ffn2_b"][i]), lw["wg2"], lw["wu2"], lw["wd2"],
                                           ple=(p[i], lw["wpg"], lw["wpp"]))
        saved.append(dict(lw=lw, x0=x, hg1=hg1, hu1=hu1, r1=r1, x1=x1, hr=hr, hl=hl, hgd=hgd, ocat=ocat, opre_r=opre_r, st_r=st_r,
                          xc=xc, hs=hs, opre_g=opre_g, tmat=tmat, st_g=st_g, r2=r2, x2=x2, hg2=hg2, hu2=hu2,
                          r3=r3, pg=pg, pp=pp))
        x = x3

    dx, loss = loss_and_grad(x, target)
    grads = [None] * DEPTH
    big = [None] * DEPTH
    for i in reversed(range(DEPTH)):
        sv = saved[i]
        lw = sv["lw"]
        tag = f"_l{i}"
        dx2, act2, dhg2, dhu2, dy2, dg3, db3, dpg, dpp = ffn_bwd(
            dx, sv["r3"], sv["x2"], sv["hg2"], sv["hu2"], row(wt["ln_ffn2_g"][i]), lw["wg2"], lw["wu2"], lw["wd2"],
            ple=(sv["pg"], sv["pp"], lw["wpg"]))
        g, bg = {}, {}
        bg["ffn2_w_gate"] = wgrad(dhg2, sv["x2"], "wgrad_gate2" + tag)
        bg["ffn2_w_up"] = wgrad(dhu2, sv["x2"], "wgrad_up2" + tag)
        bg["ffn2_w_down"] = wgrad(act2, dy2, "wgrad_down2" + tag)
        bg["ple_w_gate"] = wgrad(sv["x2"], dpg, "wgrad_pgate" + tag)
        bg["ple_w_proj"] = wgrad(dpp, p[i], "wgrad_pproj" + tag).reshape(PLE_DIM, D_MODEL)
        g["ln_ffn2_g"], g["ln_ffn2_b"] = dg3[0], db3[0]
        dr2, dr2b, do_r, do_l, do_g, dg2, db2 = out_bwd(dx2, sv["r2"], row(wt["ln_mix_g"][i]), lw["w_out"])
        g["ln_mix_g"], g["ln_mix_b"] = dg2[0], db2[0]
        bg["w_out"] = wgrad(sv["ocat"], dr2b, "wgrad_out" + tag)
        dhr, dgn = ret_bwd(sv["hr"], pos, row(wt["ret_norm_g"][i]), sv["opre_r"], sv["st_r"], do_r)
        g["ret_norm_g"] = dgn[0]
        dhl, dcw, dcb, dwa, dba, dwx, dbx, dlam = lru_bwd(
            sv["hl"], wt["lru_conv_w"][i], row(wt["lru_conv_b"][i]), lw["wa"], row(wt["lru_b_a"][i]), lw["wx"],
            row(wt["lru_b_x"][i]), row(wt["lru_lambda"][i]), sv["xc"], sv["hs"], do_l)
        g["lru_conv_w"], g["lru_conv_b"] = dcw, dcb[0]
        g["lru_w_a"], g["lru_b_a"], g["lru_w_x"], g["lru_b_x"], g["lru_lambda"] = _diag_blocks(dwa), dba[0], _diag_blocks(dwx), dbx[0], dlam[0]
        dhq, dab, dgcw, dal, ddt, dng = gdn_bwd(sv["hgd"], wt["gdn_conv_w"][i], lw["al"], lw["dt"], lw["ng"], sv["opre_g"],
                                                sv["tmat"], sv["st_g"], do_g)
        g["gdn_conv_w"] = dgcw
        g["gdn_a_log"], g["gdn_dt_bias"] = dal[0, ::HEAD], ddt[0, ::HEAD]
        g["gdn_norm_g"] = dng[0].reshape(GDN_HEADS, HEAD).sum(0)
        dx1 = win_bwd(dr2, dhr, dhl, dhq, dab, lw["w_r"], lw["w_l"], lw["w_g"], lw["w_ab"])
        used = RET_IN + LRU_IN + 4 * GDN_W + AB_ROWS
        bg["w_in"] = jnp.concatenate(
            [wgrad(dhr, sv["x1"], "wgrad_in_r" + tag), wgrad(dhl, sv["x1"], "wgrad_in_l" + tag),
             wgrad(dhq, sv["x1"], "wgrad_in_q" + tag), wgrad(dab, sv["x1"], "wgrad_in_ab" + tag)[0:AB_ROWS],
             jnp.zeros((WIN_T_ROWS - used, D_MODEL), BF16)], axis=0)
        dx, act1, dhg1, dhu1, dy1, dg1, db1 = ffn_bwd(dx1, sv["r1"], sv["x0"], sv["hg1"], sv["hu1"], row(wt["ln_ffn1_g"][i]),
                                                      lw["wg1"], lw["wu1"], lw["wd1"])
        bg["ffn1_w_gate"] = wgrad(dhg1, sv["x0"], "wgrad_gate1" + tag)
        bg["ffn1_w_up"] = wgrad(dhu1, sv["x0"], "wgrad_up1" + tag)
        bg["ffn1_w_down"] = wgrad(act1, dy1, "wgrad_down1" + tag)
        g["ln_ffn1_g"], g["ln_ffn1_b"] = dg1[0], db1[0]
        grads[i] = g
        big[i] = bg
    return loss, dx, {k: jnp.stack([grads[i][k] for i in range(DEPTH)]) for k in grads[0]}, big


def _natural_grad(name, rows):
    if name == "ple_w_proj":
        return rows.reshape(-1, PLE_DIM).T
    return rows.T if name in _TRANSPOSED else rows


_SPLIT = dict(ffn1_w_gate=2, ffn1_w_up=2, ffn1_w_down=1, w_in=2, w_out=1, ffn2_w_gate=2, ffn2_w_up=2, ffn2_w_down=1,
              ple_w_gate=1, ple_w_proj=2)
_CONV = ("lru_conv_w", "gdn_conv_w")
_WHOLE = ("ln_ffn1_g", "ln_ffn1_b", "ret_norm_g", "lru_conv_b", "lru_w_a", "lru_b_a", "lru_w_x", "lru_b_x", "lru_lambda",
          "gdn_a_log", "gdn_dt_bias", "gdn_norm_g", "ln_mix_g", "ln_mix_b", "ln_ffn2_g", "ln_ffn2_b")
_WEIGHTS = ("ln_ffn1_g", "ln_ffn1_b", "ffn1_w_gate", "ffn1_w_up", "ffn1_w_down", "w_in", "ret_norm_g", "lru_conv_w", "lru_conv_b",
            "lru_w_a", "lru_b_a", "lru_w_x", "lru_b_x", "lru_lambda", "gdn_conv_w", "gdn_a_log", "gdn_dt_bias", "gdn_norm_g",
            "w_out", "ln_mix_g", "ln_mix_b", "ffn2_w_gate", "ffn2_w_up", "ffn2_w_down", "ple_w_gate", "ple_w_proj",
            "ln_ffn2_g", "ln_ffn2_b")
_INPUTS = ("x", "p", "positions") + _WEIGHTS + ("loss_target",) + tuple("m_" + n for n in _WEIGHTS) + tuple("v_" + n for n in _WEIGHTS)

BIG_COLS = 1024
SMALL_COLS = LANES
SMALL_ROWS_MULT = 8


def _pack(arrays, dtype, cols, rows_mult):
    flat = jnp.concatenate([a.reshape(-1).astype(dtype) for a in arrays])
    rows = -(-flat.shape[0] // cols)
    rows = -(-rows // rows_mult) * rows_mult
    return jnp.pad(flat, (0, rows * cols - flat.shape[0])).reshape(rows, cols)


def _unpack(packed, shapes):
    flat = packed.reshape(-1)
    out, off = [], 0
    for shp in shapes:
        size = int(np.prod(shp))
        out.append(flat[off:off + size].reshape(shp))
        off += size
    return out


def _as2d(a):
    return a.reshape(-1, a.shape[-1])


def kernel(x, p, positions, ln_ffn1_g, ln_ffn1_b, ffn1_w_gate, ffn1_w_up, ffn1_w_down, w_in, ret_norm_g, lru_conv_w, lru_conv_b, lru_w_a, lru_b_a, lru_w_x, lru_b_x, lru_lambda, gdn_conv_w, gdn_a_log, gdn_dt_bias, gdn_norm_g, w_out, ln_mix_g, ln_mix_b, ffn2_w_gate, ffn2_w_up, ffn2_w_down, ple_w_gate, ple_w_proj, ln_ffn2_g, ln_ffn2_b, loss_target, m_ln_ffn1_g, m_ln_ffn1_b, m_ffn1_w_gate, m_ffn1_w_up, m_ffn1_w_down, m_w_in, m_ret_norm_g, m_lru_conv_w, m_lru_conv_b, m_lru_w_a, m_lru_b_a, m_lru_w_x, m_lru_b_x, m_lru_lambda, m_gdn_conv_w, m_gdn_a_log, m_gdn_dt_bias, m_gdn_norm_g, m_w_out, m_ln_mix_g, m_ln_mix_b, m_ffn2_w_gate, m_ffn2_w_up, m_ffn2_w_down, m_ple_w_gate, m_ple_w_proj, m_ln_ffn2_g, m_ln_ffn2_b, v_ln_ffn1_g, v_ln_ffn1_b, v_ffn1_w_gate, v_ffn1_w_up, v_ffn1_w_down, v_w_in, v_ret_norm_g, v_lru_conv_w, v_lru_conv_b, v_lru_w_a, v_lru_b_a, v_lru_w_x, v_lru_b_x, v_lru_lambda, v_gdn_conv_w, v_gdn_a_log, v_gdn_dt_bias, v_gdn_norm_g, v_w_out, v_ln_mix_g, v_ln_mix_b, v_ffn2_w_gate, v_ffn2_w_up, v_ffn2_w_down, v_ple_w_gate, v_ple_w_proj, v_ln_ffn2_g, v_ln_ffn2_b):
    a = dict(zip(_INPUTS, (x, p, positions, ln_ffn1_g, ln_ffn1_b, ffn1_w_gate, ffn1_w_up, ffn1_w_down, w_in, ret_norm_g, lru_conv_w, lru_conv_b, lru_w_a, lru_b_a, lru_w_x, lru_b_x, lru_lambda, gdn_conv_w, gdn_a_log, gdn_dt_bias, gdn_norm_g, w_out, ln_mix_g, ln_mix_b, ffn2_w_gate, ffn2_w_up, ffn2_w_down, ple_w_gate, ple_w_proj, ln_ffn2_g, ln_ffn2_b, loss_target, m_ln_ffn1_g, m_ln_ffn1_b, m_ffn1_w_gate, m_ffn1_w_up, m_ffn1_w_down, m_w_in, m_ret_norm_g, m_lru_conv_w, m_lru_conv_b, m_lru_w_a, m_lru_b_a, m_lru_w_x, m_lru_b_x, m_lru_lambda, m_gdn_conv_w, m_gdn_a_log, m_gdn_dt_bias, m_gdn_norm_g, m_w_out, m_ln_mix_g, m_ln_mix_b, m_ffn2_w_gate, m_ffn2_w_up, m_ffn2_w_down, m_ple_w_gate, m_ple_w_proj, m_ln_ffn2_g, m_ln_ffn2_b, v_ln_ffn1_g, v_ln_ffn1_b, v_ffn1_w_gate, v_ffn1_w_up, v_ffn1_w_down, v_w_in, v_ret_norm_g, v_lru_conv_w, v_lru_conv_b, v_lru_w_a, v_lru_b_a, v_lru_w_x, v_lru_b_x, v_lru_lambda, v_gdn_conv_w, v_gdn_a_log, v_gdn_dt_bias, v_gdn_norm_g, v_w_out, v_ln_mix_g, v_ln_mix_b, v_ffn2_w_gate, v_ffn2_w_up, v_ffn2_w_down, v_ple_w_gate, v_ple_w_proj, v_ln_ffn2_g, v_ln_ffn2_b)))
    assert len(a) == len(_INPUTS)
    core = lax.axis_index("c")
    chip = 2 * lax.axis_index("x") + lax.axis_index("y")
    big = list(_SPLIT)

    full = gather_layer_weights([a[n][0].astype(BF16) for n in big], [a[n][1].astype(BF16) for n in big], "gather_weights")
    wt = {}
    for i, n in enumerate(big):
        wt[n] = [jnp.concatenate([full[layer][i][k] for k in range(N_CHIPS)], axis=_SPLIT[n] - 1) for layer in range(DEPTH)]
    conv_g = all_gather8(_pack([a[n] for n in _CONV], F32, SMALL_COLS, SMALL_ROWS_MULT), "gather_conv_weights")[0::2]
    conv_g = conv_g.reshape(N_CHIPS, -1)
    off = 0
    for n in _CONV:
        shp = a[n].shape
        size = int(np.prod(shp))
        parts = conv_g[:, off:off + size].reshape((N_CHIPS,) + shp)
        wt[n] = jnp.concatenate([parts[k] for k in range(N_CHIPS)], axis=2)
        off += size
    for n in _WHOLE:
        wt[n] = a[n]

    seq = a["x"].shape[1]
    loss_part, dx, grads, big_src = _local_step(a["x"][0], a["p"][:, 0], a["positions"].reshape(seq, 1), a["loss_target"][0], wt)
    loss = lax.psum(loss_part[0, 0], ("x", "y", "c"))

    pieces = [(r, stride) for _, r, stride in _GRAD_PIECES]
    names = [n for n, _, _ in _GRAD_PIECES]
    layer0 = [big_src[0][n] for n in names]
    layer1 = [big_src[1][n] for n in names]
    from_sibling = sibling_swap_layers(layer0, layer1, "reduce_pair_swap")
    pair = [pair_add(u, v, w, "reduce_pair_add_" + n) for n, u, v, w in zip(names, layer0, layer1, from_sibling)]
    arrived = chip_exchange(pair, pieces, "reduce_chip_exchange")
    my_layer_sum = sum_leading(arrived, "reduce_chip_sum")
    other_layer_sum = sibling_swap(my_layer_sum, "reduce_pair_share")
    reduced = [jnp.where(core == layer, my_layer_sum, other_layer_sum) for layer in range(DEPTH)]
    big_grads = {}
    off = 0
    for n, r, _ in _GRAD_PIECES:
        per_layer = []
        for layer in range(DEPTH):
            rows = reduced[layer][off:off + r]
            if n == "w_in":
                rows = lax.dynamic_slice_in_dim(rows, chip * (WIN_SHARD - WIN_STRIDE), WIN_SHARD, axis=0)
            per_layer.append(_natural_grad(n, rows))
        big_grads[n] = jnp.stack(per_layer)
        off += r

    small_names = list(_WHOLE) + list(_CONV)
    small_local = _pack([grads[n] for n in small_names], F32, SMALL_COLS, SMALL_ROWS_MULT)
    small_sum = sum_leading(all_gather8(small_local, "gather_small_grads"), "sum_small_grads")
    small_grads = dict(zip(small_names, _unpack(small_sum, [grads[n].shape for n in small_names])))
    for n in _CONV:
        width = a[n].shape[2]
        small_grads[n] = lax.dynamic_slice_in_dim(small_grads[n], chip * width, width, axis=2)

    new = {}
    for n in big:
        d, nm, nv = adamw(_as2d(a[n]), _as2d(big_grads[n]), _as2d(a["m_" + n]), _as2d(a["v_" + n]), "adamw_" + n)
        new[n] = tuple(t.reshape(a[n].shape) for t in (d, nm, nv))
    pk = lambda prefix: _pack([a[prefix + n] for n in small_names], F32, SMALL_COLS, SMALL_ROWS_MULT)
    pg = _pack([small_grads[n] for n in small_names], F32, SMALL_COLS, SMALL_ROWS_MULT)
    outs = adamw(pk(""), pg, pk("m_"), pk("v_"), "adamw_small")
    shapes = [a[n].shape for n in small_names]
    for n, d, nm, nv in zip(small_names, *[_unpack(o, shapes) for o in outs]):
        new[n] = (d, nm, nv)
    all_grads = {**big_grads, **small_grads}
    return (loss, dx[None], *[all_grads[n] for n in _WEIGHTS], *[new[n][0] for n in _WEIGHTS],
            *[new[n][1] for n in _WEIGHTS], *[new[n][2] for n in _WEIGHTS])
```

```python
import functools
import math

import numpy as np
import jax
import jax.numpy as jnp
from jax import lax
from jax.experimental import pallas as pl
from jax.experimental.pallas import tpu as pltpu

F32 = jnp.float32
BF16 = jnp.bfloat16

D_MODEL = 1024
D_FF = 2816
PLE_DIM = 256
DEPTH = 2
CHUNK = 64
RET_HEADS = 4
RET_W = 256
LRU_W = 384
LRU_BLOCKS = 6
GDN_HEADS = 6
GDN_W = 384
HEAD = 64
D_IN = 3340
RET_IN = 4 * RET_W
LRU_IN = 2 * LRU_W
GDN_IN = 6 * GDN_W
ROPE_THETA = 10000.0
ALPHA = (2 * DEPTH) ** 0.25
LN_EPS = 1e-5
LRU_C = 8.0
N_CHIPS = 4
N_DEV = 8

ADAM_LR = 0.001
ADAM_B1 = 0.9
ADAM_B2 = 0.999
ADAM_EPS = 1e-08
ADAM_WD = 0.01
ADAM_STEP = 10

LANES = 128
VMEM_LIMIT = 56 * 1024 * 1024
ROW_TILE = 256
ROW_TILE_BWD = 512
SCAN_TILE = 256


def _params(*sem):
    return pltpu.CompilerParams(dimension_semantics=sem, vmem_limit_bytes=VMEM_LIMIT)


def _mm(a, b):
    return jnp.dot(a, b, preferred_element_type=F32)


def _mm_nt(a, b):
    return lax.dot_general(a, b, (((1,), (1,)), ((), ())), preferred_element_type=F32)


def _mm_tn(a, b):
    return lax.dot_general(a, b, (((0,), (0,)), ((), ())), preferred_element_type=F32)


def _split(a):
    hi = a.astype(BF16)
    lo = (a - hi.astype(F32)).astype(BF16)
    return hi, lo


def _mm3(a, b):
    ah, al = _split(a)
    bh, bl = _split(b)
    return _mm(ah, bh) + (_mm(ah, bl) + _mm(al, bh))


def _sigmoid(x):
    return jax.nn.sigmoid(x)


def _log1p(u):
    w = 1.0 + u
    return jnp.where(w == 1.0, u, jnp.log(w) * (u / jnp.where(w == 1.0, 1.0, w - 1.0)))


def _expm1(y):
    u = jnp.exp(y)
    um1 = u - 1.0
    safe = jnp.where((u == 1.0) | (um1 == -1.0), 1.0, jnp.log(jnp.where(u == 0.0, 1.0, u)))
    return jnp.where(u == 1.0, y, jnp.where(um1 == -1.0, -1.0, um1 * (y / safe)))


def _softplus(x):
    return jnp.maximum(x, 0.0) + _log1p(jnp.exp(-jnp.abs(x)))


_GELU_C = math.sqrt(2.0 / math.pi)


def _gelu(x):
    return 0.5 * x * (1.0 + jnp.tanh(_GELU_C * (x + 0.044715 * (x * x * x))))


def _gelu_grad(x):
    t = jnp.tanh(_GELU_C * (x + 0.044715 * (x * x * x)))
    return 0.5 * (1.0 + t) + 0.5 * x * (1.0 - t * t) * (_GELU_C * (1.0 + 3.0 * 0.044715 * (x * x)))


def _silu_and_grad(x):
    s = _sigmoid(x)
    return x * s, s * (1.0 + x * (1.0 - s))


def _group_sum_slab(x):
    lane = lax.broadcasted_iota(jnp.int32, x.shape, 1)
    low = jnp.sum(x[:, 0:LANES // 2], axis=1, keepdims=True)
    high = jnp.sum(x[:, LANES // 2:], axis=1, keepdims=True)
    return jnp.where(lane < LANES // 2, low, high)


def _group_sum(x):
    n = x.shape[1] // LANES
    if n == 1:
        return _group_sum_slab(x)
    return jnp.concatenate([_group_sum_slab(x[:, LANES * i:LANES * (i + 1)]) for i in range(n)], axis=1)


def _rows_prefix_sum(x):
    n = x.shape[0]
    row = lax.broadcasted_iota(jnp.int32, x.shape, 0)
    d = 1
    while d < n:
        x = x + jnp.where(row >= d, pltpu.roll(x, d, 0), 0.0)
        d *= 2
    return x


def _rows_suffix_sum(x):
    n = x.shape[0]
    row = lax.broadcasted_iota(jnp.int32, x.shape, 0)
    d = 1
    while d < n:
        x = x + jnp.where(row < n - d, pltpu.roll(x, n - d, 0), 0.0)
        d *= 2
    return x


def _shift_rows(cur, prev, j):
    row = lax.broadcasted_iota(jnp.int32, cur.shape, 0)
    return jnp.where(row < j, pltpu.roll(prev, j, 0), pltpu.roll(cur, j, 0))


def _shift_rows_up(cur, nxt, j):
    n = cur.shape[0]
    row = lax.broadcasted_iota(jnp.int32, cur.shape, 0)
    return jnp.where(row < n - j, pltpu.roll(cur, n - j, 0), pltpu.roll(nxt, n - j, 0))


def _layer_norm_stats(r):
    mu = jnp.mean(r, axis=-1, keepdims=True)
    d = r - mu
    var = jnp.mean(d * d, axis=-1, keepdims=True)
    rstd = lax.rsqrt(var + LN_EPS)
    return d * rstd, rstd


def _load_resident(step, pairs, sems):
    @pl.when(step == 0)
    def _():
        cps = [pltpu.make_async_copy(h, v, sems.at[i]) for i, (h, v) in enumerate(pairs)]
        for c in cps:
            c.start()
        for c in cps:
            c.wait()


def _row_spec(tile, width):
    return pl.BlockSpec((tile, width), lambda i: (i, 0))


def _full_spec(shape):
    nd = len(shape)
    return pl.BlockSpec(shape, lambda i: (0,) * nd)


_ANY = pl.BlockSpec(memory_space=pl.ANY)


def ffn_fwd(x, ln_g, ln_b, w_gate, w_up, w_down, ple=None):
    s = x.shape[0]
    tm = ROW_TILE
    with_ple = ple is not None
    weights = [w_gate, w_up, w_down] + ([ple[1], ple[2]] if with_ple else [])

    def body(*refs):
        it = iter(refs)
        x_ref, g_ref, b_ref = next(it), next(it), next(it)
        p_ref = next(it) if with_ple else None
        w_hbm = [next(it) for _ in weights]
        hg_ref, hu_ref, r_ref, xn_ref, xnb_ref = next(it), next(it), next(it), next(it), next(it)
        pg_ref, pp_ref = (next(it), next(it)) if with_ple else (None, None)
        w_vm = [next(it) for _ in weights]
        sems = next(it)
        _load_resident(pl.program_id(0), list(zip(w_hbm, w_vm)), sems)
        xv = x_ref[...]
        xb = xv.astype(BF16)
        hg = _mm(xb, w_vm[0][...])
        hu = _mm(xb, w_vm[1][...])
        hg_ref[...] = hg
        hu_ref[...] = hu
        act = (hg * _sigmoid(hg)) * hu
        r = ALPHA * xv + 0.5 * _mm(act.astype(BF16), w_vm[2][...])
        if with_ple:
            pg = _mm(xb, w_vm[3][...])
            pp = _mm(p_ref[...].astype(BF16), w_vm[4][...])
            pg_ref[...] = pg
            pp_ref[...] = pp
            r = r + _sigmoid(pg) * pp
        r_ref[...] = r
        xhat, _ = _layer_norm_stats(r)
        xn = xhat * g_ref[...] + b_ref[...]
        xn_ref[...] = xn
        xnb_ref[...] = xn.astype(BF16)

    d, f = D_MODEL, D_FF
    in_specs = [_row_spec(tm, d), _full_spec((1, d)), _full_spec((1, d))]
    args = [x, ln_g, ln_b]
    if with_ple:
        in_specs.append(_row_spec(tm, PLE_DIM))
        args.append(ple[0])
    in_specs += [_ANY] * len(weights)
    args += weights
    out_shape = [jax.ShapeDtypeStruct((s, f), F32), jax.ShapeDtypeStruct((s, f), F32),
                 jax.ShapeDtypeStruct((s, d), F32), jax.ShapeDtypeStruct((s, d), F32), jax.ShapeDtypeStruct((s, d), BF16)]
    out_specs = [_row_spec(tm, f), _row_spec(tm, f), _row_spec(tm, d), _row_spec(tm, d), _row_spec(tm, d)]
    if with_ple:
        out_shape += [jax.ShapeDtypeStruct((s, d), F32)] * 2
        out_specs += [_row_spec(tm, d)] * 2
    scratch = [pltpu.VMEM(w.shape, w.dtype) for w in weights] + [pltpu.SemaphoreType.DMA((len(weights),))]
    return pl.pallas_call(
        body, name="ffn_fwd_ple" if with_ple else "ffn_fwd", grid=(s // tm,), in_specs=in_specs, out_specs=out_specs,
        out_shape=out_shape, scratch_shapes=scratch, compiler_params=_params("arbitrary"),
    )(*args)


def ffn_bwd(dxn, r, x, hg, hu, ln_g, w_gate, w_up, w_down, ple=None):
    s = x.shape[0]
    with_ple = ple is not None
    d, f = D_MODEL, D_FF
    suffix = "_ple" if with_ple else ""

    tm = ROW_TILE

    def body_a(*refs):
        it = iter(refs)
        dxn_ref, r_ref, hg_ref, hu_ref, g_ref = (next(it) for _ in range(5))
        pg_ref, pp_ref = (next(it), next(it)) if with_ple else (None, None)
        wd_hbm = next(it)
        dr_ref, act_ref, dhg_ref, dhu_ref, dy_ref, dg_ref, db_ref = (next(it) for _ in range(7))
        dpg_ref, dpp_ref = (next(it), next(it)) if with_ple else (None, None)
        wd_vm, sems = next(it), next(it)
        step = pl.program_id(0)
        _load_resident(step, [(wd_hbm, wd_vm)], sems)

        @pl.when(step == 0)
        def _():
            dg_ref[...] = jnp.zeros_like(dg_ref)
            db_ref[...] = jnp.zeros_like(db_ref)

        dxn_v = dxn_ref[...]
        xhat, rstd = _layer_norm_stats(r_ref[...])
        dg_ref[...] += jnp.sum(dxn_v * xhat, axis=0, keepdims=True)
        db_ref[...] += jnp.sum(dxn_v, axis=0, keepdims=True)
        dyh = dxn_v * g_ref[...]
        dr = rstd * (dyh - jnp.mean(dyh, axis=-1, keepdims=True) - xhat * jnp.mean(dyh * xhat, axis=-1, keepdims=True))
        dr_ref[...] = dr
        dy = (0.5 * dr).astype(BF16)
        dy_ref[...] = dy
        da = _mm_nt(dy, wd_vm[...])
        hg_v = hg_ref[...]
        hu_v = hu_ref[...]
        sil, dsil = _silu_and_grad(hg_v)
        act_ref[...] = (sil * hu_v).astype(BF16)
        dhu_ref[...] = (da * sil).astype(BF16)
        dhg_ref[...] = (da * hu_v * dsil).astype(BF16)
        if with_ple:
            sp = _sigmoid(pg_ref[...])
            dpp_ref[...] = (dr * sp).astype(BF16)
            dpg_ref[...] = (dr * pp_ref[...] * sp * (1.0 - sp)).astype(BF16)

    in_specs = [_row_spec(tm, d), _row_spec(tm, d), _row_spec(tm, f), _row_spec(tm, f), _full_spec((1, d))]
    args = [dxn, r, hg, hu, ln_g]
    if with_ple:
        in_specs += [_row_spec(tm, d), _row_spec(tm, d)]
        args += [ple[0], ple[1]]
    out_shape = [jax.ShapeDtypeStruct((s, d), F32), jax.ShapeDtypeStruct((s, f), BF16), jax.ShapeDtypeStruct((s, f), BF16),
                 jax.ShapeDtypeStruct((s, f), BF16), jax.ShapeDtypeStruct((s, d), BF16),
                 jax.ShapeDtypeStruct((1, d), F32), jax.ShapeDtypeStruct((1, d), F32)]
    out_specs = [_row_spec(tm, d), _row_spec(tm, f), _row_spec(tm, f), _row_spec(tm, f), _row_spec(tm, d),
                 _full_spec((1, d)), _full_spec((1, d))]
    if with_ple:
        out_shape += [jax.ShapeDtypeStruct((s, d), BF16)] * 2
        out_specs += [_row_spec(tm, d)] * 2
    first = pl.pallas_call(
        body_a, name="ffn_bwd_hidden" + suffix, grid=(s // tm,), in_specs=in_specs + [_ANY], out_specs=out_specs,
        out_shape=out_shape, scratch_shapes=[pltpu.VMEM(w_down.shape, w_down.dtype), pltpu.SemaphoreType.DMA((1,))],
        compiler_params=_params("arbitrary"),
    )(*args, w_down)
    dr, act, dhg, dhu, dy, dg, db = first[:7]

    tb = min(ROW_TILE_BWD, s)
    weights = [w_gate, w_up] + ([ple[2]] if with_ple else [])

    def body_b(*refs):
        it = iter(refs)
        dr_ref, dhg_ref, dhu_ref = next(it), next(it), next(it)
        dpg_ref = next(it) if with_ple else None
        w_hbm = [next(it) for _ in weights]
        dx_ref = next(it)
        w_vm = [next(it) for _ in weights]
        sems = next(it)
        _load_resident(pl.program_id(0), list(zip(w_hbm, w_vm)), sems)
        dx = ALPHA * dr_ref[...] + _mm_nt(dhg_ref[...], w_vm[0][...]) + _mm_nt(dhu_ref[...], w_vm[1][...])
        if with_ple:
            dx = dx + _mm_nt(dpg_ref[...], w_vm[2][...])
        dx_ref[...] = dx

    in_specs = [_row_spec(tb, d), _row_spec(tb, f), _row_spec(tb, f)] + ([_row_spec(tb, d)] if with_ple else [])
    args = [dr, dhg, dhu] + ([first[7]] if with_ple else [])
    dx = pl.pallas_call(
        body_b, name="ffn_bwd_input" + suffix, grid=(s // tb,), in_specs=in_specs + [_ANY] * len(weights),
        out_specs=_row_spec(tb, d), out_shape=jax.ShapeDtypeStruct((s, d), F32),
        scratch_shapes=[pltpu.VMEM(w.shape, w.dtype) for w in weights] + [pltpu.SemaphoreType.DMA((len(weights),))],
        compiler_params=_params("arbitrary"),
    )(*args, *weights)
    return (dx, act, dhg, dhu, dy, dg, db) + tuple(first[7:])


def win_fwd(x1, w_r, w_l, w_g):
    s = x1.shape[0]
    tm = min(ROW_TILE_BWD, s)
    weights = [w_r, w_l, w_g]

    def body(x_ref, wr_h, wl_h, wg_h, hr_ref, hl_ref, hgd_ref, wr_v, wl_v, wg_v, sems):
        _load_resident(pl.program_id(0), [(wr_h, wr_v), (wl_h, wl_v), (wg_h, wg_v)], sems)
        xb = x_ref[...].astype(BF16)
        hr_ref[...] = _mm(xb, wr_v[...])
        hl_ref[...] = _mm(xb, wl_v[...])
        hgd_ref[...] = _mm(xb, wg_v[...])

    return pl.pallas_call(
        body, name="win_fwd", grid=(s // tm,),
        in_specs=[_row_spec(tm, D_MODEL), _ANY, _ANY, _ANY],
        out_specs=[_row_spec(tm, RET_IN), _row_spec(tm, LRU_IN), _row_spec(tm, GDN_IN)],
        out_shape=[jax.ShapeDtypeStruct((s, RET_IN), F32), jax.ShapeDtypeStruct((s, LRU_IN), F32),
                   jax.ShapeDtypeStruct((s, GDN_IN), F32)],
        scratch_shapes=[pltpu.VMEM(w.shape, w.dtype) for w in weights] + [pltpu.SemaphoreType.DMA((3,))],
        compiler_params=_params("arbitrary"),
    )(x1, *weights)


def win_bwd(dr2, dhr, dhl, dhq, dab, w_r, w_l, w_g, w_ab):
    s = dr2.shape[0]
    tm = min(ROW_TILE_BWD, s)
    weights = [w_r, w_l, w_g, w_ab]
    nq = 4 * GDN_W

    def body(dr_ref, dhr_ref, dhl_ref, dhq_ref, dab_ref, wr_h, wl_h, wg_h, wab_h, dx_ref, wr_v, wl_v, wg_v, wab_v, sems):
        _load_resident(pl.program_id(0), [(wr_h, wr_v), (wl_h, wl_v), (wg_h, wg_v), (wab_h, wab_v)], sems)
        dx_ref[...] = (ALPHA * dr_ref[...] + _mm_nt(dhr_ref[...], wr_v[...]) + _mm_nt(dhl_ref[...], wl_v[...])
                       + _mm_nt(dhq_ref[...], wg_v[:, 0:nq]) + _mm_nt(dab_ref[...], wab_v[...]))

    return pl.pallas_call(
        body, name="win_bwd", grid=(s // tm,),
        in_specs=[_row_spec(tm, D_MODEL), _row_spec(tm, RET_IN), _row_spec(tm, LRU_IN), _row_spec(tm, nq), _row_spec(tm, LANES),
                  _ANY, _ANY, _ANY, _ANY],
        out_specs=_row_spec(tm, D_MODEL),
        out_shape=jax.ShapeDtypeStruct((s, D_MODEL), F32),
        scratch_shapes=[pltpu.VMEM(w.shape, w.dtype) for w in weights] + [pltpu.SemaphoreType.DMA((4,))],
        compiler_params=_params("arbitrary"),
    )(dr2, dhr, dhl, dhq, dab, *weights)


def out_fwd(o_r, o_l, o_g, x1, w_out, ln_g, ln_b):
    s = x1.shape[0]
    tm = ROW_TILE

    def body(or_ref, ol_ref, og_ref, x_ref, g_ref, b_ref, w_h, r_ref, xn_ref, xnb_ref, ocat_ref, w_v, sems):
        _load_resident(pl.program_id(0), [(w_h, w_v)], sems)
        ocat = jnp.concatenate([or_ref[...], ol_ref[...], og_ref[...]], axis=1).astype(BF16)
        ocat_ref[...] = ocat
        r = ALPHA * x_ref[...] + _mm(ocat, w_v[...])
        r_ref[...] = r
        xhat, _ = _layer_norm_stats(r)
        xn = xhat * g_ref[...] + b_ref[...]
        xn_ref[...] = xn
        xnb_ref[...] = xn.astype(BF16)

    d = D_MODEL
    return pl.pallas_call(
        body, name="out_fwd", grid=(s // tm,),
        in_specs=[_row_spec(tm, RET_W), _row_spec(tm, LRU_W), _row_spec(tm, GDN_W), _row_spec(tm, d),
                  _full_spec((1, d)), _full_spec((1, d)), _ANY],
        out_specs=[_row_spec(tm, d)] * 4,
        out_shape=[jax.ShapeDtypeStruct((s, d), F32)] * 2 + [jax.ShapeDtypeStruct((s, d), BF16)] * 2,
        scratch_shapes=[pltpu.VMEM(w_out.shape, w_out.dtype), pltpu.SemaphoreType.DMA((1,))],
        compiler_params=_params("arbitrary"),
    )(o_r, o_l, o_g, x1, ln_g, ln_b, w_out)


def out_bwd(dxn, r2, ln_g, w_out):
    s = dxn.shape[0]
    tm = ROW_TILE

    def body(dxn_ref, r_ref, g_ref, w_h, dr_ref, drb_ref, dor_ref, dol_ref, dog_ref, dg_ref, db_ref, w_v, sems):
        step = pl.program_id(0)
        _load_resident(step, [(w_h, w_v)], sems)

        @pl.when(step == 0)
        def _():
            dg_ref[...] = jnp.zeros_like(dg_ref)
            db_ref[...] = jnp.zeros_like(db_ref)

        dxn_v = dxn_ref[...]
        xhat, rstd = _layer_norm_stats(r_ref[...])
        dg_ref[...] += jnp.sum(dxn_v * xhat, axis=0, keepdims=True)
        db_ref[...] += jnp.sum(dxn_v, axis=0, keepdims=True)
        dyh = dxn_v * g_ref[...]
        dr = rstd * (dyh - jnp.mean(dyh, axis=-1, keepdims=True) - xhat * jnp.mean(dyh * xhat, axis=-1, keepdims=True))
        dr_ref[...] = dr
        drb = dr.astype(BF16)
        drb_ref[...] = drb
        dor_ref[...] = _mm_nt(drb, w_v[0:RET_W, :])
        dol_ref[...] = _mm_nt(drb, w_v[RET_W:RET_W + LRU_W, :])
        dog_ref[...] = _mm_nt(drb, w_v[RET_W + LRU_W:, :])

    d = D_MODEL
    return pl.pallas_call(
        body, name="out_bwd", grid=(s // tm,),
        in_specs=[_row_spec(tm, d), _row_spec(tm, d), _full_spec((1, d)), _ANY],
        out_specs=[_row_spec(tm, d), _row_spec(tm, d), _row_spec(tm, RET_W), _row_spec(tm, LRU_W), _row_spec(tm, GDN_W),
                   _full_spec((1, d)), _full_spec((1, d))],
        out_shape=[jax.ShapeDtypeStruct((s, d), F32), jax.ShapeDtypeStruct((s, d), BF16),
                   jax.ShapeDtypeStruct((s, RET_W), F32), jax.ShapeDtypeStruct((s, LRU_W), F32),
                   jax.ShapeDtypeStruct((s, GDN_W), F32), jax.ShapeDtypeStruct((1, d), F32), jax.ShapeDtypeStruct((1, d), F32)],
        scratch_shapes=[pltpu.VMEM(w_out.shape, w_out.dtype), pltpu.SemaphoreType.DMA((1,))],
        compiler_params=_params("arbitrary"),
    )(dxn, r2, ln_g, w_out)


def wgrad(a, b, name, out_dtype=BF16):
    s, m = a.shape
    n = b.shape[1]
    tk = 1024 if s % 1024 == 0 else s
    tm = next((c for c in (1408, 1024, 768, 512, 384, 256) if m % c == 0), m)
    tn = next((c for c in (1408, 1152, 1024, 768, 512) if n % c == 0), n)
    nk = s // tk

    def body(a_ref, b_ref, o_ref, acc_ref):
        k = pl.program_id(2)

        @pl.when(k == 0)
        def _():
            acc_ref[...] = jnp.zeros_like(acc_ref)

        acc_ref[...] += _mm_tn(a_ref[...].astype(BF16), b_ref[...].astype(BF16))

        @pl.when(k == nk - 1)
        def _():
            o_ref[...] = acc_ref[...].astype(o_ref.dtype)

    return pl.pallas_call(
        body, name=name, grid=(m // tm, n // tn, nk),
        in_specs=[pl.BlockSpec((tk, tm), lambda i, j, k: (k, i)), pl.BlockSpec((tk, tn), lambda i, j, k: (k, j))],
        out_specs=pl.BlockSpec((tm, tn), lambda i, j, k: (i, j)),
        out_shape=jax.ShapeDtypeStruct((m, n), out_dtype),
        scratch_shapes=[pltpu.VMEM((tm, tn), F32)],
        compiler_params=_params("arbitrary", "arbitrary", "arbitrary"),
    )(a, b)


def loss_and_grad(y, target):
    s, d = y.shape
    tm = ROW_TILE

    def body(y_ref, t_ref, dy_ref, l_ref):
        @pl.when(pl.program_id(0) == 0)
        def _():
            l_ref[...] = jnp.zeros_like(l_ref)

        err = y_ref[...] - t_ref[...]
        dy_ref[...] = err / d
        l_ref[...] += 0.5 * jnp.sum(jnp.mean(err * err, axis=-1, keepdims=True), axis=0, keepdims=True)

    return pl.pallas_call(
        body, name="loss_and_grad", grid=(s // tm,),
        in_specs=[_row_spec(tm, d), _row_spec(tm, d)],
        out_specs=[_row_spec(tm, d), _full_spec((1, 1))],
        out_shape=[jax.ShapeDtypeStruct((s, d), F32), jax.ShapeDtypeStruct((1, 1), F32)],
        compiler_params=_params("arbitrary"),
    )(y, target)


def _ret_consts():
    lg = np.log1p(-np.exp2(-5.0 - np.arange(RET_HEADS, dtype=np.float64)))
    idx = np.arange(CHUNK, dtype=np.float64)
    intra = np.exp(np.abs(idx[:, None] - idx[None, :])[None] * lg[:, None, None])
    cross = np.repeat(np.exp((idx + 1.0)[:, None] * lg[None, :]), HEAD, axis=1)
    tail = np.repeat(np.exp((CHUNK - 1.0 - idx)[:, None] * lg[None, :]), HEAD, axis=1)
    dec = np.repeat(np.exp(CHUNK * lg)[None, :], HEAD, axis=1)
    half = HEAD // 2
    inv_freq = (ROPE_THETA ** (-jnp.arange(half, dtype=F32) / half))
    invf = jnp.tile(inv_freq, 2 * LANES // HEAD)[None, :]
    sgn = np.tile(np.concatenate([-np.ones(half), np.ones(half)]), LANES // HEAD)[None, :]
    f = lambda a: jnp.asarray(a, F32)
    return dict(intra=f(intra), cross=f(cross), tail=f(tail), dec=f(dec), invf=invf, sgn=f(sgn))


def _swap_halves(t):
    lane = lax.broadcasted_iota(jnp.int32, t.shape, 1)
    return jnp.where((lane & 32) == 0, pltpu.roll(t, LANES - 32, 1), pltpu.roll(t, 32, 1))


def _rope(t, c, s):
    return t * c + _swap_halves(t) * s


def _rope_transposed(g, c, s):
    return g * c + _swap_halves(g * s)


def _head_mask(hd):
    lane = lax.broadcasted_iota(jnp.int32, (1, LANES), 1)
    return ((lane >= HEAD * hd) & (lane < HEAD * (hd + 1))).astype(F32)


def _block_diag_mask():
    r = lax.broadcasted_iota(jnp.int32, (LANES, LANES), 0)
    c = lax.broadcasted_iota(jnp.int32, (LANES, LANES), 1)
    return ((r >= HEAD) == (c >= HEAD)).astype(F32)


def _ret_specs(n_of):
    cst = lambda shape: pl.BlockSpec(shape, lambda i: (0,) * len(shape))
    return [pl.BlockSpec((CHUNK, RET_IN), lambda i: (n_of(i), 0)), pl.BlockSpec((CHUNK, 1), lambda i: (n_of(i), 0)),
            cst((1, LANES)), cst((1, LANES)), cst((RET_HEADS, CHUNK, CHUNK)), cst((CHUNK, RET_W)), cst((CHUNK, RET_W)),
            cst((1, RET_W)), cst((1, RET_W))]


def ret_fwd(hr, pos, norm_g):
    s = hr.shape[0]
    n_chunks = s // CHUNK
    cs = _ret_consts()
    n_slab = RET_W // LANES

    def body(hr_ref, pos_ref, invf_ref, sgn_ref, intra_ref, cross_ref, tail_ref, dec_ref, g_ref, o_ref, opre_ref, st_ref, state):
        @pl.when(pl.program_id(0) == 0)
        def _():
            state[...] = jnp.zeros_like(state)

        ang = pos_ref[...].astype(F32) * invf_ref[...]
        cosv = jnp.cos(ang)
        sinv = jnp.sin(ang) * sgn_ref[...]
        bd = _block_diag_mask()
        for sl in range(n_slab):
            lanes = slice(LANES * sl, LANES * (sl + 1))
            rows = slice(LANES * sl, LANES * (sl + 1))
            q = hr_ref[:, LANES * sl:LANES * (sl + 1)]
            k = hr_ref[:, RET_W + LANES * sl:RET_W + LANES * (sl + 1)]
            v = hr_ref[:, 2 * RET_W + LANES * sl:2 * RET_W + LANES * (sl + 1)]
            gate = hr_ref[:, 3 * RET_W + LANES * sl:3 * RET_W + LANES * (sl + 1)]
            qt = _rope(q, cosv, sinv) * (HEAD ** -0.5)
            kt = _rope(k, cosv, sinv)
            st = state[rows, :]
            st_ref[rows, :] = st
            o = _mm(qt * cross_ref[:, lanes], st)
            for hd in range(2):
                m = _head_mask(hd)
                sc = _mm_nt(qt * m, kt) * intra_ref[2 * sl + hd]
                o = o + _mm(sc, v) * m
            state[rows, :] = st * dec_ref[:, lanes] + _mm_tn(kt, v * tail_ref[:, lanes]) * bd
            opre_ref[:, lanes] = o
            mu = _group_sum_slab(o) * (1.0 / HEAD)
            dlt = o - mu
            var = _group_sum_slab(dlt * dlt) * (1.0 / HEAD)
            on = dlt * lax.rsqrt(var + 1e-5)
            o_ref[:, lanes] = on * g_ref[:, lanes] * (gate * _sigmoid(gate))

    out_row = lambda w: pl.BlockSpec((CHUNK, w), lambda i: (i, 0))
    return pl.pallas_call(
        body, name="ret_fwd", grid=(n_chunks,),
        in_specs=_ret_specs(lambda i: i),
        out_specs=[out_row(RET_W), out_row(RET_W), pl.BlockSpec((RET_W, LANES), lambda i: (i, 0))],
        out_shape=[jax.ShapeDtypeStruct((s, RET_W), F32), jax.ShapeDtypeStruct((s, RET_W), F32),
                   jax.ShapeDtypeStruct((n_chunks * RET_W, LANES), F32)],
        scratch_shapes=[pltpu.VMEM((RET_W, LANES), F32)],
        compiler_params=_params("arbitrary"),
    )(hr, pos, cs["invf"], cs["sgn"], cs["intra"], cs["cross"], cs["tail"], cs["dec"], norm_g)


def ret_bwd(hr, pos, norm_g, opre, states, dout):
    s = hr.shape[0]
    n_chunks = s // CHUNK
    cs = _ret_consts()
    n_slab = RET_W // LANES
    rev = lambda i: n_chunks - 1 - i

    def body(hr_ref, pos_ref, invf_ref, sgn_ref, intra_ref, cross_ref, tail_ref, dec_ref, g_ref, opre_ref, st_ref, do_ref,
             dh_ref, dg_ref, gstate):
        @pl.when(pl.program_id(0) == 0)
        def _():
            gstate[...] = jnp.zeros_like(gstate)
            dg_ref[...] = jnp.zeros_like(dg_ref)

        ang = pos_ref[...].astype(F32) * invf_ref[...]
        cosv = jnp.cos(ang)
        sinv = jnp.sin(ang) * sgn_ref[...]
        bd = _block_diag_mask()
        for sl in range(n_slab):
            lanes = slice(LANES * sl, LANES * (sl + 1))
            rows = slice(LANES * sl, LANES * (sl + 1))
            q = hr_ref[:, LANES * sl:LANES * (sl + 1)]
            k = hr_ref[:, RET_W + LANES * sl:RET_W + LANES * (sl + 1)]
            v = hr_ref[:, 2 * RET_W + LANES * sl:2 * RET_W + LANES * (sl + 1)]
            gate = hr_ref[:, 3 * RET_W + LANES * sl:3 * RET_W + LANES * (sl + 1)]
            qt = _rope(q, cosv, sinv) * (HEAD ** -0.5)
            kt = _rope(k, cosv, sinv)
            o = opre_ref[:, lanes]
            mu = _group_sum_slab(o) * (1.0 / HEAD)
            dlt = o - mu
            var = _group_sum_slab(dlt * dlt) * (1.0 / HEAD)
            rstd = lax.rsqrt(var + 1e-5)
            on = dlt * rstd
            sil, dsil = _silu_and_grad(gate)
            dout_v = do_ref[:, lanes]
            gn = g_ref[:, lanes]
            dg_ref[:, lanes] += jnp.sum(dout_v * on * sil, axis=0, keepdims=True)
            d_on = dout_v * gn * sil
            dgate = dout_v * on * gn * dsil
            d_o = rstd * (d_on - _group_sum_slab(d_on) * (1.0 / HEAD) - on * (_group_sum_slab(d_on * on) * (1.0 / HEAD)))
            st = st_ref[rows, :]
            gs = gstate[rows, :]
            cross = cross_ref[:, lanes]
            tail = tail_ref[:, lanes]
            dqt = _mm_nt(d_o, st) * cross
            ds_here = _mm_tn(qt * cross, d_o) * bd
            vt = v * tail
            dkt = _mm_nt(vt, gs)
            dv = _mm(kt, gs) * tail
            for hd in range(2):
                m = _head_mask(hd)
                qm = qt * m
                dom = d_o * m
                intra = intra_ref[2 * sl + hd]
                sc = _mm_nt(qm, kt) * intra
                dsc = _mm_nt(dom, v) * intra
                dqt = dqt + _mm(dsc, kt) * m
                dkt = dkt + _mm_tn(dsc, qm)
                dv = dv + _mm_tn(sc, dom)
            gstate[rows, :] = gs * dec_ref[:, lanes] + ds_here
            dh_ref[:, LANES * sl:LANES * (sl + 1)] = _rope_transposed(dqt * (HEAD ** -0.5), cosv, sinv).astype(BF16)
            dh_ref[:, RET_W + LANES * sl:RET_W + LANES * (sl + 1)] = _rope_transposed(dkt, cosv, sinv).astype(BF16)
            dh_ref[:, 2 * RET_W + LANES * sl:2 * RET_W + LANES * (sl + 1)] = dv.astype(BF16)
            dh_ref[:, 3 * RET_W + LANES * sl:3 * RET_W + LANES * (sl + 1)] = dgate.astype(BF16)

    row = lambda w: pl.BlockSpec((CHUNK, w), lambda i: (rev(i), 0))
    return pl.pallas_call(
        body, name="ret_bwd", grid=(n_chunks,),
        in_specs=_ret_specs(rev) + [row(RET_W), pl.BlockSpec((RET_W, LANES), lambda i: (rev(i), 0)), row(RET_W)],
        out_specs=[row(RET_IN), pl.BlockSpec((1, RET_W), lambda i: (0, 0))],
        out_shape=[jax.ShapeDtypeStruct((s, RET_IN), BF16), jax.ShapeDtypeStruct((1, RET_W), F32)],
        scratch_shapes=[pltpu.VMEM((RET_W, LANES), F32)],
        compiler_params=_params("arbitrary"),
    )(hr, pos, cs["invf"], cs["sgn"], cs["intra"], cs["cross"], cs["tail"], cs["dec"], norm_g, opre, states, dout)


def _lru_gates(xc, wa_ref, ba_ref, wx_ref, bx_ref, lam_ref):
    xcb = xc.astype(BF16)
    r = _sigmoid(_mm(xcb, wa_ref[...].astype(BF16)) + ba_ref[...])
    ig = _sigmoid(_mm(xcb, wx_ref[...].astype(BF16)) + bx_ref[...])
    lam = lam_ref[...]
    ls = jnp.minimum(lam, 0.0) - _log1p(jnp.exp(-jnp.abs(lam)))
    la = (LRU_C * r) * ls
    a = jnp.exp(la)
    mult = jnp.sqrt(-_expm1(2.0 * la))
    return r, ig, ls, a, mult


def _lru_conv(x, xprev, w_ref, b_ref):
    xc = b_ref[...] + w_ref[3:4, :] * x
    for j in (1, 2, 3):
        xc = xc + w_ref[3 - j:4 - j, :] * _shift_rows(x, xprev, j)
    return xc


def lru_fwd(hl, conv_w, conv_b, w_a, b_a, w_x, b_x, lam):
    s = hl.shape[0]
    ts = SCAN_TILE
    w = LRU_W

    def body(hl_ref, hp_ref, cw_ref, cb_ref, wa_ref, ba_ref, wx_ref, bx_ref, lam_ref, o_ref, xc_ref, h_ref, carry):
        i = pl.program_id(0)

        @pl.when(i == 0)
        def _():
            carry[...] = jnp.zeros_like(carry)

        x = hl_ref[:, 0:w]
        gate = hl_ref[:, w:2 * w]
        xprev = hp_ref[...] * (i > 0).astype(F32)
        xc = _lru_conv(x, xprev, cw_ref, cb_ref)
        xc_ref[...] = xc
        _, ig, _, a, mult = _lru_gates(xc, wa_ref, ba_ref, wx_ref, bx_ref, lam_ref)
        b = mult * (ig * xc)
        row = lax.broadcasted_iota(jnp.int32, (ts, w), 0)
        d = 1
        while d < ts:
            ap = jnp.where(row >= d, pltpu.roll(a, d, 0), 1.0)
            bp = jnp.where(row >= d, pltpu.roll(b, d, 0), 0.0)
            b = a * bp + b
            a = a * ap
            d *= 2
        h = b + a * carry[0:1, :]
        h_ref[...] = h
        carry[0:1, :] = h[ts - 1:ts, :]
        o_ref[...] = h * _gelu(gate)

    cst = lambda shape: pl.BlockSpec(shape, lambda i: (0, 0))
    return pl.pallas_call(
        body, name="lru_fwd", grid=(s // ts,),
        in_specs=[_row_spec(ts, 2 * w), pl.BlockSpec((ts, w), lambda i: (jnp.maximum(i - 1, 0), 0)),
                  cst((4, w)), cst((1, w)), cst((w, w)), cst((1, w)), cst((w, w)), cst((1, w)), cst((1, w))],
        out_specs=[_row_spec(ts, w)] * 3,
        out_shape=[jax.ShapeDtypeStruct((s, w), F32)] * 3,
        scratch_shapes=[pltpu.VMEM((8, w), F32)],
        compiler_params=_params("arbitrary"),
    )(hl, hl, conv_w, conv_b, w_a, b_a, w_x, b_x, lam)


def lru_bwd(hl, conv_w, conv_b, w_a, b_a, w_x, b_x, lam, xc_saved, h_saved, dout):
    s = hl.shape[0]
    ts = SCAN_TILE
    w = LRU_W
    nb = s // ts
    rev = lambda i: nb - 1 - i

    def body(hl_ref, hp_ref, cw_ref, cb_ref, wa_ref, ba_ref, wx_ref, bx_ref, lam_ref, xc_ref, h_ref, hprev_ref, do_ref,
             dhl_ref, dcw_ref, dcb_ref, dwa_ref, dba_ref, dwx_ref, dbx_ref, dlam_ref, carry, dxc_next):
        i = pl.program_id(0)
        blk = nb - 1 - i

        @pl.when(i == 0)
        def _():
            carry[...] = jnp.zeros_like(carry)
            dxc_next[...] = jnp.zeros_like(dxc_next)
            for ref in (dcw_ref, dcb_ref, dwa_ref, dba_ref, dwx_ref, dbx_ref, dlam_ref):
                ref[...] = jnp.zeros_like(ref)

        first = (blk > 0).astype(F32)
        x = hl_ref[:, 0:w]
        gate = hl_ref[:, w:2 * w]
        xprev = hp_ref[...] * first
        xc = xc_ref[...]
        h = h_ref[...]
        hprev = hprev_ref[...] * first
        r, ig, ls, a, mult = _lru_gates(xc, wa_ref, ba_ref, wx_ref, bx_ref, lam_ref)
        do = do_ref[...]
        dh = do * _gelu(gate)
        dgate = do * h * _gelu_grad(gate)
        row = lax.broadcasted_iota(jnp.int32, (ts, w), 0)
        ca = jnp.where(row < ts - 1, pltpu.roll(a, ts - 1, 0), 1.0)
        cb = dh
        d = 1
        while d < ts:
            an = jnp.where(row < ts - d, pltpu.roll(ca, ts - d, 0), 1.0)
            bn = jnp.where(row < ts - d, pltpu.roll(cb, ts - d, 0), 0.0)
            cb = cb + ca * bn
            ca = ca * an
            d *= 2
        lamb = cb + ca * carry[0:1, :]
        carry[0:1, :] = a[0:1, :] * lamb[0:1, :]
        h_before = _shift_rows(h, hprev, 1)
        da = lamb * h_before
        ix = ig * xc
        dmult = lamb * ix
        dig = lamb * mult * xc
        dxc = lamb * mult * ig
        dla = (da - dmult * a / mult) * a
        dr = dla * LRU_C * ls
        dlam_ref[...] += jnp.sum(dla * LRU_C * r, axis=0, keepdims=True) * _sigmoid(-lam_ref[...])
        dpa = dr * r * (1.0 - r)
        dpx = dig * ig * (1.0 - ig)
        dba_ref[...] += jnp.sum(dpa, axis=0, keepdims=True)
        dbx_ref[...] += jnp.sum(dpx, axis=0, keepdims=True)
        dpab = dpa.astype(BF16)
        dpxb = dpx.astype(BF16)
        xcb = xc.astype(BF16)
        dxc = dxc + _mm_nt(dpab, wa_ref[...].astype(BF16)) + _mm_nt(dpxb, wx_ref[...].astype(BF16))
        dwa_ref[...] += _mm_tn(xcb, dpab)
        dwx_ref[...] += _mm_tn(xcb, dpxb)
        dcb_ref[...] += jnp.sum(dxc, axis=0, keepdims=True)
        nxt = dxc_next[...]
        dx = cw_ref[3:4, :] * dxc
        dcw_ref[3:4, :] += jnp.sum(dxc * x, axis=0, keepdims=True)
        for j in (1, 2, 3):
            dx = dx + cw_ref[3 - j:4 - j, :] * _shift_rows_up(dxc, nxt, j)
            dcw_ref[3 - j:4 - j, :] += jnp.sum(dxc * _shift_rows(x, xprev, j), axis=0, keepdims=True)
        dxc_next[...] = dxc
        dhl_ref[:, 0:w] = dx.astype(BF16)
        dhl_ref[:, w:2 * w] = dgate.astype(BF16)

    cst = lambda shape: pl.BlockSpec(shape, lambda i: (0, 0))
    rowr = lambda width: pl.BlockSpec((ts, width), lambda i: (rev(i), 0))
    prevr = lambda width: pl.BlockSpec((ts, width), lambda i: (jnp.maximum(rev(i) - 1, 0), 0))
    return pl.pallas_call(
        body, name="lru_bwd", grid=(nb,),
        in_specs=[rowr(2 * w), prevr(w), cst((4, w)), cst((1, w)), cst((w, w)), cst((1, w)), cst((w, w)), cst((1, w)), cst((1, w)),
                  rowr(w), rowr(w), prevr(w), rowr(w)],
        out_specs=[rowr(2 * w), cst((4, w)), cst((1, w)), cst((w, w)), cst((1, w)), cst((w, w)), cst((1, w)), cst((1, w))],
        out_shape=[jax.ShapeDtypeStruct((s, 2 * w), BF16), jax.ShapeDtypeStruct((4, w), F32), jax.ShapeDtypeStruct((1, w), F32),
                   jax.ShapeDtypeStruct((w, w), F32), jax.ShapeDtypeStruct((1, w), F32), jax.ShapeDtypeStruct((w, w), F32),
                   jax.ShapeDtypeStruct((1, w), F32), jax.ShapeDtypeStruct((1, w), F32)],
        scratch_shapes=[pltpu.VMEM((8, w), F32), pltpu.VMEM((ts, w), F32)],
        compiler_params=_params("arbitrary"),
    )(hl, hl, conv_w, conv_b, w_a, b_a, w_x, b_x, lam, xc_saved, h_saved, h_saved, dout)


GDN_QKV = 3 * GDN_W


def _tri_inverse_many(nms):
    r = lax.broadcasted_iota(jnp.int32, nms[0].shape, 0)
    c = lax.broadcasted_iota(jnp.int32, nms[0].shape, 1)
    eye = (r == c).astype(F32)
    ts = [eye - nm for nm in nms]
    ps = list(nms)
    for _ in range(5):
        ps = [_mm3(p, p) for p in ps]
        ts = [t + _mm3(t, p) for t, p in zip(ts, ps)]
    return ts


def _gdn_front(hx_ref, hprev, cw_ref, al_ref, dt_ref):
    w = GDN_W
    x = hx_ref[:, 0:GDN_QKV]
    y = cw_ref[3:4, :] * x
    for j in (1, 2, 3):
        y = y + cw_ref[3 - j:4 - j, :] * _shift_rows(x, hprev, j)
    qkv, dsil = _silu_and_grad(y)
    q, k, v = qkv[:, 0:w], qkv[:, w:2 * w], qkv[:, 2 * w:3 * w]
    rq = lax.rsqrt(_group_sum(q * q) + 1e-6)
    rk = lax.rsqrt(_group_sum(k * k) + 1e-6)
    beta = _sigmoid(hx_ref[:, 5 * w:6 * w])
    sp_in = hx_ref[:, 4 * w:5 * w] + dt_ref[...]
    neg_a = -jnp.exp(al_ref[...])
    g = neg_a * _softplus(sp_in)
    gc = _rows_prefix_sum(g)
    return dict(x=x, dsil=dsil, qn=q * rq, kn=k * rk, v=v, rq=rq, rk=rk, beta=beta, sp_in=sp_in, neg_a=neg_a, g=g, gc=gc)


def _stack_heads(x):
    return jnp.concatenate([x * _head_mask(0), x * _head_mask(1)], axis=0)


def _unstack_heads(y):
    return y[0:CHUNK] + y[CHUNK:2 * CHUNK]


def _head_transpose(x):
    return jnp.concatenate([x[:, 0:HEAD].T, x[:, HEAD:2 * HEAD].T], axis=1)


def _head_total(x):
    cols = jnp.broadcast_to(jnp.sum(x, axis=0, keepdims=True), (8, LANES))
    return _group_sum_slab(cols)[0:1]


def _slab_tri_masks():
    r = lax.broadcasted_iota(jnp.int32, (CHUNK, LANES), 0)
    c = lax.broadcasted_iota(jnp.int32, (CHUNK, LANES), 1) & (HEAD - 1)
    return r >= c, r > c


def _gdn_slab(fr, sl, tri):
    lower, strict = tri
    ls = lambda a: a[:, LANES * sl:LANES * (sl + 1)]
    k = ls(fr["kn"])
    q = ls(fr["qn"]) * (HEAD ** -0.5)
    v = ls(fr["v"])
    beta = ls(fr["beta"])
    gc = ls(fr["gc"])
    e = jnp.exp(gc)
    gl = gc[CHUNK - 1:CHUNK, :]
    xt = jnp.exp(gl - gc)
    dec = jnp.where(lower, jnp.exp(jnp.minimum(gc - _head_transpose(gc), 0.0)), 0.0)
    kbd = _stack_heads(k)
    kk = _mm_nt(k, kbd)
    qkr = _mm_nt(q, kbd)
    return dict(k=k, q=q, v=v, beta=beta, e=e, egl=jnp.exp(gl), xt=xt, dec=dec, kk=kk, qkr=qkr, kbd=kbd,
                nm=jnp.where(strict, beta * kk * dec, 0.0))


def gdn_fwd(hx, conv_w, a_log_e, dt_bias_e, norm_g_e):
    s = hx.shape[0]
    n_chunks = s // CHUNK
    w = GDN_W
    n_slab = w // LANES

    def body(hx_ref, hp_ref, cw_ref, al_ref, dt_ref, ng_ref, o_ref, opre_ref, t_ref, st_ref, state):
        n = pl.program_id(0)

        @pl.when(n == 0)
        def _():
            state[...] = jnp.zeros_like(state)

        fr = _gdn_front(hx_ref, hp_ref[...] * (n > 0).astype(F32), cw_ref, al_ref, dt_ref)
        tri = _slab_tri_masks()
        bd = _block_diag_mask()
        st_all = state[...]
        st_ref[...] = st_all
        slabs = [_gdn_slab(fr, sl, tri) for sl in range(n_slab)]
        tbd = _tri_inverse_many([_stack_heads(sq["nm"]) for sq in slabs])
        ts, outs, new_states = [], [], []
        for sl, (sq, t) in enumerate(zip(slabs, tbd)):
            t = _unstack_heads(t)
            ts.append(t)
            u = _mm(t, _stack_heads(sq["v"] * sq["beta"]))
            wk = _mm(t, _stack_heads(sq["k"] * (sq["beta"] * sq["e"])))
            st = st_all[LANES * sl:LANES * (sl + 1), :]
            vnew = u - _mm(wk, st)
            outs.append(_mm(sq["q"] * sq["e"], st) + _mm(sq["qkr"] * sq["dec"], _stack_heads(vnew)))
            new_states.append(st * sq["egl"] + _mm_tn(sq["k"] * sq["xt"], vnew) * bd)
        t_ref[...] = jnp.concatenate(ts, axis=1)
        state[...] = jnp.concatenate(new_states, axis=0)
        o = jnp.concatenate(outs, axis=1)
        opre_ref[...] = o
        rinv = lax.rsqrt(_group_sum(o * o) * (1.0 / HEAD) + 1e-6)
        z = hx_ref[:, 3 * w:4 * w]
        o_ref[...] = (o * rinv) * ng_ref[...] * (z * _sigmoid(z))

    cst = lambda shape: pl.BlockSpec(shape, lambda i: (0, 0))
    row = lambda width: pl.BlockSpec((CHUNK, width), lambda i: (i, 0))
    return pl.pallas_call(
        body, name="gdn_fwd", grid=(n_chunks,),
        in_specs=[row(GDN_IN), pl.BlockSpec((CHUNK, GDN_QKV), lambda i: (jnp.maximum(i - 1, 0), 0)),
                  cst((4, GDN_QKV)), cst((1, w)), cst((1, w)), cst((1, w))],
        out_specs=[row(w)] * 3 + [pl.BlockSpec((w, LANES), lambda i: (i, 0))],
        out_shape=[jax.ShapeDtypeStruct((s, w), F32)] * 3 + [jax.ShapeDtypeStruct((n_chunks * w, LANES), F32)],
        scratch_shapes=[pltpu.VMEM((w, LANES), F32)],
        compiler_params=_params("arbitrary"),
    )(hx, hx, conv_w, a_log_e, dt_bias_e, norm_g_e)


def gdn_bwd(hx, conv_w, a_log_e, dt_bias_e, norm_g_e, opre, tmat, states, dout):
    s = hx.shape[0]
    n_chunks = s // CHUNK
    w = GDN_W
    rev = lambda i: n_chunks - 1 - i

    def body(hx_ref, hp_ref, cw_ref, al_ref, dt_ref, ng_ref, opre_ref, t_ref, st_ref, do_ref,
             dhx_ref, dab_ref, dcw_ref, dal_ref, ddt_ref, dng_ref, dstate, dy_next):
        i = pl.program_id(0)
        n = n_chunks - 1 - i

        @pl.when(i == 0)
        def _():
            dstate[...] = jnp.zeros_like(dstate)
            dy_next[...] = jnp.zeros_like(dy_next)
            for ref in (dcw_ref, dal_ref, ddt_ref, dng_ref):
                ref[...] = jnp.zeros_like(ref)

        hprev = hp_ref[...] * (n > 0).astype(F32)
        fr = _gdn_front(hx_ref, hprev, cw_ref, al_ref, dt_ref)
        tri = _slab_tri_masks()
        lower, strict = tri
        o = opre_ref[...]
        rinv = lax.rsqrt(_group_sum(o * o) * (1.0 / HEAD) + 1e-6)
        yn = o * rinv
        z = hx_ref[:, 3 * w:4 * w]
        sil, dsil_z = _silu_and_grad(z)
        dout_v = do_ref[...]
        ng = ng_ref[...]
        dng_ref[...] += jnp.sum(dout_v * yn * sil, axis=0, keepdims=True)
        dz = dout_v * yn * ng * dsil_z
        dyn = dout_v * ng * sil
        d_o = rinv * (dyn - yn * (_group_sum(dyn * yn) * (1.0 / HEAD)))
        last_row = (lax.broadcasted_iota(jnp.int32, (CHUNK, LANES), 0) == CHUNK - 1).astype(F32)
        bd = _block_diag_mask()
        gsum = _group_sum_slab
        t_all, st_all, dsn_all = t_ref[...], st_ref[...], dstate[...]
        new_ds, dq_l, dk_l, dv_l, dbeta_l, dgc_l = [], [], [], [], [], []
        for sl in range(w // LANES):
            lanes = slice(LANES * sl, LANES * (sl + 1))
            sq = _gdn_slab(fr, sl, tri)
            k, q, v, beta, e, xt, dec, kk, qkr, kbd = (sq[n_] for n_ in ("k", "q", "v", "beta", "e", "xt", "dec", "kk", "qkr", "kbd"))
            t = t_all[:, lanes]
            st = st_all[lanes, :]
            dsn = dsn_all[lanes, :]
            do_s = d_o[:, lanes]
            u = _mm(t, _stack_heads(v * beta))
            wk = _mm(t, _stack_heads(k * (beta * e)))
            vnew = u - _mm(wk, st)
            qk = qkr * dec
            kt = k * xt
            dqd = _mm_nt(do_s, st)
            ds = _mm_tn(q * e, do_s) * bd
            dqk = _mm_nt(do_s, _stack_heads(vnew))
            dvnew = _unstack_heads(_mm_tn(qk, do_s) * bd) + _mm(kt, dsn)
            dkt = _mm_nt(vnew, dsn)
            ds = ds + sq["egl"] * dsn
            dgl = _head_total(dsn * st) * sq["egl"]
            dwk = -_mm_nt(dvnew, st)
            ds = ds - _mm_tn(wk, dvnew) * bd
            drv = _unstack_heads(_mm_tn(t, dvnew) * bd)
            drk = _unstack_heads(_mm_tn(t, dwk) * bd)
            dnm = jnp.where(strict, -(_mm_nt(drv, _stack_heads(u)) + _mm_nt(drk, _stack_heads(wk))), 0.0)
            dbeta = gsum(dnm * kk * dec)
            dkk = dnm * beta * dec
            ddec = dnm * beta * kk + dqk * qkr
            mq = dqk * dec
            dq = _mm(mq, kbd) + dqd * e
            dk = (_unstack_heads(_mm_tn(mq, q) * bd) + _mm(dkk, kbd) + _unstack_heads(_mm_tn(dkk, k) * bd)
                  + drk * (beta * e) + dkt * xt)
            dv_l.append(drv * beta)
            rks = gsum(drk * k)
            dbeta = dbeta + gsum(drv * v) + rks * e
            de = rks * beta + gsum(dqd * q)
            dxt = gsum(dkt * k) * xt
            dgl = dgl + jnp.sum(dxt, axis=0, keepdims=True)
            dd = ddec * dec
            dgc = de * e - dxt + gsum(dd) - gsum(_head_transpose(dd)) + last_row * dgl
            new_ds.append(ds)
            dq_l.append(dq * (HEAD ** -0.5))
            dk_l.append(dk)
            dbeta_l.append(dbeta)
            dgc_l.append(dgc)
        dstate[...] = jnp.concatenate(new_ds, axis=0)
        dg = _rows_suffix_sum(jnp.concatenate(dgc_l, axis=1))
        dal_ref[...] += jnp.sum(dg * fr["g"], axis=0, keepdims=True)
        da = dg * fr["neg_a"] * _sigmoid(fr["sp_in"])
        ddt_ref[...] += jnp.sum(da, axis=0, keepdims=True)
        beta_all = fr["beta"]
        db = jnp.concatenate(dbeta_l, axis=1) * beta_all * (1.0 - beta_all)
        lane = lax.broadcasted_iota(jnp.int32, (CHUNK, LANES), 1)
        dab = jnp.zeros((CHUNK, LANES), F32)
        for hd in range(GDN_HEADS):
            dab = jnp.where(lane == hd, da[:, HEAD * hd:HEAD * hd + 1], dab)
            dab = jnp.where(lane == GDN_HEADS + hd, db[:, HEAD * hd:HEAD * hd + 1], dab)
        dab_ref[...] = dab.astype(BF16)
        dqn = jnp.concatenate(dq_l, axis=1)
        dkn = jnp.concatenate(dk_l, axis=1)
        dq_raw = fr["rq"] * (dqn - fr["qn"] * _group_sum(dqn * fr["qn"]))
        dk_raw = fr["rk"] * (dkn - fr["kn"] * _group_sum(dkn * fr["kn"]))
        dy = jnp.concatenate([dq_raw, dk_raw] + dv_l, axis=1) * fr["dsil"]
        nxt = dy_next[...]
        x = fr["x"]
        dx = cw_ref[3:4, :] * dy
        dcw_ref[3:4, :] += jnp.sum(dy * x, axis=0, keepdims=True)
        for j in (1, 2, 3):
            dx = dx + cw_ref[3 - j:4 - j, :] * _shift_rows_up(dy, nxt, j)
            dcw_ref[3 - j:4 - j, :] += jnp.sum(dy * _shift_rows(x, hprev, j), axis=0, keepdims=True)
        dy_next[...] = dy
        dhx_ref[:, 0:GDN_QKV] = dx.astype(BF16)
        dhx_ref[:, 3 * w:4 * w] = dz.astype(BF16)

    cst = lambda shape: pl.BlockSpec(shape, lambda i: (0, 0))
    row = lambda width: pl.BlockSpec((CHUNK, width), lambda i: (rev(i), 0))
    buf = lambda width: pltpu.VMEM((CHUNK, width), F32)
    return pl.pallas_call(
        body, name="gdn_bwd", grid=(n_chunks,),
        in_specs=[row(GDN_IN), pl.BlockSpec((CHUNK, GDN_QKV), lambda i: (jnp.maximum(rev(i) - 1, 0), 0)),
                  cst((4, GDN_QKV)), cst((1, w)), cst((1, w)), cst((1, w)), row(w), row(w),
                  pl.BlockSpec((w, LANES), lambda i: (rev(i), 0)), row(w)],
        out_specs=[row(4 * w), row(LANES), cst((4, GDN_QKV)), cst((1, w)), cst((1, w)), cst((1, w))],
        out_shape=[jax.ShapeDtypeStruct((s, 4 * w), BF16), jax.ShapeDtypeStruct((s, LANES), BF16),
                   jax.ShapeDtypeStruct((4, GDN_QKV), F32),
                   jax.ShapeDtypeStruct((1, w), F32), jax.ShapeDtypeStruct((1, w), F32), jax.ShapeDtypeStruct((1, w), F32)],
        scratch_shapes=[pltpu.VMEM((w, LANES), F32), buf(GDN_QKV)],
        compiler_params=_params("arbitrary"),
    )(hx, hx, conv_w, a_log_e, dt_bias_e, norm_g_e, opre, tmat, states, dout)


_MESH = pl.DeviceIdType.MESH


def all_gather8(x, name):
    m, n = x.shape

    def body(x_ref, out_ref, send_sems, recv_sems, local_sem):
        px, py, pc = lax.axis_index("x"), lax.axis_index("y"), lax.axis_index("c")
        me, sibling = (px, py, pc), (px, py, 1 - pc)
        chips = [(1 - px, py), (px, 1 - py), (1 - px, 1 - py)]

        def slot(dx, dy, dc):
            return out_ref.at[4 * dx + 2 * dy + dc]

        def copy(k, block, to, src=None):
            return pltpu.make_async_remote_copy(
                src_ref=slot(*block) if src is None else src, dst_ref=slot(*block),
                send_sem=send_sems.at[k], recv_sem=recv_sems.at[k], device_id=to, device_id_type=_MESH)

        mine = pltpu.make_async_copy(x_ref, slot(*me), local_sem)
        mine.start()
        first = [copy(0, me, sibling, src=x_ref)]
        first += [copy(1 + j, me, (*chip, pc), src=x_ref) for j, chip in enumerate(chips)]
        for cp in first:
            cp.start()
        passed = [copy(4 + j, (*chip, pc), sibling) for j, chip in enumerate(chips)]
        for j, chip in enumerate(chips):
            copy(1 + j, (*chip, pc), me).wait_recv()
            passed[j].start()
        copy(0, sibling, me).wait_recv()
        for j, chip in enumerate(chips):
            copy(4 + j, (*chip, 1 - pc), me).wait_recv()
        for cp in first + passed:
            cp.wait_send()
        mine.wait()

    return pl.pallas_call(
        body, name=name, out_shape=jax.ShapeDtypeStruct((N_DEV, m, n), x.dtype),
        in_specs=[_ANY], out_specs=_ANY,
        scratch_shapes=[pltpu.SemaphoreType.DMA((7,)), pltpu.SemaphoreType.DMA((7,)), pltpu.SemaphoreType.DMA],
    )(x)


def gather_layer_weights(shards0, shards1, name):
    n = len(shards0)

    def body(*refs):
        s0, s1 = refs[0:n], refs[n:2 * n]
        f0, f1 = refs[2 * n:3 * n], refs[3 * n:4 * n]
        own_send, own_recv, ici_send, ici_recv, fwd_send, fwd_recv = refs[4 * n:]
        px, py, pc = lax.axis_index("x"), lax.axis_index("y"), lax.axis_index("c")
        mine = 2 * px + py
        sibling = (px, py, 1 - pc)
        chips = [(1 - px, py), (px, 1 - py), (1 - px, 1 - py)]

        def copy(src, dst, sems_s, sems_r, k, to):
            return pltpu.make_async_remote_copy(src_ref=src, dst_ref=dst, send_sem=sems_s.at[k], recv_sem=sems_r.at[k],
                                                device_id=to, device_id_type=_MESH)

        def run(my_shards, my_full, other_full):
            own = []
            for li, (shards, full) in enumerate(((s0, f0), (s1, f1))):
                for i in range(n):
                    own.append(copy(shards[i], full[i].at[mine], own_send, own_recv, li * n + i, sibling))
            ici = []
            for i in range(n):
                for j, (cx, cy) in enumerate(chips):
                    ici.append(copy(my_shards[i], my_full[i].at[mine], ici_send, ici_recv, 3 * i + j, (cx, cy, pc)))
            for cp in own + ici:
                cp.start()
            fwd = []
            for i in range(n):
                for j, (cx, cy) in enumerate(chips):
                    slot = my_full[i].at[2 * cx + cy]
                    copy(my_shards[i], slot, ici_send, ici_recv, 3 * i + j, (cx, cy, pc)).wait_recv()
                    cp = copy(slot, slot, fwd_send, fwd_recv, 3 * i + j, sibling)
                    cp.start()
                    fwd.append(cp)
            for li, full in enumerate((f0, f1)):
                for i in range(n):
                    copy(s0[i], full[i].at[mine], own_send, own_recv, li * n + i, sibling).wait_recv()
            for i in range(n):
                for j, (cx, cy) in enumerate(chips):
                    slot = other_full[i].at[2 * cx + cy]
                    copy(slot, slot, fwd_send, fwd_recv, 3 * i + j, sibling).wait_recv()
            for cp in own + ici + fwd:
                cp.wait_send()

        @pl.when(pc == 0)
        def _():
            run(s0, f0, f1)

        @pl.when(pc == 1)
        def _():
            run(s1, f1, f0)

    full_shapes = [jax.ShapeDtypeStruct((N_CHIPS,) + v.shape, v.dtype) for v in shards0]
    dma = pltpu.SemaphoreType.DMA
    outs = pl.pallas_call(
        body, name=name, out_shape=full_shapes + full_shapes, in_specs=[_ANY] * (2 * n), out_specs=[_ANY] * (2 * n),
        scratch_shapes=[dma((2 * n,)), dma((2 * n,)), dma((3 * n,)), dma((3 * n,)), dma((3 * n,)), dma((3 * n,))],
    )(*shards0, *shards1)
    return outs[0:n], outs[n:2 * n]


def sibling_swap_layers(arrays0, arrays1, name):
    n = len(arrays0)

    def body(*refs):
        a0, a1, outs = refs[0:n], refs[n:2 * n], refs[2 * n:3 * n]
        send_sems, recv_sems = refs[3 * n:]
        px, py, pc = lax.axis_index("x"), lax.axis_index("y"), lax.axis_index("c")

        def run(send):
            cps = [pltpu.make_async_remote_copy(
                src_ref=send[i], dst_ref=outs[i], send_sem=send_sems.at[i], recv_sem=recv_sems.at[i],
                device_id=(px, py, 1 - pc), device_id_type=_MESH) for i in range(n)]
            for cp in cps:
                cp.start()
            for cp in cps:
                cp.wait()

        @pl.when(pc == 0)
        def _():
            run(a1)

        @pl.when(pc == 1)
        def _():
            run(a0)

    return pl.pallas_call(
        body, name=name, out_shape=[jax.ShapeDtypeStruct(v.shape, v.dtype) for v in arrays0],
        in_specs=[_ANY] * (2 * n), out_specs=[_ANY] * n,
        scratch_shapes=[pltpu.SemaphoreType.DMA((n,)), pltpu.SemaphoreType.DMA((n,))],
    )(*arrays0, *arrays1)


def chip_exchange(arrays, pieces, name):
    n = len(pieces)
    offs = [0]
    for r, _ in pieces:
        offs.append(offs[-1] + r)

    def body(*refs):
        srcs = refs[0:n]
        q_ref, send_sems, recv_sems, local_sems = refs[n:]
        px, py, pc = lax.axis_index("x"), lax.axis_index("y"), lax.axis_index("c")
        mine = 2 * px + py
        chips = [(1 - px, py), (px, 1 - py), (1 - px, 1 - py)]
        locals_, sends = [], []
        for i, (r, stride) in enumerate(pieces):
            dst = pl.ds(offs[i], r)
            lc = pltpu.make_async_copy(srcs[i].at[pl.ds(mine * stride, r)], q_ref.at[mine, dst], local_sems.at[i])
            lc.start()
            locals_.append(lc)
            for j, (cx, cy) in enumerate(chips):
                cp = pltpu.make_async_remote_copy(
                    src_ref=srcs[i].at[pl.ds((2 * cx + cy) * stride, r)], dst_ref=q_ref.at[mine, dst],
                    send_sem=send_sems.at[3 * i + j], recv_sem=recv_sems.at[3 * i + j], device_id=(cx, cy, pc), device_id_type=_MESH)
                cp.start()
                sends.append(cp)
        for i, (r, stride) in enumerate(pieces):
            dst = pl.ds(offs[i], r)
            for j, (cx, cy) in enumerate(chips):
                pltpu.make_async_remote_copy(
                    src_ref=srcs[i].at[pl.ds(mine * stride, r)], dst_ref=q_ref.at[2 * cx + cy, dst],
                    send_sem=send_sems.at[3 * i + j], recv_sem=recv_sems.at[3 * i + j], device_id=(cx, cy, pc),
                    device_id_type=_MESH).wait_recv()
        for cp in sends:
            cp.wait_send()
        for lc in locals_:
            lc.wait()

    return pl.pallas_call(
        body, name=name, out_shape=jax.ShapeDtypeStruct((N_CHIPS, offs[-1], arrays[0].shape[1]), arrays[0].dtype),
        in_specs=[_ANY] * n, out_specs=_ANY,
        scratch_shapes=[pltpu.SemaphoreType.DMA((3 * n,)), pltpu.SemaphoreType.DMA((3 * n,)), pltpu.SemaphoreType.DMA((n,))],
    )(*arrays)


def sibling_swap(x, name):
    def body(x_ref, out_ref, send_sem, recv_sem):
        px, py, pc = lax.axis_index("x"), lax.axis_index("y"), lax.axis_index("c")
        cp = pltpu.make_async_remote_copy(
            src_ref=x_ref, dst_ref=out_ref, send_sem=send_sem, recv_sem=recv_sem,
            device_id=(px, py, 1 - pc), device_id_type=_MESH)
        cp.start()
        cp.wait()

    return pl.pallas_call(
        body, name=name, out_shape=jax.ShapeDtypeStruct(x.shape, x.dtype), in_specs=[_ANY], out_specs=_ANY,
        scratch_shapes=[pltpu.SemaphoreType.DMA, pltpu.SemaphoreType.DMA],
    )(x)


ELT_TILE = 128


def _elt_rows(m):
    for t in (512, 256, ELT_TILE, 16, 8):
        if m % t == 0:
            return t
    return m


def pair_add(a0, a1, b, name):
    m, n = b.shape
    tm = _elt_rows(m)

    def body(a0_ref, a1_ref, b_ref, o_ref):
        mine = jnp.where(lax.axis_index("c") == 0, a0_ref[...], a1_ref[...])
        o_ref[...] = (mine.astype(F32) + b_ref[...].astype(F32)).astype(o_ref.dtype)

    return pl.pallas_call(
        body, name=name, grid=(m // tm,), in_specs=[_row_spec(tm, n)] * 3, out_specs=_row_spec(tm, n),
        out_shape=jax.ShapeDtypeStruct((m, n), b.dtype), compiler_params=_params("arbitrary"),
    )(a0, a1, b)


def sum_leading(q, name):
    kk, m, n = q.shape
    tm = _elt_rows(m)

    def body(q_ref, o_ref):
        acc = q_ref[0].astype(F32)
        for i in range(1, kk):
            acc = acc + q_ref[i].astype(F32)
        o_ref[...] = acc

    return pl.pallas_call(
        body, name=name, grid=(m // tm,), in_specs=[pl.BlockSpec((kk, tm, n), lambda i: (0, i, 0))],
        out_specs=_row_spec(tm, n), out_shape=jax.ShapeDtypeStruct((m, n), F32), compiler_params=_params("arbitrary"),
    )(q)


def adamw(w, g, m, v, name):
    rows, cols = w.shape
    tm = _elt_rows(rows)

    def body(w_ref, g_ref, m_ref, v_ref, d_ref, nm_ref, nv_ref):
        gv = g_ref[...]
        nm = ADAM_B1 * m_ref[...] + (1.0 - ADAM_B1) * gv
        nv = ADAM_B2 * v_ref[...] + (1.0 - ADAM_B2) * jnp.square(gv)
        nm_ref[...] = nm
        nv_ref[...] = nv
        m_hat = nm / (1.0 - ADAM_B1 ** ADAM_STEP)
        v_hat = nv / (1.0 - ADAM_B2 ** ADAM_STEP)
        d_ref[...] = -ADAM_LR * (m_hat / (jnp.sqrt(v_hat) + ADAM_EPS) + ADAM_WD * w_ref[...])

    spec = _row_spec(tm, cols)
    return pl.pallas_call(
        body, name=name, grid=(rows // tm,), in_specs=[spec] * 4, out_specs=[spec] * 3,
        out_shape=[jax.ShapeDtypeStruct((rows, cols), F32)] * 3, compiler_params=_params("arbitrary"),
    )(w, g, m, v)


def _block_diag_dense(w):
    g = w.shape[0]
    return jnp.einsum("gij,gh->gihj", w, jnp.eye(g, dtype=w.dtype)).reshape(g * w.shape[1], g * w.shape[2])


def _diag_blocks(m):
    return jnp.stack([m[HEAD * i:HEAD * (i + 1), HEAD * i:HEAD * (i + 1)] for i in range(LRU_BLOCKS)])


def _rep(v):
    return jnp.repeat(v, HEAD, axis=-1)


def _split_w_in(w_in):
    gdn0 = RET_IN + LRU_IN
    gdn1 = gdn0 + 4 * GDN_W
    w_r = w_in[:, 0:RET_IN]
    w_l = w_in[:, RET_IN:gdn0]
    w_g = jnp.concatenate([w_in[:, gdn0:gdn1], _rep(w_in[:, gdn1:gdn1 + GDN_HEADS]), _rep(w_in[:, gdn1 + GDN_HEADS:])], axis=1)
    w_ab = jnp.pad(w_in[:, gdn1:], ((0, 0), (0, LANES - 2 * GDN_HEADS)))
    return w_r, w_l, w_g, w_ab


WIN_SHARD = D_IN // N_CHIPS
WIN_STRIDE = 832
WIN_ROWS = 960
WIN_T_ROWS = WIN_STRIDE * (N_CHIPS - 1) + WIN_ROWS
AB_ROWS = 16

_GRAD_PIECES = (("ffn1_w_gate", 704, 704), ("ffn1_w_up", 704, 704), ("ffn1_w_down", 704, 704), ("w_in", WIN_ROWS, WIN_STRIDE),
                ("w_out", 256, 256), ("ffn2_w_gate", 704, 704), ("ffn2_w_up", 704, 704), ("ffn2_w_down", 704, 704),
                ("ple_w_gate", 256, 256), ("ple_w_proj", 64, 64))
_TRANSPOSED = ("ffn1_w_gate", "ffn1_w_up", "w_in", "ffn2_w_gate", "ffn2_w_up", "ple_w_proj")


def _local_step(x, p, pos, target, wt):
    row = lambda v: v[None, :]
    saved = []
    xb = x.astype(BF16)
    for i in range(DEPTH):
        w_r, w_l, w_g, w_ab = _split_w_in(wt["w_in"][i])
        lw = dict(
            wg1=wt["ffn1_w_gate"][i], wu1=wt["ffn1_w_up"][i], wd1=wt["ffn1_w_down"][i], w_r=w_r, w_l=w_l, w_g=w_g, w_ab=w_ab,
            w_out=wt["w_out"][i], wg2=wt["ffn2_w_gate"][i], wu2=wt["ffn2_w_up"][i], wd2=wt["ffn2_w_down"][i],
            wpg=wt["ple_w_gate"][i], wpp=wt["ple_w_proj"][i],
            wa=_block_diag_dense(wt["lru_w_a"][i]), wx=_block_diag_dense(wt["lru_w_x"][i]),
            al=row(_rep(wt["gdn_a_log"][i])), dt=row(_rep(wt["gdn_dt_bias"][i])), ng=row(jnp.tile(wt["gdn_norm_g"][i], GDN_HEADS)))
        hg1, hu1, r1, x1, x1b = ffn_fwd(x, row(wt["ln_ffn1_g"][i]), row(wt["ln_ffn1_b"][i]), lw["wg1"], lw["wu1"], lw["wd1"])
        hr, hl, hgd = win_fwd(x1, w_r, w_l, w_g)
        o_r, opre_r, st_r = ret_fwd(hr, pos, row(wt["ret_norm_g"][i]))
        o_l, xc, hs = lru_fwd(hl, wt["lru_conv_w"][i], row(wt["lru_conv_b"][i]), lw["wa"], row(wt["lru_b_a"][i]), lw["wx"],
                              row(wt["lru_b_x"][i]), row(wt["lru_lambda"][i]))
        o_g, opre_g, tmat, st_g = gdn_fwd(hgd, wt["gdn_conv_w"][i], lw["al"], lw["dt"], lw["ng"])
        r2, x2, x2b, ocat = out_fwd(o_r, o_l, o_g, x1, lw["w_out"], row(wt["ln_mix_g"][i]), row(wt["ln_mix_b"][i]))
        hg2, hu2, r3, x3, x3b, pg, pp = ffn_fwd(x2, row(wt["ln_ffn2_g"][i]), row(wt["ln_ffn2_b"][i]), lw["wg2"], lw["wu2"],
                                                lw["wd2"], ple=(p[i], lw["wpg"], lw["wpp"]))
        saved.append(dict(lw=lw, x0=xb, hg1=hg1, hu1=hu1, r1=r1, x1=x1b, hr=hr, hl=hl, hgd=hgd, ocat=ocat, opre_r=opre_r,
                          st_r=st_r, xc=xc, hs=hs, opre_g=opre_g, tmat=tmat, st_g=st_g, r2=r2, x2=x2b, hg2=hg2, hu2=hu2,
                          r3=r3, pg=pg, pp=pp))
        x, xb = x3, x3b

    dx, loss = loss_and_grad(x, target)
    grads = [None] * DEPTH
    big = [None] * DEPTH
    for i in reversed(range(DEPTH)):
        sv = saved[i]
        lw = sv["lw"]
        tag = f"_l{i}"
        dx2, act2, dhg2, dhu2, dy2, dg3, db3, dpg, dpp = ffn_bwd(
            dx, sv["r3"], sv["x2"], sv["hg2"], sv["hu2"], row(wt["ln_ffn2_g"][i]), lw["wg2"], lw["wu2"], lw["wd2"],
            ple=(sv["pg"], sv["pp"], lw["wpg"]))
        g, bg = {}, {}
        bg["ffn2_w_gate"] = wgrad(dhg2, sv["x2"], "wgrad_gate2" + tag)
        bg["ffn2_w_up"] = wgrad(dhu2, sv["x2"], "wgrad_up2" + tag)
        bg["ffn2_w_down"] = wgrad(act2, dy2, "wgrad_down2" + tag)
        bg["ple_w_gate"] = wgrad(sv["x2"], dpg, "wgrad_pgate" + tag)
        bg["ple_w_proj"] = wgrad(dpp, p[i], "wgrad_pproj" + tag).reshape(PLE_DIM, D_MODEL)
        g["ln_ffn2_g"], g["ln_ffn2_b"] = dg3[0], db3[0]
        dr2, dr2b, do_r, do_l, do_g, dg2, db2 = out_bwd(dx2, sv["r2"], row(wt["ln_mix_g"][i]), lw["w_out"])
        g["ln_mix_g"], g["ln_mix_b"] = dg2[0], db2[0]
        bg["w_out"] = wgrad(sv["ocat"], dr2b, "wgrad_out" + tag)
        dhr, dgn = ret_bwd(sv["hr"], pos, row(wt["ret_norm_g"][i]), sv["opre_r"], sv["st_r"], do_r)
        g["ret_norm_g"] = dgn[0]
        dhl, dcw, dcb, dwa, dba, dwx, dbx, dlam = lru_bwd(
            sv["hl"], wt["lru_conv_w"][i], row(wt["lru_conv_b"][i]), lw["wa"], row(wt["lru_b_a"][i]), lw["wx"],
            row(wt["lru_b_x"][i]), row(wt["lru_lambda"][i]), sv["xc"], sv["hs"], do_l)
        g["lru_conv_w"], g["lru_conv_b"] = dcw, dcb[0]
        g["lru_w_a"], g["lru_b_a"], g["lru_w_x"], g["lru_b_x"], g["lru_lambda"] = _diag_blocks(dwa), dba[0], _diag_blocks(dwx), dbx[0], dlam[0]
        dhq, dab, dgcw, dal, ddt, dng = gdn_bwd(sv["hgd"], wt["gdn_conv_w"][i], lw["al"], lw["dt"], lw["ng"], sv["opre_g"],
                                                sv["tmat"], sv["st_g"], do_g)
        g["gdn_conv_w"] = dgcw
        g["gdn_a_log"], g["gdn_dt_bias"] = dal[0, ::HEAD], ddt[0, ::HEAD]
        g["gdn_norm_g"] = dng[0].reshape(GDN_HEADS, HEAD).sum(0)
        dx1 = win_bwd(dr2, dhr, dhl, dhq, dab, lw["w_r"], lw["w_l"], lw["w_g"], lw["w_ab"])
        used = RET_IN + LRU_IN + 4 * GDN_W + AB_ROWS
        bg["w_in"] = jnp.concatenate(
            [wgrad(dhr, sv["x1"], "wgrad_in_r" + tag), wgrad(dhl, sv["x1"], "wgrad_in_l" + tag),
             wgrad(dhq, sv["x1"], "wgrad_in_q" + tag), wgrad(dab, sv["x1"], "wgrad_in_ab" + tag)[0:AB_ROWS],
             jnp.zeros((WIN_T_ROWS - used, D_MODEL), BF16)], axis=0)
        dx, act1, dhg1, dhu1, dy1, dg1, db1 = ffn_bwd(dx1, sv["r1"], sv["x0"], sv["hg1"], sv["hu1"], row(wt["ln_ffn1_g"][i]),
                                                      lw["wg1"], lw["wu1"], lw["wd1"])
        bg["ffn1_w_gate"] = wgrad(dhg1, sv["x0"], "wgrad_gate1" + tag)
        bg["ffn1_w_up"] = wgrad(dhu1, sv["x0"], "wgrad_up1" + tag)
        bg["ffn1_w_down"] = wgrad(act1, dy1, "wgrad_down1" + tag)
        g["ln_ffn1_g"], g["ln_ffn1_b"] = dg1[0], db1[0]
        grads[i] = g
        big[i] = bg
    return loss, dx, {k: jnp.stack([grads[i][k] for i in range(DEPTH)]) for k in grads[0]}, big


def _natural_grad(name, rows):
    if name == "ple_w_proj":
        return rows.reshape(-1, PLE_DIM).T
    return rows.T if name in _TRANSPOSED else rows


_SPLIT = dict(ffn1_w_gate=2, ffn1_w_up=2, ffn1_w_down=1, w_in=2, w_out=1, ffn2_w_gate=2, ffn2_w_up=2, ffn2_w_down=1,
              ple_w_gate=1, ple_w_proj=2)
_CONV = ("lru_conv_w", "gdn_conv_w")
_WHOLE = ("ln_ffn1_g", "ln_ffn1_b", "ret_norm_g", "lru_conv_b", "lru_w_a", "lru_b_a", "lru_w_x", "lru_b_x", "lru_lambda",
          "gdn_a_log", "gdn_dt_bias", "gdn_norm_g", "ln_mix_g", "ln_mix_b", "ln_ffn2_g", "ln_ffn2_b")
_WEIGHTS = ("ln_ffn1_g", "ln_ffn1_b", "ffn1_w_gate", "ffn1_w_up", "ffn1_w_down", "w_in", "ret_norm_g", "lru_conv_w", "lru_conv_b",
            "lru_w_a", "lru_b_a", "lru_w_x", "lru_b_x", "lru_lambda", "gdn_conv_w", "gdn_a_log", "gdn_dt_bias", "gdn_norm_g",
            "w_out", "ln_mix_g", "ln_mix_b", "ffn2_w_gate", "ffn2_w_up", "ffn2_w_down", "ple_w_gate", "ple_w_proj",
            "ln_ffn2_g", "ln_ffn2_b")
_INPUTS = ("x", "p", "positions") + _WEIGHTS + ("loss_target",) + tuple("m_" + n for n in _WEIGHTS) + tuple("v_" + n for n in _WEIGHTS)

BIG_COLS = 1024
SMALL_COLS = LANES
SMALL_ROWS_MULT = 8


def _pack(arrays, dtype, cols, rows_mult):
    flat = jnp.concatenate([a.reshape(-1).astype(dtype) for a in arrays])
    rows = -(-flat.shape[0] // cols)
    rows = -(-rows // rows_mult) * rows_mult
    return jnp.pad(flat, (0, rows * cols - flat.shape[0])).reshape(rows, cols)


def _unpack(packed, shapes):
    flat = packed.reshape(-1)
    out, off = [], 0
    for shp in shapes:
        size = int(np.prod(shp))
        out.append(flat[off:off + size].reshape(shp))
        off += size
    return out


def _as2d(a):
    return a.reshape(-1, a.shape[-1])


def kernel(x, p, positions, ln_ffn1_g, ln_ffn1_b, ffn1_w_gate, ffn1_w_up, ffn1_w_down, w_in, ret_norm_g, lru_conv_w, lru_conv_b, lru_w_a, lru_b_a, lru_w_x, lru_b_x, lru_lambda, gdn_conv_w, gdn_a_log, gdn_dt_bias, gdn_norm_g, w_out, ln_mix_g, ln_mix_b, ffn2_w_gate, ffn2_w_up, ffn2_w_down, ple_w_gate, ple_w_proj, ln_ffn2_g, ln_ffn2_b, loss_target, m_ln_ffn1_g, m_ln_ffn1_b, m_ffn1_w_gate, m_ffn1_w_up, m_ffn1_w_down, m_w_in, m_ret_norm_g, m_lru_conv_w, m_lru_conv_b, m_lru_w_a, m_lru_b_a, m_lru_w_x, m_lru_b_x, m_lru_lambda, m_gdn_conv_w, m_gdn_a_log, m_gdn_dt_bias, m_gdn_norm_g, m_w_out, m_ln_mix_g, m_ln_mix_b, m_ffn2_w_gate, m_ffn2_w_up, m_ffn2_w_down, m_ple_w_gate, m_ple_w_proj, m_ln_ffn2_g, m_ln_ffn2_b, v_ln_ffn1_g, v_ln_ffn1_b, v_ffn1_w_gate, v_ffn1_w_up, v_ffn1_w_down, v_w_in, v_ret_norm_g, v_lru_conv_w, v_lru_conv_b, v_lru_w_a, v_lru_b_a, v_lru_w_x, v_lru_b_x, v_lru_lambda, v_gdn_conv_w, v_gdn_a_log, v_gdn_dt_bias, v_gdn_norm_g, v_w_out, v_ln_mix_g, v_ln_mix_b, v_ffn2_w_gate, v_ffn2_w_up, v_ffn2_w_down, v_ple_w_gate, v_ple_w_proj, v_ln_ffn2_g, v_ln_ffn2_b):
    a = dict(zip(_INPUTS, (x, p, positions, ln_ffn1_g, ln_ffn1_b, ffn1_w_gate, ffn1_w_up, ffn1_w_down, w_in, ret_norm_g, lru_conv_w, lru_conv_b, lru_w_a, lru_b_a, lru_w_x, lru_b_x, lru_lambda, gdn_conv_w, gdn_a_log, gdn_dt_bias, gdn_norm_g, w_out, ln_mix_g, ln_mix_b, ffn2_w_gate, ffn2_w_up, ffn2_w_down, ple_w_gate, ple_w_proj, ln_ffn2_g, ln_ffn2_b, loss_target, m_ln_ffn1_g, m_ln_ffn1_b, m_ffn1_w_gate, m_ffn1_w_up, m_ffn1_w_down, m_w_in, m_ret_norm_g, m_lru_conv_w, m_lru_conv_b, m_lru_w_a, m_lru_b_a, m_lru_w_x, m_lru_b_x, m_lru_lambda, m_gdn_conv_w, m_gdn_a_log, m_gdn_dt_bias, m_gdn_norm_g, m_w_out, m_ln_mix_g, m_ln_mix_b, m_ffn2_w_gate, m_ffn2_w_up, m_ffn2_w_down, m_ple_w_gate, m_ple_w_proj, m_ln_ffn2_g, m_ln_ffn2_b, v_ln_ffn1_g, v_ln_ffn1_b, v_ffn1_w_gate, v_ffn1_w_up, v_ffn1_w_down, v_w_in, v_ret_norm_g, v_lru_conv_w, v_lru_conv_b, v_lru_w_a, v_lru_b_a, v_lru_w_x, v_lru_b_x, v_lru_lambda, v_gdn_conv_w, v_gdn_a_log, v_gdn_dt_bias, v_gdn_norm_g, v_w_out, v_ln_mix_g, v_ln_mix_b, v_ffn2_w_gate, v_ffn2_w_up, v_ffn2_w_down, v_ple_w_gate, v_ple_w_proj, v_ln_ffn2_g, v_ln_ffn2_b)))
    assert len(a) == len(_INPUTS)
    core = lax.axis_index("c")
    chip = 2 * lax.axis_index("x") + lax.axis_index("y")
    big = list(_SPLIT)

    full = gather_layer_weights([a[n][0].astype(BF16) for n in big], [a[n][1].astype(BF16) for n in big], "gather_weights")
    wt = {}
    for i, n in enumerate(big):
        wt[n] = [jnp.concatenate([full[layer][i][k] for k in range(N_CHIPS)], axis=_SPLIT[n] - 1) for layer in range(DEPTH)]
    conv_g = all_gather8(_pack([a[n] for n in _CONV], F32, SMALL_COLS, SMALL_ROWS_MULT), "gather_conv_weights")[0::2]
    conv_g = conv_g.reshape(N_CHIPS, -1)
    off = 0
    for n in _CONV:
        shp = a[n].shape
        size = int(np.prod(shp))
        parts = conv_g[:, off:off + size].reshape((N_CHIPS,) + shp)
        wt[n] = jnp.concatenate([parts[k] for k in range(N_CHIPS)], axis=2)
        off += size
    for n in _WHOLE:
        wt[n] = a[n]

    seq = a["x"].shape[1]
    loss_part, dx, grads, big_src = _local_step(a["x"][0], a["p"][:, 0], a["positions"].reshape(seq, 1), a["loss_target"][0], wt)
    loss = lax.psum(loss_part[0, 0], ("x", "y", "c"))

    pieces = [(r, stride) for _, r, stride in _GRAD_PIECES]
    names = [n for n, _, _ in _GRAD_PIECES]
    layer0 = [big_src[0][n] for n in names]
    layer1 = [big_src[1][n] for n in names]
    from_sibling = sibling_swap_layers(layer0, layer1, "reduce_pair_swap")
    pair = [pair_add(u, v, w, "reduce_pair_add_" + n) for n, u, v, w in zip(names, layer0, layer1, from_sibling)]
    arrived = chip_exchange(pair, pieces, "reduce_chip_exchange")
    my_layer_sum = sum_leading(arrived, "reduce_chip_sum")
    other_layer_sum = sibling_swap(my_layer_sum, "reduce_pair_share")
    reduced = [jnp.where(core == layer, my_layer_sum, other_layer_sum) for layer in range(DEPTH)]
    big_grads = {}
    off = 0
    for n, r, _ in _GRAD_PIECES:
        per_layer = []
        for layer in range(DEPTH):
            rows = reduced[layer][off:off + r]
            if n == "w_in":
                rows = lax.dynamic_slice_in_dim(rows, chip * (WIN_SHARD - WIN_STRIDE), WIN_SHARD, axis=0)
            per_layer.append(_natural_grad(n, rows))
        big_grads[n] = jnp.stack(per_layer)
        off += r

    small_names = list(_WHOLE) + list(_CONV)
    small_local = _pack([grads[n] for n in small_names], F32, SMALL_COLS, SMALL_ROWS_MULT)
    small_sum = sum_leading(all_gather8(small_local, "gather_small_grads"), "sum_small_grads")
    small_grads = dict(zip(small_names, _unpack(small_sum, [grads[n].shape for n in small_names])))
    for n in _CONV:
        width = a[n].shape[2]
        small_grads[n] = lax.dynamic_slice_in_dim(small_grads[n], chip * width, width, axis=2)

    new = {}
    for n in big:
        d, nm, nv = adamw(_as2d(a[n]), _as2d(big_grads[n]), _as2d(a["m_" + n]), _as2d(a["v_" + n]), "adamw_" + n)
        new[n] = tuple(t.reshape(a[n].shape) for t in (d, nm, nv))
    pk = lambda prefix: _pack([a[prefix + n] for n in small_names], F32, SMALL_COLS, SMALL_ROWS_MULT)
    pg = _pack([small_grads[n] for n in small_names], F32, SMALL_COLS, SMALL_ROWS_MULT)
    outs = adamw(pk(""), pg, pk("m_"), pk("v_"), "adamw_small")
    shapes = [a[n].shape for n in small_names]
    for n, d, nm, nv in zip(small_names, *[_unpack(o, shapes) for o in outs]):
        new[n] = (d, nm, nv)
    all_grads = {**big_grads, **small_grads}
    return (loss, dx[None], *[all_grads[n] for n in _WEIGHTS], *[new[n][0] for n in _WEIGHTS],
            *[new[n][1] for n in _WEIGHTS], *[new[n][2] for n in _WEIGHTS])
```

```python
import functools
import math

import numpy as np
import jax
import jax.numpy as jnp
from jax import lax
from jax.experimental import pallas as pl
from jax.experimental.pallas import tpu as pltpu

F32 = jnp.float32
BF16 = jnp.bfloat16

D_MODEL = 1024
D_FF = 2816
PLE_DIM = 256
DEPTH = 2
CHUNK = 64
RET_HEADS = 4
RET_W = 256
LRU_W = 384
LRU_BLOCKS = 6
GDN_HEADS = 6
GDN_W = 384
HEAD = 64
D_IN = 3340
RET_IN = 4 * RET_W
LRU_IN = 2 * LRU_W
GDN_IN = 6 * GDN_W
ROPE_THETA = 10000.0
ALPHA = (2 * DEPTH) ** 0.25
LN_EPS = 1e-5
LRU_C = 8.0
N_CHIPS = 4
N_DEV = 8

ADAM_LR = 0.001
ADAM_B1 = 0.9
ADAM_B2 = 0.999
ADAM_EPS = 1e-08
ADAM_WD = 0.01
ADAM_STEP = 10

LANES = 128
VMEM_LIMIT = 56 * 1024 * 1024
ROW_TILE = 256
ROW_TILE_BWD = 512
SCAN_TILE = 256


def _params(*sem):
    return pltpu.CompilerParams(dimension_semantics=sem, vmem_limit_bytes=VMEM_LIMIT)


def _mm(a, b):
    return jnp.dot(a, b, preferred_element_type=F32)


def _mm_nt(a, b):
    return lax.dot_general(a, b, (((1,), (1,)), ((), ())), preferred_element_type=F32)


def _mm_tn(a, b):
    return lax.dot_general(a, b, (((0,), (0,)), ((), ())), preferred_element_type=F32)


def _split(a):
    hi = a.astype(BF16)
    lo = (a - hi.astype(F32)).astype(BF16)
    return hi, lo


def _mm3(a, b):
    ah, al = _split(a)
    bh, bl = _split(b)
    return _mm(ah, bh) + (_mm(ah, bl) + _mm(al, bh))


def _sigmoid(x):
    return jax.nn.sigmoid(x)


def _log1p(u):
    w = 1.0 + u
    return jnp.where(w == 1.0, u, jnp.log(w) * (u / jnp.where(w == 1.0, 1.0, w - 1.0)))


def _expm1(y):
    u = jnp.exp(y)
    um1 = u - 1.0
    safe = jnp.where((u == 1.0) | (um1 == -1.0), 1.0, jnp.log(jnp.where(u == 0.0, 1.0, u)))
    return jnp.where(u == 1.0, y, jnp.where(um1 == -1.0, -1.0, um1 * (y / safe)))


def _softplus(x):
    return jnp.maximum(x, 0.0) + _log1p(jnp.exp(-jnp.abs(x)))


_GELU_C = math.sqrt(2.0 / math.pi)


def _gelu(x):
    return 0.5 * x * (1.0 + jnp.tanh(_GELU_C * (x + 0.044715 * (x * x * x))))


def _gelu_grad(x):
    t = jnp.tanh(_GELU_C * (x + 0.044715 * (x * x * x)))
    return 0.5 * (1.0 + t) + 0.5 * x * (1.0 - t * t) * (_GELU_C * (1.0 + 3.0 * 0.044715 * (x * x)))


def _silu_and_grad(x):
    s = _sigmoid(x)
    return x * s, s * (1.0 + x * (1.0 - s))


def _group_sum_slab(x):
    lane = lax.broadcasted_iota(jnp.int32, x.shape, 1)
    low = jnp.sum(x[:, 0:LANES // 2], axis=1, keepdims=True)
    high = jnp.sum(x[:, LANES // 2:], axis=1, keepdims=True)
    return jnp.where(lane < LANES // 2, low, high)


def _group_sum(x):
    n = x.shape[1] // LANES
    if n == 1:
        return _group_sum_slab(x)
    return jnp.concatenate([_group_sum_slab(x[:, LANES * i:LANES * (i + 1)]) for i in range(n)], axis=1)


def _rows_prefix_sum(x):
    n = x.shape[0]
    row = lax.broadcasted_iota(jnp.int32, x.shape, 0)
    d = 1
    while d < n:
        x = x + jnp.where(row >= d, pltpu.roll(x, d, 0), 0.0)
        d *= 2
    return x


def _rows_suffix_sum(x):
    n = x.shape[0]
    row = lax.broadcasted_iota(jnp.int32, x.shape, 0)
    d = 1
    while d < n:
        x = x + jnp.where(row < n - d, pltpu.roll(x, n - d, 0), 0.0)
        d *= 2
    return x


def _shift_rows(cur, prev, j):
    row = lax.broadcasted_iota(jnp.int32, cur.shape, 0)
    return jnp.where(row < j, pltpu.roll(prev, j, 0), pltpu.roll(cur, j, 0))


def _shift_rows_up(cur, nxt, j):
    n = cur.shape[0]
    row = lax.broadcasted_iota(jnp.int32, cur.shape, 0)
    return jnp.where(row < n - j, pltpu.roll(cur, n - j, 0), pltpu.roll(nxt, n - j, 0))


def _layer_norm_stats(r):
    mu = jnp.mean(r, axis=-1, keepdims=True)
    d = r - mu
    var = jnp.mean(d * d, axis=-1, keepdims=True)
    rstd = lax.rsqrt(var + LN_EPS)
    return d * rstd, rstd


def _load_resident(step, pairs, sems):
    @pl.when(step == 0)
    def _():
        cps = [pltpu.make_async_copy(h, v, sems.at[i]) for i, (h, v) in enumerate(pairs)]
        for c in cps:
            c.start()
        for c in cps:
            c.wait()


def _row_spec(tile, width):
    return pl.BlockSpec((tile, width), lambda i: (i, 0))


def _full_spec(shape):
    nd = len(shape)
    return pl.BlockSpec(shape, lambda i: (0,) * nd)


_ANY = pl.BlockSpec(memory_space=pl.ANY)


def ffn_fwd(x, ln_g, ln_b, w_gate, w_up, w_down, ple=None):
    s = x.shape[0]
    tm = ROW_TILE
    with_ple = ple is not None
    weights = [w_gate, w_up, w_down] + ([ple[1], ple[2]] if with_ple else [])

    def body(*refs):
        it = iter(refs)
        x_ref, g_ref, b_ref = next(it), next(it), next(it)
        p_ref = next(it) if with_ple else None
        w_hbm = [next(it) for _ in weights]
        hg_ref, hu_ref, r_ref, xn_ref, xnb_ref = next(it), next(it), next(it), next(it), next(it)
        pg_ref, pp_ref = (next(it), next(it)) if with_ple else (None, None)
        w_vm = [next(it) for _ in weights]
        sems = next(it)
        _load_resident(pl.program_id(0), list(zip(w_hbm, w_vm)), sems)
        xv = x_ref[...]
        xb = xv.astype(BF16)
        hg = _mm(xb, w_vm[0][...])
        hu = _mm(xb, w_vm[1][...])
        hg_ref[...] = hg
        hu_ref[...] = hu
        act = (hg * _sigmoid(hg)) * hu
        r = ALPHA * xv + 0.5 * _mm(act.astype(BF16), w_vm[2][...])
        if with_ple:
            pg = _mm(xb, w_vm[3][...])
            pp = _mm(p_ref[...].astype(BF16), w_vm[4][...])
            pg_ref[...] = pg
            pp_ref[...] = pp
            r = r + _sigmoid(pg) * pp
        r_ref[...] = r
        xhat, _ = _layer_norm_stats(r)
        xn = xhat * g_ref[...] + b_ref[...]
        xn_ref[...] = xn
        xnb_ref[...] = xn.astype(BF16)

    d, f = D_MODEL, D_FF
    in_specs = [_row_spec(tm, d), _full_spec((1, d)), _full_spec((1, d))]
    args = [x, ln_g, ln_b]
    if with_ple:
        in_specs.append(_row_spec(tm, PLE_DIM))
        args.append(ple[0])
    in_specs += [_ANY] * len(weights)
    args += weights
    out_shape = [jax.ShapeDtypeStruct((s, f), F32), jax.ShapeDtypeStruct((s, f), F32),
                 jax.ShapeDtypeStruct((s, d), F32), jax.ShapeDtypeStruct((s, d), F32), jax.ShapeDtypeStruct((s, d), BF16)]
    out_specs = [_row_spec(tm, f), _row_spec(tm, f), _row_spec(tm, d), _row_spec(tm, d), _row_spec(tm, d)]
    if with_ple:
        out_shape += [jax.ShapeDtypeStruct((s, d), F32)] * 2
        out_specs += [_row_spec(tm, d)] * 2
    scratch = [pltpu.VMEM(w.shape, w.dtype) for w in weights] + [pltpu.SemaphoreType.DMA((len(weights),))]
    return pl.pallas_call(
        body, name="ffn_fwd_ple" if with_ple else "ffn_fwd", grid=(s // tm,), in_specs=in_specs, out_specs=out_specs,
        out_shape=out_shape, scratch_shapes=scratch, compiler_params=_params("arbitrary"),
    )(*args)


def ffn_bwd(dxn, r, x, hg, hu, ln_g, w_gate, w_up, w_down, ple=None):
    s = x.shape[0]
    with_ple = ple is not None
    d, f = D_MODEL, D_FF
    suffix = "_ple" if with_ple else ""

    tm = ROW_TILE

    def body_a(*refs):
        it = iter(refs)
        dxn_ref, r_ref, hg_ref, hu_ref, g_ref = (next(it) for _ in range(5))
        pg_ref, pp_ref = (next(it), next(it)) if with_ple else (None, None)
        wd_hbm = next(it)
        dr_ref, act_ref, dhg_ref, dhu_ref, dy_ref, dg_ref, db_ref = (next(it) for _ in range(7))
        dpg_ref, dpp_ref = (next(it), next(it)) if with_ple else (None, None)
        wd_vm, sems = next(it), next(it)
        step = pl.program_id(0)
        _load_resident(step, [(wd_hbm, wd_vm)], sems)

        @pl.when(step == 0)
        def _():
            dg_ref[...] = jnp.zeros_like(dg_ref)
            db_ref[...] = jnp.zeros_like(db_ref)

        dxn_v = dxn_ref[...]
        xhat, rstd = _layer_norm_stats(r_ref[...])
        dg_ref[...] += jnp.sum(dxn_v * xhat, axis=0, keepdims=True)
        db_ref[...] += jnp.sum(dxn_v, axis=0, keepdims=True)
        dyh = dxn_v * g_ref[...]
        dr = rstd * (dyh - jnp.mean(dyh, axis=-1, keepdims=True) - xhat * jnp.mean(dyh * xhat, axis=-1, keepdims=True))
        dr_ref[...] = dr
        dy = (0.5 * dr).astype(BF16)
        dy_ref[...] = dy
        da = _mm_nt(dy, wd_vm[...])
        hg_v = hg_ref[...]
        hu_v = hu_ref[...]
        sil, dsil = _silu_and_grad(hg_v)
        act_ref[...] = (sil * hu_v).astype(BF16)
        dhu_ref[...] = (da * sil).astype(BF16)
        dhg_ref[...] = (da * hu_v * dsil).astype(BF16)
        if with_ple:
            sp = _sigmoid(pg_ref[...])
            dpp_ref[...] = (dr * sp).astype(BF16)
            dpg_ref[...] = (dr * pp_ref[...] * sp * (1.0 - sp)).astype(BF16)

    in_specs = [_row_spec(tm, d), _row_spec(tm, d), _row_spec(tm, f), _row_spec(tm, f), _full_spec((1, d))]
    args = [dxn, r, hg, hu, ln_g]
    if with_ple:
        in_specs += [_row_spec(tm, d), _row_spec(tm, d)]
        args += [ple[0], ple[1]]
    out_shape = [jax.ShapeDtypeStruct((s, d), F32), jax.ShapeDtypeStruct((s, f), BF16), jax.ShapeDtypeStruct((s, f), BF16),
                 jax.ShapeDtypeStruct((s, f), BF16), jax.ShapeDtypeStruct((s, d), BF16),
                 jax.ShapeDtypeStruct((1, d), F32), jax.ShapeDtypeStruct((1, d), F32)]
    out_specs = [_row_spec(tm, d), _row_spec(tm, f), _row_spec(tm, f), _row_spec(tm, f), _row_spec(tm, d),
                 _full_spec((1, d)), _full_spec((1, d))]
    if with_ple:
        out_shape += [jax.ShapeDtypeStruct((s, d), BF16)] * 2
        out_specs += [_row_spec(tm, d)] * 2
    first = pl.pallas_call(
        body_a, name="ffn_bwd_hidden" + suffix, grid=(s // tm,), in_specs=in_specs + [_ANY], out_specs=out_specs,
        out_shape=out_shape, scratch_shapes=[pltpu.VMEM(w_down.shape, w_down.dtype), pltpu.SemaphoreType.DMA((1,))],
        compiler_params=_params("arbitrary"),
    )(*args, w_down)
    dr, act, dhg, dhu, dy, dg, db = first[:7]

    tb = min(ROW_TILE_BWD, s)
    weights = [w_gate, w_up] + ([ple[2]] if with_ple else [])

    def body_b(*refs):
        it = iter(refs)
        dr_ref, dhg_ref, dhu_ref = next(it), next(it), next(it)
        dpg_ref = next(it) if with_ple else None
        w_hbm = [next(it) for _ in weights]
        dx_ref = next(it)
        w_vm = [next(it) for _ in weights]
        sems = next(it)
        _load_resident(pl.program_id(0), list(zip(w_hbm, w_vm)), sems)
        dx = ALPHA * dr_ref[...] + _mm_nt(dhg_ref[...], w_vm[0][...]) + _mm_nt(dhu_ref[...], w_vm[1][...])
        if with_ple:
            dx = dx + _mm_nt(dpg_ref[...], w_vm[2][...])
        dx_ref[...] = dx

    in_specs = [_row_spec(tb, d), _row_spec(tb, f), _row_spec(tb, f)] + ([_row_spec(tb, d)] if with_ple else [])
    args = [dr, dhg, dhu] + ([first[7]] if with_ple else [])
    dx = pl.pallas_call(
        body_b, name="ffn_bwd_input" + suffix, grid=(s // tb,), in_specs=in_specs + [_ANY] * len(weights),
        out_specs=_row_spec(tb, d), out_shape=jax.ShapeDtypeStruct((s, d), F32),
        scratch_shapes=[pltpu.VMEM(w.shape, w.dtype) for w in weights] + [pltpu.SemaphoreType.DMA((len(weights),))],
        compiler_params=_params("arbitrary"),
    )(*args, *weights)
    return (dx, act, dhg, dhu, dy, dg, db) + tuple(first[7:])


def win_fwd(x1, w_r, w_l, w_g):
    s = x1.shape[0]
    tm = min(ROW_TILE_BWD, s)
    weights = [w_r, w_l, w_g]

    def body(x_ref, wr_h, wl_h, wg_h, hr_ref, hl_ref, hgd_ref, wr_v, wl_v, wg_v, sems):
        _load_resident(pl.program_id(0), [(wr_h, wr_v), (wl_h, wl_v), (wg_h, wg_v)], sems)
        xb = x_ref[...].astype(BF16)
        hr_ref[...] = _mm(xb, wr_v[...])
        hl_ref[...] = _mm(xb, wl_v[...])
        hgd_ref[...] = _mm(xb, wg_v[...])

    return pl.pallas_call(
        body, name="win_fwd", grid=(s // tm,),
        in_specs=[_row_spec(tm, D_MODEL), _ANY, _ANY, _ANY],
        out_specs=[_row_spec(tm, RET_IN), _row_spec(tm, LRU_IN), _row_spec(tm, GDN_IN)],
        out_shape=[jax.ShapeDtypeStruct((s, RET_IN), F32), jax.ShapeDtypeStruct((s, LRU_IN), F32),
                   jax.ShapeDtypeStruct((s, GDN_IN), F32)],
        scratch_shapes=[pltpu.VMEM(w.shape, w.dtype) for w in weights] + [pltpu.SemaphoreType.DMA((3,))],
        compiler_params=_params("arbitrary"),
    )(x1, *weights)


def win_bwd(dr2, dhr, dhl, dhq, dab, w_r, w_l, w_g, w_ab):
    s = dr2.shape[0]
    tm = min(ROW_TILE_BWD, s)
    weights = [w_r, w_l, w_g, w_ab]
    nq = 4 * GDN_W

    def body(dr_ref, dhr_ref, dhl_ref, dhq_ref, dab_ref, wr_h, wl_h, wg_h, wab_h, dx_ref, wr_v, wl_v, wg_v, wab_v, sems):
        _load_resident(pl.program_id(0), [(wr_h, wr_v), (wl_h, wl_v), (wg_h, wg_v), (wab_h, wab_v)], sems)
        dx_ref[...] = (ALPHA * dr_ref[...] + _mm_nt(dhr_ref[...], wr_v[...]) + _mm_nt(dhl_ref[...], wl_v[...])
                       + _mm_nt(dhq_ref[...], wg_v[:, 0:nq]) + _mm_nt(dab_ref[...], wab_v[...]))

    return pl.pallas_call(
        body, name="win_bwd", grid=(s // tm,),
        in_specs=[_row_spec(tm, D_MODEL), _row_spec(tm, RET_IN), _row_spec(tm, LRU_IN), _row_spec(tm, nq), _row_spec(tm, LANES),
                  _ANY, _ANY, _ANY, _ANY],
        out_specs=_row_spec(tm, D_MODEL),
        out_shape=jax.ShapeDtypeStruct((s, D_MODEL), F32),
        scratch_shapes=[pltpu.VMEM(w.shape, w.dtype) for w in weights] + [pltpu.SemaphoreType.DMA((4,))],
        compiler_params=_params("arbitrary"),
    )(dr2, dhr, dhl, dhq, dab, *weights)


def out_fwd(o_r, o_l, o_g, x1, w_out, ln_g, ln_b):
    s = x1.shape[0]
    tm = ROW_TILE

    def body(or_ref, ol_ref, og_ref, x_ref, g_ref, b_ref, w_h, r_ref, xn_ref, xnb_ref, ocat_ref, w_v, sems):
        _load_resident(pl.program_id(0), [(w_h, w_v)], sems)
        ocat = jnp.concatenate([or_ref[...], ol_ref[...], og_ref[...]], axis=1).astype(BF16)
        ocat_ref[...] = ocat
        r = ALPHA * x_ref[...] + _mm(ocat, w_v[...])
        r_ref[...] = r
        xhat, _ = _layer_norm_stats(r)
        xn = xhat * g_ref[...] + b_ref[...]
        xn_ref[...] = xn
        xnb_ref[...] = xn.astype(BF16)

    d = D_MODEL
    return pl.pallas_call(
        body, name="out_fwd", grid=(s // tm,),
        in_specs=[_row_spec(tm, RET_W), _row_spec(tm, LRU_W), _row_spec(tm, GDN_W), _row_spec(tm, d),
                  _full_spec((1, d)), _full_spec((1, d)), _ANY],
        out_specs=[_row_spec(tm, d)] * 4,
        out_shape=[jax.ShapeDtypeStruct((s, d), F32)] * 2 + [jax.ShapeDtypeStruct((s, d), BF16)] * 2,
        scratch_shapes=[pltpu.VMEM(w_out.shape, w_out.dtype), pltpu.SemaphoreType.DMA((1,))],
        compiler_params=_params("arbitrary"),
    )(o_r, o_l, o_g, x1, ln_g, ln_b, w_out)


def out_bwd(dxn, r2, ln_g, w_out):
    s = dxn.shape[0]
    tm = ROW_TILE

    def body(dxn_ref, r_ref, g_ref, w_h, dr_ref, drb_ref, dor_ref, dol_ref, dog_ref, dg_ref, db_ref, w_v, sems):
        step = pl.program_id(0)
        _load_resident(step, [(w_h, w_v)], sems)

        @pl.when(step == 0)
        def _():
            dg_ref[...] = jnp.zeros_like(dg_ref)
            db_ref[...] = jnp.zeros_like(db_ref)

        dxn_v = dxn_ref[...]
        xhat, rstd = _layer_norm_stats(r_ref[...])
        dg_ref[...] += jnp.sum(dxn_v * xhat, axis=0, keepdims=True)
        db_ref[...] += jnp.sum(dxn_v, axis=0, keepdims=True)
        dyh = dxn_v * g_ref[...]
        dr = rstd * (dyh - jnp.mean(dyh, axis=-1, keepdims=True) - xhat * jnp.mean(dyh * xhat, axis=-1, keepdims=True))
        dr_ref[...] = dr
        drb = dr.astype(BF16)
        drb_ref[...] = drb
        dor_ref[...] = _mm_nt(drb, w_v[0:RET_W, :])
        dol_ref[...] = _mm_nt(drb, w_v[RET_W:RET_W + LRU_W, :])
        dog_ref[...] = _mm_nt(drb, w_v[RET_W + LRU_W:, :])

    d = D_MODEL
    return pl.pallas_call(
        body, name="out_bwd", grid=(s // tm,),
        in_specs=[_row_spec(tm, d), _row_spec(tm, d), _full_spec((1, d)), _ANY],
        out_specs=[_row_spec(tm, d), _row_spec(tm, d), _row_spec(tm, RET_W), _row_spec(tm, LRU_W), _row_spec(tm, GDN_W),
                   _full_spec((1, d)), _full_spec((1, d))],
        out_shape=[jax.ShapeDtypeStruct((s, d), F32), jax.ShapeDtypeStruct((s, d), BF16),
                   jax.ShapeDtypeStruct((s, RET_W), F32), jax.ShapeDtypeStruct((s, LRU_W), F32),
                   jax.ShapeDtypeStruct((s, GDN_W), F32), jax.ShapeDtypeStruct((1, d), F32), jax.ShapeDtypeStruct((1, d), F32)],
        scratch_shapes=[pltpu.VMEM(w_out.shape, w_out.dtype), pltpu.SemaphoreType.DMA((1,))],
        compiler_params=_params("arbitrary"),
    )(dxn, r2, ln_g, w_out)


def wgrad(a, b, name, out_dtype=BF16):
    s, m = a.shape
    n = b.shape[1]
    tk = 1024 if s % 1024 == 0 else s
    tm = next((c for c in (1408, 1024, 768, 512, 384, 256) if m % c == 0), m)
    tn = next((c for c in (1408, 1152, 1024, 768, 512) if n % c == 0), n)
    nk = s // tk

    def body(a_ref, b_ref, o_ref, acc_ref):
        k = pl.program_id(2)

        @pl.when(k == 0)
        def _():
            acc_ref[...] = jnp.zeros_like(acc_ref)

        acc_ref[...] += _mm_tn(a_ref[...].astype(BF16), b_ref[...].astype(BF16))

        @pl.when(k == nk - 1)
        def _():
            o_ref[...] = acc_ref[...].astype(o_ref.dtype)

    return pl.pallas_call(
        body, name=name, grid=(m // tm, n // tn, nk),
        in_specs=[pl.BlockSpec((tk, tm), lambda i, j, k: (k, i)), pl.BlockSpec((tk, tn), lambda i, j, k: (k, j))],
        out_specs=pl.BlockSpec((tm, tn), lambda i, j, k: (i, j)),
        out_shape=jax.ShapeDtypeStruct((m, n), out_dtype),
        scratch_shapes=[pltpu.VMEM((tm, tn), F32)],
        compiler_params=_params("arbitrary", "arbitrary", "arbitrary"),
    )(a, b)


def loss_and_grad(y, target):
    s, d = y.shape
    tm = ROW_TILE

    def body(y_ref, t_ref, dy_ref, l_ref):
        @pl.when(pl.program_id(0) == 0)
        def _():
            l_ref[...] = jnp.zeros_like(l_ref)

        err = y_ref[...] - t_ref[...]
        dy_ref[...] = err / d
        l_ref[...] += 0.5 * jnp.sum(jnp.mean(err * err, axis=-1, keepdims=True), axis=0, keepdims=True)

    return pl.pallas_call(
        body, name="loss_and_grad", grid=(s // tm,),
        in_specs=[_row_spec(tm, d), _row_spec(tm, d)],
        out_specs=[_row_spec(tm, d), _full_spec((1, 1))],
        out_shape=[jax.ShapeDtypeStruct((s, d), F32), jax.ShapeDtypeStruct((1, 1), F32)],
        compiler_params=_params("arbitrary"),
    )(y, target)


def _ret_consts():
    lg = np.log1p(-np.exp2(-5.0 - np.arange(RET_HEADS, dtype=np.float64)))
    idx = np.arange(CHUNK, dtype=np.float64)
    intra = np.exp(np.abs(idx[:, None] - idx[None, :])[None] * lg[:, None, None])
    cross = np.repeat(np.exp((idx + 1.0)[:, None] * lg[None, :]), HEAD, axis=1)
    tail = np.repeat(np.exp((CHUNK - 1.0 - idx)[:, None] * lg[None, :]), HEAD, axis=1)
    dec = np.repeat(np.exp(CHUNK * lg)[None, :], HEAD, axis=1)
    half = HEAD // 2
    inv_freq = (ROPE_THETA ** (-jnp.arange(half, dtype=F32) / half))
    invf = jnp.tile(inv_freq, 2 * LANES // HEAD)[None, :]
    sgn = np.tile(np.concatenate([-np.ones(half), np.ones(half)]), LANES // HEAD)[None, :]
    f = lambda a: jnp.asarray(a, F32)
    return dict(intra=f(intra), cross=f(cross), tail=f(tail), dec=f(dec), invf=invf, sgn=f(sgn))


def _swap_halves(t):
    lane = lax.broadcasted_iota(jnp.int32, t.shape, 1)
    return jnp.where((lane & 32) == 0, pltpu.roll(t, LANES - 32, 1), pltpu.roll(t, 32, 1))


def _rope(t, c, s):
    return t * c + _swap_halves(t) * s


def _rope_transposed(g, c, s):
    return g * c + _swap_halves(g * s)


def _head_mask(hd):
    lane = lax.broadcasted_iota(jnp.int32, (1, LANES), 1)
    return ((lane >= HEAD * hd) & (lane < HEAD * (hd + 1))).astype(F32)


def _block_diag_mask():
    r = lax.broadcasted_iota(jnp.int32, (LANES, LANES), 0)
    c = lax.broadcasted_iota(jnp.int32, (LANES, LANES), 1)
    return ((r >= HEAD) == (c >= HEAD)).astype(F32)


RET_STEP_CHUNKS = 4


def _ret_specs(n_of, gch):
    cst = lambda shape: pl.BlockSpec(shape, lambda i: (0,) * len(shape))
    return [pl.BlockSpec((CHUNK * gch, RET_IN), lambda i: (n_of(i), 0)), pl.BlockSpec((CHUNK * gch, 1), lambda i: (n_of(i), 0)),
            cst((1, LANES)), cst((1, LANES)), cst((RET_HEADS, CHUNK, CHUNK)), cst((CHUNK, RET_W)), cst((CHUNK, RET_W)),
            cst((1, RET_W)), cst((1, RET_W))]


def ret_fwd(hr, pos, norm_g):
    s = hr.shape[0]
    n_chunks = s // CHUNK
    cs = _ret_consts()
    n_slab = RET_W // LANES

    gch = min(RET_STEP_CHUNKS, n_chunks)

    def body(hr_ref, pos_ref, invf_ref, sgn_ref, intra_ref, cross_ref, tail_ref, dec_ref, g_ref, o_ref, opre_ref, st_ref, state):
        @pl.when(pl.program_id(0) == 0)
        def _():
            state[...] = jnp.zeros_like(state)

        bd = _block_diag_mask()
        sts = [state[LANES * sl:LANES * (sl + 1), :] for sl in range(n_slab)]
        for c in range(gch):
            tok = slice(CHUNK * c, CHUNK * (c + 1))
            ang = pos_ref[tok, :].astype(F32) * invf_ref[...]
            cosv = jnp.cos(ang)
            sinv = jnp.sin(ang) * sgn_ref[...]
            for sl in range(n_slab):
                lanes = slice(LANES * sl, LANES * (sl + 1))
                q = hr_ref[tok, LANES * sl:LANES * (sl + 1)]
                k = hr_ref[tok, RET_W + LANES * sl:RET_W + LANES * (sl + 1)]
                v = hr_ref[tok, 2 * RET_W + LANES * sl:2 * RET_W + LANES * (sl + 1)]
                gate = hr_ref[tok, 3 * RET_W + LANES * sl:3 * RET_W + LANES * (sl + 1)]
                qt = _rope(q, cosv, sinv) * (HEAD ** -0.5)
                kt = _rope(k, cosv, sinv)
                st = sts[sl]
                st_ref[RET_W * c + LANES * sl:RET_W * c + LANES * (sl + 1), :] = st
                o = _mm(qt * cross_ref[:, lanes], st)
                for hd in range(2):
                    m = _head_mask(hd)
                    sc = _mm_nt(qt * m, kt) * intra_ref[2 * sl + hd]
                    o = o + _mm(sc, v) * m
                sts[sl] = st * dec_ref[:, lanes] + _mm_tn(kt, v * tail_ref[:, lanes]) * bd
                opre_ref[tok, lanes] = o
                mu = _group_sum_slab(o) * (1.0 / HEAD)
                dlt = o - mu
                var = _group_sum_slab(dlt * dlt) * (1.0 / HEAD)
                on = dlt * lax.rsqrt(var + 1e-5)
                o_ref[tok, lanes] = on * g_ref[:, lanes] * (gate * _sigmoid(gate))
        for sl in range(n_slab):
            state[LANES * sl:LANES * (sl + 1), :] = sts[sl]

    out_row = lambda w: pl.BlockSpec((CHUNK * gch, w), lambda i: (i, 0))
    return pl.pallas_call(
        body, name="ret_fwd", grid=(n_chunks // gch,),
        in_specs=_ret_specs(lambda i: i, gch),
        out_specs=[out_row(RET_W), out_row(RET_W), pl.BlockSpec((RET_W * gch, LANES), lambda i: (i, 0))],
        out_shape=[jax.ShapeDtypeStruct((s, RET_W), F32), jax.ShapeDtypeStruct((s, RET_W), F32),
                   jax.ShapeDtypeStruct((n_chunks * RET_W, LANES), F32)],
        scratch_shapes=[pltpu.VMEM((RET_W, LANES), F32)],
        compiler_params=_params("arbitrary"),
    )(hr, pos, cs["invf"], cs["sgn"], cs["intra"], cs["cross"], cs["tail"], cs["dec"], norm_g)


def ret_bwd(hr, pos, norm_g, opre, states, dout):
    s = hr.shape[0]
    n_chunks = s // CHUNK
    cs = _ret_consts()
    n_slab = RET_W // LANES
    gch = min(RET_STEP_CHUNKS, n_chunks)
    rev = lambda i: n_chunks // gch - 1 - i

    def body(hr_ref, pos_ref, invf_ref, sgn_ref, intra_ref, cross_ref, tail_ref, dec_ref, g_ref, opre_ref, st_ref, do_ref,
             dh_ref, dg_ref, gstate):
        @pl.when(pl.program_id(0) == 0)
        def _():
            gstate[...] = jnp.zeros_like(gstate)
            dg_ref[...] = jnp.zeros_like(dg_ref)

        bd = _block_diag_mask()
        gss = [gstate[LANES * sl:LANES * (sl + 1), :] for sl in range(n_slab)]
        dgs = [jnp.zeros((1, LANES), F32) for _ in range(n_slab)]
        for c in reversed(range(gch)):
            tok = slice(CHUNK * c, CHUNK * (c + 1))
            ang = pos_ref[tok, :].astype(F32) * invf_ref[...]
            cosv = jnp.cos(ang)
            sinv = jnp.sin(ang) * sgn_ref[...]
            for sl in range(n_slab):
                lanes = slice(LANES * sl, LANES * (sl + 1))
                q = hr_ref[tok, LANES * sl:LANES * (sl + 1)]
                k = hr_ref[tok, RET_W + LANES * sl:RET_W + LANES * (sl + 1)]
                v = hr_ref[tok, 2 * RET_W + LANES * sl:2 * RET_W + LANES * (sl + 1)]
                gate = hr_ref[tok, 3 * RET_W + LANES * sl:3 * RET_W + LANES * (sl + 1)]
                qt = _rope(q, cosv, sinv) * (HEAD ** -0.5)
                kt = _rope(k, cosv, sinv)
                o = opre_ref[tok, lanes]
                mu = _group_sum_slab(o) * (1.0 / HEAD)
                dlt = o - mu
                var = _group_sum_slab(dlt * dlt) * (1.0 / HEAD)
                rstd = lax.rsqrt(var + 1e-5)
                on = dlt * rstd
                sil, dsil = _silu_and_grad(gate)
                dout_v = do_ref[tok, lanes]
                gn = g_ref[:, lanes]
                dgs[sl] = dgs[sl] + jnp.sum(dout_v * on * sil, axis=0, keepdims=True)
                d_on = dout_v * gn * sil
                dgate = dout_v * on * gn * dsil
                d_o = rstd * (d_on - _group_sum_slab(d_on) * (1.0 / HEAD) - on * (_group_sum_slab(d_on * on) * (1.0 / HEAD)))
                st = st_ref[RET_W * c + LANES * sl:RET_W * c + LANES * (sl + 1), :]
                gs = gss[sl]
                cross = cross_ref[:, lanes]
                tail = tail_ref[:, lanes]
                dqt = _mm_nt(d_o, st) * cross
                ds_here = _mm_tn(qt * cross, d_o) * bd
                vt = v * tail
                dkt = _mm_nt(vt, gs)
                dv = _mm(kt, gs) * tail
                for hd in range(2):
                    m = _head_mask(hd)
                    qm = qt * m
                    dom = d_o * m
                    intra = intra_ref[2 * sl + hd]
                    sc = _mm_nt(qm, kt) * intra
                    dsc = _mm_nt(dom, v) * intra
                    dqt = dqt + _mm(dsc, kt) * m
                    dkt = dkt + _mm_tn(dsc, qm)
                    dv = dv + _mm_tn(sc, dom)
                gss[sl] = gs * dec_ref[:, lanes] + ds_here
                dh_ref[tok, LANES * sl:LANES * (sl + 1)] = _rope_transposed(dqt * (HEAD ** -0.5), cosv, sinv).astype(BF16)
                dh_ref[tok, RET_W + LANES * sl:RET_W + LANES * (sl + 1)] = _rope_transposed(dkt, cosv, sinv).astype(BF16)
                dh_ref[tok, 2 * RET_W + LANES * sl:2 * RET_W + LANES * (sl + 1)] = dv.astype(BF16)
                dh_ref[tok, 3 * RET_W + LANES * sl:3 * RET_W + LANES * (sl + 1)] = dgate.astype(BF16)
        for sl in range(n_slab):
            gstate[LANES * sl:LANES * (sl + 1), :] = gss[sl]
            dg_ref[:, LANES * sl:LANES * (sl + 1)] += dgs[sl]

    row = lambda w: pl.BlockSpec((CHUNK * gch, w), lambda i: (rev(i), 0))
    return pl.pallas_call(
        body, name="ret_bwd", grid=(n_chunks // gch,),
        in_specs=_ret_specs(rev, gch) + [row(RET_W), pl.BlockSpec((RET_W * gch, LANES), lambda i: (rev(i), 0)), row(RET_W)],
        out_specs=[row(RET_IN), pl.BlockSpec((1, RET_W), lambda i: (0, 0))],
        out_shape=[jax.ShapeDtypeStruct((s, RET_IN), BF16), jax.ShapeDtypeStruct((1, RET_W), F32)],
        scratch_shapes=[pltpu.VMEM((RET_W, LANES), F32)],
        compiler_params=_params("arbitrary"),
    )(hr, pos, cs["invf"], cs["sgn"], cs["intra"], cs["cross"], cs["tail"], cs["dec"], norm_g, opre, states, dout)


def _lru_gates(xc, wa_ref, ba_ref, wx_ref, bx_ref, lam_ref):
    xcb = xc.astype(BF16)
    r = _sigmoid(_mm(xcb, wa_ref[...].astype(BF16)) + ba_ref[...])
    ig = _sigmoid(_mm(xcb, wx_ref[...].astype(BF16)) + bx_ref[...])
    lam = lam_ref[...]
    ls = jnp.minimum(lam, 0.0) - _log1p(jnp.exp(-jnp.abs(lam)))
    la = (LRU_C * r) * ls
    a = jnp.exp(la)
    mult = jnp.sqrt(-_expm1(2.0 * la))
    return r, ig, ls, a, mult


def _lru_conv(x, xprev, w_ref, b_ref):
    xc = b_ref[...] + w_ref[3:4, :] * x
    for j in (1, 2, 3):
        xc = xc + w_ref[3 - j:4 - j, :] * _shift_rows(x, xprev, j)
    return xc


def lru_fwd(hl, conv_w, conv_b, w_a, b_a, w_x, b_x, lam):
    s = hl.shape[0]
    ts = SCAN_TILE
    w = LRU_W

    def body(hl_ref, hp_ref, cw_ref, cb_ref, wa_ref, ba_ref, wx_ref, bx_ref, lam_ref, o_ref, xc_ref, h_ref, carry):
        i = pl.program_id(0)

        @pl.when(i == 0)
        def _():
            carry[...] = jnp.zeros_like(carry)

        x = hl_ref[:, 0:w]
        gate = hl_ref[:, w:2 * w]
        xprev = hp_ref[...] * (i > 0).astype(F32)
        xc = _lru_conv(x, xprev, cw_ref, cb_ref)
        xc_ref[...] = xc
        _, ig, _, a, mult = _lru_gates(xc, wa_ref, ba_ref, wx_ref, bx_ref, lam_ref)
        b = mult * (ig * xc)
        row = lax.broadcasted_iota(jnp.int32, (ts, w), 0)
        d = 1
        while d < ts:
            ap = jnp.where(row >= d, pltpu.roll(a, d, 0), 1.0)
            bp = jnp.where(row >= d, pltpu.roll(b, d, 0), 0.0)
            b = a * bp + b
            a = a * ap
            d *= 2
        h = b + a * carry[0:1, :]
        h_ref[...] = h
        carry[0:1, :] = h[ts - 1:ts, :]
        o_ref[...] = h * _gelu(gate)

    cst = lambda shape: pl.BlockSpec(shape, lambda i: (0, 0))
    return pl.pallas_call(
        body, name="lru_fwd", grid=(s // ts,),
        in_specs=[_row_spec(ts, 2 * w), pl.BlockSpec((ts, w), lambda i: (jnp.maximum(i - 1, 0), 0)),
                  cst((4, w)), cst((1, w)), cst((w, w)), cst((1, w)), cst((w, w)), cst((1, w)), cst((1, w))],
        out_specs=[_row_spec(ts, w)] * 3,
        out_shape=[jax.ShapeDtypeStruct((s, w), F32)] * 3,
        scratch_shapes=[pltpu.VMEM((8, w), F32)],
        compiler_params=_params("arbitrary"),
    )(hl, hl, conv_w, conv_b, w_a, b_a, w_x, b_x, lam)


def lru_bwd(hl, conv_w, conv_b, w_a, b_a, w_x, b_x, lam, xc_saved, h_saved, dout):
    s = hl.shape[0]
    ts = SCAN_TILE
    w = LRU_W
    nb = s // ts
    rev = lambda i: nb - 1 - i

    def body(hl_ref, hp_ref, cw_ref, cb_ref, wa_ref, ba_ref, wx_ref, bx_ref, lam_ref, xc_ref, h_ref, hprev_ref, do_ref,
             dhl_ref, dcw_ref, dcb_ref, dwa_ref, dba_ref, dwx_ref, dbx_ref, dlam_ref, carry, dxc_next):
        i = pl.program_id(0)
        blk = nb - 1 - i

        @pl.when(i == 0)
        def _():
            carry[...] = jnp.zeros_like(carry)
            dxc_next[...] = jnp.zeros_like(dxc_next)
            for ref in (dcw_ref, dcb_ref, dwa_ref, dba_ref, dwx_ref, dbx_ref, dlam_ref):
                ref[...] = jnp.zeros_like(ref)

        first = (blk > 0).astype(F32)
        x = hl_ref[:, 0:w]
        gate = hl_ref[:, w:2 * w]
        xprev = hp_ref[...] * first
        xc = xc_ref[...]
        h = h_ref[...]
        hprev = hprev_ref[...] * first
        r, ig, ls, a, mult = _lru_gates(xc, wa_ref, ba_ref, wx_ref, bx_ref, lam_ref)
        do = do_ref[...]
        dh = do * _gelu(gate)
        dgate = do * h * _gelu_grad(gate)
        row = lax.broadcasted_iota(jnp.int32, (ts, w), 0)
        ca = jnp.where(row < ts - 1, pltpu.roll(a, ts - 1, 0), 1.0)
        cb = dh
        d = 1
        while d < ts:
            an = jnp.where(row < ts - d, pltpu.roll(ca, ts - d, 0), 1.0)
            bn = jnp.where(row < ts - d, pltpu.roll(cb, ts - d, 0), 0.0)
            cb = cb + ca * bn
            ca = ca * an
            d *= 2
        lamb = cb + ca * carry[0:1, :]
        carry[0:1, :] = a[0:1, :] * lamb[0:1, :]
        h_before = _shift_rows(h, hprev, 1)
        da = lamb * h_before
        ix = ig * xc
        dmult = lamb * ix
        dig = lamb * mult * xc
        dxc = lamb * mult * ig
        dla = (da - dmult * a / mult) * a
        dr = dla * LRU_C * ls
        dlam_ref[...] += jnp.sum(dla * LRU_C * r, axis=0, keepdims=True) * _sigmoid(-lam_ref[...])
        dpa = dr * r * (1.0 - r)
        dpx = dig * ig * (1.0 - ig)
        dba_ref[...] += jnp.sum(dpa, axis=0, keepdims=True)
        dbx_ref[...] += jnp.sum(dpx, axis=0, keepdims=True)
        dpab = dpa.astype(BF16)
        dpxb = dpx.astype(BF16)
        xcb = xc.astype(BF16)
        dxc = dxc + _mm_nt(dpab, wa_ref[...].astype(BF16)) + _mm_nt(dpxb, wx_ref[...].astype(BF16))
        dwa_ref[...] += _mm_tn(xcb, dpab)
        dwx_ref[...] += _mm_tn(xcb, dpxb)
        dcb_ref[...] += jnp.sum(dxc, axis=0, keepdims=True)
        nxt = dxc_next[...]
        dx = cw_ref[3:4, :] * dxc
        dcw_ref[3:4, :] += jnp.sum(dxc * x, axis=0, keepdims=True)
        for j in (1, 2, 3):
            dx = dx + cw_ref[3 - j:4 - j, :] * _shift_rows_up(dxc, nxt, j)
            dcw_ref[3 - j:4 - j, :] += jnp.sum(dxc * _shift_rows(x, xprev, j), axis=0, keepdims=True)
        dxc_next[...] = dxc
        dhl_ref[:, 0:w] = dx.astype(BF16)
        dhl_ref[:, w:2 * w] = dgate.astype(BF16)

    cst = lambda shape: pl.BlockSpec(shape, lambda i: (0, 0))
    rowr = lambda width: pl.BlockSpec((ts, width), lambda i: (rev(i), 0))
    prevr = lambda width: pl.BlockSpec((ts, width), lambda i: (jnp.maximum(rev(i) - 1, 0), 0))
    return pl.pallas_call(
        body, name="lru_bwd", grid=(nb,),
        in_specs=[rowr(2 * w), prevr(w), cst((4, w)), cst((1, w)), cst((w, w)), cst((1, w)), cst((w, w)), cst((1, w)), cst((1, w)),
                  rowr(w), rowr(w), prevr(w), rowr(w)],
        out_specs=[rowr(2 * w), cst((4, w)), cst((1, w)), cst((w, w)), cst((1, w)), cst((w, w)), cst((1, w)), cst((1, w))],
        out_shape=[jax.ShapeDtypeStruct((s, 2 * w), BF16), jax.ShapeDtypeStruct((4, w), F32), jax.ShapeDtypeStruct((1, w), F32),
                   jax.ShapeDtypeStruct((w, w), F32), jax.ShapeDtypeStruct((1, w), F32), jax.ShapeDtypeStruct((w, w), F32),
                   jax.ShapeDtypeStruct((1, w), F32), jax.ShapeDtypeStruct((1, w), F32)],
        scratch_shapes=[pltpu.VMEM((8, w), F32), pltpu.VMEM((ts, w), F32)],
        compiler_params=_params("arbitrary"),
    )(hl, hl, conv_w, conv_b, w_a, b_a, w_x, b_x, lam, xc_saved, h_saved, h_saved, dout)


GDN_QKV = 3 * GDN_W
GDN_STEP_CHUNKS = 4
GDN_BWD_STEP_CHUNKS = 2


def _tri_inverse_many(nms):
    r = lax.broadcasted_iota(jnp.int32, nms[0].shape, 0)
    c = lax.broadcasted_iota(jnp.int32, nms[0].shape, 1)
    eye = (r == c).astype(F32)
    ts = [eye - nm for nm in nms]
    ps = list(nms)
    for _ in range(5):
        ps = [_mm3(p, p) for p in ps]
        ts = [t + _mm3(t, p) for t, p in zip(ts, ps)]
    return ts


def _gdn_front(hx_ref, hprev, cw_ref, al_ref, dt_ref):
    w = GDN_W
    x = hx_ref[:, 0:GDN_QKV]
    y = cw_ref[3:4, :] * x
    for j in (1, 2, 3):
        y = y + cw_ref[3 - j:4 - j, :] * _shift_rows(x, hprev, j)
    qkv, dsil = _silu_and_grad(y)
    q, k, v = qkv[:, 0:w], qkv[:, w:2 * w], qkv[:, 2 * w:3 * w]
    rq = lax.rsqrt(_group_sum(q * q) + 1e-6)
    rk = lax.rsqrt(_group_sum(k * k) + 1e-6)
    beta = _sigmoid(hx_ref[:, 5 * w:6 * w])
    sp_in = hx_ref[:, 4 * w:5 * w] + dt_ref[...]
    neg_a = -jnp.exp(al_ref[...])
    g = neg_a * _softplus(sp_in)
    n_c = g.shape[0] // CHUNK
    gc = jnp.concatenate([_rows_prefix_sum(g[CHUNK * c:CHUNK * (c + 1)]) for c in range(n_c)], axis=0)
    return dict(x=x, dsil=dsil, qn=q * rq, kn=k * rk, v=v, rq=rq, rk=rk, beta=beta, sp_in=sp_in, neg_a=neg_a, g=g, gc=gc)


def _stack_heads(x):
    return jnp.concatenate([x * _head_mask(0), x * _head_mask(1)], axis=0)


def _unstack_heads(y):
    return y[0:CHUNK] + y[CHUNK:2 * CHUNK]


def _head_transpose(x):
    return jnp.concatenate([x[:, 0:HEAD].T, x[:, HEAD:2 * HEAD].T], axis=1)


def _head_total(x):
    cols = jnp.broadcast_to(jnp.sum(x, axis=0, keepdims=True), (8, LANES))
    return _group_sum_slab(cols)[0:1]


def _slab_tri_masks():
    r = lax.broadcasted_iota(jnp.int32, (CHUNK, LANES), 0)
    c = lax.broadcasted_iota(jnp.int32, (CHUNK, LANES), 1) & (HEAD - 1)
    return r >= c, r > c


def _gdn_slab(fr, c, sl, tri):
    lower, strict = tri
    ls = lambda a: a[CHUNK * c:CHUNK * (c + 1), LANES * sl:LANES * (sl + 1)]
    k = ls(fr["kn"])
    q = ls(fr["qn"]) * (HEAD ** -0.5)
    v = ls(fr["v"])
    beta = ls(fr["beta"])
    gc = ls(fr["gc"])
    e = jnp.exp(gc)
    gl = gc[CHUNK - 1:CHUNK, :]
    xt = jnp.exp(gl - gc)
    dec = jnp.where(lower, jnp.exp(jnp.minimum(gc - _head_transpose(gc), 0.0)), 0.0)
    kbd = _stack_heads(k)
    kk = _mm_nt(k, kbd)
    qkr = _mm_nt(q, kbd)
    return dict(k=k, q=q, v=v, beta=beta, e=e, egl=jnp.exp(gl), xt=xt, dec=dec, kk=kk, qkr=qkr, kbd=kbd,
                nm=jnp.where(strict, beta * kk * dec, 0.0))


def gdn_fwd(hx, conv_w, a_log_e, dt_bias_e, norm_g_e):
    s = hx.shape[0]
    n_chunks = s // CHUNK
    w = GDN_W
    n_slab = w // LANES
    gch = min(GDN_STEP_CHUNKS, n_chunks)

    def body(hx_ref, hp_ref, cw_ref, al_ref, dt_ref, ng_ref, o_ref, opre_ref, t_ref, st_ref, state):
        n = pl.program_id(0)

        @pl.when(n == 0)
        def _():
            state[...] = jnp.zeros_like(state)

        fr = _gdn_front(hx_ref, hp_ref[...] * (n > 0).astype(F32), cw_ref, al_ref, dt_ref)
        tri = _slab_tri_masks()
        bd = _block_diag_mask()
        sts = [state[LANES * sl:LANES * (sl + 1), :] for sl in range(n_slab)]
        slabs = [[_gdn_slab(fr, c, sl, tri) for sl in range(n_slab)] for c in range(gch)]
        tbd = _tri_inverse_many([_stack_heads(sq["nm"]) for row_ in slabs for sq in row_])
        o_rows = []
        for c in range(gch):
            ts, outs = [], []
            st_ref[w * c:w * (c + 1), :] = jnp.concatenate(sts, axis=0)
            for sl in range(n_slab):
                sq = slabs[c][sl]
                t = _unstack_heads(tbd[n_slab * c + sl])
                ts.append(t)
                u = _mm(t, _stack_heads(sq["v"] * sq["beta"]))
                wk = _mm(t, _stack_heads(sq["k"] * (sq["beta"] * sq["e"])))
                st = sts[sl]
                vnew = u - _mm(wk, st)
                outs.append(_mm(sq["q"] * sq["e"], st) + _mm(sq["qkr"] * sq["dec"], _stack_heads(vnew)))
                sts[sl] = st * sq["egl"] + _mm_tn(sq["k"] * sq["xt"], vnew) * bd
            t_ref[CHUNK * c:CHUNK * (c + 1), :] = jnp.concatenate(ts, axis=1)
            o_rows.append(jnp.concatenate(outs, axis=1))
        state[...] = jnp.concatenate(sts, axis=0)
        o = jnp.concatenate(o_rows, axis=0)
        opre_ref[...] = o
        rinv = lax.rsqrt(_group_sum(o * o) * (1.0 / HEAD) + 1e-6)
        z = hx_ref[:, 3 * w:4 * w]
        o_ref[...] = (o * rinv) * ng_ref[...] * (z * _sigmoid(z))

    cst = lambda shape: pl.BlockSpec(shape, lambda i: (0, 0))
    row = lambda width: pl.BlockSpec((CHUNK * gch, width), lambda i: (i, 0))
    return pl.pallas_call(
        body, name="gdn_fwd", grid=(n_chunks // gch,),
        in_specs=[row(GDN_IN), pl.BlockSpec((CHUNK * gch, GDN_QKV), lambda i: (jnp.maximum(i - 1, 0), 0)),
                  cst((4, GDN_QKV)), cst((1, w)), cst((1, w)), cst((1, w))],
        out_specs=[row(w)] * 3 + [pl.BlockSpec((w * gch, LANES), lambda i: (i, 0))],
        out_shape=[jax.ShapeDtypeStruct((s, w), F32)] * 3 + [jax.ShapeDtypeStruct((n_chunks * w, LANES), F32)],
        scratch_shapes=[pltpu.VMEM((w, LANES), F32)],
        compiler_params=_params("arbitrary"),
    )(hx, hx, conv_w, a_log_e, dt_bias_e, norm_g_e)


def gdn_bwd(hx, conv_w, a_log_e, dt_bias_e, norm_g_e, opre, tmat, states, dout):
    s = hx.shape[0]
    n_chunks = s // CHUNK
    w = GDN_W
    n_slab = w // LANES
    gch = min(GDN_BWD_STEP_CHUNKS, n_chunks)
    n_blocks = n_chunks // gch
    rev = lambda i: n_blocks - 1 - i

    def body(hx_ref, hp_ref, cw_ref, al_ref, dt_ref, ng_ref, opre_ref, t_ref, st_ref, do_ref,
             dhx_ref, dab_ref, dcw_ref, dal_ref, ddt_ref, dng_ref, dstate, dy_next):
        i = pl.program_id(0)
        n = n_blocks - 1 - i

        @pl.when(i == 0)
        def _():
            dstate[...] = jnp.zeros_like(dstate)
            dy_next[...] = jnp.zeros_like(dy_next)
            for ref in (dcw_ref, dal_ref, ddt_ref, dng_ref):
                ref[...] = jnp.zeros_like(ref)

        hprev = hp_ref[...] * (n > 0).astype(F32)
        fr = _gdn_front(hx_ref, hprev, cw_ref, al_ref, dt_ref)
        tri = _slab_tri_masks()
        lower, strict = tri
        o = opre_ref[...]
        rinv = lax.rsqrt(_group_sum(o * o) * (1.0 / HEAD) + 1e-6)
        yn = o * rinv
        z = hx_ref[:, 3 * w:4 * w]
        sil, dsil_z = _silu_and_grad(z)
        dout_v = do_ref[...]
        ng = ng_ref[...]
        dng_ref[...] += jnp.sum(dout_v * yn * sil, axis=0, keepdims=True)
        dz = dout_v * yn * ng * dsil_z
        dyn = dout_v * ng * sil
        d_o = rinv * (dyn - yn * (_group_sum(dyn * yn) * (1.0 / HEAD)))
        last_row = (lax.broadcasted_iota(jnp.int32, (CHUNK, LANES), 0) == CHUNK - 1).astype(F32)
        bd = _block_diag_mask()
        gsum = _group_sum_slab
        t_all, st_all = t_ref[...], st_ref[...]
        dsns = [dstate[LANES * sl:LANES * (sl + 1), :] for sl in range(n_slab)]
        per_chunk = {}
        order = [(c_, s_) for c_ in reversed(range(gch)) for s_ in range(n_slab)]
        chain = {}
        for c, sl in order:
            lanes = slice(LANES * sl, LANES * (sl + 1))
            tok = slice(CHUNK * c, CHUNK * (c + 1))
            sq = _gdn_slab(fr, c, sl, tri)
            t = t_all[tok, lanes]
            st = st_all[w * c + LANES * sl:w * c + LANES * (sl + 1), :]
            dsn = dsns[sl]
            do_s = d_o[tok, lanes]
            u = _mm(t, _stack_heads(sq["v"] * sq["beta"]))
            wk = _mm(t, _stack_heads(sq["k"] * (sq["beta"] * sq["e"])))
            kt = sq["k"] * sq["xt"]
            dvnew = _unstack_heads(_mm_tn(sq["qkr"] * sq["dec"], do_s) * bd) + _mm(kt, dsn)
            dsns[sl] = _mm_tn(sq["q"] * sq["e"], do_s) * bd + sq["egl"] * dsn - _mm_tn(wk, dvnew) * bd
            chain[(c, sl)] = (sq, t, st, dsn, do_s, u, wk, kt, dvnew)
        for c, sl in order:
            sq, t, st, dsn, do_s, u, wk, kt, dvnew = chain[(c, sl)]
            k, q, v, beta, e, xt, dec, kk, qkr, kbd = (sq[n_] for n_ in ("k", "q", "v", "beta", "e", "xt", "dec", "kk", "qkr", "kbd"))
            vnew = u - _mm(wk, st)
            dqd = _mm_nt(do_s, st)
            dqk = _mm_nt(do_s, _stack_heads(vnew))
            dkt = _mm_nt(vnew, dsn)
            dgl = _head_total(dsn * st) * sq["egl"]
            dwk = -_mm_nt(dvnew, st)
            drv = _unstack_heads(_mm_tn(t, dvnew) * bd)
            drk = _unstack_heads(_mm_tn(t, dwk) * bd)
            dnm = jnp.where(strict, -(_mm_nt(drv, _stack_heads(u)) + _mm_nt(drk, _stack_heads(wk))), 0.0)
            dbeta = gsum(dnm * kk * dec)
            dkk = dnm * beta * dec
            ddec = dnm * beta * kk + dqk * qkr
            mq = dqk * dec
            dq = _mm(mq, kbd) + dqd * e
            dk = (_unstack_heads(_mm_tn(mq, q) * bd) + _mm(dkk, kbd) + _unstack_heads(_mm_tn(dkk, k) * bd)
                  + drk * (beta * e) + dkt * xt)
            rks = gsum(drk * k)
            dbeta = dbeta + gsum(drv * v) + rks * e
            de = rks * beta + gsum(dqd * q)
            dxt = gsum(dkt * k) * xt
            dgl = dgl + jnp.sum(dxt, axis=0, keepdims=True)
            dd = ddec * dec
            dgc = de * e - dxt + gsum(dd) - gsum(_head_transpose(dd)) + last_row * dgl
            per_chunk[(c, sl)] = dict(dq=dq * (HEAD ** -0.5), dk=dk, dv=drv * beta, dbeta=dbeta, dgc=dgc)
        for sl in range(n_slab):
            dstate[LANES * sl:LANES * (sl + 1), :] = dsns[sl]

        def block_of(name, suffix_sum=False):
            rows = []
            for c in range(gch):
                r = jnp.concatenate([per_chunk[(c, sl)][name] for sl in range(n_slab)], axis=1)
                rows.append(_rows_suffix_sum(r) if suffix_sum else r)
            return jnp.concatenate(rows, axis=0)

        dg = block_of("dgc", suffix_sum=True)
        dal_ref[...] += jnp.sum(dg * fr["g"], axis=0, keepdims=True)
        da = dg * fr["neg_a"] * _sigmoid(fr["sp_in"])
        ddt_ref[...] += jnp.sum(da, axis=0, keepdims=True)
        beta_all = fr["beta"]
        db = block_of("dbeta") * beta_all * (1.0 - beta_all)
        lane = lax.broadcasted_iota(jnp.int32, (CHUNK * gch, LANES), 1)
        dab = jnp.zeros((CHUNK * gch, LANES), F32)
        for hd in range(GDN_HEADS):
            dab = jnp.where(lane == hd, da[:, HEAD * hd:HEAD * hd + 1], dab)
            dab = jnp.where(lane == GDN_HEADS + hd, db[:, HEAD * hd:HEAD * hd + 1], dab)
        dab_ref[...] = dab.astype(BF16)
        dqn = block_of("dq")
        dkn = block_of("dk")
        dq_raw = fr["rq"] * (dqn - fr["qn"] * _group_sum(dqn * fr["qn"]))
        dk_raw = fr["rk"] * (dkn - fr["kn"] * _group_sum(dkn * fr["kn"]))
        dy = jnp.concatenate([dq_raw, dk_raw, block_of("dv")], axis=1) * fr["dsil"]
        nxt = dy_next[...]
        x = fr["x"]
        dx = cw_ref[3:4, :] * dy
        dcw_ref[3:4, :] += jnp.sum(dy * x, axis=0, keepdims=True)
        for j in (1, 2, 3):
            dx = dx + cw_ref[3 - j:4 - j, :] * _shift_rows_up(dy, nxt, j)
            dcw_ref[3 - j:4 - j, :] += jnp.sum(dy * _shift_rows(x, hprev, j), axis=0, keepdims=True)
        dy_next[...] = dy
        dhx_ref[:, 0:GDN_QKV] = dx.astype(BF16)
        dhx_ref[:, 3 * w:4 * w] = dz.astype(BF16)

    cst = lambda shape: pl.BlockSpec(shape, lambda i: (0, 0))
    row = lambda width: pl.BlockSpec((CHUNK * gch, width), lambda i: (rev(i), 0))
    buf = lambda width: pltpu.VMEM((CHUNK * gch, width), F32)
    return pl.pallas_call(
        body, name="gdn_bwd", grid=(n_blocks,),
        in_specs=[row(GDN_IN), pl.BlockSpec((CHUNK * gch, GDN_QKV), lambda i: (jnp.maximum(rev(i) - 1, 0), 0)),
                  cst((4, GDN_QKV)), cst((1, w)), cst((1, w)), cst((1, w)), row(w), row(w),
                  pl.BlockSpec((w * gch, LANES), lambda i: (rev(i), 0)), row(w)],
        out_specs=[row(4 * w), row(LANES), cst((4, GDN_QKV)), cst((1, w)), cst((1, w)), cst((1, w))],
        out_shape=[jax.ShapeDtypeStruct((s, 4 * w), BF16), jax.ShapeDtypeStruct((s, LANES), BF16),
                   jax.ShapeDtypeStruct((4, GDN_QKV), F32),
                   jax.ShapeDtypeStruct((1, w), F32), jax.ShapeDtypeStruct((1, w), F32), jax.ShapeDtypeStruct((1, w), F32)],
        scratch_shapes=[pltpu.VMEM((w, LANES), F32), buf(GDN_QKV)],
        compiler_params=_params("arbitrary"),
    )(hx, hx, conv_w, a_log_e, dt_bias_e, norm_g_e, opre, tmat, states, dout)


_MESH = pl.DeviceIdType.MESH


def all_gather8(x, name):
    m, n = x.shape

    def body(x_ref, out_ref, send_sems, recv_sems, local_sem):
        px, py, pc = lax.axis_index("x"), lax.axis_index("y"), lax.axis_index("c")
        me, sibling = (px, py, pc), (px, py, 1 - pc)
        chips = [(1 - px, py), (px, 1 - py), (1 - px, 1 - py)]

        def slot(dx, dy, dc):
            return out_ref.at[4 * dx + 2 * dy + dc]

        def copy(k, block, to, src=None):
            return pltpu.make_async_remote_copy(
                src_ref=slot(*block) if src is None else src, dst_ref=slot(*block),
                send_sem=send_sems.at[k], recv_sem=recv_sems.at[k], device_id=to, device_id_type=_MESH)

        mine = pltpu.make_async_copy(x_ref, slot(*me), local_sem)
        mine.start()
        first = [copy(0, me, sibling, src=x_ref)]
        first += [copy(1 + j, me, (*chip, pc), src=x_ref) for j, chip in enumerate(chips)]
        for cp in first:
            cp.start()
        passed = [copy(4 + j, (*chip, pc), sibling) for j, chip in enumerate(chips)]
        for j, chip in enumerate(chips):
            copy(1 + j, (*chip, pc), me).wait_recv()
            passed[j].start()
        copy(0, sibling, me).wait_recv()
        for j, chip in enumerate(chips):
            copy(4 + j, (*chip, 1 - pc), me).wait_recv()
        for cp in first + passed:
            cp.wait_send()
        mine.wait()

    return pl.pallas_call(
        body, name=name, out_shape=jax.ShapeDtypeStruct((N_DEV, m, n), x.dtype),
        in_specs=[_ANY], out_specs=_ANY,
        scratch_shapes=[pltpu.SemaphoreType.DMA((7,)), pltpu.SemaphoreType.DMA((7,)), pltpu.SemaphoreType.DMA],
    )(x)


def gather_layer_weights(shards0, shards1, name):
    n = len(shards0)

    def body(*refs):
        s0, s1 = refs[0:n], refs[n:2 * n]
        f0, f1 = refs[2 * n:3 * n], refs[3 * n:4 * n]
        own_send, own_recv, ici_send, ici_recv, fwd_send, fwd_recv = refs[4 * n:]
        px, py, pc = lax.axis_index("x"), lax.axis_index("y"), lax.axis_index("c")
        mine = 2 * px + py
        sibling = (px, py, 1 - pc)
        chips = [(1 - px, py), (px, 1 - py), (1 - px, 1 - py)]

        def copy(src, dst, sems_s, sems_r, k, to):
            return pltpu.make_async_remote_copy(src_ref=src, dst_ref=dst, send_sem=sems_s.at[k], recv_sem=sems_r.at[k],
                                                device_id=to, device_id_type=_MESH)

        def run(my_shards, my_full, other_full):
            own = []
            for li, (shards, full) in enumerate(((s0, f0), (s1, f1))):
                for i in range(n):
                    own.append(copy(shards[i], full[i].at[mine], own_send, own_recv, li * n + i, sibling))
            ici = []
            for i in range(n):
                for j, (cx, cy) in enumerate(chips):
                    ici.append(copy(my_shards[i], my_full[i].at[mine], ici_send, ici_recv, 3 * i + j, (cx, cy, pc)))
            for cp in own + ici:
                cp.start()
            fwd = []
            for i in range(n):
                for j, (cx, cy) in enumerate(chips):
                    slot = my_full[i].at[2 * cx + cy]
                    copy(my_shards[i], slot, ici_send, ici_recv, 3 * i + j, (cx, cy, pc)).wait_recv()
                    cp = copy(slot, slot, fwd_send, fwd_recv, 3 * i + j, sibling)
                    cp.start()
                    fwd.append(cp)
            for li, full in enumerate((f0, f1)):
                for i in range(n):
                    copy(s0[i], full[i].at[mine], own_send, own_recv, li * n + i, sibling).wait_recv()
            for i in range(n):
                for j, (cx, cy) in enumerate(chips):
                    slot = other_full[i].at[2 * cx + cy]
                    copy(slot, slot, fwd_send, fwd_recv, 3 * i + j, sibling).wait_recv()
            for cp in own + ici + fwd:
                cp.wait_send()

        @pl.when(pc == 0)
        def _():
            run(s0, f0, f1)

        @pl.when(pc == 1)
        def _():
            run(s1, f1, f0)

    full_shapes = [jax.ShapeDtypeStruct((N_CHIPS,) + v.shape, v.dtype) for v in shards0]
    dma = pltpu.SemaphoreType.DMA
    outs = pl.pallas_call(
        body, name=name, out_shape=full_shapes + full_shapes, in_specs=[_ANY] * (2 * n), out_specs=[_ANY] * (2 * n),
        scratch_shapes=[dma((2 * n,)), dma((2 * n,)), dma((3 * n,)), dma((3 * n,)), dma((3 * n,)), dma((3 * n,))],
    )(*shards0, *shards1)
    return outs[0:n], outs[n:2 * n]


def sibling_swap_layers(arrays0, arrays1, name):
    n = len(arrays0)

    def body(*refs):
        a0, a1, outs = refs[0:n], refs[n:2 * n], refs[2 * n:3 * n]
        send_sems, recv_sems = refs[3 * n:]
        px, py, pc = lax.axis_index("x"), lax.axis_index("y"), lax.axis_index("c")

        def run(send):
            cps = [pltpu.make_async_remote_copy(
                src_ref=send[i], dst_ref=outs[i], send_sem=send_sems.at[i], recv_sem=recv_sems.at[i],
                device_id=(px, py, 1 - pc), device_id_type=_MESH) for i in range(n)]
            for cp in cps:
                cp.start()
            for cp in cps:
                cp.wait()

        @pl.when(pc == 0)
        def _():
            run(a1)

        @pl.when(pc == 1)
        def _():
            run(a0)

    return pl.pallas_call(
        body, name=name, out_shape=[jax.ShapeDtypeStruct(v.shape, v.dtype) for v in arrays0],
        in_specs=[_ANY] * (2 * n), out_specs=[_ANY] * n,
        scratch_shapes=[pltpu.SemaphoreType.DMA((n,)), pltpu.SemaphoreType.DMA((n,))],
    )(*arrays0, *arrays1)


def chip_exchange(arrays, pieces, name):
    n = len(pieces)
    offs = [0]
    for r, _ in pieces:
        offs.append(offs[-1] + r)

    def body(*refs):
        srcs = refs[0:n]
        q_ref, send_sems, recv_sems, local_sems = refs[n:]
        px, py, pc = lax.axis_index("x"), lax.axis_index("y"), lax.axis_index("c")
        mine = 2 * px + py
        chips = [(1 - px, py), (px, 1 - py), (1 - px, 1 - py)]
        locals_, sends = [], []
        for i, (r, stride) in enumerate(pieces):
            dst = pl.ds(offs[i], r)
            lc = pltpu.make_async_copy(srcs[i].at[pl.ds(mine * stride, r)], q_ref.at[mine, dst], local_sems.at[i])
            lc.start()
            locals_.append(lc)
            for j, (cx, cy) in enumerate(chips):
                cp = pltpu.make_async_remote_copy(
                    src_ref=srcs[i].at[pl.ds((2 * cx + cy) * stride, r)], dst_ref=q_ref.at[mine, dst],
                    send_sem=send_sems.at[3 * i + j], recv_sem=recv_sems.at[3 * i + j], device_id=(cx, cy, pc), device_id_type=_MESH)
                cp.start()
                sends.append(cp)
        for i, (r, stride) in enumerate(pieces):
            dst = pl.ds(offs[i], r)
            for j, (cx, cy) in enumerate(chips):
                pltpu.make_async_remote_copy(
                    src_ref=srcs[i].at[pl.ds(mine * stride, r)], dst_ref=q_ref.at[2 * cx + cy, dst],
                    send_sem=send_sems.at[3 * i + j], recv_sem=recv_sems.at[3 * i + j], device_id=(cx, cy, pc),
                    device_id_type=_MESH).wait_recv()
        for cp in sends:
            cp.wait_send()
        for lc in locals_:
            lc.wait()

    return pl.pallas_call(
        body, name=name, out_shape=jax.ShapeDtypeStruct((N_CHIPS, offs[-1], arrays[0].shape[1]), arrays[0].dtype),
        in_specs=[_ANY] * n, out_specs=_ANY,
        scratch_shapes=[pltpu.SemaphoreType.DMA((3 * n,)), pltpu.SemaphoreType.DMA((3 * n,)), pltpu.SemaphoreType.DMA((n,))],
    )(*arrays)


def sibling_swap(x, name):
    def body(x_ref, out_ref, send_sem, recv_sem):
        px, py, pc = lax.axis_index("x"), lax.axis_index("y"), lax.axis_index("c")
        cp = pltpu.make_async_remote_copy(
            src_ref=x_ref, dst_ref=out_ref, send_sem=send_sem, recv_sem=recv_sem,
            device_id=(px, py, 1 - pc), device_id_type=_MESH)
        cp.start()
        cp.wait()

    return pl.pallas_call(
        body, name=name, out_shape=jax.ShapeDtypeStruct(x.shape, x.dtype), in_specs=[_ANY], out_specs=_ANY,
        scratch_shapes=[pltpu.SemaphoreType.DMA, pltpu.SemaphoreType.DMA],
    )(x)


ELT_TILE = 128


def _elt_rows(m):
    for t in (512, 256, ELT_TILE, 16, 8):
        if m % t == 0:
            return t
    return m


def pair_add(a0, a1, b, name):
    m, n = b.shape
    tm = _elt_rows(m)

    def body(a0_ref, a1_ref, b_ref, o_ref):
        mine = jnp.where(lax.axis_index("c") == 0, a0_ref[...], a1_ref[...])
        o_ref[...] = (mine.astype(F32) + b_ref[...].astype(F32)).astype(o_ref.dtype)

    return pl.pallas_call(
        body, name=name, grid=(m // tm,), in_specs=[_row_spec(tm, n)] * 3, out_specs=_row_spec(tm, n),
        out_shape=jax.ShapeDtypeStruct((m, n), b.dtype), compiler_params=_params("arbitrary"),
    )(a0, a1, b)


def sum_leading(q, name):
    kk, m, n = q.shape
    tm = _elt_rows(m)

    def body(q_ref, o_ref):
        acc = q_ref[0].astype(F32)
        for i in range(1, kk):
            acc = acc + q_ref[i].astype(F32)
        o_ref[...] = acc

    return pl.pallas_call(
        body, name=name, grid=(m // tm,), in_specs=[pl.BlockSpec((kk, tm, n), lambda i: (0, i, 0))],
        out_specs=_row_spec(tm, n), out_shape=jax.ShapeDtypeStruct((m, n), F32), compiler_params=_params("arbitrary"),
    )(q)


def adamw(w, g, m, v, name):
    rows, cols = w.shape
    tm = _elt_rows(rows)

    def body(w_ref, g_ref, m_ref, v_ref, d_ref, nm_ref, nv_ref):
        gv = g_ref[...]
        nm = ADAM_B1 * m_ref[...] + (1.0 - ADAM_B1) * gv
        nv = ADAM_B2 * v_ref[...] + (1.0 - ADAM_B2) * jnp.square(gv)
        nm_ref[...] = nm
        nv_ref[...] = nv
        m_hat = nm / (1.0 - ADAM_B1 ** ADAM_STEP)
        v_hat = nv / (1.0 - ADAM_B2 ** ADAM_STEP)
        d_ref[...] = -ADAM_LR * (m_hat / (jnp.sqrt(v_hat) + ADAM_EPS) + ADAM_WD * w_ref[...])

    spec = _row_spec(tm, cols)
    return pl.pallas_call(
        body, name=name, grid=(rows // tm,), in_specs=[spec] * 4, out_specs=[spec] * 3,
        out_shape=[jax.ShapeDtypeStruct((rows, cols), F32)] * 3, compiler_params=_params("arbitrary"),
    )(w, g, m, v)


def _block_diag_dense(w):
    g = w.shape[0]
    return jnp.einsum("gij,gh->gihj", w, jnp.eye(g, dtype=w.dtype)).reshape(g * w.shape[1], g * w.shape[2])


def _diag_blocks(m):
    return jnp.stack([m[HEAD * i:HEAD * (i + 1), HEAD * i:HEAD * (i + 1)] for i in range(LRU_BLOCKS)])


def _rep(v):
    return jnp.repeat(v, HEAD, axis=-1)


def _split_w_in(w_in):
    gdn0 = RET_IN + LRU_IN
    gdn1 = gdn0 + 4 * GDN_W
    w_r = w_in[:, 0:RET_IN]
    w_l = w_in[:, RET_IN:gdn0]
    w_g = jnp.concatenate([w_in[:, gdn0:gdn1], _rep(w_in[:, gdn1:gdn1 + GDN_HEADS]), _rep(w_in[:, gdn1 + GDN_HEADS:])], axis=1)
    w_ab = jnp.pad(w_in[:, gdn1:], ((0, 0), (0, LANES - 2 * GDN_HEADS)))
    return w_r, w_l, w_g, w_ab


WIN_SHARD = D_IN // N_CHIPS
WIN_STRIDE = 832
WIN_ROWS = 960
WIN_T_ROWS = WIN_STRIDE * (N_CHIPS - 1) + WIN_ROWS
AB_ROWS = 16

_GRAD_PIECES = (("ffn1_w_gate", 704, 704), ("ffn1_w_up", 704, 704), ("ffn1_w_down", 704, 704), ("w_in", WIN_ROWS, WIN_STRIDE),
                ("w_out", 256, 256), ("ffn2_w_gate", 704, 704), ("ffn2_w_up", 704, 704), ("ffn2_w_down", 704, 704),
                ("ple_w_gate", 256, 256), ("ple_w_proj", 64, 64))
_TRANSPOSED = ("ffn1_w_gate", "ffn1_w_up", "w_in", "ffn2_w_gate", "ffn2_w_up", "ple_w_proj")


def _local_step(x, p, pos, target, wt):
    row = lambda v: v[None, :]
    saved = []
    xb = x.astype(BF16)
    for i in range(DEPTH):
        w_r, w_l, w_g, w_ab = _split_w_in(wt["w_in"][i])
        lw = dict(
            wg1=wt["ffn1_w_gate"][i], wu1=wt["ffn1_w_up"][i], wd1=wt["ffn1_w_down"][i], w_r=w_r, w_l=w_l, w_g=w_g, w_ab=w_ab,
            w_out=wt["w_out"][i], wg2=wt["ffn2_w_gate"][i], wu2=wt["ffn2_w_up"][i], wd2=wt["ffn2_w_down"][i],
            wpg=wt["ple_w_gate"][i], wpp=wt["ple_w_proj"][i],
            wa=_block_diag_dense(wt["lru_w_a"][i]), wx=_block_diag_dense(wt["lru_w_x"][i]),
            al=row(_rep(wt["gdn_a_log"][i])), dt=row(_rep(wt["gdn_dt_bias"][i])), ng=row(jnp.tile(wt["gdn_norm_g"][i], GDN_HEADS)))
        hg1, hu1, r1, x1, x1b = ffn_fwd(x, row(wt["ln_ffn1_g"][i]), row(wt["ln_ffn1_b"][i]), lw["wg1"], lw["wu1"], lw["wd1"])
        hr, hl, hgd = win_fwd(x1, w_r, w_l, w_g)
        o_r, opre_r, st_r = ret_fwd(hr, pos, row(wt["ret_norm_g"][i]))
        o_l, xc, hs = lru_fwd(hl, wt["lru_conv_w"][i], row(wt["lru_conv_b"][i]), lw["wa"], row(wt["lru_b_a"][i]), lw["wx"],
                              row(wt["lru_b_x"][i]), row(wt["lru_lambda"][i]))
        o_g, opre_g, tmat, st_g = gdn_fwd(hgd, wt["gdn_conv_w"][i], lw["al"], lw["dt"], lw["ng"])
        r2, x2, x2b, ocat = out_fwd(o_r, o_l, o_g, x1, lw["w_out"], row(wt["ln_mix_g"][i]), row(wt["ln_mix_b"][i]))
        hg2, hu2, r3, x3, x3b, pg, pp = ffn_fwd(x2, row(wt["ln_ffn2_g"][i]), row(wt["ln_ffn2_b"][i]), lw["wg2"], lw["wu2"],
                                                lw["wd2"], ple=(p[i], lw["wpg"], lw["wpp"]))
        saved.append(dict(lw=lw, x0=xb, hg1=hg1, hu1=hu1, r1=r1, x1=x1b, hr=hr, hl=hl, hgd=hgd, ocat=ocat, opre_r=opre_r,
                          st_r=st_r, xc=xc, hs=hs, opre_g=opre_g, tmat=tmat, st_g=st_g, r2=r2, x2=x2b, hg2=hg2, hu2=hu2,
                          r3=r3, pg=pg, pp=pp))
        x, xb = x3, x3b

    dx, loss = loss_and_grad(x, target)
    grads = [None] * DEPTH
    big = [None] * DEPTH
    for i in reversed(range(DEPTH)):
        sv = saved[i]
        lw = sv["lw"]
        tag = f"_l{i}"
        dx2, act2, dhg2, dhu2, dy2, dg3, db3, dpg, dpp = ffn_bwd(
            dx, sv["r3"], sv["x2"], sv["hg2"], sv["hu2"], row(wt["ln_ffn2_g"][i]), lw["wg2"], lw["wu2"], lw["wd2"],
            ple=(sv["pg"], sv["pp"], lw["wpg"]))
        g, bg = {}, {}
        bg["ffn2_w_gate"] = wgrad(dhg2, sv["x2"], "wgrad_gate2" + tag)
        bg["ffn2_w_up"] = wgrad(dhu2, sv["x2"], "wgrad_up2" + tag)
        bg["ffn2_w_down"] = wgrad(act2, dy2, "wgrad_down2" + tag)
        bg["ple_w_gate"] = wgrad(sv["x2"], dpg, "wgrad_pgate" + tag)
        bg["ple_w_proj"] = wgrad(dpp, p[i], "wgrad_pproj" + tag).reshape(PLE_DIM, D_MODEL)
        g["ln_ffn2_g"], g["ln_ffn2_b"] = dg3[0], db3[0]
        dr2, dr2b, do_r, do_l, do_g, dg2, db2 = out_bwd(dx2, sv["r2"], row(wt["ln_mix_g"][i]), lw["w_out"])
        g["ln_mix_g"], g["ln_mix_b"] = dg2[0], db2[0]
        bg["w_out"] = wgrad(sv["ocat"], dr2b, "wgrad_out" + tag)
        dhr, dgn = ret_bwd(sv["hr"], pos, row(wt["ret_norm_g"][i]), sv["opre_r"], sv["st_r"], do_r)
        g["ret_norm_g"] = dgn[0]
        dhl, dcw, dcb, dwa, dba, dwx, dbx, dlam = lru_bwd(
            sv["hl"], wt["lru_conv_w"][i], row(wt["lru_conv_b"][i]), lw["wa"], row(wt["lru_b_a"][i]), lw["wx"],
            row(wt["lru_b_x"][i]), row(wt["lru_lambda"][i]), sv["xc"], sv["hs"], do_l)
        g["lru_conv_w"], g["lru_conv_b"] = dcw, dcb[0]
        g["lru_w_a"], g["lru_b_a"], g["lru_w_x"], g["lru_b_x"], g["lru_lambda"] = _diag_blocks(dwa), dba[0], _diag_blocks(dwx), dbx[0], dlam[0]
        dhq, dab, dgcw, dal, ddt, dng = gdn_bwd(sv["hgd"], wt["gdn_conv_w"][i], lw["al"], lw["dt"], lw["ng"], sv["opre_g"],
                                                sv["tmat"], sv["st_g"], do_g)
        g["gdn_conv_w"] = dgcw
        g["gdn_a_log"], g["gdn_dt_bias"] = dal[0, ::HEAD], ddt[0, ::HEAD]
        g["gdn_norm_g"] = dng[0].reshape(GDN_HEADS, HEAD).sum(0)
        dx1 = win_bwd(dr2, dhr, dhl, dhq, dab, lw["w_r"], lw["w_l"], lw["w_g"], lw["w_ab"])
        used = RET_IN + LRU_IN + 4 * GDN_W + AB_ROWS
        bg["w_in"] = jnp.concatenate(
            [wgrad(dhr, sv["x1"], "wgrad_in_r" + tag), wgrad(dhl, sv["x1"], "wgrad_in_l" + tag),
             wgrad(dhq, sv["x1"], "wgrad_in_q" + tag), wgrad(dab, sv["x1"], "wgrad_in_ab" + tag)[0:AB_ROWS],
             jnp.zeros((WIN_T_ROWS - used, D_MODEL), BF16)], axis=0)
        dx, act1, dhg1, dhu1, dy1, dg1, db1 = ffn_bwd(dx1, sv["r1"], sv["x0"], sv["hg1"], sv["hu1"], row(wt["ln_ffn1_g"][i]),
                                                      lw["wg1"], lw["wu1"], lw["wd1"])
        bg["ffn1_w_gate"] = wgrad(dhg1, sv["x0"], "wgrad_gate1" + tag)
        bg["ffn1_w_up"] = wgrad(dhu1, sv["x0"], "wgrad_up1" + tag)
        bg["ffn1_w_down"] = wgrad(act1, dy1, "wgrad_down1" + tag)
        g["ln_ffn1_g"], g["ln_ffn1_b"] = dg1[0], db1[0]
        grads[i] = g
        big[i] = bg
    return loss, dx, {k: jnp.stack([grads[i][k] for i in range(DEPTH)]) for k in grads[0]}, big


def _natural_grad(name, rows):
    if name == "ple_w_proj":
        return rows.reshape(-1, PLE_DIM).T
    return rows.T if name in _TRANSPOSED else rows


_SPLIT = dict(ffn1_w_gate=2, ffn1_w_up=2, ffn1_w_down=1, w_in=2, w_out=1, ffn2_w_gate=2, ffn2_w_up=2, ffn2_w_down=1,
              ple_w_gate=1, ple_w_proj=2)
_CONV = ("lru_conv_w", "gdn_conv_w")
_WHOLE = ("ln_ffn1_g", "ln_ffn1_b", "ret_norm_g", "lru_conv_b", "lru_w_a", "lru_b_a", "lru_w_x", "lru_b_x", "lru_lambda",
          "gdn_a_log", "gdn_dt_bias", "gdn_norm_g", "ln_mix_g", "ln_mix_b", "ln_ffn2_g", "ln_ffn2_b")
_WEIGHTS = ("ln_ffn1_g", "ln_ffn1_b", "ffn1_w_gate", "ffn1_w_up", "ffn1_w_down", "w_in", "ret_norm_g", "lru_conv_w", "lru_conv_b",
            "lru_w_a", "lru_b_a", "lru_w_x", "lru_b_x", "lru_lambda", "gdn_conv_w", "gdn_a_log", "gdn_dt_bias", "gdn_norm_g",
            "w_out", "ln_mix_g", "ln_mix_b", "ffn2_w_gate", "ffn2_w_up", "ffn2_w_down", "ple_w_gate", "ple_w_proj",
            "ln_ffn2_g", "ln_ffn2_b")
_INPUTS = ("x", "p", "positions") + _WEIGHTS + ("loss_target",) + tuple("m_" + n for n in _WEIGHTS) + tuple("v_" + n for n in _WEIGHTS)

BIG_COLS = 1024
SMALL_COLS = LANES
SMALL_ROWS_MULT = 8


def _pack(arrays, dtype, cols, rows_mult):
    flat = jnp.concatenate([a.reshape(-1).astype(dtype) for a in arrays])
    rows = -(-flat.shape[0] // cols)
    rows = -(-rows // rows_mult) * rows_mult
    return jnp.pad(flat, (0, rows * cols - flat.shape[0])).reshape(rows, cols)


def _unpack(packed, shapes):
    flat = packed.reshape(-1)
    out, off = [], 0
    for shp in shapes:
        size = int(np.prod(shp))
        out.append(flat[off:off + size].reshape(shp))
        off += size
    return out


def _as2d(a):
    return a.reshape(-1, a.shape[-1])


def kernel(x, p, positions, ln_ffn1_g, ln_ffn1_b, ffn1_w_gate, ffn1_w_up, ffn1_w_down, w_in, ret_norm_g, lru_conv_w, lru_conv_b, lru_w_a, lru_b_a, lru_w_x, lru_b_x, lru_lambda, gdn_conv_w, gdn_a_log, gdn_dt_bias, gdn_norm_g, w_out, ln_mix_g, ln_mix_b, ffn2_w_gate, ffn2_w_up, ffn2_w_down, ple_w_gate, ple_w_proj, ln_ffn2_g, ln_ffn2_b, loss_target, m_ln_ffn1_g, m_ln_ffn1_b, m_ffn1_w_gate, m_ffn1_w_up, m_ffn1_w_down, m_w_in, m_ret_norm_g, m_lru_conv_w, m_lru_conv_b, m_lru_w_a, m_lru_b_a, m_lru_w_x, m_lru_b_x, m_lru_lambda, m_gdn_conv_w, m_gdn_a_log, m_gdn_dt_bias, m_gdn_norm_g, m_w_out, m_ln_mix_g, m_ln_mix_b, m_ffn2_w_gate, m_ffn2_w_up, m_ffn2_w_down, m_ple_w_gate, m_ple_w_proj, m_ln_ffn2_g, m_ln_ffn2_b, v_ln_ffn1_g, v_ln_ffn1_b, v_ffn1_w_gate, v_ffn1_w_up, v_ffn1_w_down, v_w_in, v_ret_norm_g, v_lru_conv_w, v_lru_conv_b, v_lru_w_a, v_lru_b_a, v_lru_w_x, v_lru_b_x, v_lru_lambda, v_gdn_conv_w, v_gdn_a_log, v_gdn_dt_bias, v_gdn_norm_g, v_w_out, v_ln_mix_g, v_ln_mix_b, v_ffn2_w_gate, v_ffn2_w_up, v_ffn2_w_down, v_ple_w_gate, v_ple_w_proj, v_ln_ffn2_g, v_ln_ffn2_b):
    a = dict(zip(_INPUTS, (x, p, positions, ln_ffn1_g, ln_ffn1_b, ffn1_w_gate, ffn1_w_up, ffn1_w_down, w_in, ret_norm_g, lru_conv_w, lru_conv_b, lru_w_a, lru_b_a, lru_w_x, lru_b_x, lru_lambda, gdn_conv_w, gdn_a_log, gdn_dt_bias, gdn_norm_g, w_out, ln_mix_g, ln_mix_b, ffn2_w_gate, ffn2_w_up, ffn2_w_down, ple_w_gate, ple_w_proj, ln_ffn2_g, ln_ffn2_b, loss_target, m_ln_ffn1_g, m_ln_ffn1_b, m_ffn1_w_gate, m_ffn1_w_up, m_ffn1_w_down, m_w_in, m_ret_norm_g, m_lru_conv_w, m_lru_conv_b, m_lru_w_a, m_lru_b_a, m_lru_w_x, m_lru_b_x, m_lru_lambda, m_gdn_conv_w, m_gdn_a_log, m_gdn_dt_bias, m_gdn_norm_g, m_w_out, m_ln_mix_g, m_ln_mix_b, m_ffn2_w_gate, m_ffn2_w_up, m_ffn2_w_down, m_ple_w_gate, m_ple_w_proj, m_ln_ffn2_g, m_ln_ffn2_b, v_ln_ffn1_g, v_ln_ffn1_b, v_ffn1_w_gate, v_ffn1_w_up, v_ffn1_w_down, v_w_in, v_ret_norm_g, v_lru_conv_w, v_lru_conv_b, v_lru_w_a, v_lru_b_a, v_lru_w_x, v_lru_b_x, v_lru_lambda, v_gdn_conv_w, v_gdn_a_log, v_gdn_dt_bias, v_gdn_norm_g, v_w_out, v_ln_mix_g, v_ln_mix_b, v_ffn2_w_gate, v_ffn2_w_up, v_ffn2_w_down, v_ple_w_gate, v_ple_w_proj, v_ln_ffn2_g, v_ln_ffn2_b)))
    assert len(a) == len(_INPUTS)
    core = lax.axis_index("c")
    chip = 2 * lax.axis_index("x") + lax.axis_index("y")
    big = list(_SPLIT)

    full = gather_layer_weights([a[n][0].astype(BF16) for n in big], [a[n][1].astype(BF16) for n in big], "gather_weights")
    wt = {}
    for i, n in enumerate(big):
        wt[n] = [jnp.concatenate([full[layer][i][k] for k in range(N_CHIPS)], axis=_SPLIT[n] - 1) for layer in range(DEPTH)]
    conv_g = all_gather8(_pack([a[n] for n in _CONV], F32, SMALL_COLS, SMALL_ROWS_MULT), "gather_conv_weights")[0::2]
    conv_g = conv_g.reshape(N_CHIPS, -1)
    off = 0
    for n in _CONV:
        shp = a[n].shape
        size = int(np.prod(shp))
        parts = conv_g[:, off:off + size].reshape((N_CHIPS,) + shp)
        wt[n] = jnp.concatenate([parts[k] for k in range(N_CHIPS)], axis=2)
        off += size
    for n in _WHOLE:
        wt[n] = a[n]

    seq = a["x"].shape[1]
    loss_part, dx, grads, big_src = _local_step(a["x"][0], a["p"][:, 0], a["positions"].reshape(seq, 1), a["loss_target"][0], wt)
    loss = lax.psum(loss_part[0, 0], ("x", "y", "c"))

    pieces = [(r, stride) for _, r, stride in _GRAD_PIECES]
    names = [n for n, _, _ in _GRAD_PIECES]
    layer0 = [big_src[0][n] for n in names]
    layer1 = [big_src[1][n] for n in names]
    from_sibling = sibling_swap_layers(layer0, layer1, "reduce_pair_swap")
    pair = [pair_add(u, v, w, "reduce_pair_add_" + n) for n, u, v, w in zip(names, layer0, layer1, from_sibling)]
    arrived = chip_exchange(pair, pieces, "reduce_chip_exchange")
    my_layer_sum = sum_leading(arrived, "reduce_chip_sum")
    other_layer_sum = sibling_swap(my_layer_sum, "reduce_pair_share")
    reduced = [jnp.where(core == layer, my_layer_sum, other_layer_sum) for layer in range(DEPTH)]
    big_grads = {}
    off = 0
    for n, r, _ in _GRAD_PIECES:
        per_layer = []
        for layer in range(DEPTH):
            rows = reduced[layer][off:off + r]
            if n == "w_in":
                rows = lax.dynamic_slice_in_dim(rows, chip * (WIN_SHARD - WIN_STRIDE), WIN_SHARD, axis=0)
            per_layer.append(_natural_grad(n, rows))
        big_grads[n] = jnp.stack(per_layer)
        off += r

    small_names = list(_WHOLE) + list(_CONV)
    small_local = _pack([grads[n] for n in small_names], F32, SMALL_COLS, SMALL_ROWS_MULT)
    small_sum = sum_leading(all_gather8(small_local, "gather_small_grads"), "sum_small_grads")
    small_grads = dict(zip(small_names, _unpack(small_sum, [grads[n].shape for n in small_names])))
    for n in _CONV:
        width = a[n].shape[2]
        small_grads[n] = lax.dynamic_slice_in_dim(small_grads[n], chip * width, width, axis=2)

    new = {}
    for n in big:
        d, nm, nv = adamw(_as2d(a[n]), _as2d(big_grads[n]), _as2d(a["m_" + n]), _as2d(a["v_" + n]), "adamw_" + n)
        new[n] = tuple(t.reshape(a[n].shape) for t in (d, nm, nv))
    pk = lambda prefix: _pack([a[prefix + n] for n in small_names], F32, SMALL_COLS, SMALL_ROWS_MULT)
    pg = _pack([small_grads[n] for n in small_names], F32, SMALL_COLS, SMALL_ROWS_MULT)
    outs = adamw(pk(""), pg, pk("m_"), pk("v_"), "adamw_small")
    shapes = [a[n].shape for n in small_names]
    for n, d, nm, nv in zip(small_names, *[_unpack(o, shapes) for o in outs]):
        new[n] = (d, nm, nv)
    all_grads = {**big_grads, **small_grads}
    return (loss, dx[None], *[all_grads[n] for n in _WEIGHTS], *[new[n][0] for n in _WEIGHTS],
            *[new[n][1] for n in _WEIGHTS], *[new[n][2] for n in _WEIGHTS])
```

```python
import functools
import math

import numpy as np
import jax
import jax.numpy as jnp
from jax import lax
from jax.experimental import pallas as pl
from jax.experimental.pallas import tpu as pltpu

F32 = jnp.float32
BF16 = jnp.bfloat16

D_MODEL = 1024
D_FF = 2816
PLE_DIM = 256
DEPTH = 2
CHUNK = 64
RET_HEADS = 4
RET_W = 256
LRU_W = 384
LRU_BLOCKS = 6
GDN_HEADS = 6
GDN_W = 384
HEAD = 64
D_IN = 3340
RET_IN = 4 * RET_W
LRU_IN = 2 * LRU_W
GDN_IN = 6 * GDN_W
ROPE_THETA = 10000.0
ALPHA = (2 * DEPTH) ** 0.25
LN_EPS = 1e-5
LRU_C = 8.0
N_CHIPS = 4
N_DEV = 8

ADAM_LR = 0.001
ADAM_B1 = 0.9
ADAM_B2 = 0.999
ADAM_EPS = 1e-08
ADAM_WD = 0.01
ADAM_STEP = 10

LANES = 128
VMEM_LIMIT = 56 * 1024 * 1024
ROW_TILE = 256
ROW_TILE_BWD = 512
SCAN_TILE = 256


def _params(*sem):
    return pltpu.CompilerParams(dimension_semantics=sem, vmem_limit_bytes=VMEM_LIMIT)


def _mm(a, b):
    return jnp.dot(a, b, preferred_element_type=F32)


def _mm_nt(a, b):
    return lax.dot_general(a, b, (((1,), (1,)), ((), ())), preferred_element_type=F32)


def _mm_tn(a, b):
    return lax.dot_general(a, b, (((0,), (0,)), ((), ())), preferred_element_type=F32)


def _split(a):
    hi = a.astype(BF16)
    lo = (a - hi.astype(F32)).astype(BF16)
    return hi, lo


def _mm3(a, b):
    ah, al = _split(a)
    bh, bl = _split(b)
    return _mm(ah, bh) + (_mm(ah, bl) + _mm(al, bh))


def _sigmoid(x):
    return jax.nn.sigmoid(x)


def _log1p(u):
    w = 1.0 + u
    return jnp.where(w == 1.0, u, jnp.log(w) * (u / jnp.where(w == 1.0, 1.0, w - 1.0)))


def _expm1(y):
    u = jnp.exp(y)
    um1 = u - 1.0
    safe = jnp.where((u == 1.0) | (um1 == -1.0), 1.0, jnp.log(jnp.where(u == 0.0, 1.0, u)))
    return jnp.where(u == 1.0, y, jnp.where(um1 == -1.0, -1.0, um1 * (y / safe)))


def _softplus(x):
    return jnp.maximum(x, 0.0) + _log1p(jnp.exp(-jnp.abs(x)))


_GELU_C = math.sqrt(2.0 / math.pi)


def _gelu(x):
    return 0.5 * x * (1.0 + jnp.tanh(_GELU_C * (x + 0.044715 * (x * x * x))))


def _gelu_grad(x):
    t = jnp.tanh(_GELU_C * (x + 0.044715 * (x * x * x)))
    return 0.5 * (1.0 + t) + 0.5 * x * (1.0 - t * t) * (_GELU_C * (1.0 + 3.0 * 0.044715 * (x * x)))


def _silu_and_grad(x):
    s = _sigmoid(x)
    return x * s, s * (1.0 + x * (1.0 - s))


def _group_sum_slab(x):
    lane = lax.broadcasted_iota(jnp.int32, x.shape, 1)
    low = jnp.sum(x[:, 0:LANES // 2], axis=1, keepdims=True)
    high = jnp.sum(x[:, LANES // 2:], axis=1, keepdims=True)
    return jnp.where(lane < LANES // 2, low, high)


def _group_sum(x):
    n = x.shape[1] // LANES
    if n == 1:
        return _group_sum_slab(x)
    return jnp.concatenate([_group_sum_slab(x[:, LANES * i:LANES * (i + 1)]) for i in range(n)], axis=1)


def _rows_prefix_sum(x):
    n = x.shape[0]
    row = lax.broadcasted_iota(jnp.int32, x.shape, 0)
    d = 1
    while d < n:
        x = x + jnp.where(row >= d, pltpu.roll(x, d, 0), 0.0)
        d *= 2
    return x


def _rows_suffix_sum(x):
    n = x.shape[0]
    row = lax.broadcasted_iota(jnp.int32, x.shape, 0)
    d = 1
    while d < n:
        x = x + jnp.where(row < n - d, pltpu.roll(x, n - d, 0), 0.0)
        d *= 2
    return x


def _shift_rows(cur, prev, j):
    row = lax.broadcasted_iota(jnp.int32, cur.shape, 0)
    return jnp.where(row < j, pltpu.roll(prev, j, 0), pltpu.roll(cur, j, 0))


def _shift_rows_up(cur, nxt, j):
    n = cur.shape[0]
    row = lax.broadcasted_iota(jnp.int32, cur.shape, 0)
    return jnp.where(row < n - j, pltpu.roll(cur, n - j, 0), pltpu.roll(nxt, n - j, 0))


def _layer_norm_stats(r):
    mu = jnp.mean(r, axis=-1, keepdims=True)
    d = r - mu
    var = jnp.mean(d * d, axis=-1, keepdims=True)
    rstd = lax.rsqrt(var + LN_EPS)
    return d * rstd, rstd


def _load_resident(step, pairs, sems):
    @pl.when(step == 0)
    def _():
        cps = [pltpu.make_async_copy(h, v, sems.at[i]) for i, (h, v) in enumerate(pairs)]
        for c in cps:
            c.start()
        for c in cps:
            c.wait()


def _row_spec(tile, width):
    return pl.BlockSpec((tile, width), lambda i: (i, 0))


def _full_spec(shape):
    nd = len(shape)
    return pl.BlockSpec(shape, lambda i: (0,) * nd)


_ANY = pl.BlockSpec(memory_space=pl.ANY)


def ffn_fwd(x, ln_g, ln_b, w_gate, w_up, w_down, ple=None):
    s = x.shape[0]
    tm = ROW_TILE
    with_ple = ple is not None
    weights = [w_gate, w_up, w_down] + ([ple[1], ple[2]] if with_ple else [])

    def body(*refs):
        it = iter(refs)
        x_ref, g_ref, b_ref = next(it), next(it), next(it)
        p_ref = next(it) if with_ple else None
        w_hbm = [next(it) for _ in weights]
        hg_ref, hu_ref, r_ref, xn_ref, xnb_ref = next(it), next(it), next(it), next(it), next(it)
        pg_ref, pp_ref = (next(it), next(it)) if with_ple else (None, None)
        w_vm = [next(it) for _ in weights]
        sems = next(it)
        _load_resident(pl.program_id(0), list(zip(w_hbm, w_vm)), sems)
        xv = x_ref[...]
        xb = xv.astype(BF16)
        hg = _mm(xb, w_vm[0][...])
        hu = _mm(xb, w_vm[1][...])
        hg_ref[...] = hg
        hu_ref[...] = hu
        act = (hg * _sigmoid(hg)) * hu
        r = ALPHA * xv + 0.5 * _mm(act.astype(BF16), w_vm[2][...])
        if with_ple:
            pg = _mm(xb, w_vm[3][...])
            pp = _mm(p_ref[...].astype(BF16), w_vm[4][...])
            pg_ref[...] = pg
            pp_ref[...] = pp
            r = r + _sigmoid(pg) * pp
        r_ref[...] = r
        xhat, _ = _layer_norm_stats(r)
        xn = xhat * g_ref[...] + b_ref[...]
        xn_ref[...] = xn
        xnb_ref[...] = xn.astype(BF16)

    d, f = D_MODEL, D_FF
    in_specs = [_row_spec(tm, d), _full_spec((1, d)), _full_spec((1, d))]
    args = [x, ln_g, ln_b]
    if with_ple:
        in_specs.append(_row_spec(tm, PLE_DIM))
        args.append(ple[0])
    in_specs += [_ANY] * len(weights)
    args += weights
    out_shape = [jax.ShapeDtypeStruct((s, f), F32), jax.ShapeDtypeStruct((s, f), F32),
                 jax.ShapeDtypeStruct((s, d), F32), jax.ShapeDtypeStruct((s, d), F32), jax.ShapeDtypeStruct((s, d), BF16)]
    out_specs = [_row_spec(tm, f), _row_spec(tm, f), _row_spec(tm, d), _row_spec(tm, d), _row_spec(tm, d)]
    if with_ple:
        out_shape += [jax.ShapeDtypeStruct((s, d), F32)] * 2
        out_specs += [_row_spec(tm, d)] * 2
    scratch = [pltpu.VMEM(w.shape, w.dtype) for w in weights] + [pltpu.SemaphoreType.DMA((len(weights),))]
    return pl.pallas_call(
        body, name="ffn_fwd_ple" if with_ple else "ffn_fwd", grid=(s // tm,), in_specs=in_specs, out_specs=out_specs,
        out_shape=out_shape, scratch_shapes=scratch, compiler_params=_params("arbitrary"),
    )(*args)


def ffn_bwd(dxn, r, x, hg, hu, ln_g, w_gate, w_up, w_down, ple=None):
    s = x.shape[0]
    with_ple = ple is not None
    d, f = D_MODEL, D_FF
    suffix = "_ple" if with_ple else ""

    tm = ROW_TILE

    def body_a(*refs):
        it = iter(refs)
        dxn_ref, r_ref, hg_ref, hu_ref, g_ref = (next(it) for _ in range(5))
        pg_ref, pp_ref = (next(it), next(it)) if with_ple else (None, None)
        wd_hbm = next(it)
        dr_ref, act_ref, dhg_ref, dhu_ref, dy_ref, dg_ref, db_ref = (next(it) for _ in range(7))
        dpg_ref, dpp_ref = (next(it), next(it)) if with_ple else (None, None)
        wd_vm, sems = next(it), next(it)
        step = pl.program_id(0)
        _load_resident(step, [(wd_hbm, wd_vm)], sems)

        @pl.when(step == 0)
        def _():
            dg_ref[...] = jnp.zeros_like(dg_ref)
            db_ref[...] = jnp.zeros_like(db_ref)

        dxn_v = dxn_ref[...]
        xhat, rstd = _layer_norm_stats(r_ref[...])
        dg_ref[...] += jnp.sum(dxn_v * xhat, axis=0, keepdims=True)
        db_ref[...] += jnp.sum(dxn_v, axis=0, keepdims=True)
        dyh = dxn_v * g_ref[...]
        dr = rstd * (dyh - jnp.mean(dyh, axis=-1, keepdims=True) - xhat * jnp.mean(dyh * xhat, axis=-1, keepdims=True))
        dr_ref[...] = dr
        dy = (0.5 * dr).astype(BF16)
        dy_ref[...] = dy
        da = _mm_nt(dy, wd_vm[...])
        hg_v = hg_ref[...]
        hu_v = hu_ref[...]
        sil, dsil = _silu_and_grad(hg_v)
        act_ref[...] = (sil * hu_v).astype(BF16)
        dhu_ref[...] = (da * sil).astype(BF16)
        dhg_ref[...] = (da * hu_v * dsil).astype(BF16)
        if with_ple:
            sp = _sigmoid(pg_ref[...])
            dpp_ref[...] = (dr * sp).astype(BF16)
            dpg_ref[...] = (dr * pp_ref[...] * sp * (1.0 - sp)).astype(BF16)

    in_specs = [_row_spec(tm, d), _row_spec(tm, d), _row_spec(tm, f), _row_spec(tm, f), _full_spec((1, d))]
    args = [dxn, r, hg, hu, ln_g]
    if with_ple:
        in_specs += [_row_spec(tm, d), _row_spec(tm, d)]
        args += [ple[0], ple[1]]
    out_shape = [jax.ShapeDtypeStruct((s, d), F32), jax.ShapeDtypeStruct((s, f), BF16), jax.ShapeDtypeStruct((s, f), BF16),
                 jax.ShapeDtypeStruct((s, f), BF16), jax.ShapeDtypeStruct((s, d), BF16),
                 jax.ShapeDtypeStruct((1, d), F32), jax.ShapeDtypeStruct((1, d), F32)]
    out_specs = [_row_spec(tm, d), _row_spec(tm, f), _row_spec(tm, f), _row_spec(tm, f), _row_spec(tm, d),
                 _full_spec((1, d)), _full_spec((1, d))]
    if with_ple:
        out_shape += [jax.ShapeDtypeStruct((s, d), BF16)] * 2
        out_specs += [_row_spec(tm, d)] * 2
    first = pl.pallas_call(
        body_a, name="ffn_bwd_hidden" + suffix, grid=(s // tm,), in_specs=in_specs + [_ANY], out_specs=out_specs,
        out_shape=out_shape, scratch_shapes=[pltpu.VMEM(w_down.shape, w_down.dtype), pltpu.SemaphoreType.DMA((1,))],
        compiler_params=_params("arbitrary"),
    )(*args, w_down)
    dr, act, dhg, dhu, dy, dg, db = first[:7]

    tb = min(ROW_TILE_BWD, s)
    weights = [w_gate, w_up] + ([ple[2]] if with_ple else [])

    def body_b(*refs):
        it = iter(refs)
        dr_ref, dhg_ref, dhu_ref = next(it), next(it), next(it)
        dpg_ref = next(it) if with_ple else None
        w_hbm = [next(it) for _ in weights]
        dx_ref = next(it)
        w_vm = [next(it) for _ in weights]
        sems = next(it)
        _load_resident(pl.program_id(0), list(zip(w_hbm, w_vm)), sems)
        dx = ALPHA * dr_ref[...] + _mm_nt(dhg_ref[...], w_vm[0][...]) + _mm_nt(dhu_ref[...], w_vm[1][...])
        if with_ple:
            dx = dx + _mm_nt(dpg_ref[...], w_vm[2][...])
        dx_ref[...] = dx

    in_specs = [_row_spec(tb, d), _row_spec(tb, f), _row_spec(tb, f)] + ([_row_spec(tb, d)] if with_ple else [])
    args = [dr, dhg, dhu] + ([first[7]] if with_ple else [])
    dx = pl.pallas_call(
        body_b, name="ffn_bwd_input" + suffix, grid=(s // tb,), in_specs=in_specs + [_ANY] * len(weights),
        out_specs=_row_spec(tb, d), out_shape=jax.ShapeDtypeStruct((s, d), F32),
        scratch_shapes=[pltpu.VMEM(w.shape, w.dtype) for w in weights] + [pltpu.SemaphoreType.DMA((len(weights),))],
        compiler_params=_params("arbitrary"),
    )(*args, *weights)
    return (dx, act, dhg, dhu, dy, dg, db) + tuple(first[7:])


def win_fwd(x1, w_r, w_l, w_g):
    s = x1.shape[0]
    tm = min(ROW_TILE_BWD, s)
    weights = [w_r, w_l, w_g]

    def body(x_ref, wr_h, wl_h, wg_h, hr_ref, hl_ref, hgd_ref, wr_v, wl_v, wg_v, sems):
        _load_resident(pl.program_id(0), [(wr_h, wr_v), (wl_h, wl_v), (wg_h, wg_v)], sems)
        xb = x_ref[...].astype(BF16)
        hr_ref[...] = _mm(xb, wr_v[...])
        hl_ref[...] = _mm(xb, wl_v[...])
        hgd_ref[...] = _mm(xb, wg_v[...])

    return pl.pallas_call(
        body, name="win_fwd", grid=(s // tm,),
        in_specs=[_row_spec(tm, D_MODEL), _ANY, _ANY, _ANY],
        out_specs=[_row_spec(tm, RET_IN), _row_spec(tm, LRU_IN), _row_spec(tm, GDN_IN)],
        out_shape=[jax.ShapeDtypeStruct((s, RET_IN), F32), jax.ShapeDtypeStruct((s, LRU_IN), F32),
                   jax.ShapeDtypeStruct((s, GDN_IN), F32)],
        scratch_shapes=[pltpu.VMEM(w.shape, w.dtype) for w in weights] + [pltpu.SemaphoreType.DMA((3,))],
        compiler_params=_params("arbitrary"),
    )(x1, *weights)


def win_bwd(dr2, dhr, dhl, dhq, dab, w_r, w_l, w_g, w_ab):
    s = dr2.shape[0]
    tm = min(ROW_TILE_BWD, s)
    weights = [w_r, w_l, w_g, w_ab]
    nq = 4 * GDN_W

    def body(dr_ref, dhr_ref, dhl_ref, dhq_ref, dab_ref, wr_h, wl_h, wg_h, wab_h, dx_ref, wr_v, wl_v, wg_v, wab_v, sems):
        _load_resident(pl.program_id(0), [(wr_h, wr_v), (wl_h, wl_v), (wg_h, wg_v), (wab_h, wab_v)], sems)
        dx_ref[...] = (ALPHA * dr_ref[...] + _mm_nt(dhr_ref[...], wr_v[...]) + _mm_nt(dhl_ref[...], wl_v[...])
                       + _mm_nt(dhq_ref[...], wg_v[:, 0:nq]) + _mm_nt(dab_ref[...], wab_v[...]))

    return pl.pallas_call(
        body, name="win_bwd", grid=(s // tm,),
        in_specs=[_row_spec(tm, D_MODEL), _row_spec(tm, RET_IN), _row_spec(tm, LRU_IN), _row_spec(tm, nq), _row_spec(tm, LANES),
                  _ANY, _ANY, _ANY, _ANY],
        out_specs=_row_spec(tm, D_MODEL),
        out_shape=jax.ShapeDtypeStruct((s, D_MODEL), F32),
        scratch_shapes=[pltpu.VMEM(w.shape, w.dtype) for w in weights] + [pltpu.SemaphoreType.DMA((4,))],
        compiler_params=_params("arbitrary"),
    )(dr2, dhr, dhl, dhq, dab, *weights)


def out_fwd(o_r, o_l, o_g, x1, w_out, ln_g, ln_b):
    s = x1.shape[0]
    tm = ROW_TILE

    def body(or_ref, ol_ref, og_ref, x_ref, g_ref, b_ref, w_h, r_ref, xn_ref, xnb_ref, ocat_ref, w_v, sems):
        _load_resident(pl.program_id(0), [(w_h, w_v)], sems)
        ocat = jnp.concatenate([or_ref[...], ol_ref[...], og_ref[...]], axis=1).astype(BF16)
        ocat_ref[...] = ocat
        r = ALPHA * x_ref[...] + _mm(ocat, w_v[...])
        r_ref[...] = r
        xhat, _ = _layer_norm_stats(r)
        xn = xhat * g_ref[...] + b_ref[...]
        xn_ref[...] = xn
        xnb_ref[...] = xn.astype(BF16)

    d = D_MODEL
    return pl.pallas_call(
        body, name="out_fwd", grid=(s // tm,),
        in_specs=[_row_spec(tm, RET_W), _row_spec(tm, LRU_W), _row_spec(tm, GDN_W), _row_spec(tm, d),
                  _full_spec((1, d)), _full_spec((1, d)), _ANY],
        out_specs=[_row_spec(tm, d)] * 4,
        out_shape=[jax.ShapeDtypeStruct((s, d), F32)] * 2 + [jax.ShapeDtypeStruct((s, d), BF16)] * 2,
        scratch_shapes=[pltpu.VMEM(w_out.shape, w_out.dtype), pltpu.SemaphoreType.DMA((1,))],
        compiler_params=_params("arbitrary"),
    )(o_r, o_l, o_g, x1, ln_g, ln_b, w_out)


def out_bwd(dxn, r2, ln_g, w_out):
    s = dxn.shape[0]
    tm = ROW_TILE

    def body(dxn_ref, r_ref, g_ref, w_h, dr_ref, drb_ref, dor_ref, dol_ref, dog_ref, dg_ref, db_ref, w_v, sems):
        step = pl.program_id(0)
        _load_resident(step, [(w_h, w_v)], sems)

        @pl.when(step == 0)
        def _():
            dg_ref[...] = jnp.zeros_like(dg_ref)
            db_ref[...] = jnp.zeros_like(db_ref)

        dxn_v = dxn_ref[...]
        xhat, rstd = _layer_norm_stats(r_ref[...])
        dg_ref[...] += jnp.sum(dxn_v * xhat, axis=0, keepdims=True)
        db_ref[...] += jnp.sum(dxn_v, axis=0, keepdims=True)
        dyh = dxn_v * g_ref[...]
        dr = rstd * (dyh - jnp.mean(dyh, axis=-1, keepdims=True) - xhat * jnp.mean(dyh * xhat, axis=-1, keepdims=True))
        dr_ref[...] = dr
        drb = dr.astype(BF16)
        drb_ref[...] = drb
        dor_ref[...] = _mm_nt(drb, w_v[0:RET_W, :])
        dol_ref[...] = _mm_nt(drb, w_v[RET_W:RET_W + LRU_W, :])
        dog_ref[...] = _mm_nt(drb, w_v[RET_W + LRU_W:, :])

    d = D_MODEL
    return pl.pallas_call(
        body, name="out_bwd", grid=(s // tm,),
        in_specs=[_row_spec(tm, d), _row_spec(tm, d), _full_spec((1, d)), _ANY],
        out_specs=[_row_spec(tm, d), _row_spec(tm, d), _row_spec(tm, RET_W), _row_spec(tm, LRU_W), _row_spec(tm, GDN_W),
                   _full_spec((1, d)), _full_spec((1, d))],
        out_shape=[jax.ShapeDtypeStruct((s, d), F32), jax.ShapeDtypeStruct((s, d), BF16),
                   jax.ShapeDtypeStruct((s, RET_W), F32), jax.ShapeDtypeStruct((s, LRU_W), F32),
                   jax.ShapeDtypeStruct((s, GDN_W), F32), jax.ShapeDtypeStruct((1, d), F32), jax.ShapeDtypeStruct((1, d), F32)],
        scratch_shapes=[pltpu.VMEM(w_out.shape, w_out.dtype), pltpu.SemaphoreType.DMA((1,))],
        compiler_params=_params("arbitrary"),
    )(dxn, r2, ln_g, w_out)


def wgrad(a, b, name, out_dtype=BF16):
    s, m = a.shape
    n = b.shape[1]
    tk = 1024 if s % 1024 == 0 else s
    tm = next((c for c in (1408, 1024, 768, 512, 384, 256) if m % c == 0), m)
    tn = next((c for c in (1408, 1152, 1024, 768, 512) if n % c == 0), n)
    nk = s // tk

    def body(a_ref, b_ref, o_ref, acc_ref):
        k = pl.program_id(2)

        @pl.when(k == 0)
        def _():
            acc_ref[...] = jnp.zeros_like(acc_ref)

        acc_ref[...] += _mm_tn(a_ref[...].astype(BF16), b_ref[...].astype(BF16))

        @pl.when(k == nk - 1)
        def _():
            o_ref[...] = acc_ref[...].astype(o_ref.dtype)

    return pl.pallas_call(
        body, name=name, grid=(m // tm, n // tn, nk),
        in_specs=[pl.BlockSpec((tk, tm), lambda i, j, k: (k, i)), pl.BlockSpec((tk, tn), lambda i, j, k: (k, j))],
        out_specs=pl.BlockSpec((tm, tn), lambda i, j, k: (i, j)),
        out_shape=jax.ShapeDtypeStruct((m, n), out_dtype),
        scratch_shapes=[pltpu.VMEM((tm, tn), F32)],
        compiler_params=_params("arbitrary", "arbitrary", "arbitrary"),
    )(a, b)


def loss_and_grad(y, target):
    s, d = y.shape
    tm = ROW_TILE

    def body(y_ref, t_ref, dy_ref, l_ref):
        @pl.when(pl.program_id(0) == 0)
        def _():
            l_ref[...] = jnp.zeros_like(l_ref)

        err = y_ref[...] - t_ref[...]
        dy_ref[...] = err / d
        l_ref[...] += 0.5 * jnp.sum(jnp.mean(err * err, axis=-1, keepdims=True), axis=0, keepdims=True)

    return pl.pallas_call(
        body, name="loss_and_grad", grid=(s // tm,),
        in_specs=[_row_spec(tm, d), _row_spec(tm, d)],
        out_specs=[_row_spec(tm, d), _full_spec((1, 1))],
        out_shape=[jax.ShapeDtypeStruct((s, d), F32), jax.ShapeDtypeStruct((1, 1), F32)],
        compiler_params=_params("arbitrary"),
    )(y, target)


def _ret_consts():
    lg = np.log1p(-np.exp2(-5.0 - np.arange(RET_HEADS, dtype=np.float64)))
    idx = np.arange(CHUNK, dtype=np.float64)
    intra = np.exp(np.abs(idx[:, None] - idx[None, :])[None] * lg[:, None, None])
    cross = np.repeat(np.exp((idx + 1.0)[:, None] * lg[None, :]), HEAD, axis=1)
    tail = np.repeat(np.exp((CHUNK - 1.0 - idx)[:, None] * lg[None, :]), HEAD, axis=1)
    dec = np.repeat(np.exp(CHUNK * lg)[None, :], HEAD, axis=1)
    half = HEAD // 2
    inv_freq = (ROPE_THETA ** (-jnp.arange(half, dtype=F32) / half))
    invf = jnp.tile(inv_freq, 2 * LANES // HEAD)[None, :]
    sgn = np.tile(np.concatenate([-np.ones(half), np.ones(half)]), LANES // HEAD)[None, :]
    f = lambda a: jnp.asarray(a, F32)
    return dict(intra=f(intra), cross=f(cross), tail=f(tail), dec=f(dec), invf=invf, sgn=f(sgn))


def _swap_halves(t):
    lane = lax.broadcasted_iota(jnp.int32, t.shape, 1)
    return jnp.where((lane & 32) == 0, pltpu.roll(t, LANES - 32, 1), pltpu.roll(t, 32, 1))


def _rope(t, c, s):
    return t * c + _swap_halves(t) * s


def _rope_transposed(g, c, s):
    return g * c + _swap_halves(g * s)


def _head_mask(hd):
    lane = lax.broadcasted_iota(jnp.int32, (1, LANES), 1)
    return ((lane >= HEAD * hd) & (lane < HEAD * (hd + 1))).astype(F32)


def _block_diag_mask():
    r = lax.broadcasted_iota(jnp.int32, (LANES, LANES), 0)
    c = lax.broadcasted_iota(jnp.int32, (LANES, LANES), 1)
    return ((r >= HEAD) == (c >= HEAD)).astype(F32)


RET_STEP_CHUNKS = 4


def _ret_specs(n_of, gch):
    cst = lambda shape: pl.BlockSpec(shape, lambda i: (0,) * len(shape))
    return [pl.BlockSpec((CHUNK * gch, RET_IN), lambda i: (n_of(i), 0)), pl.BlockSpec((CHUNK * gch, 1), lambda i: (n_of(i), 0)),
            cst((1, LANES)), cst((1, LANES)), cst((RET_HEADS, CHUNK, CHUNK)), cst((CHUNK, RET_W)), cst((CHUNK, RET_W)),
            cst((1, RET_W)), cst((1, RET_W))]


def ret_fwd(hr, pos, norm_g):
    s = hr.shape[0]
    n_chunks = s // CHUNK
    cs = _ret_consts()
    n_slab = RET_W // LANES

    gch = min(RET_STEP_CHUNKS, n_chunks)

    def body(hr_ref, pos_ref, invf_ref, sgn_ref, intra_ref, cross_ref, tail_ref, dec_ref, g_ref, o_ref, opre_ref, st_ref, state):
        @pl.when(pl.program_id(0) == 0)
        def _():
            state[...] = jnp.zeros_like(state)

        bd = _block_diag_mask()
        sts = [state[LANES * sl:LANES * (sl + 1), :] for sl in range(n_slab)]
        for c in range(gch):
            tok = slice(CHUNK * c, CHUNK * (c + 1))
            ang = pos_ref[tok, :].astype(F32) * invf_ref[...]
            cosv = jnp.cos(ang)
            sinv = jnp.sin(ang) * sgn_ref[...]
            for sl in range(n_slab):
                lanes = slice(LANES * sl, LANES * (sl + 1))
                q = hr_ref[tok, LANES * sl:LANES * (sl + 1)]
                k = hr_ref[tok, RET_W + LANES * sl:RET_W + LANES * (sl + 1)]
                v = hr_ref[tok, 2 * RET_W + LANES * sl:2 * RET_W + LANES * (sl + 1)]
                gate = hr_ref[tok, 3 * RET_W + LANES * sl:3 * RET_W + LANES * (sl + 1)]
                qt = _rope(q, cosv, sinv) * (HEAD ** -0.5)
                kt = _rope(k, cosv, sinv)
                st = sts[sl]
                st_ref[RET_W * c + LANES * sl:RET_W * c + LANES * (sl + 1), :] = st
                o = _mm(qt * cross_ref[:, lanes], st)
                for hd in range(2):
                    m = _head_mask(hd)
                    sc = _mm_nt(qt * m, kt) * intra_ref[2 * sl + hd]
                    o = o + _mm(sc, v) * m
                sts[sl] = st * dec_ref[:, lanes] + _mm_tn(kt, v * tail_ref[:, lanes]) * bd
                opre_ref[tok, lanes] = o
                mu = _group_sum_slab(o) * (1.0 / HEAD)
                dlt = o - mu
                var = _group_sum_slab(dlt * dlt) * (1.0 / HEAD)
                on = dlt * lax.rsqrt(var + 1e-5)
                o_ref[tok, lanes] = on * g_ref[:, lanes] * (gate * _sigmoid(gate))
        for sl in range(n_slab):
            state[LANES * sl:LANES * (sl + 1), :] = sts[sl]

    out_row = lambda w: pl.BlockSpec((CHUNK * gch, w), lambda i: (i, 0))
    return pl.pallas_call(
        body, name="ret_fwd", grid=(n_chunks // gch,),
        in_specs=_ret_specs(lambda i: i, gch),
        out_specs=[out_row(RET_W), out_row(RET_W), pl.BlockSpec((RET_W * gch, LANES), lambda i: (i, 0))],
        out_shape=[jax.ShapeDtypeStruct((s, RET_W), F32), jax.ShapeDtypeStruct((s, RET_W), F32),
                   jax.ShapeDtypeStruct((n_chunks * RET_W, LANES), F32)],
        scratch_shapes=[pltpu.VMEM((RET_W, LANES), F32)],
        compiler_params=_params("arbitrary"),
    )(hr, pos, cs["invf"], cs["sgn"], cs["intra"], cs["cross"], cs["tail"], cs["dec"], norm_g)


def ret_bwd(hr, pos, norm_g, opre, states, dout):
    s = hr.shape[0]
    n_chunks = s // CHUNK
    cs = _ret_consts()
    n_slab = RET_W // LANES
    gch = min(RET_STEP_CHUNKS, n_chunks)
    rev = lambda i: n_chunks // gch - 1 - i

    def body(hr_ref, pos_ref, invf_ref, sgn_ref, intra_ref, cross_ref, tail_ref, dec_ref, g_ref, opre_ref, st_ref, do_ref,
             dh_ref, dg_ref, gstate):
        @pl.when(pl.program_id(0) == 0)
        def _():
            gstate[...] = jnp.zeros_like(gstate)
            dg_ref[...] = jnp.zeros_like(dg_ref)

        bd = _block_diag_mask()
        gss = [gstate[LANES * sl:LANES * (sl + 1), :] for sl in range(n_slab)]
        dgs = [jnp.zeros((1, LANES), F32) for _ in range(n_slab)]
        for c in reversed(range(gch)):
            tok = slice(CHUNK * c, CHUNK * (c + 1))
            ang = pos_ref[tok, :].astype(F32) * invf_ref[...]
            cosv = jnp.cos(ang)
            sinv = jnp.sin(ang) * sgn_ref[...]
            for sl in range(n_slab):
                lanes = slice(LANES * sl, LANES * (sl + 1))
                q = hr_ref[tok, LANES * sl:LANES * (sl + 1)]
                k = hr_ref[tok, RET_W + LANES * sl:RET_W + LANES * (sl + 1)]
                v = hr_ref[tok, 2 * RET_W + LANES * sl:2 * RET_W + LANES * (sl + 1)]
                gate = hr_ref[tok, 3 * RET_W + LANES * sl:3 * RET_W + LANES * (sl + 1)]
                qt = _rope(q, cosv, sinv) * (HEAD ** -0.5)
                kt = _rope(k, cosv, sinv)
                o = opre_ref[tok, lanes]
                mu = _group_sum_slab(o) * (1.0 / HEAD)
                dlt = o - mu
                var = _group_sum_slab(dlt * dlt) * (1.0 / HEAD)
                rstd = lax.rsqrt(var + 1e-5)
                on = dlt * rstd
                sil, dsil = _silu_and_grad(gate)
                dout_v = do_ref[tok, lanes]
                gn = g_ref[:, lanes]
                dgs[sl] = dgs[sl] + jnp.sum(dout_v * on * sil, axis=0, keepdims=True)
                d_on = dout_v * gn * sil
                dgate = dout_v * on * gn * dsil
                d_o = rstd * (d_on - _group_sum_slab(d_on) * (1.0 / HEAD) - on * (_group_sum_slab(d_on * on) * (1.0 / HEAD)))
                st = st_ref[RET_W * c + LANES * sl:RET_W * c + LANES * (sl + 1), :]
                gs = gss[sl]
                cross = cross_ref[:, lanes]
                tail = tail_ref[:, lanes]
                dqt = _mm_nt(d_o, st) * cross
                ds_here = _mm_tn(qt * cross, d_o) * bd
                vt = v * tail
                dkt = _mm_nt(vt, gs)
                dv = _mm(kt, gs) * tail
                for hd in range(2):
                    m = _head_mask(hd)
                    qm = qt * m
                    dom = d_o * m
                    intra = intra_ref[2 * sl + hd]
                    sc = _mm_nt(qm, kt) * intra
                    dsc = _mm_nt(dom, v) * intra
                    dqt = dqt + _mm(dsc, kt) * m
                    dkt = dkt + _mm_tn(dsc, qm)
                    dv = dv + _mm_tn(sc, dom)
                gss[sl] = gs * dec_ref[:, lanes] + ds_here
                dh_ref[tok, LANES * sl:LANES * (sl + 1)] = _rope_transposed(dqt * (HEAD ** -0.5), cosv, sinv).astype(BF16)
                dh_ref[tok, RET_W + LANES * sl:RET_W + LANES * (sl + 1)] = _rope_transposed(dkt, cosv, sinv).astype(BF16)
                dh_ref[tok, 2 * RET_W + LANES * sl:2 * RET_W + LANES * (sl + 1)] = dv.astype(BF16)
                dh_ref[tok, 3 * RET_W + LANES * sl:3 * RET_W + LANES * (sl + 1)] = dgate.astype(BF16)
        for sl in range(n_slab):
            gstate[LANES * sl:LANES * (sl + 1), :] = gss[sl]
            dg_ref[:, LANES * sl:LANES * (sl + 1)] += dgs[sl]

    row = lambda w: pl.BlockSpec((CHUNK * gch, w), lambda i: (rev(i), 0))
    return pl.pallas_call(
        body, name="ret_bwd", grid=(n_chunks // gch,),
        in_specs=_ret_specs(rev, gch) + [row(RET_W), pl.BlockSpec((RET_W * gch, LANES), lambda i: (rev(i), 0)), row(RET_W)],
        out_specs=[row(RET_IN), pl.BlockSpec((1, RET_W), lambda i: (0, 0))],
        out_shape=[jax.ShapeDtypeStruct((s, RET_IN), BF16), jax.ShapeDtypeStruct((1, RET_W), F32)],
        scratch_shapes=[pltpu.VMEM((RET_W, LANES), F32)],
        compiler_params=_params("arbitrary"),
    )(hr, pos, cs["invf"], cs["sgn"], cs["intra"], cs["cross"], cs["tail"], cs["dec"], norm_g, opre, states, dout)


def _lru_gates(xc, wa_ref, ba_ref, wx_ref, bx_ref, lam_ref):
    xcb = xc.astype(BF16)
    r = _sigmoid(_mm(xcb, wa_ref[...].astype(BF16)) + ba_ref[...])
    ig = _sigmoid(_mm(xcb, wx_ref[...].astype(BF16)) + bx_ref[...])
    lam = lam_ref[...]
    ls = jnp.minimum(lam, 0.0) - _log1p(jnp.exp(-jnp.abs(lam)))
    la = (LRU_C * r) * ls
    a = jnp.exp(la)
    mult = jnp.sqrt(-_expm1(2.0 * la))
    return r, ig, ls, a, mult


def _lru_conv(x, xprev, w_ref, b_ref):
    xc = b_ref[...] + w_ref[3:4, :] * x
    for j in (1, 2, 3):
        xc = xc + w_ref[3 - j:4 - j, :] * _shift_rows(x, xprev, j)
    return xc


def lru_fwd(hl, conv_w, conv_b, w_a, b_a, w_x, b_x, lam):
    s = hl.shape[0]
    ts = SCAN_TILE
    w = LRU_W

    def body(hl_ref, hp_ref, cw_ref, cb_ref, wa_ref, ba_ref, wx_ref, bx_ref, lam_ref, o_ref, xc_ref, h_ref, carry):
        i = pl.program_id(0)

        @pl.when(i == 0)
        def _():
            carry[...] = jnp.zeros_like(carry)

        x = hl_ref[:, 0:w]
        gate = hl_ref[:, w:2 * w]
        xprev = hp_ref[...] * (i > 0).astype(F32)
        xc = _lru_conv(x, xprev, cw_ref, cb_ref)
        xc_ref[...] = xc
        _, ig, _, a, mult = _lru_gates(xc, wa_ref, ba_ref, wx_ref, bx_ref, lam_ref)
        b = mult * (ig * xc)
        row = lax.broadcasted_iota(jnp.int32, (ts, w), 0)
        d = 1
        while d < ts:
            ap = jnp.where(row >= d, pltpu.roll(a, d, 0), 1.0)
            bp = jnp.where(row >= d, pltpu.roll(b, d, 0), 0.0)
            b = a * bp + b
            a = a * ap
            d *= 2
        h = b + a * carry[0:1, :]
        h_ref[...] = h
        carry[0:1, :] = h[ts - 1:ts, :]
        o_ref[...] = h * _gelu(gate)

    cst = lambda shape: pl.BlockSpec(shape, lambda i: (0, 0))
    return pl.pallas_call(
        body, name="lru_fwd", grid=(s // ts,),
        in_specs=[_row_spec(ts, 2 * w), pl.BlockSpec((ts, w), lambda i: (jnp.maximum(i - 1, 0), 0)),
                  cst((4, w)), cst((1, w)), cst((w, w)), cst((1, w)), cst((w, w)), cst((1, w)), cst((1, w))],
        out_specs=[_row_spec(ts, w)] * 3,
        out_shape=[jax.ShapeDtypeStruct((s, w), F32)] * 3,
        scratch_shapes=[pltpu.VMEM((8, w), F32)],
        compiler_params=_params("arbitrary"),
    )(hl, hl, conv_w, conv_b, w_a, b_a, w_x, b_x, lam)


def lru_bwd(hl, conv_w, conv_b, w_a, b_a, w_x, b_x, lam, xc_saved, h_saved, dout):
    s = hl.shape[0]
    ts = SCAN_TILE
    w = LRU_W
    nb = s // ts
    rev = lambda i: nb - 1 - i

    def body(hl_ref, hp_ref, cw_ref, cb_ref, wa_ref, ba_ref, wx_ref, bx_ref, lam_ref, xc_ref, h_ref, hprev_ref, do_ref,
             dhl_ref, dcw_ref, dcb_ref, dwa_ref, dba_ref, dwx_ref, dbx_ref, dlam_ref, carry, dxc_next):
        i = pl.program_id(0)
        blk = nb - 1 - i

        @pl.when(i == 0)
        def _():
            carry[...] = jnp.zeros_like(carry)
            dxc_next[...] = jnp.zeros_like(dxc_next)
            for ref in (dcw_ref, dcb_ref, dwa_ref, dba_ref, dwx_ref, dbx_ref, dlam_ref):
                ref[...] = jnp.zeros_like(ref)

        first = (blk > 0).astype(F32)
        x = hl_ref[:, 0:w]
        gate = hl_ref[:, w:2 * w]
        xprev = hp_ref[...] * first
        xc = xc_ref[...]
        h = h_ref[...]
        hprev = hprev_ref[...] * first
        r, ig, ls, a, mult = _lru_gates(xc, wa_ref, ba_ref, wx_ref, bx_ref, lam_ref)
        do = do_ref[...]
        dh = do * _gelu(gate)
        dgate = do * h * _gelu_grad(gate)
        row = lax.broadcasted_iota(jnp.int32, (ts, w), 0)
        ca = jnp.where(row < ts - 1, pltpu.roll(a, ts - 1, 0), 1.0)
        cb = dh
        d = 1
        while d < ts:
            an = jnp.where(row < ts - d, pltpu.roll(ca, ts - d, 0), 1.0)
            bn = jnp.where(row < ts - d, pltpu.roll(cb, ts - d, 0), 0.0)
            cb = cb + ca * bn
            ca = ca * an
            d *= 2
        lamb = cb + ca * carry[0:1, :]
        carry[0:1, :] = a[0:1, :] * lamb[0:1, :]
        h_before = _shift_rows(h, hprev, 1)
        da = lamb * h_before
        ix = ig * xc
        dmult = lamb * ix
        dig = lamb * mult * xc
        dxc = lamb * mult * ig
        dla = (da - dmult * a / mult) * a
        dr = dla * LRU_C * ls
        dlam_ref[...] += jnp.sum(dla * LRU_C * r, axis=0, keepdims=True) * _sigmoid(-lam_ref[...])
        dpa = dr * r * (1.0 - r)
        dpx = dig * ig * (1.0 - ig)
        dba_ref[...] += jnp.sum(dpa, axis=0, keepdims=True)
        dbx_ref[...] += jnp.sum(dpx, axis=0, keepdims=True)
        dpab = dpa.astype(BF16)
        dpxb = dpx.astype(BF16)
        xcb = xc.astype(BF16)
        dxc = dxc + _mm_nt(dpab, wa_ref[...].astype(BF16)) + _mm_nt(dpxb, wx_ref[...].astype(BF16))
        dwa_ref[...] += _mm_tn(xcb, dpab)
        dwx_ref[...] += _mm_tn(xcb, dpxb)
        dcb_ref[...] += jnp.sum(dxc, axis=0, keepdims=True)
        nxt = dxc_next[...]
        dx = cw_ref[3:4, :] * dxc
        dcw_ref[3:4, :] += jnp.sum(dxc * x, axis=0, keepdims=True)
        for j in (1, 2, 3):
            dx = dx + cw_ref[3 - j:4 - j, :] * _shift_rows_up(dxc, nxt, j)
            dcw_ref[3 - j:4 - j, :] += jnp.sum(dxc * _shift_rows(x, xprev, j), axis=0, keepdims=True)
        dxc_next[...] = dxc
        dhl_ref[:, 0:w] = dx.astype(BF16)
        dhl_ref[:, w:2 * w] = dgate.astype(BF16)

    cst = lambda shape: pl.BlockSpec(shape, lambda i: (0, 0))
    rowr = lambda width: pl.BlockSpec((ts, width), lambda i: (rev(i), 0))
    prevr = lambda width: pl.BlockSpec((ts, width), lambda i: (jnp.maximum(rev(i) - 1, 0), 0))
    return pl.pallas_call(
        body, name="lru_bwd", grid=(nb,),
        in_specs=[rowr(2 * w), prevr(w), cst((4, w)), cst((1, w)), cst((w, w)), cst((1, w)), cst((w, w)), cst((1, w)), cst((1, w)),
                  rowr(w), rowr(w), prevr(w), rowr(w)],
        out_specs=[rowr(2 * w), cst((4, w)), cst((1, w)), cst((w, w)), cst((1, w)), cst((w, w)), cst((1, w)), cst((1, w))],
        out_shape=[jax.ShapeDtypeStruct((s, 2 * w), BF16), jax.ShapeDtypeStruct((4, w), F32), jax.ShapeDtypeStruct((1, w), F32),
                   jax.ShapeDtypeStruct((w, w), F32), jax.ShapeDtypeStruct((1, w), F32), jax.ShapeDtypeStruct((w, w), F32),
                   jax.ShapeDtypeStruct((1, w), F32), jax.ShapeDtypeStruct((1, w), F32)],
        scratch_shapes=[pltpu.VMEM((8, w), F32), pltpu.VMEM((ts, w), F32)],
        compiler_params=_params("arbitrary"),
    )(hl, hl, conv_w, conv_b, w_a, b_a, w_x, b_x, lam, xc_saved, h_saved, h_saved, dout)


GDN_QKV = 3 * GDN_W
GDN_STEP_CHUNKS = 4
GDN_BWD_STEP_CHUNKS = 2


def _tri_inverse_many(nms):
    r = lax.broadcasted_iota(jnp.int32, nms[0].shape, 0)
    c = lax.broadcasted_iota(jnp.int32, nms[0].shape, 1)
    eye = (r == c).astype(F32)
    ts = [eye - nm for nm in nms]
    ps = list(nms)
    for _ in range(5):
        ps = [_mm3(p, p) for p in ps]
        ts = [t + _mm3(t, p) for t, p in zip(ts, ps)]
    return ts


def _gdn_front(hx_ref, hprev, cw_ref, al_ref, dt_ref):
    w = GDN_W
    x = hx_ref[:, 0:GDN_QKV]
    y = cw_ref[3:4, :] * x
    for j in (1, 2, 3):
        y = y + cw_ref[3 - j:4 - j, :] * _shift_rows(x, hprev, j)
    qkv, dsil = _silu_and_grad(y)
    q, k, v = qkv[:, 0:w], qkv[:, w:2 * w], qkv[:, 2 * w:3 * w]
    rq = lax.rsqrt(_group_sum(q * q) + 1e-6)
    rk = lax.rsqrt(_group_sum(k * k) + 1e-6)
    beta = _sigmoid(hx_ref[:, 5 * w:6 * w])
    sp_in = hx_ref[:, 4 * w:5 * w] + dt_ref[...]
    neg_a = -jnp.exp(al_ref[...])
    g = neg_a * _softplus(sp_in)
    n_c = g.shape[0] // CHUNK
    gc = jnp.concatenate([_rows_prefix_sum(g[CHUNK * c:CHUNK * (c + 1)]) for c in range(n_c)], axis=0)
    return dict(x=x, dsil=dsil, qn=q * rq, kn=k * rk, v=v, rq=rq, rk=rk, beta=beta, sp_in=sp_in, neg_a=neg_a, g=g, gc=gc)


def _stack_heads(x):
    return jnp.concatenate([x * _head_mask(0), x * _head_mask(1)], axis=0)


def _unstack_heads(y):
    return y[0:CHUNK] + y[CHUNK:2 * CHUNK]


def _head_transpose(x):
    return jnp.concatenate([x[:, 0:HEAD].T, x[:, HEAD:2 * HEAD].T], axis=1)


def _head_total(x):
    cols = jnp.broadcast_to(jnp.sum(x, axis=0, keepdims=True), (8, LANES))
    return _group_sum_slab(cols)[0:1]


def _slab_tri_masks():
    r = lax.broadcasted_iota(jnp.int32, (CHUNK, LANES), 0)
    c = lax.broadcasted_iota(jnp.int32, (CHUNK, LANES), 1) & (HEAD - 1)
    return r >= c, r > c


def _gdn_slab(fr, c, sl, tri):
    lower, strict = tri
    ls = lambda a: a[CHUNK * c:CHUNK * (c + 1), LANES * sl:LANES * (sl + 1)]
    k = ls(fr["kn"])
    q = ls(fr["qn"]) * (HEAD ** -0.5)
    v = ls(fr["v"])
    beta = ls(fr["beta"])
    gc = ls(fr["gc"])
    e = jnp.exp(gc)
    gl = gc[CHUNK - 1:CHUNK, :]
    xt = jnp.exp(gl - gc)
    dec = jnp.where(lower, jnp.exp(jnp.minimum(gc - _head_transpose(gc), 0.0)), 0.0)
    kbd = _stack_heads(k)
    kk = _mm_nt(k, kbd)
    qkr = _mm_nt(q, kbd)
    return dict(k=k, q=q, v=v, beta=beta, e=e, egl=jnp.exp(gl), xt=xt, dec=dec, kk=kk, qkr=qkr, kbd=kbd,
                nm=jnp.where(strict, beta * kk * dec, 0.0))


def gdn_fwd(hx, conv_w, a_log_e, dt_bias_e, norm_g_e, fused=None):
    s = hx.shape[0]
    n_chunks = s // CHUNK
    w = GDN_W
    n_slab = w // LANES
    gch = min(GDN_STEP_CHUNKS, n_chunks)

    def body(hx_ref, hp_ref, cw_ref, al_ref, dt_ref, ng_ref, o_ref, opre_ref, t_ref, st_ref, state):
        n = pl.program_id(0)

        @pl.when(n == 0)
        def _():
            state[...] = jnp.zeros_like(state)

        fr = _gdn_front(hx_ref, hp_ref[...] * (n > 0).astype(F32), cw_ref, al_ref, dt_ref)
        tri = _slab_tri_masks()
        bd = _block_diag_mask()
        sts = [state[LANES * sl:LANES * (sl + 1), :] for sl in range(n_slab)]
        slabs = [[_gdn_slab(fr, c, sl, tri) for sl in range(n_slab)] for c in range(gch)]
        tbd = _tri_inverse_many([_stack_heads(sq["nm"]) for row_ in slabs for sq in row_])
        o_rows = []
        for c in range(gch):
            ts, outs = [], []
            st_ref[w * c:w * (c + 1), :] = jnp.concatenate(sts, axis=0)
            for sl in range(n_slab):
                sq = slabs[c][sl]
                t = _unstack_heads(tbd[n_slab * c + sl])
                ts.append(t)
                u = _mm(t, _stack_heads(sq["v"] * sq["beta"]))
                wk = _mm(t, _stack_heads(sq["k"] * (sq["beta"] * sq["e"])))
                st = sts[sl]
                vnew = u - _mm(wk, st)
                outs.append(_mm(sq["q"] * sq["e"], st) + _mm(sq["qkr"] * sq["dec"], _stack_heads(vnew)))
                sts[sl] = st * sq["egl"] + _mm_tn(sq["k"] * sq["xt"], vnew) * bd
            t_ref[CHUNK * c:CHUNK * (c + 1), :] = jnp.concatenate(ts, axis=1)
            o_rows.append(jnp.concatenate(outs, axis=1))
        state[...] = jnp.concatenate(sts, axis=0)
        o = jnp.concatenate(o_rows, axis=0)
        opre_ref[...] = o
        rinv = lax.rsqrt(_group_sum(o * o) * (1.0 / HEAD) + 1e-6)
        z = hx_ref[:, 3 * w:4 * w]
        o_ref[...] = (o * rinv) * ng_ref[...] * (z * _sigmoid(z))

    cst = lambda shape: pl.BlockSpec(shape, lambda i: (0, 0))
    row = lambda width: pl.BlockSpec((CHUNK * gch, width), lambda i: (i, 0))
    f_in, f_out, f_scr, _ = fused if fused is not None else ([], [], [], None)
    outs = pl.pallas_call(
        _fuse(body, 6, 4, 1, n_chunks // gch, fused), name="gdn_fwd" + ("_gather" if fused is not None else ""),
        grid=(n_chunks // gch,),
        in_specs=[row(GDN_IN), pl.BlockSpec((CHUNK * gch, GDN_QKV), lambda i: (jnp.maximum(i - 1, 0), 0)),
                  cst((4, GDN_QKV)), cst((1, w)), cst((1, w)), cst((1, w))] + [_ANY] * len(f_in),
        out_specs=[row(w)] * 3 + [pl.BlockSpec((w * gch, LANES), lambda i: (i, 0))] + [_ANY] * len(f_out),
        out_shape=[jax.ShapeDtypeStruct((s, w), F32)] * 3 + [jax.ShapeDtypeStruct((n_chunks * w, LANES), F32)] + list(f_out),
        scratch_shapes=[pltpu.VMEM((w, LANES), F32)] + list(f_scr),
        compiler_params=_params("arbitrary"),
    )(hx, hx, conv_w, a_log_e, dt_bias_e, norm_g_e, *f_in)
    return tuple(outs[:4]) + ((list(outs[4:]),) if fused is not None else ())


def gdn_bwd(hx, conv_w, a_log_e, dt_bias_e, norm_g_e, opre, tmat, states, dout, fused=None):
    s = hx.shape[0]
    n_chunks = s // CHUNK
    w = GDN_W
    n_slab = w // LANES
    gch = min(GDN_BWD_STEP_CHUNKS, n_chunks)
    n_blocks = n_chunks // gch
    rev = lambda i: n_blocks - 1 - i

    def body(hx_ref, hp_ref, cw_ref, al_ref, dt_ref, ng_ref, opre_ref, t_ref, st_ref, do_ref,
             dhx_ref, dab_ref, dcw_ref, dal_ref, ddt_ref, dng_ref, dstate, dy_next):
        i = pl.program_id(0)
        n = n_blocks - 1 - i

        @pl.when(i == 0)
        def _():
            dstate[...] = jnp.zeros_like(dstate)
            dy_next[...] = jnp.zeros_like(dy_next)
            for ref in (dcw_ref, dal_ref, ddt_ref, dng_ref):
                ref[...] = jnp.zeros_like(ref)

        hprev = hp_ref[...] * (n > 0).astype(F32)
        fr = _gdn_front(hx_ref, hprev, cw_ref, al_ref, dt_ref)
        tri = _slab_tri_masks()
        lower, strict = tri
        o = opre_ref[...]
        rinv = lax.rsqrt(_group_sum(o * o) * (1.0 / HEAD) + 1e-6)
        yn = o * rinv
        z = hx_ref[:, 3 * w:4 * w]
        sil, dsil_z = _silu_and_grad(z)
        dout_v = do_ref[...]
        ng = ng_ref[...]
        dng_ref[...] += jnp.sum(dout_v * yn * sil, axis=0, keepdims=True)
        dz = dout_v * yn * ng * dsil_z
        dyn = dout_v * ng * sil
        d_o = rinv * (dyn - yn * (_group_sum(dyn * yn) * (1.0 / HEAD)))
        last_row = (lax.broadcasted_iota(jnp.int32, (CHUNK, LANES), 0) == CHUNK - 1).astype(F32)
        bd = _block_diag_mask()
        gsum = _group_sum_slab
        t_all, st_all = t_ref[...], st_ref[...]
        dsns = [dstate[LANES * sl:LANES * (sl + 1), :] for sl in range(n_slab)]
        per_chunk = {}
        order = [(c_, s_) for c_ in reversed(range(gch)) for s_ in range(n_slab)]
        chain = {}
        for c, sl in order:
            lanes = slice(LANES * sl, LANES * (sl + 1))
            tok = slice(CHUNK * c, CHUNK * (c + 1))
            sq = _gdn_slab(fr, c, sl, tri)
            t = t_all[tok, lanes]
            st = st_all[w * c + LANES * sl:w * c + LANES * (sl + 1), :]
            dsn = dsns[sl]
            do_s = d_o[tok, lanes]
            u = _mm(t, _stack_heads(sq["v"] * sq["beta"]))
            wk = _mm(t, _stack_heads(sq["k"] * (sq["beta"] * sq["e"])))
            kt = sq["k"] * sq["xt"]
            dvnew = _unstack_heads(_mm_tn(sq["qkr"] * sq["dec"], do_s) * bd) + _mm(kt, dsn)
            dsns[sl] = _mm_tn(sq["q"] * sq["e"], do_s) * bd + sq["egl"] * dsn - _mm_tn(wk, dvnew) * bd
            chain[(c, sl)] = (sq, t, st, dsn, do_s, u, wk, kt, dvnew)
        for c, sl in order:
            sq, t, st, dsn, do_s, u, wk, kt, dvnew = chain[(c, sl)]
            k, q, v, beta, e, xt, dec, kk, qkr, kbd = (sq[n_] for n_ in ("k", "q", "v", "beta", "e", "xt", "dec", "kk", "qkr", "kbd"))
            vnew = u - _mm(wk, st)
            dqd = _mm_nt(do_s, st)
            dqk = _mm_nt(do_s, _stack_heads(vnew))
            dkt = _mm_nt(vnew, dsn)
            dgl = _head_total(dsn * st) * sq["egl"]
            dwk = -_mm_nt(dvnew, st)
            drv = _unstack_heads(_mm_tn(t, dvnew) * bd)
            drk = _unstack_heads(_mm_tn(t, dwk) * bd)
            dnm = jnp.where(strict, -(_mm_nt(drv, _stack_heads(u)) + _mm_nt(drk, _stack_heads(wk))), 0.0)
            dbeta = gsum(dnm * kk * dec)
            dkk = dnm * beta * dec
            ddec = dnm * beta * kk + dqk * qkr
            mq = dqk * dec
            dq = _mm(mq, kbd) + dqd * e
            dk = (_unstack_heads(_mm_tn(mq, q) * bd) + _mm(dkk, kbd) + _unstack_heads(_mm_tn(dkk, k) * bd)
                  + drk * (beta * e) + dkt * xt)
            rks = gsum(drk * k)
            dbeta = dbeta + gsum(drv * v) + rks * e
            de = rks * beta + gsum(dqd * q)
            dxt = gsum(dkt * k) * xt
            dgl = dgl + jnp.sum(dxt, axis=0, keepdims=True)
            dd = ddec * dec
            dgc = de * e - dxt + gsum(dd) - gsum(_head_transpose(dd)) + last_row * dgl
            per_chunk[(c, sl)] = dict(dq=dq * (HEAD ** -0.5), dk=dk, dv=drv * beta, dbeta=dbeta, dgc=dgc)
        for sl in range(n_slab):
            dstate[LANES * sl:LANES * (sl + 1), :] = dsns[sl]

        def block_of(name, suffix_sum=False):
            rows = []
            for c in range(gch):
                r = jnp.concatenate([per_chunk[(c, sl)][name] for sl in range(n_slab)], axis=1)
                rows.append(_rows_suffix_sum(r) if suffix_sum else r)
            return jnp.concatenate(rows, axis=0)

        dg = block_of("dgc", suffix_sum=True)
        dal_ref[...] += jnp.sum(dg * fr["g"], axis=0, keepdims=True)
        da = dg * fr["neg_a"] * _sigmoid(fr["sp_in"])
        ddt_ref[...] += jnp.sum(da, axis=0, keepdims=True)
        beta_all = fr["beta"]
        db = block_of("dbeta") * beta_all * (1.0 - beta_all)
        lane = lax.broadcasted_iota(jnp.int32, (CHUNK * gch, LANES), 1)
        dab = jnp.zeros((CHUNK * gch, LANES), F32)
        for hd in range(GDN_HEADS):
            dab = jnp.where(lane == hd, da[:, HEAD * hd:HEAD * hd + 1], dab)
            dab = jnp.where(lane == GDN_HEADS + hd, db[:, HEAD * hd:HEAD * hd + 1], dab)
        dab_ref[...] = dab.astype(BF16)
        dqn = block_of("dq")
        dkn = block_of("dk")
        dq_raw = fr["rq"] * (dqn - fr["qn"] * _group_sum(dqn * fr["qn"]))
        dk_raw = fr["rk"] * (dkn - fr["kn"] * _group_sum(dkn * fr["kn"]))
        dy = jnp.concatenate([dq_raw, dk_raw, block_of("dv")], axis=1) * fr["dsil"]
        nxt = dy_next[...]
        x = fr["x"]
        dx = cw_ref[3:4, :] * dy
        dcw_ref[3:4, :] += jnp.sum(dy * x, axis=0, keepdims=True)
        for j in (1, 2, 3):
            dx = dx + cw_ref[3 - j:4 - j, :] * _shift_rows_up(dy, nxt, j)
            dcw_ref[3 - j:4 - j, :] += jnp.sum(dy * _shift_rows(x, hprev, j), axis=0, keepdims=True)
        dy_next[...] = dy
        dhx_ref[:, 0:GDN_QKV] = dx.astype(BF16)
        dhx_ref[:, 3 * w:4 * w] = dz.astype(BF16)

    cst = lambda shape: pl.BlockSpec(shape, lambda i: (0, 0))
    row = lambda width: pl.BlockSpec((CHUNK * gch, width), lambda i: (rev(i), 0))
    buf = lambda width: pltpu.VMEM((CHUNK * gch, width), F32)
    f_in, f_out, f_scr, _ = fused if fused is not None else ([], [], [], None)
    outs = pl.pallas_call(
        _fuse(body, 10, 6, 2, n_blocks, fused), name="gdn_bwd" + ("_exchange" if fused is not None else ""), grid=(n_blocks,),
        in_specs=[row(GDN_IN), pl.BlockSpec((CHUNK * gch, GDN_QKV), lambda i: (jnp.maximum(rev(i) - 1, 0), 0)),
                  cst((4, GDN_QKV)), cst((1, w)), cst((1, w)), cst((1, w)), row(w), row(w),
                  pl.BlockSpec((w * gch, LANES), lambda i: (rev(i), 0)), row(w)] + [_ANY] * len(f_in),
        out_specs=[row(4 * w), row(LANES), cst((4, GDN_QKV)), cst((1, w)), cst((1, w)), cst((1, w))] + [_ANY] * len(f_out),
        out_shape=[jax.ShapeDtypeStruct((s, 4 * w), BF16), jax.ShapeDtypeStruct((s, LANES), BF16),
                   jax.ShapeDtypeStruct((4, GDN_QKV), F32),
                   jax.ShapeDtypeStruct((1, w), F32), jax.ShapeDtypeStruct((1, w), F32), jax.ShapeDtypeStruct((1, w), F32)]
        + list(f_out),
        scratch_shapes=[pltpu.VMEM((w, LANES), F32), buf(GDN_QKV)] + list(f_scr),
        compiler_params=_params("arbitrary"),
    )(hx, hx, conv_w, a_log_e, dt_bias_e, norm_g_e, opre, tmat, states, dout, *f_in)
    return tuple(outs[:6]) + ((list(outs[6:]),) if fused is not None else ())


_MESH = pl.DeviceIdType.MESH


def all_gather8(x, name):
    m, n = x.shape

    def body(x_ref, out_ref, send_sems, recv_sems, local_sem):
        px, py, pc = lax.axis_index("x"), lax.axis_index("y"), lax.axis_index("c")
        me, sibling = (px, py, pc), (px, py, 1 - pc)
        chips = [(1 - px, py), (px, 1 - py), (1 - px, 1 - py)]

        def slot(dx, dy, dc):
            return out_ref.at[4 * dx + 2 * dy + dc]

        def copy(k, block, to, src=None):
            return pltpu.make_async_remote_copy(
                src_ref=slot(*block) if src is None else src, dst_ref=slot(*block),
                send_sem=send_sems.at[k], recv_sem=recv_sems.at[k], device_id=to, device_id_type=_MESH)

        mine = pltpu.make_async_copy(x_ref, slot(*me), local_sem)
        mine.start()
        first = [copy(0, me, sibling, src=x_ref)]
        first += [copy(1 + j, me, (*chip, pc), src=x_ref) for j, chip in enumerate(chips)]
        for cp in first:
            cp.start()
        passed = [copy(4 + j, (*chip, pc), sibling) for j, chip in enumerate(chips)]
        for j, chip in enumerate(chips):
            copy(1 + j, (*chip, pc), me).wait_recv()
            passed[j].start()
        copy(0, sibling, me).wait_recv()
        for j, chip in enumerate(chips):
            copy(4 + j, (*chip, 1 - pc), me).wait_recv()
        for cp in first + passed:
            cp.wait_send()
        mine.wait()

    return pl.pallas_call(
        body, name=name, out_shape=jax.ShapeDtypeStruct((N_DEV, m, n), x.dtype),
        in_specs=[_ANY], out_specs=_ANY,
        scratch_shapes=[pltpu.SemaphoreType.DMA((7,)), pltpu.SemaphoreType.DMA((7,)), pltpu.SemaphoreType.DMA],
    )(x)


def _weight_gather_steps(s0, s1, f0, f1, sems):
    n = len(s0)
    own_send, own_recv, ici_send, ici_recv, fwd_send, fwd_recv = sems
    px, py, pc = lax.axis_index("x"), lax.axis_index("y"), lax.axis_index("c")
    mine = 2 * px + py
    sibling = (px, py, 1 - pc)
    chips = [(1 - px, py), (px, 1 - py), (1 - px, 1 - py)]

    def copy(src, dst, sems_s, sems_r, k, to):
        return pltpu.make_async_remote_copy(src_ref=src, dst_ref=dst, send_sem=sems_s.at[k], recv_sem=sems_r.at[k],
                                            device_id=to, device_id_type=_MESH)

    def first_copies(my_shards, my_full):
        own = [copy(shards[i], full[i].at[mine], own_send, own_recv, li * n + i, sibling)
               for li, (shards, full) in enumerate(((s0, f0), (s1, f1))) for i in range(n)]
        ici = [copy(my_shards[i], my_full[i].at[mine], ici_send, ici_recv, 3 * i + j, (cx, cy, pc))
               for i in range(n) for j, (cx, cy) in enumerate(chips)]
        return own + ici

    def begin(my_shards, my_full):
        for cp in first_copies(my_shards, my_full):
            cp.start()

    def end(my_shards, my_full, other_full):
        fwd = []
        for i in range(n):
            for j, (cx, cy) in enumerate(chips):
                slot = my_full[i].at[2 * cx + cy]
                copy(my_shards[i], slot, ici_send, ici_recv, 3 * i + j, (cx, cy, pc)).wait_recv()
                cp = copy(slot, slot, fwd_send, fwd_recv, 3 * i + j, sibling)
                cp.start()
                fwd.append(cp)
        for li, (shards, full) in enumerate(((s0, f0), (s1, f1))):
            for i in range(n):
                copy(shards[i], full[i].at[mine], own_send, own_recv, li * n + i, sibling).wait_recv()
        for i in range(n):
            for j, (cx, cy) in enumerate(chips):
                slot = other_full[i].at[2 * cx + cy]
                copy(slot, slot, fwd_send, fwd_recv, 3 * i + j, sibling).wait_recv()
        for cp in first_copies(my_shards, my_full) + fwd:
            cp.wait_send()

    def start():
        @pl.when(pc == 0)
        def _():
            begin(s0, f0)

        @pl.when(pc == 1)
        def _():
            begin(s1, f1)

    def finish():
        @pl.when(pc == 0)
        def _():
            end(s0, f0, f1)

        @pl.when(pc == 1)
        def _():
            end(s1, f1, f0)

    return start, finish


def _weight_gather_operands(shards0, shards1):
    n = len(shards0)
    full = [jax.ShapeDtypeStruct((N_CHIPS,) + v.shape, v.dtype) for v in list(shards0) + list(shards1)]
    dma = pltpu.SemaphoreType.DMA
    return full, [dma((2 * n,)), dma((2 * n,)), dma((3 * n,)), dma((3 * n,)), dma((3 * n,)), dma((3 * n,))]


def gather_layer_weights(shards0, shards1, name):
    n = len(shards0)

    def body(*refs):
        start, finish = _weight_gather_steps(refs[0:n], refs[n:2 * n], refs[2 * n:3 * n], refs[3 * n:4 * n], refs[4 * n:])
        start()
        finish()

    full, sems = _weight_gather_operands(shards0, shards1)
    outs = pl.pallas_call(
        body, name=name, out_shape=full, in_specs=[_ANY] * (2 * n), out_specs=[_ANY] * (2 * n), scratch_shapes=sems,
    )(*shards0, *shards1)
    return outs[0:n], outs[n:2 * n]


def _piece_offsets(pieces):
    offs = [0]
    for r, _ in pieces:
        offs.append(offs[-1] + r)
    return offs


def _chip_exchange_steps(srcs, q_ref, sems, pieces, owner):
    send_sems, recv_sems, local_sems = sems
    offs = _piece_offsets(pieces)
    px, py, pc = lax.axis_index("x"), lax.axis_index("y"), lax.axis_index("c")
    mine = 2 * px + py
    chips = [(1 - px, py), (px, 1 - py), (1 - px, 1 - py)]

    def copies():
        locals_, sends = [], []
        for i, (r, stride) in enumerate(pieces):
            dst = pl.ds(offs[i], r)
            locals_.append(pltpu.make_async_copy(srcs[i].at[pl.ds(mine * stride, r)], q_ref.at[mine, dst], local_sems.at[i]))
            for j, (cx, cy) in enumerate(chips):
                sends.append(pltpu.make_async_remote_copy(
                    src_ref=srcs[i].at[pl.ds((2 * cx + cy) * stride, r)], dst_ref=q_ref.at[mine, dst],
                    send_sem=send_sems.at[3 * i + j], recv_sem=recv_sems.at[3 * i + j], device_id=(cx, cy, pc),
                    device_id_type=_MESH))
        return locals_, sends

    def start():
        @pl.when(pc == owner)
        def _():
            locals_, sends = copies()
            for cp in locals_ + sends:
                cp.start()

    def finish():
        @pl.when(pc == owner)
        def _():
            for i, (r, stride) in enumerate(pieces):
                dst = pl.ds(offs[i], r)
                for j, (cx, cy) in enumerate(chips):
                    pltpu.make_async_remote_copy(
                        src_ref=srcs[i].at[pl.ds(mine * stride, r)], dst_ref=q_ref.at[2 * cx + cy, dst],
                        send_sem=send_sems.at[3 * i + j], recv_sem=recv_sems.at[3 * i + j], device_id=(cx, cy, pc),
                        device_id_type=_MESH).wait_recv()
            locals_, sends = copies()
            for cp in sends:
                cp.wait_send()
            for lc in locals_:
                lc.wait()

    return start, finish


def _chip_exchange_operands(arrays, pieces):
    n = len(pieces)
    dma = pltpu.SemaphoreType.DMA
    q = jax.ShapeDtypeStruct((N_CHIPS, _piece_offsets(pieces)[-1], arrays[0].shape[1]), arrays[0].dtype)
    return q, [dma((3 * n,)), dma((3 * n,)), dma((n,))]


def chip_exchange(arrays, pieces, owner, name):
    n = len(pieces)

    def body(*refs):
        start, finish = _chip_exchange_steps(refs[0:n], refs[n], refs[n + 1:], pieces, owner)
        start()
        finish()

    q, sems = _chip_exchange_operands(arrays, pieces)
    return pl.pallas_call(body, name=name, out_shape=q, in_specs=[_ANY] * n, out_specs=_ANY, scratch_shapes=sems)(*arrays)


def sibling_send(arrays, to_core, name):
    n = len(arrays)

    def body(*refs):
        srcs, outs = refs[0:n], refs[n:2 * n]
        send_sems, recv_sems = refs[2 * n:]
        px, py, pc = lax.axis_index("x"), lax.axis_index("y"), lax.axis_index("c")
        cps = [pltpu.make_async_remote_copy(
            src_ref=srcs[i], dst_ref=outs[i], send_sem=send_sems.at[i], recv_sem=recv_sems.at[i],
            device_id=(px, py, to_core), device_id_type=_MESH) for i in range(n)]

        @pl.when(pc != to_core)
        def _():
            for cp in cps:
                cp.start()
            for cp in cps:
                cp.wait_send()

        @pl.when(pc == to_core)
        def _():
            for cp in cps:
                cp.wait_recv()

    return pl.pallas_call(
        body, name=name, out_shape=[jax.ShapeDtypeStruct(v.shape, v.dtype) for v in arrays],
        in_specs=[_ANY] * n, out_specs=[_ANY] * n,
        scratch_shapes=[pltpu.SemaphoreType.DMA((n,)), pltpu.SemaphoreType.DMA((n,))],
    )(*arrays)


def _fuse(body, n_in, n_out, n_scratch, n_steps, fused):
    if fused is None:
        return body
    f_in, f_out, f_scr, steps = fused
    a, b, c = len(f_in), len(f_out), len(f_scr)

    def wrapped(*refs):
        ins, rest = refs[:n_in + a], refs[n_in + a:]
        outs, scr = rest[:n_out + b], rest[n_out + b:]
        start, finish = steps(ins[n_in:], outs[n_out:], scr[n_scratch:])
        step = pl.program_id(0)

        @pl.when(step == 0)
        def _():
            start()

        body(*ins[:n_in], *outs[:n_out], *scr[:n_scratch])

        @pl.when(step == n_steps - 1)
        def _():
            finish()

    return wrapped


def sibling_swap(x, name):
    def body(x_ref, out_ref, send_sem, recv_sem):
        px, py, pc = lax.axis_index("x"), lax.axis_index("y"), lax.axis_index("c")
        cp = pltpu.make_async_remote_copy(
            src_ref=x_ref, dst_ref=out_ref, send_sem=send_sem, recv_sem=recv_sem,
            device_id=(px, py, 1 - pc), device_id_type=_MESH)
        cp.start()
        cp.wait()

    return pl.pallas_call(
        body, name=name, out_shape=jax.ShapeDtypeStruct(x.shape, x.dtype), in_specs=[_ANY], out_specs=_ANY,
        scratch_shapes=[pltpu.SemaphoreType.DMA, pltpu.SemaphoreType.DMA],
    )(x)


ELT_TILE = 128


def _elt_rows(m):
    for t in (512, 256, ELT_TILE, 16, 8):
        if m % t == 0:
            return t
    return m


def pair_add(a, b, name):
    m, n = b.shape
    tm = _elt_rows(m)

    def body(a_ref, b_ref, o_ref):
        o_ref[...] = (a_ref[...].astype(F32) + b_ref[...].astype(F32)).astype(o_ref.dtype)

    return pl.pallas_call(
        body, name=name, grid=(m // tm,), in_specs=[_row_spec(tm, n)] * 2, out_specs=_row_spec(tm, n),
        out_shape=jax.ShapeDtypeStruct((m, n), b.dtype), compiler_params=_params("arbitrary"),
    )(a, b)


def sum_leading(q, name):
    kk, m, n = q.shape
    tm = _elt_rows(m)

    def body(q_ref, o_ref):
        acc = q_ref[0].astype(F32)
        for i in range(1, kk):
            acc = acc + q_ref[i].astype(F32)
        o_ref[...] = acc

    return pl.pallas_call(
        body, name=name, grid=(m // tm,), in_specs=[pl.BlockSpec((kk, tm, n), lambda i: (0, i, 0))],
        out_specs=_row_spec(tm, n), out_shape=jax.ShapeDtypeStruct((m, n), F32), compiler_params=_params("arbitrary"),
    )(q)


def adamw(w, g, m, v, name):
    rows, cols = w.shape
    tm = _elt_rows(rows)

    def body(w_ref, g_ref, m_ref, v_ref, d_ref, nm_ref, nv_ref):
        gv = g_ref[...]
        nm = ADAM_B1 * m_ref[...] + (1.0 - ADAM_B1) * gv
        nv = ADAM_B2 * v_ref[...] + (1.0 - ADAM_B2) * jnp.square(gv)
        nm_ref[...] = nm
        nv_ref[...] = nv
        m_hat = nm / (1.0 - ADAM_B1 ** ADAM_STEP)
        v_hat = nv / (1.0 - ADAM_B2 ** ADAM_STEP)
        d_ref[...] = -ADAM_LR * (m_hat / (jnp.sqrt(v_hat) + ADAM_EPS) + ADAM_WD * w_ref[...])

    spec = _row_spec(tm, cols)
    return pl.pallas_call(
        body, name=name, grid=(rows // tm,), in_specs=[spec] * 4, out_specs=[spec] * 3,
        out_shape=[jax.ShapeDtypeStruct((rows, cols), F32)] * 3, compiler_params=_params("arbitrary"),
    )(w, g, m, v)


def _block_diag_dense(w):
    g = w.shape[0]
    return jnp.einsum("gij,gh->gihj", w, jnp.eye(g, dtype=w.dtype)).reshape(g * w.shape[1], g * w.shape[2])


def _diag_blocks(m):
    return jnp.stack([m[HEAD * i:HEAD * (i + 1), HEAD * i:HEAD * (i + 1)] for i in range(LRU_BLOCKS)])


def _rep(v):
    return jnp.repeat(v, HEAD, axis=-1)


def _split_w_in(w_in):
    gdn0 = RET_IN + LRU_IN
    gdn1 = gdn0 + 4 * GDN_W
    w_r = w_in[:, 0:RET_IN]
    w_l = w_in[:, RET_IN:gdn0]
    w_g = jnp.concatenate([w_in[:, gdn0:gdn1], _rep(w_in[:, gdn1:gdn1 + GDN_HEADS]), _rep(w_in[:, gdn1 + GDN_HEADS:])], axis=1)
    w_ab = jnp.pad(w_in[:, gdn1:], ((0, 0), (0, LANES - 2 * GDN_HEADS)))
    return w_r, w_l, w_g, w_ab


WIN_SHARD = D_IN // N_CHIPS
WIN_STRIDE = 832
WIN_ROWS = 960
WIN_T_ROWS = WIN_STRIDE * (N_CHIPS - 1) + WIN_ROWS
AB_ROWS = 16

_GRAD_PIECES = (("ffn1_w_gate", 704, 704), ("ffn1_w_up", 704, 704), ("ffn1_w_down", 704, 704), ("w_in", WIN_ROWS, WIN_STRIDE),
                ("w_out", 256, 256), ("ffn2_w_gate", 704, 704), ("ffn2_w_up", 704, 704), ("ffn2_w_down", 704, 704),
                ("ple_w_gate", 256, 256), ("ple_w_proj", 64, 64))
_TRANSPOSED = ("ffn1_w_gate", "ffn1_w_up", "w_in", "ffn2_w_gate", "ffn2_w_up", "ple_w_proj")


def _local_step(x, p, pos, target, wt, mesh=None):
    row = lambda v: v[None, :]
    saved = []
    xb = x.astype(BF16)
    pieces = [(r, stride) for _, r, stride in _GRAD_PIECES]
    grad_names = [n for n, _, _ in _GRAD_PIECES]
    for i in range(DEPTH):
        w_r, w_l, w_g, w_ab = _split_w_in(wt["w_in"][i])
        lw = dict(
            wg1=wt["ffn1_w_gate"][i], wu1=wt["ffn1_w_up"][i], wd1=wt["ffn1_w_down"][i], w_r=w_r, w_l=w_l, w_g=w_g, w_ab=w_ab,
            w_out=wt["w_out"][i], wg2=wt["ffn2_w_gate"][i], wu2=wt["ffn2_w_up"][i], wd2=wt["ffn2_w_down"][i],
            wpg=wt["ple_w_gate"][i], wpp=wt["ple_w_proj"][i],
            wa=_block_diag_dense(wt["lru_w_a"][i]), wx=_block_diag_dense(wt["lru_w_x"][i]),
            al=row(_rep(wt["gdn_a_log"][i])), dt=row(_rep(wt["gdn_dt_bias"][i])), ng=row(jnp.tile(wt["gdn_norm_g"][i], GDN_HEADS)))
        hg1, hu1, r1, x1, x1b = ffn_fwd(x, row(wt["ln_ffn1_g"][i]), row(wt["ln_ffn1_b"][i]), lw["wg1"], lw["wu1"], lw["wd1"])
        hr, hl, hgd = win_fwd(x1, w_r, w_l, w_g)
        o_r, opre_r, st_r = ret_fwd(hr, pos, row(wt["ret_norm_g"][i]))
        o_l, xc, hs = lru_fwd(hl, wt["lru_conv_w"][i], row(wt["lru_conv_b"][i]), lw["wa"], row(wt["lru_b_a"][i]), lw["wx"],
                              row(wt["lru_b_x"][i]), row(wt["lru_lambda"][i]))
        if mesh is not None and i == 0:
            half0, half1 = mesh[0]
            full_shapes, sems = _weight_gather_operands(half0, half1)
            n_half = len(half0)
            steps = lambda ins, outs, scr: _weight_gather_steps(ins[:n_half], ins[n_half:], outs[:n_half], outs[n_half:], scr)
            o_g, opre_g, tmat, st_g, gathered = gdn_fwd(hgd, wt["gdn_conv_w"][i], lw["al"], lw["dt"], lw["ng"],
                                                        fused=(list(half0) + list(half1), full_shapes, sems, steps))
            wt = {**wt, **{n: [wt[n][0], w1] for n, w1 in mesh[1](gathered).items()}}
        else:
            o_g, opre_g, tmat, st_g = gdn_fwd(hgd, wt["gdn_conv_w"][i], lw["al"], lw["dt"], lw["ng"])
        r2, x2, x2b, ocat = out_fwd(o_r, o_l, o_g, x1, lw["w_out"], row(wt["ln_mix_g"][i]), row(wt["ln_mix_b"][i]))
        hg2, hu2, r3, x3, x3b, pg, pp = ffn_fwd(x2, row(wt["ln_ffn2_g"][i]), row(wt["ln_ffn2_b"][i]), lw["wg2"], lw["wu2"],
                                                lw["wd2"], ple=(p[i], lw["wpg"], lw["wpp"]))
        saved.append(dict(lw=lw, x0=xb, hg1=hg1, hu1=hu1, r1=r1, x1=x1b, hr=hr, hl=hl, hgd=hgd, ocat=ocat, opre_r=opre_r,
                          st_r=st_r, xc=xc, hs=hs, opre_g=opre_g, tmat=tmat, st_g=st_g, r2=r2, x2=x2b, hg2=hg2, hu2=hu2,
                          r3=r3, pg=pg, pp=pp))
        x, xb = x3, x3b

    dx, loss = loss_and_grad(x, target)
    grads = [None] * DEPTH
    big = [None] * DEPTH
    pair_sums = [None] * DEPTH
    for i in reversed(range(DEPTH)):
        sv = saved[i]
        lw = sv["lw"]
        tag = f"_l{i}"
        dx2, act2, dhg2, dhu2, dy2, dg3, db3, dpg, dpp = ffn_bwd(
            dx, sv["r3"], sv["x2"], sv["hg2"], sv["hu2"], row(wt["ln_ffn2_g"][i]), lw["wg2"], lw["wu2"], lw["wd2"],
            ple=(sv["pg"], sv["pp"], lw["wpg"]))
        g, bg = {}, {}
        bg["ffn2_w_gate"] = wgrad(dhg2, sv["x2"], "wgrad_gate2" + tag)
        bg["ffn2_w_up"] = wgrad(dhu2, sv["x2"], "wgrad_up2" + tag)
        bg["ffn2_w_down"] = wgrad(act2, dy2, "wgrad_down2" + tag)
        bg["ple_w_gate"] = wgrad(sv["x2"], dpg, "wgrad_pgate" + tag)
        bg["ple_w_proj"] = wgrad(dpp, p[i], "wgrad_pproj" + tag).reshape(PLE_DIM, D_MODEL)
        g["ln_ffn2_g"], g["ln_ffn2_b"] = dg3[0], db3[0]
        dr2, dr2b, do_r, do_l, do_g, dg2, db2 = out_bwd(dx2, sv["r2"], row(wt["ln_mix_g"][i]), lw["w_out"])
        g["ln_mix_g"], g["ln_mix_b"] = dg2[0], db2[0]
        bg["w_out"] = wgrad(sv["ocat"], dr2b, "wgrad_out" + tag)
        dhr, dgn = ret_bwd(sv["hr"], pos, row(wt["ret_norm_g"][i]), sv["opre_r"], sv["st_r"], do_r)
        g["ret_norm_g"] = dgn[0]
        dhl, dcw, dcb, dwa, dba, dwx, dbx, dlam = lru_bwd(
            sv["hl"], wt["lru_conv_w"][i], row(wt["lru_conv_b"][i]), lw["wa"], row(wt["lru_b_a"][i]), lw["wx"],
            row(wt["lru_b_x"][i]), row(wt["lru_lambda"][i]), sv["xc"], sv["hs"], do_l)
        g["lru_conv_w"], g["lru_conv_b"] = dcw, dcb[0]
        g["lru_w_a"], g["lru_b_a"], g["lru_w_x"], g["lru_b_x"], g["lru_lambda"] = _diag_blocks(dwa), dba[0], _diag_blocks(dwx), dbx[0], dlam[0]
        if mesh is not None and i == 0:
            q_shape, sems = _chip_exchange_operands(pair_sums[1], pieces)
            steps = lambda ins, outs, scr: _chip_exchange_steps(ins, outs[0], scr, pieces, 1)
            dhq, dab, dgcw, dal, ddt, dng, arrived = gdn_bwd(
                sv["hgd"], wt["gdn_conv_w"][i], lw["al"], lw["dt"], lw["ng"], sv["opre_g"], sv["tmat"], sv["st_g"], do_g,
                fused=(pair_sums[1], [q_shape], sems, steps))
            big[1] = arrived[0]
        else:
            dhq, dab, dgcw, dal, ddt, dng = gdn_bwd(sv["hgd"], wt["gdn_conv_w"][i], lw["al"], lw["dt"], lw["ng"], sv["opre_g"],
                                                    sv["tmat"], sv["st_g"], do_g)
        g["gdn_conv_w"] = dgcw
        g["gdn_a_log"], g["gdn_dt_bias"] = dal[0, ::HEAD], ddt[0, ::HEAD]
        g["gdn_norm_g"] = dng[0].reshape(GDN_HEADS, HEAD).sum(0)
        dx1 = win_bwd(dr2, dhr, dhl, dhq, dab, lw["w_r"], lw["w_l"], lw["w_g"], lw["w_ab"])
        used = RET_IN + LRU_IN + 4 * GDN_W + AB_ROWS
        bg["w_in"] = jnp.concatenate(
            [wgrad(dhr, sv["x1"], "wgrad_in_r" + tag), wgrad(dhl, sv["x1"], "wgrad_in_l" + tag),
             wgrad(dhq, sv["x1"], "wgrad_in_q" + tag), wgrad(dab, sv["x1"], "wgrad_in_ab" + tag)[0:AB_ROWS],
             jnp.zeros((WIN_T_ROWS - used, D_MODEL), BF16)], axis=0)
        dx, act1, dhg1, dhu1, dy1, dg1, db1 = ffn_bwd(dx1, sv["r1"], sv["x0"], sv["hg1"], sv["hu1"], row(wt["ln_ffn1_g"][i]),
                                                      lw["wg1"], lw["wu1"], lw["wd1"])
        bg["ffn1_w_gate"] = wgrad(dhg1, sv["x0"], "wgrad_gate1" + tag)
        bg["ffn1_w_up"] = wgrad(dhu1, sv["x0"], "wgrad_up1" + tag)
        bg["ffn1_w_down"] = wgrad(act1, dy1, "wgrad_down1" + tag)
        g["ln_ffn1_g"], g["ln_ffn1_b"] = dg1[0], db1[0]
        grads[i] = g
        if mesh is None:
            big[i] = bg
        else:
            mine = [bg[n] for n in grad_names]
            theirs = sibling_send(mine, i, f"reduce_pair_send_l{i}")
            pair_sums[i] = [pair_add(u, v, f"reduce_pair_add_l{i}_" + n) for n, u, v in zip(grad_names, mine, theirs)]
            if i == 0:
                big[0] = chip_exchange(pair_sums[0], pieces, 0, "reduce_chip_exchange_l0")
    return loss, dx, {k: jnp.stack([grads[i][k] for i in range(DEPTH)]) for k in grads[0]}, big


def _natural_grad(name, rows):
    if name == "ple_w_proj":
        return rows.reshape(-1, PLE_DIM).T
    return rows.T if name in _TRANSPOSED else rows


_SPLIT = dict(ffn1_w_gate=2, ffn1_w_up=2, ffn1_w_down=1, w_in=2, w_out=1, ffn2_w_gate=2, ffn2_w_up=2, ffn2_w_down=1,
              ple_w_gate=1, ple_w_proj=2)
_CONV = ("lru_conv_w", "gdn_conv_w")
_WHOLE = ("ln_ffn1_g", "ln_ffn1_b", "ret_norm_g", "lru_conv_b", "lru_w_a", "lru_b_a", "lru_w_x", "lru_b_x", "lru_lambda",
          "gdn_a_log", "gdn_dt_bias", "gdn_norm_g", "ln_mix_g", "ln_mix_b", "ln_ffn2_g", "ln_ffn2_b")
_WEIGHTS = ("ln_ffn1_g", "ln_ffn1_b", "ffn1_w_gate", "ffn1_w_up", "ffn1_w_down", "w_in", "ret_norm_g", "lru_conv_w", "lru_conv_b",
            "lru_w_a", "lru_b_a", "lru_w_x", "lru_b_x", "lru_lambda", "gdn_conv_w", "gdn_a_log", "gdn_dt_bias", "gdn_norm_g",
            "w_out", "ln_mix_g", "ln_mix_b", "ffn2_w_gate", "ffn2_w_up", "ffn2_w_down", "ple_w_gate", "ple_w_proj",
            "ln_ffn2_g", "ln_ffn2_b")
_INPUTS = ("x", "p", "positions") + _WEIGHTS + ("loss_target",) + tuple("m_" + n for n in _WEIGHTS) + tuple("v_" + n for n in _WEIGHTS)

BIG_COLS = 1024
SMALL_COLS = LANES
SMALL_ROWS_MULT = 8


def _pack(arrays, dtype, cols, rows_mult):
    flat = jnp.concatenate([a.reshape(-1).astype(dtype) for a in arrays])
    rows = -(-flat.shape[0] // cols)
    rows = -(-rows // rows_mult) * rows_mult
    return jnp.pad(flat, (0, rows * cols - flat.shape[0])).reshape(rows, cols)


def _unpack(packed, shapes):
    flat = packed.reshape(-1)
    out, off = [], 0
    for shp in shapes:
        size = int(np.prod(shp))
        out.append(flat[off:off + size].reshape(shp))
        off += size
    return out


def _as2d(a):
    return a.reshape(-1, a.shape[-1])


def kernel(x, p, positions, ln_ffn1_g, ln_ffn1_b, ffn1_w_gate, ffn1_w_up, ffn1_w_down, w_in, ret_norm_g, lru_conv_w, lru_conv_b, lru_w_a, lru_b_a, lru_w_x, lru_b_x, lru_lambda, gdn_conv_w, gdn_a_log, gdn_dt_bias, gdn_norm_g, w_out, ln_mix_g, ln_mix_b, ffn2_w_gate, ffn2_w_up, ffn2_w_down, ple_w_gate, ple_w_proj, ln_ffn2_g, ln_ffn2_b, loss_target, m_ln_ffn1_g, m_ln_ffn1_b, m_ffn1_w_gate, m_ffn1_w_up, m_ffn1_w_down, m_w_in, m_ret_norm_g, m_lru_conv_w, m_lru_conv_b, m_lru_w_a, m_lru_b_a, m_lru_w_x, m_lru_b_x, m_lru_lambda, m_gdn_conv_w, m_gdn_a_log, m_gdn_dt_bias, m_gdn_norm_g, m_w_out, m_ln_mix_g, m_ln_mix_b, m_ffn2_w_gate, m_ffn2_w_up, m_ffn2_w_down, m_ple_w_gate, m_ple_w_proj, m_ln_ffn2_g, m_ln_ffn2_b, v_ln_ffn1_g, v_ln_ffn1_b, v_ffn1_w_gate, v_ffn1_w_up, v_ffn1_w_down, v_w_in, v_ret_norm_g, v_lru_conv_w, v_lru_conv_b, v_lru_w_a, v_lru_b_a, v_lru_w_x, v_lru_b_x, v_lru_lambda, v_gdn_conv_w, v_gdn_a_log, v_gdn_dt_bias, v_gdn_norm_g, v_w_out, v_ln_mix_g, v_ln_mix_b, v_ffn2_w_gate, v_ffn2_w_up, v_ffn2_w_down, v_ple_w_gate, v_ple_w_proj, v_ln_ffn2_g, v_ln_ffn2_b):
    a = dict(zip(_INPUTS, (x, p, positions, ln_ffn1_g, ln_ffn1_b, ffn1_w_gate, ffn1_w_up, ffn1_w_down, w_in, ret_norm_g, lru_conv_w, lru_conv_b, lru_w_a, lru_b_a, lru_w_x, lru_b_x, lru_lambda, gdn_conv_w, gdn_a_log, gdn_dt_bias, gdn_norm_g, w_out, ln_mix_g, ln_mix_b, ffn2_w_gate, ffn2_w_up, ffn2_w_down, ple_w_gate, ple_w_proj, ln_ffn2_g, ln_ffn2_b, loss_target, m_ln_ffn1_g, m_ln_ffn1_b, m_ffn1_w_gate, m_ffn1_w_up, m_ffn1_w_down, m_w_in, m_ret_norm_g, m_lru_conv_w, m_lru_conv_b, m_lru_w_a, m_lru_b_a, m_lru_w_x, m_lru_b_x, m_lru_lambda, m_gdn_conv_w, m_gdn_a_log, m_gdn_dt_bias, m_gdn_norm_g, m_w_out, m_ln_mix_g, m_ln_mix_b, m_ffn2_w_gate, m_ffn2_w_up, m_ffn2_w_down, m_ple_w_gate, m_ple_w_proj, m_ln_ffn2_g, m_ln_ffn2_b, v_ln_ffn1_g, v_ln_ffn1_b, v_ffn1_w_gate, v_ffn1_w_up, v_ffn1_w_down, v_w_in, v_ret_norm_g, v_lru_conv_w, v_lru_conv_b, v_lru_w_a, v_lru_b_a, v_lru_w_x, v_lru_b_x, v_lru_lambda, v_gdn_conv_w, v_gdn_a_log, v_gdn_dt_bias, v_gdn_norm_g, v_w_out, v_ln_mix_g, v_ln_mix_b, v_ffn2_w_gate, v_ffn2_w_up, v_ffn2_w_down, v_ple_w_gate, v_ple_w_proj, v_ln_ffn2_g, v_ln_ffn2_b)))
    assert len(a) == len(_INPUTS)
    core = lax.axis_index("c")
    chip = 2 * lax.axis_index("x") + lax.axis_index("y")
    big = list(_SPLIT)

    n_half = len(big) // 2

    def layer_shards(layer):
        s = [a[n][layer].astype(BF16) for n in big]
        return s[:n_half], s[n_half:]

    def assemble(gathered):
        return {n: jnp.concatenate([gathered[i][k] for k in range(N_CHIPS)], axis=_SPLIT[n] - 1) for i, n in enumerate(big)}

    first = gather_layer_weights(*layer_shards(0), "gather_weights_l0")
    wt = {n: [w0, None] for n, w0 in assemble(list(first[0]) + list(first[1])).items()}
    conv_g = all_gather8(_pack([a[n] for n in _CONV], F32, SMALL_COLS, SMALL_ROWS_MULT), "gather_conv_weights")[0::2]
    conv_g = conv_g.reshape(N_CHIPS, -1)
    off = 0
    for n in _CONV:
        shp = a[n].shape
        size = int(np.prod(shp))
        parts = conv_g[:, off:off + size].reshape((N_CHIPS,) + shp)
        wt[n] = jnp.concatenate([parts[k] for k in range(N_CHIPS)], axis=2)
        off += size
    for n in _WHOLE:
        wt[n] = a[n]

    seq = a["x"].shape[1]
    loss_part, dx, grads, arrived = _local_step(a["x"][0], a["p"][:, 0], a["positions"].reshape(seq, 1), a["loss_target"][0],
                                                wt, mesh=(layer_shards(1), assemble))
    loss = lax.psum(loss_part[0, 0], ("x", "y", "c"))

    sums = [sum_leading(arrived[layer], f"reduce_chip_sum_l{layer}") for layer in range(DEPTH)]
    my_layer_sum = jnp.where(core == 0, sums[0], sums[1])
    other_layer_sum = sibling_swap(my_layer_sum, "reduce_pair_share")
    reduced = [jnp.where(core == layer, my_layer_sum, other_layer_sum) for layer in range(DEPTH)]
    big_grads = {}
    off = 0
    for n, r, _ in _GRAD_PIECES:
        per_layer = []
        for layer in range(DEPTH):
            rows = reduced[layer][off:off + r]
            if n == "w_in":
                rows = lax.dynamic_slice_in_dim(rows, chip * (WIN_SHARD - WIN_STRIDE), WIN_SHARD, axis=0)
            per_layer.append(_natural_grad(n, rows))
        big_grads[n] = jnp.stack(per_layer)
        off += r

    small_names = list(_WHOLE) + list(_CONV)
    small_local = _pack([grads[n] for n in small_names], F32, SMALL_COLS, SMALL_ROWS_MULT)
    small_sum = sum_leading(all_gather8(small_local, "gather_small_grads"), "sum_small_grads")
    small_grads = dict(zip(small_names, _unpack(small_sum, [grads[n].shape for n in small_names])))
    for n in _CONV:
        width = a[n].shape[2]
        small_grads[n] = lax.dynamic_slice_in_dim(small_grads[n], chip * width, width, axis=2)

    new = {}
    for n in big:
        d, nm, nv = adamw(_as2d(a[n]), _as2d(big_grads[n]), _as2d(a["m_" + n]), _as2d(a["v_" + n]), "adamw_" + n)
        new[n] = tuple(t.reshape(a[n].shape) for t in (d, nm, nv))
    pk = lambda prefix: _pack([a[prefix + n] for n in small_names], F32, SMALL_COLS, SMALL_ROWS_MULT)
    pg = _pack([small_grads[n] for n in small_names], F32, SMALL_COLS, SMALL_ROWS_MULT)
    outs = adamw(pk(""), pg, pk("m_"), pk("v_"), "adamw_small")
    shapes = [a[n].shape for n in small_names]
    for n, d, nm, nv in zip(small_names, *[_unpack(o, shapes) for o in outs]):
        new[n] = (d, nm, nv)
    all_grads = {**big_grads, **small_grads}
    return (loss, dx[None], *[all_grads[n] for n in _WEIGHTS], *[new[n][0] for n in _WEIGHTS],
            *[new[n][1] for n in _WEIGHTS], *[new[n][2] for n in _WEIGHTS])
```

```python
import functools
import math

import numpy as np
import jax
import jax.numpy as jnp
from jax import lax
from jax.experimental import pallas as pl
from jax.experimental.pallas import tpu as pltpu

F32 = jnp.float32
BF16 = jnp.bfloat16

D_MODEL = 1024
D_FF = 2816
PLE_DIM = 256
DEPTH = 2
CHUNK = 64
RET_HEADS = 4
RET_W = 256
LRU_W = 384
LRU_BLOCKS = 6
GDN_HEADS = 6
GDN_W = 384
HEAD = 64
D_IN = 3340
RET_IN = 4 * RET_W
LRU_IN = 2 * LRU_W
GDN_IN = 6 * GDN_W
ROPE_THETA = 10000.0
ALPHA = (2 * DEPTH) ** 0.25
LN_EPS = 1e-5
LRU_C = 8.0
N_CHIPS = 4
N_DEV = 8

ADAM_LR = 0.001
ADAM_B1 = 0.9
ADAM_B2 = 0.999
ADAM_EPS = 1e-08
ADAM_WD = 0.01
ADAM_STEP = 10

LANES = 128
VMEM_LIMIT = 56 * 1024 * 1024
ROW_TILE = 256
ROW_TILE_BWD = 512
SCAN_TILE = 256


def _params(*sem):
    return pltpu.CompilerParams(dimension_semantics=sem, vmem_limit_bytes=VMEM_LIMIT)


def _mm(a, b):
    return jnp.dot(a, b, preferred_element_type=F32)


def _mm_nt(a, b):
    return lax.dot_general(a, b, (((1,), (1,)), ((), ())), preferred_element_type=F32)


def _mm_tn(a, b):
    return lax.dot_general(a, b, (((0,), (0,)), ((), ())), preferred_element_type=F32)


def _split(a):
    hi = a.astype(BF16)
    lo = (a - hi.astype(F32)).astype(BF16)
    return hi, lo


def _mm3(a, b):
    ah, al = _split(a)
    bh, bl = _split(b)
    return _mm(ah, bh) + (_mm(ah, bl) + _mm(al, bh))


def _sigmoid(x):
    return jax.nn.sigmoid(x)


def _log1p(u):
    w = 1.0 + u
    return jnp.where(w == 1.0, u, jnp.log(w) * (u / jnp.where(w == 1.0, 1.0, w - 1.0)))


def _expm1(y):
    u = jnp.exp(y)
    um1 = u - 1.0
    safe = jnp.where((u == 1.0) | (um1 == -1.0), 1.0, jnp.log(jnp.where(u == 0.0, 1.0, u)))
    return jnp.where(u == 1.0, y, jnp.where(um1 == -1.0, -1.0, um1 * (y / safe)))


def _softplus(x):
    return jnp.maximum(x, 0.0) + _log1p(jnp.exp(-jnp.abs(x)))


_GELU_C = math.sqrt(2.0 / math.pi)


def _gelu(x):
    return 0.5 * x * (1.0 + jnp.tanh(_GELU_C * (x + 0.044715 * (x * x * x))))


def _gelu_grad(x):
    t = jnp.tanh(_GELU_C * (x + 0.044715 * (x * x * x)))
    return 0.5 * (1.0 + t) + 0.5 * x * (1.0 - t * t) * (_GELU_C * (1.0 + 3.0 * 0.044715 * (x * x)))


def _silu_and_grad(x):
    s = _sigmoid(x)
    return x * s, s * (1.0 + x * (1.0 - s))


def _group_sum_slab(x):
    lane = lax.broadcasted_iota(jnp.int32, x.shape, 1)
    low = jnp.sum(x[:, 0:LANES // 2], axis=1, keepdims=True)
    high = jnp.sum(x[:, LANES // 2:], axis=1, keepdims=True)
    return jnp.where(lane < LANES // 2, low, high)


def _group_sum(x):
    n = x.shape[1] // LANES
    if n == 1:
        return _group_sum_slab(x)
    return jnp.concatenate([_group_sum_slab(x[:, LANES * i:LANES * (i + 1)]) for i in range(n)], axis=1)


def _rows_prefix_sum(x):
    n = x.shape[0]
    row = lax.broadcasted_iota(jnp.int32, x.shape, 0)
    d = 1
    while d < n:
        x = x + jnp.where(row >= d, pltpu.roll(x, d, 0), 0.0)
        d *= 2
    return x


def _rows_suffix_sum(x):
    n = x.shape[0]
    row = lax.broadcasted_iota(jnp.int32, x.shape, 0)
    d = 1
    while d < n:
        x = x + jnp.where(row < n - d, pltpu.roll(x, n - d, 0), 0.0)
        d *= 2
    return x


def _shift_rows(cur, prev, j):
    row = lax.broadcasted_iota(jnp.int32, cur.shape, 0)
    return jnp.where(row < j, pltpu.roll(prev, j, 0), pltpu.roll(cur, j, 0))


def _shift_rows_up(cur, nxt, j):
    n = cur.shape[0]
    row = lax.broadcasted_iota(jnp.int32, cur.shape, 0)
    return jnp.where(row < n - j, pltpu.roll(cur, n - j, 0), pltpu.roll(nxt, n - j, 0))


def _layer_norm_stats(r):
    mu = jnp.mean(r, axis=-1, keepdims=True)
    d = r - mu
    var = jnp.mean(d * d, axis=-1, keepdims=True)
    rstd = lax.rsqrt(var + LN_EPS)
    return d * rstd, rstd


def _load_resident(step, pairs, sems):
    @pl.when(step == 0)
    def _():
        cps = [pltpu.make_async_copy(h, v, sems.at[i]) for i, (h, v) in enumerate(pairs)]
        for c in cps:
            c.start()
        for c in cps:
            c.wait()


def _row_spec(tile, width):
    return pl.BlockSpec((tile, width), lambda i: (i, 0))


def _full_spec(shape):
    nd = len(shape)
    return pl.BlockSpec(shape, lambda i: (0,) * nd)


_ANY = pl.BlockSpec(memory_space=pl.ANY)


def ffn_fwd(x, ln_g, ln_b, w_gate, w_up, w_down, ple=None, fused=None):
    s = x.shape[0]
    tm = ROW_TILE
    with_ple = ple is not None
    weights = [w_gate, w_up, w_down] + ([ple[1], ple[2]] if with_ple else [])

    def body(*refs):
        it = iter(refs)
        x_ref, g_ref, b_ref = next(it), next(it), next(it)
        p_ref = next(it) if with_ple else None
        w_hbm = [next(it) for _ in weights]
        hg_ref, hu_ref, r_ref, xn_ref, xnb_ref = next(it), next(it), next(it), next(it), next(it)
        pg_ref, pp_ref = (next(it), next(it)) if with_ple else (None, None)
        w_vm = [next(it) for _ in weights]
        sems = next(it)
        _load_resident(pl.program_id(0), list(zip(w_hbm, w_vm)), sems)
        xv = x_ref[...]
        xb = xv.astype(BF16)
        hg = _mm(xb, w_vm[0][...])
        hu = _mm(xb, w_vm[1][...])
        hg_ref[...] = hg
        hu_ref[...] = hu
        act = (hg * _sigmoid(hg)) * hu
        r = ALPHA * xv + 0.5 * _mm(act.astype(BF16), w_vm[2][...])
        if with_ple:
            pg = _mm(xb, w_vm[3][...])
            pp = _mm(p_ref[...].astype(BF16), w_vm[4][...])
            pg_ref[...] = pg
            pp_ref[...] = pp
            r = r + _sigmoid(pg) * pp
        r_ref[...] = r
        xhat, _ = _layer_norm_stats(r)
        xn = xhat * g_ref[...] + b_ref[...]
        xn_ref[...] = xn
        xnb_ref[...] = xn.astype(BF16)

    d, f = D_MODEL, D_FF
    in_specs = [_row_spec(tm, d), _full_spec((1, d)), _full_spec((1, d))]
    args = [x, ln_g, ln_b]
    if with_ple:
        in_specs.append(_row_spec(tm, PLE_DIM))
        args.append(ple[0])
    in_specs += [_ANY] * len(weights)
    args += weights
    out_shape = [jax.ShapeDtypeStruct((s, f), F32), jax.ShapeDtypeStruct((s, f), F32),
                 jax.ShapeDtypeStruct((s, d), F32), jax.ShapeDtypeStruct((s, d), F32), jax.ShapeDtypeStruct((s, d), BF16)]
    out_specs = [_row_spec(tm, f), _row_spec(tm, f), _row_spec(tm, d), _row_spec(tm, d), _row_spec(tm, d)]
    if with_ple:
        out_shape += [jax.ShapeDtypeStruct((s, d), F32)] * 2
        out_specs += [_row_spec(tm, d)] * 2
    scratch = [pltpu.VMEM(w.shape, w.dtype) for w in weights] + [pltpu.SemaphoreType.DMA((len(weights),))]
    f_in, f_out, f_scr, _ = fused if fused is not None else ([], [], [], None)
    n_out = len(out_shape)
    outs = pl.pallas_call(
        _fuse(body, len(args), n_out, len(scratch), s // tm, fused),
        name=("ffn_fwd_ple" if with_ple else "ffn_fwd") + ("_gather" if fused is not None else ""), grid=(s // tm,),
        in_specs=in_specs + [_ANY] * len(f_in), out_specs=out_specs + [_ANY] * len(f_out),
        out_shape=out_shape + list(f_out), scratch_shapes=scratch + list(f_scr), compiler_params=_params("arbitrary"),
    )(*args, *f_in)
    return tuple(outs[:n_out]) + ((list(outs[n_out:]),) if fused is not None else ())


def ffn_bwd(dxn, r, x, hg, hu, ln_g, w_gate, w_up, w_down, ple=None):
    s = x.shape[0]
    with_ple = ple is not None
    d, f = D_MODEL, D_FF
    suffix = "_ple" if with_ple else ""

    tm = ROW_TILE

    def body_a(*refs):
        it = iter(refs)
        dxn_ref, r_ref, hg_ref, hu_ref, g_ref = (next(it) for _ in range(5))
        pg_ref, pp_ref = (next(it), next(it)) if with_ple else (None, None)
        wd_hbm = next(it)
        dr_ref, act_ref, dhg_ref, dhu_ref, dy_ref, dg_ref, db_ref = (next(it) for _ in range(7))
        dpg_ref, dpp_ref = (next(it), next(it)) if with_ple else (None, None)
        wd_vm, sems = next(it), next(it)
        step = pl.program_id(0)
        _load_resident(step, [(wd_hbm, wd_vm)], sems)

        @pl.when(step == 0)
        def _():
            dg_ref[...] = jnp.zeros_like(dg_ref)
            db_ref[...] = jnp.zeros_like(db_ref)

        dxn_v = dxn_ref[...]
        xhat, rstd = _layer_norm_stats(r_ref[...])
        dg_ref[...] += jnp.sum(dxn_v * xhat, axis=0, keepdims=True)
        db_ref[...] += jnp.sum(dxn_v, axis=0, keepdims=True)
        dyh = dxn_v * g_ref[...]
        dr = rstd * (dyh - jnp.mean(dyh, axis=-1, keepdims=True) - xhat * jnp.mean(dyh * xhat, axis=-1, keepdims=True))
        dr_ref[...] = dr
        dy = (0.5 * dr).astype(BF16)
        dy_ref[...] = dy
        da = _mm_nt(dy, wd_vm[...])
        hg_v = hg_ref[...]
        hu_v = hu_ref[...]
        sil, dsil = _silu_and_grad(hg_v)
        act_ref[...] = (sil * hu_v).astype(BF16)
        dhu_ref[...] = (da * sil).astype(BF16)
        dhg_ref[...] = (da * hu_v * dsil).astype(BF16)
        if with_ple:
            sp = _sigmoid(pg_ref[...])
            dpp_ref[...] = (dr * sp).astype(BF16)
            dpg_ref[...] = (dr * pp_ref[...] * sp * (1.0 - sp)).astype(BF16)

    in_specs = [_row_spec(tm, d), _row_spec(tm, d), _row_spec(tm, f), _row_spec(tm, f), _full_spec((1, d))]
    args = [dxn, r, hg, hu, ln_g]
    if with_ple:
        in_specs += [_row_spec(tm, d), _row_spec(tm, d)]
        args += [ple[0], ple[1]]
    out_shape = [jax.ShapeDtypeStruct((s, d), F32), jax.ShapeDtypeStruct((s, f), BF16), jax.ShapeDtypeStruct((s, f), BF16),
                 jax.ShapeDtypeStruct((s, f), BF16), jax.ShapeDtypeStruct((s, d), BF16),
                 jax.ShapeDtypeStruct((1, d), F32), jax.ShapeDtypeStruct((1, d), F32)]
    out_specs = [_row_spec(tm, d), _row_spec(tm, f), _row_spec(tm, f), _row_spec(tm, f), _row_spec(tm, d),
                 _full_spec((1, d)), _full_spec((1, d))]
    if with_ple:
        out_shape += [jax.ShapeDtypeStruct((s, d), BF16)] * 2
        out_specs += [_row_spec(tm, d)] * 2
    first = pl.pallas_call(
        body_a, name="ffn_bwd_hidden" + suffix, grid=(s // tm,), in_specs=in_specs + [_ANY], out_specs=out_specs,
        out_shape=out_shape, scratch_shapes=[pltpu.VMEM(w_down.shape, w_down.dtype), pltpu.SemaphoreType.DMA((1,))],
        compiler_params=_params("arbitrary"),
    )(*args, w_down)
    dr, act, dhg, dhu, dy, dg, db = first[:7]

    tb = min(ROW_TILE_BWD, s)
    weights = [w_gate, w_up] + ([ple[2]] if with_ple else [])

    def body_b(*refs):
        it = iter(refs)
        dr_ref, dhg_ref, dhu_ref = next(it), next(it), next(it)
        dpg_ref = next(it) if with_ple else None
        w_hbm = [next(it) for _ in weights]
        dx_ref = next(it)
        w_vm = [next(it) for _ in weights]
        sems = next(it)
        _load_resident(pl.program_id(0), list(zip(w_hbm, w_vm)), sems)
        dx = ALPHA * dr_ref[...] + _mm_nt(dhg_ref[...], w_vm[0][...]) + _mm_nt(dhu_ref[...], w_vm[1][...])
        if with_ple:
            dx = dx + _mm_nt(dpg_ref[...], w_vm[2][...])
        dx_ref[...] = dx

    in_specs = [_row_spec(tb, d), _row_spec(tb, f), _row_spec(tb, f)] + ([_row_spec(tb, d)] if with_ple else [])
    args = [dr, dhg, dhu] + ([first[7]] if with_ple else [])
    dx = pl.pallas_call(
        body_b, name="ffn_bwd_input" + suffix, grid=(s // tb,), in_specs=in_specs + [_ANY] * len(weights),
        out_specs=_row_spec(tb, d), out_shape=jax.ShapeDtypeStruct((s, d), F32),
        scratch_shapes=[pltpu.VMEM(w.shape, w.dtype) for w in weights] + [pltpu.SemaphoreType.DMA((len(weights),))],
        compiler_params=_params("arbitrary"),
    )(*args, *weights)
    return (dx, act, dhg, dhu, dy, dg, db) + tuple(first[7:])


def win_fwd(x1, w_r, w_l, w_g):
    s = x1.shape[0]
    tm = min(ROW_TILE_BWD, s)
    weights = [w_r, w_l, w_g]

    def body(x_ref, wr_h, wl_h, wg_h, hr_ref, hl_ref, hgd_ref, wr_v, wl_v, wg_v, sems):
        _load_resident(pl.program_id(0), [(wr_h, wr_v), (wl_h, wl_v), (wg_h, wg_v)], sems)
        xb = x_ref[...].astype(BF16)
        hr_ref[...] = _mm(xb, wr_v[...])
        hl_ref[...] = _mm(xb, wl_v[...])
        hgd_ref[...] = _mm(xb, wg_v[...])

    return pl.pallas_call(
        body, name="win_fwd", grid=(s // tm,),
        in_specs=[_row_spec(tm, D_MODEL), _ANY, _ANY, _ANY],
        out_specs=[_row_spec(tm, RET_IN), _row_spec(tm, LRU_IN), _row_spec(tm, GDN_IN)],
        out_shape=[jax.ShapeDtypeStruct((s, RET_IN), F32), jax.ShapeDtypeStruct((s, LRU_IN), F32),
                   jax.ShapeDtypeStruct((s, GDN_IN), F32)],
        scratch_shapes=[pltpu.VMEM(w.shape, w.dtype) for w in weights] + [pltpu.SemaphoreType.DMA((3,))],
        compiler_params=_params("arbitrary"),
    )(x1, *weights)


def win_bwd(dr2, dhr, dhl, dhq, dab, w_r, w_l, w_g, w_ab):
    s = dr2.shape[0]
    tm = min(ROW_TILE_BWD, s)
    weights = [w_r, w_l, w_g, w_ab]
    nq = 4 * GDN_W

    def body(dr_ref, dhr_ref, dhl_ref, dhq_ref, dab_ref, wr_h, wl_h, wg_h, wab_h, dx_ref, wr_v, wl_v, wg_v, wab_v, sems):
        _load_resident(pl.program_id(0), [(wr_h, wr_v), (wl_h, wl_v), (wg_h, wg_v), (wab_h, wab_v)], sems)
        dx_ref[...] = (ALPHA * dr_ref[...] + _mm_nt(dhr_ref[...], wr_v[...]) + _mm_nt(dhl_ref[...], wl_v[...])
                       + _mm_nt(dhq_ref[...], wg_v[:, 0:nq]) + _mm_nt(dab_ref[...], wab_v[...]))

    return pl.pallas_call(
        body, name="win_bwd", grid=(s // tm,),
        in_specs=[_row_spec(tm, D_MODEL), _row_spec(tm, RET_IN), _row_spec(tm, LRU_IN), _row_spec(tm, nq), _row_spec(tm, LANES),
                  _ANY, _ANY, _ANY, _ANY],
        out_specs=_row_spec(tm, D_MODEL),
        out_shape=jax.ShapeDtypeStruct((s, D_MODEL), F32),
        scratch_shapes=[pltpu.VMEM(w.shape, w.dtype) for w in weights] + [pltpu.SemaphoreType.DMA((4,))],
        compiler_params=_params("arbitrary"),
    )(dr2, dhr, dhl, dhq, dab, *weights)


def out_fwd(o_r, o_l, o_g, x1, w_out, ln_g, ln_b):
    s = x1.shape[0]
    tm = ROW_TILE

    def body(or_ref, ol_ref, og_ref, x_ref, g_ref, b_ref, w_h, r_ref, xn_ref, xnb_ref, ocat_ref, w_v, sems):
        _load_resident(pl.program_id(0), [(w_h, w_v)], sems)
        ocat = jnp.concatenate([or_ref[...], ol_ref[...], og_ref[...]], axis=1).astype(BF16)
        ocat_ref[...] = ocat
        r = ALPHA * x_ref[...] + _mm(ocat, w_v[...])
        r_ref[...] = r
        xhat, _ = _layer_norm_stats(r)
        xn = xhat * g_ref[...] + b_ref[...]
        xn_ref[...] = xn
        xnb_ref[...] = xn.astype(BF16)

    d = D_MODEL
    return pl.pallas_call(
        body, name="out_fwd", grid=(s // tm,),
        in_specs=[_row_spec(tm, RET_W), _row_spec(tm, LRU_W), _row_spec(tm, GDN_W), _row_spec(tm, d),
                  _full_spec((1, d)), _full_spec((1, d)), _ANY],
        out_specs=[_row_spec(tm, d)] * 4,
        out_shape=[jax.ShapeDtypeStruct((s, d), F32)] * 2 + [jax.ShapeDtypeStruct((s, d), BF16)] * 2,
        scratch_shapes=[pltpu.VMEM(w_out.shape, w_out.dtype), pltpu.SemaphoreType.DMA((1,))],
        compiler_params=_params("arbitrary"),
    )(o_r, o_l, o_g, x1, ln_g, ln_b, w_out)


def out_bwd(dxn, r2, ln_g, w_out):
    s = dxn.shape[0]
    tm = ROW_TILE

    def body(dxn_ref, r_ref, g_ref, w_h, dr_ref, drb_ref, dor_ref, dol_ref, dog_ref, dg_ref, db_ref, w_v, sems):
        step = pl.program_id(0)
        _load_resident(step, [(w_h, w_v)], sems)

        @pl.when(step == 0)
        def _():
            dg_ref[...] = jnp.zeros_like(dg_ref)
            db_ref[...] = jnp.zeros_like(db_ref)

        dxn_v = dxn_ref[...]
        xhat, rstd = _layer_norm_stats(r_ref[...])
        dg_ref[...] += jnp.sum(dxn_v * xhat, axis=0, keepdims=True)
        db_ref[...] += jnp.sum(dxn_v, axis=0, keepdims=True)
        dyh = dxn_v * g_ref[...]
        dr = rstd * (dyh - jnp.mean(dyh, axis=-1, keepdims=True) - xhat * jnp.mean(dyh * xhat, axis=-1, keepdims=True))
        dr_ref[...] = dr
        drb = dr.astype(BF16)
        drb_ref[...] = drb
        dor_ref[...] = _mm_nt(drb, w_v[0:RET_W, :])
        dol_ref[...] = _mm_nt(drb, w_v[RET_W:RET_W + LRU_W, :])
        dog_ref[...] = _mm_nt(drb, w_v[RET_W + LRU_W:, :])

    d = D_MODEL
    return pl.pallas_call(
        body, name="out_bwd", grid=(s // tm,),
        in_specs=[_row_spec(tm, d), _row_spec(tm, d), _full_spec((1, d)), _ANY],
        out_specs=[_row_spec(tm, d), _row_spec(tm, d), _row_spec(tm, RET_W), _row_spec(tm, LRU_W), _row_spec(tm, GDN_W),
                   _full_spec((1, d)), _full_spec((1, d))],
        out_shape=[jax.ShapeDtypeStruct((s, d), F32), jax.ShapeDtypeStruct((s, d), BF16),
                   jax.ShapeDtypeStruct((s, RET_W), F32), jax.ShapeDtypeStruct((s, LRU_W), F32),
                   jax.ShapeDtypeStruct((s, GDN_W), F32), jax.ShapeDtypeStruct((1, d), F32), jax.ShapeDtypeStruct((1, d), F32)],
        scratch_shapes=[pltpu.VMEM(w_out.shape, w_out.dtype), pltpu.SemaphoreType.DMA((1,))],
        compiler_params=_params("arbitrary"),
    )(dxn, r2, ln_g, w_out)


def wgrad(a, b, name, out_dtype=BF16):
    s, m = a.shape
    n = b.shape[1]
    tk = 1024 if s % 1024 == 0 else s
    tm = next((c for c in (1408, 1024, 768, 512, 384, 256) if m % c == 0), m)
    tn = next((c for c in (1408, 1152, 1024, 768, 512) if n % c == 0), n)
    nk = s // tk

    def body(a_ref, b_ref, o_ref, acc_ref):
        k = pl.program_id(2)

        @pl.when(k == 0)
        def _():
            acc_ref[...] = jnp.zeros_like(acc_ref)

        acc_ref[...] += _mm_tn(a_ref[...].astype(BF16), b_ref[...].astype(BF16))

        @pl.when(k == nk - 1)
        def _():
            o_ref[...] = acc_ref[...].astype(o_ref.dtype)

    return pl.pallas_call(
        body, name=name, grid=(m // tm, n // tn, nk),
        in_specs=[pl.BlockSpec((tk, tm), lambda i, j, k: (k, i)), pl.BlockSpec((tk, tn), lambda i, j, k: (k, j))],
        out_specs=pl.BlockSpec((tm, tn), lambda i, j, k: (i, j)),
        out_shape=jax.ShapeDtypeStruct((m, n), out_dtype),
        scratch_shapes=[pltpu.VMEM((tm, tn), F32)],
        compiler_params=_params("arbitrary", "arbitrary", "arbitrary"),
    )(a, b)


def loss_and_grad(y, target):
    s, d = y.shape
    tm = ROW_TILE

    def body(y_ref, t_ref, dy_ref, l_ref):
        @pl.when(pl.program_id(0) == 0)
        def _():
            l_ref[...] = jnp.zeros_like(l_ref)

        err = y_ref[...] - t_ref[...]
        dy_ref[...] = err / d
        l_ref[...] += 0.5 * jnp.sum(jnp.mean(err * err, axis=-1, keepdims=True), axis=0, keepdims=True)

    return pl.pallas_call(
        body, name="loss_and_grad", grid=(s // tm,),
        in_specs=[_row_spec(tm, d), _row_spec(tm, d)],
        out_specs=[_row_spec(tm, d), _full_spec((1, 1))],
        out_shape=[jax.ShapeDtypeStruct((s, d), F32), jax.ShapeDtypeStruct((1, 1), F32)],
        compiler_params=_params("arbitrary"),
    )(y, target)


def _ret_consts():
    lg = np.log1p(-np.exp2(-5.0 - np.arange(RET_HEADS, dtype=np.float64)))
    idx = np.arange(CHUNK, dtype=np.float64)
    intra = np.exp(np.abs(idx[:, None] - idx[None, :])[None] * lg[:, None, None])
    cross = np.repeat(np.exp((idx + 1.0)[:, None] * lg[None, :]), HEAD, axis=1)
    tail = np.repeat(np.exp((CHUNK - 1.0 - idx)[:, None] * lg[None, :]), HEAD, axis=1)
    dec = np.repeat(np.exp(CHUNK * lg)[None, :], HEAD, axis=1)
    half = HEAD // 2
    inv_freq = (ROPE_THETA ** (-jnp.arange(half, dtype=F32) / half))
    invf = jnp.tile(inv_freq, 2 * LANES // HEAD)[None, :]
    sgn = np.tile(np.concatenate([-np.ones(half), np.ones(half)]), LANES // HEAD)[None, :]
    f = lambda a: jnp.asarray(a, F32)
    return dict(intra=f(intra), cross=f(cross), tail=f(tail), dec=f(dec), invf=invf, sgn=f(sgn))


def _swap_halves(t):
    lane = lax.broadcasted_iota(jnp.int32, t.shape, 1)
    return jnp.where((lane & 32) == 0, pltpu.roll(t, LANES - 32, 1), pltpu.roll(t, 32, 1))


def _rope(t, c, s):
    return t * c + _swap_halves(t) * s


def _rope_transposed(g, c, s):
    return g * c + _swap_halves(g * s)


def _head_mask(hd):
    lane = lax.broadcasted_iota(jnp.int32, (1, LANES), 1)
    return ((lane >= HEAD * hd) & (lane < HEAD * (hd + 1))).astype(F32)


def _block_diag_mask():
    r = lax.broadcasted_iota(jnp.int32, (LANES, LANES), 0)
    c = lax.broadcasted_iota(jnp.int32, (LANES, LANES), 1)
    return ((r >= HEAD) == (c >= HEAD)).astype(F32)


RET_STEP_CHUNKS = 4


def _ret_specs(n_of, gch):
    cst = lambda shape: pl.BlockSpec(shape, lambda i: (0,) * len(shape))
    return [pl.BlockSpec((CHUNK * gch, RET_IN), lambda i: (n_of(i), 0)), pl.BlockSpec((CHUNK * gch, 1), lambda i: (n_of(i), 0)),
            cst((1, LANES)), cst((1, LANES)), cst((RET_HEADS, CHUNK, CHUNK)), cst((CHUNK, RET_W)), cst((CHUNK, RET_W)),
            cst((1, RET_W)), cst((1, RET_W))]


def ret_fwd(hr, pos, norm_g):
    s = hr.shape[0]
    n_chunks = s // CHUNK
    cs = _ret_consts()
    n_slab = RET_W // LANES

    gch = min(RET_STEP_CHUNKS, n_chunks)

    def body(hr_ref, pos_ref, invf_ref, sgn_ref, intra_ref, cross_ref, tail_ref, dec_ref, g_ref, o_ref, opre_ref, st_ref, state):
        @pl.when(pl.program_id(0) == 0)
        def _():
            state[...] = jnp.zeros_like(state)

        bd = _block_diag_mask()
        sts = [state[LANES * sl:LANES * (sl + 1), :] for sl in range(n_slab)]
        for c in range(gch):
            tok = slice(CHUNK * c, CHUNK * (c + 1))
            ang = pos_ref[tok, :].astype(F32) * invf_ref[...]
            cosv = jnp.cos(ang)
            sinv = jnp.sin(ang) * sgn_ref[...]
            for sl in range(n_slab):
                lanes = slice(LANES * sl, LANES * (sl + 1))
                q = hr_ref[tok, LANES * sl:LANES * (sl + 1)]
                k = hr_ref[tok, RET_W + LANES * sl:RET_W + LANES * (sl + 1)]
                v = hr_ref[tok, 2 * RET_W + LANES * sl:2 * RET_W + LANES * (sl + 1)]
                gate = hr_ref[tok, 3 * RET_W + LANES * sl:3 * RET_W + LANES * (sl + 1)]
                qt = _rope(q, cosv, sinv) * (HEAD ** -0.5)
                kt = _rope(k, cosv, sinv)
                st = sts[sl]
                st_ref[RET_W * c + LANES * sl:RET_W * c + LANES * (sl + 1), :] = st
                o = _mm(qt * cross_ref[:, lanes], st)
                for hd in range(2):
                    m = _head_mask(hd)
                    sc = _mm_nt(qt * m, kt) * intra_ref[2 * sl + hd]
                    o = o + _mm(sc, v) * m
                sts[sl] = st * dec_ref[:, lanes] + _mm_tn(kt, v * tail_ref[:, lanes]) * bd
                opre_ref[tok, lanes] = o
                mu = _group_sum_slab(o) * (1.0 / HEAD)
                dlt = o - mu
                var = _group_sum_slab(dlt * dlt) * (1.0 / HEAD)
                on = dlt * lax.rsqrt(var + 1e-5)
                o_ref[tok, lanes] = on * g_ref[:, lanes] * (gate * _sigmoid(gate))
        for sl in range(n_slab):
            state[LANES * sl:LANES * (sl + 1), :] = sts[sl]

    out_row = lambda w: pl.BlockSpec((CHUNK * gch, w), lambda i: (i, 0))
    return pl.pallas_call(
        body, name="ret_fwd", grid=(n_chunks // gch,),
        in_specs=_ret_specs(lambda i: i, gch),
        out_specs=[out_row(RET_W), out_row(RET_W), pl.BlockSpec((RET_W * gch, LANES), lambda i: (i, 0))],
        out_shape=[jax.ShapeDtypeStruct((s, RET_W), F32), jax.ShapeDtypeStruct((s, RET_W), F32),
                   jax.ShapeDtypeStruct((n_chunks * RET_W, LANES), F32)],
        scratch_shapes=[pltpu.VMEM((RET_W, LANES), F32)],
        compiler_params=_params("arbitrary"),
    )(hr, pos, cs["invf"], cs["sgn"], cs["intra"], cs["cross"], cs["tail"], cs["dec"], norm_g)


def ret_bwd(hr, pos, norm_g, opre, states, dout):
    s = hr.shape[0]
    n_chunks = s // CHUNK
    cs = _ret_consts()
    n_slab = RET_W // LANES
    gch = min(RET_STEP_CHUNKS, n_chunks)
    rev = lambda i: n_chunks // gch - 1 - i

    def body(hr_ref, pos_ref, invf_ref, sgn_ref, intra_ref, cross_ref, tail_ref, dec_ref, g_ref, opre_ref, st_ref, do_ref,
             dh_ref, dg_ref, gstate):
        @pl.when(pl.program_id(0) == 0)
        def _():
            gstate[...] = jnp.zeros_like(gstate)
            dg_ref[...] = jnp.zeros_like(dg_ref)

        bd = _block_diag_mask()
        gss = [gstate[LANES * sl:LANES * (sl + 1), :] for sl in range(n_slab)]
        dgs = [jnp.zeros((1, LANES), F32) for _ in range(n_slab)]
        for c in reversed(range(gch)):
            tok = slice(CHUNK * c, CHUNK * (c + 1))
            ang = pos_ref[tok, :].astype(F32) * invf_ref[...]
            cosv = jnp.cos(ang)
            sinv = jnp.sin(ang) * sgn_ref[...]
            for sl in range(n_slab):
                lanes = slice(LANES * sl, LANES * (sl + 1))
                q = hr_ref[tok, LANES * sl:LANES * (sl + 1)]
                k = hr_ref[tok, RET_W + LANES * sl:RET_W + LANES * (sl + 1)]
                v = hr_ref[tok, 2 * RET_W + LANES * sl:2 * RET_W + LANES * (sl + 1)]
                gate = hr_ref[tok, 3 * RET_W + LANES * sl:3 * RET_W + LANES * (sl + 1)]
                qt = _rope(q, cosv, sinv) * (HEAD ** -0.5)
                kt = _rope(k, cosv, sinv)
                o = opre_ref[tok, lanes]
                mu = _group_sum_slab(o) * (1.0 / HEAD)
                dlt = o - mu
                var = _group_sum_slab(dlt * dlt) * (1.0 / HEAD)
                rstd = lax.rsqrt(var + 1e-5)
                on = dlt * rstd
                sil, dsil = _silu_and_grad(gate)
                dout_v = do_ref[tok, lanes]
                gn = g_ref[:, lanes]
                dgs[sl] = dgs[sl] + jnp.sum(dout_v * on * sil, axis=0, keepdims=True)
                d_on = dout_v * gn * sil
                dgate = dout_v * on * gn * dsil
                d_o = rstd * (d_on - _group_sum_slab(d_on) * (1.0 / HEAD) - on * (_group_sum_slab(d_on * on) * (1.0 / HEAD)))
                st = st_ref[RET_W * c + LANES * sl:RET_W * c + LANES * (sl + 1), :]
                gs = gss[sl]
                cross = cross_ref[:, lanes]
                tail = tail_ref[:, lanes]
                dqt = _mm_nt(d_o, st) * cross
                ds_here = _mm_tn(qt * cross, d_o) * bd
                vt = v * tail
                dkt = _mm_nt(vt, gs)
                dv = _mm(kt, gs) * tail
                for hd in range(2):
                    m = _head_mask(hd)
                    qm = qt * m
                    dom = d_o * m
                    intra = intra_ref[2 * sl + hd]
                    sc = _mm_nt(qm, kt) * intra
                    dsc = _mm_nt(dom, v) * intra
                    dqt = dqt + _mm(dsc, kt) * m
                    dkt = dkt + _mm_tn(dsc, qm)
                    dv = dv + _mm_tn(sc, dom)
                gss[sl] = gs * dec_ref[:, lanes] + ds_here
                dh_ref[tok, LANES * sl:LANES * (sl + 1)] = _rope_transposed(dqt * (HEAD ** -0.5), cosv, sinv).astype(BF16)
                dh_ref[tok, RET_W + LANES * sl:RET_W + LANES * (sl + 1)] = _rope_transposed(dkt, cosv, sinv).astype(BF16)
                dh_ref[tok, 2 * RET_W + LANES * sl:2 * RET_W + LANES * (sl + 1)] = dv.astype(BF16)
                dh_ref[tok, 3 * RET_W + LANES * sl:3 * RET_W + LANES * (sl + 1)] = dgate.astype(BF16)
        for sl in range(n_slab):
            gstate[LANES * sl:LANES * (sl + 1), :] = gss[sl]
            dg_ref[:, LANES * sl:LANES * (sl + 1)] += dgs[sl]

    row = lambda w: pl.BlockSpec((CHUNK * gch, w), lambda i: (rev(i), 0))
    return pl.pallas_call(
        body, name="ret_bwd", grid=(n_chunks // gch,),
        in_specs=_ret_specs(rev, gch) + [row(RET_W), pl.BlockSpec((RET_W * gch, LANES), lambda i: (rev(i), 0)), row(RET_W)],
        out_specs=[row(RET_IN), pl.BlockSpec((1, RET_W), lambda i: (0, 0))],
        out_shape=[jax.ShapeDtypeStruct((s, RET_IN), BF16), jax.ShapeDtypeStruct((1, RET_W), F32)],
        scratch_shapes=[pltpu.VMEM((RET_W, LANES), F32)],
        compiler_params=_params("arbitrary"),
    )(hr, pos, cs["invf"], cs["sgn"], cs["intra"], cs["cross"], cs["tail"], cs["dec"], norm_g, opre, states, dout)


def _lru_gates(xc, wa_ref, ba_ref, wx_ref, bx_ref, lam_ref):
    xcb = xc.astype(BF16)
    r = _sigmoid(_mm(xcb, wa_ref[...].astype(BF16)) + ba_ref[...])
    ig = _sigmoid(_mm(xcb, wx_ref[...].astype(BF16)) + bx_ref[...])
    lam = lam_ref[...]
    ls = jnp.minimum(lam, 0.0) - _log1p(jnp.exp(-jnp.abs(lam)))
    la = (LRU_C * r) * ls
    a = jnp.exp(la)
    mult = jnp.sqrt(-_expm1(2.0 * la))
    return r, ig, ls, a, mult


def _lru_conv(x, xprev, w_ref, b_ref):
    xc = b_ref[...] + w_ref[3:4, :] * x
    for j in (1, 2, 3):
        xc = xc + w_ref[3 - j:4 - j, :] * _shift_rows(x, xprev, j)
    return xc


def lru_fwd(hl, conv_w, conv_b, w_a, b_a, w_x, b_x, lam):
    s = hl.shape[0]
    ts = SCAN_TILE
    w = LRU_W

    def body(hl_ref, hp_ref, cw_ref, cb_ref, wa_ref, ba_ref, wx_ref, bx_ref, lam_ref, o_ref, xc_ref, h_ref, carry):
        i = pl.program_id(0)

        @pl.when(i == 0)
        def _():
            carry[...] = jnp.zeros_like(carry)

        x = hl_ref[:, 0:w]
        gate = hl_ref[:, w:2 * w]
        xprev = hp_ref[...] * (i > 0).astype(F32)
        xc = _lru_conv(x, xprev, cw_ref, cb_ref)
        xc_ref[...] = xc
        _, ig, _, a, mult = _lru_gates(xc, wa_ref, ba_ref, wx_ref, bx_ref, lam_ref)
        b = mult * (ig * xc)
        row = lax.broadcasted_iota(jnp.int32, (ts, w), 0)
        d = 1
        while d < ts:
            ap = jnp.where(row >= d, pltpu.roll(a, d, 0), 1.0)
            bp = jnp.where(row >= d, pltpu.roll(b, d, 0), 0.0)
            b = a * bp + b
            a = a * ap
            d *= 2
        h = b + a * carry[0:1, :]
        h_ref[...] = h
        carry[0:1, :] = h[ts - 1:ts, :]
        o_ref[...] = h * _gelu(gate)

    cst = lambda shape: pl.BlockSpec(shape, lambda i: (0, 0))
    return pl.pallas_call(
        body, name="lru_fwd", grid=(s // ts,),
        in_specs=[_row_spec(ts, 2 * w), pl.BlockSpec((ts, w), lambda i: (jnp.maximum(i - 1, 0), 0)),
                  cst((4, w)), cst((1, w)), cst((w, w)), cst((1, w)), cst((w, w)), cst((1, w)), cst((1, w))],
        out_specs=[_row_spec(ts, w)] * 3,
        out_shape=[jax.ShapeDtypeStruct((s, w), F32)] * 3,
        scratch_shapes=[pltpu.VMEM((8, w), F32)],
        compiler_params=_params("arbitrary"),
    )(hl, hl, conv_w, conv_b, w_a, b_a, w_x, b_x, lam)


def lru_bwd(hl, conv_w, conv_b, w_a, b_a, w_x, b_x, lam, xc_saved, h_saved, dout):
    s = hl.shape[0]
    ts = SCAN_TILE
    w = LRU_W
    nb = s // ts
    rev = lambda i: nb - 1 - i

    def body(hl_ref, hp_ref, cw_ref, cb_ref, wa_ref, ba_ref, wx_ref, bx_ref, lam_ref, xc_ref, h_ref, hprev_ref, do_ref,
             dhl_ref, dcw_ref, dcb_ref, dwa_ref, dba_ref, dwx_ref, dbx_ref, dlam_ref, carry, dxc_next):
        i = pl.program_id(0)
        blk = nb - 1 - i

        @pl.when(i == 0)
        def _():
            carry[...] = jnp.zeros_like(carry)
            dxc_next[...] = jnp.zeros_like(dxc_next)
            for ref in (dcw_ref, dcb_ref, dwa_ref, dba_ref, dwx_ref, dbx_ref, dlam_ref):
                ref[...] = jnp.zeros_like(ref)

        first = (blk > 0).astype(F32)
        x = hl_ref[:, 0:w]
        gate = hl_ref[:, w:2 * w]
        xprev = hp_ref[...] * first
        xc = xc_ref[...]
        h = h_ref[...]
        hprev = hprev_ref[...] * first
        r, ig, ls, a, mult = _lru_gates(xc, wa_ref, ba_ref, wx_ref, bx_ref, lam_ref)
        do = do_ref[...]
        dh = do * _gelu(gate)
        dgate = do * h * _gelu_grad(gate)
        row = lax.broadcasted_iota(jnp.int32, (ts, w), 0)
        ca = jnp.where(row < ts - 1, pltpu.roll(a, ts - 1, 0), 1.0)
        cb = dh
        d = 1
        while d < ts:
            an = jnp.where(row < ts - d, pltpu.roll(ca, ts - d, 0), 1.0)
            bn = jnp.where(row < ts - d, pltpu.roll(cb, ts - d, 0), 0.0)
            cb = cb + ca * bn
            ca = ca * an
            d *= 2
        lamb = cb + ca * carry[0:1, :]
        carry[0:1, :] = a[0:1, :] * lamb[0:1, :]
        h_before = _shift_rows(h, hprev, 1)
        da = lamb * h_before
        ix = ig * xc
        dmult = lamb * ix
        dig = lamb * mult * xc
        dxc = lamb * mult * ig
        dla = (da - dmult * a / mult) * a
        dr = dla * LRU_C * ls
        dlam_ref[...] += jnp.sum(dla * LRU_C * r, axis=0, keepdims=True) * _sigmoid(-lam_ref[...])
        dpa = dr * r * (1.0 - r)
        dpx = dig * ig * (1.0 - ig)
        dba_ref[...] += jnp.sum(dpa, axis=0, keepdims=True)
        dbx_ref[...] += jnp.sum(dpx, axis=0, keepdims=True)
        dpab = dpa.astype(BF16)
        dpxb = dpx.astype(BF16)
        xcb = xc.astype(BF16)
        dxc = dxc + _mm_nt(dpab, wa_ref[...].astype(BF16)) + _mm_nt(dpxb, wx_ref[...].astype(BF16))
        dwa_ref[...] += _mm_tn(xcb, dpab)
        dwx_ref[...] += _mm_tn(xcb, dpxb)
        dcb_ref[...] += jnp.sum(dxc, axis=0, keepdims=True)
        nxt = dxc_next[...]
        dx = cw_ref[3:4, :] * dxc
        dcw_ref[3:4, :] += jnp.sum(dxc * x, axis=0, keepdims=True)
        for j in (1, 2, 3):
            dx = dx + cw_ref[3 - j:4 - j, :] * _shift_rows_up(dxc, nxt, j)
            dcw_ref[3 - j:4 - j, :] += jnp.sum(dxc * _shift_rows(x, xprev, j), axis=0, keepdims=True)
        dxc_next[...] = dxc
        dhl_ref[:, 0:w] = dx.astype(BF16)
        dhl_ref[:, w:2 * w] = dgate.astype(BF16)

    cst = lambda shape: pl.BlockSpec(shape, lambda i: (0, 0))
    rowr = lambda width: pl.BlockSpec((ts, width), lambda i: (rev(i), 0))
    prevr = lambda width: pl.BlockSpec((ts, width), lambda i: (jnp.maximum(rev(i) - 1, 0), 0))
    return pl.pallas_call(
        body, name="lru_bwd", grid=(nb,),
        in_specs=[rowr(2 * w), prevr(w), cst((4, w)), cst((1, w)), cst((w, w)), cst((1, w)), cst((w, w)), cst((1, w)), cst((1, w)),
                  rowr(w), rowr(w), prevr(w), rowr(w)],
        out_specs=[rowr(2 * w), cst((4, w)), cst((1, w)), cst((w, w)), cst((1, w)), cst((w, w)), cst((1, w)), cst((1, w))],
        out_shape=[jax.ShapeDtypeStruct((s, 2 * w), BF16), jax.ShapeDtypeStruct((4, w), F32), jax.ShapeDtypeStruct((1, w), F32),
                   jax.ShapeDtypeStruct((w, w), F32), jax.ShapeDtypeStruct((1, w), F32), jax.ShapeDtypeStruct((w, w), F32),
                   jax.ShapeDtypeStruct((1, w), F32), jax.ShapeDtypeStruct((1, w), F32)],
        scratch_shapes=[pltpu.VMEM((8, w), F32), pltpu.VMEM((ts, w), F32)],
        compiler_params=_params("arbitrary"),
    )(hl, hl, conv_w, conv_b, w_a, b_a, w_x, b_x, lam, xc_saved, h_saved, h_saved, dout)


GDN_QKV = 3 * GDN_W
GDN_STEP_CHUNKS = 4
GDN_BWD_STEP_CHUNKS = 2


def _tri_inverse_many(nms):
    r = lax.broadcasted_iota(jnp.int32, nms[0].shape, 0)
    c = lax.broadcasted_iota(jnp.int32, nms[0].shape, 1)
    eye = (r == c).astype(F32)
    ts = [eye - nm for nm in nms]
    ps = list(nms)
    for _ in range(5):
        ps = [_mm3(p, p) for p in ps]
        ts = [t + _mm3(t, p) for t, p in zip(ts, ps)]
    return ts


def _gdn_front(hx_ref, hprev, cw_ref, al_ref, dt_ref):
    w = GDN_W
    x = hx_ref[:, 0:GDN_QKV]
    y = cw_ref[3:4, :] * x
    for j in (1, 2, 3):
        y = y + cw_ref[3 - j:4 - j, :] * _shift_rows(x, hprev, j)
    qkv, dsil = _silu_and_grad(y)
    q, k, v = qkv[:, 0:w], qkv[:, w:2 * w], qkv[:, 2 * w:3 * w]
    rq = lax.rsqrt(_group_sum(q * q) + 1e-6)
    rk = lax.rsqrt(_group_sum(k * k) + 1e-6)
    beta = _sigmoid(hx_ref[:, 5 * w:6 * w])
    sp_in = hx_ref[:, 4 * w:5 * w] + dt_ref[...]
    neg_a = -jnp.exp(al_ref[...])
    g = neg_a * _softplus(sp_in)
    n_c = g.shape[0] // CHUNK
    gc = jnp.concatenate([_rows_prefix_sum(g[CHUNK * c:CHUNK * (c + 1)]) for c in range(n_c)], axis=0)
    return dict(x=x, dsil=dsil, qn=q * rq, kn=k * rk, v=v, rq=rq, rk=rk, beta=beta, sp_in=sp_in, neg_a=neg_a, g=g, gc=gc)


def _stack_heads(x):
    return jnp.concatenate([x * _head_mask(0), x * _head_mask(1)], axis=0)


def _unstack_heads(y):
    return y[0:CHUNK] + y[CHUNK:2 * CHUNK]


def _head_transpose(x):
    return jnp.concatenate([x[:, 0:HEAD].T, x[:, HEAD:2 * HEAD].T], axis=1)


def _head_total(x):
    cols = jnp.broadcast_to(jnp.sum(x, axis=0, keepdims=True), (8, LANES))
    return _group_sum_slab(cols)[0:1]


def _slab_tri_masks():
    r = lax.broadcasted_iota(jnp.int32, (CHUNK, LANES), 0)
    c = lax.broadcasted_iota(jnp.int32, (CHUNK, LANES), 1) & (HEAD - 1)
    return r >= c, r > c


def _gdn_slab(fr, c, sl, tri):
    lower, strict = tri
    ls = lambda a: a[CHUNK * c:CHUNK * (c + 1), LANES * sl:LANES * (sl + 1)]
    k = ls(fr["kn"])
    q = ls(fr["qn"]) * (HEAD ** -0.5)
    v = ls(fr["v"])
    beta = ls(fr["beta"])
    gc = ls(fr["gc"])
    e = jnp.exp(gc)
    gl = gc[CHUNK - 1:CHUNK, :]
    xt = jnp.exp(gl - gc)
    dec = jnp.where(lower, jnp.exp(jnp.minimum(gc - _head_transpose(gc), 0.0)), 0.0)
    kbd = _stack_heads(k)
    kk = _mm_nt(k, kbd)
    qkr = _mm_nt(q, kbd)
    return dict(k=k, q=q, v=v, beta=beta, e=e, egl=jnp.exp(gl), xt=xt, dec=dec, kk=kk, qkr=qkr, kbd=kbd,
                nm=jnp.where(strict, beta * kk * dec, 0.0))


def gdn_fwd(hx, conv_w, a_log_e, dt_bias_e, norm_g_e, fused=None):
    s = hx.shape[0]
    n_chunks = s // CHUNK
    w = GDN_W
    n_slab = w // LANES
    gch = min(GDN_STEP_CHUNKS, n_chunks)

    def body(hx_ref, hp_ref, cw_ref, al_ref, dt_ref, ng_ref, o_ref, opre_ref, t_ref, st_ref, state):
        n = pl.program_id(0)

        @pl.when(n == 0)
        def _():
            state[...] = jnp.zeros_like(state)

        fr = _gdn_front(hx_ref, hp_ref[...] * (n > 0).astype(F32), cw_ref, al_ref, dt_ref)
        tri = _slab_tri_masks()
        bd = _block_diag_mask()
        sts = [state[LANES * sl:LANES * (sl + 1), :] for sl in range(n_slab)]
        slabs = [[_gdn_slab(fr, c, sl, tri) for sl in range(n_slab)] for c in range(gch)]
        tbd = _tri_inverse_many([_stack_heads(sq["nm"]) for row_ in slabs for sq in row_])
        o_rows = []
        for c in range(gch):
            ts, outs = [], []
            st_ref[w * c:w * (c + 1), :] = jnp.concatenate(sts, axis=0)
            for sl in range(n_slab):
                sq = slabs[c][sl]
                t = _unstack_heads(tbd[n_slab * c + sl])
                ts.append(t)
                u = _mm(t, _stack_heads(sq["v"] * sq["beta"]))
                wk = _mm(t, _stack_heads(sq["k"] * (sq["beta"] * sq["e"])))
                st = sts[sl]
                vnew = u - _mm(wk, st)
                outs.append(_mm(sq["q"] * sq["e"], st) + _mm(sq["qkr"] * sq["dec"], _stack_heads(vnew)))
                sts[sl] = st * sq["egl"] + _mm_tn(sq["k"] * sq["xt"], vnew) * bd
            t_ref[CHUNK * c:CHUNK * (c + 1), :] = jnp.concatenate(ts, axis=1)
            o_rows.append(jnp.concatenate(outs, axis=1))
        state[...] = jnp.concatenate(sts, axis=0)
        o = jnp.concatenate(o_rows, axis=0)
        opre_ref[...] = o
        rinv = lax.rsqrt(_group_sum(o * o) * (1.0 / HEAD) + 1e-6)
        z = hx_ref[:, 3 * w:4 * w]
        o_ref[...] = (o * rinv) * ng_ref[...] * (z * _sigmoid(z))

    cst = lambda shape: pl.BlockSpec(shape, lambda i: (0, 0))
    row = lambda width: pl.BlockSpec((CHUNK * gch, width), lambda i: (i, 0))
    f_in, f_out, f_scr, _ = fused if fused is not None else ([], [], [], None)
    outs = pl.pallas_call(
        _fuse(body, 6, 4, 1, n_chunks // gch, fused), name="gdn_fwd" + ("_gather" if fused is not None else ""),
        grid=(n_chunks // gch,),
        in_specs=[row(GDN_IN), pl.BlockSpec((CHUNK * gch, GDN_QKV), lambda i: (jnp.maximum(i - 1, 0), 0)),
                  cst((4, GDN_QKV)), cst((1, w)), cst((1, w)), cst((1, w))] + [_ANY] * len(f_in),
        out_specs=[row(w)] * 3 + [pl.BlockSpec((w * gch, LANES), lambda i: (i, 0))] + [_ANY] * len(f_out),
        out_shape=[jax.ShapeDtypeStruct((s, w), F32)] * 3 + [jax.ShapeDtypeStruct((n_chunks * w, LANES), F32)] + list(f_out),
        scratch_shapes=[pltpu.VMEM((w, LANES), F32)] + list(f_scr),
        compiler_params=_params("arbitrary"),
    )(hx, hx, conv_w, a_log_e, dt_bias_e, norm_g_e, *f_in)
    return tuple(outs[:4]) + ((list(outs[4:]),) if fused is not None else ())


def gdn_bwd(hx, conv_w, a_log_e, dt_bias_e, norm_g_e, opre, tmat, states, dout, fused=None):
    s = hx.shape[0]
    n_chunks = s // CHUNK
    w = GDN_W
    n_slab = w // LANES
    gch = min(GDN_BWD_STEP_CHUNKS, n_chunks)
    n_blocks = n_chunks // gch
    rev = lambda i: n_blocks - 1 - i

    def body(hx_ref, hp_ref, cw_ref, al_ref, dt_ref, ng_ref, opre_ref, t_ref, st_ref, do_ref,
             dhx_ref, dab_ref, dcw_ref, dal_ref, ddt_ref, dng_ref, dstate, dy_next):
        i = pl.program_id(0)
        n = n_blocks - 1 - i

        @pl.when(i == 0)
        def _():
            dstate[...] = jnp.zeros_like(dstate)
            dy_next[...] = jnp.zeros_like(dy_next)
            for ref in (dcw_ref, dal_ref, ddt_ref, dng_ref):
                ref[...] = jnp.zeros_like(ref)

        hprev = hp_ref[...] * (n > 0).astype(F32)
        fr = _gdn_front(hx_ref, hprev, cw_ref, al_ref, dt_ref)
        tri = _slab_tri_masks()
        lower, strict = tri
        o = opre_ref[...]
        rinv = lax.rsqrt(_group_sum(o * o) * (1.0 / HEAD) + 1e-6)
        yn = o * rinv
        z = hx_ref[:, 3 * w:4 * w]
        sil, dsil_z = _silu_and_grad(z)
        dout_v = do_ref[...]
        ng = ng_ref[...]
        dng_ref[...] += jnp.sum(dout_v * yn * sil, axis=0, keepdims=True)
        dz = dout_v * yn * ng * dsil_z
        dyn = dout_v * ng * sil
        d_o = rinv * (dyn - yn * (_group_sum(dyn * yn) * (1.0 / HEAD)))
        last_row = (lax.broadcasted_iota(jnp.int32, (CHUNK, LANES), 0) == CHUNK - 1).astype(F32)
        bd = _block_diag_mask()
        gsum = _group_sum_slab
        t_all, st_all = t_ref[...], st_ref[...]
        dsns = [dstate[LANES * sl:LANES * (sl + 1), :] for sl in range(n_slab)]
        per_chunk = {}
        order = [(c_, s_) for c_ in reversed(range(gch)) for s_ in range(n_slab)]
        chain = {}
        for c, sl in order:
            lanes = slice(LANES * sl, LANES * (sl + 1))
            tok = slice(CHUNK * c, CHUNK * (c + 1))
            sq = _gdn_slab(fr, c, sl, tri)
            t = t_all[tok, lanes]
            st = st_all[w * c + LANES * sl:w * c + LANES * (sl + 1), :]
            dsn = dsns[sl]
            do_s = d_o[tok, lanes]
            u = _mm(t, _stack_heads(sq["v"] * sq["beta"]))
            wk = _mm(t, _stack_heads(sq["k"] * (sq["beta"] * sq["e"])))
            kt = sq["k"] * sq["xt"]
            dvnew = _unstack_heads(_mm_tn(sq["qkr"] * sq["dec"], do_s) * bd) + _mm(kt, dsn)
            dsns[sl] = _mm_tn(sq["q"] * sq["e"], do_s) * bd + sq["egl"] * dsn - _mm_tn(wk, dvnew) * bd
            chain[(c, sl)] = (sq, t, st, dsn, do_s, u, wk, kt, dvnew)
        for c, sl in order:
            sq, t, st, dsn, do_s, u, wk, kt, dvnew = chain[(c, sl)]
            k, q, v, beta, e, xt, dec, kk, qkr, kbd = (sq[n_] for n_ in ("k", "q", "v", "beta", "e", "xt", "dec", "kk", "qkr", "kbd"))
            vnew = u - _mm(wk, st)
            dqd = _mm_nt(do_s, st)
            dqk = _mm_nt(do_s, _stack_heads(vnew))
            dkt = _mm_nt(vnew, dsn)
            dgl = _head_total(dsn * st) * sq["egl"]
            dwk = -_mm_nt(dvnew, st)
            drv = _unstack_heads(_mm_tn(t, dvnew) * bd)
            drk = _unstack_heads(_mm_tn(t, dwk) * bd)
            dnm = jnp.where(strict, -(_mm_nt(drv, _stack_heads(u)) + _mm_nt(drk, _stack_heads(wk))), 0.0)
            dbeta = gsum(dnm * kk * dec)
            dkk = dnm * beta * dec
            ddec = dnm * beta * kk + dqk * qkr
            mq = dqk * dec
            dq = _mm(mq, kbd) + dqd * e
            dk = (_unstack_heads(_mm_tn(mq, q) * bd) + _mm(dkk, kbd) + _unstack_heads(_mm_tn(dkk, k) * bd)
                  + drk * (beta * e) + dkt * xt)
            rks = gsum(drk * k)
            dbeta = dbeta + gsum(drv * v) + rks * e
            de = rks * beta + gsum(dqd * q)
            dxt = gsum(dkt * k) * xt
            dgl = dgl + jnp.sum(dxt, axis=0, keepdims=True)
            dd = ddec * dec
            dgc = de * e - dxt + gsum(dd) - gsum(_head_transpose(dd)) + last_row * dgl
            per_chunk[(c, sl)] = dict(dq=dq * (HEAD ** -0.5), dk=dk, dv=drv * beta, dbeta=dbeta, dgc=dgc)
        for sl in range(n_slab):
            dstate[LANES * sl:LANES * (sl + 1), :] = dsns[sl]

        def block_of(name, suffix_sum=False):
            rows = []
            for c in range(gch):
                r = jnp.concatenate([per_chunk[(c, sl)][name] for sl in range(n_slab)], axis=1)
                rows.append(_rows_suffix_sum(r) if suffix_sum else r)
            return jnp.concatenate(rows, axis=0)

        dg = block_of("dgc", suffix_sum=True)
        dal_ref[...] += jnp.sum(dg * fr["g"], axis=0, keepdims=True)
        da = dg * fr["neg_a"] * _sigmoid(fr["sp_in"])
        ddt_ref[...] += jnp.sum(da, axis=0, keepdims=True)
        beta_all = fr["beta"]
        db = block_of("dbeta") * beta_all * (1.0 - beta_all)
        lane = lax.broadcasted_iota(jnp.int32, (CHUNK * gch, LANES), 1)
        dab = jnp.zeros((CHUNK * gch, LANES), F32)
        for hd in range(GDN_HEADS):
            dab = jnp.where(lane == hd, da[:, HEAD * hd:HEAD * hd + 1], dab)
            dab = jnp.where(lane == GDN_HEADS + hd, db[:, HEAD * hd:HEAD * hd + 1], dab)
        dab_ref[...] = dab.astype(BF16)
        dqn = block_of("dq")
        dkn = block_of("dk")
        dq_raw = fr["rq"] * (dqn - fr["qn"] * _group_sum(dqn * fr["qn"]))
        dk_raw = fr["rk"] * (dkn - fr["kn"] * _group_sum(dkn * fr["kn"]))
        dy = jnp.concatenate([dq_raw, dk_raw, block_of("dv")], axis=1) * fr["dsil"]
        nxt = dy_next[...]
        x = fr["x"]
        dx = cw_ref[3:4, :] * dy
        dcw_ref[3:4, :] += jnp.sum(dy * x, axis=0, keepdims=True)
        for j in (1, 2, 3):
            dx = dx + cw_ref[3 - j:4 - j, :] * _shift_rows_up(dy, nxt, j)
            dcw_ref[3 - j:4 - j, :] += jnp.sum(dy * _shift_rows(x, hprev, j), axis=0, keepdims=True)
        dy_next[...] = dy
        dhx_ref[:, 0:GDN_QKV] = dx.astype(BF16)
        dhx_ref[:, 3 * w:4 * w] = dz.astype(BF16)

    cst = lambda shape: pl.BlockSpec(shape, lambda i: (0, 0))
    row = lambda width: pl.BlockSpec((CHUNK * gch, width), lambda i: (rev(i), 0))
    buf = lambda width: pltpu.VMEM((CHUNK * gch, width), F32)
    f_in, f_out, f_scr, _ = fused if fused is not None else ([], [], [], None)
    outs = pl.pallas_call(
        _fuse(body, 10, 6, 2, n_blocks, fused), name="gdn_bwd" + ("_exchange" if fused is not None else ""), grid=(n_blocks,),
        in_specs=[row(GDN_IN), pl.BlockSpec((CHUNK * gch, GDN_QKV), lambda i: (jnp.maximum(rev(i) - 1, 0), 0)),
                  cst((4, GDN_QKV)), cst((1, w)), cst((1, w)), cst((1, w)), row(w), row(w),
                  pl.BlockSpec((w * gch, LANES), lambda i: (rev(i), 0)), row(w)] + [_ANY] * len(f_in),
        out_specs=[row(4 * w), row(LANES), cst((4, GDN_QKV)), cst((1, w)), cst((1, w)), cst((1, w))] + [_ANY] * len(f_out),
        out_shape=[jax.ShapeDtypeStruct((s, 4 * w), BF16), jax.ShapeDtypeStruct((s, LANES), BF16),
                   jax.ShapeDtypeStruct((4, GDN_QKV), F32),
                   jax.ShapeDtypeStruct((1, w), F32), jax.ShapeDtypeStruct((1, w), F32), jax.ShapeDtypeStruct((1, w), F32)]
        + list(f_out),
        scratch_shapes=[pltpu.VMEM((w, LANES), F32), buf(GDN_QKV)] + list(f_scr),
        compiler_params=_params("arbitrary"),
    )(hx, hx, conv_w, a_log_e, dt_bias_e, norm_g_e, opre, tmat, states, dout, *f_in)
    return tuple(outs[:6]) + ((list(outs[6:]),) if fused is not None else ())


_MESH = pl.DeviceIdType.MESH


def all_gather8(x, name):
    m, n = x.shape

    def body(x_ref, out_ref, send_sems, recv_sems, local_sem):
        px, py, pc = lax.axis_index("x"), lax.axis_index("y"), lax.axis_index("c")
        me, sibling = (px, py, pc), (px, py, 1 - pc)
        chips = [(1 - px, py), (px, 1 - py), (1 - px, 1 - py)]

        def slot(dx, dy, dc):
            return out_ref.at[4 * dx + 2 * dy + dc]

        def copy(k, block, to, src=None):
            return pltpu.make_async_remote_copy(
                src_ref=slot(*block) if src is None else src, dst_ref=slot(*block),
                send_sem=send_sems.at[k], recv_sem=recv_sems.at[k], device_id=to, device_id_type=_MESH)

        mine = pltpu.make_async_copy(x_ref, slot(*me), local_sem)
        mine.start()
        first = [copy(0, me, sibling, src=x_ref)]
        first += [copy(1 + j, me, (*chip, pc), src=x_ref) for j, chip in enumerate(chips)]
        for cp in first:
            cp.start()
        passed = [copy(4 + j, (*chip, pc), sibling) for j, chip in enumerate(chips)]
        for j, chip in enumerate(chips):
            copy(1 + j, (*chip, pc), me).wait_recv()
            passed[j].start()
        copy(0, sibling, me).wait_recv()
        for j, chip in enumerate(chips):
            copy(4 + j, (*chip, 1 - pc), me).wait_recv()
        for cp in first + passed:
            cp.wait_send()
        mine.wait()

    return pl.pallas_call(
        body, name=name, out_shape=jax.ShapeDtypeStruct((N_DEV, m, n), x.dtype),
        in_specs=[_ANY], out_specs=_ANY,
        scratch_shapes=[pltpu.SemaphoreType.DMA((7,)), pltpu.SemaphoreType.DMA((7,)), pltpu.SemaphoreType.DMA],
    )(x)


def _weight_gather_steps(s0, s1, f0, f1, sems):
    n0 = len(s0)
    own_send, own_recv, ici_send, ici_recv, fwd_send, fwd_recv = sems
    px, py, pc = lax.axis_index("x"), lax.axis_index("y"), lax.axis_index("c")
    mine = 2 * px + py
    sibling = (px, py, 1 - pc)
    chips = [(1 - px, py), (px, 1 - py), (1 - px, 1 - py)]

    def copy(src, dst, sems_s, sems_r, k, to):
        return pltpu.make_async_remote_copy(src_ref=src, dst_ref=dst, send_sem=sems_s.at[k], recv_sem=sems_r.at[k],
                                            device_id=to, device_id_type=_MESH)

    def own_copies():
        return [copy(shards[i], full[i].at[mine], own_send, own_recv, base + i, sibling)
                for base, shards, full in ((0, s0, f0), (n0, s1, f1)) for i in range(len(shards))]

    def first_copies(my_shards, my_full):
        ici = [copy(my_shards[i], my_full[i].at[mine], ici_send, ici_recv, 3 * i + j, (cx, cy, pc))
               for i in range(len(my_shards)) for j, (cx, cy) in enumerate(chips)]
        return own_copies() + ici

    def begin(my_shards, my_full):
        for cp in first_copies(my_shards, my_full):
            cp.start()

    def end(my_shards, my_full, other_full):
        fwd = []
        for i in range(len(my_shards)):
            for j, (cx, cy) in enumerate(chips):
                slot = my_full[i].at[2 * cx + cy]
                copy(my_shards[i], slot, ici_send, ici_recv, 3 * i + j, (cx, cy, pc)).wait_recv()
                cp = copy(slot, slot, fwd_send, fwd_recv, 3 * i + j, sibling)
                cp.start()
                fwd.append(cp)
        for cp in own_copies():
            cp.wait_recv()
        for i in range(len(other_full)):
            for j, (cx, cy) in enumerate(chips):
                slot = other_full[i].at[2 * cx + cy]
                copy(slot, slot, fwd_send, fwd_recv, 3 * i + j, sibling).wait_recv()
        for cp in first_copies(my_shards, my_full) + fwd:
            cp.wait_send()

    def start():
        @pl.when(pc == 0)
        def _():
            begin(s0, f0)

        @pl.when(pc == 1)
        def _():
            begin(s1, f1)

    def finish():
        @pl.when(pc == 0)
        def _():
            end(s0, f0, f1)

        @pl.when(pc == 1)
        def _():
            end(s1, f1, f0)

    return start, finish


def _weight_gather_operands(shards0, shards1):
    both, most = len(shards0) + len(shards1), max(len(shards0), len(shards1))
    full = [jax.ShapeDtypeStruct((N_CHIPS,) + v.shape, v.dtype) for v in list(shards0) + list(shards1)]
    dma = pltpu.SemaphoreType.DMA
    return full, [dma((both,)), dma((both,)), dma((3 * most,)), dma((3 * most,)), dma((3 * most,)), dma((3 * most,))]


def _weight_gather_fused(shards0, shards1):
    n0, both = len(shards0), len(shards0) + len(shards1)
    full, sems = _weight_gather_operands(shards0, shards1)
    steps = lambda ins, outs, scr: _weight_gather_steps(ins[:n0], ins[n0:both], outs[:n0], outs[n0:both], scr)
    return list(shards0) + list(shards1), full, sems, steps


def gather_layer_weights(shards0, shards1, name):
    ins, full, sems, steps = _weight_gather_fused(shards0, shards1)
    both = len(ins)

    def body(*refs):
        start, finish = steps(refs[0:both], refs[both:2 * both], refs[2 * both:])
        start()
        finish()

    return pl.pallas_call(
        body, name=name, out_shape=full, in_specs=[_ANY] * both, out_specs=[_ANY] * both, scratch_shapes=sems,
    )(*ins)


def _piece_offsets(pieces):
    offs = [0]
    for r, _ in pieces:
        offs.append(offs[-1] + r)
    return offs


def _chip_exchange_steps(srcs, q_ref, sems, pieces, owner):
    send_sems, recv_sems = sems
    offs = _piece_offsets(pieces)
    px, py, pc = lax.axis_index("x"), lax.axis_index("y"), lax.axis_index("c")
    mine = 2 * px + py
    chips = [(1 - px, py), (px, 1 - py), (1 - px, 1 - py)]

    def copies():
        sends = []
        for i, (r, stride) in enumerate(pieces):
            dst = pl.ds(offs[i], r)
            for j, (cx, cy) in enumerate(chips):
                sends.append(pltpu.make_async_remote_copy(
                    src_ref=srcs[i].at[pl.ds((2 * cx + cy) * stride, r)], dst_ref=q_ref.at[mine, dst],
                    send_sem=send_sems.at[3 * i + j], recv_sem=recv_sems.at[3 * i + j], device_id=(cx, cy, pc),
                    device_id_type=_MESH))
        return sends

    def start():
        @pl.when(pc == owner)
        def _():
            for cp in copies():
                cp.start()

    def finish():
        @pl.when(pc == owner)
        def _():
            for i, (r, stride) in enumerate(pieces):
                dst = pl.ds(offs[i], r)
                for j, (cx, cy) in enumerate(chips):
                    pltpu.make_async_remote_copy(
                        src_ref=srcs[i].at[pl.ds(mine * stride, r)], dst_ref=q_ref.at[2 * cx + cy, dst],
                        send_sem=send_sems.at[3 * i + j], recv_sem=recv_sems.at[3 * i + j], device_id=(cx, cy, pc),
                        device_id_type=_MESH).wait_recv()
            for cp in copies():
                cp.wait_send()

    return start, finish


def _chip_exchange_operands(arrays, pieces):
    n = len(pieces)
    dma = pltpu.SemaphoreType.DMA
    q = jax.ShapeDtypeStruct((N_CHIPS, _piece_offsets(pieces)[-1], arrays[0].shape[1]), arrays[0].dtype)
    return q, [dma((3 * n,)), dma((3 * n,))]


def _own_share(arrays, pieces, chip):
    return jnp.concatenate([lax.dynamic_slice_in_dim(arr, chip * stride, r, axis=0) for arr, (r, stride) in zip(arrays, pieces)],
                           axis=0)


def chip_exchange(arrays, pieces, owner, name):
    n = len(pieces)

    def body(*refs):
        start, finish = _chip_exchange_steps(refs[0:n], refs[n], refs[n + 1:], pieces, owner)
        start()
        finish()

    q, sems = _chip_exchange_operands(arrays, pieces)
    return pl.pallas_call(body, name=name, out_shape=q, in_specs=[_ANY] * n, out_specs=_ANY, scratch_shapes=sems)(*arrays)


def sibling_send(arrays, to_core, name):
    n = len(arrays)

    def body(*refs):
        srcs, outs = refs[0:n], refs[n:2 * n]
        send_sems, recv_sems = refs[2 * n:]
        px, py, pc = lax.axis_index("x"), lax.axis_index("y"), lax.axis_index("c")
        cps = [pltpu.make_async_remote_copy(
            src_ref=srcs[i], dst_ref=outs[i], send_sem=send_sems.at[i], recv_sem=recv_sems.at[i],
            device_id=(px, py, to_core), device_id_type=_MESH) for i in range(n)]

        @pl.when(pc != to_core)
        def _():
            for cp in cps:
                cp.start()
            for cp in cps:
                cp.wait_send()

        @pl.when(pc == to_core)
        def _():
            for cp in cps:
                cp.wait_recv()

    return pl.pallas_call(
        body, name=name, out_shape=[jax.ShapeDtypeStruct(v.shape, v.dtype) for v in arrays],
        in_specs=[_ANY] * n, out_specs=[_ANY] * n,
        scratch_shapes=[pltpu.SemaphoreType.DMA((n,)), pltpu.SemaphoreType.DMA((n,))],
    )(*arrays)


def _fuse(body, n_in, n_out, n_scratch, n_steps, fused):
    if fused is None:
        return body
    f_in, f_out, f_scr, steps = fused
    a, b, c = len(f_in), len(f_out), len(f_scr)

    def wrapped(*refs):
        ins, rest = refs[:n_in + a], refs[n_in + a:]
        outs, scr = rest[:n_out + b], rest[n_out + b:]
        start, finish = steps(ins[n_in:], outs[n_out:], scr[n_scratch:])
        step = pl.program_id(0)

        @pl.when(step == 0)
        def _():
            start()

        body(*ins[:n_in], *outs[:n_out], *scr[:n_scratch])

        @pl.when(step == n_steps - 1)
        def _():
            finish()

    return wrapped


def sibling_swap(x, name):
    def body(x_ref, out_ref, send_sem, recv_sem):
        px, py, pc = lax.axis_index("x"), lax.axis_index("y"), lax.axis_index("c")
        cp = pltpu.make_async_remote_copy(
            src_ref=x_ref, dst_ref=out_ref, send_sem=send_sem, recv_sem=recv_sem,
            device_id=(px, py, 1 - pc), device_id_type=_MESH)
        cp.start()
        cp.wait()

    return pl.pallas_call(
        body, name=name, out_shape=jax.ShapeDtypeStruct(x.shape, x.dtype), in_specs=[_ANY], out_specs=_ANY,
        scratch_shapes=[pltpu.SemaphoreType.DMA, pltpu.SemaphoreType.DMA],
    )(x)


ELT_TILE = 128


def _elt_rows(m):
    for t in (512, 256, ELT_TILE, 16, 8):
        if m % t == 0:
            return t
    return m


def pair_add(a, b, name):
    m, n = b.shape
    tm = _elt_rows(m)

    def body(a_ref, b_ref, o_ref):
        o_ref[...] = (a_ref[...].astype(F32) + b_ref[...].astype(F32)).astype(o_ref.dtype)

    return pl.pallas_call(
        body, name=name, grid=(m // tm,), in_specs=[_row_spec(tm, n)] * 2, out_specs=_row_spec(tm, n),
        out_shape=jax.ShapeDtypeStruct((m, n), b.dtype), compiler_params=_params("arbitrary"),
    )(a, b)


def sum_leading(q, name):
    kk, m, n = q.shape
    tm = _elt_rows(m)

    def body(q_ref, o_ref):
        acc = q_ref[0].astype(F32)
        for i in range(1, kk):
            acc = acc + q_ref[i].astype(F32)
        o_ref[...] = acc

    return pl.pallas_call(
        body, name=name, grid=(m // tm,), in_specs=[pl.BlockSpec((kk, tm, n), lambda i: (0, i, 0))],
        out_specs=_row_spec(tm, n), out_shape=jax.ShapeDtypeStruct((m, n), F32), compiler_params=_params("arbitrary"),
    )(q)


def sum_shares(q0, q1, own0, own1, name):
    _, m, n = q0.shape
    tm = _elt_rows(m)

    def body(q0_ref, q1_ref, o0_ref, o1_ref, out_ref):
        first = lax.axis_index("c") == 0
        mine = 2 * lax.axis_index("x") + lax.axis_index("y")
        own = jnp.where(first, o0_ref[...], o1_ref[...])
        acc = None
        for k in range(N_CHIPS):
            term = jnp.where(mine == k, own, jnp.where(first, q0_ref[k], q1_ref[k])).astype(F32)
            acc = term if acc is None else acc + term
        out_ref[...] = acc

    slots = pl.BlockSpec((N_CHIPS, tm, n), lambda i: (0, i, 0))
    return pl.pallas_call(
        body, name=name, grid=(m // tm,), in_specs=[slots, slots, _row_spec(tm, n), _row_spec(tm, n)],
        out_specs=_row_spec(tm, n), out_shape=jax.ShapeDtypeStruct((m, n), F32), compiler_params=_params("arbitrary"),
    )(q0, q1, own0, own1)


def adamw(w, g, m, v, name):
    rows, cols = w.shape
    tm = _elt_rows(rows)

    def body(w_ref, g_ref, m_ref, v_ref, d_ref, nm_ref, nv_ref):
        gv = g_ref[...]
        nm = ADAM_B1 * m_ref[...] + (1.0 - ADAM_B1) * gv
        nv = ADAM_B2 * v_ref[...] + (1.0 - ADAM_B2) * jnp.square(gv)
        nm_ref[...] = nm
        nv_ref[...] = nv
        m_hat = nm / (1.0 - ADAM_B1 ** ADAM_STEP)
        v_hat = nv / (1.0 - ADAM_B2 ** ADAM_STEP)
        d_ref[...] = -ADAM_LR * (m_hat / (jnp.sqrt(v_hat) + ADAM_EPS) + ADAM_WD * w_ref[...])

    spec = _row_spec(tm, cols)
    return pl.pallas_call(
        body, name=name, grid=(rows // tm,), in_specs=[spec] * 4, out_specs=[spec] * 3,
        out_shape=[jax.ShapeDtypeStruct((rows, cols), F32)] * 3, compiler_params=_params("arbitrary"),
    )(w, g, m, v)


def _block_diag_dense(w):
    g = w.shape[0]
    return jnp.einsum("gij,gh->gihj", w, jnp.eye(g, dtype=w.dtype)).reshape(g * w.shape[1], g * w.shape[2])


def _diag_blocks(m):
    return jnp.stack([m[HEAD * i:HEAD * (i + 1), HEAD * i:HEAD * (i + 1)] for i in range(LRU_BLOCKS)])


def _rep(v):
    return jnp.repeat(v, HEAD, axis=-1)


def _split_w_in(w_in):
    gdn0 = RET_IN + LRU_IN
    gdn1 = gdn0 + 4 * GDN_W
    w_r = w_in[:, 0:RET_IN]
    w_l = w_in[:, RET_IN:gdn0]
    w_g = jnp.concatenate([w_in[:, gdn0:gdn1], _rep(w_in[:, gdn1:gdn1 + GDN_HEADS]), _rep(w_in[:, gdn1 + GDN_HEADS:])], axis=1)
    w_ab = jnp.pad(w_in[:, gdn1:], ((0, 0), (0, LANES - 2 * GDN_HEADS)))
    return w_r, w_l, w_g, w_ab


WIN_SHARD = D_IN // N_CHIPS
WIN_STRIDE = 832
WIN_ROWS = 960
WIN_T_ROWS = WIN_STRIDE * (N_CHIPS - 1) + WIN_ROWS
AB_ROWS = 16

_GRAD_PIECES = (("ffn1_w_gate", 704, 704), ("ffn1_w_up", 704, 704), ("ffn1_w_down", 704, 704), ("w_in", WIN_ROWS, WIN_STRIDE),
                ("w_out", 256, 256), ("ffn2_w_gate", 704, 704), ("ffn2_w_up", 704, 704), ("ffn2_w_down", 704, 704),
                ("ple_w_gate", 256, 256), ("ple_w_proj", 64, 64))
_TRANSPOSED = ("ffn1_w_gate", "ffn1_w_up", "w_in", "ffn2_w_gate", "ffn2_w_up", "ple_w_proj")


def _local_step(x, p, pos, target, wt, mesh=None):
    row = lambda v: v[None, :]
    saved = []
    xb = x.astype(BF16)
    pieces = [(r, stride) for _, r, stride in _GRAD_PIECES]
    grad_names = [n for n, _, _ in _GRAD_PIECES]
    for i in range(DEPTH):
        ffn1 = (wt["ffn1_w_gate"][i], wt["ffn1_w_up"][i], wt["ffn1_w_down"][i])
        if mesh is not None and i == 0:
            half0, half1, make = mesh["rest0"]
            hg1, hu1, r1, x1, x1b, gathered = ffn_fwd(x, row(wt["ln_ffn1_g"][i]), row(wt["ln_ffn1_b"][i]), *ffn1,
                                                      fused=_weight_gather_fused(half0, half1))
            wt = {**wt, **{n: [w0, None] for n, w0 in make(gathered).items()}}
        else:
            hg1, hu1, r1, x1, x1b = ffn_fwd(x, row(wt["ln_ffn1_g"][i]), row(wt["ln_ffn1_b"][i]), *ffn1)
        w_r, w_l, w_g, w_ab = _split_w_in(wt["w_in"][i])
        lw = dict(
            wg1=ffn1[0], wu1=ffn1[1], wd1=ffn1[2], w_r=w_r, w_l=w_l, w_g=w_g, w_ab=w_ab,
            w_out=wt["w_out"][i], wg2=wt["ffn2_w_gate"][i], wu2=wt["ffn2_w_up"][i], wd2=wt["ffn2_w_down"][i],
            wpg=wt["ple_w_gate"][i], wpp=wt["ple_w_proj"][i],
            wa=_block_diag_dense(wt["lru_w_a"][i]), wx=_block_diag_dense(wt["lru_w_x"][i]),
            al=row(_rep(wt["gdn_a_log"][i])), dt=row(_rep(wt["gdn_dt_bias"][i])), ng=row(jnp.tile(wt["gdn_norm_g"][i], GDN_HEADS)))
        hr, hl, hgd = win_fwd(x1, w_r, w_l, w_g)
        o_r, opre_r, st_r = ret_fwd(hr, pos, row(wt["ret_norm_g"][i]))
        o_l, xc, hs = lru_fwd(hl, wt["lru_conv_w"][i], row(wt["lru_conv_b"][i]), lw["wa"], row(wt["lru_b_a"][i]), lw["wx"],
                              row(wt["lru_b_x"][i]), row(wt["lru_lambda"][i]))
        if mesh is not None and i == 0:
            half0, half1, make = mesh["layer1"]
            o_g, opre_g, tmat, st_g, gathered = gdn_fwd(hgd, wt["gdn_conv_w"][i], lw["al"], lw["dt"], lw["ng"],
                                                        fused=_weight_gather_fused(half0, half1))
            wt = {**wt, **{n: [wt[n][0], w1] for n, w1 in make(gathered).items()}}
        else:
            o_g, opre_g, tmat, st_g = gdn_fwd(hgd, wt["gdn_conv_w"][i], lw["al"], lw["dt"], lw["ng"])
        r2, x2, x2b, ocat = out_fwd(o_r, o_l, o_g, x1, lw["w_out"], row(wt["ln_mix_g"][i]), row(wt["ln_mix_b"][i]))
        hg2, hu2, r3, x3, x3b, pg, pp = ffn_fwd(x2, row(wt["ln_ffn2_g"][i]), row(wt["ln_ffn2_b"][i]), lw["wg2"], lw["wu2"],
                                                lw["wd2"], ple=(p[i], lw["wpg"], lw["wpp"]))
        saved.append(dict(lw=lw, x0=xb, hg1=hg1, hu1=hu1, r1=r1, x1=x1b, hr=hr, hl=hl, hgd=hgd, ocat=ocat, opre_r=opre_r,
                          st_r=st_r, xc=xc, hs=hs, opre_g=opre_g, tmat=tmat, st_g=st_g, r2=r2, x2=x2b, hg2=hg2, hu2=hu2,
                          r3=r3, pg=pg, pp=pp))
        x, xb = x3, x3b

    dx, loss = loss_and_grad(x, target)
    grads = [None] * DEPTH
    big = [None] * DEPTH
    pair_sums = [None] * DEPTH
    for i in reversed(range(DEPTH)):
        sv = saved[i]
        lw = sv["lw"]
        tag = f"_l{i}"
        dx2, act2, dhg2, dhu2, dy2, dg3, db3, dpg, dpp = ffn_bwd(
            dx, sv["r3"], sv["x2"], sv["hg2"], sv["hu2"], row(wt["ln_ffn2_g"][i]), lw["wg2"], lw["wu2"], lw["wd2"],
            ple=(sv["pg"], sv["pp"], lw["wpg"]))
        g, bg = {}, {}
        bg["ffn2_w_gate"] = wgrad(dhg2, sv["x2"], "wgrad_gate2" + tag)
        bg["ffn2_w_up"] = wgrad(dhu2, sv["x2"], "wgrad_up2" + tag)
        bg["ffn2_w_down"] = wgrad(act2, dy2, "wgrad_down2" + tag)
        bg["ple_w_gate"] = wgrad(sv["x2"], dpg, "wgrad_pgate" + tag)
        bg["ple_w_proj"] = wgrad(dpp, p[i], "wgrad_pproj" + tag).reshape(PLE_DIM, D_MODEL)
        g["ln_ffn2_g"], g["ln_ffn2_b"] = dg3[0], db3[0]
        dr2, dr2b, do_r, do_l, do_g, dg2, db2 = out_bwd(dx2, sv["r2"], row(wt["ln_mix_g"][i]), lw["w_out"])
        g["ln_mix_g"], g["ln_mix_b"] = dg2[0], db2[0]
        bg["w_out"] = wgrad(sv["ocat"], dr2b, "wgrad_out" + tag)
        dhr, dgn = ret_bwd(sv["hr"], pos, row(wt["ret_norm_g"][i]), sv["opre_r"], sv["st_r"], do_r)
        g["ret_norm_g"] = dgn[0]
        dhl, dcw, dcb, dwa, dba, dwx, dbx, dlam = lru_bwd(
            sv["hl"], wt["lru_conv_w"][i], row(wt["lru_conv_b"][i]), lw["wa"], row(wt["lru_b_a"][i]), lw["wx"],
            row(wt["lru_b_x"][i]), row(wt["lru_lambda"][i]), sv["xc"], sv["hs"], do_l)
        g["lru_conv_w"], g["lru_conv_b"] = dcw, dcb[0]
        g["lru_w_a"], g["lru_b_a"], g["lru_w_x"], g["lru_b_x"], g["lru_lambda"] = _diag_blocks(dwa), dba[0], _diag_blocks(dwx), dbx[0], dlam[0]
        if mesh is not None and i == 0:
            q_shape, sems = _chip_exchange_operands(pair_sums[1], pieces)
            steps = lambda ins, outs, scr: _chip_exchange_steps(ins, outs[0], scr, pieces, 1)
            dhq, dab, dgcw, dal, ddt, dng, arrived = gdn_bwd(
                sv["hgd"], wt["gdn_conv_w"][i], lw["al"], lw["dt"], lw["ng"], sv["opre_g"], sv["tmat"], sv["st_g"], do_g,
                fused=(pair_sums[1], [q_shape], sems, steps))
            big[1] = arrived[0]
        else:
            dhq, dab, dgcw, dal, ddt, dng = gdn_bwd(sv["hgd"], wt["gdn_conv_w"][i], lw["al"], lw["dt"], lw["ng"], sv["opre_g"],
                                                    sv["tmat"], sv["st_g"], do_g)
        g["gdn_conv_w"] = dgcw
        g["gdn_a_log"], g["gdn_dt_bias"] = dal[0, ::HEAD], ddt[0, ::HEAD]
        g["gdn_norm_g"] = dng[0].reshape(GDN_HEADS, HEAD).sum(0)
        dx1 = win_bwd(dr2, dhr, dhl, dhq, dab, lw["w_r"], lw["w_l"], lw["w_g"], lw["w_ab"])
        used = RET_IN + LRU_IN + 4 * GDN_W + AB_ROWS
        bg["w_in"] = jnp.concatenate(
            [wgrad(dhr, sv["x1"], "wgrad_in_r" + tag), wgrad(dhl, sv["x1"], "wgrad_in_l" + tag),
             wgrad(dhq, sv["x1"], "wgrad_in_q" + tag), wgrad(dab, sv["x1"], "wgrad_in_ab" + tag)[0:AB_ROWS],
             jnp.zeros((WIN_T_ROWS - used, D_MODEL), BF16)], axis=0)
        dx, act1, dhg1, dhu1, dy1, dg1, db1 = ffn_bwd(dx1, sv["r1"], sv["x0"], sv["hg1"], sv["hu1"], row(wt["ln_ffn1_g"][i]),
                                                      lw["wg1"], lw["wu1"], lw["wd1"])
        bg["ffn1_w_gate"] = wgrad(dhg1, sv["x0"], "wgrad_gate1" + tag)
        bg["ffn1_w_up"] = wgrad(dhu1, sv["x0"], "wgrad_up1" + tag)
        bg["ffn1_w_down"] = wgrad(act1, dy1, "wgrad_down1" + tag)
        g["ln_ffn1_g"], g["ln_ffn1_b"] = dg1[0], db1[0]
        grads[i] = g
        if mesh is None:
            big[i] = bg
        else:
            mine = [bg[n] for n in grad_names]
            theirs = sibling_send(mine, i, f"reduce_pair_send_l{i}")
            pair_sums[i] = [pair_add(u, v, f"reduce_pair_add_l{i}_" + n) for n, u, v in zip(grad_names, mine, theirs)]
            if i == 0:
                big[0] = chip_exchange(pair_sums[0], pieces, 0, "reduce_chip_exchange_l0")
    if mesh is not None:
        chip = 2 * lax.axis_index("x") + lax.axis_index("y")
        big = (big, [_own_share(pair_sums[layer], pieces, chip) for layer in range(DEPTH)])
    return loss, dx, {k: jnp.stack([grads[i][k] for i in range(DEPTH)]) for k in grads[0]}, big


def _natural_grad(name, rows):
    if name == "ple_w_proj":
        return rows.reshape(-1, PLE_DIM).T
    return rows.T if name in _TRANSPOSED else rows


_SPLIT = dict(ffn1_w_gate=2, ffn1_w_up=2, ffn1_w_down=1, w_in=2, w_out=1, ffn2_w_gate=2, ffn2_w_up=2, ffn2_w_down=1,
              ple_w_gate=1, ple_w_proj=2)
_CONV = ("lru_conv_w", "gdn_conv_w")
_WHOLE = ("ln_ffn1_g", "ln_ffn1_b", "ret_norm_g", "lru_conv_b", "lru_w_a", "lru_b_a", "lru_w_x", "lru_b_x", "lru_lambda",
          "gdn_a_log", "gdn_dt_bias", "gdn_norm_g", "ln_mix_g", "ln_mix_b", "ln_ffn2_g", "ln_ffn2_b")
_WEIGHTS = ("ln_ffn1_g", "ln_ffn1_b", "ffn1_w_gate", "ffn1_w_up", "ffn1_w_down", "w_in", "ret_norm_g", "lru_conv_w", "lru_conv_b",
            "lru_w_a", "lru_b_a", "lru_w_x", "lru_b_x", "lru_lambda", "gdn_conv_w", "gdn_a_log", "gdn_dt_bias", "gdn_norm_g",
            "w_out", "ln_mix_g", "ln_mix_b", "ffn2_w_gate", "ffn2_w_up", "ffn2_w_down", "ple_w_gate", "ple_w_proj",
            "ln_ffn2_g", "ln_ffn2_b")
_INPUTS = ("x", "p", "positions") + _WEIGHTS + ("loss_target",) + tuple("m_" + n for n in _WEIGHTS) + tuple("v_" + n for n in _WEIGHTS)

BIG_COLS = 1024
SMALL_COLS = LANES
SMALL_ROWS_MULT = 8


def _pack(arrays, dtype, cols, rows_mult):
    flat = jnp.concatenate([a.reshape(-1).astype(dtype) for a in arrays])
    rows = -(-flat.shape[0] // cols)
    rows = -(-rows // rows_mult) * rows_mult
    return jnp.pad(flat, (0, rows * cols - flat.shape[0])).reshape(rows, cols)


def _unpack(packed, shapes):
    flat = packed.reshape(-1)
    out, off = [], 0
    for shp in shapes:
        size = int(np.prod(shp))
        out.append(flat[off:off + size].reshape(shp))
        off += size
    return out


def _as2d(a):
    return a.reshape(-1, a.shape[-1])


def kernel(x, p, positions, ln_ffn1_g, ln_ffn1_b, ffn1_w_gate, ffn1_w_up, ffn1_w_down, w_in, ret_norm_g, lru_conv_w, lru_conv_b, lru_w_a, lru_b_a, lru_w_x, lru_b_x, lru_lambda, gdn_conv_w, gdn_a_log, gdn_dt_bias, gdn_norm_g, w_out, ln_mix_g, ln_mix_b, ffn2_w_gate, ffn2_w_up, ffn2_w_down, ple_w_gate, ple_w_proj, ln_ffn2_g, ln_ffn2_b, loss_target, m_ln_ffn1_g, m_ln_ffn1_b, m_ffn1_w_gate, m_ffn1_w_up, m_ffn1_w_down, m_w_in, m_ret_norm_g, m_lru_conv_w, m_lru_conv_b, m_lru_w_a, m_lru_b_a, m_lru_w_x, m_lru_b_x, m_lru_lambda, m_gdn_conv_w, m_gdn_a_log, m_gdn_dt_bias, m_gdn_norm_g, m_w_out, m_ln_mix_g, m_ln_mix_b, m_ffn2_w_gate, m_ffn2_w_up, m_ffn2_w_down, m_ple_w_gate, m_ple_w_proj, m_ln_ffn2_g, m_ln_ffn2_b, v_ln_ffn1_g, v_ln_ffn1_b, v_ffn1_w_gate, v_ffn1_w_up, v_ffn1_w_down, v_w_in, v_ret_norm_g, v_lru_conv_w, v_lru_conv_b, v_lru_w_a, v_lru_b_a, v_lru_w_x, v_lru_b_x, v_lru_lambda, v_gdn_conv_w, v_gdn_a_log, v_gdn_dt_bias, v_gdn_norm_g, v_w_out, v_ln_mix_g, v_ln_mix_b, v_ffn2_w_gate, v_ffn2_w_up, v_ffn2_w_down, v_ple_w_gate, v_ple_w_proj, v_ln_ffn2_g, v_ln_ffn2_b):
    a = dict(zip(_INPUTS, (x, p, positions, ln_ffn1_g, ln_ffn1_b, ffn1_w_gate, ffn1_w_up, ffn1_w_down, w_in, ret_norm_g, lru_conv_w, lru_conv_b, lru_w_a, lru_b_a, lru_w_x, lru_b_x, lru_lambda, gdn_conv_w, gdn_a_log, gdn_dt_bias, gdn_norm_g, w_out, ln_mix_g, ln_mix_b, ffn2_w_gate, ffn2_w_up, ffn2_w_down, ple_w_gate, ple_w_proj, ln_ffn2_g, ln_ffn2_b, loss_target, m_ln_ffn1_g, m_ln_ffn1_b, m_ffn1_w_gate, m_ffn1_w_up, m_ffn1_w_down, m_w_in, m_ret_norm_g, m_lru_conv_w, m_lru_conv_b, m_lru_w_a, m_lru_b_a, m_lru_w_x, m_lru_b_x, m_lru_lambda, m_gdn_conv_w, m_gdn_a_log, m_gdn_dt_bias, m_gdn_norm_g, m_w_out, m_ln_mix_g, m_ln_mix_b, m_ffn2_w_gate, m_ffn2_w_up, m_ffn2_w_down, m_ple_w_gate, m_ple_w_proj, m_ln_ffn2_g, m_ln_ffn2_b, v_ln_ffn1_g, v_ln_ffn1_b, v_ffn1_w_gate, v_ffn1_w_up, v_ffn1_w_down, v_w_in, v_ret_norm_g, v_lru_conv_w, v_lru_conv_b, v_lru_w_a, v_lru_b_a, v_lru_w_x, v_lru_b_x, v_lru_lambda, v_gdn_conv_w, v_gdn_a_log, v_gdn_dt_bias, v_gdn_norm_g, v_w_out, v_ln_mix_g, v_ln_mix_b, v_ffn2_w_gate, v_ffn2_w_up, v_ffn2_w_down, v_ple_w_gate, v_ple_w_proj, v_ln_ffn2_g, v_ln_ffn2_b)))
    assert len(a) == len(_INPUTS)
    core = lax.axis_index("c")
    chip = 2 * lax.axis_index("x") + lax.axis_index("y")
    big = list(_SPLIT)

    def group(layer, names, n_first):
        shards = [a[n][layer].astype(BF16) for n in names]

        def make(gathered):
            return {n: jnp.concatenate([gathered[i][k] for k in range(N_CHIPS)], axis=_SPLIT[n] - 1) for i, n in enumerate(names)}

        return shards[:n_first], shards[n_first:], make

    ffn1_0, ffn1_1, make_ffn1 = group(0, big[:3], 2)
    wt = {n: [w0, None] for n, w0 in make_ffn1(gather_layer_weights(ffn1_0, ffn1_1, "gather_weights_ffn1_l0")).items()}
    conv_g = all_gather8(_pack([a[n] for n in _CONV], F32, SMALL_COLS, SMALL_ROWS_MULT), "gather_conv_weights")[0::2]
    conv_g = conv_g.reshape(N_CHIPS, -1)
    off = 0
    for n in _CONV:
        shp = a[n].shape
        size = int(np.prod(shp))
        parts = conv_g[:, off:off + size].reshape((N_CHIPS,) + shp)
        wt[n] = jnp.concatenate([parts[k] for k in range(N_CHIPS)], axis=2)
        off += size
    for n in _WHOLE:
        wt[n] = a[n]

    seq = a["x"].shape[1]
    mesh = dict(rest0=group(0, big[3:], 4), layer1=group(1, big, len(big) // 2))
    loss_part, dx, grads, (arrived, own) = _local_step(a["x"][0], a["p"][:, 0], a["positions"].reshape(seq, 1),
                                                       a["loss_target"][0], wt, mesh=mesh)
    loss = lax.psum(loss_part[0, 0], ("x", "y", "c"))

    my_layer_sum = sum_shares(arrived[0], arrived[1], own[0], own[1], "reduce_chip_sum")
    other_layer_sum = sibling_swap(my_layer_sum, "reduce_pair_share")
    reduced = [jnp.where(core == layer, my_layer_sum, other_layer_sum) for layer in range(DEPTH)]
    big_grads = {}
    off = 0
    for n, r, _ in _GRAD_PIECES:
        per_layer = []
        for layer in range(DEPTH):
            rows = reduced[layer][off:off + r]
            if n == "w_in":
                rows = lax.dynamic_slice_in_dim(rows, chip * (WIN_SHARD - WIN_STRIDE), WIN_SHARD, axis=0)
            per_layer.append(_natural_grad(n, rows))
        big_grads[n] = jnp.stack(per_layer)
        off += r

    small_names = list(_WHOLE) + list(_CONV)
    small_local = _pack([grads[n] for n in small_names], F32, SMALL_COLS, SMALL_ROWS_MULT)
    small_sum = sum_leading(all_gather8(small_local, "gather_small_grads"), "sum_small_grads")
    small_grads = dict(zip(small_names, _unpack(small_sum, [grads[n].shape for n in small_names])))
    for n in _CONV:
        width = a[n].shape[2]
        small_grads[n] = lax.dynamic_slice_in_dim(small_grads[n], chip * width, width, axis=2)

    new = {}
    for n in big:
        d, nm, nv = adamw(_as2d(a[n]), _as2d(big_grads[n]), _as2d(a["m_" + n]), _as2d(a["v_" + n]), "adamw_" + n)
        new[n] = tuple(t.reshape(a[n].shape) for t in (d, nm, nv))
    pk = lambda prefix: _pack([a[prefix + n] for n in small_names], F32, SMALL_COLS, SMALL_ROWS_MULT)
    pg = _pack([small_grads[n] for n in small_names], F32, SMALL_COLS, SMALL_ROWS_MULT)
    outs = adamw(pk(""), pg, pk("m_"), pk("v_"), "adamw_small")
    shapes = [a[n].shape for n in small_names]
    for n, d, nm, nv in zip(small_names, *[_unpack(o, shapes) for o in outs]):
        new[n] = (d, nm, nv)
    all_grads = {**big_grads, **small_grads}
    return (loss, dx[None], *[all_grads[n] for n in _WEIGHTS], *[new[n][0] for n in _WEIGHTS],
            *[new[n][1] for n in _WEIGHTS], *[new[n][2] for n in _WEIGHTS])
```

```python
import functools
import math

import numpy as np
import jax
import jax.numpy as jnp
from jax import lax
from jax.experimental import pallas as pl
from jax.experimental.pallas import tpu as pltpu

F32 = jnp.float32
BF16 = jnp.bfloat16

D_MODEL = 1024
D_FF = 2816
PLE_DIM = 256
DEPTH = 2
CHUNK = 64
RET_HEADS = 4
RET_W = 256
LRU_W = 384
LRU_BLOCKS = 6
GDN_HEADS = 6
GDN_W = 384
HEAD = 64
D_IN = 3340
RET_IN = 4 * RET_W
LRU_IN = 2 * LRU_W
GDN_IN = 6 * GDN_W
ROPE_THETA = 10000.0
ALPHA = (2 * DEPTH) ** 0.25
LN_EPS = 1e-5
LRU_C = 8.0
N_CHIPS = 4
N_DEV = 8

ADAM_LR = 0.001
ADAM_B1 = 0.9
ADAM_B2 = 0.999
ADAM_EPS = 1e-08
ADAM_WD = 0.01
ADAM_STEP = 10

LANES = 128
VMEM_LIMIT = 56 * 1024 * 1024
ROW_TILE = 256
ROW_TILE_BWD = 512
SCAN_TILE = 256


def _params(*sem):
    return pltpu.CompilerParams(dimension_semantics=sem, vmem_limit_bytes=VMEM_LIMIT)


def _operand(a):
    return a.astype(BF16)


def _mm(a, b):
    return jnp.dot(_operand(a), _operand(b), preferred_element_type=F32)


def _mm_nt(a, b):
    return lax.dot_general(_operand(a), _operand(b), (((1,), (1,)), ((), ())), preferred_element_type=F32)


def _mm_tn(a, b):
    return lax.dot_general(_operand(a), _operand(b), (((0,), (0,)), ((), ())), preferred_element_type=F32)


def _split(a):
    hi = a.astype(BF16)
    lo = (a - hi.astype(F32)).astype(BF16)
    return hi, lo


def _mm3(a, b):
    ah, al = _split(a)
    bh, bl = _split(b)
    return _mm(ah, bh) + (_mm(ah, bl) + _mm(al, bh))


def _sigmoid(x):
    return jax.nn.sigmoid(x)


def _log1p(u):
    w = 1.0 + u
    return jnp.where(w == 1.0, u, jnp.log(w) * (u / jnp.where(w == 1.0, 1.0, w - 1.0)))


def _expm1(y):
    u = jnp.exp(y)
    um1 = u - 1.0
    safe = jnp.where((u == 1.0) | (um1 == -1.0), 1.0, jnp.log(jnp.where(u == 0.0, 1.0, u)))
    return jnp.where(u == 1.0, y, jnp.where(um1 == -1.0, -1.0, um1 * (y / safe)))


def _softplus(x):
    return jnp.maximum(x, 0.0) + _log1p(jnp.exp(-jnp.abs(x)))


_GELU_C = math.sqrt(2.0 / math.pi)


def _gelu(x):
    return 0.5 * x * (1.0 + jnp.tanh(_GELU_C * (x + 0.044715 * (x * x * x))))


def _gelu_grad(x):
    t = jnp.tanh(_GELU_C * (x + 0.044715 * (x * x * x)))
    return 0.5 * (1.0 + t) + 0.5 * x * (1.0 - t * t) * (_GELU_C * (1.0 + 3.0 * 0.044715 * (x * x)))


def _silu_and_grad(x):
    s = _sigmoid(x)
    return x * s, s * (1.0 + x * (1.0 - s))


def _group_sum_slab(x):
    lane = lax.broadcasted_iota(jnp.int32, x.shape, 1)
    low = jnp.sum(x[:, 0:LANES // 2], axis=1, keepdims=True)
    high = jnp.sum(x[:, LANES // 2:], axis=1, keepdims=True)
    return jnp.where(lane < LANES // 2, low, high)


def _group_sum(x):
    n = x.shape[1] // LANES
    if n == 1:
        return _group_sum_slab(x)
    return jnp.concatenate([_group_sum_slab(x[:, LANES * i:LANES * (i + 1)]) for i in range(n)], axis=1)


def _rows_prefix_sum(x):
    n = x.shape[0]
    row = lax.broadcasted_iota(jnp.int32, x.shape, 0)
    d = 1
    while d < n:
        x = x + jnp.where(row >= d, pltpu.roll(x, d, 0), 0.0)
        d *= 2
    return x


def _rows_suffix_sum(x):
    n = x.shape[0]
    row = lax.broadcasted_iota(jnp.int32, x.shape, 0)
    d = 1
    while d < n:
        x = x + jnp.where(row < n - d, pltpu.roll(x, n - d, 0), 0.0)
        d *= 2
    return x


def _shift_rows(cur, prev, j):
    row = lax.broadcasted_iota(jnp.int32, cur.shape, 0)
    return jnp.where(row < j, pltpu.roll(prev, j, 0), pltpu.roll(cur, j, 0))


def _shift_rows_up(cur, nxt, j):
    n = cur.shape[0]
    row = lax.broadcasted_iota(jnp.int32, cur.shape, 0)
    return jnp.where(row < n - j, pltpu.roll(cur, n - j, 0), pltpu.roll(nxt, n - j, 0))


def _layer_norm_stats(r):
    mu = jnp.mean(r, axis=-1, keepdims=True)
    d = r - mu
    var = jnp.mean(d * d, axis=-1, keepdims=True)
    rstd = lax.rsqrt(var + LN_EPS)
    return d * rstd, rstd


def _load_resident(step, pairs, sems):
    @pl.when(step == 0)
    def _():
        cps = [pltpu.make_async_copy(h, v, sems.at[i]) for i, (h, v) in enumerate(pairs)]
        for c in cps:
            c.start()
        for c in cps:
            c.wait()


def _row_spec(tile, width):
    return pl.BlockSpec((tile, width), lambda i: (i, 0))


def _full_spec(shape):
    nd = len(shape)
    return pl.BlockSpec(shape, lambda i: (0,) * nd)


_ANY = pl.BlockSpec(memory_space=pl.ANY)


def ffn_fwd(x, ln_g, ln_b, w_gate, w_up, w_down, ple=None, fused=None):
    s = x.shape[0]
    tm = ROW_TILE
    with_ple = ple is not None
    weights = [w_gate, w_up, w_down] + ([ple[1], ple[2]] if with_ple else [])

    def body(*refs):
        it = iter(refs)
        x_ref, g_ref, b_ref = next(it), next(it), next(it)
        p_ref = next(it) if with_ple else None
        w_hbm = [next(it) for _ in weights]
        hg_ref, hu_ref, r_ref, xn_ref, xnb_ref = next(it), next(it), next(it), next(it), next(it)
        pg_ref, pp_ref = (next(it), next(it)) if with_ple else (None, None)
        w_vm = [next(it) for _ in weights]
        sems = next(it)
        _load_resident(pl.program_id(0), list(zip(w_hbm, w_vm)), sems)
        xv = x_ref[...]
        xb = xv.astype(BF16)
        hg = _mm(xb, w_vm[0][...])
        hu = _mm(xb, w_vm[1][...])
        hg_ref[...] = hg
        hu_ref[...] = hu
        act = (hg * _sigmoid(hg)) * hu
        r = ALPHA * xv + 0.5 * _mm(act.astype(BF16), w_vm[2][...])
        if with_ple:
            pg = _mm(xb, w_vm[3][...])
            pp = _mm(p_ref[...].astype(BF16), w_vm[4][...])
            pg_ref[...] = pg
            pp_ref[...] = pp
            r = r + _sigmoid(pg) * pp
        r_ref[...] = r
        xhat, _ = _layer_norm_stats(r)
        xn = xhat * g_ref[...] + b_ref[...]
        xn_ref[...] = xn
        xnb_ref[...] = xn.astype(BF16)

    d, f = D_MODEL, D_FF
    in_specs = [_row_spec(tm, d), _full_spec((1, d)), _full_spec((1, d))]
    args = [x, ln_g, ln_b]
    if with_ple:
        in_specs.append(_row_spec(tm, PLE_DIM))
        args.append(ple[0])
    in_specs += [_ANY] * len(weights)
    args += weights
    out_shape = [jax.ShapeDtypeStruct((s, f), F32), jax.ShapeDtypeStruct((s, f), F32),
                 jax.ShapeDtypeStruct((s, d), F32), jax.ShapeDtypeStruct((s, d), F32), jax.ShapeDtypeStruct((s, d), BF16)]
    out_specs = [_row_spec(tm, f), _row_spec(tm, f), _row_spec(tm, d), _row_spec(tm, d), _row_spec(tm, d)]
    if with_ple:
        out_shape += [jax.ShapeDtypeStruct((s, d), F32)] * 2
        out_specs += [_row_spec(tm, d)] * 2
    scratch = [pltpu.VMEM(w.shape, w.dtype) for w in weights] + [pltpu.SemaphoreType.DMA((len(weights),))]
    f_in, f_out, f_scr, _ = fused if fused is not None else ([], [], [], None)
    n_out = len(out_shape)
    outs = pl.pallas_call(
        _fuse(body, len(args), n_out, len(scratch), s // tm, fused),
        name=("ffn_fwd_ple" if with_ple else "ffn_fwd") + ("_gather" if fused is not None else ""), grid=(s // tm,),
        in_specs=in_specs + [_ANY] * len(f_in), out_specs=out_specs + [_ANY] * len(f_out),
        out_shape=out_shape + list(f_out), scratch_shapes=scratch + list(f_scr), compiler_params=_params("arbitrary"),
    )(*args, *f_in)
    return tuple(outs[:n_out]) + ((list(outs[n_out:]),) if fused is not None else ())


def ffn_bwd(dxn, r, x, hg, hu, ln_g, w_gate, w_up, w_down, ple=None):
    s = x.shape[0]
    with_ple = ple is not None
    d, f = D_MODEL, D_FF
    suffix = "_ple" if with_ple else ""

    tm = ROW_TILE

    def body_a(*refs):
        it = iter(refs)
        dxn_ref, r_ref, hg_ref, hu_ref, g_ref = (next(it) for _ in range(5))
        pg_ref, pp_ref = (next(it), next(it)) if with_ple else (None, None)
        wd_hbm = next(it)
        dr_ref, act_ref, dhg_ref, dhu_ref, dy_ref, dg_ref, db_ref = (next(it) for _ in range(7))
        dpg_ref, dpp_ref = (next(it), next(it)) if with_ple else (None, None)
        wd_vm, sems = next(it), next(it)
        step = pl.program_id(0)
        _load_resident(step, [(wd_hbm, wd_vm)], sems)

        @pl.when(step == 0)
        def _():
            dg_ref[...] = jnp.zeros_like(dg_ref)
            db_ref[...] = jnp.zeros_like(db_ref)

        dxn_v = dxn_ref[...]
        xhat, rstd = _layer_norm_stats(r_ref[...])
        dg_ref[...] += jnp.sum(dxn_v * xhat, axis=0, keepdims=True)
        db_ref[...] += jnp.sum(dxn_v, axis=0, keepdims=True)
        dyh = dxn_v * g_ref[...]
        dr = rstd * (dyh - jnp.mean(dyh, axis=-1, keepdims=True) - xhat * jnp.mean(dyh * xhat, axis=-1, keepdims=True))
        dr_ref[...] = dr
        dy = (0.5 * dr).astype(BF16)
        dy_ref[...] = dy
        da = _mm_nt(dy, wd_vm[...])
        hg_v = hg_ref[...]
        hu_v = hu_ref[...]
        sil, dsil = _silu_and_grad(hg_v)
        act_ref[...] = (sil * hu_v).astype(BF16)
        dhu_ref[...] = (da * sil).astype(BF16)
        dhg_ref[...] = (da * hu_v * dsil).astype(BF16)
        if with_ple:
            sp = _sigmoid(pg_ref[...])
            dpp_ref[...] = (dr * sp).astype(BF16)
            dpg_ref[...] = (dr * pp_ref[...] * sp * (1.0 - sp)).astype(BF16)

    in_specs = [_row_spec(tm, d), _row_spec(tm, d), _row_spec(tm, f), _row_spec(tm, f), _full_spec((1, d))]
    args = [dxn, r, hg, hu, ln_g]
    if with_ple:
        in_specs += [_row_spec(tm, d), _row_spec(tm, d)]
        args += [ple[0], ple[1]]
    out_shape = [jax.ShapeDtypeStruct((s, d), F32), jax.ShapeDtypeStruct((s, f), BF16), jax.ShapeDtypeStruct((s, f), BF16),
                 jax.ShapeDtypeStruct((s, f), BF16), jax.ShapeDtypeStruct((s, d), BF16),
                 jax.ShapeDtypeStruct((1, d), F32), jax.ShapeDtypeStruct((1, d), F32)]
    out_specs = [_row_spec(tm, d), _row_spec(tm, f), _row_spec(tm, f), _row_spec(tm, f), _row_spec(tm, d),
                 _full_spec((1, d)), _full_spec((1, d))]
    if with_ple:
        out_shape += [jax.ShapeDtypeStruct((s, d), BF16)] * 2
        out_specs += [_row_spec(tm, d)] * 2
    first = pl.pallas_call(
        body_a, name="ffn_bwd_hidden" + suffix, grid=(s // tm,), in_specs=in_specs + [_ANY], out_specs=out_specs,
        out_shape=out_shape, scratch_shapes=[pltpu.VMEM(w_down.shape, w_down.dtype), pltpu.SemaphoreType.DMA((1,))],
        compiler_params=_params("arbitrary"),
    )(*args, w_down)
    dr, act, dhg, dhu, dy, dg, db = first[:7]

    tb = min(ROW_TILE_BWD, s)
    weights = [w_gate, w_up] + ([ple[2]] if with_ple else [])

    def body_b(*refs):
        it = iter(refs)
        dr_ref, dhg_ref, dhu_ref = next(it), next(it), next(it)
        dpg_ref = next(it) if with_ple else None
        w_hbm = [next(it) for _ in weights]
        dx_ref = next(it)
        w_vm = [next(it) for _ in weights]
        sems = next(it)
        _load_resident(pl.program_id(0), list(zip(w_hbm, w_vm)), sems)
        dx = ALPHA * dr_ref[...] + _mm_nt(dhg_ref[...], w_vm[0][...]) + _mm_nt(dhu_ref[...], w_vm[1][...])
        if with_ple:
            dx = dx + _mm_nt(dpg_ref[...], w_vm[2][...])
        dx_ref[...] = dx

    in_specs = [_row_spec(tb, d), _row_spec(tb, f), _row_spec(tb, f)] + ([_row_spec(tb, d)] if with_ple else [])
    args = [dr, dhg, dhu] + ([first[7]] if with_ple else [])
    dx = pl.pallas_call(
        body_b, name="ffn_bwd_input" + suffix, grid=(s // tb,), in_specs=in_specs + [_ANY] * len(weights),
        out_specs=_row_spec(tb, d), out_shape=jax.ShapeDtypeStruct((s, d), F32),
        scratch_shapes=[pltpu.VMEM(w.shape, w.dtype) for w in weights] + [pltpu.SemaphoreType.DMA((len(weights),))],
        compiler_params=_params("arbitrary"),
    )(*args, *weights)
    return (dx, act, dhg, dhu, dy, dg, db) + tuple(first[7:])


def win_fwd(x1, w_r, w_l, w_g):
    s = x1.shape[0]
    tm = min(ROW_TILE_BWD, s)
    weights = [w_r, w_l, w_g]

    def body(x_ref, wr_h, wl_h, wg_h, hr_ref, hl_ref, hgd_ref, wr_v, wl_v, wg_v, sems):
        _load_resident(pl.program_id(0), [(wr_h, wr_v), (wl_h, wl_v), (wg_h, wg_v)], sems)
        xb = x_ref[...].astype(BF16)
        hr_ref[...] = _mm(xb, wr_v[...])
        hl_ref[...] = _mm(xb, wl_v[...])
        hgd_ref[...] = _mm(xb, wg_v[...])

    return pl.pallas_call(
        body, name="win_fwd", grid=(s // tm,),
        in_specs=[_row_spec(tm, D_MODEL), _ANY, _ANY, _ANY],
        out_specs=[_row_spec(tm, RET_IN), _row_spec(tm, LRU_IN), _row_spec(tm, GDN_IN)],
        out_shape=[jax.ShapeDtypeStruct((s, RET_IN), F32), jax.ShapeDtypeStruct((s, LRU_IN), F32),
                   jax.ShapeDtypeStruct((s, GDN_IN), F32)],
        scratch_shapes=[pltpu.VMEM(w.shape, w.dtype) for w in weights] + [pltpu.SemaphoreType.DMA((3,))],
        compiler_params=_params("arbitrary"),
    )(x1, *weights)


def win_bwd(dr2, dhr, dhl, dhq, dab, w_r, w_l, w_g, w_ab):
    s = dr2.shape[0]
    tm = min(ROW_TILE_BWD, s)
    weights = [w_r, w_l, w_g, w_ab]
    nq = 4 * GDN_W

    def body(dr_ref, dhr_ref, dhl_ref, dhq_ref, dab_ref, wr_h, wl_h, wg_h, wab_h, dx_ref, wr_v, wl_v, wg_v, wab_v, sems):
        _load_resident(pl.program_id(0), [(wr_h, wr_v), (wl_h, wl_v), (wg_h, wg_v), (wab_h, wab_v)], sems)
        dx_ref[...] = (ALPHA * dr_ref[...] + _mm_nt(dhr_ref[...], wr_v[...]) + _mm_nt(dhl_ref[...], wl_v[...])
                       + _mm_nt(dhq_ref[...], wg_v[:, 0:nq]) + _mm_nt(dab_ref[...], wab_v[...]))

    return pl.pallas_call(
        body, name="win_bwd", grid=(s // tm,),
        in_specs=[_row_spec(tm, D_MODEL), _row_spec(tm, RET_IN), _row_spec(tm, LRU_IN), _row_spec(tm, nq), _row_spec(tm, LANES),
                  _ANY, _ANY, _ANY, _ANY],
        out_specs=_row_spec(tm, D_MODEL),
        out_shape=jax.ShapeDtypeStruct((s, D_MODEL), F32),
        scratch_shapes=[pltpu.VMEM(w.shape, w.dtype) for w in weights] + [pltpu.SemaphoreType.DMA((4,))],
        compiler_params=_params("arbitrary"),
    )(dr2, dhr, dhl, dhq, dab, *weights)


def out_fwd(o_r, o_l, o_g, x1, w_out, ln_g, ln_b):
    s = x1.shape[0]
    tm = ROW_TILE

    def body(or_ref, ol_ref, og_ref, x_ref, g_ref, b_ref, w_h, r_ref, xn_ref, xnb_ref, ocat_ref, w_v, sems):
        _load_resident(pl.program_id(0), [(w_h, w_v)], sems)
        ocat = jnp.concatenate([or_ref[...], ol_ref[...], og_ref[...]], axis=1).astype(BF16)
        ocat_ref[...] = ocat
        r = ALPHA * x_ref[...] + _mm(ocat, w_v[...])
        r_ref[...] = r
        xhat, _ = _layer_norm_stats(r)
        xn = xhat * g_ref[...] + b_ref[...]
        xn_ref[...] = xn
        xnb_ref[...] = xn.astype(BF16)

    d = D_MODEL
    return pl.pallas_call(
        body, name="out_fwd", grid=(s // tm,),
        in_specs=[_row_spec(tm, RET_W), _row_spec(tm, LRU_W), _row_spec(tm, GDN_W), _row_spec(tm, d),
                  _full_spec((1, d)), _full_spec((1, d)), _ANY],
        out_specs=[_row_spec(tm, d)] * 4,
        out_shape=[jax.ShapeDtypeStruct((s, d), F32)] * 2 + [jax.ShapeDtypeStruct((s, d), BF16)] * 2,
        scratch_shapes=[pltpu.VMEM(w_out.shape, w_out.dtype), pltpu.SemaphoreType.DMA((1,))],
        compiler_params=_params("arbitrary"),
    )(o_r, o_l, o_g, x1, ln_g, ln_b, w_out)


def out_bwd(dxn, r2, ln_g, w_out):
    s = dxn.shape[0]
    tm = ROW_TILE

    def body(dxn_ref, r_ref, g_ref, w_h, dr_ref, drb_ref, dor_ref, dol_ref, dog_ref, dg_ref, db_ref, w_v, sems):
        step = pl.program_id(0)
        _load_resident(step, [(w_h, w_v)], sems)

        @pl.when(step == 0)
        def _():
            dg_ref[...] = jnp.zeros_like(dg_ref)
            db_ref[...] = jnp.zeros_like(db_ref)

        dxn_v = dxn_ref[...]
        xhat, rstd = _layer_norm_stats(r_ref[...])
        dg_ref[...] += jnp.sum(dxn_v * xhat, axis=0, keepdims=True)
        db_ref[...] += jnp.sum(dxn_v, axis=0, keepdims=True)
        dyh = dxn_v * g_ref[...]
        dr = rstd * (dyh - jnp.mean(dyh, axis=-1, keepdims=True) - xhat * jnp.mean(dyh * xhat, axis=-1, keepdims=True))
        dr_ref[...] = dr
        drb = dr.astype(BF16)
        drb_ref[...] = drb
        dor_ref[...] = _mm_nt(drb, w_v[0:RET_W, :])
        dol_ref[...] = _mm_nt(drb, w_v[RET_W:RET_W + LRU_W, :])
        dog_ref[...] = _mm_nt(drb, w_v[RET_W + LRU_W:, :])

    d = D_MODEL
    return pl.pallas_call(
        body, name="out_bwd", grid=(s // tm,),
        in_specs=[_row_spec(tm, d), _row_spec(tm, d), _full_spec((1, d)), _ANY],
        out_specs=[_row_spec(tm, d), _row_spec(tm, d), _row_spec(tm, RET_W), _row_spec(tm, LRU_W), _row_spec(tm, GDN_W),
                   _full_spec((1, d)), _full_spec((1, d))],
        out_shape=[jax.ShapeDtypeStruct((s, d), F32), jax.ShapeDtypeStruct((s, d), BF16),
                   jax.ShapeDtypeStruct((s, RET_W), F32), jax.ShapeDtypeStruct((s, LRU_W), F32),
                   jax.ShapeDtypeStruct((s, GDN_W), F32), jax.ShapeDtypeStruct((1, d), F32), jax.ShapeDtypeStruct((1, d), F32)],
        scratch_shapes=[pltpu.VMEM(w_out.shape, w_out.dtype), pltpu.SemaphoreType.DMA((1,))],
        compiler_params=_params("arbitrary"),
    )(dxn, r2, ln_g, w_out)


def wgrad(a, b, name, out_dtype=BF16):
    s, m = a.shape
    n = b.shape[1]
    tk = 1024 if s % 1024 == 0 else s
    tm = next((c for c in (1408, 1024, 768, 512, 384, 256) if m % c == 0), m)
    tn = next((c for c in (1408, 1152, 1024, 768, 512) if n % c == 0), n)
    nk = s // tk

    def body(a_ref, b_ref, o_ref, acc_ref):
        k = pl.program_id(2)

        @pl.when(k == 0)
        def _():
            acc_ref[...] = jnp.zeros_like(acc_ref)

        acc_ref[...] += _mm_tn(a_ref[...].astype(BF16), b_ref[...].astype(BF16))

        @pl.when(k == nk - 1)
        def _():
            o_ref[...] = acc_ref[...].astype(o_ref.dtype)

    return pl.pallas_call(
        body, name=name, grid=(m // tm, n // tn, nk),
        in_specs=[pl.BlockSpec((tk, tm), lambda i, j, k: (k, i)), pl.BlockSpec((tk, tn), lambda i, j, k: (k, j))],
        out_specs=pl.BlockSpec((tm, tn), lambda i, j, k: (i, j)),
        out_shape=jax.ShapeDtypeStruct((m, n), out_dtype),
        scratch_shapes=[pltpu.VMEM((tm, tn), F32)],
        compiler_params=_params("arbitrary", "arbitrary", "arbitrary"),
    )(a, b)


def loss_and_grad(y, target):
    s, d = y.shape
    tm = ROW_TILE

    def body(y_ref, t_ref, dy_ref, l_ref):
        @pl.when(pl.program_id(0) == 0)
        def _():
            l_ref[...] = jnp.zeros_like(l_ref)

        err = y_ref[...] - t_ref[...]
        dy_ref[...] = err / d
        l_ref[...] += 0.5 * jnp.sum(jnp.mean(err * err, axis=-1, keepdims=True), axis=0, keepdims=True)

    return pl.pallas_call(
        body, name="loss_and_grad", grid=(s // tm,),
        in_specs=[_row_spec(tm, d), _row_spec(tm, d)],
        out_specs=[_row_spec(tm, d), _full_spec((1, 1))],
        out_shape=[jax.ShapeDtypeStruct((s, d), F32), jax.ShapeDtypeStruct((1, 1), F32)],
        compiler_params=_params("arbitrary"),
    )(y, target)


def _ret_consts():
    lg = np.log1p(-np.exp2(-5.0 - np.arange(RET_HEADS, dtype=np.float64)))
    idx = np.arange(CHUNK, dtype=np.float64)
    intra = np.exp(np.abs(idx[:, None] - idx[None, :])[None] * lg[:, None, None])
    cross = np.repeat(np.exp((idx + 1.0)[:, None] * lg[None, :]), HEAD, axis=1)
    tail = np.repeat(np.exp((CHUNK - 1.0 - idx)[:, None] * lg[None, :]), HEAD, axis=1)
    dec = np.repeat(np.exp(CHUNK * lg)[None, :], HEAD, axis=1)
    half = HEAD // 2
    inv_freq = (ROPE_THETA ** (-jnp.arange(half, dtype=F32) / half))
    invf = jnp.tile(inv_freq, 2 * LANES // HEAD)[None, :]
    sgn = np.tile(np.concatenate([-np.ones(half), np.ones(half)]), LANES // HEAD)[None, :]
    f = lambda a: jnp.asarray(a, F32)
    return dict(intra=f(intra), cross=f(cross), tail=f(tail), dec=f(dec), invf=invf, sgn=f(sgn))


def _swap_halves(t):
    lane = lax.broadcasted_iota(jnp.int32, t.shape, 1)
    return jnp.where((lane & 32) == 0, pltpu.roll(t, LANES - 32, 1), pltpu.roll(t, 32, 1))


def _rope(t, c, s):
    return t * c + _swap_halves(t) * s


def _rope_transposed(g, c, s):
    return g * c + _swap_halves(g * s)


def _head_mask(hd):
    lane = lax.broadcasted_iota(jnp.int32, (1, LANES), 1)
    return ((lane >= HEAD * hd) & (lane < HEAD * (hd + 1))).astype(F32)


def _block_diag_mask():
    r = lax.broadcasted_iota(jnp.int32, (LANES, LANES), 0)
    c = lax.broadcasted_iota(jnp.int32, (LANES, LANES), 1)
    return ((r >= HEAD) == (c >= HEAD)).astype(F32)


RET_STEP_CHUNKS = 4


def _ret_specs(n_of, gch):
    cst = lambda shape: pl.BlockSpec(shape, lambda i: (0,) * len(shape))
    return [pl.BlockSpec((CHUNK * gch, RET_IN), lambda i: (n_of(i), 0)), pl.BlockSpec((CHUNK * gch, 1), lambda i: (n_of(i), 0)),
            cst((1, LANES)), cst((1, LANES)), cst((RET_HEADS, CHUNK, CHUNK)), cst((CHUNK, RET_W)), cst((CHUNK, RET_W)),
            cst((1, RET_W)), cst((1, RET_W))]


def ret_fwd(hr, pos, norm_g):
    s = hr.shape[0]
    n_chunks = s // CHUNK
    cs = _ret_consts()
    n_slab = RET_W // LANES

    gch = min(RET_STEP_CHUNKS, n_chunks)

    def body(hr_ref, pos_ref, invf_ref, sgn_ref, intra_ref, cross_ref, tail_ref, dec_ref, g_ref, o_ref, opre_ref, st_ref, state):
        @pl.when(pl.program_id(0) == 0)
        def _():
            state[...] = jnp.zeros_like(state)

        bd = _block_diag_mask()
        sts = [state[LANES * sl:LANES * (sl + 1), :] for sl in range(n_slab)]
        for c in range(gch):
            tok = slice(CHUNK * c, CHUNK * (c + 1))
            ang = pos_ref[tok, :].astype(F32) * invf_ref[...]
            cosv = jnp.cos(ang)
            sinv = jnp.sin(ang) * sgn_ref[...]
            for sl in range(n_slab):
                lanes = slice(LANES * sl, LANES * (sl + 1))
                q = hr_ref[tok, LANES * sl:LANES * (sl + 1)]
                k = hr_ref[tok, RET_W + LANES * sl:RET_W + LANES * (sl + 1)]
                v = hr_ref[tok, 2 * RET_W + LANES * sl:2 * RET_W + LANES * (sl + 1)]
                gate = hr_ref[tok, 3 * RET_W + LANES * sl:3 * RET_W + LANES * (sl + 1)]
                qt = _rope(q, cosv, sinv) * (HEAD ** -0.5)
                kt = _rope(k, cosv, sinv)
                st = sts[sl]
                st_ref[RET_W * c + LANES * sl:RET_W * c + LANES * (sl + 1), :] = st
                o = _mm(qt * cross_ref[:, lanes], st)
                for hd in range(2):
                    m = _head_mask(hd)
                    sc = _mm_nt(qt * m, kt) * intra_ref[2 * sl + hd]
                    o = o + _mm(sc, v) * m
                sts[sl] = st * dec_ref[:, lanes] + _mm_tn(kt, v * tail_ref[:, lanes]) * bd
                opre_ref[tok, lanes] = o
                mu = _group_sum_slab(o) * (1.0 / HEAD)
                dlt = o - mu
                var = _group_sum_slab(dlt * dlt) * (1.0 / HEAD)
                on = dlt * lax.rsqrt(var + 1e-5)
                o_ref[tok, lanes] = on * g_ref[:, lanes] * (gate * _sigmoid(gate))
        for sl in range(n_slab):
            state[LANES * sl:LANES * (sl + 1), :] = sts[sl]

    out_row = lambda w: pl.BlockSpec((CHUNK * gch, w), lambda i: (i, 0))
    return pl.pallas_call(
        body, name="ret_fwd", grid=(n_chunks // gch,),
        in_specs=_ret_specs(lambda i: i, gch),
        out_specs=[out_row(RET_W), out_row(RET_W), pl.BlockSpec((RET_W * gch, LANES), lambda i: (i, 0))],
        out_shape=[jax.ShapeDtypeStruct((s, RET_W), F32), jax.ShapeDtypeStruct((s, RET_W), F32),
                   jax.ShapeDtypeStruct((n_chunks * RET_W, LANES), F32)],
        scratch_shapes=[pltpu.VMEM((RET_W, LANES), F32)],
        compiler_params=_params("arbitrary"),
    )(hr, pos, cs["invf"], cs["sgn"], cs["intra"], cs["cross"], cs["tail"], cs["dec"], norm_g)


def ret_bwd(hr, pos, norm_g, opre, states, dout):
    s = hr.shape[0]
    n_chunks = s // CHUNK
    cs = _ret_consts()
    n_slab = RET_W // LANES
    gch = min(RET_STEP_CHUNKS, n_chunks)
    rev = lambda i: n_chunks // gch - 1 - i

    def body(hr_ref, pos_ref, invf_ref, sgn_ref, intra_ref, cross_ref, tail_ref, dec_ref, g_ref, opre_ref, st_ref, do_ref,
             dh_ref, dg_ref, gstate):
        @pl.when(pl.program_id(0) == 0)
        def _():
            gstate[...] = jnp.zeros_like(gstate)
            dg_ref[...] = jnp.zeros_like(dg_ref)

        bd = _block_diag_mask()
        gss = [gstate[LANES * sl:LANES * (sl + 1), :] for sl in range(n_slab)]
        dgs = [jnp.zeros((1, LANES), F32) for _ in range(n_slab)]
        for c in reversed(range(gch)):
            tok = slice(CHUNK * c, CHUNK * (c + 1))
            ang = pos_ref[tok, :].astype(F32) * invf_ref[...]
            cosv = jnp.cos(ang)
            sinv = jnp.sin(ang) * sgn_ref[...]
            for sl in range(n_slab):
                lanes = slice(LANES * sl, LANES * (sl + 1))
                q = hr_ref[tok, LANES * sl:LANES * (sl + 1)]
                k = hr_ref[tok, RET_W + LANES * sl:RET_W + LANES * (sl + 1)]
                v = hr_ref[tok, 2 * RET_W + LANES * sl:2 * RET_W + LANES * (sl + 1)]
                gate = hr_ref[tok, 3 * RET_W + LANES * sl:3 * RET_W + LANES * (sl + 1)]
                qt = _rope(q, cosv, sinv) * (HEAD ** -0.5)
                kt = _rope(k, cosv, sinv)
                o = opre_ref[tok, lanes]
                mu = _group_sum_slab(o) * (1.0 / HEAD)
                dlt = o - mu
                var = _group_sum_slab(dlt * dlt) * (1.0 / HEAD)
                rstd = lax.rsqrt(var + 1e-5)
                on = dlt * rstd
                sil, dsil = _silu_and_grad(gate)
                dout_v = do_ref[tok, lanes]
                gn = g_ref[:, lanes]
                dgs[sl] = dgs[sl] + jnp.sum(dout_v * on * sil, axis=0, keepdims=True)
                d_on = dout_v * gn * sil
                dgate = dout_v * on * gn * dsil
                d_o = rstd * (d_on - _group_sum_slab(d_on) * (1.0 / HEAD) - on * (_group_sum_slab(d_on * on) * (1.0 / HEAD)))
                st = st_ref[RET_W * c + LANES * sl:RET_W * c + LANES * (sl + 1), :]
                gs = gss[sl]
                cross = cross_ref[:, lanes]
                tail = tail_ref[:, lanes]
                dqt = _mm_nt(d_o, st) * cross
                ds_here = _mm_tn(qt * cross, d_o) * bd
                vt = v * tail
                dkt = _mm_nt(vt, gs)
                dv = _mm(kt, gs) * tail
                for hd in range(2):
                    m = _head_mask(hd)
                    qm = qt * m
                    dom = d_o * m
                    intra = intra_ref[2 * sl + hd]
                    sc = _mm_nt(qm, kt) * intra
                    dsc = _mm_nt(dom, v) * intra
                    dqt = dqt + _mm(dsc, kt) * m
                    dkt = dkt + _mm_tn(dsc, qm)
                    dv = dv + _mm_tn(sc, dom)
                gss[sl] = gs * dec_ref[:, lanes] + ds_here
                dh_ref[tok, LANES * sl:LANES * (sl + 1)] = _rope_transposed(dqt * (HEAD ** -0.5), cosv, sinv).astype(BF16)
                dh_ref[tok, RET_W + LANES * sl:RET_W + LANES * (sl + 1)] = _rope_transposed(dkt, cosv, sinv).astype(BF16)
                dh_ref[tok, 2 * RET_W + LANES * sl:2 * RET_W + LANES * (sl + 1)] = dv.astype(BF16)
                dh_ref[tok, 3 * RET_W + LANES * sl:3 * RET_W + LANES * (sl + 1)] = dgate.astype(BF16)
        for sl in range(n_slab):
            gstate[LANES * sl:LANES * (sl + 1), :] = gss[sl]
            dg_ref[:, LANES * sl:LANES * (sl + 1)] += dgs[sl]

    row = lambda w: pl.BlockSpec((CHUNK * gch, w), lambda i: (rev(i), 0))
    return pl.pallas_call(
        body, name="ret_bwd", grid=(n_chunks // gch,),
        in_specs=_ret_specs(rev, gch) + [row(RET_W), pl.BlockSpec((RET_W * gch, LANES), lambda i: (rev(i), 0)), row(RET_W)],
        out_specs=[row(RET_IN), pl.BlockSpec((1, RET_W), lambda i: (0, 0))],
        out_shape=[jax.ShapeDtypeStruct((s, RET_IN), BF16), jax.ShapeDtypeStruct((1, RET_W), F32)],
        scratch_shapes=[pltpu.VMEM((RET_W, LANES), F32)],
        compiler_params=_params("arbitrary"),
    )(hr, pos, cs["invf"], cs["sgn"], cs["intra"], cs["cross"], cs["tail"], cs["dec"], norm_g, opre, states, dout)


def _lru_gates(xc, wa_ref, ba_ref, wx_ref, bx_ref, lam_ref):
    xcb = xc.astype(BF16)
    r = _sigmoid(_mm(xcb, wa_ref[...].astype(BF16)) + ba_ref[...])
    ig = _sigmoid(_mm(xcb, wx_ref[...].astype(BF16)) + bx_ref[...])
    lam = lam_ref[...]
    ls = jnp.minimum(lam, 0.0) - _log1p(jnp.exp(-jnp.abs(lam)))
    la = (LRU_C * r) * ls
    a = jnp.exp(la)
    mult = jnp.sqrt(-_expm1(2.0 * la))
    return r, ig, ls, a, mult


def _lru_conv(x, xprev, w_ref, b_ref):
    xc = b_ref[...] + w_ref[3:4, :] * x
    for j in (1, 2, 3):
        xc = xc + w_ref[3 - j:4 - j, :] * _shift_rows(x, xprev, j)
    return xc


def lru_fwd(hl, conv_w, conv_b, w_a, b_a, w_x, b_x, lam):
    s = hl.shape[0]
    ts = SCAN_TILE
    w = LRU_W

    def body(hl_ref, hp_ref, cw_ref, cb_ref, wa_ref, ba_ref, wx_ref, bx_ref, lam_ref, o_ref, xc_ref, h_ref, carry):
        i = pl.program_id(0)

        @pl.when(i == 0)
        def _():
            carry[...] = jnp.zeros_like(carry)

        x = hl_ref[:, 0:w]
        gate = hl_ref[:, w:2 * w]
        xprev = hp_ref[...] * (i > 0).astype(F32)
        xc = _lru_conv(x, xprev, cw_ref, cb_ref)
        xc_ref[...] = xc
        _, ig, _, a, mult = _lru_gates(xc, wa_ref, ba_ref, wx_ref, bx_ref, lam_ref)
        b = mult * (ig * xc)
        row = lax.broadcasted_iota(jnp.int32, (ts, w), 0)
        d = 1
        while d < ts:
            ap = jnp.where(row >= d, pltpu.roll(a, d, 0), 1.0)
            bp = jnp.where(row >= d, pltpu.roll(b, d, 0), 0.0)
            b = a * bp + b
            a = a * ap
            d *= 2
        h = b + a * carry[0:1, :]
        h_ref[...] = h
        carry[0:1, :] = h[ts - 1:ts, :]
        o_ref[...] = h * _gelu(gate)

    cst = lambda shape: pl.BlockSpec(shape, lambda i: (0, 0))
    return pl.pallas_call(
        body, name="lru_fwd", grid=(s // ts,),
        in_specs=[_row_spec(ts, 2 * w), pl.BlockSpec((ts, w), lambda i: (jnp.maximum(i - 1, 0), 0)),
                  cst((4, w)), cst((1, w)), cst((w, w)), cst((1, w)), cst((w, w)), cst((1, w)), cst((1, w))],
        out_specs=[_row_spec(ts, w)] * 3,
        out_shape=[jax.ShapeDtypeStruct((s, w), F32)] * 3,
        scratch_shapes=[pltpu.VMEM((8, w), F32)],
        compiler_params=_params("arbitrary"),
    )(hl, hl, conv_w, conv_b, w_a, b_a, w_x, b_x, lam)


def lru_bwd(hl, conv_w, conv_b, w_a, b_a, w_x, b_x, lam, xc_saved, h_saved, dout):
    s = hl.shape[0]
    ts = SCAN_TILE
    w = LRU_W
    nb = s // ts
    rev = lambda i: nb - 1 - i

    def body(hl_ref, hp_ref, cw_ref, cb_ref, wa_ref, ba_ref, wx_ref, bx_ref, lam_ref, xc_ref, h_ref, hprev_ref, do_ref,
             dhl_ref, dcw_ref, dcb_ref, dwa_ref, dba_ref, dwx_ref, dbx_ref, dlam_ref, carry, dxc_next):
        i = pl.program_id(0)
        blk = nb - 1 - i

        @pl.when(i == 0)
        def _():
            carry[...] = jnp.zeros_like(carry)
            dxc_next[...] = jnp.zeros_like(dxc_next)
            for ref in (dcw_ref, dcb_ref, dwa_ref, dba_ref, dwx_ref, dbx_ref, dlam_ref):
                ref[...] = jnp.zeros_like(ref)

        first = (blk > 0).astype(F32)
        x = hl_ref[:, 0:w]
        gate = hl_ref[:, w:2 * w]
        xprev = hp_ref[...] * first
        xc = xc_ref[...]
        h = h_ref[...]
        hprev = hprev_ref[...] * first
        r, ig, ls, a, mult = _lru_gates(xc, wa_ref, ba_ref, wx_ref, bx_ref, lam_ref)
        do = do_ref[...]
        dh = do * _gelu(gate)
        dgate = do * h * _gelu_grad(gate)
        row = lax.broadcasted_iota(jnp.int32, (ts, w), 0)
        ca = jnp.where(row < ts - 1, pltpu.roll(a, ts - 1, 0), 1.0)
        cb = dh
        d = 1
        while d < ts:
            an = jnp.where(row < ts - d, pltpu.roll(ca, ts - d, 0), 1.0)
            bn = jnp.where(row < ts - d, pltpu.roll(cb, ts - d, 0), 0.0)
            cb = cb + ca * bn
            ca = ca * an
            d *= 2
        lamb = cb + ca * carry[0:1, :]
        carry[0:1, :] = a[0:1, :] * lamb[0:1, :]
        h_before = _shift_rows(h, hprev, 1)
        da = lamb * h_before
        ix = ig * xc
        dmult = lamb * ix
        dig = lamb * mult * xc
        dxc = lamb * mult * ig
        dla = (da - dmult * a / mult) * a
        dr = dla * LRU_C * ls
        dlam_ref[...] += jnp.sum(dla * LRU_C * r, axis=0, keepdims=True) * _sigmoid(-lam_ref[...])
        dpa = dr * r * (1.0 - r)
        dpx = dig * ig * (1.0 - ig)
        dba_ref[...] += jnp.sum(dpa, axis=0, keepdims=True)
        dbx_ref[...] += jnp.sum(dpx, axis=0, keepdims=True)
        dpab = dpa.astype(BF16)
        dpxb = dpx.astype(BF16)
        xcb = xc.astype(BF16)
        dxc = dxc + _mm_nt(dpab, wa_ref[...].astype(BF16)) + _mm_nt(dpxb, wx_ref[...].astype(BF16))
        dwa_ref[...] += _mm_tn(xcb, dpab)
        dwx_ref[...] += _mm_tn(xcb, dpxb)
        dcb_ref[...] += jnp.sum(dxc, axis=0, keepdims=True)
        nxt = dxc_next[...]
        dx = cw_ref[3:4, :] * dxc
        dcw_ref[3:4, :] += jnp.sum(dxc * x, axis=0, keepdims=True)
        for j in (1, 2, 3):
            dx = dx + cw_ref[3 - j:4 - j, :] * _shift_rows_up(dxc, nxt, j)
            dcw_ref[3 - j:4 - j, :] += jnp.sum(dxc * _shift_rows(x, xprev, j), axis=0, keepdims=True)
        dxc_next[...] = dxc
        dhl_ref[:, 0:w] = dx.astype(BF16)
        dhl_ref[:, w:2 * w] = dgate.astype(BF16)

    cst = lambda shape: pl.BlockSpec(shape, lambda i: (0, 0))
    rowr = lambda width: pl.BlockSpec((ts, width), lambda i: (rev(i), 0))
    prevr = lambda width: pl.BlockSpec((ts, width), lambda i: (jnp.maximum(rev(i) - 1, 0), 0))
    return pl.pallas_call(
        body, name="lru_bwd", grid=(nb,),
        in_specs=[rowr(2 * w), prevr(w), cst((4, w)), cst((1, w)), cst((w, w)), cst((1, w)), cst((w, w)), cst((1, w)), cst((1, w)),
                  rowr(w), rowr(w), prevr(w), rowr(w)],
        out_specs=[rowr(2 * w), cst((4, w)), cst((1, w)), cst((w, w)), cst((1, w)), cst((w, w)), cst((1, w)), cst((1, w))],
        out_shape=[jax.ShapeDtypeStruct((s, 2 * w), BF16), jax.ShapeDtypeStruct((4, w), F32), jax.ShapeDtypeStruct((1, w), F32),
                   jax.ShapeDtypeStruct((w, w), F32), jax.ShapeDtypeStruct((1, w), F32), jax.ShapeDtypeStruct((w, w), F32),
                   jax.ShapeDtypeStruct((1, w), F32), jax.ShapeDtypeStruct((1, w), F32)],
        scratch_shapes=[pltpu.VMEM((8, w), F32), pltpu.VMEM((ts, w), F32)],
        compiler_params=_params("arbitrary"),
    )(hl, hl, conv_w, conv_b, w_a, b_a, w_x, b_x, lam, xc_saved, h_saved, h_saved, dout)


GDN_QKV = 3 * GDN_W
GDN_STEP_CHUNKS = 4
GDN_BWD_STEP_CHUNKS = 2


def _tri_inverse_many(nms):
    r = lax.broadcasted_iota(jnp.int32, nms[0].shape, 0)
    c = lax.broadcasted_iota(jnp.int32, nms[0].shape, 1)
    eye = (r == c).astype(F32)
    ts = [eye - nm for nm in nms]
    ps = list(nms)
    for _ in range(5):
        ps = [_mm3(p, p) for p in ps]
        ts = [t + _mm3(t, p) for t, p in zip(ts, ps)]
    return ts


def _gdn_front(hx_ref, hprev, cw_ref, al_ref, dt_ref):
    w = GDN_W
    x = hx_ref[:, 0:GDN_QKV]
    y = cw_ref[3:4, :] * x
    for j in (1, 2, 3):
        y = y + cw_ref[3 - j:4 - j, :] * _shift_rows(x, hprev, j)
    qkv, dsil = _silu_and_grad(y)
    q, k, v = qkv[:, 0:w], qkv[:, w:2 * w], qkv[:, 2 * w:3 * w]
    rq = lax.rsqrt(_group_sum(q * q) + 1e-6)
    rk = lax.rsqrt(_group_sum(k * k) + 1e-6)
    beta = _sigmoid(hx_ref[:, 5 * w:6 * w])
    sp_in = hx_ref[:, 4 * w:5 * w] + dt_ref[...]
    neg_a = -jnp.exp(al_ref[...])
    g = neg_a * _softplus(sp_in)
    n_c = g.shape[0] // CHUNK
    gc = jnp.concatenate([_rows_prefix_sum(g[CHUNK * c:CHUNK * (c + 1)]) for c in range(n_c)], axis=0)
    return dict(x=x, dsil=dsil, qn=q * rq, kn=k * rk, v=v, rq=rq, rk=rk, beta=beta, sp_in=sp_in, neg_a=neg_a, g=g, gc=gc)


def _stack_heads(x):
    return jnp.concatenate([x * _head_mask(0), x * _head_mask(1)], axis=0)


def _unstack_heads(y):
    return y[0:CHUNK] + y[CHUNK:2 * CHUNK]


def _head_transpose(x):
    return jnp.concatenate([x[:, 0:HEAD].T, x[:, HEAD:2 * HEAD].T], axis=1)


def _head_total(x):
    cols = jnp.broadcast_to(jnp.sum(x, axis=0, keepdims=True), (8, LANES))
    return _group_sum_slab(cols)[0:1]


def _slab_tri_masks():
    r = lax.broadcasted_iota(jnp.int32, (CHUNK, LANES), 0)
    c = lax.broadcasted_iota(jnp.int32, (CHUNK, LANES), 1) & (HEAD - 1)
    return r >= c, r > c


def _gdn_slab(fr, c, sl, tri, transposed=False):
    lower, strict = tri
    ls = lambda a: a[CHUNK * c:CHUNK * (c + 1), LANES * sl:LANES * (sl + 1)]
    k = ls(fr["kn"])
    q = ls(fr["qn"]) * (HEAD ** -0.5)
    v = ls(fr["v"])
    beta = ls(fr["beta"])
    gc = ls(fr["gc"])
    e = jnp.exp(gc)
    gl = gc[CHUNK - 1:CHUNK, :]
    xt = jnp.exp(gl - gc)
    gc_t = _head_transpose(gc)
    dec = jnp.where(lower, jnp.exp(jnp.minimum(gc - gc_t, 0.0)), 0.0)
    kbd = _stack_heads(k)
    kk = _mm_nt(k, kbd)
    qkr = _mm_nt(q, kbd)
    out = dict(k=k, q=q, v=v, beta=beta, e=e, egl=jnp.exp(gl), xt=xt, dec=dec, kk=kk, qkr=qkr, kbd=kbd,
               nm=jnp.where(strict, beta * kk * dec, 0.0))
    if transposed:
        r = lax.broadcasted_iota(jnp.int32, (CHUNK, LANES), 0)
        col = lax.broadcasted_iota(jnp.int32, (CHUNK, LANES), 1) & (HEAD - 1)
        qbd = _stack_heads(q)
        out.update(dec_t=jnp.where(r <= col, jnp.exp(jnp.minimum(gc_t - gc, 0.0)), 0.0), beta_t=_head_transpose(beta),
                   qbd=qbd, kqr=_mm_nt(k, qbd), strict_t=r < col)
    return out


def gdn_fwd(hx, conv_w, a_log_e, dt_bias_e, norm_g_e, fused=None):
    s = hx.shape[0]
    n_chunks = s // CHUNK
    w = GDN_W
    n_slab = w // LANES
    gch = min(GDN_STEP_CHUNKS, n_chunks)

    def body(hx_ref, hp_ref, cw_ref, al_ref, dt_ref, ng_ref, o_ref, opre_ref, t_ref, st_ref, state):
        n = pl.program_id(0)

        @pl.when(n == 0)
        def _():
            state[...] = jnp.zeros_like(state)

        fr = _gdn_front(hx_ref, hp_ref[...] * (n > 0).astype(F32), cw_ref, al_ref, dt_ref)
        tri = _slab_tri_masks()
        bd = _block_diag_mask()
        sts = [state[LANES * sl:LANES * (sl + 1), :] for sl in range(n_slab)]
        slabs = [[_gdn_slab(fr, c, sl, tri) for sl in range(n_slab)] for c in range(gch)]
        tbd = _tri_inverse_many([_stack_heads(sq["nm"]) for row_ in slabs for sq in row_])
        o_rows = []
        for c in range(gch):
            ts, outs = [], []
            st_ref[w * c:w * (c + 1), :] = jnp.concatenate(sts, axis=0)
            for sl in range(n_slab):
                sq = slabs[c][sl]
                t = _unstack_heads(tbd[n_slab * c + sl])
                ts.append(t)
                u = _mm(t, _stack_heads(sq["v"] * sq["beta"]))
                wk = _mm(t, _stack_heads(sq["k"] * (sq["beta"] * sq["e"])))
                st = sts[sl]
                vnew = u - _mm(wk, st)
                outs.append(_mm(sq["q"] * sq["e"], st) + _mm(sq["qkr"] * sq["dec"], _stack_heads(vnew)))
                sts[sl] = st * sq["egl"] + _mm_tn(sq["k"] * sq["xt"], vnew) * bd
            t_ref[CHUNK * c:CHUNK * (c + 1), :] = jnp.concatenate(ts, axis=1)
            o_rows.append(jnp.concatenate(outs, axis=1))
        state[...] = jnp.concatenate(sts, axis=0)
        o = jnp.concatenate(o_rows, axis=0)
        opre_ref[...] = o
        rinv = lax.rsqrt(_group_sum(o * o) * (1.0 / HEAD) + 1e-6)
        z = hx_ref[:, 3 * w:4 * w]
        o_ref[...] = (o * rinv) * ng_ref[...] * (z * _sigmoid(z))

    cst = lambda shape: pl.BlockSpec(shape, lambda i: (0, 0))
    row = lambda width: pl.BlockSpec((CHUNK * gch, width), lambda i: (i, 0))
    f_in, f_out, f_scr, _ = fused if fused is not None else ([], [], [], None)
    outs = pl.pallas_call(
        _fuse(body, 6, 4, 1, n_chunks // gch, fused), name="gdn_fwd" + ("_gather" if fused is not None else ""),
        grid=(n_chunks // gch,),
        in_specs=[row(GDN_IN), pl.BlockSpec((CHUNK * gch, GDN_QKV), lambda i: (jnp.maximum(i - 1, 0), 0)),
                  cst((4, GDN_QKV)), cst((1, w)), cst((1, w)), cst((1, w))] + [_ANY] * len(f_in),
        out_specs=[row(w)] * 3 + [pl.BlockSpec((w * gch, LANES), lambda i: (i, 0))] + [_ANY] * len(f_out),
        out_shape=[jax.ShapeDtypeStruct((s, w), F32)] * 3 + [jax.ShapeDtypeStruct((n_chunks * w, LANES), F32)] + list(f_out),
        scratch_shapes=[pltpu.VMEM((w, LANES), F32)] + list(f_scr),
        compiler_params=_params("arbitrary"),
    )(hx, hx, conv_w, a_log_e, dt_bias_e, norm_g_e, *f_in)
    return tuple(outs[:4]) + ((list(outs[4:]),) if fused is not None else ())


def gdn_bwd(hx, conv_w, a_log_e, dt_bias_e, norm_g_e, opre, tmat, states, dout, fused=None):
    s = hx.shape[0]
    n_chunks = s // CHUNK
    w = GDN_W
    n_slab = w // LANES
    gch = min(GDN_BWD_STEP_CHUNKS, n_chunks)
    n_blocks = n_chunks // gch
    rev = lambda i: n_blocks - 1 - i

    def body(hx_ref, hp_ref, cw_ref, al_ref, dt_ref, ng_ref, opre_ref, t_ref, st_ref, do_ref,
             dhx_ref, dab_ref, dcw_ref, dal_ref, ddt_ref, dng_ref, dstate, dy_next):
        i = pl.program_id(0)
        n = n_blocks - 1 - i

        @pl.when(i == 0)
        def _():
            dstate[...] = jnp.zeros_like(dstate)
            dy_next[...] = jnp.zeros_like(dy_next)
            for ref in (dcw_ref, dal_ref, ddt_ref, dng_ref):
                ref[...] = jnp.zeros_like(ref)

        hprev = hp_ref[...] * (n > 0).astype(F32)
        fr = _gdn_front(hx_ref, hprev, cw_ref, al_ref, dt_ref)
        tri = _slab_tri_masks()
        lower, strict = tri
        o = opre_ref[...]
        rinv = lax.rsqrt(_group_sum(o * o) * (1.0 / HEAD) + 1e-6)
        yn = o * rinv
        z = hx_ref[:, 3 * w:4 * w]
        sil, dsil_z = _silu_and_grad(z)
        dout_v = do_ref[...]
        ng = ng_ref[...]
        dng_ref[...] += jnp.sum(dout_v * yn * sil, axis=0, keepdims=True)
        dz = dout_v * yn * ng * dsil_z
        dyn = dout_v * ng * sil
        d_o = rinv * (dyn - yn * (_group_sum(dyn * yn) * (1.0 / HEAD)))
        last_row = (lax.broadcasted_iota(jnp.int32, (CHUNK, LANES), 0) == CHUNK - 1).astype(F32)
        bd = _block_diag_mask()
        gsum = _group_sum_slab
        t_all, st_all = t_ref[...], st_ref[...]
        dsns = [dstate[LANES * sl:LANES * (sl + 1), :] for sl in range(n_slab)]
        per_chunk = {}
        order = [(c_, s_) for c_ in reversed(range(gch)) for s_ in range(n_slab)]
        chain = {}
        for c, sl in order:
            lanes = slice(LANES * sl, LANES * (sl + 1))
            tok = slice(CHUNK * c, CHUNK * (c + 1))
            sq = _gdn_slab(fr, c, sl, tri, transposed=True)
            t = t_all[tok, lanes]
            st = st_all[w * c + LANES * sl:w * c + LANES * (sl + 1), :]
            dsn = dsns[sl]
            do_s = d_o[tok, lanes]
            u = _mm(t, _stack_heads(sq["v"] * sq["beta"]))
            wk = _mm(t, _stack_heads(sq["k"] * (sq["beta"] * sq["e"])))
            kt = sq["k"] * sq["xt"]
            dobd = _stack_heads(do_s)
            dvnew = _mm(sq["kqr"] * sq["dec_t"], dobd) + _mm(kt, dsn)
            dsns[sl] = _mm_tn(sq["q"] * sq["e"], do_s) * bd + sq["egl"] * dsn - _mm_tn(wk, dvnew) * bd
            chain[(c, sl)] = (sq, t, st, dsn, do_s, dobd, u, wk, kt, dvnew)
        for c, sl in order:
            sq, t, st, dsn, do_s, dobd, u, wk, kt, dvnew = chain[(c, sl)]
            k, q, v, beta, e, xt, dec, kk, qkr, kbd = (sq[n_] for n_ in ("k", "q", "v", "beta", "e", "xt", "dec", "kk", "qkr", "kbd"))
            dec_t, beta_t, kqr, qbd = sq["dec_t"], sq["beta_t"], sq["kqr"], sq["qbd"]
            t_t = _head_transpose(t)
            vnew = u - _mm(wk, st)
            dqd = _mm_nt(do_s, st)
            dqk = _mm_nt(do_s, _stack_heads(vnew))
            dqk_t = _mm_nt(vnew, dobd)
            dkt = _mm_nt(vnew, dsn)
            dgl = _head_total(dsn * st) * sq["egl"]
            dwk = -_mm_nt(dvnew, st)
            drv = _mm(t_t, _stack_heads(dvnew))
            drk = _mm(t_t, _stack_heads(dwk))
            dnm = jnp.where(strict, -(_mm_nt(drv, _stack_heads(u)) + _mm_nt(drk, _stack_heads(wk))), 0.0)
            dnm_t = jnp.where(sq["strict_t"], -(_mm_nt(u, _stack_heads(drv)) + _mm_nt(wk, _stack_heads(drk))), 0.0)
            dbeta = gsum(dnm * kk * dec)
            dkk = dnm * beta * dec
            dkk_t = dnm_t * beta_t * dec_t
            ddec = dnm * beta * kk + dqk * qkr
            dd_t = (dnm_t * beta_t * kk + dqk_t * kqr) * dec_t
            dq = _mm(dqk * dec, kbd) + dqd * e
            dk = _mm(dqk_t * dec_t, qbd) + _mm(dkk + dkk_t, kbd) + drk * (beta * e) + dkt * xt
            rks = gsum(drk * k)
            dbeta = dbeta + gsum(drv * v) + rks * e
            de = rks * beta + gsum(dqd * q)
            dxt = gsum(dkt * k) * xt
            dgl = dgl + jnp.sum(dxt, axis=0, keepdims=True)
            dgc = de * e - dxt + gsum(ddec * dec) - gsum(dd_t) + last_row * dgl
            per_chunk[(c, sl)] = dict(dq=dq * (HEAD ** -0.5), dk=dk, dv=drv * beta, dbeta=dbeta, dgc=dgc)
        for sl in range(n_slab):
            dstate[LANES * sl:LANES * (sl + 1), :] = dsns[sl]

        def block_of(name, suffix_sum=False):
            rows = []
            for c in range(gch):
                r = jnp.concatenate([per_chunk[(c, sl)][name] for sl in range(n_slab)], axis=1)
                rows.append(_rows_suffix_sum(r) if suffix_sum else r)
            return jnp.concatenate(rows, axis=0)

        dg = block_of("dgc", suffix_sum=True)
        dal_ref[...] += jnp.sum(dg * fr["g"], axis=0, keepdims=True)
        da = dg * fr["neg_a"] * _sigmoid(fr["sp_in"])
        ddt_ref[...] += jnp.sum(da, axis=0, keepdims=True)
        beta_all = fr["beta"]
        db = block_of("dbeta") * beta_all * (1.0 - beta_all)
        lane = lax.broadcasted_iota(jnp.int32, (CHUNK * gch, LANES), 1)
        dab = jnp.zeros((CHUNK * gch, LANES), F32)
        for hd in range(GDN_HEADS):
            dab = jnp.where(lane == hd, da[:, HEAD * hd:HEAD * hd + 1], dab)
            dab = jnp.where(lane == GDN_HEADS + hd, db[:, HEAD * hd:HEAD * hd + 1], dab)
        dab_ref[...] = dab.astype(BF16)
        dqn = block_of("dq")
        dkn = block_of("dk")
        dq_raw = fr["rq"] * (dqn - fr["qn"] * _group_sum(dqn * fr["qn"]))
        dk_raw = fr["rk"] * (dkn - fr["kn"] * _group_sum(dkn * fr["kn"]))
        dy = jnp.concatenate([dq_raw, dk_raw, block_of("dv")], axis=1) * fr["dsil"]
        nxt = dy_next[...]
        x = fr["x"]
        dx = cw_ref[3:4, :] * dy
        dcw_ref[3:4, :] += jnp.sum(dy * x, axis=0, keepdims=True)
        for j in (1, 2, 3):
            dx = dx + cw_ref[3 - j:4 - j, :] * _shift_rows_up(dy, nxt, j)
            dcw_ref[3 - j:4 - j, :] += jnp.sum(dy * _shift_rows(x, hprev, j), axis=0, keepdims=True)
        dy_next[...] = dy
        dhx_ref[:, 0:GDN_QKV] = dx.astype(BF16)
        dhx_ref[:, 3 * w:4 * w] = dz.astype(BF16)

    cst = lambda shape: pl.BlockSpec(shape, lambda i: (0, 0))
    row = lambda width: pl.BlockSpec((CHUNK * gch, width), lambda i: (rev(i), 0))
    buf = lambda width: pltpu.VMEM((CHUNK * gch, width), F32)
    f_in, f_out, f_scr, _ = fused if fused is not None else ([], [], [], None)
    outs = pl.pallas_call(
        _fuse(body, 10, 6, 2, n_blocks, fused), name="gdn_bwd" + ("_exchange" if fused is not None else ""), grid=(n_blocks,),
        in_specs=[row(GDN_IN), pl.BlockSpec((CHUNK * gch, GDN_QKV), lambda i: (jnp.maximum(rev(i) - 1, 0), 0)),
                  cst((4, GDN_QKV)), cst((1, w)), cst((1, w)), cst((1, w)), row(w), row(w),
                  pl.BlockSpec((w * gch, LANES), lambda i: (rev(i), 0)), row(w)] + [_ANY] * len(f_in),
        out_specs=[row(4 * w), row(LANES), cst((4, GDN_QKV)), cst((1, w)), cst((1, w)), cst((1, w))] + [_ANY] * len(f_out),
        out_shape=[jax.ShapeDtypeStruct((s, 4 * w), BF16), jax.ShapeDtypeStruct((s, LANES), BF16),
                   jax.ShapeDtypeStruct((4, GDN_QKV), F32),
                   jax.ShapeDtypeStruct((1, w), F32), jax.ShapeDtypeStruct((1, w), F32), jax.ShapeDtypeStruct((1, w), F32)]
        + list(f_out),
        scratch_shapes=[pltpu.VMEM((w, LANES), F32), buf(GDN_QKV)] + list(f_scr),
        compiler_params=_params("arbitrary"),
    )(hx, hx, conv_w, a_log_e, dt_bias_e, norm_g_e, opre, tmat, states, dout, *f_in)
    return tuple(outs[:6]) + ((list(outs[6:]),) if fused is not None else ())


_MESH = pl.DeviceIdType.MESH


def all_gather8(x, name):
    m, n = x.shape

    def body(x_ref, out_ref, send_sems, recv_sems, local_sem):
        px, py, pc = lax.axis_index("x"), lax.axis_index("y"), lax.axis_index("c")
        me, sibling = (px, py, pc), (px, py, 1 - pc)
        chips = [(1 - px, py), (px, 1 - py), (1 - px, 1 - py)]

        def slot(dx, dy, dc):
            return out_ref.at[4 * dx + 2 * dy + dc]

        def copy(k, block, to, src=None):
            return pltpu.make_async_remote_copy(
                src_ref=slot(*block) if src is None else src, dst_ref=slot(*block),
                send_sem=send_sems.at[k], recv_sem=recv_sems.at[k], device_id=to, device_id_type=_MESH)

        mine = pltpu.make_async_copy(x_ref, slot(*me), local_sem)
        mine.start()
        first = [copy(0, me, sibling, src=x_ref)]
        first += [copy(1 + j, me, (*chip, pc), src=x_ref) for j, chip in enumerate(chips)]
        for cp in first:
            cp.start()
        passed = [copy(4 + j, (*chip, pc), sibling) for j, chip in enumerate(chips)]
        for j, chip in enumerate(chips):
            copy(1 + j, (*chip, pc), me).wait_recv()
            passed[j].start()
        copy(0, sibling, me).wait_recv()
        for j, chip in enumerate(chips):
            copy(4 + j, (*chip, 1 - pc), me).wait_recv()
        for cp in first + passed:
            cp.wait_send()
        mine.wait()

    return pl.pallas_call(
        body, name=name, out_shape=jax.ShapeDtypeStruct((N_DEV, m, n), x.dtype),
        in_specs=[_ANY], out_specs=_ANY,
        scratch_shapes=[pltpu.SemaphoreType.DMA((7,)), pltpu.SemaphoreType.DMA((7,)), pltpu.SemaphoreType.DMA],
    )(x)


def _weight_gather_steps(s0, s1, f0, f1, sems):
    n0 = len(s0)
    own_send, own_recv, ici_send, ici_recv, fwd_send, fwd_recv = sems
    px, py, pc = lax.axis_index("x"), lax.axis_index("y"), lax.axis_index("c")
    mine = 2 * px + py
    sibling = (px, py, 1 - pc)
    chips = [(1 - px, py), (px, 1 - py), (1 - px, 1 - py)]

    def copy(src, dst, sems_s, sems_r, k, to):
        return pltpu.make_async_remote_copy(src_ref=src, dst_ref=dst, send_sem=sems_s.at[k], recv_sem=sems_r.at[k],
                                            device_id=to, device_id_type=_MESH)

    def own_copies():
        return [copy(shards[i], full[i].at[mine], own_send, own_recv, base + i, sibling)
                for base, shards, full in ((0, s0, f0), (n0, s1, f1)) for i in range(len(shards))]

    def first_copies(my_shards, my_full):
        ici = [copy(my_shards[i], my_full[i].at[mine], ici_send, ici_recv, 3 * i + j, (cx, cy, pc))
               for i in range(len(my_shards)) for j, (cx, cy) in enumerate(chips)]
        return own_copies() + ici

    def begin(my_shards, my_full):
        for cp in first_copies(my_shards, my_full):
            cp.start()

    def end(my_shards, my_full, other_full):
        fwd = []
        for i in range(len(my_shards)):
            for j, (cx, cy) in enumerate(chips):
                slot = my_full[i].at[2 * cx + cy]
                copy(my_shards[i], slot, ici_send, ici_recv, 3 * i + j, (cx, cy, pc)).wait_recv()
                cp = copy(slot, slot, fwd_send, fwd_recv, 3 * i + j, sibling)
                cp.start()
                fwd.append(cp)
        for cp in own_copies():
            cp.wait_recv()
        for i in range(len(other_full)):
            for j, (cx, cy) in enumerate(chips):
                slot = other_full[i].at[2 * cx + cy]
                copy(slot, slot, fwd_send, fwd_recv, 3 * i + j, sibling).wait_recv()
        for cp in first_copies(my_shards, my_full) + fwd:
            cp.wait_send()

    def start():
        @pl.when(pc == 0)
        def _():
            begin(s0, f0)

        @pl.when(pc == 1)
        def _():
            begin(s1, f1)

    def finish():
        @pl.when(pc == 0)
        def _():
            end(s0, f0, f1)

        @pl.when(pc == 1)
        def _():
            end(s1, f1, f0)

    return start, finish


def _weight_gather_operands(shards0, shards1):
    both, most = len(shards0) + len(shards1), max(len(shards0), len(shards1))
    full = [jax.ShapeDtypeStruct((N_CHIPS,) + v.shape, v.dtype) for v in list(shards0) + list(shards1)]
    dma = pltpu.SemaphoreType.DMA
    return full, [dma((both,)), dma((both,)), dma((3 * most,)), dma((3 * most,)), dma((3 * most,)), dma((3 * most,))]


def _weight_gather_fused(shards0, shards1):
    n0, both = len(shards0), len(shards0) + len(shards1)
    full, sems = _weight_gather_operands(shards0, shards1)
    steps = lambda ins, outs, scr: _weight_gather_steps(ins[:n0], ins[n0:both], outs[:n0], outs[n0:both], scr)
    return list(shards0) + list(shards1), full, sems, steps


def gather_layer_weights(shards0, shards1, name):
    ins, full, sems, steps = _weight_gather_fused(shards0, shards1)
    both = len(ins)

    def body(*refs):
        start, finish = steps(refs[0:both], refs[both:2 * both], refs[2 * both:])
        start()
        finish()

    return pl.pallas_call(
        body, name=name, out_shape=full, in_specs=[_ANY] * both, out_specs=[_ANY] * both, scratch_shapes=sems,
    )(*ins)


def _piece_offsets(pieces):
    offs = [0]
    for r, _ in pieces:
        offs.append(offs[-1] + r)
    return offs


def _chip_exchange_steps(srcs, q_ref, sems, pieces, owner):
    send_sems, recv_sems = sems
    offs = _piece_offsets(pieces)
    px, py, pc = lax.axis_index("x"), lax.axis_index("y"), lax.axis_index("c")
    mine = 2 * px + py
    chips = [(1 - px, py), (px, 1 - py), (1 - px, 1 - py)]

    def copies():
        sends = []
        for i, (r, stride) in enumerate(pieces):
            dst = pl.ds(offs[i], r)
            for j, (cx, cy) in enumerate(chips):
                sends.append(pltpu.make_async_remote_copy(
                    src_ref=srcs[i].at[pl.ds((2 * cx + cy) * stride, r)], dst_ref=q_ref.at[mine, dst],
                    send_sem=send_sems.at[3 * i + j], recv_sem=recv_sems.at[3 * i + j], device_id=(cx, cy, pc),
                    device_id_type=_MESH))
        return sends

    def start():
        @pl.when(pc == owner)
        def _():
            for cp in copies():
                cp.start()

    def finish():
        @pl.when(pc == owner)
        def _():
            for i, (r, stride) in enumerate(pieces):
                dst = pl.ds(offs[i], r)
                for j, (cx, cy) in enumerate(chips):
                    pltpu.make_async_remote_copy(
                        src_ref=srcs[i].at[pl.ds(mine * stride, r)], dst_ref=q_ref.at[2 * cx + cy, dst],
                        send_sem=send_sems.at[3 * i + j], recv_sem=recv_sems.at[3 * i + j], device_id=(cx, cy, pc),
                        device_id_type=_MESH).wait_recv()
            for cp in copies():
                cp.wait_send()

    return start, finish


def _chip_exchange_operands(arrays, pieces):
    n = len(pieces)
    dma = pltpu.SemaphoreType.DMA
    q = jax.ShapeDtypeStruct((N_CHIPS, _piece_offsets(pieces)[-1], arrays[0].shape[1]), arrays[0].dtype)
    return q, [dma((3 * n,)), dma((3 * n,))]


def _own_share(arrays, pieces, chip):
    return jnp.concatenate([lax.dynamic_slice_in_dim(arr, chip * stride, r, axis=0) for arr, (r, stride) in zip(arrays, pieces)],
                           axis=0)


def chip_exchange(arrays, pieces, owner, name):
    n = len(pieces)

    def body(*refs):
        start, finish = _chip_exchange_steps(refs[0:n], refs[n], refs[n + 1:], pieces, owner)
        start()
        finish()

    q, sems = _chip_exchange_operands(arrays, pieces)
    return pl.pallas_call(body, name=name, out_shape=q, in_specs=[_ANY] * n, out_specs=_ANY, scratch_shapes=sems)(*arrays)


def sibling_send(arrays, to_core, name):
    n = len(arrays)

    def body(*refs):
        srcs, outs = refs[0:n], refs[n:2 * n]
        send_sems, recv_sems = refs[2 * n:]
        px, py, pc = lax.axis_index("x"), lax.axis_index("y"), lax.axis_index("c")
        cps = [pltpu.make_async_remote_copy(
            src_ref=srcs[i], dst_ref=outs[i], send_sem=send_sems.at[i], recv_sem=recv_sems.at[i],
            device_id=(px, py, to_core), device_id_type=_MESH) for i in range(n)]

        @pl.when(pc != to_core)
        def _():
            for cp in cps:
                cp.start()
            for cp in cps:
                cp.wait_send()

        @pl.when(pc == to_core)
        def _():
            for cp in cps:
                cp.wait_recv()

    return pl.pallas_call(
        body, name=name, out_shape=[jax.ShapeDtypeStruct(v.shape, v.dtype) for v in arrays],
        in_specs=[_ANY] * n, out_specs=[_ANY] * n,
        scratch_shapes=[pltpu.SemaphoreType.DMA((n,)), pltpu.SemaphoreType.DMA((n,))],
    )(*arrays)


def _fuse(body, n_in, n_out, n_scratch, n_steps, fused):
    if fused is None:
        return body
    f_in, f_out, f_scr, steps = fused
    a, b, c = len(f_in), len(f_out), len(f_scr)

    def wrapped(*refs):
        ins, rest = refs[:n_in + a], refs[n_in + a:]
        outs, scr = rest[:n_out + b], rest[n_out + b:]
        start, finish = steps(ins[n_in:], outs[n_out:], scr[n_scratch:])
        step = pl.program_id(0)

        @pl.when(step == 0)
        def _():
            start()

        body(*ins[:n_in], *outs[:n_out], *scr[:n_scratch])

        @pl.when(step == n_steps - 1)
        def _():
            finish()

    return wrapped


def sibling_swap(x, name):
    def body(x_ref, out_ref, send_sem, recv_sem):
        px, py, pc = lax.axis_index("x"), lax.axis_index("y"), lax.axis_index("c")
        cp = pltpu.make_async_remote_copy(
            src_ref=x_ref, dst_ref=out_ref, send_sem=send_sem, recv_sem=recv_sem,
            device_id=(px, py, 1 - pc), device_id_type=_MESH)
        cp.start()
        cp.wait()

    return pl.pallas_call(
        body, name=name, out_shape=jax.ShapeDtypeStruct(x.shape, x.dtype), in_specs=[_ANY], out_specs=_ANY,
        scratch_shapes=[pltpu.SemaphoreType.DMA, pltpu.SemaphoreType.DMA],
    )(x)


ELT_TILE = 128


def _elt_rows(m):
    for t in (512, 256, ELT_TILE, 16, 8):
        if m % t == 0:
            return t
    return m


def pair_add(a, b, name):
    m, n = b.shape
    tm = _elt_rows(m)

    def body(a_ref, b_ref, o_ref):
        o_ref[...] = (a_ref[...].astype(F32) + b_ref[...].astype(F32)).astype(o_ref.dtype)

    return pl.pallas_call(
        body, name=name, grid=(m // tm,), in_specs=[_row_spec(tm, n)] * 2, out_specs=_row_spec(tm, n),
        out_shape=jax.ShapeDtypeStruct((m, n), b.dtype), compiler_params=_params("arbitrary"),
    )(a, b)


def sum_leading(q, name):
    kk, m, n = q.shape
    tm = _elt_rows(m)

    def body(q_ref, o_ref):
        acc = q_ref[0].astype(F32)
        for i in range(1, kk):
            acc = acc + q_ref[i].astype(F32)
        o_ref[...] = acc

    return pl.pallas_call(
        body, name=name, grid=(m // tm,), in_specs=[pl.BlockSpec((kk, tm, n), lambda i: (0, i, 0))],
        out_specs=_row_spec(tm, n), out_shape=jax.ShapeDtypeStruct((m, n), F32), compiler_params=_params("arbitrary"),
    )(q)


def sum_shares(q0, q1, own0, own1, name):
    _, m, n = q0.shape
    tm = _elt_rows(m)

    def body(q0_ref, q1_ref, o0_ref, o1_ref, out_ref):
        first = lax.axis_index("c") == 0
        mine = 2 * lax.axis_index("x") + lax.axis_index("y")
        own = jnp.where(first, o0_ref[...], o1_ref[...])
        acc = None
        for k in range(N_CHIPS):
            term = jnp.where(mine == k, own, jnp.where(first, q0_ref[k], q1_ref[k])).astype(F32)
            acc = term if acc is None else acc + term
        out_ref[...] = acc

    slots = pl.BlockSpec((N_CHIPS, tm, n), lambda i: (0, i, 0))
    return pl.pallas_call(
        body, name=name, grid=(m // tm,), in_specs=[slots, slots, _row_spec(tm, n), _row_spec(tm, n)],
        out_specs=_row_spec(tm, n), out_shape=jax.ShapeDtypeStruct((m, n), F32), compiler_params=_params("arbitrary"),
    )(q0, q1, own0, own1)


def adamw(w, g, m, v, name):
    rows, cols = w.shape
    tm = _elt_rows(rows)

    def body(w_ref, g_ref, m_ref, v_ref, d_ref, nm_ref, nv_ref):
        gv = g_ref[...]
        nm = ADAM_B1 * m_ref[...] + (1.0 - ADAM_B1) * gv
        nv = ADAM_B2 * v_ref[...] + (1.0 - ADAM_B2) * jnp.square(gv)
        nm_ref[...] = nm
        nv_ref[...] = nv
        m_hat = nm / (1.0 - ADAM_B1 ** ADAM_STEP)
        v_hat = nv / (1.0 - ADAM_B2 ** ADAM_STEP)
        d_ref[...] = -ADAM_LR * (m_hat / (jnp.sqrt(v_hat) + ADAM_EPS) + ADAM_WD * w_ref[...])

    spec = _row_spec(tm, cols)
    return pl.pallas_call(
        body, name=name, grid=(rows // tm,), in_specs=[spec] * 4, out_specs=[spec] * 3,
        out_shape=[jax.ShapeDtypeStruct((rows, cols), F32)] * 3, compiler_params=_params("arbitrary"),
    )(w, g, m, v)


def _block_diag_dense(w):
    g = w.shape[0]
    return jnp.einsum("gij,gh->gihj", w, jnp.eye(g, dtype=w.dtype)).reshape(g * w.shape[1], g * w.shape[2])


def _diag_blocks(m):
    return jnp.stack([m[HEAD * i:HEAD * (i + 1), HEAD * i:HEAD * (i + 1)] for i in range(LRU_BLOCKS)])


def _rep(v):
    return jnp.repeat(v, HEAD, axis=-1)


def _split_w_in(w_in):
    gdn0 = RET_IN + LRU_IN
    gdn1 = gdn0 + 4 * GDN_W
    w_r = w_in[:, 0:RET_IN]
    w_l = w_in[:, RET_IN:gdn0]
    w_g = jnp.concatenate([w_in[:, gdn0:gdn1], _rep(w_in[:, gdn1:gdn1 + GDN_HEADS]), _rep(w_in[:, gdn1 + GDN_HEADS:])], axis=1)
    w_ab = jnp.pad(w_in[:, gdn1:], ((0, 0), (0, LANES - 2 * GDN_HEADS)))
    return w_r, w_l, w_g, w_ab


WIN_SHARD = D_IN // N_CHIPS
WIN_STRIDE = 832
WIN_ROWS = 960
WIN_T_ROWS = WIN_STRIDE * (N_CHIPS - 1) + WIN_ROWS
AB_ROWS = 16

_GRAD_PIECES = (("ffn1_w_gate", 704, 704), ("ffn1_w_up", 704, 704), ("ffn1_w_down", 704, 704), ("w_in", WIN_ROWS, WIN_STRIDE),
                ("w_out", 256, 256), ("ffn2_w_gate", 704, 704), ("ffn2_w_up", 704, 704), ("ffn2_w_down", 704, 704),
                ("ple_w_gate", 256, 256), ("ple_w_proj", 64, 64))
_TRANSPOSED = ("ffn1_w_gate", "ffn1_w_up", "w_in", "ffn2_w_gate", "ffn2_w_up", "ple_w_proj")


def _local_step(x, p, pos, target, wt, mesh=None):
    row = lambda v: v[None, :]
    saved = []
    xb = x.astype(BF16)
    pieces = [(r, stride) for _, r, stride in _GRAD_PIECES]
    grad_names = [n for n, _, _ in _GRAD_PIECES]
    for i in range(DEPTH):
        ffn1 = (wt["ffn1_w_gate"][i], wt["ffn1_w_up"][i], wt["ffn1_w_down"][i])
        if mesh is not None and i == 0:
            half0, half1, make = mesh["rest0"]
            hg1, hu1, r1, x1, x1b, gathered = ffn_fwd(x, row(wt["ln_ffn1_g"][i]), row(wt["ln_ffn1_b"][i]), *ffn1,
                                                      fused=_weight_gather_fused(half0, half1))
            wt = {**wt, **{n: [w0, None] for n, w0 in make(gathered).items()}}
        else:
            hg1, hu1, r1, x1, x1b = ffn_fwd(x, row(wt["ln_ffn1_g"][i]), row(wt["ln_ffn1_b"][i]), *ffn1)
        w_r, w_l, w_g, w_ab = _split_w_in(wt["w_in"][i])
        lw = dict(
            wg1=ffn1[0], wu1=ffn1[1], wd1=ffn1[2], w_r=w_r, w_l=w_l, w_g=w_g, w_ab=w_ab,
            w_out=wt["w_out"][i], wg2=wt["ffn2_w_gate"][i], wu2=wt["ffn2_w_up"][i], wd2=wt["ffn2_w_down"][i],
            wpg=wt["ple_w_gate"][i], wpp=wt["ple_w_proj"][i],
            wa=_block_diag_dense(wt["lru_w_a"][i]), wx=_block_diag_dense(wt["lru_w_x"][i]),
            al=row(_rep(wt["gdn_a_log"][i])), dt=row(_rep(wt["gdn_dt_bias"][i])), ng=row(jnp.tile(wt["gdn_norm_g"][i], GDN_HEADS)))
        hr, hl, hgd = win_fwd(x1, w_r, w_l, w_g)
        o_r, opre_r, st_r = ret_fwd(hr, pos, row(wt["ret_norm_g"][i]))
        o_l, xc, hs = lru_fwd(hl, wt["lru_conv_w"][i], row(wt["lru_conv_b"][i]), lw["wa"], row(wt["lru_b_a"][i]), lw["wx"],
                              row(wt["lru_b_x"][i]), row(wt["lru_lambda"][i]))
        if mesh is not None and i == 0:
            half0, half1, make = mesh["layer1"]
            o_g, opre_g, tmat, st_g, gathered = gdn_fwd(hgd, wt["gdn_conv_w"][i], lw["al"], lw["dt"], lw["ng"],
                                                        fused=_weight_gather_fused(half0, half1))
            wt = {**wt, **{n: [wt[n][0], w1] for n, w1 in make(gathered).items()}}
        else:
            o_g, opre_g, tmat, st_g = gdn_fwd(hgd, wt["gdn_conv_w"][i], lw["al"], lw["dt"], lw["ng"])
        r2, x2, x2b, ocat = out_fwd(o_r, o_l, o_g, x1, lw["w_out"], row(wt["ln_mix_g"][i]), row(wt["ln_mix_b"][i]))
        hg2, hu2, r3, x3, x3b, pg, pp = ffn_fwd(x2, row(wt["ln_ffn2_g"][i]), row(wt["ln_ffn2_b"][i]), lw["wg2"], lw["wu2"],
                                                lw["wd2"], ple=(p[i], lw["wpg"], lw["wpp"]))
        saved.append(dict(lw=lw, x0=xb, hg1=hg1, hu1=hu1, r1=r1, x1=x1b, hr=hr, hl=hl, hgd=hgd, ocat=ocat, opre_r=opre_r,
                          st_r=st_r, xc=xc, hs=hs, opre_g=opre_g, tmat=tmat, st_g=st_g, r2=r2, x2=x2b, hg2=hg2, hu2=hu2,
                          r3=r3, pg=pg, pp=pp))
        x, xb = x3, x3b

    dx, loss = loss_and_grad(x, target)
    grads = [None] * DEPTH
    big = [None] * DEPTH
    pair_sums = [None] * DEPTH
    for i in reversed(range(DEPTH)):
        sv = saved[i]
        lw = sv["lw"]
        tag = f"_l{i}"
        dx2, act2, dhg2, dhu2, dy2, dg3, db3, dpg, dpp = ffn_bwd(
            dx, sv["r3"], sv["x2"], sv["hg2"], sv["hu2"], row(wt["ln_ffn2_g"][i]), lw["wg2"], lw["wu2"], lw["wd2"],
            ple=(sv["pg"], sv["pp"], lw["wpg"]))
        g, bg = {}, {}
        bg["ffn2_w_gate"] = wgrad(dhg2, sv["x2"], "wgrad_gate2" + tag)
        bg["ffn2_w_up"] = wgrad(dhu2, sv["x2"], "wgrad_up2" + tag)
        bg["ffn2_w_down"] = wgrad(act2, dy2, "wgrad_down2" + tag)
        bg["ple_w_gate"] = wgrad(sv["x2"], dpg, "wgrad_pgate" + tag)
        bg["ple_w_proj"] = wgrad(dpp, p[i], "wgrad_pproj" + tag).reshape(PLE_DIM, D_MODEL)
        g["ln_ffn2_g"], g["ln_ffn2_b"] = dg3[0], db3[0]
        dr2, dr2b, do_r, do_l, do_g, dg2, db2 = out_bwd(dx2, sv["r2"], row(wt["ln_mix_g"][i]), lw["w_out"])
        g["ln_mix_g"], g["ln_mix_b"] = dg2[0], db2[0]
        bg["w_out"] = wgrad(sv["ocat"], dr2b, "wgrad_out" + tag)
        dhr, dgn = ret_bwd(sv["hr"], pos, row(wt["ret_norm_g"][i]), sv["opre_r"], sv["st_r"], do_r)
        g["ret_norm_g"] = dgn[0]
        dhl, dcw, dcb, dwa, dba, dwx, dbx, dlam = lru_bwd(
            sv["hl"], wt["lru_conv_w"][i], row(wt["lru_conv_b"][i]), lw["wa"], row(wt["lru_b_a"][i]), lw["wx"],
            row(wt["lru_b_x"][i]), row(wt["lru_lambda"][i]), sv["xc"], sv["hs"], do_l)
        g["lru_conv_w"], g["lru_conv_b"] = dcw, dcb[0]
        g["lru_w_a"], g["lru_b_a"], g["lru_w_x"], g["lru_b_x"], g["lru_lambda"] = _diag_blocks(dwa), dba[0], _diag_blocks(dwx), dbx[0], dlam[0]
        if mesh is not None and i == 0:
            q_shape, sems = _chip_exchange_operands(pair_sums[1], pieces)
            steps = lambda ins, outs, scr: _chip_exchange_steps(ins, outs[0], scr, pieces, 1)
            dhq, dab, dgcw, dal, ddt, dng, arrived = gdn_bwd(
                sv["hgd"], wt["gdn_conv_w"][i], lw["al"], lw["dt"], lw["ng"], sv["opre_g"], sv["tmat"], sv["st_g"], do_g,
                fused=(pair_sums[1], [q_shape], sems, steps))
            big[1] = arrived[0]
        else:
            dhq, dab, dgcw, dal, ddt, dng = gdn_bwd(sv["hgd"], wt["gdn_conv_w"][i], lw["al"], lw["dt"], lw["ng"], sv["opre_g"],
                                                    sv["tmat"], sv["st_g"], do_g)
        g["gdn_conv_w"] = dgcw
        g["gdn_a_log"], g["gdn_dt_bias"] = dal[0, ::HEAD], ddt[0, ::HEAD]
        g["gdn_norm_g"] = dng[0].reshape(GDN_HEADS, HEAD).sum(0)
        dx1 = win_bwd(dr2, dhr, dhl, dhq, dab, lw["w_r"], lw["w_l"], lw["w_g"], lw["w_ab"])
        used = RET_IN + LRU_IN + 4 * GDN_W + AB_ROWS
        bg["w_in"] = jnp.concatenate(
            [wgrad(dhr, sv["x1"], "wgrad_in_r" + tag), wgrad(dhl, sv["x1"], "wgrad_in_l" + tag),
             wgrad(dhq, sv["x1"], "wgrad_in_q" + tag), wgrad(dab, sv["x1"], "wgrad_in_ab" + tag)[0:AB_ROWS],
             jnp.zeros((WIN_T_ROWS - used, D_MODEL), BF16)], axis=0)
        dx, act1, dhg1, dhu1, dy1, dg1, db1 = ffn_bwd(dx1, sv["r1"], sv["x0"], sv["hg1"], sv["hu1"], row(wt["ln_ffn1_g"][i]),
                                                      lw["wg1"], lw["wu1"], lw["wd1"])
        bg["ffn1_w_gate"] = wgrad(dhg1, sv["x0"], "wgrad_gate1" + tag)
        bg["ffn1_w_up"] = wgrad(dhu1, sv["x0"], "wgrad_up1" + tag)
        bg["ffn1_w_down"] = wgrad(act1, dy1, "wgrad_down1" + tag)
        g["ln_ffn1_g"], g["ln_ffn1_b"] = dg1[0], db1[0]
        grads[i] = g
        if mesh is None:
            big[i] = bg
        else:
            mine = [bg[n] for n in grad_names]
            theirs = sibling_send(mine, i, f"reduce_pair_send_l{i}")
            pair_sums[i] = [pair_add(u, v, f"reduce_pair_add_l{i}_" + n) for n, u, v in zip(grad_names, mine, theirs)]
            if i == 0:
                big[0] = chip_exchange(pair_sums[0], pieces, 0, "reduce_chip_exchange_l0")
    if mesh is not None:
        chip = 2 * lax.axis_index("x") + lax.axis_index("y")
        big = (big, [_own_share(pair_sums[layer], pieces, chip) for layer in range(DEPTH)])
    return loss, dx, {k: jnp.stack([grads[i][k] for i in range(DEPTH)]) for k in grads[0]}, big


def _natural_grad(name, rows):
    if name == "ple_w_proj":
        return rows.reshape(-1, PLE_DIM).T
    return rows.T if name in _TRANSPOSED else rows


_SPLIT = dict(ffn1_w_gate=2, ffn1_w_up=2, ffn1_w_down=1, w_in=2, w_out=1, ffn2_w_gate=2, ffn2_w_up=2, ffn2_w_down=1,
              ple_w_gate=1, ple_w_proj=2)
_CONV = ("lru_conv_w", "gdn_conv_w")
_WHOLE = ("ln_ffn1_g", "ln_ffn1_b", "ret_norm_g", "lru_conv_b", "lru_w_a", "lru_b_a", "lru_w_x", "lru_b_x", "lru_lambda",
          "gdn_a_log", "gdn_dt_bias", "gdn_norm_g", "ln_mix_g", "ln_mix_b", "ln_ffn2_g", "ln_ffn2_b")
_WEIGHTS = ("ln_ffn1_g", "ln_ffn1_b", "ffn1_w_gate", "ffn1_w_up", "ffn1_w_down", "w_in", "ret_norm_g", "lru_conv_w", "lru_conv_b",
            "lru_w_a", "lru_b_a", "lru_w_x", "lru_b_x", "lru_lambda", "gdn_conv_w", "gdn_a_log", "gdn_dt_bias", "gdn_norm_g",
            "w_out", "ln_mix_g", "ln_mix_b", "ffn2_w_gate", "ffn2_w_up", "ffn2_w_down", "ple_w_gate", "ple_w_proj",
            "ln_ffn2_g", "ln_ffn2_b")
_INPUTS = ("x", "p", "positions") + _WEIGHTS + ("loss_target",) + tuple("m_" + n for n in _WEIGHTS) + tuple("v_" + n for n in _WEIGHTS)

BIG_COLS = 1024
SMALL_COLS = LANES
SMALL_ROWS_MULT = 8


def _pack(arrays, dtype, cols, rows_mult):
    flat = jnp.concatenate([a.reshape(-1).astype(dtype) for a in arrays])
    rows = -(-flat.shape[0] // cols)
    rows = -(-rows // rows_mult) * rows_mult
    return jnp.pad(flat, (0, rows * cols - flat.shape[0])).reshape(rows, cols)


def _unpack(packed, shapes):
    flat = packed.reshape(-1)
    out, off = [], 0
    for shp in shapes:
        size = int(np.prod(shp))
        out.append(flat[off:off + size].reshape(shp))
        off += size
    return out


def _as2d(a):
    return a.reshape(-1, a.shape[-1])


def kernel(x, p, positions, ln_ffn1_g, ln_ffn1_b, ffn1_w_gate, ffn1_w_up, ffn1_w_down, w_in, ret_norm_g, lru_conv_w, lru_conv_b, lru_w_a, lru_b_a, lru_w_x, lru_b_x, lru_lambda, gdn_conv_w, gdn_a_log, gdn_dt_bias, gdn_norm_g, w_out, ln_mix_g, ln_mix_b, ffn2_w_gate, ffn2_w_up, ffn2_w_down, ple_w_gate, ple_w_proj, ln_ffn2_g, ln_ffn2_b, loss_target, m_ln_ffn1_g, m_ln_ffn1_b, m_ffn1_w_gate, m_ffn1_w_up, m_ffn1_w_down, m_w_in, m_ret_norm_g, m_lru_conv_w, m_lru_conv_b, m_lru_w_a, m_lru_b_a, m_lru_w_x, m_lru_b_x, m_lru_lambda, m_gdn_conv_w, m_gdn_a_log, m_gdn_dt_bias, m_gdn_norm_g, m_w_out, m_ln_mix_g, m_ln_mix_b, m_ffn2_w_gate, m_ffn2_w_up, m_ffn2_w_down, m_ple_w_gate, m_ple_w_proj, m_ln_ffn2_g, m_ln_ffn2_b, v_ln_ffn1_g, v_ln_ffn1_b, v_ffn1_w_gate, v_ffn1_w_up, v_ffn1_w_down, v_w_in, v_ret_norm_g, v_lru_conv_w, v_lru_conv_b, v_lru_w_a, v_lru_b_a, v_lru_w_x, v_lru_b_x, v_lru_lambda, v_gdn_conv_w, v_gdn_a_log, v_gdn_dt_bias, v_gdn_norm_g, v_w_out, v_ln_mix_g, v_ln_mix_b, v_ffn2_w_gate, v_ffn2_w_up, v_ffn2_w_down, v_ple_w_gate, v_ple_w_proj, v_ln_ffn2_g, v_ln_ffn2_b):
    a = dict(zip(_INPUTS, (x, p, positions, ln_ffn1_g, ln_ffn1_b, ffn1_w_gate, ffn1_w_up, ffn1_w_down, w_in, ret_norm_g, lru_conv_w, lru_conv_b, lru_w_a, lru_b_a, lru_w_x, lru_b_x, lru_lambda, gdn_conv_w, gdn_a_log, gdn_dt_bias, gdn_norm_g, w_out, ln_mix_g, ln_mix_b, ffn2_w_gate, ffn2_w_up, ffn2_w_down, ple_w_gate, ple_w_proj, ln_ffn2_g, ln_ffn2_b, loss_target, m_ln_ffn1_g, m_ln_ffn1_b, m_ffn1_w_gate, m_ffn1_w_up, m_ffn1_w_down, m_w_in, m_ret_norm_g, m_lru_conv_w, m_lru_conv_b, m_lru_w_a, m_lru_b_a, m_lru_w_x, m_lru_b_x, m_lru_lambda, m_gdn_conv_w, m_gdn_a_log, m_gdn_dt_bias, m_gdn_norm_g, m_w_out, m_ln_mix_g, m_ln_mix_b, m_ffn2_w_gate, m_ffn2_w_up, m_ffn2_w_down, m_ple_w_gate, m_ple_w_proj, m_ln_ffn2_g, m_ln_ffn2_b, v_ln_ffn1_g, v_ln_ffn1_b, v_ffn1_w_gate, v_ffn1_w_up, v_ffn1_w_down, v_w_in, v_ret_norm_g, v_lru_conv_w, v_lru_conv_b, v_lru_w_a, v_lru_b_a, v_lru_w_x, v_lru_b_x, v_lru_lambda, v_gdn_conv_w, v_gdn_a_log, v_gdn_dt_bias, v_gdn_norm_g, v_w_out, v_ln_mix_g, v_ln_mix_b, v_ffn2_w_gate, v_ffn2_w_up, v_ffn2_w_down, v_ple_w_gate, v_ple_w_proj, v_ln_ffn2_g, v_ln_ffn2_b)))
    assert len(a) == len(_INPUTS)
    core = lax.axis_index("c")
    chip = 2 * lax.axis_index("x") + lax.axis_index("y")
    big = list(_SPLIT)

    def group(layer, names, n_first):
        shards = [a[n][layer].astype(BF16) for n in names]

        def make(gathered):
            return {n: jnp.concatenate([gathered[i][k] for k in range(N_CHIPS)], axis=_SPLIT[n] - 1) for i, n in enumerate(names)}

        return shards[:n_first], shards[n_first:], make

    ffn1_0, ffn1_1, make_ffn1 = group(0, big[:3], 2)
    wt = {n: [w0, None] for n, w0 in make_ffn1(gather_layer_weights(ffn1_0, ffn1_1, "gather_weights_ffn1_l0")).items()}
    conv_g = all_gather8(_pack([a[n] for n in _CONV], F32, SMALL_COLS, SMALL_ROWS_MULT), "gather_conv_weights")[0::2]
    conv_g = conv_g.reshape(N_CHIPS, -1)
    off = 0
    for n in _CONV:
        shp = a[n].shape
        size = int(np.prod(shp))
        parts = conv_g[:, off:off + size].reshape((N_CHIPS,) + shp)
        wt[n] = jnp.concatenate([parts[k] for k in range(N_CHIPS)], axis=2)
        off += size
    for n in _WHOLE:
        wt[n] = a[n]

    seq = a["x"].shape[1]
    mesh = dict(rest0=group(0, big[3:], 4), layer1=group(1, big, len(big) // 2))
    loss_part, dx, grads, (arrived, own) = _local_step(a["x"][0], a["p"][:, 0], a["positions"].reshape(seq, 1),
                                                       a["loss_target"][0], wt, mesh=mesh)
    loss = lax.psum(loss_part[0, 0], ("x", "y", "c"))

    my_layer_sum = sum_shares(arrived[0], arrived[1], own[0], own[1], "reduce_chip_sum")
    other_layer_sum = sibling_swap(my_layer_sum, "reduce_pair_share")
    reduced = [jnp.where(core == layer, my_layer_sum, other_layer_sum) for layer in range(DEPTH)]
    big_grads = {}
    off = 0
    for n, r, _ in _GRAD_PIECES:
        per_layer = []
        for layer in range(DEPTH):
            rows = reduced[layer][off:off + r]
            if n == "w_in":
                rows = lax.dynamic_slice_in_dim(rows, chip * (WIN_SHARD - WIN_STRIDE), WIN_SHARD, axis=0)
            per_layer.append(_natural_grad(n, rows))
        big_grads[n] = jnp.stack(per_layer)
        off += r

    small_names = list(_WHOLE) + list(_CONV)
    small_local = _pack([grads[n] for n in small_names], F32, SMALL_COLS, SMALL_ROWS_MULT)
    small_sum = sum_leading(all_gather8(small_local, "gather_small_grads"), "sum_small_grads")
    small_grads = dict(zip(small_names, _unpack(small_sum, [grads[n].shape for n in small_names])))
    for n in _CONV:
        width = a[n].shape[2]
        small_grads[n] = lax.dynamic_slice_in_dim(small_grads[n], chip * width, width, axis=2)

    new = {}
    for n in big:
        d, nm, nv = adamw(_as2d(a[n]), _as2d(big_grads[n]), _as2d(a["m_" + n]), _as2d(a["v_" + n]), "adamw_" + n)
        new[n] = tuple(t.reshape(a[n].shape) for t in (d, nm, nv))
    pk = lambda prefix: _pack([a[prefix + n] for n in small_names], F32, SMALL_COLS, SMALL_ROWS_MULT)
    pg = _pack([small_grads[n] for n in small_names], F32, SMALL_COLS, SMALL_ROWS_MULT)
    outs = adamw(pk(""), pg, pk("m_"), pk("v_"), "adamw_small")
    shapes = [a[n].shape for n in small_names]
    for n, d, nm, nv in zip(small_names, *[_unpack(o, shapes) for o in outs]):
        new[n] = (d, nm, nv)
    all_grads = {**big_grads, **small_grads}
    return (loss, dx[None], *[all_grads[n] for n in _WEIGHTS], *[new[n][0] for n in _WEIGHTS],
            *[new[n][1] for n in _WEIGHTS], *[new[n][2] for n in _WEIGHTS])
```

```python
import functools
import math

import numpy as np
import jax
import jax.numpy as jnp
from jax import lax
from jax.experimental import pallas as pl
from jax.experimental.pallas import tpu as pltpu

F32 = jnp.float32
BF16 = jnp.bfloat16

D_MODEL = 1024
D_FF = 2816
PLE_DIM = 256
DEPTH = 2
CHUNK = 64
RET_HEADS = 4
RET_W = 256
LRU_W = 384
LRU_BLOCKS = 6
GDN_HEADS = 6
GDN_W = 384
HEAD = 64
D_IN = 3340
RET_IN = 4 * RET_W
LRU_IN = 2 * LRU_W
GDN_IN = 6 * GDN_W
ROPE_THETA = 10000.0
ALPHA = (2 * DEPTH) ** 0.25
LN_EPS = 1e-5
LRU_C = 8.0
N_CHIPS = 4
N_DEV = 8

ADAM_LR = 0.001
ADAM_B1 = 0.9
ADAM_B2 = 0.999
ADAM_EPS = 1e-08
ADAM_WD = 0.01
ADAM_STEP = 10

LANES = 128
VMEM_LIMIT = 56 * 1024 * 1024
ROW_TILE = 256
ROW_TILE_BWD = 512
SCAN_TILE = 256


def _params(*sem):
    return pltpu.CompilerParams(dimension_semantics=sem, vmem_limit_bytes=VMEM_LIMIT)


def _operand(a):
    return a.astype(BF16)


def _mm(a, b):
    return jnp.dot(_operand(a), _operand(b), preferred_element_type=F32)


def _mm_nt(a, b):
    return lax.dot_general(_operand(a), _operand(b), (((1,), (1,)), ((), ())), preferred_element_type=F32)


def _mm_tn(a, b):
    return lax.dot_general(_operand(a), _operand(b), (((0,), (0,)), ((), ())), preferred_element_type=F32)


def _split(a):
    hi = a.astype(BF16)
    lo = (a - hi.astype(F32)).astype(BF16)
    return hi, lo


def _mm3(a, b):
    ah, al = _split(a)
    bh, bl = _split(b)
    return _mm(ah, bh) + (_mm(ah, bl) + _mm(al, bh))


def _sigmoid(x):
    return jax.nn.sigmoid(x)


def _log1p(u):
    w = 1.0 + u
    return jnp.where(w == 1.0, u, jnp.log(w) * (u / jnp.where(w == 1.0, 1.0, w - 1.0)))


def _expm1(y):
    u = jnp.exp(y)
    um1 = u - 1.0
    safe = jnp.where((u == 1.0) | (um1 == -1.0), 1.0, jnp.log(jnp.where(u == 0.0, 1.0, u)))
    return jnp.where(u == 1.0, y, jnp.where(um1 == -1.0, -1.0, um1 * (y / safe)))


def _softplus(x):
    return jnp.maximum(x, 0.0) + _log1p(jnp.exp(-jnp.abs(x)))


_GELU_C = math.sqrt(2.0 / math.pi)


def _gelu(x):
    return 0.5 * x * (1.0 + jnp.tanh(_GELU_C * (x + 0.044715 * (x * x * x))))


def _gelu_grad(x):
    t = jnp.tanh(_GELU_C * (x + 0.044715 * (x * x * x)))
    return 0.5 * (1.0 + t) + 0.5 * x * (1.0 - t * t) * (_GELU_C * (1.0 + 3.0 * 0.044715 * (x * x)))


def _silu_and_grad(x):
    s = _sigmoid(x)
    return x * s, s * (1.0 + x * (1.0 - s))


def _group_sum_slab(x):
    lane = lax.broadcasted_iota(jnp.int32, x.shape, 1)
    low = jnp.sum(x[:, 0:LANES // 2], axis=1, keepdims=True)
    high = jnp.sum(x[:, LANES // 2:], axis=1, keepdims=True)
    return jnp.where(lane < LANES // 2, low, high)


def _group_sum(x):
    n = x.shape[1] // LANES
    if n == 1:
        return _group_sum_slab(x)
    return jnp.concatenate([_group_sum_slab(x[:, LANES * i:LANES * (i + 1)]) for i in range(n)], axis=1)


def _rows_prefix_sum(x):
    n = x.shape[0]
    row = lax.broadcasted_iota(jnp.int32, x.shape, 0)
    d = 1
    while d < n:
        x = x + jnp.where(row >= d, pltpu.roll(x, d, 0), 0.0)
        d *= 2
    return x


def _rows_suffix_sum(x):
    n = x.shape[0]
    row = lax.broadcasted_iota(jnp.int32, x.shape, 0)
    d = 1
    while d < n:
        x = x + jnp.where(row < n - d, pltpu.roll(x, n - d, 0), 0.0)
        d *= 2
    return x


def _shift_rows(cur, prev, j):
    row = lax.broadcasted_iota(jnp.int32, cur.shape, 0)
    return jnp.where(row < j, pltpu.roll(prev, j, 0), pltpu.roll(cur, j, 0))


def _shift_rows_up(cur, nxt, j):
    n = cur.shape[0]
    row = lax.broadcasted_iota(jnp.int32, cur.shape, 0)
    return jnp.where(row < n - j, pltpu.roll(cur, n - j, 0), pltpu.roll(nxt, n - j, 0))


def _layer_norm_stats(r):
    mu = jnp.mean(r, axis=-1, keepdims=True)
    d = r - mu
    var = jnp.mean(d * d, axis=-1, keepdims=True)
    rstd = lax.rsqrt(var + LN_EPS)
    return d * rstd, rstd


def _load_resident(step, pairs, sems):
    @pl.when(step == 0)
    def _():
        cps = [pltpu.make_async_copy(h, v, sems.at[i]) for i, (h, v) in enumerate(pairs)]
        for c in cps:
            c.start()
        for c in cps:
            c.wait()


def _row_spec(tile, width):
    return pl.BlockSpec((tile, width), lambda i: (i, 0))


def _full_spec(shape):
    nd = len(shape)
    return pl.BlockSpec(shape, lambda i: (0,) * nd)


_ANY = pl.BlockSpec(memory_space=pl.ANY)


def ffn_fwd(x, ln_g, ln_b, w_gate, w_up, w_down, ple=None, fused=None):
    s = x.shape[0]
    tm = ROW_TILE
    with_ple = ple is not None
    weights = [w_gate, w_up, w_down] + ([ple[1], ple[2]] if with_ple else [])

    def body(*refs):
        it = iter(refs)
        x_ref, g_ref, b_ref = next(it), next(it), next(it)
        p_ref = next(it) if with_ple else None
        w_hbm = [next(it) for _ in weights]
        hg_ref, hu_ref, r_ref, xn_ref, xnb_ref = next(it), next(it), next(it), next(it), next(it)
        pg_ref, pp_ref = (next(it), next(it)) if with_ple else (None, None)
        w_vm = [next(it) for _ in weights]
        sems = next(it)
        _load_resident(pl.program_id(0), list(zip(w_hbm, w_vm)), sems)
        xv = x_ref[...]
        xb = xv.astype(BF16)
        hg = _mm(xb, w_vm[0][...])
        hu = _mm(xb, w_vm[1][...])
        hg_ref[...] = hg
        hu_ref[...] = hu
        act = (hg * _sigmoid(hg)) * hu
        r = ALPHA * xv + 0.5 * _mm(act.astype(BF16), w_vm[2][...])
        if with_ple:
            pg = _mm(xb, w_vm[3][...])
            pp = _mm(p_ref[...].astype(BF16), w_vm[4][...])
            pg_ref[...] = pg
            pp_ref[...] = pp
            r = r + _sigmoid(pg) * pp
        r_ref[...] = r
        xhat, _ = _layer_norm_stats(r)
        xn = xhat * g_ref[...] + b_ref[...]
        xn_ref[...] = xn
        xnb_ref[...] = xn.astype(BF16)

    d, f = D_MODEL, D_FF
    in_specs = [_row_spec(tm, d), _full_spec((1, d)), _full_spec((1, d))]
    args = [x, ln_g, ln_b]
    if with_ple:
        in_specs.append(_row_spec(tm, PLE_DIM))
        args.append(ple[0])
    in_specs += [_ANY] * len(weights)
    args += weights
    out_shape = [jax.ShapeDtypeStruct((s, f), F32), jax.ShapeDtypeStruct((s, f), F32),
                 jax.ShapeDtypeStruct((s, d), F32), jax.ShapeDtypeStruct((s, d), F32), jax.ShapeDtypeStruct((s, d), BF16)]
    out_specs = [_row_spec(tm, f), _row_spec(tm, f), _row_spec(tm, d), _row_spec(tm, d), _row_spec(tm, d)]
    if with_ple:
        out_shape += [jax.ShapeDtypeStruct((s, d), F32)] * 2
        out_specs += [_row_spec(tm, d)] * 2
    scratch = [pltpu.VMEM(w.shape, w.dtype) for w in weights] + [pltpu.SemaphoreType.DMA((len(weights),))]
    f_in, f_out, f_scr, _ = fused if fused is not None else ([], [], [], None)
    n_out = len(out_shape)
    outs = pl.pallas_call(
        _fuse(body, len(args), n_out, len(scratch), s // tm, fused),
        name=("ffn_fwd_ple" if with_ple else "ffn_fwd") + ("_gather" if fused is not None else ""), grid=(s // tm,),
        in_specs=in_specs + [_ANY] * len(f_in), out_specs=out_specs + [_ANY] * len(f_out),
        out_shape=out_shape + list(f_out), scratch_shapes=scratch + list(f_scr), compiler_params=_params("arbitrary"),
    )(*args, *f_in)
    return tuple(outs[:n_out]) + ((list(outs[n_out:]),) if fused is not None else ())


def ffn_bwd(dxn, r, x, hg, hu, ln_g, w_gate, w_up, w_down, ple=None):
    s = x.shape[0]
    with_ple = ple is not None
    d, f = D_MODEL, D_FF
    suffix = "_ple" if with_ple else ""

    tm = ROW_TILE

    def body_a(*refs):
        it = iter(refs)
        dxn_ref, r_ref, hg_ref, hu_ref, g_ref = (next(it) for _ in range(5))
        pg_ref, pp_ref = (next(it), next(it)) if with_ple else (None, None)
        wd_hbm = next(it)
        dr_ref, act_ref, dhg_ref, dhu_ref, dy_ref, dg_ref, db_ref = (next(it) for _ in range(7))
        dpg_ref, dpp_ref = (next(it), next(it)) if with_ple else (None, None)
        wd_vm, sems = next(it), next(it)
        step = pl.program_id(0)
        _load_resident(step, [(wd_hbm, wd_vm)], sems)

        @pl.when(step == 0)
        def _():
            dg_ref[...] = jnp.zeros_like(dg_ref)
            db_ref[...] = jnp.zeros_like(db_ref)

        dxn_v = dxn_ref[...]
        xhat, rstd = _layer_norm_stats(r_ref[...])
        dg_ref[...] += jnp.sum(dxn_v * xhat, axis=0, keepdims=True)
        db_ref[...] += jnp.sum(dxn_v, axis=0, keepdims=True)
        dyh = dxn_v * g_ref[...]
        dr = rstd * (dyh - jnp.mean(dyh, axis=-1, keepdims=True) - xhat * jnp.mean(dyh * xhat, axis=-1, keepdims=True))
        dr_ref[...] = dr
        dy = (0.5 * dr).astype(BF16)
        dy_ref[...] = dy
        da = _mm_nt(dy, wd_vm[...])
        hg_v = hg_ref[...]
        hu_v = hu_ref[...]
        sil, dsil = _silu_and_grad(hg_v)
        act_ref[...] = (sil * hu_v).astype(BF16)
        dhu_ref[...] = (da * sil).astype(BF16)
        dhg_ref[...] = (da * hu_v * dsil).astype(BF16)
        if with_ple:
            sp = _sigmoid(pg_ref[...])
            dpp_ref[...] = (dr * sp).astype(BF16)
            dpg_ref[...] = (dr * pp_ref[...] * sp * (1.0 - sp)).astype(BF16)

    in_specs = [_row_spec(tm, d), _row_spec(tm, d), _row_spec(tm, f), _row_spec(tm, f), _full_spec((1, d))]
    args = [dxn, r, hg, hu, ln_g]
    if with_ple:
        in_specs += [_row_spec(tm, d), _row_spec(tm, d)]
        args += [ple[0], ple[1]]
    out_shape = [jax.ShapeDtypeStruct((s, d), F32), jax.ShapeDtypeStruct((s, f), BF16), jax.ShapeDtypeStruct((s, f), BF16),
                 jax.ShapeDtypeStruct((s, f), BF16), jax.ShapeDtypeStruct((s, d), BF16),
                 jax.ShapeDtypeStruct((1, d), F32), jax.ShapeDtypeStruct((1, d), F32)]
    out_specs = [_row_spec(tm, d), _row_spec(tm, f), _row_spec(tm, f), _row_spec(tm, f), _row_spec(tm, d),
                 _full_spec((1, d)), _full_spec((1, d))]
    if with_ple:
        out_shape += [jax.ShapeDtypeStruct((s, d), BF16)] * 2
        out_specs += [_row_spec(tm, d)] * 2
    first = pl.pallas_call(
        body_a, name="ffn_bwd_hidden" + suffix, grid=(s // tm,), in_specs=in_specs + [_ANY], out_specs=out_specs,
        out_shape=out_shape, scratch_shapes=[pltpu.VMEM(w_down.shape, w_down.dtype), pltpu.SemaphoreType.DMA((1,))],
        compiler_params=_params("arbitrary"),
    )(*args, w_down)
    dr, act, dhg, dhu, dy, dg, db = first[:7]

    tb = min(ROW_TILE_BWD, s)
    weights = [w_gate, w_up] + ([ple[2]] if with_ple else [])

    def body_b(*refs):
        it = iter(refs)
        dr_ref, dhg_ref, dhu_ref = next(it), next(it), next(it)
        dpg_ref = next(it) if with_ple else None
        w_hbm = [next(it) for _ in weights]
        dx_ref = next(it)
        w_vm = [next(it) for _ in weights]
        sems = next(it)
        _load_resident(pl.program_id(0), list(zip(w_hbm, w_vm)), sems)
        dx = ALPHA * dr_ref[...] + _mm_nt(dhg_ref[...], w_vm[0][...]) + _mm_nt(dhu_ref[...], w_vm[1][...])
        if with_ple:
            dx = dx + _mm_nt(dpg_ref[...], w_vm[2][...])
        dx_ref[...] = dx

    in_specs = [_row_spec(tb, d), _row_spec(tb, f), _row_spec(tb, f)] + ([_row_spec(tb, d)] if with_ple else [])
    args = [dr, dhg, dhu] + ([first[7]] if with_ple else [])
    dx = pl.pallas_call(
        body_b, name="ffn_bwd_input" + suffix, grid=(s // tb,), in_specs=in_specs + [_ANY] * len(weights),
        out_specs=_row_spec(tb, d), out_shape=jax.ShapeDtypeStruct((s, d), F32),
        scratch_shapes=[pltpu.VMEM(w.shape, w.dtype) for w in weights] + [pltpu.SemaphoreType.DMA((len(weights),))],
        compiler_params=_params("arbitrary"),
    )(*args, *weights)
    return (dx, act, dhg, dhu, dy, dg, db) + tuple(first[7:])


def win_fwd(x1, w_r, w_l, w_g):
    s = x1.shape[0]
    tm = min(ROW_TILE_BWD, s)
    weights = [w_r, w_l, w_g]

    def body(x_ref, wr_h, wl_h, wg_h, hr_ref, hl_ref, hgd_ref, wr_v, wl_v, wg_v, sems):
        _load_resident(pl.program_id(0), [(wr_h, wr_v), (wl_h, wl_v), (wg_h, wg_v)], sems)
        xb = x_ref[...].astype(BF16)
        hr_ref[...] = _mm(xb, wr_v[...])
        hl_ref[...] = _mm(xb, wl_v[...])
        hgd_ref[...] = _mm(xb, wg_v[...])

    return pl.pallas_call(
        body, name="win_fwd", grid=(s // tm,),
        in_specs=[_row_spec(tm, D_MODEL), _ANY, _ANY, _ANY],
        out_specs=[_row_spec(tm, RET_IN), _row_spec(tm, LRU_IN), _row_spec(tm, GDN_IN)],
        out_shape=[jax.ShapeDtypeStruct((s, RET_IN), F32), jax.ShapeDtypeStruct((s, LRU_IN), F32),
                   jax.ShapeDtypeStruct((s, GDN_IN), F32)],
        scratch_shapes=[pltpu.VMEM(w.shape, w.dtype) for w in weights] + [pltpu.SemaphoreType.DMA((3,))],
        compiler_params=_params("arbitrary"),
    )(x1, *weights)


def win_bwd(dr2, dhr, dhl, dhq, dab, w_r, w_l, w_g, w_ab):
    s = dr2.shape[0]
    tm = min(ROW_TILE_BWD, s)
    weights = [w_r, w_l, w_g, w_ab]
    nq = 4 * GDN_W

    def body(dr_ref, dhr_ref, dhl_ref, dhq_ref, dab_ref, wr_h, wl_h, wg_h, wab_h, dx_ref, wr_v, wl_v, wg_v, wab_v, sems):
        _load_resident(pl.program_id(0), [(wr_h, wr_v), (wl_h, wl_v), (wg_h, wg_v), (wab_h, wab_v)], sems)
        dx_ref[...] = (ALPHA * dr_ref[...] + _mm_nt(dhr_ref[...], wr_v[...]) + _mm_nt(dhl_ref[...], wl_v[...])
                       + _mm_nt(dhq_ref[...], wg_v[:, 0:nq]) + _mm_nt(dab_ref[...], wab_v[...]))

    return pl.pallas_call(
        body, name="win_bwd", grid=(s // tm,),
        in_specs=[_row_spec(tm, D_MODEL), _row_spec(tm, RET_IN), _row_spec(tm, LRU_IN), _row_spec(tm, nq), _row_spec(tm, LANES),
                  _ANY, _ANY, _ANY, _ANY],
        out_specs=_row_spec(tm, D_MODEL),
        out_shape=jax.ShapeDtypeStruct((s, D_MODEL), F32),
        scratch_shapes=[pltpu.VMEM(w.shape, w.dtype) for w in weights] + [pltpu.SemaphoreType.DMA((4,))],
        compiler_params=_params("arbitrary"),
    )(dr2, dhr, dhl, dhq, dab, *weights)


def out_fwd(o_r, o_l, o_g, x1, w_out, ln_g, ln_b):
    s = x1.shape[0]
    tm = ROW_TILE

    def body(or_ref, ol_ref, og_ref, x_ref, g_ref, b_ref, w_h, r_ref, xn_ref, xnb_ref, ocat_ref, w_v, sems):
        _load_resident(pl.program_id(0), [(w_h, w_v)], sems)
        ocat = jnp.concatenate([or_ref[...], ol_ref[...], og_ref[...]], axis=1).astype(BF16)
        ocat_ref[...] = ocat
        r = ALPHA * x_ref[...] + _mm(ocat, w_v[...])
        r_ref[...] = r
        xhat, _ = _layer_norm_stats(r)
        xn = xhat * g_ref[...] + b_ref[...]
        xn_ref[...] = xn
        xnb_ref[...] = xn.astype(BF16)

    d = D_MODEL
    return pl.pallas_call(
        body, name="out_fwd", grid=(s // tm,),
        in_specs=[_row_spec(tm, RET_W), _row_spec(tm, LRU_W), _row_spec(tm, GDN_W), _row_spec(tm, d),
                  _full_spec((1, d)), _full_spec((1, d)), _ANY],
        out_specs=[_row_spec(tm, d)] * 4,
        out_shape=[jax.ShapeDtypeStruct((s, d), F32)] * 2 + [jax.ShapeDtypeStruct((s, d), BF16)] * 2,
        scratch_shapes=[pltpu.VMEM(w_out.shape, w_out.dtype), pltpu.SemaphoreType.DMA((1,))],
        compiler_params=_params("arbitrary"),
    )(o_r, o_l, o_g, x1, ln_g, ln_b, w_out)


def out_bwd(dxn, r2, ln_g, w_out):
    s = dxn.shape[0]
    tm = ROW_TILE

    def body(dxn_ref, r_ref, g_ref, w_h, dr_ref, drb_ref, dor_ref, dol_ref, dog_ref, dg_ref, db_ref, w_v, sems):
        step = pl.program_id(0)
        _load_resident(step, [(w_h, w_v)], sems)

        @pl.when(step == 0)
        def _():
            dg_ref[...] = jnp.zeros_like(dg_ref)
            db_ref[...] = jnp.zeros_like(db_ref)

        dxn_v = dxn_ref[...]
        xhat, rstd = _layer_norm_stats(r_ref[...])
        dg_ref[...] += jnp.sum(dxn_v * xhat, axis=0, keepdims=True)
        db_ref[...] += jnp.sum(dxn_v, axis=0, keepdims=True)
        dyh = dxn_v * g_ref[...]
        dr = rstd * (dyh - jnp.mean(dyh, axis=-1, keepdims=True) - xhat * jnp.mean(dyh * xhat, axis=-1, keepdims=True))
        dr_ref[...] = dr
        drb = dr.astype(BF16)
        drb_ref[...] = drb
        dor_ref[...] = _mm_nt(drb, w_v[0:RET_W, :])
        dol_ref[...] = _mm_nt(drb, w_v[RET_W:RET_W + LRU_W, :])
        dog_ref[...] = _mm_nt(drb, w_v[RET_W + LRU_W:, :])

    d = D_MODEL
    return pl.pallas_call(
        body, name="out_bwd", grid=(s // tm,),
        in_specs=[_row_spec(tm, d), _row_spec(tm, d), _full_spec((1, d)), _ANY],
        out_specs=[_row_spec(tm, d), _row_spec(tm, d), _row_spec(tm, RET_W), _row_spec(tm, LRU_W), _row_spec(tm, GDN_W),
                   _full_spec((1, d)), _full_spec((1, d))],
        out_shape=[jax.ShapeDtypeStruct((s, d), F32), jax.ShapeDtypeStruct((s, d), BF16),
                   jax.ShapeDtypeStruct((s, RET_W), F32), jax.ShapeDtypeStruct((s, LRU_W), F32),
                   jax.ShapeDtypeStruct((s, GDN_W), F32), jax.ShapeDtypeStruct((1, d), F32), jax.ShapeDtypeStruct((1, d), F32)],
        scratch_shapes=[pltpu.VMEM(w_out.shape, w_out.dtype), pltpu.SemaphoreType.DMA((1,))],
        compiler_params=_params("arbitrary"),
    )(dxn, r2, ln_g, w_out)


def wgrad(a, b, name, out_dtype=BF16):
    s, m = a.shape
    n = b.shape[1]
    tk = 1024 if s % 1024 == 0 else s
    tm = next((c for c in (1408, 1024, 768, 512, 384, 256) if m % c == 0), m)
    tn = next((c for c in (1408, 1152, 1024, 768, 512) if n % c == 0), n)
    nk = s // tk

    def body(a_ref, b_ref, o_ref, acc_ref):
        k = pl.program_id(2)

        @pl.when(k == 0)
        def _():
            acc_ref[...] = jnp.zeros_like(acc_ref)

        acc_ref[...] += _mm_tn(a_ref[...].astype(BF16), b_ref[...].astype(BF16))

        @pl.when(k == nk - 1)
        def _():
            o_ref[...] = acc_ref[...].astype(o_ref.dtype)

    return pl.pallas_call(
        body, name=name, grid=(m // tm, n // tn, nk),
        in_specs=[pl.BlockSpec((tk, tm), lambda i, j, k: (k, i)), pl.BlockSpec((tk, tn), lambda i, j, k: (k, j))],
        out_specs=pl.BlockSpec((tm, tn), lambda i, j, k: (i, j)),
        out_shape=jax.ShapeDtypeStruct((m, n), out_dtype),
        scratch_shapes=[pltpu.VMEM((tm, tn), F32)],
        compiler_params=_params("arbitrary", "arbitrary", "arbitrary"),
    )(a, b)


def loss_and_grad(y, target):
    s, d = y.shape
    tm = ROW_TILE

    def body(y_ref, t_ref, dy_ref, l_ref):
        @pl.when(pl.program_id(0) == 0)
        def _():
            l_ref[...] = jnp.zeros_like(l_ref)

        err = y_ref[...] - t_ref[...]
        dy_ref[...] = err / d
        l_ref[...] += 0.5 * jnp.sum(jnp.mean(err * err, axis=-1, keepdims=True), axis=0, keepdims=True)

    return pl.pallas_call(
        body, name="loss_and_grad", grid=(s // tm,),
        in_specs=[_row_spec(tm, d), _row_spec(tm, d)],
        out_specs=[_row_spec(tm, d), _full_spec((1, 1))],
        out_shape=[jax.ShapeDtypeStruct((s, d), F32), jax.ShapeDtypeStruct((1, 1), F32)],
        compiler_params=_params("arbitrary"),
    )(y, target)


def _ret_consts():
    lg = np.log1p(-np.exp2(-5.0 - np.arange(RET_HEADS, dtype=np.float64)))
    idx = np.arange(CHUNK, dtype=np.float64)
    intra = np.exp(np.abs(idx[:, None] - idx[None, :])[None] * lg[:, None, None])
    cross = np.repeat(np.exp((idx + 1.0)[:, None] * lg[None, :]), HEAD, axis=1)
    tail = np.repeat(np.exp((CHUNK - 1.0 - idx)[:, None] * lg[None, :]), HEAD, axis=1)
    dec = np.repeat(np.exp(CHUNK * lg)[None, :], HEAD, axis=1)
    half = HEAD // 2
    inv_freq = (ROPE_THETA ** (-jnp.arange(half, dtype=F32) / half))
    invf = jnp.tile(inv_freq, 2 * LANES // HEAD)[None, :]
    sgn = np.tile(np.concatenate([-np.ones(half), np.ones(half)]), LANES // HEAD)[None, :]
    f = lambda a: jnp.asarray(a, F32)
    return dict(intra=f(intra), cross=f(cross), tail=f(tail), dec=f(dec), invf=invf, sgn=f(sgn))


def _swap_halves(t):
    lane = lax.broadcasted_iota(jnp.int32, t.shape, 1)
    return jnp.where((lane & 32) == 0, pltpu.roll(t, LANES - 32, 1), pltpu.roll(t, 32, 1))


def _rope(t, c, s):
    return t * c + _swap_halves(t) * s


def _rope_transposed(g, c, s):
    return g * c + _swap_halves(g * s)


def _head_mask(hd):
    lane = lax.broadcasted_iota(jnp.int32, (1, LANES), 1)
    return ((lane >= HEAD * hd) & (lane < HEAD * (hd + 1))).astype(F32)


def _block_diag_mask():
    r = lax.broadcasted_iota(jnp.int32, (LANES, LANES), 0)
    c = lax.broadcasted_iota(jnp.int32, (LANES, LANES), 1)
    return ((r >= HEAD) == (c >= HEAD)).astype(F32)


RET_STEP_CHUNKS = 4


def _ret_specs(n_of, gch):
    cst = lambda shape: pl.BlockSpec(shape, lambda i: (0,) * len(shape))
    return [pl.BlockSpec((CHUNK * gch, RET_IN), lambda i: (n_of(i), 0)), pl.BlockSpec((CHUNK * gch, 1), lambda i: (n_of(i), 0)),
            cst((1, LANES)), cst((1, LANES)), cst((RET_HEADS, CHUNK, CHUNK)), cst((CHUNK, RET_W)), cst((CHUNK, RET_W)),
            cst((1, RET_W)), cst((1, RET_W))]


def ret_fwd(hr, pos, norm_g):
    s = hr.shape[0]
    n_chunks = s // CHUNK
    cs = _ret_consts()
    n_slab = RET_W // LANES

    gch = min(RET_STEP_CHUNKS, n_chunks)

    def body(hr_ref, pos_ref, invf_ref, sgn_ref, intra_ref, cross_ref, tail_ref, dec_ref, g_ref, o_ref, opre_ref, st_ref, state):
        @pl.when(pl.program_id(0) == 0)
        def _():
            state[...] = jnp.zeros_like(state)

        bd = _block_diag_mask()
        sts = [state[LANES * sl:LANES * (sl + 1), :] for sl in range(n_slab)]
        for c in range(gch):
            tok = slice(CHUNK * c, CHUNK * (c + 1))
            ang = pos_ref[tok, :].astype(F32) * invf_ref[...]
            cosv = jnp.cos(ang)
            sinv = jnp.sin(ang) * sgn_ref[...]
            for sl in range(n_slab):
                lanes = slice(LANES * sl, LANES * (sl + 1))
                q = hr_ref[tok, LANES * sl:LANES * (sl + 1)]
                k = hr_ref[tok, RET_W + LANES * sl:RET_W + LANES * (sl + 1)]
                v = hr_ref[tok, 2 * RET_W + LANES * sl:2 * RET_W + LANES * (sl + 1)]
                gate = hr_ref[tok, 3 * RET_W + LANES * sl:3 * RET_W + LANES * (sl + 1)]
                qt = _rope(q, cosv, sinv) * (HEAD ** -0.5)
                kt = _rope(k, cosv, sinv)
                st = sts[sl]
                st_ref[RET_W * c + LANES * sl:RET_W * c + LANES * (sl + 1), :] = st
                o = _mm(qt * cross_ref[:, lanes], st)
                for hd in range(2):
                    m = _head_mask(hd)
                    sc = _mm_nt(qt * m, kt) * intra_ref[2 * sl + hd]
                    o = o + _mm(sc, v) * m
                sts[sl] = st * dec_ref[:, lanes] + _mm_tn(kt, v * tail_ref[:, lanes]) * bd
                opre_ref[tok, lanes] = o
                mu = _group_sum_slab(o) * (1.0 / HEAD)
                dlt = o - mu
                var = _group_sum_slab(dlt * dlt) * (1.0 / HEAD)
                on = dlt * lax.rsqrt(var + 1e-5)
                o_ref[tok, lanes] = on * g_ref[:, lanes] * (gate * _sigmoid(gate))
        for sl in range(n_slab):
            state[LANES * sl:LANES * (sl + 1), :] = sts[sl]

    out_row = lambda w: pl.BlockSpec((CHUNK * gch, w), lambda i: (i, 0))
    return pl.pallas_call(
        body, name="ret_fwd", grid=(n_chunks // gch,),
        in_specs=_ret_specs(lambda i: i, gch),
        out_specs=[out_row(RET_W), out_row(RET_W), pl.BlockSpec((RET_W * gch, LANES), lambda i: (i, 0))],
        out_shape=[jax.ShapeDtypeStruct((s, RET_W), F32), jax.ShapeDtypeStruct((s, RET_W), F32),
                   jax.ShapeDtypeStruct((n_chunks * RET_W, LANES), F32)],
        scratch_shapes=[pltpu.VMEM((RET_W, LANES), F32)],
        compiler_params=_params("arbitrary"),
    )(hr, pos, cs["invf"], cs["sgn"], cs["intra"], cs["cross"], cs["tail"], cs["dec"], norm_g)


def ret_bwd(hr, pos, norm_g, opre, states, dout):
    s = hr.shape[0]
    n_chunks = s // CHUNK
    cs = _ret_consts()
    n_slab = RET_W // LANES
    gch = min(RET_STEP_CHUNKS, n_chunks)
    rev = lambda i: n_chunks // gch - 1 - i

    def body(hr_ref, pos_ref, invf_ref, sgn_ref, intra_ref, cross_ref, tail_ref, dec_ref, g_ref, opre_ref, st_ref, do_ref,
             dh_ref, dg_ref, gstate):
        @pl.when(pl.program_id(0) == 0)
        def _():
            gstate[...] = jnp.zeros_like(gstate)
            dg_ref[...] = jnp.zeros_like(dg_ref)

        bd = _block_diag_mask()
        gss = [gstate[LANES * sl:LANES * (sl + 1), :] for sl in range(n_slab)]
        dgs = [jnp.zeros((1, LANES), F32) for _ in range(n_slab)]
        for c in reversed(range(gch)):
            tok = slice(CHUNK * c, CHUNK * (c + 1))
            ang = pos_ref[tok, :].astype(F32) * invf_ref[...]
            cosv = jnp.cos(ang)
            sinv = jnp.sin(ang) * sgn_ref[...]
            for sl in range(n_slab):
                lanes = slice(LANES * sl, LANES * (sl + 1))
                q = hr_ref[tok, LANES * sl:LANES * (sl + 1)]
                k = hr_ref[tok, RET_W + LANES * sl:RET_W + LANES * (sl + 1)]
                v = hr_ref[tok, 2 * RET_W + LANES * sl:2 * RET_W + LANES * (sl + 1)]
                gate = hr_ref[tok, 3 * RET_W + LANES * sl:3 * RET_W + LANES * (sl + 1)]
                qt = _rope(q, cosv, sinv) * (HEAD ** -0.5)
                kt = _rope(k, cosv, sinv)
                o = opre_ref[tok, lanes]
                mu = _group_sum_slab(o) * (1.0 / HEAD)
                dlt = o - mu
                var = _group_sum_slab(dlt * dlt) * (1.0 / HEAD)
                rstd = lax.rsqrt(var + 1e-5)
                on = dlt * rstd
                sil, dsil = _silu_and_grad(gate)
                dout_v = do_ref[tok, lanes]
                gn = g_ref[:, lanes]
                dgs[sl] = dgs[sl] + jnp.sum(dout_v * on * sil, axis=0, keepdims=True)
                d_on = dout_v * gn * sil
                dgate = dout_v * on * gn * dsil
                d_o = rstd * (d_on - _group_sum_slab(d_on) * (1.0 / HEAD) - on * (_group_sum_slab(d_on * on) * (1.0 / HEAD)))
                st = st_ref[RET_W * c + LANES * sl:RET_W * c + LANES * (sl + 1), :]
                gs = gss[sl]
                cross = cross_ref[:, lanes]
                tail = tail_ref[:, lanes]
                dqt = _mm_nt(d_o, st) * cross
                ds_here = _mm_tn(qt * cross, d_o) * bd
                vt = v * tail
                dkt = _mm_nt(vt, gs)
                dv = _mm(kt, gs) * tail
                for hd in range(2):
                    m = _head_mask(hd)
                    qm = qt * m
                    dom = d_o * m
                    intra = intra_ref[2 * sl + hd]
                    sc = _mm_nt(qm, kt) * intra
                    dsc = _mm_nt(dom, v) * intra
                    dqt = dqt + _mm(dsc, kt) * m
                    dkt = dkt + _mm_tn(dsc, qm)
                    dv = dv + _mm_tn(sc, dom)
                gss[sl] = gs * dec_ref[:, lanes] + ds_here
                dh_ref[tok, LANES * sl:LANES * (sl + 1)] = _rope_transposed(dqt * (HEAD ** -0.5), cosv, sinv).astype(BF16)
                dh_ref[tok, RET_W + LANES * sl:RET_W + LANES * (sl + 1)] = _rope_transposed(dkt, cosv, sinv).astype(BF16)
                dh_ref[tok, 2 * RET_W + LANES * sl:2 * RET_W + LANES * (sl + 1)] = dv.astype(BF16)
                dh_ref[tok, 3 * RET_W + LANES * sl:3 * RET_W + LANES * (sl + 1)] = dgate.astype(BF16)
        for sl in range(n_slab):
            gstate[LANES * sl:LANES * (sl + 1), :] = gss[sl]
            dg_ref[:, LANES * sl:LANES * (sl + 1)] += dgs[sl]

    row = lambda w: pl.BlockSpec((CHUNK * gch, w), lambda i: (rev(i), 0))
    return pl.pallas_call(
        body, name="ret_bwd", grid=(n_chunks // gch,),
        in_specs=_ret_specs(rev, gch) + [row(RET_W), pl.BlockSpec((RET_W * gch, LANES), lambda i: (rev(i), 0)), row(RET_W)],
        out_specs=[row(RET_IN), pl.BlockSpec((1, RET_W), lambda i: (0, 0))],
        out_shape=[jax.ShapeDtypeStruct((s, RET_IN), BF16), jax.ShapeDtypeStruct((1, RET_W), F32)],
        scratch_shapes=[pltpu.VMEM((RET_W, LANES), F32)],
        compiler_params=_params("arbitrary"),
    )(hr, pos, cs["invf"], cs["sgn"], cs["intra"], cs["cross"], cs["tail"], cs["dec"], norm_g, opre, states, dout)


def _lru_gates(xc, wa_ref, ba_ref, wx_ref, bx_ref, lam_ref):
    xcb = xc.astype(BF16)
    r = _sigmoid(_mm(xcb, wa_ref[...].astype(BF16)) + ba_ref[...])
    ig = _sigmoid(_mm(xcb, wx_ref[...].astype(BF16)) + bx_ref[...])
    lam = lam_ref[...]
    ls = jnp.minimum(lam, 0.0) - _log1p(jnp.exp(-jnp.abs(lam)))
    la = (LRU_C * r) * ls
    a = jnp.exp(la)
    mult = jnp.sqrt(-_expm1(2.0 * la))
    return r, ig, ls, a, mult


def _lru_conv(x, xprev, w_ref, b_ref):
    xc = b_ref[...] + w_ref[3:4, :] * x
    for j in (1, 2, 3):
        xc = xc + w_ref[3 - j:4 - j, :] * _shift_rows(x, xprev, j)
    return xc


def lru_fwd(hl, conv_w, conv_b, w_a, b_a, w_x, b_x, lam):
    s = hl.shape[0]
    ts = SCAN_TILE
    w = LRU_W

    def body(hl_ref, hp_ref, cw_ref, cb_ref, wa_ref, ba_ref, wx_ref, bx_ref, lam_ref, o_ref, xc_ref, h_ref, carry):
        i = pl.program_id(0)

        @pl.when(i == 0)
        def _():
            carry[...] = jnp.zeros_like(carry)

        x = hl_ref[:, 0:w]
        gate = hl_ref[:, w:2 * w]
        xprev = hp_ref[...] * (i > 0).astype(F32)
        xc = _lru_conv(x, xprev, cw_ref, cb_ref)
        xc_ref[...] = xc
        _, ig, _, a, mult = _lru_gates(xc, wa_ref, ba_ref, wx_ref, bx_ref, lam_ref)
        b = mult * (ig * xc)
        row = lax.broadcasted_iota(jnp.int32, (ts, w), 0)
        d = 1
        while d < ts:
            ap = jnp.where(row >= d, pltpu.roll(a, d, 0), 1.0)
            bp = jnp.where(row >= d, pltpu.roll(b, d, 0), 0.0)
            b = a * bp + b
            a = a * ap
            d *= 2
        h = b + a * carry[0:1, :]
        h_ref[...] = h
        carry[0:1, :] = h[ts - 1:ts, :]
        o_ref[...] = h * _gelu(gate)

    cst = lambda shape: pl.BlockSpec(shape, lambda i: (0, 0))
    return pl.pallas_call(
        body, name="lru_fwd", grid=(s // ts,),
        in_specs=[_row_spec(ts, 2 * w), pl.BlockSpec((ts, w), lambda i: (jnp.maximum(i - 1, 0), 0)),
                  cst((4, w)), cst((1, w)), cst((w, w)), cst((1, w)), cst((w, w)), cst((1, w)), cst((1, w))],
        out_specs=[_row_spec(ts, w)] * 3,
        out_shape=[jax.ShapeDtypeStruct((s, w), F32)] * 3,
        scratch_shapes=[pltpu.VMEM((8, w), F32)],
        compiler_params=_params("arbitrary"),
    )(hl, hl, conv_w, conv_b, w_a, b_a, w_x, b_x, lam)


def lru_bwd(hl, conv_w, conv_b, w_a, b_a, w_x, b_x, lam, xc_saved, h_saved, dout):
    s = hl.shape[0]
    ts = SCAN_TILE
    w = LRU_W
    nb = s // ts
    rev = lambda i: nb - 1 - i

    def body(hl_ref, hp_ref, cw_ref, cb_ref, wa_ref, ba_ref, wx_ref, bx_ref, lam_ref, xc_ref, h_ref, hprev_ref, do_ref,
             dhl_ref, dcw_ref, dcb_ref, dwa_ref, dba_ref, dwx_ref, dbx_ref, dlam_ref, carry, dxc_next):
        i = pl.program_id(0)
        blk = nb - 1 - i

        @pl.when(i == 0)
        def _():
            carry[...] = jnp.zeros_like(carry)
            dxc_next[...] = jnp.zeros_like(dxc_next)
            for ref in (dcw_ref, dcb_ref, dwa_ref, dba_ref, dwx_ref, dbx_ref, dlam_ref):
                ref[...] = jnp.zeros_like(ref)

        first = (blk > 0).astype(F32)
        x = hl_ref[:, 0:w]
        gate = hl_ref[:, w:2 * w]
        xprev = hp_ref[...] * first
        xc = xc_ref[...]
        h = h_ref[...]
        hprev = hprev_ref[...] * first
        r, ig, ls, a, mult = _lru_gates(xc, wa_ref, ba_ref, wx_ref, bx_ref, lam_ref)
        do = do_ref[...]
        dh = do * _gelu(gate)
        dgate = do * h * _gelu_grad(gate)
        row = lax.broadcasted_iota(jnp.int32, (ts, w), 0)
        ca = jnp.where(row < ts - 1, pltpu.roll(a, ts - 1, 0), 1.0)
        cb = dh
        d = 1
        while d < ts:
            an = jnp.where(row < ts - d, pltpu.roll(ca, ts - d, 0), 1.0)
            bn = jnp.where(row < ts - d, pltpu.roll(cb, ts - d, 0), 0.0)
            cb = cb + ca * bn
            ca = ca * an
            d *= 2
        lamb = cb + ca * carry[0:1, :]
        carry[0:1, :] = a[0:1, :] * lamb[0:1, :]
        h_before = _shift_rows(h, hprev, 1)
        da = lamb * h_before
        ix = ig * xc
        dmult = lamb * ix
        dig = lamb * mult * xc
        dxc = lamb * mult * ig
        dla = (da - dmult * a / mult) * a
        dr = dla * LRU_C * ls
        dlam_ref[...] += jnp.sum(dla * LRU_C * r, axis=0, keepdims=True) * _sigmoid(-lam_ref[...])
        dpa = dr * r * (1.0 - r)
        dpx = dig * ig * (1.0 - ig)
        dba_ref[...] += jnp.sum(dpa, axis=0, keepdims=True)
        dbx_ref[...] += jnp.sum(dpx, axis=0, keepdims=True)
        dpab = dpa.astype(BF16)
        dpxb = dpx.astype(BF16)
        xcb = xc.astype(BF16)
        dxc = dxc + _mm_nt(dpab, wa_ref[...].astype(BF16)) + _mm_nt(dpxb, wx_ref[...].astype(BF16))
        dwa_ref[...] += _mm_tn(xcb, dpab)
        dwx_ref[...] += _mm_tn(xcb, dpxb)
        dcb_ref[...] += jnp.sum(dxc, axis=0, keepdims=True)
        nxt = dxc_next[...]
        dx = cw_ref[3:4, :] * dxc
        dcw_ref[3:4, :] += jnp.sum(dxc * x, axis=0, keepdims=True)
        for j in (1, 2, 3):
            dx = dx + cw_ref[3 - j:4 - j, :] * _shift_rows_up(dxc, nxt, j)
            dcw_ref[3 - j:4 - j, :] += jnp.sum(dxc * _shift_rows(x, xprev, j), axis=0, keepdims=True)
        dxc_next[...] = dxc
        dhl_ref[:, 0:w] = dx.astype(BF16)
        dhl_ref[:, w:2 * w] = dgate.astype(BF16)

    cst = lambda shape: pl.BlockSpec(shape, lambda i: (0, 0))
    rowr = lambda width: pl.BlockSpec((ts, width), lambda i: (rev(i), 0))
    prevr = lambda width: pl.BlockSpec((ts, width), lambda i: (jnp.maximum(rev(i) - 1, 0), 0))
    return pl.pallas_call(
        body, name="lru_bwd", grid=(nb,),
        in_specs=[rowr(2 * w), prevr(w), cst((4, w)), cst((1, w)), cst((w, w)), cst((1, w)), cst((w, w)), cst((1, w)), cst((1, w)),
                  rowr(w), rowr(w), prevr(w), rowr(w)],
        out_specs=[rowr(2 * w), cst((4, w)), cst((1, w)), cst((w, w)), cst((1, w)), cst((w, w)), cst((1, w)), cst((1, w))],
        out_shape=[jax.ShapeDtypeStruct((s, 2 * w), BF16), jax.ShapeDtypeStruct((4, w), F32), jax.ShapeDtypeStruct((1, w), F32),
                   jax.ShapeDtypeStruct((w, w), F32), jax.ShapeDtypeStruct((1, w), F32), jax.ShapeDtypeStruct((w, w), F32),
                   jax.ShapeDtypeStruct((1, w), F32), jax.ShapeDtypeStruct((1, w), F32)],
        scratch_shapes=[pltpu.VMEM((8, w), F32), pltpu.VMEM((ts, w), F32)],
        compiler_params=_params("arbitrary"),
    )(hl, hl, conv_w, conv_b, w_a, b_a, w_x, b_x, lam, xc_saved, h_saved, h_saved, dout)


GDN_QKV = 3 * GDN_W
GDN_STEP_CHUNKS = 4
GDN_BWD_STEP_CHUNKS = 2


def _tri_inverse_many(nms):
    r = lax.broadcasted_iota(jnp.int32, nms[0].shape, 0)
    c = lax.broadcasted_iota(jnp.int32, nms[0].shape, 1)
    eye = (r == c).astype(F32)
    ts = [eye - nm for nm in nms]
    ps = list(nms)
    for _ in range(5):
        ps = [_mm3(p, p) for p in ps]
        ts = [t + _mm3(t, p) for t, p in zip(ts, ps)]
    return ts


def _gdn_front(hx_ref, hprev, cw_ref, al_ref, dt_ref):
    w = GDN_W
    x = hx_ref[:, 0:GDN_QKV]
    y = cw_ref[3:4, :] * x
    for j in (1, 2, 3):
        y = y + cw_ref[3 - j:4 - j, :] * _shift_rows(x, hprev, j)
    qkv, dsil = _silu_and_grad(y)
    q, k, v = qkv[:, 0:w], qkv[:, w:2 * w], qkv[:, 2 * w:3 * w]
    rq = lax.rsqrt(_group_sum(q * q) + 1e-6)
    rk = lax.rsqrt(_group_sum(k * k) + 1e-6)
    beta = _sigmoid(hx_ref[:, 5 * w:6 * w])
    sp_in = hx_ref[:, 4 * w:5 * w] + dt_ref[...]
    neg_a = -jnp.exp(al_ref[...])
    g = neg_a * _softplus(sp_in)
    n_c = g.shape[0] // CHUNK
    gc = jnp.concatenate([_rows_prefix_sum(g[CHUNK * c:CHUNK * (c + 1)]) for c in range(n_c)], axis=0)
    return dict(x=x, dsil=dsil, qn=q * rq, kn=k * rk, v=v, rq=rq, rk=rk, beta=beta, sp_in=sp_in, neg_a=neg_a, g=g, gc=gc)


def _stack_heads(x):
    return jnp.concatenate([x * _head_mask(0), x * _head_mask(1)], axis=0)


def _unstack_heads(y):
    return y[0:CHUNK] + y[CHUNK:2 * CHUNK]


def _head_transpose(x):
    return jnp.concatenate([x[:, 0:HEAD].T, x[:, HEAD:2 * HEAD].T], axis=1)


def _head_total(x):
    cols = jnp.broadcast_to(jnp.sum(x, axis=0, keepdims=True), (8, LANES))
    return _group_sum_slab(cols)[0:1]


def _slab_tri_masks():
    r = lax.broadcasted_iota(jnp.int32, (CHUNK, LANES), 0)
    c = lax.broadcasted_iota(jnp.int32, (CHUNK, LANES), 1) & (HEAD - 1)
    return r >= c, r > c


def _gdn_slab(fr, c, sl, tri, transposed=False):
    lower, strict = tri
    ls = lambda a: a[CHUNK * c:CHUNK * (c + 1), LANES * sl:LANES * (sl + 1)]
    k = ls(fr["kn"])
    q = ls(fr["qn"]) * (HEAD ** -0.5)
    v = ls(fr["v"])
    beta = ls(fr["beta"])
    gc = ls(fr["gc"])
    e = jnp.exp(gc)
    gl = gc[CHUNK - 1:CHUNK, :]
    xt = jnp.exp(gl - gc)
    gc_t = _head_transpose(gc)
    dec = jnp.where(lower, jnp.exp(jnp.minimum(gc - gc_t, 0.0)), 0.0)
    kbd = _stack_heads(k)
    kk = _mm_nt(k, kbd)
    qkr = _mm_nt(q, kbd)
    out = dict(k=k, q=q, v=v, beta=beta, e=e, egl=jnp.exp(gl), xt=xt, dec=dec, kk=kk, qkr=qkr, kbd=kbd,
               nm=jnp.where(strict, beta * kk * dec, 0.0))
    if transposed:
        r = lax.broadcasted_iota(jnp.int32, (CHUNK, LANES), 0)
        col = lax.broadcasted_iota(jnp.int32, (CHUNK, LANES), 1) & (HEAD - 1)
        qbd = _stack_heads(q)
        out.update(dec_t=jnp.where(r <= col, jnp.exp(jnp.minimum(gc_t - gc, 0.0)), 0.0), beta_t=_head_transpose(beta),
                   qbd=qbd, kqr=_mm_nt(k, qbd), strict_t=r < col)
    return out


def gdn_fwd(hx, conv_w, a_log_e, dt_bias_e, norm_g_e, fused=None):
    s = hx.shape[0]
    n_chunks = s // CHUNK
    w = GDN_W
    n_slab = w // LANES
    gch = min(GDN_STEP_CHUNKS, n_chunks)

    def body(hx_ref, hp_ref, cw_ref, al_ref, dt_ref, ng_ref, o_ref, opre_ref, t_ref, st_ref, state):
        n = pl.program_id(0)

        @pl.when(n == 0)
        def _():
            state[...] = jnp.zeros_like(state)

        fr = _gdn_front(hx_ref, hp_ref[...] * (n > 0).astype(F32), cw_ref, al_ref, dt_ref)
        tri = _slab_tri_masks()
        bd = _block_diag_mask()
        sts = [state[LANES * sl:LANES * (sl + 1), :] for sl in range(n_slab)]
        slabs = [[_gdn_slab(fr, c, sl, tri) for sl in range(n_slab)] for c in range(gch)]
        tbd = _tri_inverse_many([_stack_heads(sq["nm"]) for row_ in slabs for sq in row_])
        o_rows = []
        for c in range(gch):
            ts, outs = [], []
            st_ref[w * c:w * (c + 1), :] = jnp.concatenate(sts, axis=0)
            for sl in range(n_slab):
                sq = slabs[c][sl]
                t = _unstack_heads(tbd[n_slab * c + sl])
                ts.append(t)
                u = _mm(t, _stack_heads(sq["v"] * sq["beta"]))
                wk = _mm(t, _stack_heads(sq["k"] * (sq["beta"] * sq["e"])))
                st = sts[sl]
                vnew = u - _mm(wk, st)
                outs.append(_mm(sq["q"] * sq["e"], st) + _mm(sq["qkr"] * sq["dec"], _stack_heads(vnew)))
                sts[sl] = st * sq["egl"] + _mm_tn(sq["k"] * sq["xt"], vnew) * bd
            t_ref[CHUNK * c:CHUNK * (c + 1), :] = jnp.concatenate(ts, axis=1)
            o_rows.append(jnp.concatenate(outs, axis=1))
        state[...] = jnp.concatenate(sts, axis=0)
        o = jnp.concatenate(o_rows, axis=0)
        opre_ref[...] = o
        rinv = lax.rsqrt(_group_sum(o * o) * (1.0 / HEAD) + 1e-6)
        z = hx_ref[:, 3 * w:4 * w]
        o_ref[...] = (o * rinv) * ng_ref[...] * (z * _sigmoid(z))

    cst = lambda shape: pl.BlockSpec(shape, lambda i: (0, 0))
    row = lambda width: pl.BlockSpec((CHUNK * gch, width), lambda i: (i, 0))
    f_in, f_out, f_scr, _ = fused if fused is not None else ([], [], [], None)
    outs = pl.pallas_call(
        _fuse(body, 6, 4, 1, n_chunks // gch, fused), name="gdn_fwd" + ("_gather" if fused is not None else ""),
        grid=(n_chunks // gch,),
        in_specs=[row(GDN_IN), pl.BlockSpec((CHUNK * gch, GDN_QKV), lambda i: (jnp.maximum(i - 1, 0), 0)),
                  cst((4, GDN_QKV)), cst((1, w)), cst((1, w)), cst((1, w))] + [_ANY] * len(f_in),
        out_specs=[row(w)] * 3 + [pl.BlockSpec((w * gch, LANES), lambda i: (i, 0))] + [_ANY] * len(f_out),
        out_shape=[jax.ShapeDtypeStruct((s, w), F32)] * 3 + [jax.ShapeDtypeStruct((n_chunks * w, LANES), F32)] + list(f_out),
        scratch_shapes=[pltpu.VMEM((w, LANES), F32)] + list(f_scr),
        compiler_params=_params("arbitrary"),
    )(hx, hx, conv_w, a_log_e, dt_bias_e, norm_g_e, *f_in)
    return tuple(outs[:4]) + ((list(outs[4:]),) if fused is not None else ())


def gdn_bwd(hx, conv_w, a_log_e, dt_bias_e, norm_g_e, opre, tmat, states, dout, fused=None):
    s = hx.shape[0]
    n_chunks = s // CHUNK
    w = GDN_W
    n_slab = w // LANES
    gch = min(GDN_BWD_STEP_CHUNKS, n_chunks)
    n_blocks = n_chunks // gch
    rev = lambda i: n_blocks - 1 - i

    def body(hx_ref, hp_ref, cw_ref, al_ref, dt_ref, ng_ref, opre_ref, t_ref, st_ref, do_ref,
             dhx_ref, dab_ref, dcw_ref, dal_ref, ddt_ref, dng_ref, dstate, dy_next):
        i = pl.program_id(0)
        n = n_blocks - 1 - i

        @pl.when(i == 0)
        def _():
            dstate[...] = jnp.zeros_like(dstate)
            dy_next[...] = jnp.zeros_like(dy_next)
            for ref in (dcw_ref, dal_ref, ddt_ref, dng_ref):
                ref[...] = jnp.zeros_like(ref)

        hprev = hp_ref[...] * (n > 0).astype(F32)
        fr = _gdn_front(hx_ref, hprev, cw_ref, al_ref, dt_ref)
        tri = _slab_tri_masks()
        lower, strict = tri
        o = opre_ref[...]
        rinv = lax.rsqrt(_group_sum(o * o) * (1.0 / HEAD) + 1e-6)
        yn = o * rinv
        z = hx_ref[:, 3 * w:4 * w]
        sil, dsil_z = _silu_and_grad(z)
        dout_v = do_ref[...]
        ng = ng_ref[...]
        dng_ref[...] += jnp.sum(dout_v * yn * sil, axis=0, keepdims=True)
        dz = dout_v * yn * ng * dsil_z
        dyn = dout_v * ng * sil
        d_o = rinv * (dyn - yn * (_group_sum(dyn * yn) * (1.0 / HEAD)))
        last_row = (lax.broadcasted_iota(jnp.int32, (CHUNK, LANES), 0) == CHUNK - 1).astype(F32)
        bd = _block_diag_mask()
        gsum = _group_sum_slab
        t_all, st_all = t_ref[...], st_ref[...]
        dsns = [dstate[LANES * sl:LANES * (sl + 1), :] for sl in range(n_slab)]
        per_chunk = {}
        order = [(c_, s_) for c_ in reversed(range(gch)) for s_ in range(n_slab)]
        chain = {}
        for c, sl in order:
            lanes = slice(LANES * sl, LANES * (sl + 1))
            tok = slice(CHUNK * c, CHUNK * (c + 1))
            sq = _gdn_slab(fr, c, sl, tri, transposed=True)
            t = t_all[tok, lanes]
            st = st_all[w * c + LANES * sl:w * c + LANES * (sl + 1), :]
            dsn = dsns[sl]
            do_s = d_o[tok, lanes]
            u = _mm(t, _stack_heads(sq["v"] * sq["beta"]))
            wk = _mm(t, _stack_heads(sq["k"] * (sq["beta"] * sq["e"])))
            kt = sq["k"] * sq["xt"]
            dobd = _stack_heads(do_s)
            dvnew = _mm(sq["kqr"] * sq["dec_t"], dobd) + _mm(kt, dsn)
            dsns[sl] = _mm_tn(sq["q"] * sq["e"], do_s) * bd + sq["egl"] * dsn - _mm_tn(wk, dvnew) * bd
            chain[(c, sl)] = (sq, t, st, dsn, do_s, dobd, u, wk, kt, dvnew)
        for c, sl in order:
            sq, t, st, dsn, do_s, dobd, u, wk, kt, dvnew = chain[(c, sl)]
            k, q, v, beta, e, xt, dec, kk, qkr, kbd = (sq[n_] for n_ in ("k", "q", "v", "beta", "e", "xt", "dec", "kk", "qkr", "kbd"))
            dec_t, beta_t, kqr, qbd = sq["dec_t"], sq["beta_t"], sq["kqr"], sq["qbd"]
            t_t = _head_transpose(t)
            vnew = u - _mm(wk, st)
            dqd = _mm_nt(do_s, st)
            dqk = _mm_nt(do_s, _stack_heads(vnew))
            dqk_t = _mm_nt(vnew, dobd)
            dkt = _mm_nt(vnew, dsn)
            dgl = _head_total(dsn * st) * sq["egl"]
            dwk = -_mm_nt(dvnew, st)
            drv = _mm(t_t, _stack_heads(dvnew))
            drk = _mm(t_t, _stack_heads(dwk))
            dnm = jnp.where(strict, -(_mm_nt(drv, _stack_heads(u)) + _mm_nt(drk, _stack_heads(wk))), 0.0)
            dnm_t = jnp.where(sq["strict_t"], -(_mm_nt(u, _stack_heads(drv)) + _mm_nt(wk, _stack_heads(drk))), 0.0)
            dbeta = gsum(dnm * kk * dec)
            dkk = dnm * beta * dec
            dkk_t = dnm_t * beta_t * dec_t
            ddec = dnm * beta * kk + dqk * qkr
            dd_t = (dnm_t * beta_t * kk + dqk_t * kqr) * dec_t
            dq = _mm(dqk * dec, kbd) + dqd * e
            dk = _mm(dqk_t * dec_t, qbd) + _mm(dkk + dkk_t, kbd) + drk * (beta * e) + dkt * xt
            rks = gsum(drk * k)
            dbeta = dbeta + gsum(drv * v) + rks * e
            de = rks * beta + gsum(dqd * q)
            dxt = gsum(dkt * k) * xt
            dgl = dgl + jnp.sum(dxt, axis=0, keepdims=True)
            dgc = de * e - dxt + gsum(ddec * dec) - gsum(dd_t) + last_row * dgl
            per_chunk[(c, sl)] = dict(dq=dq * (HEAD ** -0.5), dk=dk, dv=drv * beta, dbeta=dbeta, dgc=dgc)
        for sl in range(n_slab):
            dstate[LANES * sl:LANES * (sl + 1), :] = dsns[sl]

        def block_of(name, suffix_sum=False):
            rows = []
            for c in range(gch):
                r = jnp.concatenate([per_chunk[(c, sl)][name] for sl in range(n_slab)], axis=1)
                rows.append(_rows_suffix_sum(r) if suffix_sum else r)
            return jnp.concatenate(rows, axis=0)

        dg = block_of("dgc", suffix_sum=True)
        dal_ref[...] += jnp.sum(dg * fr["g"], axis=0, keepdims=True)
        da = dg * fr["neg_a"] * _sigmoid(fr["sp_in"])
        ddt_ref[...] += jnp.sum(da, axis=0, keepdims=True)
        beta_all = fr["beta"]
        db = block_of("dbeta") * beta_all * (1.0 - beta_all)
        lane = lax.broadcasted_iota(jnp.int32, (CHUNK * gch, LANES), 1)
        dab = jnp.zeros((CHUNK * gch, LANES), F32)
        for hd in range(GDN_HEADS):
            dab = jnp.where(lane == hd, da[:, HEAD * hd:HEAD * hd + 1], dab)
            dab = jnp.where(lane == GDN_HEADS + hd, db[:, HEAD * hd:HEAD * hd + 1], dab)
        dab_ref[...] = dab.astype(BF16)
        dqn = block_of("dq")
        dkn = block_of("dk")
        dq_raw = fr["rq"] * (dqn - fr["qn"] * _group_sum(dqn * fr["qn"]))
        dk_raw = fr["rk"] * (dkn - fr["kn"] * _group_sum(dkn * fr["kn"]))
        dy = jnp.concatenate([dq_raw, dk_raw, block_of("dv")], axis=1) * fr["dsil"]
        nxt = dy_next[...]
        x = fr["x"]
        dx = cw_ref[3:4, :] * dy
        dcw_ref[3:4, :] += jnp.sum(dy * x, axis=0, keepdims=True)
        for j in (1, 2, 3):
            dx = dx + cw_ref[3 - j:4 - j, :] * _shift_rows_up(dy, nxt, j)
            dcw_ref[3 - j:4 - j, :] += jnp.sum(dy * _shift_rows(x, hprev, j), axis=0, keepdims=True)
        dy_next[...] = dy
        dhx_ref[:, 0:GDN_QKV] = dx.astype(BF16)
        dhx_ref[:, 3 * w:4 * w] = dz.astype(BF16)

    cst = lambda shape: pl.BlockSpec(shape, lambda i: (0, 0))
    row = lambda width: pl.BlockSpec((CHUNK * gch, width), lambda i: (rev(i), 0))
    buf = lambda width: pltpu.VMEM((CHUNK * gch, width), F32)
    f_in, f_out, f_scr, _ = fused if fused is not None else ([], [], [], None)
    outs = pl.pallas_call(
        _fuse(body, 10, 6, 2, n_blocks, fused), name="gdn_bwd" + ("_exchange" if fused is not None else ""), grid=(n_blocks,),
        in_specs=[row(GDN_IN), pl.BlockSpec((CHUNK * gch, GDN_QKV), lambda i: (jnp.maximum(rev(i) - 1, 0), 0)),
                  cst((4, GDN_QKV)), cst((1, w)), cst((1, w)), cst((1, w)), row(w), row(w),
                  pl.BlockSpec((w * gch, LANES), lambda i: (rev(i), 0)), row(w)] + [_ANY] * len(f_in),
        out_specs=[row(4 * w), row(LANES), cst((4, GDN_QKV)), cst((1, w)), cst((1, w)), cst((1, w))] + [_ANY] * len(f_out),
        out_shape=[jax.ShapeDtypeStruct((s, 4 * w), BF16), jax.ShapeDtypeStruct((s, LANES), BF16),
                   jax.ShapeDtypeStruct((4, GDN_QKV), F32),
                   jax.ShapeDtypeStruct((1, w), F32), jax.ShapeDtypeStruct((1, w), F32), jax.ShapeDtypeStruct((1, w), F32)]
        + list(f_out),
        scratch_shapes=[pltpu.VMEM((w, LANES), F32), buf(GDN_QKV)] + list(f_scr),
        compiler_params=_params("arbitrary"),
    )(hx, hx, conv_w, a_log_e, dt_bias_e, norm_g_e, opre, tmat, states, dout, *f_in)
    return tuple(outs[:6]) + ((list(outs[6:]),) if fused is not None else ())


_MESH = pl.DeviceIdType.MESH


def all_gather8(x, name):
    m, n = x.shape

    def body(x_ref, out_ref, send_sems, recv_sems, local_sem):
        px, py, pc = lax.axis_index("x"), lax.axis_index("y"), lax.axis_index("c")
        me, sibling = (px, py, pc), (px, py, 1 - pc)
        chips = [(1 - px, py), (px, 1 - py), (1 - px, 1 - py)]

        def slot(dx, dy, dc):
            return out_ref.at[4 * dx + 2 * dy + dc]

        def copy(k, block, to, src=None):
            return pltpu.make_async_remote_copy(
                src_ref=slot(*block) if src is None else src, dst_ref=slot(*block),
                send_sem=send_sems.at[k], recv_sem=recv_sems.at[k], device_id=to, device_id_type=_MESH)

        mine = pltpu.make_async_copy(x_ref, slot(*me), local_sem)
        mine.start()
        first = [copy(0, me, sibling, src=x_ref)]
        first += [copy(1 + j, me, (*chip, pc), src=x_ref) for j, chip in enumerate(chips)]
        for cp in first:
            cp.start()
        passed = [copy(4 + j, (*chip, pc), sibling) for j, chip in enumerate(chips)]
        for j, chip in enumerate(chips):
            copy(1 + j, (*chip, pc), me).wait_recv()
            passed[j].start()
        copy(0, sibling, me).wait_recv()
        for j, chip in enumerate(chips):
            copy(4 + j, (*chip, 1 - pc), me).wait_recv()
        for cp in first + passed:
            cp.wait_send()
        mine.wait()

    return pl.pallas_call(
        body, name=name, out_shape=jax.ShapeDtypeStruct((N_DEV, m, n), x.dtype),
        in_specs=[_ANY], out_specs=_ANY,
        scratch_shapes=[pltpu.SemaphoreType.DMA((7,)), pltpu.SemaphoreType.DMA((7,)), pltpu.SemaphoreType.DMA],
    )(x)


def _weight_gather_steps(s0, s1, f0, f1, sems):
    n0 = len(s0)
    own_send, own_recv, ici_send, ici_recv, fwd_send, fwd_recv = sems
    px, py, pc = lax.axis_index("x"), lax.axis_index("y"), lax.axis_index("c")
    mine = 2 * px + py
    sibling = (px, py, 1 - pc)
    chips = [(1 - px, py), (px, 1 - py), (1 - px, 1 - py)]

    def copy(src, dst, sems_s, sems_r, k, to):
        return pltpu.make_async_remote_copy(src_ref=src, dst_ref=dst, send_sem=sems_s.at[k], recv_sem=sems_r.at[k],
                                            device_id=to, device_id_type=_MESH)

    def own_copies():
        return [copy(shards[i], full[i].at[mine], own_send, own_recv, base + i, sibling)
                for base, shards, full in ((0, s0, f0), (n0, s1, f1)) for i in range(len(shards))]

    def first_copies(my_shards, my_full):
        ici = [copy(my_shards[i], my_full[i].at[mine], ici_send, ici_recv, 3 * i + j, (cx, cy, pc))
               for i in range(len(my_shards)) for j, (cx, cy) in enumerate(chips)]
        return own_copies() + ici

    def begin(my_shards, my_full):
        for cp in first_copies(my_shards, my_full):
            cp.start()

    def end(my_shards, my_full, other_full):
        fwd = []
        for i in range(len(my_shards)):
            for j, (cx, cy) in enumerate(chips):
                slot = my_full[i].at[2 * cx + cy]
                copy(my_shards[i], slot, ici_send, ici_recv, 3 * i + j, (cx, cy, pc)).wait_recv()
                cp = copy(slot, slot, fwd_send, fwd_recv, 3 * i + j, sibling)
                cp.start()
                fwd.append(cp)
        for cp in own_copies():
            cp.wait_recv()
        for i in range(len(other_full)):
            for j, (cx, cy) in enumerate(chips):
                slot = other_full[i].at[2 * cx + cy]
                copy(slot, slot, fwd_send, fwd_recv, 3 * i + j, sibling).wait_recv()
        for cp in first_copies(my_shards, my_full) + fwd:
            cp.wait_send()

    def start():
        @pl.when(pc == 0)
        def _():
            begin(s0, f0)

        @pl.when(pc == 1)
        def _():
            begin(s1, f1)

    def finish():
        @pl.when(pc == 0)
        def _():
            end(s0, f0, f1)

        @pl.when(pc == 1)
        def _():
            end(s1, f1, f0)

    return start, finish


def _weight_gather_operands(shards0, shards1):
    both, most = len(shards0) + len(shards1), max(len(shards0), len(shards1))
    full = [jax.ShapeDtypeStruct((N_CHIPS,) + v.shape, v.dtype) for v in list(shards0) + list(shards1)]
    dma = pltpu.SemaphoreType.DMA
    return full, [dma((both,)), dma((both,)), dma((3 * most,)), dma((3 * most,)), dma((3 * most,)), dma((3 * most,))]


def _weight_gather_fused(shards0, shards1):
    n0, both = len(shards0), len(shards0) + len(shards1)
    full, sems = _weight_gather_operands(shards0, shards1)
    steps = lambda ins, outs, scr: _weight_gather_steps(ins[:n0], ins[n0:both], outs[:n0], outs[n0:both], scr)
    return list(shards0) + list(shards1), full, sems, steps


def gather_layer_weights(shards0, shards1, name):
    ins, full, sems, steps = _weight_gather_fused(shards0, shards1)
    both = len(ins)

    def body(*refs):
        start, finish = steps(refs[0:both], refs[both:2 * both], refs[2 * both:])
        start()
        finish()

    return pl.pallas_call(
        body, name=name, out_shape=full, in_specs=[_ANY] * both, out_specs=[_ANY] * both, scratch_shapes=sems,
    )(*ins)


def _piece_offsets(pieces):
    offs = [0]
    for r, _ in pieces:
        offs.append(offs[-1] + r)
    return offs


def _chip_exchange_steps(srcs, q_ref, sems, pieces, owner):
    send_sems, recv_sems = sems
    offs = _piece_offsets(pieces)
    px, py, pc = lax.axis_index("x"), lax.axis_index("y"), lax.axis_index("c")
    mine = 2 * px + py
    chips = [(1 - px, py), (px, 1 - py), (1 - px, 1 - py)]

    def copies():
        sends = []
        for i, (r, stride) in enumerate(pieces):
            dst = pl.ds(offs[i], r)
            for j, (cx, cy) in enumerate(chips):
                sends.append(pltpu.make_async_remote_copy(
                    src_ref=srcs[i].at[pl.ds((2 * cx + cy) * stride, r)], dst_ref=q_ref.at[mine, dst],
                    send_sem=send_sems.at[3 * i + j], recv_sem=recv_sems.at[3 * i + j], device_id=(cx, cy, pc),
                    device_id_type=_MESH))
        return sends

    def start():
        @pl.when(pc == owner)
        def _():
            for cp in copies():
                cp.start()

    def finish():
        @pl.when(pc == owner)
        def _():
            for i, (r, stride) in enumerate(pieces):
                dst = pl.ds(offs[i], r)
                for j, (cx, cy) in enumerate(chips):
                    pltpu.make_async_remote_copy(
                        src_ref=srcs[i].at[pl.ds(mine * stride, r)], dst_ref=q_ref.at[2 * cx + cy, dst],
                        send_sem=send_sems.at[3 * i + j], recv_sem=recv_sems.at[3 * i + j], device_id=(cx, cy, pc),
                        device_id_type=_MESH).wait_recv()
            for cp in copies():
                cp.wait_send()

    return start, finish


def _chip_exchange_operands(arrays, pieces):
    n = len(pieces)
    dma = pltpu.SemaphoreType.DMA
    q = jax.ShapeDtypeStruct((N_CHIPS, _piece_offsets(pieces)[-1], arrays[0].shape[1]), arrays[0].dtype)
    return q, [dma((3 * n,)), dma((3 * n,))]


def _own_share(arrays, pieces, chip):
    return jnp.concatenate([lax.dynamic_slice_in_dim(arr, chip * stride, r, axis=0) for arr, (r, stride) in zip(arrays, pieces)],
                           axis=0)


def chip_exchange(arrays, pieces, owner, name):
    n = len(pieces)

    def body(*refs):
        start, finish = _chip_exchange_steps(refs[0:n], refs[n], refs[n + 1:], pieces, owner)
        start()
        finish()

    q, sems = _chip_exchange_operands(arrays, pieces)
    return pl.pallas_call(body, name=name, out_shape=q, in_specs=[_ANY] * n, out_specs=_ANY, scratch_shapes=sems)(*arrays)


def sibling_send(arrays, to_core, name):
    n = len(arrays)

    def body(*refs):
        srcs, outs = refs[0:n], refs[n:2 * n]
        send_sems, recv_sems = refs[2 * n:]
        px, py, pc = lax.axis_index("x"), lax.axis_index("y"), lax.axis_index("c")
        cps = [pltpu.make_async_remote_copy(
            src_ref=srcs[i], dst_ref=outs[i], send_sem=send_sems.at[i], recv_sem=recv_sems.at[i],
            device_id=(px, py, to_core), device_id_type=_MESH) for i in range(n)]

        @pl.when(pc != to_core)
        def _():
            for cp in cps:
                cp.start()
            for cp in cps:
                cp.wait_send()

        @pl.when(pc == to_core)
        def _():
            for cp in cps:
                cp.wait_recv()

    return pl.pallas_call(
        body, name=name, out_shape=[jax.ShapeDtypeStruct(v.shape, v.dtype) for v in arrays],
        in_specs=[_ANY] * n, out_specs=[_ANY] * n,
        scratch_shapes=[pltpu.SemaphoreType.DMA((n,)), pltpu.SemaphoreType.DMA((n,))],
    )(*arrays)


def _fuse(body, n_in, n_out, n_scratch, n_steps, fused):
    if fused is None:
        return body
    f_in, f_out, f_scr, steps = fused
    a, b, c = len(f_in), len(f_out), len(f_scr)

    def wrapped(*refs):
        ins, rest = refs[:n_in + a], refs[n_in + a:]
        outs, scr = rest[:n_out + b], rest[n_out + b:]
        start, finish = steps(ins[n_in:], outs[n_out:], scr[n_scratch:])
        step = pl.program_id(0)

        @pl.when(step == 0)
        def _():
            start()

        body(*ins[:n_in], *outs[:n_out], *scr[:n_scratch])

        @pl.when(step == n_steps - 1)
        def _():
            finish()

    return wrapped


def sibling_swap(x, name):
    def body(x_ref, out_ref, send_sem, recv_sem):
        px, py, pc = lax.axis_index("x"), lax.axis_index("y"), lax.axis_index("c")
        cp = pltpu.make_async_remote_copy(
            src_ref=x_ref, dst_ref=out_ref, send_sem=send_sem, recv_sem=recv_sem,
            device_id=(px, py, 1 - pc), device_id_type=_MESH)
        cp.start()
        cp.wait()

    return pl.pallas_call(
        body, name=name, out_shape=jax.ShapeDtypeStruct(x.shape, x.dtype), in_specs=[_ANY], out_specs=_ANY,
        scratch_shapes=[pltpu.SemaphoreType.DMA, pltpu.SemaphoreType.DMA],
    )(x)


ELT_TILE = 128


def _elt_rows(m):
    for t in (512, 256, ELT_TILE, 16, 8):
        if m % t == 0:
            return t
    return m


def pair_add(a, b, name):
    m, n = b.shape
    tm = _elt_rows(m)

    def body(a_ref, b_ref, o_ref):
        o_ref[...] = (a_ref[...].astype(F32) + b_ref[...].astype(F32)).astype(o_ref.dtype)

    return pl.pallas_call(
        body, name=name, grid=(m // tm,), in_specs=[_row_spec(tm, n)] * 2, out_specs=_row_spec(tm, n),
        out_shape=jax.ShapeDtypeStruct((m, n), b.dtype), compiler_params=_params("arbitrary"),
    )(a, b)


def sum_leading(q, name):
    kk, m, n = q.shape
    tm = _elt_rows(m)

    def body(q_ref, o_ref):
        acc = q_ref[0].astype(F32)
        for i in range(1, kk):
            acc = acc + q_ref[i].astype(F32)
        o_ref[...] = acc

    return pl.pallas_call(
        body, name=name, grid=(m // tm,), in_specs=[pl.BlockSpec((kk, tm, n), lambda i: (0, i, 0))],
        out_specs=_row_spec(tm, n), out_shape=jax.ShapeDtypeStruct((m, n), F32), compiler_params=_params("arbitrary"),
    )(q)


def sum_shares(q0, q1, own0, own1, name):
    _, m, n = q0.shape
    tm = _elt_rows(m)

    def body(q0_ref, q1_ref, o0_ref, o1_ref, out_ref):
        first = lax.axis_index("c") == 0
        mine = 2 * lax.axis_index("x") + lax.axis_index("y")
        own = jnp.where(first, o0_ref[...], o1_ref[...])
        acc = None
        for k in range(N_CHIPS):
            term = jnp.where(mine == k, own, jnp.where(first, q0_ref[k], q1_ref[k])).astype(F32)
            acc = term if acc is None else acc + term
        out_ref[...] = acc

    slots = pl.BlockSpec((N_CHIPS, tm, n), lambda i: (0, i, 0))
    return pl.pallas_call(
        body, name=name, grid=(m // tm,), in_specs=[slots, slots, _row_spec(tm, n), _row_spec(tm, n)],
        out_specs=_row_spec(tm, n), out_shape=jax.ShapeDtypeStruct((m, n), F32), compiler_params=_params("arbitrary"),
    )(q0, q1, own0, own1)


def adamw(w, g, m, v, name):
    rows, cols = w.shape
    tm = _elt_rows(rows)

    def body(w_ref, g_ref, m_ref, v_ref, d_ref, nm_ref, nv_ref):
        gv = g_ref[...]
        nm = ADAM_B1 * m_ref[...] + (1.0 - ADAM_B1) * gv
        nv = ADAM_B2 * v_ref[...] + (1.0 - ADAM_B2) * jnp.square(gv)
        nm_ref[...] = nm
        nv_ref[...] = nv
        m_hat = nm / (1.0 - ADAM_B1 ** ADAM_STEP)
        v_hat = nv / (1.0 - ADAM_B2 ** ADAM_STEP)
        d_ref[...] = -ADAM_LR * (m_hat / (jnp.sqrt(v_hat) + ADAM_EPS) + ADAM_WD * w_ref[...])

    spec = _row_spec(tm, cols)
    return pl.pallas_call(
        body, name=name, grid=(rows // tm,), in_specs=[spec] * 4, out_specs=[spec] * 3,
        out_shape=[jax.ShapeDtypeStruct((rows, cols), F32)] * 3, compiler_params=_params("arbitrary"),
    )(w, g, m, v)


def _block_diag_dense(w):
    g = w.shape[0]
    return jnp.einsum("gij,gh->gihj", w, jnp.eye(g, dtype=w.dtype)).reshape(g * w.shape[1], g * w.shape[2])


def _diag_blocks(m):
    return jnp.stack([m[HEAD * i:HEAD * (i + 1), HEAD * i:HEAD * (i + 1)] for i in range(LRU_BLOCKS)])


def _rep(v):
    return jnp.repeat(v, HEAD, axis=-1)


def _split_w_in(w_in):
    gdn0 = RET_IN + LRU_IN
    gdn1 = gdn0 + 4 * GDN_W
    w_r = w_in[:, 0:RET_IN]
    w_l = w_in[:, RET_IN:gdn0]
    w_g = jnp.concatenate([w_in[:, gdn0:gdn1], _rep(w_in[:, gdn1:gdn1 + GDN_HEADS]), _rep(w_in[:, gdn1 + GDN_HEADS:])], axis=1)
    w_ab = jnp.pad(w_in[:, gdn1:], ((0, 0), (0, LANES - 2 * GDN_HEADS)))
    return w_r, w_l, w_g, w_ab


WIN_SHARD = D_IN // N_CHIPS
WIN_STRIDE = 832
WIN_ROWS = 960
WIN_T_ROWS = WIN_STRIDE * (N_CHIPS - 1) + WIN_ROWS
AB_ROWS = 16

_GRAD_PIECES = (("ffn1_w_gate", 704, 704), ("ffn1_w_up", 704, 704), ("ffn1_w_down", 704, 704), ("w_in", WIN_ROWS, WIN_STRIDE),
                ("w_out", 256, 256), ("ffn2_w_gate", 704, 704), ("ffn2_w_up", 704, 704), ("ffn2_w_down", 704, 704),
                ("ple_w_gate", 256, 256), ("ple_w_proj", 64, 64))
_TRANSPOSED = ("ffn1_w_gate", "ffn1_w_up", "w_in", "ffn2_w_gate", "ffn2_w_up", "ple_w_proj")
N_EARLY = 5


def _local_step(x, p, pos, target, wt, mesh=None):
    row = lambda v: v[None, :]
    saved = []
    xb = x.astype(BF16)
    pieces = [(r, stride) for _, r, stride in _GRAD_PIECES]
    grad_names = [n for n, _, _ in _GRAD_PIECES]
    n_late = len(pieces) - N_EARLY
    for i in range(DEPTH):
        ffn1 = (wt["ffn1_w_gate"][i], wt["ffn1_w_up"][i], wt["ffn1_w_down"][i])
        if mesh is not None and i == 0:
            half0, half1, make = mesh["rest0"]
            hg1, hu1, r1, x1, x1b, gathered = ffn_fwd(x, row(wt["ln_ffn1_g"][i]), row(wt["ln_ffn1_b"][i]), *ffn1,
                                                      fused=_weight_gather_fused(half0, half1))
            wt = {**wt, **{n: [w0, None] for n, w0 in make(gathered).items()}}
        else:
            hg1, hu1, r1, x1, x1b = ffn_fwd(x, row(wt["ln_ffn1_g"][i]), row(wt["ln_ffn1_b"][i]), *ffn1)
        w_r, w_l, w_g, w_ab = _split_w_in(wt["w_in"][i])
        lw = dict(
            wg1=ffn1[0], wu1=ffn1[1], wd1=ffn1[2], w_r=w_r, w_l=w_l, w_g=w_g, w_ab=w_ab,
            w_out=wt["w_out"][i], wg2=wt["ffn2_w_gate"][i], wu2=wt["ffn2_w_up"][i], wd2=wt["ffn2_w_down"][i],
            wpg=wt["ple_w_gate"][i], wpp=wt["ple_w_proj"][i],
            wa=_block_diag_dense(wt["lru_w_a"][i]), wx=_block_diag_dense(wt["lru_w_x"][i]),
            al=row(_rep(wt["gdn_a_log"][i])), dt=row(_rep(wt["gdn_dt_bias"][i])), ng=row(jnp.tile(wt["gdn_norm_g"][i], GDN_HEADS)))
        hr, hl, hgd = win_fwd(x1, w_r, w_l, w_g)
        o_r, opre_r, st_r = ret_fwd(hr, pos, row(wt["ret_norm_g"][i]))
        o_l, xc, hs = lru_fwd(hl, wt["lru_conv_w"][i], row(wt["lru_conv_b"][i]), lw["wa"], row(wt["lru_b_a"][i]), lw["wx"],
                              row(wt["lru_b_x"][i]), row(wt["lru_lambda"][i]))
        if mesh is not None and i == 0:
            half0, half1, make = mesh["layer1"]
            o_g, opre_g, tmat, st_g, gathered = gdn_fwd(hgd, wt["gdn_conv_w"][i], lw["al"], lw["dt"], lw["ng"],
                                                        fused=_weight_gather_fused(half0, half1))
            wt = {**wt, **{n: [wt[n][0], w1] for n, w1 in make(gathered).items()}}
        else:
            o_g, opre_g, tmat, st_g = gdn_fwd(hgd, wt["gdn_conv_w"][i], lw["al"], lw["dt"], lw["ng"])
        r2, x2, x2b, ocat = out_fwd(o_r, o_l, o_g, x1, lw["w_out"], row(wt["ln_mix_g"][i]), row(wt["ln_mix_b"][i]))
        hg2, hu2, r3, x3, x3b, pg, pp = ffn_fwd(x2, row(wt["ln_ffn2_g"][i]), row(wt["ln_ffn2_b"][i]), lw["wg2"], lw["wu2"],
                                                lw["wd2"], ple=(p[i], lw["wpg"], lw["wpp"]))
        saved.append(dict(lw=lw, x0=xb, hg1=hg1, hu1=hu1, r1=r1, x1=x1b, hr=hr, hl=hl, hgd=hgd, ocat=ocat, opre_r=opre_r,
                          st_r=st_r, xc=xc, hs=hs, opre_g=opre_g, tmat=tmat, st_g=st_g, r2=r2, x2=x2b, hg2=hg2, hu2=hu2,
                          r3=r3, pg=pg, pp=pp))
        x, xb = x3, x3b

    dx, loss = loss_and_grad(x, target)
    grads = [None] * DEPTH
    big = [None] * DEPTH
    pair_sums = [None] * DEPTH
    for i in reversed(range(DEPTH)):
        sv = saved[i]
        lw = sv["lw"]
        tag = f"_l{i}"
        dx2, act2, dhg2, dhu2, dy2, dg3, db3, dpg, dpp = ffn_bwd(
            dx, sv["r3"], sv["x2"], sv["hg2"], sv["hu2"], row(wt["ln_ffn2_g"][i]), lw["wg2"], lw["wu2"], lw["wd2"],
            ple=(sv["pg"], sv["pp"], lw["wpg"]))
        g, bg = {}, {}
        bg["ffn2_w_gate"] = wgrad(dhg2, sv["x2"], "wgrad_gate2" + tag)
        bg["ffn2_w_up"] = wgrad(dhu2, sv["x2"], "wgrad_up2" + tag)
        bg["ffn2_w_down"] = wgrad(act2, dy2, "wgrad_down2" + tag)
        bg["ple_w_gate"] = wgrad(sv["x2"], dpg, "wgrad_pgate" + tag)
        bg["ple_w_proj"] = wgrad(dpp, p[i], "wgrad_pproj" + tag).reshape(PLE_DIM, D_MODEL)
        g["ln_ffn2_g"], g["ln_ffn2_b"] = dg3[0], db3[0]
        dr2, dr2b, do_r, do_l, do_g, dg2, db2 = out_bwd(dx2, sv["r2"], row(wt["ln_mix_g"][i]), lw["w_out"])
        g["ln_mix_g"], g["ln_mix_b"] = dg2[0], db2[0]
        bg["w_out"] = wgrad(sv["ocat"], dr2b, "wgrad_out" + tag)
        dhr, dgn = ret_bwd(sv["hr"], pos, row(wt["ret_norm_g"][i]), sv["opre_r"], sv["st_r"], do_r)
        g["ret_norm_g"] = dgn[0]
        dhl, dcw, dcb, dwa, dba, dwx, dbx, dlam = lru_bwd(
            sv["hl"], wt["lru_conv_w"][i], row(wt["lru_conv_b"][i]), lw["wa"], row(wt["lru_b_a"][i]), lw["wx"],
            row(wt["lru_b_x"][i]), row(wt["lru_lambda"][i]), sv["xc"], sv["hs"], do_l)
        g["lru_conv_w"], g["lru_conv_b"] = dcw, dcb[0]
        g["lru_w_a"], g["lru_b_a"], g["lru_w_x"], g["lru_b_x"], g["lru_lambda"] = _diag_blocks(dwa), dba[0], _diag_blocks(dwx), dbx[0], dlam[0]
        if mesh is not None and i == 0:
            early = [bg[n] for n in grad_names[n_late:]]
            early_sums = [pair_add(u, v, "reduce_pair_add_l0_" + n) for n, u, v in
                          zip(grad_names[n_late:], early, sibling_send(early, 0, "reduce_pair_send_l0_early"))]
            q1_shape, sems1 = _chip_exchange_operands(pair_sums[1], pieces)
            q0_shape, sems0 = _chip_exchange_operands(early_sums, pieces[n_late:])
            n1 = len(pieces)

            def steps(ins, outs, scr):
                start1, finish1 = _chip_exchange_steps(ins[:n1], outs[0], scr[:2], pieces, 1)
                start0, finish0 = _chip_exchange_steps(ins[n1:], outs[1], scr[2:], pieces[n_late:], 0)
                return (lambda: (start1(), start0())), (lambda: (finish1(), finish0()))

            dhq, dab, dgcw, dal, ddt, dng, arrived = gdn_bwd(
                sv["hgd"], wt["gdn_conv_w"][i], lw["al"], lw["dt"], lw["ng"], sv["opre_g"], sv["tmat"], sv["st_g"], do_g,
                fused=(pair_sums[1] + early_sums, [q1_shape, q0_shape], sems1 + sems0, steps))
            big[1], early_arrived = arrived
        else:
            dhq, dab, dgcw, dal, ddt, dng = gdn_bwd(sv["hgd"], wt["gdn_conv_w"][i], lw["al"], lw["dt"], lw["ng"], sv["opre_g"],
                                                    sv["tmat"], sv["st_g"], do_g)
        g["gdn_conv_w"] = dgcw
        g["gdn_a_log"], g["gdn_dt_bias"] = dal[0, ::HEAD], ddt[0, ::HEAD]
        g["gdn_norm_g"] = dng[0].reshape(GDN_HEADS, HEAD).sum(0)
        dx1 = win_bwd(dr2, dhr, dhl, dhq, dab, lw["w_r"], lw["w_l"], lw["w_g"], lw["w_ab"])
        used = RET_IN + LRU_IN + 4 * GDN_W + AB_ROWS
        bg["w_in"] = jnp.concatenate(
            [wgrad(dhr, sv["x1"], "wgrad_in_r" + tag), wgrad(dhl, sv["x1"], "wgrad_in_l" + tag),
             wgrad(dhq, sv["x1"], "wgrad_in_q" + tag), wgrad(dab, sv["x1"], "wgrad_in_ab" + tag)[0:AB_ROWS],
             jnp.zeros((WIN_T_ROWS - used, D_MODEL), BF16)], axis=0)
        dx, act1, dhg1, dhu1, dy1, dg1, db1 = ffn_bwd(dx1, sv["r1"], sv["x0"], sv["hg1"], sv["hu1"], row(wt["ln_ffn1_g"][i]),
                                                      lw["wg1"], lw["wu1"], lw["wd1"])
        bg["ffn1_w_gate"] = wgrad(dhg1, sv["x0"], "wgrad_gate1" + tag)
        bg["ffn1_w_up"] = wgrad(dhu1, sv["x0"], "wgrad_up1" + tag)
        bg["ffn1_w_down"] = wgrad(act1, dy1, "wgrad_down1" + tag)
        g["ln_ffn1_g"], g["ln_ffn1_b"] = dg1[0], db1[0]
        grads[i] = g
        if mesh is None:
            big[i] = bg
        else:
            names = grad_names if i == 1 else grad_names[:n_late]
            mine = [bg[n] for n in names]
            theirs = sibling_send(mine, i, f"reduce_pair_send_l{i}")
            sums = [pair_add(u, v, f"reduce_pair_add_l{i}_" + n) for n, u, v in zip(names, mine, theirs)]
            if i == 1:
                pair_sums[1] = sums
            else:
                late_arrived = chip_exchange(sums, pieces[:n_late], 0, "reduce_chip_exchange_l0")
                pair_sums[0] = sums + early_sums
                big[0] = jnp.concatenate([late_arrived, early_arrived], axis=1)
    if mesh is not None:
        chip = 2 * lax.axis_index("x") + lax.axis_index("y")
        big = (big, [_own_share(pair_sums[layer], pieces, chip) for layer in range(DEPTH)])
    return loss, dx, {k: jnp.stack([grads[i][k] for i in range(DEPTH)]) for k in grads[0]}, big


def _natural_grad(name, rows):
    if name == "ple_w_proj":
        return rows.reshape(-1, PLE_DIM).T
    return rows.T if name in _TRANSPOSED else rows


_SPLIT = dict(ffn1_w_gate=2, ffn1_w_up=2, ffn1_w_down=1, w_in=2, w_out=1, ffn2_w_gate=2, ffn2_w_up=2, ffn2_w_down=1,
              ple_w_gate=1, ple_w_proj=2)
_CONV = ("lru_conv_w", "gdn_conv_w")
_WHOLE = ("ln_ffn1_g", "ln_ffn1_b", "ret_norm_g", "lru_conv_b", "lru_w_a", "lru_b_a", "lru_w_x", "lru_b_x", "lru_lambda",
          "gdn_a_log", "gdn_dt_bias", "gdn_norm_g", "ln_mix_g", "ln_mix_b", "ln_ffn2_g", "ln_ffn2_b")
_WEIGHTS = ("ln_ffn1_g", "ln_ffn1_b", "ffn1_w_gate", "ffn1_w_up", "ffn1_w_down", "w_in", "ret_norm_g", "lru_conv_w", "lru_conv_b",
            "lru_w_a", "lru_b_a", "lru_w_x", "lru_b_x", "lru_lambda", "gdn_conv_w", "gdn_a_log", "gdn_dt_bias", "gdn_norm_g",
            "w_out", "ln_mix_g", "ln_mix_b", "ffn2_w_gate", "ffn2_w_up", "ffn2_w_down", "ple_w_gate", "ple_w_proj",
            "ln_ffn2_g", "ln_ffn2_b")
_INPUTS = ("x", "p", "positions") + _WEIGHTS + ("loss_target",) + tuple("m_" + n for n in _WEIGHTS) + tuple("v_" + n for n in _WEIGHTS)

BIG_COLS = 1024
SMALL_COLS = LANES
SMALL_ROWS_MULT = 8


def _pack(arrays, dtype, cols, rows_mult):
    flat = jnp.concatenate([a.reshape(-1).astype(dtype) for a in arrays])
    rows = -(-flat.shape[0] // cols)
    rows = -(-rows // rows_mult) * rows_mult
    return jnp.pad(flat, (0, rows * cols - flat.shape[0])).reshape(rows, cols)


def _unpack(packed, shapes):
    flat = packed.reshape(-1)
    out, off = [], 0
    for shp in shapes:
        size = int(np.prod(shp))
        out.append(flat[off:off + size].reshape(shp))
        off += size
    return out


def _as2d(a):
    return a.reshape(-1, a.shape[-1])


def kernel(x, p, positions, ln_ffn1_g, ln_ffn1_b, ffn1_w_gate, ffn1_w_up, ffn1_w_down, w_in, ret_norm_g, lru_conv_w, lru_conv_b, lru_w_a, lru_b_a, lru_w_x, lru_b_x, lru_lambda, gdn_conv_w, gdn_a_log, gdn_dt_bias, gdn_norm_g, w_out, ln_mix_g, ln_mix_b, ffn2_w_gate, ffn2_w_up, ffn2_w_down, ple_w_gate, ple_w_proj, ln_ffn2_g, ln_ffn2_b, loss_target, m_ln_ffn1_g, m_ln_ffn1_b, m_ffn1_w_gate, m_ffn1_w_up, m_ffn1_w_down, m_w_in, m_ret_norm_g, m_lru_conv_w, m_lru_conv_b, m_lru_w_a, m_lru_b_a, m_lru_w_x, m_lru_b_x, m_lru_lambda, m_gdn_conv_w, m_gdn_a_log, m_gdn_dt_bias, m_gdn_norm_g, m_w_out, m_ln_mix_g, m_ln_mix_b, m_ffn2_w_gate, m_ffn2_w_up, m_ffn2_w_down, m_ple_w_gate, m_ple_w_proj, m_ln_ffn2_g, m_ln_ffn2_b, v_ln_ffn1_g, v_ln_ffn1_b, v_ffn1_w_gate, v_ffn1_w_up, v_ffn1_w_down, v_w_in, v_ret_norm_g, v_lru_conv_w, v_lru_conv_b, v_lru_w_a, v_lru_b_a, v_lru_w_x, v_lru_b_x, v_lru_lambda, v_gdn_conv_w, v_gdn_a_log, v_gdn_dt_bias, v_gdn_norm_g, v_w_out, v_ln_mix_g, v_ln_mix_b, v_ffn2_w_gate, v_ffn2_w_up, v_ffn2_w_down, v_ple_w_gate, v_ple_w_proj, v_ln_ffn2_g, v_ln_ffn2_b):
    a = dict(zip(_INPUTS, (x, p, positions, ln_ffn1_g, ln_ffn1_b, ffn1_w_gate, ffn1_w_up, ffn1_w_down, w_in, ret_norm_g, lru_conv_w, lru_conv_b, lru_w_a, lru_b_a, lru_w_x, lru_b_x, lru_lambda, gdn_conv_w, gdn_a_log, gdn_dt_bias, gdn_norm_g, w_out, ln_mix_g, ln_mix_b, ffn2_w_gate, ffn2_w_up, ffn2_w_down, ple_w_gate, ple_w_proj, ln_ffn2_g, ln_ffn2_b, loss_target, m_ln_ffn1_g, m_ln_ffn1_b, m_ffn1_w_gate, m_ffn1_w_up, m_ffn1_w_down, m_w_in, m_ret_norm_g, m_lru_conv_w, m_lru_conv_b, m_lru_w_a, m_lru_b_a, m_lru_w_x, m_lru_b_x, m_lru_lambda, m_gdn_conv_w, m_gdn_a_log, m_gdn_dt_bias, m_gdn_norm_g, m_w_out, m_ln_mix_g, m_ln_mix_b, m_ffn2_w_gate, m_ffn2_w_up, m_ffn2_w_down, m_ple_w_gate, m_ple_w_proj, m_ln_ffn2_g, m_ln_ffn2_b, v_ln_ffn1_g, v_ln_ffn1_b, v_ffn1_w_gate, v_ffn1_w_up, v_ffn1_w_down, v_w_in, v_ret_norm_g, v_lru_conv_w, v_lru_conv_b, v_lru_w_a, v_lru_b_a, v_lru_w_x, v_lru_b_x, v_lru_lambda, v_gdn_conv_w, v_gdn_a_log, v_gdn_dt_bias, v_gdn_norm_g, v_w_out, v_ln_mix_g, v_ln_mix_b, v_ffn2_w_gate, v_ffn2_w_up, v_ffn2_w_down, v_ple_w_gate, v_ple_w_proj, v_ln_ffn2_g, v_ln_ffn2_b)))
    assert len(a) == len(_INPUTS)
    core = lax.axis_index("c")
    chip = 2 * lax.axis_index("x") + lax.axis_index("y")
    big = list(_SPLIT)

    def group(layer, names, n_first):
        shards = [a[n][layer].astype(BF16) for n in names]

        def make(gathered):
            return {n: jnp.concatenate([gathered[i][k] for k in range(N_CHIPS)], axis=_SPLIT[n] - 1) for i, n in enumerate(names)}

        return shards[:n_first], shards[n_first:], make

    ffn1_0, ffn1_1, make_ffn1 = group(0, big[:3], 2)
    wt = {n: [w0, None] for n, w0 in make_ffn1(gather_layer_weights(ffn1_0, ffn1_1, "gather_weights_ffn1_l0")).items()}
    conv_g = all_gather8(_pack([a[n] for n in _CONV], F32, SMALL_COLS, SMALL_ROWS_MULT), "gather_conv_weights")[0::2]
    conv_g = conv_g.reshape(N_CHIPS, -1)
    off = 0
    for n in _CONV:
        shp = a[n].shape
        size = int(np.prod(shp))
        parts = conv_g[:, off:off + size].reshape((N_CHIPS,) + shp)
        wt[n] = jnp.concatenate([parts[k] for k in range(N_CHIPS)], axis=2)
        off += size
    for n in _WHOLE:
        wt[n] = a[n]

    seq = a["x"].shape[1]
    mesh = dict(rest0=group(0, big[3:], 4), layer1=group(1, big, len(big) // 2))
    loss_part, dx, grads, (arrived, own) = _local_step(a["x"][0], a["p"][:, 0], a["positions"].reshape(seq, 1),
                                                       a["loss_target"][0], wt, mesh=mesh)
    loss = lax.psum(loss_part[0, 0], ("x", "y", "c"))

    my_layer_sum = sum_shares(arrived[0], arrived[1], own[0], own[1], "reduce_chip_sum")
    other_layer_sum = sibling_swap(my_layer_sum, "reduce_pair_share")
    reduced = [jnp.where(core == layer, my_layer_sum, other_layer_sum) for layer in range(DEPTH)]
    big_grads = {}
    off = 0
    for n, r, _ in _GRAD_PIECES:
        per_layer = []
        for layer in range(DEPTH):
            rows = reduced[layer][off:off + r]
            if n == "w_in":
                rows = lax.dynamic_slice_in_dim(rows, chip * (WIN_SHARD - WIN_STRIDE), WIN_SHARD, axis=0)
            per_layer.append(_natural_grad(n, rows))
        big_grads[n] = jnp.stack(per_layer)
        off += r

    small_names = list(_WHOLE) + list(_CONV)
    small_local = _pack([grads[n] for n in small_names], F32, SMALL_COLS, SMALL_ROWS_MULT)
    small_sum = sum_leading(all_gather8(small_local, "gather_small_grads"), "sum_small_grads")
    small_grads = dict(zip(small_names, _unpack(small_sum, [grads[n].shape for n in small_names])))
    for n in _CONV:
        width = a[n].shape[2]
        small_grads[n] = lax.dynamic_slice_in_dim(small_grads[n], chip * width, width, axis=2)

    new = {}
    for n in big:
        d, nm, nv = adamw(_as2d(a[n]), _as2d(big_grads[n]), _as2d(a["m_" + n]), _as2d(a["v_" + n]), "adamw_" + n)
        new[n] = tuple(t.reshape(a[n].shape) for t in (d, nm, nv))
    pk = lambda prefix: _pack([a[prefix + n] for n in small_names], F32, SMALL_COLS, SMALL_ROWS_MULT)
    pg = _pack([small_grads[n] for n in small_names], F32, SMALL_COLS, SMALL_ROWS_MULT)
    outs = adamw(pk(""), pg, pk("m_"), pk("v_"), "adamw_small")
    shapes = [a[n].shape for n in small_names]
    for n, d, nm, nv in zip(small_names, *[_unpack(o, shapes) for o in outs]):
        new[n] = (d, nm, nv)
    all_grads = {**big_grads, **small_grads}
    return (loss, dx[None], *[all_grads[n] for n in _WEIGHTS], *[new[n][0] for n in _WEIGHTS],
            *[new[n][1] for n in _WEIGHTS], *[new[n][2] for n in _WEIGHTS])
```

```python
import functools
import math

import numpy as np
import jax
import jax.numpy as jnp
from jax import lax
from jax.experimental import pallas as pl
from jax.experimental.pallas import tpu as pltpu

F32 = jnp.float32
BF16 = jnp.bfloat16

D_MODEL = 1024
D_FF = 2816
PLE_DIM = 256
DEPTH = 2
CHUNK = 64
RET_HEADS = 4
RET_W = 256
LRU_W = 384
LRU_BLOCKS = 6
GDN_HEADS = 6
GDN_W = 384
HEAD = 64
D_IN = 3340
RET_IN = 4 * RET_W
LRU_IN = 2 * LRU_W
GDN_IN = 6 * GDN_W
ROPE_THETA = 10000.0
ALPHA = (2 * DEPTH) ** 0.25
LN_EPS = 1e-5
LRU_C = 8.0
N_CHIPS = 4
N_DEV = 8

ADAM_LR = 0.001
ADAM_B1 = 0.9
ADAM_B2 = 0.999
ADAM_EPS = 1e-08
ADAM_WD = 0.01
ADAM_STEP = 10

LANES = 128
VMEM_LIMIT = 56 * 1024 * 1024
ROW_TILE = 256
ROW_TILE_BWD = 512
SCAN_TILE = 256


def _params(*sem):
    return pltpu.CompilerParams(dimension_semantics=sem, vmem_limit_bytes=VMEM_LIMIT)


def _operand(a):
    return a.astype(BF16)


def _mm(a, b):
    return jnp.dot(_operand(a), _operand(b), preferred_element_type=F32)


def _mm_nt(a, b):
    return lax.dot_general(_operand(a), _operand(b), (((1,), (1,)), ((), ())), preferred_element_type=F32)


def _mm_tn(a, b):
    return lax.dot_general(_operand(a), _operand(b), (((0,), (0,)), ((), ())), preferred_element_type=F32)


def _split(a):
    hi = a.astype(BF16)
    lo = (a - hi.astype(F32)).astype(BF16)
    return hi, lo


def _mm3(a, b):
    ah, al = _split(a)
    bh, bl = _split(b)
    return _mm(ah, bh) + (_mm(ah, bl) + _mm(al, bh))


def _sigmoid(x):
    return jax.nn.sigmoid(x)


def _log1p(u):
    w = 1.0 + u
    return jnp.where(w == 1.0, u, jnp.log(w) * (u / jnp.where(w == 1.0, 1.0, w - 1.0)))


def _expm1(y):
    u = jnp.exp(y)
    um1 = u - 1.0
    safe = jnp.where((u == 1.0) | (um1 == -1.0), 1.0, jnp.log(jnp.where(u == 0.0, 1.0, u)))
    return jnp.where(u == 1.0, y, jnp.where(um1 == -1.0, -1.0, um1 * (y / safe)))


def _softplus(x):
    return jnp.maximum(x, 0.0) + _log1p(jnp.exp(-jnp.abs(x)))


_GELU_C = math.sqrt(2.0 / math.pi)


def _gelu(x):
    return 0.5 * x * (1.0 + jnp.tanh(_GELU_C * (x + 0.044715 * (x * x * x))))


def _gelu_grad(x):
    t = jnp.tanh(_GELU_C * (x + 0.044715 * (x * x * x)))
    return 0.5 * (1.0 + t) + 0.5 * x * (1.0 - t * t) * (_GELU_C * (1.0 + 3.0 * 0.044715 * (x * x)))


def _silu_and_grad(x):
    s = _sigmoid(x)
    return x * s, s * (1.0 + x * (1.0 - s))


def _group_sum_slab(x):
    lane = lax.broadcasted_iota(jnp.int32, x.shape, 1)
    low = jnp.sum(x[:, 0:LANES // 2], axis=1, keepdims=True)
    high = jnp.sum(x[:, LANES // 2:], axis=1, keepdims=True)
    return jnp.where(lane < LANES // 2, low, high)


def _group_sum(x):
    n = x.shape[1] // LANES
    if n == 1:
        return _group_sum_slab(x)
    return jnp.concatenate([_group_sum_slab(x[:, LANES * i:LANES * (i + 1)]) for i in range(n)], axis=1)


def _rows_prefix_sum(x):
    n = x.shape[0]
    row = lax.broadcasted_iota(jnp.int32, x.shape, 0)
    d = 1
    while d < n:
        x = x + jnp.where(row >= d, pltpu.roll(x, d, 0), 0.0)
        d *= 2
    return x


def _rows_suffix_sum(x):
    n = x.shape[0]
    row = lax.broadcasted_iota(jnp.int32, x.shape, 0)
    d = 1
    while d < n:
        x = x + jnp.where(row < n - d, pltpu.roll(x, n - d, 0), 0.0)
        d *= 2
    return x


def _shift_rows(cur, prev, j):
    row = lax.broadcasted_iota(jnp.int32, cur.shape, 0)
    return jnp.where(row < j, pltpu.roll(prev, j, 0), pltpu.roll(cur, j, 0))


def _shift_rows_up(cur, nxt, j):
    n = cur.shape[0]
    row = lax.broadcasted_iota(jnp.int32, cur.shape, 0)
    return jnp.where(row < n - j, pltpu.roll(cur, n - j, 0), pltpu.roll(nxt, n - j, 0))


def _layer_norm_stats(r):
    mu = jnp.mean(r, axis=-1, keepdims=True)
    d = r - mu
    var = jnp.mean(d * d, axis=-1, keepdims=True)
    rstd = lax.rsqrt(var + LN_EPS)
    return d * rstd, rstd


def _load_resident(step, pairs, sems):
    @pl.when(step == 0)
    def _():
        cps = [pltpu.make_async_copy(h, v, sems.at[i]) for i, (h, v) in enumerate(pairs)]
        for c in cps:
            c.start()
        for c in cps:
            c.wait()


def _row_spec(tile, width):
    return pl.BlockSpec((tile, width), lambda i: (i, 0))


def _full_spec(shape):
    nd = len(shape)
    return pl.BlockSpec(shape, lambda i: (0,) * nd)


_ANY = pl.BlockSpec(memory_space=pl.ANY)


def ffn_fwd(x, ln_g, ln_b, w_gate, w_up, w_down, ple=None, fused=None):
    s = x.shape[0]
    tm = ROW_TILE
    with_ple = ple is not None
    weights = [w_gate, w_up, w_down] + ([ple[1], ple[2]] if with_ple else [])

    def body(*refs):
        it = iter(refs)
        x_ref, g_ref, b_ref = next(it), next(it), next(it)
        p_ref = next(it) if with_ple else None
        w_hbm = [next(it) for _ in weights]
        hg_ref, hu_ref, r_ref, xn_ref, xnb_ref = next(it), next(it), next(it), next(it), next(it)
        pg_ref, pp_ref = (next(it), next(it)) if with_ple else (None, None)
        w_vm = [next(it) for _ in weights]
        sems = next(it)
        _load_resident(pl.program_id(0), list(zip(w_hbm, w_vm)), sems)
        xv = x_ref[...]
        xb = xv.astype(BF16)
        hg = _mm(xb, w_vm[0][...])
        hu = _mm(xb, w_vm[1][...])
        hg_ref[...] = hg
        hu_ref[...] = hu
        act = (hg * _sigmoid(hg)) * hu
        r = ALPHA * xv + 0.5 * _mm(act.astype(BF16), w_vm[2][...])
        if with_ple:
            pg = _mm(xb, w_vm[3][...])
            pp = _mm(p_ref[...].astype(BF16), w_vm[4][...])
            pg_ref[...] = pg
            pp_ref[...] = pp
            r = r + _sigmoid(pg) * pp
        r_ref[...] = r
        xhat, _ = _layer_norm_stats(r)
        xn = xhat * g_ref[...] + b_ref[...]
        xn_ref[...] = xn
        xnb_ref[...] = xn.astype(BF16)

    d, f = D_MODEL, D_FF
    in_specs = [_row_spec(tm, d), _full_spec((1, d)), _full_spec((1, d))]
    args = [x, ln_g, ln_b]
    if with_ple:
        in_specs.append(_row_spec(tm, PLE_DIM))
        args.append(ple[0])
    in_specs += [_ANY] * len(weights)
    args += weights
    out_shape = [jax.ShapeDtypeStruct((s, f), F32), jax.ShapeDtypeStruct((s, f), F32),
                 jax.ShapeDtypeStruct((s, d), F32), jax.ShapeDtypeStruct((s, d), F32), jax.ShapeDtypeStruct((s, d), BF16)]
    out_specs = [_row_spec(tm, f), _row_spec(tm, f), _row_spec(tm, d), _row_spec(tm, d), _row_spec(tm, d)]
    if with_ple:
        out_shape += [jax.ShapeDtypeStruct((s, d), F32)] * 2
        out_specs += [_row_spec(tm, d)] * 2
    scratch = [pltpu.VMEM(w.shape, w.dtype) for w in weights] + [pltpu.SemaphoreType.DMA((len(weights),))]
    f_in, f_out, f_scr, _ = fused if fused is not None else ([], [], [], None)
    n_out = len(out_shape)
    outs = pl.pallas_call(
        _fuse(body, len(args), n_out, len(scratch), s // tm, fused),
        name=("ffn_fwd_ple" if with_ple else "ffn_fwd") + ("_gather" if fused is not None else ""), grid=(s // tm,),
        in_specs=in_specs + [_ANY] * len(f_in), out_specs=out_specs + [_ANY] * len(f_out),
        out_shape=out_shape + list(f_out), scratch_shapes=scratch + list(f_scr), compiler_params=_params("arbitrary"),
    )(*args, *f_in)
    return tuple(outs[:n_out]) + ((list(outs[n_out:]),) if fused is not None else ())


def ffn_bwd(dxn, r, x, hg, hu, ln_g, w_gate, w_up, w_down, ple=None):
    s = x.shape[0]
    with_ple = ple is not None
    d, f = D_MODEL, D_FF
    suffix = "_ple" if with_ple else ""

    tm = ROW_TILE

    def body_a(*refs):
        it = iter(refs)
        dxn_ref, r_ref, hg_ref, hu_ref, g_ref = (next(it) for _ in range(5))
        pg_ref, pp_ref = (next(it), next(it)) if with_ple else (None, None)
        wd_hbm = next(it)
        dr_ref, act_ref, dhg_ref, dhu_ref, dy_ref, dg_ref, db_ref = (next(it) for _ in range(7))
        dpg_ref, dpp_ref = (next(it), next(it)) if with_ple else (None, None)
        wd_vm, sems = next(it), next(it)
        step = pl.program_id(0)
        _load_resident(step, [(wd_hbm, wd_vm)], sems)

        @pl.when(step == 0)
        def _():
            dg_ref[...] = jnp.zeros_like(dg_ref)
            db_ref[...] = jnp.zeros_like(db_ref)

        dxn_v = dxn_ref[...]
        xhat, rstd = _layer_norm_stats(r_ref[...])
        dg_ref[...] += jnp.sum(dxn_v * xhat, axis=0, keepdims=True)
        db_ref[...] += jnp.sum(dxn_v, axis=0, keepdims=True)
        dyh = dxn_v * g_ref[...]
        dr = rstd * (dyh - jnp.mean(dyh, axis=-1, keepdims=True) - xhat * jnp.mean(dyh * xhat, axis=-1, keepdims=True))
        dr_ref[...] = dr
        dy = (0.5 * dr).astype(BF16)
        dy_ref[...] = dy
        da = _mm_nt(dy, wd_vm[...])
        hg_v = hg_ref[...]
        hu_v = hu_ref[...]
        sil, dsil = _silu_and_grad(hg_v)
        act_ref[...] = (sil * hu_v).astype(BF16)
        dhu_ref[...] = (da * sil).astype(BF16)
        dhg_ref[...] = (da * hu_v * dsil).astype(BF16)
        if with_ple:
            sp = _sigmoid(pg_ref[...])
            dpp_ref[...] = (dr * sp).astype(BF16)
            dpg_ref[...] = (dr * pp_ref[...] * sp * (1.0 - sp)).astype(BF16)

    in_specs = [_row_spec(tm, d), _row_spec(tm, d), _row_spec(tm, f), _row_spec(tm, f), _full_spec((1, d))]
    args = [dxn, r, hg, hu, ln_g]
    if with_ple:
        in_specs += [_row_spec(tm, d), _row_spec(tm, d)]
        args += [ple[0], ple[1]]
    out_shape = [jax.ShapeDtypeStruct((s, d), F32), jax.ShapeDtypeStruct((s, f), BF16), jax.ShapeDtypeStruct((s, f), BF16),
                 jax.ShapeDtypeStruct((s, f), BF16), jax.ShapeDtypeStruct((s, d), BF16),
                 jax.ShapeDtypeStruct((1, d), F32), jax.ShapeDtypeStruct((1, d), F32)]
    out_specs = [_row_spec(tm, d), _row_spec(tm, f), _row_spec(tm, f), _row_spec(tm, f), _row_spec(tm, d),
                 _full_spec((1, d)), _full_spec((1, d))]
    if with_ple:
        out_shape += [jax.ShapeDtypeStruct((s, d), BF16)] * 2
        out_specs += [_row_spec(tm, d)] * 2
    first = pl.pallas_call(
        body_a, name="ffn_bwd_hidden" + suffix, grid=(s // tm,), in_specs=in_specs + [_ANY], out_specs=out_specs,
        out_shape=out_shape, scratch_shapes=[pltpu.VMEM(w_down.shape, w_down.dtype), pltpu.SemaphoreType.DMA((1,))],
        compiler_params=_params("arbitrary"),
    )(*args, w_down)
    dr, act, dhg, dhu, dy, dg, db = first[:7]

    tb = min(ROW_TILE_BWD, s)
    weights = [w_gate, w_up] + ([ple[2]] if with_ple else [])

    def body_b(*refs):
        it = iter(refs)
        dr_ref, dhg_ref, dhu_ref = next(it), next(it), next(it)
        dpg_ref = next(it) if with_ple else None
        w_hbm = [next(it) for _ in weights]
        dx_ref = next(it)
        w_vm = [next(it) for _ in weights]
        sems = next(it)
        _load_resident(pl.program_id(0), list(zip(w_hbm, w_vm)), sems)
        dx = ALPHA * dr_ref[...] + _mm_nt(dhg_ref[...], w_vm[0][...]) + _mm_nt(dhu_ref[...], w_vm[1][...])
        if with_ple:
            dx = dx + _mm_nt(dpg_ref[...], w_vm[2][...])
        dx_ref[...] = dx

    in_specs = [_row_spec(tb, d), _row_spec(tb, f), _row_spec(tb, f)] + ([_row_spec(tb, d)] if with_ple else [])
    args = [dr, dhg, dhu] + ([first[7]] if with_ple else [])
    dx = pl.pallas_call(
        body_b, name="ffn_bwd_input" + suffix, grid=(s // tb,), in_specs=in_specs + [_ANY] * len(weights),
        out_specs=_row_spec(tb, d), out_shape=jax.ShapeDtypeStruct((s, d), F32),
        scratch_shapes=[pltpu.VMEM(w.shape, w.dtype) for w in weights] + [pltpu.SemaphoreType.DMA((len(weights),))],
        compiler_params=_params("arbitrary"),
    )(*args, *weights)
    return (dx, act, dhg, dhu, dy, dg, db) + tuple(first[7:])


def win_fwd(x1, w_r, w_l, w_g):
    s = x1.shape[0]
    tm = min(ROW_TILE_BWD, s)
    weights = [w_r, w_l, w_g]

    def body(x_ref, wr_h, wl_h, wg_h, hr_ref, hl_ref, hgd_ref, wr_v, wl_v, wg_v, sems):
        _load_resident(pl.program_id(0), [(wr_h, wr_v), (wl_h, wl_v), (wg_h, wg_v)], sems)
        xb = x_ref[...].astype(BF16)
        hr_ref[...] = _mm(xb, wr_v[...])
        hl_ref[...] = _mm(xb, wl_v[...])
        hgd_ref[...] = _mm(xb, wg_v[...])

    return pl.pallas_call(
        body, name="win_fwd", grid=(s // tm,),
        in_specs=[_row_spec(tm, D_MODEL), _ANY, _ANY, _ANY],
        out_specs=[_row_spec(tm, RET_IN), _row_spec(tm, LRU_IN), _row_spec(tm, GDN_IN)],
        out_shape=[jax.ShapeDtypeStruct((s, RET_IN), F32), jax.ShapeDtypeStruct((s, LRU_IN), F32),
                   jax.ShapeDtypeStruct((s, GDN_IN), F32)],
        scratch_shapes=[pltpu.VMEM(w.shape, w.dtype) for w in weights] + [pltpu.SemaphoreType.DMA((3,))],
        compiler_params=_params("arbitrary"),
    )(x1, *weights)


def win_bwd(dr2, dhr, dhl, dhq, dab, w_r, w_l, w_g, w_ab):
    s = dr2.shape[0]
    tm = min(ROW_TILE_BWD, s)
    weights = [w_r, w_l, w_g, w_ab]
    nq = 4 * GDN_W

    def body(dr_ref, dhr_ref, dhl_ref, dhq_ref, dab_ref, wr_h, wl_h, wg_h, wab_h, dx_ref, wr_v, wl_v, wg_v, wab_v, sems):
        _load_resident(pl.program_id(0), [(wr_h, wr_v), (wl_h, wl_v), (wg_h, wg_v), (wab_h, wab_v)], sems)
        dx_ref[...] = (ALPHA * dr_ref[...] + _mm_nt(dhr_ref[...], wr_v[...]) + _mm_nt(dhl_ref[...], wl_v[...])
                       + _mm_nt(dhq_ref[...], wg_v[:, 0:nq]) + _mm_nt(dab_ref[...], wab_v[...]))

    return pl.pallas_call(
        body, name="win_bwd", grid=(s // tm,),
        in_specs=[_row_spec(tm, D_MODEL), _row_spec(tm, RET_IN), _row_spec(tm, LRU_IN), _row_spec(tm, nq), _row_spec(tm, LANES),
                  _ANY, _ANY, _ANY, _ANY],
        out_specs=_row_spec(tm, D_MODEL),
        out_shape=jax.ShapeDtypeStruct((s, D_MODEL), F32),
        scratch_shapes=[pltpu.VMEM(w.shape, w.dtype) for w in weights] + [pltpu.SemaphoreType.DMA((4,))],
        compiler_params=_params("arbitrary"),
    )(dr2, dhr, dhl, dhq, dab, *weights)


def out_fwd(o_r, o_l, o_g, x1, w_out, ln_g, ln_b):
    s = x1.shape[0]
    tm = ROW_TILE

    def body(or_ref, ol_ref, og_ref, x_ref, g_ref, b_ref, w_h, r_ref, xn_ref, xnb_ref, ocat_ref, w_v, sems):
        _load_resident(pl.program_id(0), [(w_h, w_v)], sems)
        ocat = jnp.concatenate([or_ref[...], ol_ref[...], og_ref[...]], axis=1).astype(BF16)
        ocat_ref[...] = ocat
        r = ALPHA * x_ref[...] + _mm(ocat, w_v[...])
        r_ref[...] = r
        xhat, _ = _layer_norm_stats(r)
        xn = xhat * g_ref[...] + b_ref[...]
        xn_ref[...] = xn
        xnb_ref[...] = xn.astype(BF16)

    d = D_MODEL
    return pl.pallas_call(
        body, name="out_fwd", grid=(s // tm,),
        in_specs=[_row_spec(tm, RET_W), _row_spec(tm, LRU_W), _row_spec(tm, GDN_W), _row_spec(tm, d),
                  _full_spec((1, d)), _full_spec((1, d)), _ANY],
        out_specs=[_row_spec(tm, d)] * 4,
        out_shape=[jax.ShapeDtypeStruct((s, d), F32)] * 2 + [jax.ShapeDtypeStruct((s, d), BF16)] * 2,
        scratch_shapes=[pltpu.VMEM(w_out.shape, w_out.dtype), pltpu.SemaphoreType.DMA((1,))],
        compiler_params=_params("arbitrary"),
    )(o_r, o_l, o_g, x1, ln_g, ln_b, w_out)


def out_bwd(dxn, r2, ln_g, w_out):
    s = dxn.shape[0]
    tm = ROW_TILE

    def body(dxn_ref, r_ref, g_ref, w_h, dr_ref, drb_ref, dor_ref, dol_ref, dog_ref, dg_ref, db_ref, w_v, sems):
        step = pl.program_id(0)
        _load_resident(step, [(w_h, w_v)], sems)

        @pl.when(step == 0)
        def _():
            dg_ref[...] = jnp.zeros_like(dg_ref)
            db_ref[...] = jnp.zeros_like(db_ref)

        dxn_v = dxn_ref[...]
        xhat, rstd = _layer_norm_stats(r_ref[...])
        dg_ref[...] += jnp.sum(dxn_v * xhat, axis=0, keepdims=True)
        db_ref[...] += jnp.sum(dxn_v, axis=0, keepdims=True)
        dyh = dxn_v * g_ref[...]
        dr = rstd * (dyh - jnp.mean(dyh, axis=-1, keepdims=True) - xhat * jnp.mean(dyh * xhat, axis=-1, keepdims=True))
        dr_ref[...] = dr
        drb = dr.astype(BF16)
        drb_ref[...] = drb
        dor_ref[...] = _mm_nt(drb, w_v[0:RET_W, :])
        dol_ref[...] = _mm_nt(drb, w_v[RET_W:RET_W + LRU_W, :])
        dog_ref[...] = _mm_nt(drb, w_v[RET_W + LRU_W:, :])

    d = D_MODEL
    return pl.pallas_call(
        body, name="out_bwd", grid=(s // tm,),
        in_specs=[_row_spec(tm, d), _row_spec(tm, d), _full_spec((1, d)), _ANY],
        out_specs=[_row_spec(tm, d), _row_spec(tm, d), _row_spec(tm, RET_W), _row_spec(tm, LRU_W), _row_spec(tm, GDN_W),
                   _full_spec((1, d)), _full_spec((1, d))],
        out_shape=[jax.ShapeDtypeStruct((s, d), F32), jax.ShapeDtypeStruct((s, d), BF16),
                   jax.ShapeDtypeStruct((s, RET_W), F32), jax.ShapeDtypeStruct((s, LRU_W), F32),
                   jax.ShapeDtypeStruct((s, GDN_W), F32), jax.ShapeDtypeStruct((1, d), F32), jax.ShapeDtypeStruct((1, d), F32)],
        scratch_shapes=[pltpu.VMEM(w_out.shape, w_out.dtype), pltpu.SemaphoreType.DMA((1,))],
        compiler_params=_params("arbitrary"),
    )(dxn, r2, ln_g, w_out)


def wgrad(a, b, name, out_dtype=BF16):
    s, m = a.shape
    n = b.shape[1]
    tk = 1024 if s % 1024 == 0 else s
    tm = next((c for c in (1408, 1024, 768, 512, 384, 256) if m % c == 0), m)
    tn = next((c for c in (1408, 1152, 1024, 768, 512) if n % c == 0), n)
    nk = s // tk

    def body(a_ref, b_ref, o_ref, acc_ref):
        k = pl.program_id(2)

        @pl.when(k == 0)
        def _():
            acc_ref[...] = jnp.zeros_like(acc_ref)

        acc_ref[...] += _mm_tn(a_ref[...].astype(BF16), b_ref[...].astype(BF16))

        @pl.when(k == nk - 1)
        def _():
            o_ref[...] = acc_ref[...].astype(o_ref.dtype)

    return pl.pallas_call(
        body, name=name, grid=(m // tm, n // tn, nk),
        in_specs=[pl.BlockSpec((tk, tm), lambda i, j, k: (k, i)), pl.BlockSpec((tk, tn), lambda i, j, k: (k, j))],
        out_specs=pl.BlockSpec((tm, tn), lambda i, j, k: (i, j)),
        out_shape=jax.ShapeDtypeStruct((m, n), out_dtype),
        scratch_shapes=[pltpu.VMEM((tm, tn), F32)],
        compiler_params=_params("arbitrary", "arbitrary", "arbitrary"),
    )(a, b)


def loss_and_grad(y, target):
    s, d = y.shape
    tm = ROW_TILE

    def body(y_ref, t_ref, dy_ref, l_ref):
        @pl.when(pl.program_id(0) == 0)
        def _():
            l_ref[...] = jnp.zeros_like(l_ref)

        err = y_ref[...] - t_ref[...]
        dy_ref[...] = err / d
        l_ref[...] += 0.5 * jnp.sum(jnp.mean(err * err, axis=-1, keepdims=True), axis=0, keepdims=True)

    return pl.pallas_call(
        body, name="loss_and_grad", grid=(s // tm,),
        in_specs=[_row_spec(tm, d), _row_spec(tm, d)],
        out_specs=[_row_spec(tm, d), _full_spec((1, 1))],
        out_shape=[jax.ShapeDtypeStruct((s, d), F32), jax.ShapeDtypeStruct((1, 1), F32)],
        compiler_params=_params("arbitrary"),
    )(y, target)


def _ret_consts():
    lg = np.log1p(-np.exp2(-5.0 - np.arange(RET_HEADS, dtype=np.float64)))
    idx = np.arange(CHUNK, dtype=np.float64)
    intra = np.exp(np.abs(idx[:, None] - idx[None, :])[None] * lg[:, None, None])
    cross = np.repeat(np.exp((idx + 1.0)[:, None] * lg[None, :]), HEAD, axis=1)
    tail = np.repeat(np.exp((CHUNK - 1.0 - idx)[:, None] * lg[None, :]), HEAD, axis=1)
    dec = np.repeat(np.exp(CHUNK * lg)[None, :], HEAD, axis=1)
    half = HEAD // 2
    inv_freq = (ROPE_THETA ** (-jnp.arange(half, dtype=F32) / half))
    invf = jnp.tile(inv_freq, 2 * LANES // HEAD)[None, :]
    sgn = np.tile(np.concatenate([-np.ones(half), np.ones(half)]), LANES // HEAD)[None, :]
    f = lambda a: jnp.asarray(a, F32)
    return dict(intra=f(intra), cross=f(cross), tail=f(tail), dec=f(dec), invf=invf, sgn=f(sgn))


def _swap_halves(t):
    lane = lax.broadcasted_iota(jnp.int32, t.shape, 1)
    return jnp.where((lane & 32) == 0, pltpu.roll(t, LANES - 32, 1), pltpu.roll(t, 32, 1))


def _rope(t, c, s):
    return t * c + _swap_halves(t) * s


def _rope_transposed(g, c, s):
    return g * c + _swap_halves(g * s)


def _head_mask(hd):
    lane = lax.broadcasted_iota(jnp.int32, (1, LANES), 1)
    return ((lane >= HEAD * hd) & (lane < HEAD * (hd + 1))).astype(F32)


def _block_diag_mask():
    r = lax.broadcasted_iota(jnp.int32, (LANES, LANES), 0)
    c = lax.broadcasted_iota(jnp.int32, (LANES, LANES), 1)
    return ((r >= HEAD) == (c >= HEAD)).astype(F32)


RET_STEP_CHUNKS = 4


def _ret_specs(n_of, gch):
    cst = lambda shape: pl.BlockSpec(shape, lambda i: (0,) * len(shape))
    return [pl.BlockSpec((CHUNK * gch, RET_IN), lambda i: (n_of(i), 0)), pl.BlockSpec((CHUNK * gch, 1), lambda i: (n_of(i), 0)),
            cst((1, LANES)), cst((1, LANES)), cst((RET_HEADS, CHUNK, CHUNK)), cst((CHUNK, RET_W)), cst((CHUNK, RET_W)),
            cst((1, RET_W)), cst((1, RET_W))]


def ret_fwd(hr, pos, norm_g):
    s = hr.shape[0]
    n_chunks = s // CHUNK
    cs = _ret_consts()
    n_slab = RET_W // LANES

    gch = min(RET_STEP_CHUNKS, n_chunks)

    def body(hr_ref, pos_ref, invf_ref, sgn_ref, intra_ref, cross_ref, tail_ref, dec_ref, g_ref, o_ref, opre_ref, st_ref, state):
        @pl.when(pl.program_id(0) == 0)
        def _():
            state[...] = jnp.zeros_like(state)

        bd = _block_diag_mask()
        sts = [state[LANES * sl:LANES * (sl + 1), :] for sl in range(n_slab)]
        for c in range(gch):
            tok = slice(CHUNK * c, CHUNK * (c + 1))
            ang = pos_ref[tok, :].astype(F32) * invf_ref[...]
            cosv = jnp.cos(ang)
            sinv = jnp.sin(ang) * sgn_ref[...]
            for sl in range(n_slab):
                lanes = slice(LANES * sl, LANES * (sl + 1))
                q = hr_ref[tok, LANES * sl:LANES * (sl + 1)]
                k = hr_ref[tok, RET_W + LANES * sl:RET_W + LANES * (sl + 1)]
                v = hr_ref[tok, 2 * RET_W + LANES * sl:2 * RET_W + LANES * (sl + 1)]
                gate = hr_ref[tok, 3 * RET_W + LANES * sl:3 * RET_W + LANES * (sl + 1)]
                qt = _rope(q, cosv, sinv) * (HEAD ** -0.5)
                kt = _rope(k, cosv, sinv)
                st = sts[sl]
                st_ref[RET_W * c + LANES * sl:RET_W * c + LANES * (sl + 1), :] = st
                o = _mm(qt * cross_ref[:, lanes], st)
                for hd in range(2):
                    m = _head_mask(hd)
                    sc = _mm_nt(qt * m, kt) * intra_ref[2 * sl + hd]
                    o = o + _mm(sc, v) * m
                sts[sl] = st * dec_ref[:, lanes] + _mm_tn(kt, v * tail_ref[:, lanes]) * bd
                opre_ref[tok, lanes] = o
                mu = _group_sum_slab(o) * (1.0 / HEAD)
                dlt = o - mu
                var = _group_sum_slab(dlt * dlt) * (1.0 / HEAD)
                on = dlt * lax.rsqrt(var + 1e-5)
                o_ref[tok, lanes] = on * g_ref[:, lanes] * (gate * _sigmoid(gate))
        for sl in range(n_slab):
            state[LANES * sl:LANES * (sl + 1), :] = sts[sl]

    out_row = lambda w: pl.BlockSpec((CHUNK * gch, w), lambda i: (i, 0))
    return pl.pallas_call(
        body, name="ret_fwd", grid=(n_chunks // gch,),
        in_specs=_ret_specs(lambda i: i, gch),
        out_specs=[out_row(RET_W), out_row(RET_W), pl.BlockSpec((RET_W * gch, LANES), lambda i: (i, 0))],
        out_shape=[jax.ShapeDtypeStruct((s, RET_W), F32), jax.ShapeDtypeStruct((s, RET_W), F32),
                   jax.ShapeDtypeStruct((n_chunks * RET_W, LANES), F32)],
        scratch_shapes=[pltpu.VMEM((RET_W, LANES), F32)],
        compiler_params=_params("arbitrary"),
    )(hr, pos, cs["invf"], cs["sgn"], cs["intra"], cs["cross"], cs["tail"], cs["dec"], norm_g)


def ret_bwd(hr, pos, norm_g, opre, states, dout):
    s = hr.shape[0]
    n_chunks = s // CHUNK
    cs = _ret_consts()
    n_slab = RET_W // LANES
    gch = min(RET_STEP_CHUNKS, n_chunks)
    rev = lambda i: n_chunks // gch - 1 - i

    def body(hr_ref, pos_ref, invf_ref, sgn_ref, intra_ref, cross_ref, tail_ref, dec_ref, g_ref, opre_ref, st_ref, do_ref,
             dh_ref, dg_ref, gstate):
        @pl.when(pl.program_id(0) == 0)
        def _():
            gstate[...] = jnp.zeros_like(gstate)
            dg_ref[...] = jnp.zeros_like(dg_ref)

        bd = _block_diag_mask()
        gss = [gstate[LANES * sl:LANES * (sl + 1), :] for sl in range(n_slab)]
        dgs = [jnp.zeros((1, LANES), F32) for _ in range(n_slab)]
        for c in reversed(range(gch)):
            tok = slice(CHUNK * c, CHUNK * (c + 1))
            ang = pos_ref[tok, :].astype(F32) * invf_ref[...]
            cosv = jnp.cos(ang)
            sinv = jnp.sin(ang) * sgn_ref[...]
            for sl in range(n_slab):
                lanes = slice(LANES * sl, LANES * (sl + 1))
                q = hr_ref[tok, LANES * sl:LANES * (sl + 1)]
                k = hr_ref[tok, RET_W + LANES * sl:RET_W + LANES * (sl + 1)]
                v = hr_ref[tok, 2 * RET_W + LANES * sl:2 * RET_W + LANES * (sl + 1)]
                gate = hr_ref[tok, 3 * RET_W + LANES * sl:3 * RET_W + LANES * (sl + 1)]
                qt = _rope(q, cosv, sinv) * (HEAD ** -0.5)
                kt = _rope(k, cosv, sinv)
                o = opre_ref[tok, lanes]
                mu = _group_sum_slab(o) * (1.0 / HEAD)
                dlt = o - mu
                var = _group_sum_slab(dlt * dlt) * (1.0 / HEAD)
                rstd = lax.rsqrt(var + 1e-5)
                on = dlt * rstd
                sil, dsil = _silu_and_grad(gate)
                dout_v = do_ref[tok, lanes]
                gn = g_ref[:, lanes]
                dgs[sl] = dgs[sl] + jnp.sum(dout_v * on * sil, axis=0, keepdims=True)
                d_on = dout_v * gn * sil
                dgate = dout_v * on * gn * dsil
                d_o = rstd * (d_on - _group_sum_slab(d_on) * (1.0 / HEAD) - on * (_group_sum_slab(d_on * on) * (1.0 / HEAD)))
                st = st_ref[RET_W * c + LANES * sl:RET_W * c + LANES * (sl + 1), :]
                gs = gss[sl]
                cross = cross_ref[:, lanes]
                tail = tail_ref[:, lanes]
                dqt = _mm_nt(d_o, st) * cross
                ds_here = _mm_tn(qt * cross, d_o) * bd
                vt = v * tail
                dkt = _mm_nt(vt, gs)
                dv = _mm(kt, gs) * tail
                for hd in range(2):
                    m = _head_mask(hd)
                    qm = qt * m
                    dom = d_o * m
                    intra = intra_ref[2 * sl + hd]
                    sc = _mm_nt(qm, kt) * intra
                    dsc = _mm_nt(dom, v) * intra
                    dqt = dqt + _mm(dsc, kt) * m
                    dkt = dkt + _mm_tn(dsc, qm)
                    dv = dv + _mm_tn(sc, dom)
                gss[sl] = gs * dec_ref[:, lanes] + ds_here
                dh_ref[tok, LANES * sl:LANES * (sl + 1)] = _rope_transposed(dqt * (HEAD ** -0.5), cosv, sinv).astype(BF16)
                dh_ref[tok, RET_W + LANES * sl:RET_W + LANES * (sl + 1)] = _rope_transposed(dkt, cosv, sinv).astype(BF16)
                dh_ref[tok, 2 * RET_W + LANES * sl:2 * RET_W + LANES * (sl + 1)] = dv.astype(BF16)
                dh_ref[tok, 3 * RET_W + LANES * sl:3 * RET_W + LANES * (sl + 1)] = dgate.astype(BF16)
        for sl in range(n_slab):
            gstate[LANES * sl:LANES * (sl + 1), :] = gss[sl]
            dg_ref[:, LANES * sl:LANES * (sl + 1)] += dgs[sl]

    row = lambda w: pl.BlockSpec((CHUNK * gch, w), lambda i: (rev(i), 0))
    return pl.pallas_call(
        body, name="ret_bwd", grid=(n_chunks // gch,),
        in_specs=_ret_specs(rev, gch) + [row(RET_W), pl.BlockSpec((RET_W * gch, LANES), lambda i: (rev(i), 0)), row(RET_W)],
        out_specs=[row(RET_IN), pl.BlockSpec((1, RET_W), lambda i: (0, 0))],
        out_shape=[jax.ShapeDtypeStruct((s, RET_IN), BF16), jax.ShapeDtypeStruct((1, RET_W), F32)],
        scratch_shapes=[pltpu.VMEM((RET_W, LANES), F32)],
        compiler_params=_params("arbitrary"),
    )(hr, pos, cs["invf"], cs["sgn"], cs["intra"], cs["cross"], cs["tail"], cs["dec"], norm_g, opre, states, dout)


def _lru_gates(xc, wa_ref, ba_ref, wx_ref, bx_ref, lam_ref):
    xcb = xc.astype(BF16)
    r = _sigmoid(_mm(xcb, wa_ref[...].astype(BF16)) + ba_ref[...])
    ig = _sigmoid(_mm(xcb, wx_ref[...].astype(BF16)) + bx_ref[...])
    lam = lam_ref[...]
    ls = jnp.minimum(lam, 0.0) - _log1p(jnp.exp(-jnp.abs(lam)))
    la = (LRU_C * r) * ls
    a = jnp.exp(la)
    mult = jnp.sqrt(-_expm1(2.0 * la))
    return r, ig, ls, a, mult


def _lru_conv(x, xprev, w_ref, b_ref):
    xc = b_ref[...] + w_ref[3:4, :] * x
    for j in (1, 2, 3):
        xc = xc + w_ref[3 - j:4 - j, :] * _shift_rows(x, xprev, j)
    return xc


def lru_fwd(hl, conv_w, conv_b, w_a, b_a, w_x, b_x, lam):
    s = hl.shape[0]
    ts = SCAN_TILE
    w = LRU_W

    def body(hl_ref, hp_ref, cw_ref, cb_ref, wa_ref, ba_ref, wx_ref, bx_ref, lam_ref, o_ref, xc_ref, h_ref, carry):
        i = pl.program_id(0)

        @pl.when(i == 0)
        def _():
            carry[...] = jnp.zeros_like(carry)

        x = hl_ref[:, 0:w]
        gate = hl_ref[:, w:2 * w]
        xprev = hp_ref[...] * (i > 0).astype(F32)
        xc = _lru_conv(x, xprev, cw_ref, cb_ref)
        xc_ref[...] = xc
        _, ig, _, a, mult = _lru_gates(xc, wa_ref, ba_ref, wx_ref, bx_ref, lam_ref)
        b = mult * (ig * xc)
        row = lax.broadcasted_iota(jnp.int32, (ts, w), 0)
        d = 1
        while d < ts:
            ap = jnp.where(row >= d, pltpu.roll(a, d, 0), 1.0)
            bp = jnp.where(row >= d, pltpu.roll(b, d, 0), 0.0)
            b = a * bp + b
            a = a * ap
            d *= 2
        h = b + a * carry[0:1, :]
        h_ref[...] = h
        carry[0:1, :] = h[ts - 1:ts, :]
        o_ref[...] = h * _gelu(gate)

    cst = lambda shape: pl.BlockSpec(shape, lambda i: (0, 0))
    return pl.pallas_call(
        body, name="lru_fwd", grid=(s // ts,),
        in_specs=[_row_spec(ts, 2 * w), pl.BlockSpec((ts, w), lambda i: (jnp.maximum(i - 1, 0), 0)),
                  cst((4, w)), cst((1, w)), cst((w, w)), cst((1, w)), cst((w, w)), cst((1, w)), cst((1, w))],
        out_specs=[_row_spec(ts, w)] * 3,
        out_shape=[jax.ShapeDtypeStruct((s, w), F32)] * 3,
        scratch_shapes=[pltpu.VMEM((8, w), F32)],
        compiler_params=_params("arbitrary"),
    )(hl, hl, conv_w, conv_b, w_a, b_a, w_x, b_x, lam)


def lru_bwd(hl, conv_w, conv_b, w_a, b_a, w_x, b_x, lam, xc_saved, h_saved, dout):
    s = hl.shape[0]
    ts = SCAN_TILE
    w = LRU_W
    nb = s // ts
    rev = lambda i: nb - 1 - i

    def body(hl_ref, hp_ref, cw_ref, cb_ref, wa_ref, ba_ref, wx_ref, bx_ref, lam_ref, xc_ref, h_ref, hprev_ref, do_ref,
             dhl_ref, dcw_ref, dcb_ref, dwa_ref, dba_ref, dwx_ref, dbx_ref, dlam_ref, carry, dxc_next):
        i = pl.program_id(0)
        blk = nb - 1 - i

        @pl.when(i == 0)
        def _():
            carry[...] = jnp.zeros_like(carry)
            dxc_next[...] = jnp.zeros_like(dxc_next)
            for ref in (dcw_ref, dcb_ref, dwa_ref, dba_ref, dwx_ref, dbx_ref, dlam_ref):
                ref[...] = jnp.zeros_like(ref)

        first = (blk > 0).astype(F32)
        x = hl_ref[:, 0:w]
        gate = hl_ref[:, w:2 * w]
        xprev = hp_ref[...] * first
        xc = xc_ref[...]
        h = h_ref[...]
        hprev = hprev_ref[...] * first
        r, ig, ls, a, mult = _lru_gates(xc, wa_ref, ba_ref, wx_ref, bx_ref, lam_ref)
        do = do_ref[...]
        dh = do * _gelu(gate)
        dgate = do * h * _gelu_grad(gate)
        row = lax.broadcasted_iota(jnp.int32, (ts, w), 0)
        ca = jnp.where(row < ts - 1, pltpu.roll(a, ts - 1, 0), 1.0)
        cb = dh
        d = 1
        while d < ts:
            an = jnp.where(row < ts - d, pltpu.roll(ca, ts - d, 0), 1.0)
            bn = jnp.where(row < ts - d, pltpu.roll(cb, ts - d, 0), 0.0)
            cb = cb + ca * bn
            ca = ca * an
            d *= 2
        lamb = cb + ca * carry[0:1, :]
        carry[0:1, :] = a[0:1, :] * lamb[0:1, :]
        h_before = _shift_rows(h, hprev, 1)
        da = lamb * h_before
        ix = ig * xc
        dmult = lamb * ix
        dig = lamb * mult * xc
        dxc = lamb * mult * ig
        dla = (da - dmult * a / mult) * a
        dr = dla * LRU_C * ls
        dlam_ref[...] += jnp.sum(dla * LRU_C * r, axis=0, keepdims=True) * _sigmoid(-lam_ref[...])
        dpa = dr * r * (1.0 - r)
        dpx = dig * ig * (1.0 - ig)
        dba_ref[...] += jnp.sum(dpa, axis=0, keepdims=True)
        dbx_ref[...] += jnp.sum(dpx, axis=0, keepdims=True)
        dpab = dpa.astype(BF16)
        dpxb = dpx.astype(BF16)
        xcb = xc.astype(BF16)
        dxc = dxc + _mm_nt(dpab, wa_ref[...].astype(BF16)) + _mm_nt(dpxb, wx_ref[...].astype(BF16))
        dwa_ref[...] += _mm_tn(xcb, dpab)
        dwx_ref[...] += _mm_tn(xcb, dpxb)
        dcb_ref[...] += jnp.sum(dxc, axis=0, keepdims=True)
        nxt = dxc_next[...]
        dx = cw_ref[3:4, :] * dxc
        dcw_ref[3:4, :] += jnp.sum(dxc * x, axis=0, keepdims=True)
        for j in (1, 2, 3):
            dx = dx + cw_ref[3 - j:4 - j, :] * _shift_rows_up(dxc, nxt, j)
            dcw_ref[3 - j:4 - j, :] += jnp.sum(dxc * _shift_rows(x, xprev, j), axis=0, keepdims=True)
        dxc_next[...] = dxc
        dhl_ref[:, 0:w] = dx.astype(BF16)
        dhl_ref[:, w:2 * w] = dgate.astype(BF16)

    cst = lambda shape: pl.BlockSpec(shape, lambda i: (0, 0))
    rowr = lambda width: pl.BlockSpec((ts, width), lambda i: (rev(i), 0))
    prevr = lambda width: pl.BlockSpec((ts, width), lambda i: (jnp.maximum(rev(i) - 1, 0), 0))
    return pl.pallas_call(
        body, name="lru_bwd", grid=(nb,),
        in_specs=[rowr(2 * w), prevr(w), cst((4, w)), cst((1, w)), cst((w, w)), cst((1, w)), cst((w, w)), cst((1, w)), cst((1, w)),
                  rowr(w), rowr(w), prevr(w), rowr(w)],
        out_specs=[rowr(2 * w), cst((4, w)), cst((1, w)), cst((w, w)), cst((1, w)), cst((w, w)), cst((1, w)), cst((1, w))],
        out_shape=[jax.ShapeDtypeStruct((s, 2 * w), BF16), jax.ShapeDtypeStruct((4, w), F32), jax.ShapeDtypeStruct((1, w), F32),
                   jax.ShapeDtypeStruct((w, w), F32), jax.ShapeDtypeStruct((1, w), F32), jax.ShapeDtypeStruct((w, w), F32),
                   jax.ShapeDtypeStruct((1, w), F32), jax.ShapeDtypeStruct((1, w), F32)],
        scratch_shapes=[pltpu.VMEM((8, w), F32), pltpu.VMEM((ts, w), F32)],
        compiler_params=_params("arbitrary"),
    )(hl, hl, conv_w, conv_b, w_a, b_a, w_x, b_x, lam, xc_saved, h_saved, h_saved, dout)


GDN_QKV = 3 * GDN_W
GDN_STEP_CHUNKS = 4
GDN_BWD_STEP_CHUNKS = 2


def _tri_inverse_many(nms):
    r = lax.broadcasted_iota(jnp.int32, nms[0].shape, 0)
    c = lax.broadcasted_iota(jnp.int32, nms[0].shape, 1)
    eye = (r == c).astype(F32)
    ts = [eye - nm for nm in nms]
    ps = list(nms)
    for _ in range(5):
        ps = [_mm3(p, p) for p in ps]
        ts = [t + _mm3(t, p) for t, p in zip(ts, ps)]
    return ts


def _gdn_front(hx_ref, hprev, cw_ref, al_ref, dt_ref):
    w = GDN_W
    x = hx_ref[:, 0:GDN_QKV]
    y = cw_ref[3:4, :] * x
    for j in (1, 2, 3):
        y = y + cw_ref[3 - j:4 - j, :] * _shift_rows(x, hprev, j)
    qkv, dsil = _silu_and_grad(y)
    q, k, v = qkv[:, 0:w], qkv[:, w:2 * w], qkv[:, 2 * w:3 * w]
    rq = lax.rsqrt(_group_sum(q * q) + 1e-6)
    rk = lax.rsqrt(_group_sum(k * k) + 1e-6)
    beta = _sigmoid(hx_ref[:, 5 * w:6 * w])
    sp_in = hx_ref[:, 4 * w:5 * w] + dt_ref[...]
    neg_a = -jnp.exp(al_ref[...])
    g = neg_a * _softplus(sp_in)
    n_c = g.shape[0] // CHUNK
    gc = jnp.concatenate([_rows_prefix_sum(g[CHUNK * c:CHUNK * (c + 1)]) for c in range(n_c)], axis=0)
    return dict(x=x, dsil=dsil, qn=q * rq, kn=k * rk, v=v, rq=rq, rk=rk, beta=beta, sp_in=sp_in, neg_a=neg_a, g=g, gc=gc)


def _stack_heads(x):
    return jnp.concatenate([x * _head_mask(0), x * _head_mask(1)], axis=0)


def _unstack_heads(y):
    return y[0:CHUNK] + y[CHUNK:2 * CHUNK]


def _head_transpose(x):
    return jnp.concatenate([x[:, 0:HEAD].T, x[:, HEAD:2 * HEAD].T], axis=1)


def _head_total(x):
    cols = jnp.broadcast_to(jnp.sum(x, axis=0, keepdims=True), (8, LANES))
    return _group_sum_slab(cols)[0:1]


def _slab_tri_masks():
    r = lax.broadcasted_iota(jnp.int32, (CHUNK, LANES), 0)
    c = lax.broadcasted_iota(jnp.int32, (CHUNK, LANES), 1) & (HEAD - 1)
    return r >= c, r > c


def _gdn_slab(fr, c, sl, tri, transposed=False):
    lower, strict = tri
    ls = lambda a: a[CHUNK * c:CHUNK * (c + 1), LANES * sl:LANES * (sl + 1)]
    k = ls(fr["kn"])
    q = ls(fr["qn"]) * (HEAD ** -0.5)
    v = ls(fr["v"])
    beta = ls(fr["beta"])
    gc = ls(fr["gc"])
    e = jnp.exp(gc)
    gl = gc[CHUNK - 1:CHUNK, :]
    xt = jnp.exp(gl - gc)
    gc_t = _head_transpose(gc)
    dec = jnp.where(lower, jnp.exp(jnp.minimum(gc - gc_t, 0.0)), 0.0)
    kbd = _stack_heads(k)
    kk = _mm_nt(k, kbd)
    qkr = _mm_nt(q, kbd)
    out = dict(k=k, q=q, v=v, beta=beta, e=e, egl=jnp.exp(gl), xt=xt, dec=dec, kk=kk, qkr=qkr, kbd=kbd,
               nm=jnp.where(strict, beta * kk * dec, 0.0))
    if transposed:
        r = lax.broadcasted_iota(jnp.int32, (CHUNK, LANES), 0)
        col = lax.broadcasted_iota(jnp.int32, (CHUNK, LANES), 1) & (HEAD - 1)
        qbd = _stack_heads(q)
        out.update(dec_t=jnp.where(r <= col, jnp.exp(jnp.minimum(gc_t - gc, 0.0)), 0.0), beta_t=_head_transpose(beta),
                   qbd=qbd, kqr=_mm_nt(k, qbd), strict_t=r < col)
    return out


def gdn_fwd(hx, conv_w, a_log_e, dt_bias_e, norm_g_e, fused=None):
    s = hx.shape[0]
    n_chunks = s // CHUNK
    w = GDN_W
    n_slab = w // LANES
    gch = min(GDN_STEP_CHUNKS, n_chunks)

    def body(hx_ref, hp_ref, cw_ref, al_ref, dt_ref, ng_ref, o_ref, opre_ref, t_ref, st_ref, state):
        n = pl.program_id(0)

        @pl.when(n == 0)
        def _():
            state[...] = jnp.zeros_like(state)

        fr = _gdn_front(hx_ref, hp_ref[...] * (n > 0).astype(F32), cw_ref, al_ref, dt_ref)
        tri = _slab_tri_masks()
        bd = _block_diag_mask()
        sts = [state[LANES * sl:LANES * (sl + 1), :] for sl in range(n_slab)]
        slabs = [[_gdn_slab(fr, c, sl, tri) for sl in range(n_slab)] for c in range(gch)]
        tbd = _tri_inverse_many([_stack_heads(sq["nm"]) for row_ in slabs for sq in row_])
        o_rows = []
        for c in range(gch):
            ts, outs = [], []
            st_ref[w * c:w * (c + 1), :] = jnp.concatenate(sts, axis=0)
            for sl in range(n_slab):
                sq = slabs[c][sl]
                t = _unstack_heads(tbd[n_slab * c + sl])
                ts.append(t)
                u = _mm(t, _stack_heads(sq["v"] * sq["beta"]))
                wk = _mm(t, _stack_heads(sq["k"] * (sq["beta"] * sq["e"])))
                st = sts[sl]
                vnew = u - _mm(wk, st)
                outs.append(_mm(sq["q"] * sq["e"], st) + _mm(sq["qkr"] * sq["dec"], _stack_heads(vnew)))
                sts[sl] = st * sq["egl"] + _mm_tn(sq["k"] * sq["xt"], vnew) * bd
            t_ref[CHUNK * c:CHUNK * (c + 1), :] = jnp.concatenate(ts, axis=1)
            o_rows.append(jnp.concatenate(outs, axis=1))
        state[...] = jnp.concatenate(sts, axis=0)
        o = jnp.concatenate(o_rows, axis=0)
        opre_ref[...] = o
        rinv = lax.rsqrt(_group_sum(o * o) * (1.0 / HEAD) + 1e-6)
        z = hx_ref[:, 3 * w:4 * w]
        o_ref[...] = (o * rinv) * ng_ref[...] * (z * _sigmoid(z))

    cst = lambda shape: pl.BlockSpec(shape, lambda i: (0, 0))
    row = lambda width: pl.BlockSpec((CHUNK * gch, width), lambda i: (i, 0))
    f_in, f_out, f_scr, _ = fused if fused is not None else ([], [], [], None)
    outs = pl.pallas_call(
        _fuse(body, 6, 4, 1, n_chunks // gch, fused), name="gdn_fwd" + ("_gather" if fused is not None else ""),
        grid=(n_chunks // gch,),
        in_specs=[row(GDN_IN), pl.BlockSpec((CHUNK * gch, GDN_QKV), lambda i: (jnp.maximum(i - 1, 0), 0)),
                  cst((4, GDN_QKV)), cst((1, w)), cst((1, w)), cst((1, w))] + [_ANY] * len(f_in),
        out_specs=[row(w)] * 3 + [pl.BlockSpec((w * gch, LANES), lambda i: (i, 0))] + [_ANY] * len(f_out),
        out_shape=[jax.ShapeDtypeStruct((s, w), F32)] * 3 + [jax.ShapeDtypeStruct((n_chunks * w, LANES), F32)] + list(f_out),
        scratch_shapes=[pltpu.VMEM((w, LANES), F32)] + list(f_scr),
        compiler_params=_params("arbitrary"),
    )(hx, hx, conv_w, a_log_e, dt_bias_e, norm_g_e, *f_in)
    return tuple(outs[:4]) + ((list(outs[4:]),) if fused is not None else ())


def gdn_bwd(hx, conv_w, a_log_e, dt_bias_e, norm_g_e, opre, tmat, states, dout, fused=None):
    s = hx.shape[0]
    n_chunks = s // CHUNK
    w = GDN_W
    n_slab = w // LANES
    gch = min(GDN_BWD_STEP_CHUNKS, n_chunks)
    n_blocks = n_chunks // gch
    rev = lambda i: n_blocks - 1 - i

    def body(hx_ref, hp_ref, cw_ref, al_ref, dt_ref, ng_ref, opre_ref, t_ref, st_ref, do_ref,
             dhx_ref, dab_ref, dcw_ref, dal_ref, ddt_ref, dng_ref, dstate, dy_next):
        i = pl.program_id(0)
        n = n_blocks - 1 - i

        @pl.when(i == 0)
        def _():
            dstate[...] = jnp.zeros_like(dstate)
            dy_next[...] = jnp.zeros_like(dy_next)
            for ref in (dcw_ref, dal_ref, ddt_ref, dng_ref):
                ref[...] = jnp.zeros_like(ref)

        hprev = hp_ref[...] * (n > 0).astype(F32)
        fr = _gdn_front(hx_ref, hprev, cw_ref, al_ref, dt_ref)
        tri = _slab_tri_masks()
        lower, strict = tri
        o = opre_ref[...]
        rinv = lax.rsqrt(_group_sum(o * o) * (1.0 / HEAD) + 1e-6)
        yn = o * rinv
        z = hx_ref[:, 3 * w:4 * w]
        sil, dsil_z = _silu_and_grad(z)
        dout_v = do_ref[...]
        ng = ng_ref[...]
        dng_ref[...] += jnp.sum(dout_v * yn * sil, axis=0, keepdims=True)
        dz = dout_v * yn * ng * dsil_z
        dyn = dout_v * ng * sil
        d_o = rinv * (dyn - yn * (_group_sum(dyn * yn) * (1.0 / HEAD)))
        last_row = (lax.broadcasted_iota(jnp.int32, (CHUNK, LANES), 0) == CHUNK - 1).astype(F32)
        bd = _block_diag_mask()
        gsum = _group_sum_slab
        t_all, st_all = t_ref[...], st_ref[...]
        dsns = [dstate[LANES * sl:LANES * (sl + 1), :] for sl in range(n_slab)]
        per_chunk = {}
        order = [(c_, s_) for c_ in reversed(range(gch)) for s_ in range(n_slab)]
        chain = {}
        for c, sl in order:
            lanes = slice(LANES * sl, LANES * (sl + 1))
            tok = slice(CHUNK * c, CHUNK * (c + 1))
            sq = _gdn_slab(fr, c, sl, tri, transposed=True)
            t = t_all[tok, lanes]
            st = st_all[w * c + LANES * sl:w * c + LANES * (sl + 1), :]
            dsn = dsns[sl]
            do_s = d_o[tok, lanes]
            u = _mm(t, _stack_heads(sq["v"] * sq["beta"]))
            wk = _mm(t, _stack_heads(sq["k"] * (sq["beta"] * sq["e"])))
            kt = sq["k"] * sq["xt"]
            dobd = _stack_heads(do_s)
            dvnew = _mm(sq["kqr"] * sq["dec_t"], dobd) + _mm(kt, dsn)
            dsns[sl] = _mm_tn(sq["q"] * sq["e"], do_s) * bd + sq["egl"] * dsn - _mm_tn(wk, dvnew) * bd
            chain[(c, sl)] = (sq, t, st, dsn, do_s, dobd, u, wk, kt, dvnew)
        for c, sl in order:
            sq, t, st, dsn, do_s, dobd, u, wk, kt, dvnew = chain[(c, sl)]
            k, q, v, beta, e, xt, dec, kk, qkr, kbd = (sq[n_] for n_ in ("k", "q", "v", "beta", "e", "xt", "dec", "kk", "qkr", "kbd"))
            dec_t, beta_t, kqr, qbd = sq["dec_t"], sq["beta_t"], sq["kqr"], sq["qbd"]
            t_t = _head_transpose(t)
            vnew = u - _mm(wk, st)
            dqd = _mm_nt(do_s, st)
            dqk = _mm_nt(do_s, _stack_heads(vnew))
            dqk_t = _mm_nt(vnew, dobd)
            dkt = _mm_nt(vnew, dsn)
            dgl = _head_total(dsn * st) * sq["egl"]
            dwk = -_mm_nt(dvnew, st)
            drv = _mm(t_t, _stack_heads(dvnew))
            drk = _mm(t_t, _stack_heads(dwk))
            dnm = jnp.where(strict, -(_mm_nt(drv, _stack_heads(u)) + _mm_nt(drk, _stack_heads(wk))), 0.0)
            dnm_t = jnp.where(sq["strict_t"], -(_mm_nt(u, _stack_heads(drv)) + _mm_nt(wk, _stack_heads(drk))), 0.0)
            dbeta = gsum(dnm * kk * dec)
            dkk = dnm * beta * dec
            dkk_t = dnm_t * beta_t * dec_t
            ddec = dnm * beta * kk + dqk * qkr
            dd_t = (dnm_t * beta_t * kk + dqk_t * kqr) * dec_t
            dq = _mm(dqk * dec, kbd) + dqd * e
            dk = _mm(dqk_t * dec_t, qbd) + _mm(dkk + dkk_t, kbd) + drk * (beta * e) + dkt * xt
            rks = gsum(drk * k)
            dbeta = dbeta + gsum(drv * v) + rks * e
            de = rks * beta + gsum(dqd * q)
            dxt = gsum(dkt * k) * xt
            dgl = dgl + jnp.sum(dxt, axis=0, keepdims=True)
            dgc = de * e - dxt + gsum(ddec * dec) - gsum(dd_t) + last_row * dgl
            per_chunk[(c, sl)] = dict(dq=dq * (HEAD ** -0.5), dk=dk, dv=drv * beta, dbeta=dbeta, dgc=dgc)
        for sl in range(n_slab):
            dstate[LANES * sl:LANES * (sl + 1), :] = dsns[sl]

        def block_of(name, suffix_sum=False):
            rows = []
            for c in range(gch):
                r = jnp.concatenate([per_chunk[(c, sl)][name] for sl in range(n_slab)], axis=1)
                rows.append(_rows_suffix_sum(r) if suffix_sum else r)
            return jnp.concatenate(rows, axis=0)

        dg = block_of("dgc", suffix_sum=True)
        dal_ref[...] += jnp.sum(dg * fr["g"], axis=0, keepdims=True)
        da = dg * fr["neg_a"] * _sigmoid(fr["sp_in"])
        ddt_ref[...] += jnp.sum(da, axis=0, keepdims=True)
        beta_all = fr["beta"]
        db = block_of("dbeta") * beta_all * (1.0 - beta_all)
        lane = lax.broadcasted_iota(jnp.int32, (CHUNK * gch, LANES), 1)
        dab = jnp.zeros((CHUNK * gch, LANES), F32)
        for hd in range(GDN_HEADS):
            dab = jnp.where(lane == hd, da[:, HEAD * hd:HEAD * hd + 1], dab)
            dab = jnp.where(lane == GDN_HEADS + hd, db[:, HEAD * hd:HEAD * hd + 1], dab)
        dab_ref[...] = dab.astype(BF16)
        dqn = block_of("dq")
        dkn = block_of("dk")
        dq_raw = fr["rq"] * (dqn - fr["qn"] * _group_sum(dqn * fr["qn"]))
        dk_raw = fr["rk"] * (dkn - fr["kn"] * _group_sum(dkn * fr["kn"]))
        dy = jnp.concatenate([dq_raw, dk_raw, block_of("dv")], axis=1) * fr["dsil"]
        nxt = dy_next[...]
        x = fr["x"]
        dx = cw_ref[3:4, :] * dy
        dcw_ref[3:4, :] += jnp.sum(dy * x, axis=0, keepdims=True)
        for j in (1, 2, 3):
            dx = dx + cw_ref[3 - j:4 - j, :] * _shift_rows_up(dy, nxt, j)
            dcw_ref[3 - j:4 - j, :] += jnp.sum(dy * _shift_rows(x, hprev, j), axis=0, keepdims=True)
        dy_next[...] = dy
        dhx_ref[:, 0:GDN_QKV] = dx.astype(BF16)
        dhx_ref[:, 3 * w:4 * w] = dz.astype(BF16)

    cst = lambda shape: pl.BlockSpec(shape, lambda i: (0, 0))
    row = lambda width: pl.BlockSpec((CHUNK * gch, width), lambda i: (rev(i), 0))
    buf = lambda width: pltpu.VMEM((CHUNK * gch, width), F32)
    f_in, f_out, f_scr, _ = fused if fused is not None else ([], [], [], None)
    outs = pl.pallas_call(
        _fuse(body, 10, 6, 2, n_blocks, fused), name="gdn_bwd" + ("_exchange" if fused is not None else ""), grid=(n_blocks,),
        in_specs=[row(GDN_IN), pl.BlockSpec((CHUNK * gch, GDN_QKV), lambda i: (jnp.maximum(rev(i) - 1, 0), 0)),
                  cst((4, GDN_QKV)), cst((1, w)), cst((1, w)), cst((1, w)), row(w), row(w),
                  pl.BlockSpec((w * gch, LANES), lambda i: (rev(i), 0)), row(w)] + [_ANY] * len(f_in),
        out_specs=[row(4 * w), row(LANES), cst((4, GDN_QKV)), cst((1, w)), cst((1, w)), cst((1, w))] + [_ANY] * len(f_out),
        out_shape=[jax.ShapeDtypeStruct((s, 4 * w), BF16), jax.ShapeDtypeStruct((s, LANES), BF16),
                   jax.ShapeDtypeStruct((4, GDN_QKV), F32),
                   jax.ShapeDtypeStruct((1, w), F32), jax.ShapeDtypeStruct((1, w), F32), jax.ShapeDtypeStruct((1, w), F32)]
        + list(f_out),
        scratch_shapes=[pltpu.VMEM((w, LANES), F32), buf(GDN_QKV)] + list(f_scr),
        compiler_params=_params("arbitrary"),
    )(hx, hx, conv_w, a_log_e, dt_bias_e, norm_g_e, opre, tmat, states, dout, *f_in)
    return tuple(outs[:6]) + ((list(outs[6:]),) if fused is not None else ())


_MESH = pl.DeviceIdType.MESH


def all_gather8(x, name):
    m, n = x.shape

    def body(x_ref, out_ref, send_sems, recv_sems, local_sem):
        px, py, pc = lax.axis_index("x"), lax.axis_index("y"), lax.axis_index("c")
        me, sibling = (px, py, pc), (px, py, 1 - pc)
        chips = [(1 - px, py), (px, 1 - py), (1 - px, 1 - py)]

        def slot(dx, dy, dc):
            return out_ref.at[4 * dx + 2 * dy + dc]

        def copy(k, block, to, src=None):
            return pltpu.make_async_remote_copy(
                src_ref=slot(*block) if src is None else src, dst_ref=slot(*block),
                send_sem=send_sems.at[k], recv_sem=recv_sems.at[k], device_id=to, device_id_type=_MESH)

        mine = pltpu.make_async_copy(x_ref, slot(*me), local_sem)
        mine.start()
        first = [copy(0, me, sibling, src=x_ref)]
        first += [copy(1 + j, me, (*chip, pc), src=x_ref) for j, chip in enumerate(chips)]
        for cp in first:
            cp.start()
        passed = [copy(4 + j, (*chip, pc), sibling) for j, chip in enumerate(chips)]
        for j, chip in enumerate(chips):
            copy(1 + j, (*chip, pc), me).wait_recv()
            passed[j].start()
        copy(0, sibling, me).wait_recv()
        for j, chip in enumerate(chips):
            copy(4 + j, (*chip, 1 - pc), me).wait_recv()
        for cp in first + passed:
            cp.wait_send()
        mine.wait()

    return pl.pallas_call(
        body, name=name, out_shape=jax.ShapeDtypeStruct((N_DEV, m, n), x.dtype),
        in_specs=[_ANY], out_specs=_ANY,
        scratch_shapes=[pltpu.SemaphoreType.DMA((7,)), pltpu.SemaphoreType.DMA((7,)), pltpu.SemaphoreType.DMA],
    )(x)


def _weight_gather_steps(s0, s1, f0, f1, sems):
    n0 = len(s0)
    own_send, own_recv, ici_send, ici_recv, fwd_send, fwd_recv = sems
    px, py, pc = lax.axis_index("x"), lax.axis_index("y"), lax.axis_index("c")
    mine = 2 * px + py
    sibling = (px, py, 1 - pc)
    chips = [(1 - px, py), (px, 1 - py), (1 - px, 1 - py)]

    def copy(src, dst, sems_s, sems_r, k, to):
        return pltpu.make_async_remote_copy(src_ref=src, dst_ref=dst, send_sem=sems_s.at[k], recv_sem=sems_r.at[k],
                                            device_id=to, device_id_type=_MESH)

    def own_copies():
        return [copy(shards[i], full[i].at[mine], own_send, own_recv, base + i, sibling)
                for base, shards, full in ((0, s0, f0), (n0, s1, f1)) for i in range(len(shards))]

    def first_copies(my_shards, my_full):
        ici = [copy(my_shards[i], my_full[i].at[mine], ici_send, ici_recv, 3 * i + j, (cx, cy, pc))
               for i in range(len(my_shards)) for j, (cx, cy) in enumerate(chips)]
        return own_copies() + ici

    def begin(my_shards, my_full):
        for cp in first_copies(my_shards, my_full):
            cp.start()

    def end(my_shards, my_full, other_full):
        fwd = []
        for i in range(len(my_shards)):
            for j, (cx, cy) in enumerate(chips):
                slot = my_full[i].at[2 * cx + cy]
                copy(my_shards[i], slot, ici_send, ici_recv, 3 * i + j, (cx, cy, pc)).wait_recv()
                cp = copy(slot, slot, fwd_send, fwd_recv, 3 * i + j, sibling)
                cp.start()
                fwd.append(cp)
        for cp in own_copies():
            cp.wait_recv()
        for i in range(len(other_full)):
            for j, (cx, cy) in enumerate(chips):
                slot = other_full[i].at[2 * cx + cy]
                copy(slot, slot, fwd_send, fwd_recv, 3 * i + j, sibling).wait_recv()
        for cp in first_copies(my_shards, my_full) + fwd:
            cp.wait_send()

    def start():
        @pl.when(pc == 0)
        def _():
            begin(s0, f0)

        @pl.when(pc == 1)
        def _():
            begin(s1, f1)

    def finish():
        @pl.when(pc == 0)
        def _():
            end(s0, f0, f1)

        @pl.when(pc == 1)
        def _():
            end(s1, f1, f0)

    return start, finish


def _weight_gather_operands(shards0, shards1):
    both, most = len(shards0) + len(shards1), max(len(shards0), len(shards1))
    full = [jax.ShapeDtypeStruct((N_CHIPS,) + v.shape, v.dtype) for v in list(shards0) + list(shards1)]
    dma = pltpu.SemaphoreType.DMA
    return full, [dma((both,)), dma((both,)), dma((3 * most,)), dma((3 * most,)), dma((3 * most,)), dma((3 * most,))]


def _weight_gather_fused(shards0, shards1):
    n0, both = len(shards0), len(shards0) + len(shards1)
    full, sems = _weight_gather_operands(shards0, shards1)
    steps = lambda ins, outs, scr: _weight_gather_steps(ins[:n0], ins[n0:both], outs[:n0], outs[n0:both], scr)
    return list(shards0) + list(shards1), full, sems, steps


def gather_layer_weights(shards0, shards1, name):
    ins, full, sems, steps = _weight_gather_fused(shards0, shards1)
    both = len(ins)

    def body(*refs):
        start, finish = steps(refs[0:both], refs[both:2 * both], refs[2 * both:])
        start()
        finish()

    return pl.pallas_call(
        body, name=name, out_shape=full, in_specs=[_ANY] * both, out_specs=[_ANY] * both, scratch_shapes=sems,
    )(*ins)


def _piece_offsets(pieces):
    offs = [0]
    for r, _ in pieces:
        offs.append(offs[-1] + r)
    return offs


def _chip_exchange_steps(srcs, q_ref, sems, pieces, owner):
    send_sems, recv_sems = sems
    offs = _piece_offsets(pieces)
    px, py, pc = lax.axis_index("x"), lax.axis_index("y"), lax.axis_index("c")
    mine = 2 * px + py
    chips = [(1 - px, py), (px, 1 - py), (1 - px, 1 - py)]

    def copies():
        sends = []
        for i, (r, stride) in enumerate(pieces):
            dst = pl.ds(offs[i], r)
            for j, (cx, cy) in enumerate(chips):
                sends.append(pltpu.make_async_remote_copy(
                    src_ref=srcs[i].at[pl.ds((2 * cx + cy) * stride, r)], dst_ref=q_ref.at[mine, dst],
                    send_sem=send_sems.at[3 * i + j], recv_sem=recv_sems.at[3 * i + j], device_id=(cx, cy, pc),
                    device_id_type=_MESH))
        return sends

    def start():
        @pl.when(pc == owner)
        def _():
            for cp in copies():
                cp.start()

    def finish():
        @pl.when(pc == owner)
        def _():
            for i, (r, stride) in enumerate(pieces):
                dst = pl.ds(offs[i], r)
                for j, (cx, cy) in enumerate(chips):
                    pltpu.make_async_remote_copy(
                        src_ref=srcs[i].at[pl.ds(mine * stride, r)], dst_ref=q_ref.at[2 * cx + cy, dst],
                        send_sem=send_sems.at[3 * i + j], recv_sem=recv_sems.at[3 * i + j], device_id=(cx, cy, pc),
                        device_id_type=_MESH).wait_recv()
            for cp in copies():
                cp.wait_send()

    return start, finish


def _chip_exchange_operands(arrays, pieces):
    n = len(pieces)
    dma = pltpu.SemaphoreType.DMA
    q = jax.ShapeDtypeStruct((N_CHIPS, _piece_offsets(pieces)[-1], arrays[0].shape[1]), arrays[0].dtype)
    return q, [dma((3 * n,)), dma((3 * n,))]


def _own_share(arrays, pieces, chip):
    return jnp.concatenate([lax.dynamic_slice_in_dim(arr, chip * stride, r, axis=0) for arr, (r, stride) in zip(arrays, pieces)],
                           axis=0)


def chip_exchange(arrays, pieces, owner, name):
    n = len(pieces)

    def body(*refs):
        start, finish = _chip_exchange_steps(refs[0:n], refs[n], refs[n + 1:], pieces, owner)
        start()
        finish()

    q, sems = _chip_exchange_operands(arrays, pieces)
    return pl.pallas_call(body, name=name, out_shape=q, in_specs=[_ANY] * n, out_specs=_ANY, scratch_shapes=sems)(*arrays)


def sibling_send(arrays, to_core, name):
    n = len(arrays)

    def body(*refs):
        srcs, outs = refs[0:n], refs[n:2 * n]
        send_sems, recv_sems = refs[2 * n:]
        px, py, pc = lax.axis_index("x"), lax.axis_index("y"), lax.axis_index("c")
        cps = [pltpu.make_async_remote_copy(
            src_ref=srcs[i], dst_ref=outs[i], send_sem=send_sems.at[i], recv_sem=recv_sems.at[i],
            device_id=(px, py, to_core), device_id_type=_MESH) for i in range(n)]

        @pl.when(pc != to_core)
        def _():
            for cp in cps:
                cp.start()
            for cp in cps:
                cp.wait_send()

        @pl.when(pc == to_core)
        def _():
            for cp in cps:
                cp.wait_recv()

    return pl.pallas_call(
        body, name=name, out_shape=[jax.ShapeDtypeStruct(v.shape, v.dtype) for v in arrays],
        in_specs=[_ANY] * n, out_specs=[_ANY] * n,
        scratch_shapes=[pltpu.SemaphoreType.DMA((n,)), pltpu.SemaphoreType.DMA((n,))],
    )(*arrays)


def _fuse(body, n_in, n_out, n_scratch, n_steps, fused):
    if fused is None:
        return body
    f_in, f_out, f_scr, steps = fused
    a, b, c = len(f_in), len(f_out), len(f_scr)

    def wrapped(*refs):
        ins, rest = refs[:n_in + a], refs[n_in + a:]
        outs, scr = rest[:n_out + b], rest[n_out + b:]
        start, finish = steps(ins[n_in:], outs[n_out:], scr[n_scratch:])
        step = pl.program_id(0)

        @pl.when(step == 0)
        def _():
            start()

        body(*ins[:n_in], *outs[:n_out], *scr[:n_scratch])

        @pl.when(step == n_steps - 1)
        def _():
            finish()

    return wrapped


def sibling_swap(x, name):
    def body(x_ref, out_ref, send_sem, recv_sem):
        px, py, pc = lax.axis_index("x"), lax.axis_index("y"), lax.axis_index("c")
        cp = pltpu.make_async_remote_copy(
            src_ref=x_ref, dst_ref=out_ref, send_sem=send_sem, recv_sem=recv_sem,
            device_id=(px, py, 1 - pc), device_id_type=_MESH)
        cp.start()
        cp.wait()

    return pl.pallas_call(
        body, name=name, out_shape=jax.ShapeDtypeStruct(x.shape, x.dtype), in_specs=[_ANY], out_specs=_ANY,
        scratch_shapes=[pltpu.SemaphoreType.DMA, pltpu.SemaphoreType.DMA],
    )(x)


ELT_TILE = 128


def _elt_rows(m, big=False):
    for t in ((1728, 1408, 1024, 640) if big else ()) + (512, 256, ELT_TILE, 16, 8):
        if m % t == 0:
            return t
    return m


def pair_add(a, b, name):
    m, n = b.shape
    tm = _elt_rows(m, big=True)

    def body(a_ref, b_ref, o_ref):
        o_ref[...] = (a_ref[...].astype(F32) + b_ref[...].astype(F32)).astype(o_ref.dtype)

    return pl.pallas_call(
        body, name=name, grid=(m // tm,), in_specs=[_row_spec(tm, n)] * 2, out_specs=_row_spec(tm, n),
        out_shape=jax.ShapeDtypeStruct((m, n), b.dtype), compiler_params=_params("arbitrary"),
    )(a, b)


def sum_leading(q, name):
    kk, m, n = q.shape
    tm = _elt_rows(m)

    def body(q_ref, o_ref):
        acc = q_ref[0].astype(F32)
        for i in range(1, kk):
            acc = acc + q_ref[i].astype(F32)
        o_ref[...] = acc

    return pl.pallas_call(
        body, name=name, grid=(m // tm,), in_specs=[pl.BlockSpec((kk, tm, n), lambda i: (0, i, 0))],
        out_specs=_row_spec(tm, n), out_shape=jax.ShapeDtypeStruct((m, n), F32), compiler_params=_params("arbitrary"),
    )(q)


def sum_shares(q0, q1, own0, own1, name):
    _, m, n = q0.shape
    tm = next((t for t in (640,) if m % t == 0), _elt_rows(m))

    def body(q0_ref, q1_ref, o0_ref, o1_ref, out_ref):
        first = lax.axis_index("c") == 0
        mine = 2 * lax.axis_index("x") + lax.axis_index("y")
        own = jnp.where(first, o0_ref[...], o1_ref[...])
        acc = None
        for k in range(N_CHIPS):
            term = jnp.where(mine == k, own, jnp.where(first, q0_ref[k], q1_ref[k])).astype(F32)
            acc = term if acc is None else acc + term
        out_ref[...] = acc

    slots = pl.BlockSpec((N_CHIPS, tm, n), lambda i: (0, i, 0))
    return pl.pallas_call(
        body, name=name, grid=(m // tm,), in_specs=[slots, slots, _row_spec(tm, n), _row_spec(tm, n)],
        out_specs=_row_spec(tm, n), out_shape=jax.ShapeDtypeStruct((m, n), F32), compiler_params=_params("arbitrary"),
    )(q0, q1, own0, own1)


def adamw(w, g, m, v, name):
    rows, cols = w.shape
    tm = _elt_rows(rows)

    def body(w_ref, g_ref, m_ref, v_ref, d_ref, nm_ref, nv_ref):
        gv = g_ref[...]
        nm = ADAM_B1 * m_ref[...] + (1.0 - ADAM_B1) * gv
        nv = ADAM_B2 * v_ref[...] + (1.0 - ADAM_B2) * jnp.square(gv)
        nm_ref[...] = nm
        nv_ref[...] = nv
        m_hat = nm / (1.0 - ADAM_B1 ** ADAM_STEP)
        v_hat = nv / (1.0 - ADAM_B2 ** ADAM_STEP)
        d_ref[...] = -ADAM_LR * (m_hat / (jnp.sqrt(v_hat) + ADAM_EPS) + ADAM_WD * w_ref[...])

    spec = _row_spec(tm, cols)
    return pl.pallas_call(
        body, name=name, grid=(rows // tm,), in_specs=[spec] * 4, out_specs=[spec] * 3,
        out_shape=[jax.ShapeDtypeStruct((rows, cols), F32)] * 3, compiler_params=_params("arbitrary"),
    )(w, g, m, v)


def _block_diag_dense(w):
    g = w.shape[0]
    return jnp.einsum("gij,gh->gihj", w, jnp.eye(g, dtype=w.dtype)).reshape(g * w.shape[1], g * w.shape[2])


def _diag_blocks(m):
    return jnp.stack([m[HEAD * i:HEAD * (i + 1), HEAD * i:HEAD * (i + 1)] for i in range(LRU_BLOCKS)])


def _rep(v):
    return jnp.repeat(v, HEAD, axis=-1)


def _split_w_in(w_in):
    gdn0 = RET_IN + LRU_IN
    gdn1 = gdn0 + 4 * GDN_W
    w_r = w_in[:, 0:RET_IN]
    w_l = w_in[:, RET_IN:gdn0]
    w_g = jnp.concatenate([w_in[:, gdn0:gdn1], _rep(w_in[:, gdn1:gdn1 + GDN_HEADS]), _rep(w_in[:, gdn1 + GDN_HEADS:])], axis=1)
    w_ab = jnp.pad(w_in[:, gdn1:], ((0, 0), (0, LANES - 2 * GDN_HEADS)))
    return w_r, w_l, w_g, w_ab


WIN_SHARD = D_IN // N_CHIPS
WIN_STRIDE = 832
WIN_ROWS = 960
WIN_T_ROWS = WIN_STRIDE * (N_CHIPS - 1) + WIN_ROWS
AB_ROWS = 16

_GRAD_PIECES = (("ffn1_w_gate", 704, 704), ("ffn1_w_up", 704, 704), ("ffn1_w_down", 704, 704), ("w_in", WIN_ROWS, WIN_STRIDE),
                ("w_out", 256, 256), ("ffn2_w_gate", 704, 704), ("ffn2_w_up", 704, 704), ("ffn2_w_down", 704, 704),
                ("ple_w_gate", 256, 256), ("ple_w_proj", 64, 64))
_TRANSPOSED = ("ffn1_w_gate", "ffn1_w_up", "w_in", "ffn2_w_gate", "ffn2_w_up", "ple_w_proj")
N_EARLY = 5


def _local_step(x, p, pos, target, wt, mesh=None):
    row = lambda v: v[None, :]
    saved = []
    xb = x.astype(BF16)
    pieces = [(r, stride) for _, r, stride in _GRAD_PIECES]
    grad_names = [n for n, _, _ in _GRAD_PIECES]
    n_late = len(pieces) - N_EARLY
    for i in range(DEPTH):
        ffn1 = (wt["ffn1_w_gate"][i], wt["ffn1_w_up"][i], wt["ffn1_w_down"][i])
        if mesh is not None and i == 0:
            half0, half1, make = mesh["rest0"]
            hg1, hu1, r1, x1, x1b, gathered = ffn_fwd(x, row(wt["ln_ffn1_g"][i]), row(wt["ln_ffn1_b"][i]), *ffn1,
                                                      fused=_weight_gather_fused(half0, half1))
            wt = {**wt, **{n: [w0, None] for n, w0 in make(gathered).items()}}
        else:
            hg1, hu1, r1, x1, x1b = ffn_fwd(x, row(wt["ln_ffn1_g"][i]), row(wt["ln_ffn1_b"][i]), *ffn1)
        w_r, w_l, w_g, w_ab = _split_w_in(wt["w_in"][i])
        lw = dict(
            wg1=ffn1[0], wu1=ffn1[1], wd1=ffn1[2], w_r=w_r, w_l=w_l, w_g=w_g, w_ab=w_ab,
            w_out=wt["w_out"][i], wg2=wt["ffn2_w_gate"][i], wu2=wt["ffn2_w_up"][i], wd2=wt["ffn2_w_down"][i],
            wpg=wt["ple_w_gate"][i], wpp=wt["ple_w_proj"][i],
            wa=_block_diag_dense(wt["lru_w_a"][i]), wx=_block_diag_dense(wt["lru_w_x"][i]),
            al=row(_rep(wt["gdn_a_log"][i])), dt=row(_rep(wt["gdn_dt_bias"][i])), ng=row(jnp.tile(wt["gdn_norm_g"][i], GDN_HEADS)))
        hr, hl, hgd = win_fwd(x1, w_r, w_l, w_g)
        o_r, opre_r, st_r = ret_fwd(hr, pos, row(wt["ret_norm_g"][i]))
        o_l, xc, hs = lru_fwd(hl, wt["lru_conv_w"][i], row(wt["lru_conv_b"][i]), lw["wa"], row(wt["lru_b_a"][i]), lw["wx"],
                              row(wt["lru_b_x"][i]), row(wt["lru_lambda"][i]))
        if mesh is not None and i == 0:
            half0, half1, make = mesh["layer1"]
            o_g, opre_g, tmat, st_g, gathered = gdn_fwd(hgd, wt["gdn_conv_w"][i], lw["al"], lw["dt"], lw["ng"],
                                                        fused=_weight_gather_fused(half0, half1))
            wt = {**wt, **{n: [wt[n][0], w1] for n, w1 in make(gathered).items()}}
        else:
            o_g, opre_g, tmat, st_g = gdn_fwd(hgd, wt["gdn_conv_w"][i], lw["al"], lw["dt"], lw["ng"])
        r2, x2, x2b, ocat = out_fwd(o_r, o_l, o_g, x1, lw["w_out"], row(wt["ln_mix_g"][i]), row(wt["ln_mix_b"][i]))
        hg2, hu2, r3, x3, x3b, pg, pp = ffn_fwd(x2, row(wt["ln_ffn2_g"][i]), row(wt["ln_ffn2_b"][i]), lw["wg2"], lw["wu2"],
                                                lw["wd2"], ple=(p[i], lw["wpg"], lw["wpp"]))
        saved.append(dict(lw=lw, x0=xb, hg1=hg1, hu1=hu1, r1=r1, x1=x1b, hr=hr, hl=hl, hgd=hgd, ocat=ocat, opre_r=opre_r,
                          st_r=st_r, xc=xc, hs=hs, opre_g=opre_g, tmat=tmat, st_g=st_g, r2=r2, x2=x2b, hg2=hg2, hu2=hu2,
                          r3=r3, pg=pg, pp=pp))
        x, xb = x3, x3b

    dx, loss = loss_and_grad(x, target)
    grads = [None] * DEPTH
    big = [None] * DEPTH
    pair_sums = [None] * DEPTH
    for i in reversed(range(DEPTH)):
        sv = saved[i]
        lw = sv["lw"]
        tag = f"_l{i}"
        dx2, act2, dhg2, dhu2, dy2, dg3, db3, dpg, dpp = ffn_bwd(
            dx, sv["r3"], sv["x2"], sv["hg2"], sv["hu2"], row(wt["ln_ffn2_g"][i]), lw["wg2"], lw["wu2"], lw["wd2"],
            ple=(sv["pg"], sv["pp"], lw["wpg"]))
        g, bg = {}, {}
        bg["ffn2_w_gate"] = wgrad(dhg2, sv["x2"], "wgrad_gate2" + tag)
        bg["ffn2_w_up"] = wgrad(dhu2, sv["x2"], "wgrad_up2" + tag)
        bg["ffn2_w_down"] = wgrad(act2, dy2, "wgrad_down2" + tag)
        bg["ple_w_gate"] = wgrad(sv["x2"], dpg, "wgrad_pgate" + tag)
        bg["ple_w_proj"] = wgrad(dpp, p[i], "wgrad_pproj" + tag).reshape(PLE_DIM, D_MODEL)
        g["ln_ffn2_g"], g["ln_ffn2_b"] = dg3[0], db3[0]
        dr2, dr2b, do_r, do_l, do_g, dg2, db2 = out_bwd(dx2, sv["r2"], row(wt["ln_mix_g"][i]), lw["w_out"])
        g["ln_mix_g"], g["ln_mix_b"] = dg2[0], db2[0]
        bg["w_out"] = wgrad(sv["ocat"], dr2b, "wgrad_out" + tag)
        dhr, dgn = ret_bwd(sv["hr"], pos, row(wt["ret_norm_g"][i]), sv["opre_r"], sv["st_r"], do_r)
        g["ret_norm_g"] = dgn[0]
        dhl, dcw, dcb, dwa, dba, dwx, dbx, dlam = lru_bwd(
            sv["hl"], wt["lru_conv_w"][i], row(wt["lru_conv_b"][i]), lw["wa"], row(wt["lru_b_a"][i]), lw["wx"],
            row(wt["lru_b_x"][i]), row(wt["lru_lambda"][i]), sv["xc"], sv["hs"], do_l)
        g["lru_conv_w"], g["lru_conv_b"] = dcw, dcb[0]
        g["lru_w_a"], g["lru_b_a"], g["lru_w_x"], g["lru_b_x"], g["lru_lambda"] = _diag_blocks(dwa), dba[0], _diag_blocks(dwx), dbx[0], dlam[0]
        if mesh is not None and i == 0:
            early = [bg[n] for n in grad_names[n_late:]]
            early_sums = [pair_add(u, v, "reduce_pair_add_l0_" + n) for n, u, v in
                          zip(grad_names[n_late:], early, sibling_send(early, 0, "reduce_pair_send_l0_early"))]
            q1_shape, sems1 = _chip_exchange_operands(pair_sums[1], pieces)
            q0_shape, sems0 = _chip_exchange_operands(early_sums, pieces[n_late:])
            n1 = len(pieces)

            def steps(ins, outs, scr):
                start1, finish1 = _chip_exchange_steps(ins[:n1], outs[0], scr[:2], pieces, 1)
                start0, finish0 = _chip_exchange_steps(ins[n1:], outs[1], scr[2:], pieces[n_late:], 0)
                return (lambda: (start1(), start0())), (lambda: (finish1(), finish0()))

            dhq, dab, dgcw, dal, ddt, dng, arrived = gdn_bwd(
                sv["hgd"], wt["gdn_conv_w"][i], lw["al"], lw["dt"], lw["ng"], sv["opre_g"], sv["tmat"], sv["st_g"], do_g,
                fused=(pair_sums[1] + early_sums, [q1_shape, q0_shape], sems1 + sems0, steps))
            big[1], early_arrived = arrived
        else:
            dhq, dab, dgcw, dal, ddt, dng = gdn_bwd(sv["hgd"], wt["gdn_conv_w"][i], lw["al"], lw["dt"], lw["ng"], sv["opre_g"],
                                                    sv["tmat"], sv["st_g"], do_g)
        g["gdn_conv_w"] = dgcw
        g["gdn_a_log"], g["gdn_dt_bias"] = dal[0, ::HEAD], ddt[0, ::HEAD]
        g["gdn_norm_g"] = dng[0].reshape(GDN_HEADS, HEAD).sum(0)
        dx1 = win_bwd(dr2, dhr, dhl, dhq, dab, lw["w_r"], lw["w_l"], lw["w_g"], lw["w_ab"])
        used = RET_IN + LRU_IN + 4 * GDN_W + AB_ROWS
        bg["w_in"] = jnp.concatenate(
            [wgrad(dhr, sv["x1"], "wgrad_in_r" + tag), wgrad(dhl, sv["x1"], "wgrad_in_l" + tag),
             wgrad(dhq, sv["x1"], "wgrad_in_q" + tag), wgrad(dab, sv["x1"], "wgrad_in_ab" + tag)[0:AB_ROWS],
             jnp.zeros((WIN_T_ROWS - used, D_MODEL), BF16)], axis=0)
        dx, act1, dhg1, dhu1, dy1, dg1, db1 = ffn_bwd(dx1, sv["r1"], sv["x0"], sv["hg1"], sv["hu1"], row(wt["ln_ffn1_g"][i]),
                                                      lw["wg1"], lw["wu1"], lw["wd1"])
        bg["ffn1_w_gate"] = wgrad(dhg1, sv["x0"], "wgrad_gate1" + tag)
        bg["ffn1_w_up"] = wgrad(dhu1, sv["x0"], "wgrad_up1" + tag)
        bg["ffn1_w_down"] = wgrad(act1, dy1, "wgrad_down1" + tag)
        g["ln_ffn1_g"], g["ln_ffn1_b"] = dg1[0], db1[0]
        grads[i] = g
        if mesh is None:
            big[i] = bg
        else:
            names = grad_names if i == 1 else grad_names[:n_late]
            mine = [bg[n] for n in names]
            theirs = sibling_send(mine, i, f"reduce_pair_send_l{i}")
            sums = [pair_add(u, v, f"reduce_pair_add_l{i}_" + n) for n, u, v in zip(names, mine, theirs)]
            if i == 1:
                pair_sums[1] = sums
            else:
                late_arrived = chip_exchange(sums, pieces[:n_late], 0, "reduce_chip_exchange_l0")
                pair_sums[0] = sums + early_sums
                big[0] = jnp.concatenate([late_arrived, early_arrived], axis=1)
    if mesh is not None:
        chip = 2 * lax.axis_index("x") + lax.axis_index("y")
        big = (big, [_own_share(pair_sums[layer], pieces, chip) for layer in range(DEPTH)])
    return loss, dx, {k: jnp.stack([grads[i][k] for i in range(DEPTH)]) for k in grads[0]}, big


def _natural_grad(name, rows):
    if name == "ple_w_proj":
        return rows.reshape(-1, PLE_DIM).T
    return rows.T if name in _TRANSPOSED else rows


_SPLIT = dict(ffn1_w_gate=2, ffn1_w_up=2, ffn1_w_down=1, w_in=2, w_out=1, ffn2_w_gate=2, ffn2_w_up=2, ffn2_w_down=1,
              ple_w_gate=1, ple_w_proj=2)
_CONV = ("lru_conv_w", "gdn_conv_w")
_WHOLE = ("ln_ffn1_g", "ln_ffn1_b", "ret_norm_g", "lru_conv_b", "lru_w_a", "lru_b_a", "lru_w_x", "lru_b_x", "lru_lambda",
          "gdn_a_log", "gdn_dt_bias", "gdn_norm_g", "ln_mix_g", "ln_mix_b", "ln_ffn2_g", "ln_ffn2_b")
_WEIGHTS = ("ln_ffn1_g", "ln_ffn1_b", "ffn1_w_gate", "ffn1_w_up", "ffn1_w_down", "w_in", "ret_norm_g", "lru_conv_w", "lru_conv_b",
            "lru_w_a", "lru_b_a", "lru_w_x", "lru_b_x", "lru_lambda", "gdn_conv_w", "gdn_a_log", "gdn_dt_bias", "gdn_norm_g",
            "w_out", "ln_mix_g", "ln_mix_b", "ffn2_w_gate", "ffn2_w_up", "ffn2_w_down", "ple_w_gate", "ple_w_proj",
            "ln_ffn2_g", "ln_ffn2_b")
_INPUTS = ("x", "p", "positions") + _WEIGHTS + ("loss_target",) + tuple("m_" + n for n in _WEIGHTS) + tuple("v_" + n for n in _WEIGHTS)

BIG_COLS = 1024
SMALL_COLS = LANES
SMALL_ROWS_MULT = 512


def _pack(arrays, dtype, cols, rows_mult):
    flat = jnp.concatenate([a.reshape(-1).astype(dtype) for a in arrays])
    rows = -(-flat.shape[0] // cols)
    rows = -(-rows // rows_mult) * rows_mult
    return jnp.pad(flat, (0, rows * cols - flat.shape[0])).reshape(rows, cols)


def _unpack(packed, shapes):
    flat = packed.reshape(-1)
    out, off = [], 0
    for shp in shapes:
        size = int(np.prod(shp))
        out.append(flat[off:off + size].reshape(shp))
        off += size
    return out


def _as2d(a):
    return a.reshape(-1, a.shape[-1])


def kernel(x, p, positions, ln_ffn1_g, ln_ffn1_b, ffn1_w_gate, ffn1_w_up, ffn1_w_down, w_in, ret_norm_g, lru_conv_w, lru_conv_b, lru_w_a, lru_b_a, lru_w_x, lru_b_x, lru_lambda, gdn_conv_w, gdn_a_log, gdn_dt_bias, gdn_norm_g, w_out, ln_mix_g, ln_mix_b, ffn2_w_gate, ffn2_w_up, ffn2_w_down, ple_w_gate, ple_w_proj, ln_ffn2_g, ln_ffn2_b, loss_target, m_ln_ffn1_g, m_ln_ffn1_b, m_ffn1_w_gate, m_ffn1_w_up, m_ffn1_w_down, m_w_in, m_ret_norm_g, m_lru_conv_w, m_lru_conv_b, m_lru_w_a, m_lru_b_a, m_lru_w_x, m_lru_b_x, m_lru_lambda, m_gdn_conv_w, m_gdn_a_log, m_gdn_dt_bias, m_gdn_norm_g, m_w_out, m_ln_mix_g, m_ln_mix_b, m_ffn2_w_gate, m_ffn2_w_up, m_ffn2_w_down, m_ple_w_gate, m_ple_w_proj, m_ln_ffn2_g, m_ln_ffn2_b, v_ln_ffn1_g, v_ln_ffn1_b, v_ffn1_w_gate, v_ffn1_w_up, v_ffn1_w_down, v_w_in, v_ret_norm_g, v_lru_conv_w, v_lru_conv_b, v_lru_w_a, v_lru_b_a, v_lru_w_x, v_lru_b_x, v_lru_lambda, v_gdn_conv_w, v_gdn_a_log, v_gdn_dt_bias, v_gdn_norm_g, v_w_out, v_ln_mix_g, v_ln_mix_b, v_ffn2_w_gate, v_ffn2_w_up, v_ffn2_w_down, v_ple_w_gate, v_ple_w_proj, v_ln_ffn2_g, v_ln_ffn2_b):
    a = dict(zip(_INPUTS, (x, p, positions, ln_ffn1_g, ln_ffn1_b, ffn1_w_gate, ffn1_w_up, ffn1_w_down, w_in, ret_norm_g, lru_conv_w, lru_conv_b, lru_w_a, lru_b_a, lru_w_x, lru_b_x, lru_lambda, gdn_conv_w, gdn_a_log, gdn_dt_bias, gdn_norm_g, w_out, ln_mix_g, ln_mix_b, ffn2_w_gate, ffn2_w_up, ffn2_w_down, ple_w_gate, ple_w_proj, ln_ffn2_g, ln_ffn2_b, loss_target, m_ln_ffn1_g, m_ln_ffn1_b, m_ffn1_w_gate, m_ffn1_w_up, m_ffn1_w_down, m_w_in, m_ret_norm_g, m_lru_conv_w, m_lru_conv_b, m_lru_w_a, m_lru_b_a, m_lru_w_x, m_lru_b_x, m_lru_lambda, m_gdn_conv_w, m_gdn_a_log, m_gdn_dt_bias, m_gdn_norm_g, m_w_out, m_ln_mix_g, m_ln_mix_b, m_ffn2_w_gate, m_ffn2_w_up, m_ffn2_w_down, m_ple_w_gate, m_ple_w_proj, m_ln_ffn2_g, m_ln_ffn2_b, v_ln_ffn1_g, v_ln_ffn1_b, v_ffn1_w_gate, v_ffn1_w_up, v_ffn1_w_down, v_w_in, v_ret_norm_g, v_lru_conv_w, v_lru_conv_b, v_lru_w_a, v_lru_b_a, v_lru_w_x, v_lru_b_x, v_lru_lambda, v_gdn_conv_w, v_gdn_a_log, v_gdn_dt_bias, v_gdn_norm_g, v_w_out, v_ln_mix_g, v_ln_mix_b, v_ffn2_w_gate, v_ffn2_w_up, v_ffn2_w_down, v_ple_w_gate, v_ple_w_proj, v_ln_ffn2_g, v_ln_ffn2_b)))
    assert len(a) == len(_INPUTS)
    core = lax.axis_index("c")
    chip = 2 * lax.axis_index("x") + lax.axis_index("y")
    big = list(_SPLIT)

    def group(layer, names, n_first):
        shards = [a[n][layer].astype(BF16) for n in names]

        def make(gathered):
            return {n: jnp.concatenate([gathered[i][k] for k in range(N_CHIPS)], axis=_SPLIT[n] - 1) for i, n in enumerate(names)}

        return shards[:n_first], shards[n_first:], make

    ffn1_0, ffn1_1, make_ffn1 = group(0, big[:3], 2)
    wt = {n: [w0, None] for n, w0 in make_ffn1(gather_layer_weights(ffn1_0, ffn1_1, "gather_weights_ffn1_l0")).items()}
    conv_g = all_gather8(_pack([a[n] for n in _CONV], F32, SMALL_COLS, SMALL_ROWS_MULT), "gather_conv_weights")[0::2]
    conv_g = conv_g.reshape(N_CHIPS, -1)
    off = 0
    for n in _CONV:
        shp = a[n].shape
        size = int(np.prod(shp))
        parts = conv_g[:, off:off + size].reshape((N_CHIPS,) + shp)
        wt[n] = jnp.concatenate([parts[k] for k in range(N_CHIPS)], axis=2)
        off += size
    for n in _WHOLE:
        wt[n] = a[n]

    seq = a["x"].shape[1]
    mesh = dict(rest0=group(0, big[3:], 4), layer1=group(1, big, len(big) // 2))
    loss_part, dx, grads, (arrived, own) = _local_step(a["x"][0], a["p"][:, 0], a["positions"].reshape(seq, 1),
                                                       a["loss_target"][0], wt, mesh=mesh)
    loss = lax.psum(loss_part[0, 0], ("x", "y", "c"))

    my_layer_sum = sum_shares(arrived[0], arrived[1], own[0], own[1], "reduce_chip_sum")
    other_layer_sum = sibling_swap(my_layer_sum, "reduce_pair_share")
    reduced = [jnp.where(core == layer, my_layer_sum, other_layer_sum) for layer in range(DEPTH)]
    big_grads = {}
    off = 0
    for n, r, _ in _GRAD_PIECES:
        per_layer = []
        for layer in range(DEPTH):
            rows = reduced[layer][off:off + r]
            if n == "w_in":
                rows = lax.dynamic_slice_in_dim(rows, chip * (WIN_SHARD - WIN_STRIDE), WIN_SHARD, axis=0)
            per_layer.append(_natural_grad(n, rows))
        big_grads[n] = jnp.stack(per_layer)
        off += r

    small_names = list(_WHOLE) + list(_CONV)
    small_local = _pack([grads[n] for n in small_names], F32, SMALL_COLS, SMALL_ROWS_MULT)
    small_sum = sum_leading(all_gather8(small_local, "gather_small_grads"), "sum_small_grads")
    small_grads = dict(zip(small_names, _unpack(small_sum, [grads[n].shape for n in small_names])))
    for n in _CONV:
        width = a[n].shape[2]
        small_grads[n] = lax.dynamic_slice_in_dim(small_grads[n], chip * width, width, axis=2)

    new = {}
    for n in big:
        d, nm, nv = adamw(_as2d(a[n]), _as2d(big_grads[n]), _as2d(a["m_" + n]), _as2d(a["v_" + n]), "adamw_" + n)
        new[n] = tuple(t.reshape(a[n].shape) for t in (d, nm, nv))
    pk = lambda prefix: _pack([a[prefix + n] for n in small_names], F32, SMALL_COLS, SMALL_ROWS_MULT)
    pg = _pack([small_grads[n] for n in small_names], F32, SMALL_COLS, SMALL_ROWS_MULT)
    outs = adamw(pk(""), pg, pk("m_"), pk("v_"), "adamw_small")
    shapes = [a[n].shape for n in small_names]
    for n, d, nm, nv in zip(small_names, *[_unpack(o, shapes) for o in outs]):
        new[n] = (d, nm, nv)
    all_grads = {**big_grads, **small_grads}
    return (loss, dx[None], *[all_grads[n] for n in _WEIGHTS], *[new[n][0] for n in _WEIGHTS],
            *[new[n][1] for n in _WEIGHTS], *[new[n][2] for n in _WEIGHTS])
```

```python
import functools
import math

import numpy as np
import jax
import jax.numpy as jnp
from jax import lax
from jax.experimental import pallas as pl
from jax.experimental.pallas import tpu as pltpu

F32 = jnp.float32
BF16 = jnp.bfloat16

D_MODEL = 1024
D_FF = 2816
PLE_DIM = 256
DEPTH = 2
CHUNK = 64
RET_HEADS = 4
RET_W = 256
LRU_W = 384
LRU_BLOCKS = 6
GDN_HEADS = 6
GDN_W = 384
HEAD = 64
D_IN = 3340
RET_IN = 4 * RET_W
LRU_IN = 2 * LRU_W
GDN_IN = 6 * GDN_W
ROPE_THETA = 10000.0
ALPHA = (2 * DEPTH) ** 0.25
LN_EPS = 1e-5
LRU_C = 8.0
N_CHIPS = 4
N_DEV = 8

ADAM_LR = 0.001
ADAM_B1 = 0.9
ADAM_B2 = 0.999
ADAM_EPS = 1e-08
ADAM_WD = 0.01
ADAM_STEP = 10

LANES = 128
VMEM_LIMIT = 56 * 1024 * 1024
ROW_TILE = 256
ROW_TILE_BWD = 512
SCAN_TILE = 256


def _params(*sem):
    return pltpu.CompilerParams(dimension_semantics=sem, vmem_limit_bytes=VMEM_LIMIT)


def _operand(a):
    return a.astype(BF16)


def _mm(a, b):
    return jnp.dot(_operand(a), _operand(b), preferred_element_type=F32)


def _mm_nt(a, b):
    return lax.dot_general(_operand(a), _operand(b), (((1,), (1,)), ((), ())), preferred_element_type=F32)


def _mm_tn(a, b):
    return lax.dot_general(_operand(a), _operand(b), (((0,), (0,)), ((), ())), preferred_element_type=F32)


def _split(a):
    hi = a.astype(BF16)
    lo = (a - hi.astype(F32)).astype(BF16)
    return hi, lo


def _mm3(a, b):
    ah, al = _split(a)
    bh, bl = _split(b)
    return _mm(ah, bh) + (_mm(ah, bl) + _mm(al, bh))


def _sigmoid(x):
    return jax.nn.sigmoid(x)


def _log1p(u):
    w = 1.0 + u
    return jnp.where(w == 1.0, u, jnp.log(w) * (u / jnp.where(w == 1.0, 1.0, w - 1.0)))


def _expm1(y):
    u = jnp.exp(y)
    um1 = u - 1.0
    safe = jnp.where((u == 1.0) | (um1 == -1.0), 1.0, jnp.log(jnp.where(u == 0.0, 1.0, u)))
    return jnp.where(u == 1.0, y, jnp.where(um1 == -1.0, -1.0, um1 * (y / safe)))


def _softplus(x):
    return jnp.maximum(x, 0.0) + _log1p(jnp.exp(-jnp.abs(x)))


_GELU_C = math.sqrt(2.0 / math.pi)


def _gelu(x):
    return 0.5 * x * (1.0 + jnp.tanh(_GELU_C * (x + 0.044715 * (x * x * x))))


def _gelu_grad(x):
    t = jnp.tanh(_GELU_C * (x + 0.044715 * (x * x * x)))
    return 0.5 * (1.0 + t) + 0.5 * x * (1.0 - t * t) * (_GELU_C * (1.0 + 3.0 * 0.044715 * (x * x)))


def _silu_and_grad(x):
    s = _sigmoid(x)
    return x * s, s * (1.0 + x * (1.0 - s))


def _group_sum_slab(x):
    lane = lax.broadcasted_iota(jnp.int32, x.shape, 1)
    low = jnp.sum(x[:, 0:LANES // 2], axis=1, keepdims=True)
    high = jnp.sum(x[:, LANES // 2:], axis=1, keepdims=True)
    return jnp.where(lane < LANES // 2, low, high)


def _group_sum(x):
    n = x.shape[1] // LANES
    if n == 1:
        return _group_sum_slab(x)
    return jnp.concatenate([_group_sum_slab(x[:, LANES * i:LANES * (i + 1)]) for i in range(n)], axis=1)


def _rows_prefix_sum(x):
    n = x.shape[0]
    row = lax.broadcasted_iota(jnp.int32, x.shape, 0)
    d = 1
    while d < n:
        x = x + jnp.where(row >= d, pltpu.roll(x, d, 0), 0.0)
        d *= 2
    return x


def _rows_suffix_sum(x):
    n = x.shape[0]
    row = lax.broadcasted_iota(jnp.int32, x.shape, 0)
    d = 1
    while d < n:
        x = x + jnp.where(row < n - d, pltpu.roll(x, n - d, 0), 0.0)
        d *= 2
    return x


def _shift_rows(cur, prev, j):
    row = lax.broadcasted_iota(jnp.int32, cur.shape, 0)
    return jnp.where(row < j, pltpu.roll(prev, j, 0), pltpu.roll(cur, j, 0))


def _shift_rows_up(cur, nxt, j):
    n = cur.shape[0]
    row = lax.broadcasted_iota(jnp.int32, cur.shape, 0)
    return jnp.where(row < n - j, pltpu.roll(cur, n - j, 0), pltpu.roll(nxt, n - j, 0))


def _layer_norm_stats(r):
    mu = jnp.mean(r, axis=-1, keepdims=True)
    d = r - mu
    var = jnp.mean(d * d, axis=-1, keepdims=True)
    rstd = lax.rsqrt(var + LN_EPS)
    return d * rstd, rstd


def _load_resident(step, pairs, sems):
    @pl.when(step == 0)
    def _():
        cps = [pltpu.make_async_copy(h, v, sems.at[i]) for i, (h, v) in enumerate(pairs)]
        for c in cps:
            c.start()
        for c in cps:
            c.wait()


def _row_spec(tile, width):
    return pl.BlockSpec((tile, width), lambda i: (i, 0))


def _full_spec(shape):
    nd = len(shape)
    return pl.BlockSpec(shape, lambda i: (0,) * nd)


_ANY = pl.BlockSpec(memory_space=pl.ANY)


def ffn_fwd(x, ln_g, ln_b, w_gate, w_up, w_down, ple=None, fused=None):
    s = x.shape[0]
    tm = ROW_TILE
    with_ple = ple is not None
    weights = [w_gate, w_up, w_down] + ([ple[1], ple[2]] if with_ple else [])

    def body(*refs):
        it = iter(refs)
        x_ref, g_ref, b_ref = next(it), next(it), next(it)
        p_ref = next(it) if with_ple else None
        w_hbm = [next(it) for _ in weights]
        hg_ref, hu_ref, r_ref, xn_ref, xnb_ref = next(it), next(it), next(it), next(it), next(it)
        pg_ref, pp_ref = (next(it), next(it)) if with_ple else (None, None)
        w_vm = [next(it) for _ in weights]
        sems = next(it)
        _load_resident(pl.program_id(0), list(zip(w_hbm, w_vm)), sems)
        xv = x_ref[...]
        xb = xv.astype(BF16)
        hg = _mm(xb, w_vm[0][...])
        hu = _mm(xb, w_vm[1][...])
        hg_ref[...] = hg
        hu_ref[...] = hu
        act = (hg * _sigmoid(hg)) * hu
        r = ALPHA * xv + 0.5 * _mm(act.astype(BF16), w_vm[2][...])
        if with_ple:
            pg = _mm(xb, w_vm[3][...])
            pp = _mm(p_ref[...].astype(BF16), w_vm[4][...])
            pg_ref[...] = pg
            pp_ref[...] = pp
            r = r + _sigmoid(pg) * pp
        r_ref[...] = r
        xhat, _ = _layer_norm_stats(r)
        xn = xhat * g_ref[...] + b_ref[...]
        xn_ref[...] = xn
        xnb_ref[...] = xn.astype(BF16)

    d, f = D_MODEL, D_FF
    in_specs = [_row_spec(tm, d), _full_spec((1, d)), _full_spec((1, d))]
    args = [x, ln_g, ln_b]
    if with_ple:
        in_specs.append(_row_spec(tm, PLE_DIM))
        args.append(ple[0])
    in_specs += [_ANY] * len(weights)
    args += weights
    out_shape = [jax.ShapeDtypeStruct((s, f), F32), jax.ShapeDtypeStruct((s, f), F32),
                 jax.ShapeDtypeStruct((s, d), F32), jax.ShapeDtypeStruct((s, d), F32), jax.ShapeDtypeStruct((s, d), BF16)]
    out_specs = [_row_spec(tm, f), _row_spec(tm, f), _row_spec(tm, d), _row_spec(tm, d), _row_spec(tm, d)]
    if with_ple:
        out_shape += [jax.ShapeDtypeStruct((s, d), F32)] * 2
        out_specs += [_row_spec(tm, d)] * 2
    scratch = [pltpu.VMEM(w.shape, w.dtype) for w in weights] + [pltpu.SemaphoreType.DMA((len(weights),))]
    f_in, f_out, f_scr, _ = fused if fused is not None else ([], [], [], None)
    n_out = len(out_shape)
    outs = pl.pallas_call(
        _fuse(body, len(args), n_out, len(scratch), s // tm, fused),
        name=("ffn_fwd_ple" if with_ple else "ffn_fwd") + ("_gather" if fused is not None else ""), grid=(s // tm,),
        in_specs=in_specs + [_ANY] * len(f_in), out_specs=out_specs + [_ANY] * len(f_out),
        out_shape=out_shape + list(f_out), scratch_shapes=scratch + list(f_scr), compiler_params=_params("arbitrary"),
    )(*args, *f_in)
    return tuple(outs[:n_out]) + ((list(outs[n_out:]),) if fused is not None else ())


def ffn_bwd(dxn, r, x, hg, hu, ln_g, w_gate, w_up, w_down, ple=None):
    s = x.shape[0]
    with_ple = ple is not None
    d, f = D_MODEL, D_FF
    suffix = "_ple" if with_ple else ""

    tm = ROW_TILE

    def body_a(*refs):
        it = iter(refs)
        dxn_ref, r_ref, hg_ref, hu_ref, g_ref = (next(it) for _ in range(5))
        pg_ref, pp_ref = (next(it), next(it)) if with_ple else (None, None)
        wd_hbm = next(it)
        dr_ref, act_ref, dhg_ref, dhu_ref, dy_ref, dg_ref, db_ref = (next(it) for _ in range(7))
        dpg_ref, dpp_ref = (next(it), next(it)) if with_ple else (None, None)
        wd_vm, sems = next(it), next(it)
        step = pl.program_id(0)
        _load_resident(step, [(wd_hbm, wd_vm)], sems)

        @pl.when(step == 0)
        def _():
            dg_ref[...] = jnp.zeros_like(dg_ref)
            db_ref[...] = jnp.zeros_like(db_ref)

        dxn_v = dxn_ref[...]
        xhat, rstd = _layer_norm_stats(r_ref[...])
        dg_ref[...] += jnp.sum(dxn_v * xhat, axis=0, keepdims=True)
        db_ref[...] += jnp.sum(dxn_v, axis=0, keepdims=True)
        dyh = dxn_v * g_ref[...]
        dr = rstd * (dyh - jnp.mean(dyh, axis=-1, keepdims=True) - xhat * jnp.mean(dyh * xhat, axis=-1, keepdims=True))
        dr_ref[...] = dr
        dy = (0.5 * dr).astype(BF16)
        dy_ref[...] = dy
        da = _mm_nt(dy, wd_vm[...])
        hg_v = hg_ref[...]
        hu_v = hu_ref[...]
        sil, dsil = _silu_and_grad(hg_v)
        act_ref[...] = (sil * hu_v).astype(BF16)
        dhu_ref[...] = (da * sil).astype(BF16)
        dhg_ref[...] = (da * hu_v * dsil).astype(BF16)
        if with_ple:
            sp = _sigmoid(pg_ref[...])
            dpp_ref[...] = (dr * sp).astype(BF16)
            dpg_ref[...] = (dr * pp_ref[...] * sp * (1.0 - sp)).astype(BF16)

    in_specs = [_row_spec(tm, d), _row_spec(tm, d), _row_spec(tm, f), _row_spec(tm, f), _full_spec((1, d))]
    args = [dxn, r, hg, hu, ln_g]
    if with_ple:
        in_specs += [_row_spec(tm, d), _row_spec(tm, d)]
        args += [ple[0], ple[1]]
    out_shape = [jax.ShapeDtypeStruct((s, d), F32), jax.ShapeDtypeStruct((s, f), BF16), jax.ShapeDtypeStruct((s, f), BF16),
                 jax.ShapeDtypeStruct((s, f), BF16), jax.ShapeDtypeStruct((s, d), BF16),
                 jax.ShapeDtypeStruct((1, d), F32), jax.ShapeDtypeStruct((1, d), F32)]
    out_specs = [_row_spec(tm, d), _row_spec(tm, f), _row_spec(tm, f), _row_spec(tm, f), _row_spec(tm, d),
                 _full_spec((1, d)), _full_spec((1, d))]
    if with_ple:
        out_shape += [jax.ShapeDtypeStruct((s, d), BF16)] * 2
        out_specs += [_row_spec(tm, d)] * 2
    first = pl.pallas_call(
        body_a, name="ffn_bwd_hidden" + suffix, grid=(s // tm,), in_specs=in_specs + [_ANY], out_specs=out_specs,
        out_shape=out_shape, scratch_shapes=[pltpu.VMEM(w_down.shape, w_down.dtype), pltpu.SemaphoreType.DMA((1,))],
        compiler_params=_params("arbitrary"),
    )(*args, w_down)
    dr, act, dhg, dhu, dy, dg, db = first[:7]

    tb = min(ROW_TILE_BWD, s)
    weights = [w_gate, w_up] + ([ple[2]] if with_ple else [])

    def body_b(*refs):
        it = iter(refs)
        dr_ref, dhg_ref, dhu_ref = next(it), next(it), next(it)
        dpg_ref = next(it) if with_ple else None
        w_hbm = [next(it) for _ in weights]
        dx_ref = next(it)
        w_vm = [next(it) for _ in weights]
        sems = next(it)
        _load_resident(pl.program_id(0), list(zip(w_hbm, w_vm)), sems)
        dx = ALPHA * dr_ref[...] + _mm_nt(dhg_ref[...], w_vm[0][...]) + _mm_nt(dhu_ref[...], w_vm[1][...])
        if with_ple:
            dx = dx + _mm_nt(dpg_ref[...], w_vm[2][...])
        dx_ref[...] = dx

    in_specs = [_row_spec(tb, d), _row_spec(tb, f), _row_spec(tb, f)] + ([_row_spec(tb, d)] if with_ple else [])
    args = [dr, dhg, dhu] + ([first[7]] if with_ple else [])
    dx = pl.pallas_call(
        body_b, name="ffn_bwd_input" + suffix, grid=(s // tb,), in_specs=in_specs + [_ANY] * len(weights),
        out_specs=_row_spec(tb, d), out_shape=jax.ShapeDtypeStruct((s, d), F32),
        scratch_shapes=[pltpu.VMEM(w.shape, w.dtype) for w in weights] + [pltpu.SemaphoreType.DMA((len(weights),))],
        compiler_params=_params("arbitrary"),
    )(*args, *weights)
    return (dx, act, dhg, dhu, dy, dg, db) + tuple(first[7:])


def win_fwd(x1, w_r, w_l, w_g):
    s = x1.shape[0]
    tm = min(ROW_TILE_BWD, s)
    weights = [w_r, w_l, w_g]

    def body(x_ref, wr_h, wl_h, wg_h, hr_ref, hl_ref, hgd_ref, wr_v, wl_v, wg_v, sems):
        _load_resident(pl.program_id(0), [(wr_h, wr_v), (wl_h, wl_v), (wg_h, wg_v)], sems)
        xb = x_ref[...].astype(BF16)
        hr_ref[...] = _mm(xb, wr_v[...])
        hl_ref[...] = _mm(xb, wl_v[...])
        hgd_ref[...] = _mm(xb, wg_v[...])

    return pl.pallas_call(
        body, name="win_fwd", grid=(s // tm,),
        in_specs=[_row_spec(tm, D_MODEL), _ANY, _ANY, _ANY],
        out_specs=[_row_spec(tm, RET_IN), _row_spec(tm, LRU_IN), _row_spec(tm, GDN_IN)],
        out_shape=[jax.ShapeDtypeStruct((s, RET_IN), F32), jax.ShapeDtypeStruct((s, LRU_IN), F32),
                   jax.ShapeDtypeStruct((s, GDN_IN), F32)],
        scratch_shapes=[pltpu.VMEM(w.shape, w.dtype) for w in weights] + [pltpu.SemaphoreType.DMA((3,))],
        compiler_params=_params("arbitrary"),
    )(x1, *weights)


def win_bwd(dr2, dhr, dhl, dhq, dab, w_r, w_l, w_g, w_ab):
    s = dr2.shape[0]
    tm = min(ROW_TILE_BWD, s)
    weights = [w_r, w_l, w_g, w_ab]
    nq = 4 * GDN_W

    def body(dr_ref, dhr_ref, dhl_ref, dhq_ref, dab_ref, wr_h, wl_h, wg_h, wab_h, dx_ref, wr_v, wl_v, wg_v, wab_v, sems):
        _load_resident(pl.program_id(0), [(wr_h, wr_v), (wl_h, wl_v), (wg_h, wg_v), (wab_h, wab_v)], sems)
        dx_ref[...] = (ALPHA * dr_ref[...] + _mm_nt(dhr_ref[...], wr_v[...]) + _mm_nt(dhl_ref[...], wl_v[...])
                       + _mm_nt(dhq_ref[...], wg_v[:, 0:nq]) + _mm_nt(dab_ref[...], wab_v[...]))

    return pl.pallas_call(
        body, name="win_bwd", grid=(s // tm,),
        in_specs=[_row_spec(tm, D_MODEL), _row_spec(tm, RET_IN), _row_spec(tm, LRU_IN), _row_spec(tm, nq), _row_spec(tm, LANES),
                  _ANY, _ANY, _ANY, _ANY],
        out_specs=_row_spec(tm, D_MODEL),
        out_shape=jax.ShapeDtypeStruct((s, D_MODEL), F32),
        scratch_shapes=[pltpu.VMEM(w.shape, w.dtype) for w in weights] + [pltpu.SemaphoreType.DMA((4,))],
        compiler_params=_params("arbitrary"),
    )(dr2, dhr, dhl, dhq, dab, *weights)


def out_fwd(o_r, o_l, o_g, x1, w_out, ln_g, ln_b):
    s = x1.shape[0]
    tm = ROW_TILE

    def body(or_ref, ol_ref, og_ref, x_ref, g_ref, b_ref, w_h, r_ref, xn_ref, xnb_ref, ocat_ref, w_v, sems):
        _load_resident(pl.program_id(0), [(w_h, w_v)], sems)
        ocat = jnp.concatenate([or_ref[...], ol_ref[...], og_ref[...]], axis=1).astype(BF16)
        ocat_ref[...] = ocat
        r = ALPHA * x_ref[...] + _mm(ocat, w_v[...])
        r_ref[...] = r
        xhat, _ = _layer_norm_stats(r)
        xn = xhat * g_ref[...] + b_ref[...]
        xn_ref[...] = xn
        xnb_ref[...] = xn.astype(BF16)

    d = D_MODEL
    return pl.pallas_call(
        body, name="out_fwd", grid=(s // tm,),
        in_specs=[_row_spec(tm, RET_W), _row_spec(tm, LRU_W), _row_spec(tm, GDN_W), _row_spec(tm, d),
                  _full_spec((1, d)), _full_spec((1, d)), _ANY],
        out_specs=[_row_spec(tm, d)] * 4,
        out_shape=[jax.ShapeDtypeStruct((s, d), F32)] * 2 + [jax.ShapeDtypeStruct((s, d), BF16)] * 2,
        scratch_shapes=[pltpu.VMEM(w_out.shape, w_out.dtype), pltpu.SemaphoreType.DMA((1,))],
        compiler_params=_params("arbitrary"),
    )(o_r, o_l, o_g, x1, ln_g, ln_b, w_out)


def out_bwd(dxn, r2, ln_g, w_out):
    s = dxn.shape[0]
    tm = ROW_TILE

    def body(dxn_ref, r_ref, g_ref, w_h, dr_ref, drb_ref, dor_ref, dol_ref, dog_ref, dg_ref, db_ref, w_v, sems):
        step = pl.program_id(0)
        _load_resident(step, [(w_h, w_v)], sems)

        @pl.when(step == 0)
        def _():
            dg_ref[...] = jnp.zeros_like(dg_ref)
            db_ref[...] = jnp.zeros_like(db_ref)

        dxn_v = dxn_ref[...]
        xhat, rstd = _layer_norm_stats(r_ref[...])
        dg_ref[...] += jnp.sum(dxn_v * xhat, axis=0, keepdims=True)
        db_ref[...] += jnp.sum(dxn_v, axis=0, keepdims=True)
        dyh = dxn_v * g_ref[...]
        dr = rstd * (dyh - jnp.mean(dyh, axis=-1, keepdims=True) - xhat * jnp.mean(dyh * xhat, axis=-1, keepdims=True))
        dr_ref[...] = dr
        drb = dr.astype(BF16)
        drb_ref[...] = drb
        dor_ref[...] = _mm_nt(drb, w_v[0:RET_W, :])
        dol_ref[...] = _mm_nt(drb, w_v[RET_W:RET_W + LRU_W, :])
        dog_ref[...] = _mm_nt(drb, w_v[RET_W + LRU_W:, :])

    d = D_MODEL
    return pl.pallas_call(
        body, name="out_bwd", grid=(s // tm,),
        in_specs=[_row_spec(tm, d), _row_spec(tm, d), _full_spec((1, d)), _ANY],
        out_specs=[_row_spec(tm, d), _row_spec(tm, d), _row_spec(tm, RET_W), _row_spec(tm, LRU_W), _row_spec(tm, GDN_W),
                   _full_spec((1, d)), _full_spec((1, d))],
        out_shape=[jax.ShapeDtypeStruct((s, d), F32), jax.ShapeDtypeStruct((s, d), BF16),
                   jax.ShapeDtypeStruct((s, RET_W), F32), jax.ShapeDtypeStruct((s, LRU_W), F32),
                   jax.ShapeDtypeStruct((s, GDN_W), F32), jax.ShapeDtypeStruct((1, d), F32), jax.ShapeDtypeStruct((1, d), F32)],
        scratch_shapes=[pltpu.VMEM(w_out.shape, w_out.dtype), pltpu.SemaphoreType.DMA((1,))],
        compiler_params=_params("arbitrary"),
    )(dxn, r2, ln_g, w_out)


def wgrad(a, b, name, out_dtype=BF16):
    s, m = a.shape
    n = b.shape[1]
    tk = 1024 if s % 1024 == 0 else s
    tm = next((c for c in (1408, 1024, 768, 512, 384, 256) if m % c == 0), m)
    tn = next((c for c in (1408, 1152, 1024, 768, 512) if n % c == 0), n)
    nk = s // tk

    def body(a_ref, b_ref, o_ref, acc_ref):
        k = pl.program_id(2)

        @pl.when(k == 0)
        def _():
            acc_ref[...] = jnp.zeros_like(acc_ref)

        acc_ref[...] += _mm_tn(a_ref[...].astype(BF16), b_ref[...].astype(BF16))

        @pl.when(k == nk - 1)
        def _():
            o_ref[...] = acc_ref[...].astype(o_ref.dtype)

    return pl.pallas_call(
        body, name=name, grid=(m // tm, n // tn, nk),
        in_specs=[pl.BlockSpec((tk, tm), lambda i, j, k: (k, i)), pl.BlockSpec((tk, tn), lambda i, j, k: (k, j))],
        out_specs=pl.BlockSpec((tm, tn), lambda i, j, k: (i, j)),
        out_shape=jax.ShapeDtypeStruct((m, n), out_dtype),
        scratch_shapes=[pltpu.VMEM((tm, tn), F32)],
        compiler_params=_params("arbitrary", "arbitrary", "arbitrary"),
    )(a, b)


def loss_and_grad(y, target):
    s, d = y.shape
    tm = ROW_TILE

    def body(y_ref, t_ref, dy_ref, l_ref):
        @pl.when(pl.program_id(0) == 0)
        def _():
            l_ref[...] = jnp.zeros_like(l_ref)

        err = y_ref[...] - t_ref[...]
        dy_ref[...] = err / d
        l_ref[...] += 0.5 * jnp.sum(jnp.mean(err * err, axis=-1, keepdims=True), axis=0, keepdims=True)

    return pl.pallas_call(
        body, name="loss_and_grad", grid=(s // tm,),
        in_specs=[_row_spec(tm, d), _row_spec(tm, d)],
        out_specs=[_row_spec(tm, d), _full_spec((1, 1))],
        out_shape=[jax.ShapeDtypeStruct((s, d), F32), jax.ShapeDtypeStruct((1, 1), F32)],
        compiler_params=_params("arbitrary"),
    )(y, target)


def _ret_consts():
    lg = np.log1p(-np.exp2(-5.0 - np.arange(RET_HEADS, dtype=np.float64)))
    idx = np.arange(CHUNK, dtype=np.float64)
    intra = np.exp(np.abs(idx[:, None] - idx[None, :])[None] * lg[:, None, None])
    cross = np.repeat(np.exp((idx + 1.0)[:, None] * lg[None, :]), HEAD, axis=1)
    tail = np.repeat(np.exp((CHUNK - 1.0 - idx)[:, None] * lg[None, :]), HEAD, axis=1)
    dec = np.repeat(np.exp(CHUNK * lg)[None, :], HEAD, axis=1)
    half = HEAD // 2
    inv_freq = (ROPE_THETA ** (-jnp.arange(half, dtype=F32) / half))
    invf = jnp.tile(inv_freq, 2 * LANES // HEAD)[None, :]
    sgn = np.tile(np.concatenate([-np.ones(half), np.ones(half)]), LANES // HEAD)[None, :]
    f = lambda a: jnp.asarray(a, F32)
    return dict(intra=f(intra), cross=f(cross), tail=f(tail), dec=f(dec), invf=invf, sgn=f(sgn))


def _swap_halves(t):
    lane = lax.broadcasted_iota(jnp.int32, t.shape, 1)
    return jnp.where((lane & 32) == 0, pltpu.roll(t, LANES - 32, 1), pltpu.roll(t, 32, 1))


def _rope(t, c, s):
    return t * c + _swap_halves(t) * s


def _rope_transposed(g, c, s):
    return g * c + _swap_halves(g * s)


def _head_mask(hd):
    lane = lax.broadcasted_iota(jnp.int32, (1, LANES), 1)
    return ((lane >= HEAD * hd) & (lane < HEAD * (hd + 1))).astype(F32)


def _block_diag_mask():
    r = lax.broadcasted_iota(jnp.int32, (LANES, LANES), 0)
    c = lax.broadcasted_iota(jnp.int32, (LANES, LANES), 1)
    return ((r >= HEAD) == (c >= HEAD)).astype(F32)


RET_STEP_CHUNKS = 4


def _ret_specs(n_of, gch):
    cst = lambda shape: pl.BlockSpec(shape, lambda i: (0,) * len(shape))
    return [pl.BlockSpec((CHUNK * gch, RET_IN), lambda i: (n_of(i), 0)), pl.BlockSpec((CHUNK * gch, 1), lambda i: (n_of(i), 0)),
            cst((1, LANES)), cst((1, LANES)), cst((RET_HEADS, CHUNK, CHUNK)), cst((CHUNK, RET_W)), cst((CHUNK, RET_W)),
            cst((1, RET_W)), cst((1, RET_W))]


def ret_fwd(hr, pos, norm_g):
    s = hr.shape[0]
    n_chunks = s // CHUNK
    cs = _ret_consts()
    n_slab = RET_W // LANES

    gch = min(RET_STEP_CHUNKS, n_chunks)

    def body(hr_ref, pos_ref, invf_ref, sgn_ref, intra_ref, cross_ref, tail_ref, dec_ref, g_ref, o_ref, opre_ref, st_ref, state):
        @pl.when(pl.program_id(0) == 0)
        def _():
            state[...] = jnp.zeros_like(state)

        bd = _block_diag_mask()
        sts = [state[LANES * sl:LANES * (sl + 1), :] for sl in range(n_slab)]
        for c in range(gch):
            tok = slice(CHUNK * c, CHUNK * (c + 1))
            ang = pos_ref[tok, :].astype(F32) * invf_ref[...]
            cosv = jnp.cos(ang)
            sinv = jnp.sin(ang) * sgn_ref[...]
            for sl in range(n_slab):
                lanes = slice(LANES * sl, LANES * (sl + 1))
                q = hr_ref[tok, LANES * sl:LANES * (sl + 1)]
                k = hr_ref[tok, RET_W + LANES * sl:RET_W + LANES * (sl + 1)]
                v = hr_ref[tok, 2 * RET_W + LANES * sl:2 * RET_W + LANES * (sl + 1)]
                gate = hr_ref[tok, 3 * RET_W + LANES * sl:3 * RET_W + LANES * (sl + 1)]
                qt = _rope(q, cosv, sinv) * (HEAD ** -0.5)
                kt = _rope(k, cosv, sinv)
                st = sts[sl]
                st_ref[RET_W * c + LANES * sl:RET_W * c + LANES * (sl + 1), :] = st
                o = _mm(qt * cross_ref[:, lanes], st)
                for hd in range(2):
                    m = _head_mask(hd)
                    sc = _mm_nt(qt * m, kt) * intra_ref[2 * sl + hd]
                    o = o + _mm(sc, v) * m
                sts[sl] = st * dec_ref[:, lanes] + _mm_tn(kt, v * tail_ref[:, lanes]) * bd
                opre_ref[tok, lanes] = o
                mu = _group_sum_slab(o) * (1.0 / HEAD)
                dlt = o - mu
                var = _group_sum_slab(dlt * dlt) * (1.0 / HEAD)
                on = dlt * lax.rsqrt(var + 1e-5)
                o_ref[tok, lanes] = on * g_ref[:, lanes] * (gate * _sigmoid(gate))
        for sl in range(n_slab):
            state[LANES * sl:LANES * (sl + 1), :] = sts[sl]

    out_row = lambda w: pl.BlockSpec((CHUNK * gch, w), lambda i: (i, 0))
    return pl.pallas_call(
        body, name="ret_fwd", grid=(n_chunks // gch,),
        in_specs=_ret_specs(lambda i: i, gch),
        out_specs=[out_row(RET_W), out_row(RET_W), pl.BlockSpec((RET_W * gch, LANES), lambda i: (i, 0))],
        out_shape=[jax.ShapeDtypeStruct((s, RET_W), F32), jax.ShapeDtypeStruct((s, RET_W), F32),
                   jax.ShapeDtypeStruct((n_chunks * RET_W, LANES), F32)],
        scratch_shapes=[pltpu.VMEM((RET_W, LANES), F32)],
        compiler_params=_params("arbitrary"),
    )(hr, pos, cs["invf"], cs["sgn"], cs["intra"], cs["cross"], cs["tail"], cs["dec"], norm_g)


def ret_bwd(hr, pos, norm_g, opre, states, dout):
    s = hr.shape[0]
    n_chunks = s // CHUNK
    cs = _ret_consts()
    n_slab = RET_W // LANES
    gch = min(RET_STEP_CHUNKS, n_chunks)
    rev = lambda i: n_chunks // gch - 1 - i

    def body(hr_ref, pos_ref, invf_ref, sgn_ref, intra_ref, cross_ref, tail_ref, dec_ref, g_ref, opre_ref, st_ref, do_ref,
             dh_ref, dg_ref, gstate):
        @pl.when(pl.program_id(0) == 0)
        def _():
            gstate[...] = jnp.zeros_like(gstate)
            dg_ref[...] = jnp.zeros_like(dg_ref)

        bd = _block_diag_mask()
        gss = [gstate[LANES * sl:LANES * (sl + 1), :] for sl in range(n_slab)]
        dgs = [jnp.zeros((1, LANES), F32) for _ in range(n_slab)]
        for c in reversed(range(gch)):
            tok = slice(CHUNK * c, CHUNK * (c + 1))
            ang = pos_ref[tok, :].astype(F32) * invf_ref[...]
            cosv = jnp.cos(ang)
            sinv = jnp.sin(ang) * sgn_ref[...]
            for sl in range(n_slab):
                lanes = slice(LANES * sl, LANES * (sl + 1))
                q = hr_ref[tok, LANES * sl:LANES * (sl + 1)]
                k = hr_ref[tok, RET_W + LANES * sl:RET_W + LANES * (sl + 1)]
                v = hr_ref[tok, 2 * RET_W + LANES * sl:2 * RET_W + LANES * (sl + 1)]
                gate = hr_ref[tok, 3 * RET_W + LANES * sl:3 * RET_W + LANES * (sl + 1)]
                qt = _rope(q, cosv, sinv) * (HEAD ** -0.5)
                kt = _rope(k, cosv, sinv)
                o = opre_ref[tok, lanes]
                mu = _group_sum_slab(o) * (1.0 / HEAD)
                dlt = o - mu
                var = _group_sum_slab(dlt * dlt) * (1.0 / HEAD)
                rstd = lax.rsqrt(var + 1e-5)
                on = dlt * rstd
                sil, dsil = _silu_and_grad(gate)
                dout_v = do_ref[tok, lanes]
                gn = g_ref[:, lanes]
                dgs[sl] = dgs[sl] + jnp.sum(dout_v * on * sil, axis=0, keepdims=True)
                d_on = dout_v * gn * sil
                dgate = dout_v * on * gn * dsil
                d_o = rstd * (d_on - _group_sum_slab(d_on) * (1.0 / HEAD) - on * (_group_sum_slab(d_on * on) * (1.0 / HEAD)))
                st = st_ref[RET_W * c + LANES * sl:RET_W * c + LANES * (sl + 1), :]
                gs = gss[sl]
                cross = cross_ref[:, lanes]
                tail = tail_ref[:, lanes]
                dqt = _mm_nt(d_o, st) * cross
                ds_here = _mm_tn(qt * cross, d_o) * bd
                vt = v * tail
                dkt = _mm_nt(vt, gs)
                dv = _mm(kt, gs) * tail
                for hd in range(2):
                    m = _head_mask(hd)
                    qm = qt * m
                    dom = d_o * m
                    intra = intra_ref[2 * sl + hd]
                    sc = _mm_nt(qm, kt) * intra
                    dsc = _mm_nt(dom, v) * intra
                    dqt = dqt + _mm(dsc, kt) * m
                    dkt = dkt + _mm_tn(dsc, qm)
                    dv = dv + _mm_tn(sc, dom)
                gss[sl] = gs * dec_ref[:, lanes] + ds_here
                dh_ref[tok, LANES * sl:LANES * (sl + 1)] = _rope_transposed(dqt * (HEAD ** -0.5), cosv, sinv).astype(BF16)
                dh_ref[tok, RET_W + LANES * sl:RET_W + LANES * (sl + 1)] = _rope_transposed(dkt, cosv, sinv).astype(BF16)
                dh_ref[tok, 2 * RET_W + LANES * sl:2 * RET_W + LANES * (sl + 1)] = dv.astype(BF16)
                dh_ref[tok, 3 * RET_W + LANES * sl:3 * RET_W + LANES * (sl + 1)] = dgate.astype(BF16)
        for sl in range(n_slab):
            gstate[LANES * sl:LANES * (sl + 1), :] = gss[sl]
            dg_ref[:, LANES * sl:LANES * (sl + 1)] += dgs[sl]

    row = lambda w: pl.BlockSpec((CHUNK * gch, w), lambda i: (rev(i), 0))
    return pl.pallas_call(
        body, name="ret_bwd", grid=(n_chunks // gch,),
        in_specs=_ret_specs(rev, gch) + [row(RET_W), pl.BlockSpec((RET_W * gch, LANES), lambda i: (rev(i), 0)), row(RET_W)],
        out_specs=[row(RET_IN), pl.BlockSpec((1, RET_W), lambda i: (0, 0))],
        out_shape=[jax.ShapeDtypeStruct((s, RET_IN), BF16), jax.ShapeDtypeStruct((1, RET_W), F32)],
        scratch_shapes=[pltpu.VMEM((RET_W, LANES), F32)],
        compiler_params=_params("arbitrary"),
    )(hr, pos, cs["invf"], cs["sgn"], cs["intra"], cs["cross"], cs["tail"], cs["dec"], norm_g, opre, states, dout)


def _lru_gates(xc, wa_ref, ba_ref, wx_ref, bx_ref, lam_ref):
    xcb = xc.astype(BF16)
    r = _sigmoid(_mm(xcb, wa_ref[...].astype(BF16)) + ba_ref[...])
    ig = _sigmoid(_mm(xcb, wx_ref[...].astype(BF16)) + bx_ref[...])
    lam = lam_ref[...]
    ls = jnp.minimum(lam, 0.0) - _log1p(jnp.exp(-jnp.abs(lam)))
    la = (LRU_C * r) * ls
    a = jnp.exp(la)
    mult = jnp.sqrt(-_expm1(2.0 * la))
    return r, ig, ls, a, mult


def _lru_conv(x, xprev, w_ref, b_ref):
    xc = b_ref[...] + w_ref[3:4, :] * x
    for j in (1, 2, 3):
        xc = xc + w_ref[3 - j:4 - j, :] * _shift_rows(x, xprev, j)
    return xc


def lru_fwd(hl, conv_w, conv_b, w_a, b_a, w_x, b_x, lam):
    s = hl.shape[0]
    ts = SCAN_TILE
    w = LRU_W

    def body(hl_ref, hp_ref, cw_ref, cb_ref, wa_ref, ba_ref, wx_ref, bx_ref, lam_ref, o_ref, xc_ref, h_ref, carry):
        i = pl.program_id(0)

        @pl.when(i == 0)
        def _():
            carry[...] = jnp.zeros_like(carry)

        x = hl_ref[:, 0:w]
        gate = hl_ref[:, w:2 * w]
        xprev = hp_ref[...] * (i > 0).astype(F32)
        xc = _lru_conv(x, xprev, cw_ref, cb_ref)
        xc_ref[...] = xc
        _, ig, _, a, mult = _lru_gates(xc, wa_ref, ba_ref, wx_ref, bx_ref, lam_ref)
        b = mult * (ig * xc)
        row = lax.broadcasted_iota(jnp.int32, (ts, w), 0)
        d = 1
        while d < ts:
            ap = jnp.where(row >= d, pltpu.roll(a, d, 0), 1.0)
            bp = jnp.where(row >= d, pltpu.roll(b, d, 0), 0.0)
            b = a * bp + b
            a = a * ap
            d *= 2
        h = b + a * carry[0:1, :]
        h_ref[...] = h
        carry[0:1, :] = h[ts - 1:ts, :]
        o_ref[...] = h * _gelu(gate)

    cst = lambda shape: pl.BlockSpec(shape, lambda i: (0, 0))
    return pl.pallas_call(
        body, name="lru_fwd", grid=(s // ts,),
        in_specs=[_row_spec(ts, 2 * w), pl.BlockSpec((ts, w), lambda i: (jnp.maximum(i - 1, 0), 0)),
                  cst((4, w)), cst((1, w)), cst((w, w)), cst((1, w)), cst((w, w)), cst((1, w)), cst((1, w))],
        out_specs=[_row_spec(ts, w)] * 3,
        out_shape=[jax.ShapeDtypeStruct((s, w), F32)] * 3,
        scratch_shapes=[pltpu.VMEM((8, w), F32)],
        compiler_params=_params("arbitrary"),
    )(hl, hl, conv_w, conv_b, w_a, b_a, w_x, b_x, lam)


def lru_bwd(hl, conv_w, conv_b, w_a, b_a, w_x, b_x, lam, xc_saved, h_saved, dout):
    s = hl.shape[0]
    ts = SCAN_TILE
    w = LRU_W
    nb = s // ts
    rev = lambda i: nb - 1 - i

    def body(hl_ref, hp_ref, cw_ref, cb_ref, wa_ref, ba_ref, wx_ref, bx_ref, lam_ref, xc_ref, h_ref, hprev_ref, do_ref,
             dhl_ref, dcw_ref, dcb_ref, dwa_ref, dba_ref, dwx_ref, dbx_ref, dlam_ref, carry, dxc_next):
        i = pl.program_id(0)
        blk = nb - 1 - i

        @pl.when(i == 0)
        def _():
            carry[...] = jnp.zeros_like(carry)
            dxc_next[...] = jnp.zeros_like(dxc_next)
            for ref in (dcw_ref, dcb_ref, dwa_ref, dba_ref, dwx_ref, dbx_ref, dlam_ref):
                ref[...] = jnp.zeros_like(ref)

        first = (blk > 0).astype(F32)
        x = hl_ref[:, 0:w]
        gate = hl_ref[:, w:2 * w]
        xprev = hp_ref[...] * first
        xc = xc_ref[...]
        h = h_ref[...]
        hprev = hprev_ref[...] * first
        r, ig, ls, a, mult = _lru_gates(xc, wa_ref, ba_ref, wx_ref, bx_ref, lam_ref)
        do = do_ref[...]
        dh = do * _gelu(gate)
        dgate = do * h * _gelu_grad(gate)
        row = lax.broadcasted_iota(jnp.int32, (ts, w), 0)
        ca = jnp.where(row < ts - 1, pltpu.roll(a, ts - 1, 0), 1.0)
        cb = dh
        d = 1
        while d < ts:
            an = jnp.where(row < ts - d, pltpu.roll(ca, ts - d, 0), 1.0)
            bn = jnp.where(row < ts - d, pltpu.roll(cb, ts - d, 0), 0.0)
            cb = cb + ca * bn
            ca = ca * an
            d *= 2
        lamb = cb + ca * carry[0:1, :]
        carry[0:1, :] = a[0:1, :] * lamb[0:1, :]
        h_before = _shift_rows(h, hprev, 1)
        da = lamb * h_before
        ix = ig * xc
        dmult = lamb * ix
        dig = lamb * mult * xc
        dxc = lamb * mult * ig
        dla = (da - dmult * a / mult) * a
        dr = dla * LRU_C * ls
        dlam_ref[...] += jnp.sum(dla * LRU_C * r, axis=0, keepdims=True) * _sigmoid(-lam_ref[...])
        dpa = dr * r * (1.0 - r)
        dpx = dig * ig * (1.0 - ig)
        dba_ref[...] += jnp.sum(dpa, axis=0, keepdims=True)
        dbx_ref[...] += jnp.sum(dpx, axis=0, keepdims=True)
        dpab = dpa.astype(BF16)
        dpxb = dpx.astype(BF16)
        xcb = xc.astype(BF16)
        dxc = dxc + _mm_nt(dpab, wa_ref[...].astype(BF16)) + _mm_nt(dpxb, wx_ref[...].astype(BF16))
        dwa_ref[...] += _mm_tn(xcb, dpab)
        dwx_ref[...] += _mm_tn(xcb, dpxb)
        dcb_ref[...] += jnp.sum(dxc, axis=0, keepdims=True)
        nxt = dxc_next[...]
        dx = cw_ref[3:4, :] * dxc
        dcw_ref[3:4, :] += jnp.sum(dxc * x, axis=0, keepdims=True)
        for j in (1, 2, 3):
            dx = dx + cw_ref[3 - j:4 - j, :] * _shift_rows_up(dxc, nxt, j)
            dcw_ref[3 - j:4 - j, :] += jnp.sum(dxc * _shift_rows(x, xprev, j), axis=0, keepdims=True)
        dxc_next[...] = dxc
        dhl_ref[:, 0:w] = dx.astype(BF16)
        dhl_ref[:, w:2 * w] = dgate.astype(BF16)

    cst = lambda shape: pl.BlockSpec(shape, lambda i: (0, 0))
    rowr = lambda width: pl.BlockSpec((ts, width), lambda i: (rev(i), 0))
    prevr = lambda width: pl.BlockSpec((ts, width), lambda i: (jnp.maximum(rev(i) - 1, 0), 0))
    return pl.pallas_call(
        body, name="lru_bwd", grid=(nb,),
        in_specs=[rowr(2 * w), prevr(w), cst((4, w)), cst((1, w)), cst((w, w)), cst((1, w)), cst((w, w)), cst((1, w)), cst((1, w)),
                  rowr(w), rowr(w), prevr(w), rowr(w)],
        out_specs=[rowr(2 * w), cst((4, w)), cst((1, w)), cst((w, w)), cst((1, w)), cst((w, w)), cst((1, w)), cst((1, w))],
        out_shape=[jax.ShapeDtypeStruct((s, 2 * w), BF16), jax.ShapeDtypeStruct((4, w), F32), jax.ShapeDtypeStruct((1, w), F32),
                   jax.ShapeDtypeStruct((w, w), F32), jax.ShapeDtypeStruct((1, w), F32), jax.ShapeDtypeStruct((w, w), F32),
                   jax.ShapeDtypeStruct((1, w), F32), jax.ShapeDtypeStruct((1, w), F32)],
        scratch_shapes=[pltpu.VMEM((8, w), F32), pltpu.VMEM((ts, w), F32)],
        compiler_params=_params("arbitrary"),
    )(hl, hl, conv_w, conv_b, w_a, b_a, w_x, b_x, lam, xc_saved, h_saved, h_saved, dout)


GDN_QKV = 3 * GDN_W
GDN_STEP_CHUNKS = 4
GDN_BWD_STEP_CHUNKS = 2


def _tri_inverse_many(nms):
    r = lax.broadcasted_iota(jnp.int32, nms[0].shape, 0)
    c = lax.broadcasted_iota(jnp.int32, nms[0].shape, 1)
    eye = (r == c).astype(F32)
    ts = [eye - nm for nm in nms]
    ps = list(nms)
    for _ in range(5):
        ps = [_mm3(p, p) for p in ps]
        ts = [t + _mm3(t, p) for t, p in zip(ts, ps)]
    return ts


def _gdn_front(hx_ref, hprev, cw_ref, al_ref, dt_ref):
    w = GDN_W
    x = hx_ref[:, 0:GDN_QKV]
    y = cw_ref[3:4, :] * x
    for j in (1, 2, 3):
        y = y + cw_ref[3 - j:4 - j, :] * _shift_rows(x, hprev, j)
    qkv, dsil = _silu_and_grad(y)
    q, k, v = qkv[:, 0:w], qkv[:, w:2 * w], qkv[:, 2 * w:3 * w]
    rq = lax.rsqrt(_group_sum(q * q) + 1e-6)
    rk = lax.rsqrt(_group_sum(k * k) + 1e-6)
    beta = _sigmoid(hx_ref[:, 5 * w:6 * w])
    sp_in = hx_ref[:, 4 * w:5 * w] + dt_ref[...]
    neg_a = -jnp.exp(al_ref[...])
    g = neg_a * _softplus(sp_in)
    n_c = g.shape[0] // CHUNK
    gc = jnp.concatenate([_rows_prefix_sum(g[CHUNK * c:CHUNK * (c + 1)]) for c in range(n_c)], axis=0)
    return dict(x=x, dsil=dsil, qn=q * rq, kn=k * rk, v=v, rq=rq, rk=rk, beta=beta, sp_in=sp_in, neg_a=neg_a, g=g, gc=gc)


def _stack_heads(x):
    return jnp.concatenate([x * _head_mask(0), x * _head_mask(1)], axis=0)


def _unstack_heads(y):
    return y[0:CHUNK] + y[CHUNK:2 * CHUNK]


def _head_transpose(x):
    return jnp.concatenate([x[:, 0:HEAD].T, x[:, HEAD:2 * HEAD].T], axis=1)


def _head_total(x):
    cols = jnp.broadcast_to(jnp.sum(x, axis=0, keepdims=True), (8, LANES))
    return _group_sum_slab(cols)[0:1]


def _slab_tri_masks():
    r = lax.broadcasted_iota(jnp.int32, (CHUNK, LANES), 0)
    c = lax.broadcasted_iota(jnp.int32, (CHUNK, LANES), 1) & (HEAD - 1)
    return r >= c, r > c


def _gdn_slab(fr, c, sl, tri, transposed=False):
    lower, strict = tri
    ls = lambda a: a[CHUNK * c:CHUNK * (c + 1), LANES * sl:LANES * (sl + 1)]
    k = ls(fr["kn"])
    q = ls(fr["qn"]) * (HEAD ** -0.5)
    v = ls(fr["v"])
    beta = ls(fr["beta"])
    gc = ls(fr["gc"])
    e = jnp.exp(gc)
    gl = gc[CHUNK - 1:CHUNK, :]
    xt = jnp.exp(gl - gc)
    gc_t = _head_transpose(gc)
    dec = jnp.where(lower, jnp.exp(jnp.minimum(gc - gc_t, 0.0)), 0.0)
    kbd = _stack_heads(k)
    kk = _mm_nt(k, kbd)
    qkr = _mm_nt(q, kbd)
    out = dict(k=k, q=q, v=v, beta=beta, e=e, egl=jnp.exp(gl), xt=xt, dec=dec, kk=kk, qkr=qkr, kbd=kbd,
               nm=jnp.where(strict, beta * kk * dec, 0.0))
    if transposed:
        r = lax.broadcasted_iota(jnp.int32, (CHUNK, LANES), 0)
        col = lax.broadcasted_iota(jnp.int32, (CHUNK, LANES), 1) & (HEAD - 1)
        qbd = _stack_heads(q)
        out.update(dec_t=jnp.where(r <= col, jnp.exp(jnp.minimum(gc_t - gc, 0.0)), 0.0), beta_t=_head_transpose(beta),
                   qbd=qbd, kqr=_mm_nt(k, qbd), strict_t=r < col)
    return out


def gdn_fwd(hx, conv_w, a_log_e, dt_bias_e, norm_g_e, fused=None):
    s = hx.shape[0]
    n_chunks = s // CHUNK
    w = GDN_W
    n_slab = w // LANES
    gch = min(GDN_STEP_CHUNKS, n_chunks)

    def body(hx_ref, hp_ref, cw_ref, al_ref, dt_ref, ng_ref, o_ref, opre_ref, t_ref, st_ref, state):
        n = pl.program_id(0)

        @pl.when(n == 0)
        def _():
            state[...] = jnp.zeros_like(state)

        fr = _gdn_front(hx_ref, hp_ref[...] * (n > 0).astype(F32), cw_ref, al_ref, dt_ref)
        tri = _slab_tri_masks()
        bd = _block_diag_mask()
        sts = [state[LANES * sl:LANES * (sl + 1), :] for sl in range(n_slab)]
        slabs = [[_gdn_slab(fr, c, sl, tri) for sl in range(n_slab)] for c in range(gch)]
        tbd = _tri_inverse_many([_stack_heads(sq["nm"]) for row_ in slabs for sq in row_])
        o_rows = []
        for c in range(gch):
            ts, outs = [], []
            st_ref[w * c:w * (c + 1), :] = jnp.concatenate(sts, axis=0)
            for sl in range(n_slab):
                sq = slabs[c][sl]
                t = _unstack_heads(tbd[n_slab * c + sl])
                ts.append(t)
                u = _mm(t, _stack_heads(sq["v"] * sq["beta"]))
                wk = _mm(t, _stack_heads(sq["k"] * (sq["beta"] * sq["e"])))
                st = sts[sl]
                vnew = u - _mm(wk, st)
                outs.append(_mm(sq["q"] * sq["e"], st) + _mm(sq["qkr"] * sq["dec"], _stack_heads(vnew)))
                sts[sl] = st * sq["egl"] + _mm_tn(sq["k"] * sq["xt"], vnew) * bd
            t_ref[CHUNK * c:CHUNK * (c + 1), :] = jnp.concatenate(ts, axis=1)
            o_rows.append(jnp.concatenate(outs, axis=1))
        state[...] = jnp.concatenate(sts, axis=0)
        o = jnp.concatenate(o_rows, axis=0)
        opre_ref[...] = o
        rinv = lax.rsqrt(_group_sum(o * o) * (1.0 / HEAD) + 1e-6)
        z = hx_ref[:, 3 * w:4 * w]
        o_ref[...] = (o * rinv) * ng_ref[...] * (z * _sigmoid(z))

    cst = lambda shape: pl.BlockSpec(shape, lambda i: (0, 0))
    row = lambda width: pl.BlockSpec((CHUNK * gch, width), lambda i: (i, 0))
    f_in, f_out, f_scr, _ = fused if fused is not None else ([], [], [], None)
    outs = pl.pallas_call(
        _fuse(body, 6, 4, 1, n_chunks // gch, fused), name="gdn_fwd" + ("_gather" if fused is not None else ""),
        grid=(n_chunks // gch,),
        in_specs=[row(GDN_IN), pl.BlockSpec((CHUNK * gch, GDN_QKV), lambda i: (jnp.maximum(i - 1, 0), 0)),
                  cst((4, GDN_QKV)), cst((1, w)), cst((1, w)), cst((1, w))] + [_ANY] * len(f_in),
        out_specs=[row(w)] * 3 + [pl.BlockSpec((w * gch, LANES), lambda i: (i, 0))] + [_ANY] * len(f_out),
        out_shape=[jax.ShapeDtypeStruct((s, w), F32)] * 3 + [jax.ShapeDtypeStruct((n_chunks * w, LANES), F32)] + list(f_out),
        scratch_shapes=[pltpu.VMEM((w, LANES), F32)] + list(f_scr),
        compiler_params=_params("arbitrary"),
    )(hx, hx, conv_w, a_log_e, dt_bias_e, norm_g_e, *f_in)
    return tuple(outs[:4]) + ((list(outs[4:]),) if fused is not None else ())


def gdn_bwd(hx, conv_w, a_log_e, dt_bias_e, norm_g_e, opre, tmat, states, dout, fused=None):
    s = hx.shape[0]
    n_chunks = s // CHUNK
    w = GDN_W
    n_slab = w // LANES
    gch = min(GDN_BWD_STEP_CHUNKS, n_chunks)
    n_blocks = n_chunks // gch
    rev = lambda i: n_blocks - 1 - i

    def body(hx_ref, hp_ref, cw_ref, al_ref, dt_ref, ng_ref, opre_ref, t_ref, st_ref, do_ref,
             dhx_ref, dab_ref, dcw_ref, dal_ref, ddt_ref, dng_ref, dstate, dy_next):
        i = pl.program_id(0)
        n = n_blocks - 1 - i

        @pl.when(i == 0)
        def _():
            dstate[...] = jnp.zeros_like(dstate)
            dy_next[...] = jnp.zeros_like(dy_next)
            for ref in (dcw_ref, dal_ref, ddt_ref, dng_ref):
                ref[...] = jnp.zeros_like(ref)

        hprev = hp_ref[...] * (n > 0).astype(F32)
        fr = _gdn_front(hx_ref, hprev, cw_ref, al_ref, dt_ref)
        tri = _slab_tri_masks()
        lower, strict = tri
        o = opre_ref[...]
        rinv = lax.rsqrt(_group_sum(o * o) * (1.0 / HEAD) + 1e-6)
        yn = o * rinv
        z = hx_ref[:, 3 * w:4 * w]
        sil, dsil_z = _silu_and_grad(z)
        dout_v = do_ref[...]
        ng = ng_ref[...]
        dng_ref[...] += jnp.sum(dout_v * yn * sil, axis=0, keepdims=True)
        dz = dout_v * yn * ng * dsil_z
        dyn = dout_v * ng * sil
        d_o = rinv * (dyn - yn * (_group_sum(dyn * yn) * (1.0 / HEAD)))
        last_row = (lax.broadcasted_iota(jnp.int32, (CHUNK, LANES), 0) == CHUNK - 1).astype(F32)
        bd = _block_diag_mask()
        gsum = _group_sum_slab
        t_all, st_all = t_ref[...], st_ref[...]
        dsns = [dstate[LANES * sl:LANES * (sl + 1), :] for sl in range(n_slab)]
        per_chunk = {}
        order = [(c_, s_) for c_ in reversed(range(gch)) for s_ in range(n_slab)]
        chain = {}
        for c, sl in order:
            lanes = slice(LANES * sl, LANES * (sl + 1))
            tok = slice(CHUNK * c, CHUNK * (c + 1))
            sq = _gdn_slab(fr, c, sl, tri, transposed=True)
            t = t_all[tok, lanes]
            st = st_all[w * c + LANES * sl:w * c + LANES * (sl + 1), :]
            dsn = dsns[sl]
            do_s = d_o[tok, lanes]
            u = _mm(t, _stack_heads(sq["v"] * sq["beta"]))
            wk = _mm(t, _stack_heads(sq["k"] * (sq["beta"] * sq["e"])))
            kt = sq["k"] * sq["xt"]
            dobd = _stack_heads(do_s)
            dvnew = _mm(sq["kqr"] * sq["dec_t"], dobd) + _mm(kt, dsn)
            dsns[sl] = _mm_tn(sq["q"] * sq["e"], do_s) * bd + sq["egl"] * dsn - _mm_tn(wk, dvnew) * bd
            chain[(c, sl)] = (sq, t, st, dsn, do_s, dobd, u, wk, kt, dvnew)
        for c, sl in order:
            sq, t, st, dsn, do_s, dobd, u, wk, kt, dvnew = chain[(c, sl)]
            k, q, v, beta, e, xt, dec, kk, qkr, kbd = (sq[n_] for n_ in ("k", "q", "v", "beta", "e", "xt", "dec", "kk", "qkr", "kbd"))
            dec_t, beta_t, kqr, qbd = sq["dec_t"], sq["beta_t"], sq["kqr"], sq["qbd"]
            t_t = _head_transpose(t)
            vnew = u - _mm(wk, st)
            dqd = _mm_nt(do_s, st)
            dqk = _mm_nt(do_s, _stack_heads(vnew))
            dqk_t = _mm_nt(vnew, dobd)
            dkt = _mm_nt(vnew, dsn)
            dgl = _head_total(dsn * st) * sq["egl"]
            dwk = -_mm_nt(dvnew, st)
            drv = _mm(t_t, _stack_heads(dvnew))
            drk = _mm(t_t, _stack_heads(dwk))
            dnm = jnp.where(strict, -(_mm_nt(drv, _stack_heads(u)) + _mm_nt(drk, _stack_heads(wk))), 0.0)
            dnm_t = jnp.where(sq["strict_t"], -(_mm_nt(u, _stack_heads(drv)) + _mm_nt(wk, _stack_heads(drk))), 0.0)
            dbeta = gsum(dnm * kk * dec)
            dkk = dnm * beta * dec
            dkk_t = dnm_t * beta_t * dec_t
            ddec = dnm * beta * kk + dqk * qkr
            dd_t = (dnm_t * beta_t * kk + dqk_t * kqr) * dec_t
            dq = _mm(dqk * dec, kbd) + dqd * e
            dk = _mm(dqk_t * dec_t, qbd) + _mm(dkk + dkk_t, kbd) + drk * (beta * e) + dkt * xt
            rks = gsum(drk * k)
            dbeta = dbeta + gsum(drv * v) + rks * e
            de = rks * beta + gsum(dqd * q)
            dxt = gsum(dkt * k) * xt
            dgl = dgl + jnp.sum(dxt, axis=0, keepdims=True)
            dgc = de * e - dxt + gsum(ddec * dec) - gsum(dd_t) + last_row * dgl
            per_chunk[(c, sl)] = dict(dq=dq * (HEAD ** -0.5), dk=dk, dv=drv * beta, dbeta=dbeta, dgc=dgc)
        for sl in range(n_slab):
            dstate[LANES * sl:LANES * (sl + 1), :] = dsns[sl]

        def block_of(name, suffix_sum=False):
            rows = []
            for c in range(gch):
                r = jnp.concatenate([per_chunk[(c, sl)][name] for sl in range(n_slab)], axis=1)
                rows.append(_rows_suffix_sum(r) if suffix_sum else r)
            return jnp.concatenate(rows, axis=0)

        dg = block_of("dgc", suffix_sum=True)
        dal_ref[...] += jnp.sum(dg * fr["g"], axis=0, keepdims=True)
        da = dg * fr["neg_a"] * _sigmoid(fr["sp_in"])
        ddt_ref[...] += jnp.sum(da, axis=0, keepdims=True)
        beta_all = fr["beta"]
        db = block_of("dbeta") * beta_all * (1.0 - beta_all)
        lane = lax.broadcasted_iota(jnp.int32, (CHUNK * gch, LANES), 1)
        dab = jnp.zeros((CHUNK * gch, LANES), F32)
        for hd in range(GDN_HEADS):
            dab = jnp.where(lane == hd, da[:, HEAD * hd:HEAD * hd + 1], dab)
            dab = jnp.where(lane == GDN_HEADS + hd, db[:, HEAD * hd:HEAD * hd + 1], dab)
        dab_ref[...] = dab.astype(BF16)
        dqn = block_of("dq")
        dkn = block_of("dk")
        dq_raw = fr["rq"] * (dqn - fr["qn"] * _group_sum(dqn * fr["qn"]))
        dk_raw = fr["rk"] * (dkn - fr["kn"] * _group_sum(dkn * fr["kn"]))
        dy = jnp.concatenate([dq_raw, dk_raw, block_of("dv")], axis=1) * fr["dsil"]
        nxt = dy_next[...]
        x = fr["x"]
        dx = cw_ref[3:4, :] * dy
        dcw_ref[3:4, :] += jnp.sum(dy * x, axis=0, keepdims=True)
        for j in (1, 2, 3):
            dx = dx + cw_ref[3 - j:4 - j, :] * _shift_rows_up(dy, nxt, j)
            dcw_ref[3 - j:4 - j, :] += jnp.sum(dy * _shift_rows(x, hprev, j), axis=0, keepdims=True)
        dy_next[...] = dy
        dhx_ref[:, 0:GDN_QKV] = dx.astype(BF16)
        dhx_ref[:, 3 * w:4 * w] = dz.astype(BF16)

    cst = lambda shape: pl.BlockSpec(shape, lambda i: (0, 0))
    row = lambda width: pl.BlockSpec((CHUNK * gch, width), lambda i: (rev(i), 0))
    buf = lambda width: pltpu.VMEM((CHUNK * gch, width), F32)
    f_in, f_out, f_scr, _ = fused if fused is not None else ([], [], [], None)
    outs = pl.pallas_call(
        _fuse(body, 10, 6, 2, n_blocks, fused), name="gdn_bwd" + ("_exchange" if fused is not None else ""), grid=(n_blocks,),
        in_specs=[row(GDN_IN), pl.BlockSpec((CHUNK * gch, GDN_QKV), lambda i: (jnp.maximum(rev(i) - 1, 0), 0)),
                  cst((4, GDN_QKV)), cst((1, w)), cst((1, w)), cst((1, w)), row(w), row(w),
                  pl.BlockSpec((w * gch, LANES), lambda i: (rev(i), 0)), row(w)] + [_ANY] * len(f_in),
        out_specs=[row(4 * w), row(LANES), cst((4, GDN_QKV)), cst((1, w)), cst((1, w)), cst((1, w))] + [_ANY] * len(f_out),
        out_shape=[jax.ShapeDtypeStruct((s, 4 * w), BF16), jax.ShapeDtypeStruct((s, LANES), BF16),
                   jax.ShapeDtypeStruct((4, GDN_QKV), F32),
                   jax.ShapeDtypeStruct((1, w), F32), jax.ShapeDtypeStruct((1, w), F32), jax.ShapeDtypeStruct((1, w), F32)]
        + list(f_out),
        scratch_shapes=[pltpu.VMEM((w, LANES), F32), buf(GDN_QKV)] + list(f_scr),
        compiler_params=_params("arbitrary"),
    )(hx, hx, conv_w, a_log_e, dt_bias_e, norm_g_e, opre, tmat, states, dout, *f_in)
    return tuple(outs[:6]) + ((list(outs[6:]),) if fused is not None else ())


_MESH = pl.DeviceIdType.MESH


def all_gather8(x, name):
    m, n = x.shape

    def body(x_ref, out_ref, send_sems, recv_sems, local_sem):
        px, py, pc = lax.axis_index("x"), lax.axis_index("y"), lax.axis_index("c")
        me, sibling = (px, py, pc), (px, py, 1 - pc)
        chips = [(1 - px, py), (px, 1 - py), (1 - px, 1 - py)]

        def slot(dx, dy, dc):
            return out_ref.at[4 * dx + 2 * dy + dc]

        def copy(k, block, to, src=None):
            return pltpu.make_async_remote_copy(
                src_ref=slot(*block) if src is None else src, dst_ref=slot(*block),
                send_sem=send_sems.at[k], recv_sem=recv_sems.at[k], device_id=to, device_id_type=_MESH)

        mine = pltpu.make_async_copy(x_ref, slot(*me), local_sem)
        mine.start()
        first = [copy(0, me, sibling, src=x_ref)]
        first += [copy(1 + j, me, (*chip, pc), src=x_ref) for j, chip in enumerate(chips)]
        for cp in first:
            cp.start()
        passed = [copy(4 + j, (*chip, pc), sibling) for j, chip in enumerate(chips)]
        for j, chip in enumerate(chips):
            copy(1 + j, (*chip, pc), me).wait_recv()
            passed[j].start()
        copy(0, sibling, me).wait_recv()
        for j, chip in enumerate(chips):
            copy(4 + j, (*chip, 1 - pc), me).wait_recv()
        for cp in first + passed:
            cp.wait_send()
        mine.wait()

    return pl.pallas_call(
        body, name=name, out_shape=jax.ShapeDtypeStruct((N_DEV, m, n), x.dtype),
        in_specs=[_ANY], out_specs=_ANY,
        scratch_shapes=[pltpu.SemaphoreType.DMA((7,)), pltpu.SemaphoreType.DMA((7,)), pltpu.SemaphoreType.DMA],
    )(x)


def _weight_gather_steps(s0, s1, f0, f1, sems):
    n0 = len(s0)
    own_send, own_recv, ici_send, ici_recv, fwd_send, fwd_recv = sems
    px, py, pc = lax.axis_index("x"), lax.axis_index("y"), lax.axis_index("c")
    mine = 2 * px + py
    sibling = (px, py, 1 - pc)
    chips = [(1 - px, py), (px, 1 - py), (1 - px, 1 - py)]

    def copy(src, dst, sems_s, sems_r, k, to):
        return pltpu.make_async_remote_copy(src_ref=src, dst_ref=dst, send_sem=sems_s.at[k], recv_sem=sems_r.at[k],
                                            device_id=to, device_id_type=_MESH)

    def own_copies():
        return [copy(shards[i], full[i].at[mine], own_send, own_recv, base + i, sibling)
                for base, shards, full in ((0, s0, f0), (n0, s1, f1)) for i in range(len(shards))]

    def first_copies(my_shards, my_full):
        ici = [copy(my_shards[i], my_full[i].at[mine], ici_send, ici_recv, 3 * i + j, (cx, cy, pc))
               for i in range(len(my_shards)) for j, (cx, cy) in enumerate(chips)]
        return own_copies() + ici

    def begin(my_shards, my_full):
        for cp in first_copies(my_shards, my_full):
            cp.start()

    def forwards(my_shards, my_full):
        return [copy(my_full[i].at[2 * cx + cy], my_full[i].at[2 * cx + cy], fwd_send, fwd_recv, 3 * i + j, sibling)
                for i in range(len(my_shards)) for j, (cx, cy) in enumerate(chips)]

    def hand_on(my_shards, my_full):
        fwd = forwards(my_shards, my_full)
        for i in range(len(my_shards)):
            for j, (cx, cy) in enumerate(chips):
                copy(my_shards[i], my_full[i].at[2 * cx + cy], ici_send, ici_recv, 3 * i + j, (cx, cy, pc)).wait_recv()
                fwd[3 * i + j].start()

    def end(my_shards, my_full, other_full):
        for cp in own_copies():
            cp.wait_recv()
        for i in range(len(other_full)):
            for j, (cx, cy) in enumerate(chips):
                slot = other_full[i].at[2 * cx + cy]
                copy(slot, slot, fwd_send, fwd_recv, 3 * i + j, sibling).wait_recv()
        for cp in first_copies(my_shards, my_full) + forwards(my_shards, my_full):
            cp.wait_send()

    def on_core(fn0, fn1):
        def run():
            @pl.when(pc == 0)
            def _():
                fn0()

            @pl.when(pc == 1)
            def _():
                fn1()
        return run

    start = on_core(lambda: begin(s0, f0), lambda: begin(s1, f1))
    middle = on_core(lambda: hand_on(s0, f0), lambda: hand_on(s1, f1))
    finish = on_core(lambda: end(s0, f0, f1), lambda: end(s1, f1, f0))
    return start, middle, finish


def _weight_gather_operands(shards0, shards1):
    both, most = len(shards0) + len(shards1), max(len(shards0), len(shards1))
    full = [jax.ShapeDtypeStruct((N_CHIPS,) + v.shape, v.dtype) for v in list(shards0) + list(shards1)]
    dma = pltpu.SemaphoreType.DMA
    return full, [dma((both,)), dma((both,)), dma((3 * most,)), dma((3 * most,)), dma((3 * most,)), dma((3 * most,))]


def _weight_gather_fused(shards0, shards1):
    n0, both = len(shards0), len(shards0) + len(shards1)
    full, sems = _weight_gather_operands(shards0, shards1)
    steps = lambda ins, outs, scr: _weight_gather_steps(ins[:n0], ins[n0:both], outs[:n0], outs[n0:both], scr)
    return list(shards0) + list(shards1), full, sems, steps


def gather_layer_weights(shards0, shards1, name):
    ins, full, sems, steps = _weight_gather_fused(shards0, shards1)
    both = len(ins)

    def body(*refs):
        start, middle, finish = steps(refs[0:both], refs[both:2 * both], refs[2 * both:])
        start()
        middle()
        finish()

    return pl.pallas_call(
        body, name=name, out_shape=full, in_specs=[_ANY] * both, out_specs=[_ANY] * both, scratch_shapes=sems,
    )(*ins)


def _piece_offsets(pieces):
    offs = [0]
    for r, _ in pieces:
        offs.append(offs[-1] + r)
    return offs


def _chip_exchange_steps(srcs, q_ref, sems, pieces, owner):
    send_sems, recv_sems = sems
    offs = _piece_offsets(pieces)
    px, py, pc = lax.axis_index("x"), lax.axis_index("y"), lax.axis_index("c")
    mine = 2 * px + py
    chips = [(1 - px, py), (px, 1 - py), (1 - px, 1 - py)]

    def copies():
        sends = []
        for i, (r, stride) in enumerate(pieces):
            dst = pl.ds(offs[i], r)
            for j, (cx, cy) in enumerate(chips):
                sends.append(pltpu.make_async_remote_copy(
                    src_ref=srcs[i].at[pl.ds((2 * cx + cy) * stride, r)], dst_ref=q_ref.at[mine, dst],
                    send_sem=send_sems.at[3 * i + j], recv_sem=recv_sems.at[3 * i + j], device_id=(cx, cy, pc),
                    device_id_type=_MESH))
        return sends

    def start():
        @pl.when(pc == owner)
        def _():
            for cp in copies():
                cp.start()

    def finish():
        @pl.when(pc == owner)
        def _():
            for i, (r, stride) in enumerate(pieces):
                dst = pl.ds(offs[i], r)
                for j, (cx, cy) in enumerate(chips):
                    pltpu.make_async_remote_copy(
                        src_ref=srcs[i].at[pl.ds(mine * stride, r)], dst_ref=q_ref.at[2 * cx + cy, dst],
                        send_sem=send_sems.at[3 * i + j], recv_sem=recv_sems.at[3 * i + j], device_id=(cx, cy, pc),
                        device_id_type=_MESH).wait_recv()
            for cp in copies():
                cp.wait_send()

    return start, finish


def _chip_exchange_operands(arrays, pieces):
    n = len(pieces)
    dma = pltpu.SemaphoreType.DMA
    q = jax.ShapeDtypeStruct((N_CHIPS, _piece_offsets(pieces)[-1], arrays[0].shape[1]), arrays[0].dtype)
    return q, [dma((3 * n,)), dma((3 * n,))]


def _own_share(arrays, pieces, chip):
    return jnp.concatenate([lax.dynamic_slice_in_dim(arr, chip * stride, r, axis=0) for arr, (r, stride) in zip(arrays, pieces)],
                           axis=0)


def chip_exchange(arrays, pieces, owner, name):
    n = len(pieces)

    def body(*refs):
        start, finish = _chip_exchange_steps(refs[0:n], refs[n], refs[n + 1:], pieces, owner)
        start()
        finish()

    q, sems = _chip_exchange_operands(arrays, pieces)
    return pl.pallas_call(body, name=name, out_shape=q, in_specs=[_ANY] * n, out_specs=_ANY, scratch_shapes=sems)(*arrays)


def sibling_send(arrays, to_core, name):
    n = len(arrays)

    def body(*refs):
        srcs, outs = refs[0:n], refs[n:2 * n]
        send_sems, recv_sems = refs[2 * n:]
        px, py, pc = lax.axis_index("x"), lax.axis_index("y"), lax.axis_index("c")
        cps = [pltpu.make_async_remote_copy(
            src_ref=srcs[i], dst_ref=outs[i], send_sem=send_sems.at[i], recv_sem=recv_sems.at[i],
            device_id=(px, py, to_core), device_id_type=_MESH) for i in range(n)]

        @pl.when(pc != to_core)
        def _():
            for cp in cps:
                cp.start()
            for cp in cps:
                cp.wait_send()

        @pl.when(pc == to_core)
        def _():
            for cp in cps:
                cp.wait_recv()

    return pl.pallas_call(
        body, name=name, out_shape=[jax.ShapeDtypeStruct(v.shape, v.dtype) for v in arrays],
        in_specs=[_ANY] * n, out_specs=[_ANY] * n,
        scratch_shapes=[pltpu.SemaphoreType.DMA((n,)), pltpu.SemaphoreType.DMA((n,))],
    )(*arrays)


def _fuse(body, n_in, n_out, n_scratch, n_steps, fused):
    if fused is None:
        return body
    f_in, f_out, f_scr, steps = fused
    a, b, c = len(f_in), len(f_out), len(f_scr)
    late = (n_steps * FUSED_LATE_PERCENT) // 100

    def wrapped(*refs):
        ins, rest = refs[:n_in + a], refs[n_in + a:]
        outs, scr = rest[:n_out + b], rest[n_out + b:]
        start, middle, finish = steps(ins[n_in:], outs[n_out:], scr[n_scratch:])
        step = pl.program_id(0)

        @pl.when(step == 0)
        def _():
            start()

        body(*ins[:n_in], *outs[:n_out], *scr[:n_scratch])

        @pl.when(step == late)
        def _():
            middle()

        @pl.when(step == n_steps - 1)
        def _():
            finish()

    return wrapped


def sibling_swap(x, name):
    def body(x_ref, out_ref, send_sem, recv_sem):
        px, py, pc = lax.axis_index("x"), lax.axis_index("y"), lax.axis_index("c")
        cp = pltpu.make_async_remote_copy(
            src_ref=x_ref, dst_ref=out_ref, send_sem=send_sem, recv_sem=recv_sem,
            device_id=(px, py, 1 - pc), device_id_type=_MESH)
        cp.start()
        cp.wait()

    return pl.pallas_call(
        body, name=name, out_shape=jax.ShapeDtypeStruct(x.shape, x.dtype), in_specs=[_ANY], out_specs=_ANY,
        scratch_shapes=[pltpu.SemaphoreType.DMA, pltpu.SemaphoreType.DMA],
    )(x)


ELT_TILE = 128


def _elt_rows(m, big=False):
    for t in ((1728, 1408, 1024, 640) if big else ()) + (512, 256, ELT_TILE, 16, 8):
        if m % t == 0:
            return t
    return m


def pair_add(a, b, name):
    m, n = b.shape
    tm = _elt_rows(m, big=True)

    def body(a_ref, b_ref, o_ref):
        o_ref[...] = (a_ref[...].astype(F32) + b_ref[...].astype(F32)).astype(o_ref.dtype)

    return pl.pallas_call(
        body, name=name, grid=(m // tm,), in_specs=[_row_spec(tm, n)] * 2, out_specs=_row_spec(tm, n),
        out_shape=jax.ShapeDtypeStruct((m, n), b.dtype), compiler_params=_params("arbitrary"),
    )(a, b)


def sum_leading(q, name):
    kk, m, n = q.shape
    tm = _elt_rows(m)

    def body(q_ref, o_ref):
        acc = q_ref[0].astype(F32)
        for i in range(1, kk):
            acc = acc + q_ref[i].astype(F32)
        o_ref[...] = acc

    return pl.pallas_call(
        body, name=name, grid=(m // tm,), in_specs=[pl.BlockSpec((kk, tm, n), lambda i: (0, i, 0))],
        out_specs=_row_spec(tm, n), out_shape=jax.ShapeDtypeStruct((m, n), F32), compiler_params=_params("arbitrary"),
    )(q)


def sum_shares(q0, q1, own0, own1, name):
    _, m, n = q0.shape
    tm = next((t for t in (640,) if m % t == 0), _elt_rows(m))

    def body(q0_ref, q1_ref, o0_ref, o1_ref, out_ref):
        first = lax.axis_index("c") == 0
        mine = 2 * lax.axis_index("x") + lax.axis_index("y")
        own = jnp.where(first, o0_ref[...], o1_ref[...])
        acc = None
        for k in range(N_CHIPS):
            term = jnp.where(mine == k, own, jnp.where(first, q0_ref[k], q1_ref[k])).astype(F32)
            acc = term if acc is None else acc + term
        out_ref[...] = acc

    slots = pl.BlockSpec((N_CHIPS, tm, n), lambda i: (0, i, 0))
    return pl.pallas_call(
        body, name=name, grid=(m // tm,), in_specs=[slots, slots, _row_spec(tm, n), _row_spec(tm, n)],
        out_specs=_row_spec(tm, n), out_shape=jax.ShapeDtypeStruct((m, n), F32), compiler_params=_params("arbitrary"),
    )(q0, q1, own0, own1)


def adamw(w, g, m, v, name):
    rows, cols = w.shape
    tm = _elt_rows(rows)

    def body(w_ref, g_ref, m_ref, v_ref, d_ref, nm_ref, nv_ref):
        gv = g_ref[...]
        nm = ADAM_B1 * m_ref[...] + (1.0 - ADAM_B1) * gv
        nv = ADAM_B2 * v_ref[...] + (1.0 - ADAM_B2) * jnp.square(gv)
        nm_ref[...] = nm
        nv_ref[...] = nv
        m_hat = nm / (1.0 - ADAM_B1 ** ADAM_STEP)
        v_hat = nv / (1.0 - ADAM_B2 ** ADAM_STEP)
        d_ref[...] = -ADAM_LR * (m_hat / (jnp.sqrt(v_hat) + ADAM_EPS) + ADAM_WD * w_ref[...])

    spec = _row_spec(tm, cols)
    return pl.pallas_call(
        body, name=name, grid=(rows // tm,), in_specs=[spec] * 4, out_specs=[spec] * 3,
        out_shape=[jax.ShapeDtypeStruct((rows, cols), F32)] * 3, compiler_params=_params("arbitrary"),
    )(w, g, m, v)


def _block_diag_dense(w):
    g = w.shape[0]
    return jnp.einsum("gij,gh->gihj", w, jnp.eye(g, dtype=w.dtype)).reshape(g * w.shape[1], g * w.shape[2])


def _diag_blocks(m):
    return jnp.stack([m[HEAD * i:HEAD * (i + 1), HEAD * i:HEAD * (i + 1)] for i in range(LRU_BLOCKS)])


def _rep(v):
    return jnp.repeat(v, HEAD, axis=-1)


def _split_w_in(w_in):
    gdn0 = RET_IN + LRU_IN
    gdn1 = gdn0 + 4 * GDN_W
    w_r = w_in[:, 0:RET_IN]
    w_l = w_in[:, RET_IN:gdn0]
    w_g = jnp.concatenate([w_in[:, gdn0:gdn1], _rep(w_in[:, gdn1:gdn1 + GDN_HEADS]), _rep(w_in[:, gdn1 + GDN_HEADS:])], axis=1)
    w_ab = jnp.pad(w_in[:, gdn1:], ((0, 0), (0, LANES - 2 * GDN_HEADS)))
    return w_r, w_l, w_g, w_ab


WIN_SHARD = D_IN // N_CHIPS
WIN_STRIDE = 832
WIN_ROWS = 960
WIN_T_ROWS = WIN_STRIDE * (N_CHIPS - 1) + WIN_ROWS
AB_ROWS = 16

_GRAD_PIECES = (("ffn1_w_gate", 704, 704), ("ffn1_w_up", 704, 704), ("ffn1_w_down", 704, 704), ("w_in", WIN_ROWS, WIN_STRIDE),
                ("w_out", 256, 256), ("ffn2_w_gate", 704, 704), ("ffn2_w_up", 704, 704), ("ffn2_w_down", 704, 704),
                ("ple_w_gate", 256, 256), ("ple_w_proj", 64, 64))
_TRANSPOSED = ("ffn1_w_gate", "ffn1_w_up", "w_in", "ffn2_w_gate", "ffn2_w_up", "ple_w_proj")
FUSED_LATE_PERCENT = 88
N_EARLY = 5


def _local_step(x, p, pos, target, wt, mesh=None):
    row = lambda v: v[None, :]
    saved = []
    xb = x.astype(BF16)
    pieces = [(r, stride) for _, r, stride in _GRAD_PIECES]
    grad_names = [n for n, _, _ in _GRAD_PIECES]
    n_late = len(pieces) - N_EARLY
    for i in range(DEPTH):
        ffn1 = (wt["ffn1_w_gate"][i], wt["ffn1_w_up"][i], wt["ffn1_w_down"][i])
        if mesh is not None and i == 0:
            half0, half1, make = mesh["rest0"]
            hg1, hu1, r1, x1, x1b, gathered = ffn_fwd(x, row(wt["ln_ffn1_g"][i]), row(wt["ln_ffn1_b"][i]), *ffn1,
                                                      fused=_weight_gather_fused(half0, half1))
            wt = {**wt, **{n: [w0, None] for n, w0 in make(gathered).items()}}
        else:
            hg1, hu1, r1, x1, x1b = ffn_fwd(x, row(wt["ln_ffn1_g"][i]), row(wt["ln_ffn1_b"][i]), *ffn1)
        w_r, w_l, w_g, w_ab = _split_w_in(wt["w_in"][i])
        lw = dict(
            wg1=ffn1[0], wu1=ffn1[1], wd1=ffn1[2], w_r=w_r, w_l=w_l, w_g=w_g, w_ab=w_ab,
            w_out=wt["w_out"][i], wg2=wt["ffn2_w_gate"][i], wu2=wt["ffn2_w_up"][i], wd2=wt["ffn2_w_down"][i],
            wpg=wt["ple_w_gate"][i], wpp=wt["ple_w_proj"][i],
            wa=_block_diag_dense(wt["lru_w_a"][i]), wx=_block_diag_dense(wt["lru_w_x"][i]),
            al=row(_rep(wt["gdn_a_log"][i])), dt=row(_rep(wt["gdn_dt_bias"][i])), ng=row(jnp.tile(wt["gdn_norm_g"][i], GDN_HEADS)))
        hr, hl, hgd = win_fwd(x1, w_r, w_l, w_g)
        o_r, opre_r, st_r = ret_fwd(hr, pos, row(wt["ret_norm_g"][i]))
        o_l, xc, hs = lru_fwd(hl, wt["lru_conv_w"][i], row(wt["lru_conv_b"][i]), lw["wa"], row(wt["lru_b_a"][i]), lw["wx"],
                              row(wt["lru_b_x"][i]), row(wt["lru_lambda"][i]))
        if mesh is not None and i == 0:
            half0, half1, make = mesh["layer1"]
            o_g, opre_g, tmat, st_g, gathered = gdn_fwd(hgd, wt["gdn_conv_w"][i], lw["al"], lw["dt"], lw["ng"],
                                                        fused=_weight_gather_fused(half0, half1))
            wt = {**wt, **{n: [wt[n][0], w1] for n, w1 in make(gathered).items()}}
        else:
            o_g, opre_g, tmat, st_g = gdn_fwd(hgd, wt["gdn_conv_w"][i], lw["al"], lw["dt"], lw["ng"])
        r2, x2, x2b, ocat = out_fwd(o_r, o_l, o_g, x1, lw["w_out"], row(wt["ln_mix_g"][i]), row(wt["ln_mix_b"][i]))
        hg2, hu2, r3, x3, x3b, pg, pp = ffn_fwd(x2, row(wt["ln_ffn2_g"][i]), row(wt["ln_ffn2_b"][i]), lw["wg2"], lw["wu2"],
                                                lw["wd2"], ple=(p[i], lw["wpg"], lw["wpp"]))
        saved.append(dict(lw=lw, x0=xb, hg1=hg1, hu1=hu1, r1=r1, x1=x1b, hr=hr, hl=hl, hgd=hgd, ocat=ocat, opre_r=opre_r,
                          st_r=st_r, xc=xc, hs=hs, opre_g=opre_g, tmat=tmat, st_g=st_g, r2=r2, x2=x2b, hg2=hg2, hu2=hu2,
                          r3=r3, pg=pg, pp=pp))
        x, xb = x3, x3b

    dx, loss = loss_and_grad(x, target)
    grads = [None] * DEPTH
    big = [None] * DEPTH
    pair_sums = [None] * DEPTH
    for i in reversed(range(DEPTH)):
        sv = saved[i]
        lw = sv["lw"]
        tag = f"_l{i}"
        dx2, act2, dhg2, dhu2, dy2, dg3, db3, dpg, dpp = ffn_bwd(
            dx, sv["r3"], sv["x2"], sv["hg2"], sv["hu2"], row(wt["ln_ffn2_g"][i]), lw["wg2"], lw["wu2"], lw["wd2"],
            ple=(sv["pg"], sv["pp"], lw["wpg"]))
        g, bg = {}, {}
        bg["ffn2_w_gate"] = wgrad(dhg2, sv["x2"], "wgrad_gate2" + tag)
        bg["ffn2_w_up"] = wgrad(dhu2, sv["x2"], "wgrad_up2" + tag)
        bg["ffn2_w_down"] = wgrad(act2, dy2, "wgrad_down2" + tag)
        bg["ple_w_gate"] = wgrad(sv["x2"], dpg, "wgrad_pgate" + tag)
        bg["ple_w_proj"] = wgrad(dpp, p[i], "wgrad_pproj" + tag).reshape(PLE_DIM, D_MODEL)
        g["ln_ffn2_g"], g["ln_ffn2_b"] = dg3[0], db3[0]
        dr2, dr2b, do_r, do_l, do_g, dg2, db2 = out_bwd(dx2, sv["r2"], row(wt["ln_mix_g"][i]), lw["w_out"])
        g["ln_mix_g"], g["ln_mix_b"] = dg2[0], db2[0]
        bg["w_out"] = wgrad(sv["ocat"], dr2b, "wgrad_out" + tag)
        dhr, dgn = ret_bwd(sv["hr"], pos, row(wt["ret_norm_g"][i]), sv["opre_r"], sv["st_r"], do_r)
        g["ret_norm_g"] = dgn[0]
        dhl, dcw, dcb, dwa, dba, dwx, dbx, dlam = lru_bwd(
            sv["hl"], wt["lru_conv_w"][i], row(wt["lru_conv_b"][i]), lw["wa"], row(wt["lru_b_a"][i]), lw["wx"],
            row(wt["lru_b_x"][i]), row(wt["lru_lambda"][i]), sv["xc"], sv["hs"], do_l)
        g["lru_conv_w"], g["lru_conv_b"] = dcw, dcb[0]
        g["lru_w_a"], g["lru_b_a"], g["lru_w_x"], g["lru_b_x"], g["lru_lambda"] = _diag_blocks(dwa), dba[0], _diag_blocks(dwx), dbx[0], dlam[0]
        if mesh is not None and i == 0:
            early = [bg[n] for n in grad_names[n_late:]]
            early_sums = [pair_add(u, v, "reduce_pair_add_l0_" + n) for n, u, v in
                          zip(grad_names[n_late:], early, sibling_send(early, 0, "reduce_pair_send_l0_early"))]
            q1_shape, sems1 = _chip_exchange_operands(pair_sums[1], pieces)
            q0_shape, sems0 = _chip_exchange_operands(early_sums, pieces[n_late:])
            n1 = len(pieces)

            def steps(ins, outs, scr):
                start1, finish1 = _chip_exchange_steps(ins[:n1], outs[0], scr[:2], pieces, 1)
                start0, finish0 = _chip_exchange_steps(ins[n1:], outs[1], scr[2:], pieces[n_late:], 0)
                return (lambda: (start1(), start0())), (lambda: None), (lambda: (finish1(), finish0()))

            dhq, dab, dgcw, dal, ddt, dng, arrived = gdn_bwd(
                sv["hgd"], wt["gdn_conv_w"][i], lw["al"], lw["dt"], lw["ng"], sv["opre_g"], sv["tmat"], sv["st_g"], do_g,
                fused=(pair_sums[1] + early_sums, [q1_shape, q0_shape], sems1 + sems0, steps))
            big[1], early_arrived = arrived
        else:
            dhq, dab, dgcw, dal, ddt, dng = gdn_bwd(sv["hgd"], wt["gdn_conv_w"][i], lw["al"], lw["dt"], lw["ng"], sv["opre_g"],
                                                    sv["tmat"], sv["st_g"], do_g)
        g["gdn_conv_w"] = dgcw
        g["gdn_a_log"], g["gdn_dt_bias"] = dal[0, ::HEAD], ddt[0, ::HEAD]
        g["gdn_norm_g"] = dng[0].reshape(GDN_HEADS, HEAD).sum(0)
        dx1 = win_bwd(dr2, dhr, dhl, dhq, dab, lw["w_r"], lw["w_l"], lw["w_g"], lw["w_ab"])
        used = RET_IN + LRU_IN + 4 * GDN_W + AB_ROWS
        bg["w_in"] = jnp.concatenate(
            [wgrad(dhr, sv["x1"], "wgrad_in_r" + tag), wgrad(dhl, sv["x1"], "wgrad_in_l" + tag),
             wgrad(dhq, sv["x1"], "wgrad_in_q" + tag), wgrad(dab, sv["x1"], "wgrad_in_ab" + tag)[0:AB_ROWS],
             jnp.zeros((WIN_T_ROWS - used, D_MODEL), BF16)], axis=0)
        dx, act1, dhg1, dhu1, dy1, dg1, db1 = ffn_bwd(dx1, sv["r1"], sv["x0"], sv["hg1"], sv["hu1"], row(wt["ln_ffn1_g"][i]),
                                                      lw["wg1"], lw["wu1"], lw["wd1"])
        bg["ffn1_w_gate"] = wgrad(dhg1, sv["x0"], "wgrad_gate1" + tag)
        bg["ffn1_w_up"] = wgrad(dhu1, sv["x0"], "wgrad_up1" + tag)
        bg["ffn1_w_down"] = wgrad(act1, dy1, "wgrad_down1" + tag)
        g["ln_ffn1_g"], g["ln_ffn1_b"] = dg1[0], db1[0]
        grads[i] = g
        if mesh is None:
            big[i] = bg
        else:
            names = grad_names if i == 1 else grad_names[:n_late]
            mine = [bg[n] for n in names]
            theirs = sibling_send(mine, i, f"reduce_pair_send_l{i}")
            sums = [pair_add(u, v, f"reduce_pair_add_l{i}_" + n) for n, u, v in zip(names, mine, theirs)]
            if i == 1:
                pair_sums[1] = sums
            else:
                late_arrived = chip_exchange(sums, pieces[:n_late], 0, "reduce_chip_exchange_l0")
                pair_sums[0] = sums + early_sums
                big[0] = jnp.concatenate([late_arrived, early_arrived], axis=1)
    if mesh is not None:
        chip = 2 * lax.axis_index("x") + lax.axis_index("y")
        big = (big, [_own_share(pair_sums[layer], pieces, chip) for layer in range(DEPTH)])
    return loss, dx, {k: jnp.stack([grads[i][k] for i in range(DEPTH)]) for k in grads[0]}, big


def _natural_grad(name, rows):
    if name == "ple_w_proj":
        return rows.reshape(-1, PLE_DIM).T
    return rows.T if name in _TRANSPOSED else rows


_SPLIT = dict(ffn1_w_gate=2, ffn1_w_up=2, ffn1_w_down=1, w_in=2, w_out=1, ffn2_w_gate=2, ffn2_w_up=2, ffn2_w_down=1,
              ple_w_gate=1, ple_w_proj=2)
_CONV = ("lru_conv_w", "gdn_conv_w")
_WHOLE = ("ln_ffn1_g", "ln_ffn1_b", "ret_norm_g", "lru_conv_b", "lru_w_a", "lru_b_a", "lru_w_x", "lru_b_x", "lru_lambda",
          "gdn_a_log", "gdn_dt_bias", "gdn_norm_g", "ln_mix_g", "ln_mix_b", "ln_ffn2_g", "ln_ffn2_b")
_WEIGHTS = ("ln_ffn1_g", "ln_ffn1_b", "ffn1_w_gate", "ffn1_w_up", "ffn1_w_down", "w_in", "ret_norm_g", "lru_conv_w", "lru_conv_b",
            "lru_w_a", "lru_b_a", "lru_w_x", "lru_b_x", "lru_lambda", "gdn_conv_w", "gdn_a_log", "gdn_dt_bias", "gdn_norm_g",
            "w_out", "ln_mix_g", "ln_mix_b", "ffn2_w_gate", "ffn2_w_up", "ffn2_w_down", "ple_w_gate", "ple_w_proj",
            "ln_ffn2_g", "ln_ffn2_b")
_INPUTS = ("x", "p", "positions") + _WEIGHTS + ("loss_target",) + tuple("m_" + n for n in _WEIGHTS) + tuple("v_" + n for n in _WEIGHTS)

BIG_COLS = 1024
SMALL_COLS = LANES
SMALL_ROWS_MULT = 512


def _pack(arrays, dtype, cols, rows_mult):
    flat = jnp.concatenate([a.reshape(-1).astype(dtype) for a in arrays])
    rows = -(-flat.shape[0] // cols)
    rows = -(-rows // rows_mult) * rows_mult
    return jnp.pad(flat, (0, rows * cols - flat.shape[0])).reshape(rows, cols)


def _unpack(packed, shapes):
    flat = packed.reshape(-1)
    out, off = [], 0
    for shp in shapes:
        size = int(np.prod(shp))
        out.append(flat[off:off + size].reshape(shp))
        off += size
    return out


def _as2d(a):
    return a.reshape(-1, a.shape[-1])


def kernel(x, p, positions, ln_ffn1_g, ln_ffn1_b, ffn1_w_gate, ffn1_w_up, ffn1_w_down, w_in, ret_norm_g, lru_conv_w, lru_conv_b, lru_w_a, lru_b_a, lru_w_x, lru_b_x, lru_lambda, gdn_conv_w, gdn_a_log, gdn_dt_bias, gdn_norm_g, w_out, ln_mix_g, ln_mix_b, ffn2_w_gate, ffn2_w_up, ffn2_w_down, ple_w_gate, ple_w_proj, ln_ffn2_g, ln_ffn2_b, loss_target, m_ln_ffn1_g, m_ln_ffn1_b, m_ffn1_w_gate, m_ffn1_w_up, m_ffn1_w_down, m_w_in, m_ret_norm_g, m_lru_conv_w, m_lru_conv_b, m_lru_w_a, m_lru_b_a, m_lru_w_x, m_lru_b_x, m_lru_lambda, m_gdn_conv_w, m_gdn_a_log, m_gdn_dt_bias, m_gdn_norm_g, m_w_out, m_ln_mix_g, m_ln_mix_b, m_ffn2_w_gate, m_ffn2_w_up, m_ffn2_w_down, m_ple_w_gate, m_ple_w_proj, m_ln_ffn2_g, m_ln_ffn2_b, v_ln_ffn1_g, v_ln_ffn1_b, v_ffn1_w_gate, v_ffn1_w_up, v_ffn1_w_down, v_w_in, v_ret_norm_g, v_lru_conv_w, v_lru_conv_b, v_lru_w_a, v_lru_b_a, v_lru_w_x, v_lru_b_x, v_lru_lambda, v_gdn_conv_w, v_gdn_a_log, v_gdn_dt_bias, v_gdn_norm_g, v_w_out, v_ln_mix_g, v_ln_mix_b, v_ffn2_w_gate, v_ffn2_w_up, v_ffn2_w_down, v_ple_w_gate, v_ple_w_proj, v_ln_ffn2_g, v_ln_ffn2_b):
    a = dict(zip(_INPUTS, (x, p, positions, ln_ffn1_g, ln_ffn1_b, ffn1_w_gate, ffn1_w_up, ffn1_w_down, w_in, ret_norm_g, lru_conv_w, lru_conv_b, lru_w_a, lru_b_a, lru_w_x, lru_b_x, lru_lambda, gdn_conv_w, gdn_a_log, gdn_dt_bias, gdn_norm_g, w_out, ln_mix_g, ln_mix_b, ffn2_w_gate, ffn2_w_up, ffn2_w_down, ple_w_gate, ple_w_proj, ln_ffn2_g, ln_ffn2_b, loss_target, m_ln_ffn1_g, m_ln_ffn1_b, m_ffn1_w_gate, m_ffn1_w_up, m_ffn1_w_down, m_w_in, m_ret_norm_g, m_lru_conv_w, m_lru_conv_b, m_lru_w_a, m_lru_b_a, m_lru_w_x, m_lru_b_x, m_lru_lambda, m_gdn_conv_w, m_gdn_a_log, m_gdn_dt_bias, m_gdn_norm_g, m_w_out, m_ln_mix_g, m_ln_mix_b, m_ffn2_w_gate, m_ffn2_w_up, m_ffn2_w_down, m_ple_w_gate, m_ple_w_proj, m_ln_ffn2_g, m_ln_ffn2_b, v_ln_ffn1_g, v_ln_ffn1_b, v_ffn1_w_gate, v_ffn1_w_up, v_ffn1_w_down, v_w_in, v_ret_norm_g, v_lru_conv_w, v_lru_conv_b, v_lru_w_a, v_lru_b_a, v_lru_w_x, v_lru_b_x, v_lru_lambda, v_gdn_conv_w, v_gdn_a_log, v_gdn_dt_bias, v_gdn_norm_g, v_w_out, v_ln_mix_g, v_ln_mix_b, v_ffn2_w_gate, v_ffn2_w_up, v_ffn2_w_down, v_ple_w_gate, v_ple_w_proj, v_ln_ffn2_g, v_ln_ffn2_b)))
    assert len(a) == len(_INPUTS)
    core = lax.axis_index("c")
    chip = 2 * lax.axis_index("x") + lax.axis_index("y")
    big = list(_SPLIT)

    def group(layer, names, n_first):
        shards = [a[n][layer].astype(BF16) for n in names]

        def make(gathered):
            return {n: jnp.concatenate([gathered[i][k] for k in range(N_CHIPS)], axis=_SPLIT[n] - 1) for i, n in enumerate(names)}

        return shards[:n_first], shards[n_first:], make

    ffn1_0, ffn1_1, make_ffn1 = group(0, big[:3], 2)
    wt = {n: [w0, None] for n, w0 in make_ffn1(gather_layer_weights(ffn1_0, ffn1_1, "gather_weights_ffn1_l0")).items()}
    conv_g = all_gather8(_pack([a[n] for n in _CONV], F32, SMALL_COLS, 8), "gather_conv_weights")[0::2]
    conv_g = conv_g.reshape(N_CHIPS, -1)
    off = 0
    for n in _CONV:
        shp = a[n].shape
        size = int(np.prod(shp))
        parts = conv_g[:, off:off + size].reshape((N_CHIPS,) + shp)
        wt[n] = jnp.concatenate([parts[k] for k in range(N_CHIPS)], axis=2)
        off += size
    for n in _WHOLE:
        wt[n] = a[n]

    seq = a["x"].shape[1]
    mesh = dict(rest0=group(0, big[3:], 4), layer1=group(1, big, len(big) // 2))
    loss_part, dx, grads, (arrived, own) = _local_step(a["x"][0], a["p"][:, 0], a["positions"].reshape(seq, 1),
                                                       a["loss_target"][0], wt, mesh=mesh)
    loss = lax.psum(loss_part[0, 0], ("x", "y", "c"))

    my_layer_sum = sum_shares(arrived[0], arrived[1], own[0], own[1], "reduce_chip_sum")
    other_layer_sum = sibling_swap(my_layer_sum, "reduce_pair_share")
    reduced = [jnp.where(core == layer, my_layer_sum, other_layer_sum) for layer in range(DEPTH)]
    big_grads = {}
    off = 0
    for n, r, _ in _GRAD_PIECES:
        per_layer = []
        for layer in range(DEPTH):
            rows = reduced[layer][off:off + r]
            if n == "w_in":
                rows = lax.dynamic_slice_in_dim(rows, chip * (WIN_SHARD - WIN_STRIDE), WIN_SHARD, axis=0)
            per_layer.append(_natural_grad(n, rows))
        big_grads[n] = jnp.stack(per_layer)
        off += r

    small_names = list(_WHOLE) + list(_CONV)
    small_local = _pack([grads[n] for n in small_names], F32, SMALL_COLS, SMALL_ROWS_MULT)
    small_sum = sum_leading(all_gather8(small_local, "gather_small_grads"), "sum_small_grads")
    small_grads = dict(zip(small_names, _unpack(small_sum, [grads[n].shape for n in small_names])))
    for n in _CONV:
        width = a[n].shape[2]
        small_grads[n] = lax.dynamic_slice_in_dim(small_grads[n], chip * width, width, axis=2)

    new = {}
    for n in big:
        d, nm, nv = adamw(_as2d(a[n]), _as2d(big_grads[n]), _as2d(a["m_" + n]), _as2d(a["v_" + n]), "adamw_" + n)
        new[n] = tuple(t.reshape(a[n].shape) for t in (d, nm, nv))
    pk = lambda prefix: _pack([a[prefix + n] for n in small_names], F32, SMALL_COLS, SMALL_ROWS_MULT)
    pg = _pack([small_grads[n] for n in small_names], F32, SMALL_COLS, SMALL_ROWS_MULT)
    outs = adamw(pk(""), pg, pk("m_"), pk("v_"), "adamw_small")
    shapes = [a[n].shape for n in small_names]
    for n, d, nm, nv in zip(small_names, *[_unpack(o, shapes) for o in outs]):
        new[n] = (d, nm, nv)
    all_grads = {**big_grads, **small_grads}
    return (loss, dx[None], *[all_grads[n] for n in _WEIGHTS], *[new[n][0] for n in _WEIGHTS],
            *[new[n][1] for n in _WEIGHTS], *[new[n][2] for n in _WEIGHTS])
```

```python
import functools
import math

import numpy as np
import jax
import jax.numpy as jnp
from jax import lax
from jax.experimental import pallas as pl
from jax.experimental.pallas import tpu as pltpu

F32 = jnp.float32
BF16 = jnp.bfloat16

D_MODEL = 1024
D_FF = 2816
PLE_DIM = 256
DEPTH = 2
CHUNK = 64
RET_HEADS = 4
RET_W = 256
LRU_W = 384
LRU_BLOCKS = 6
GDN_HEADS = 6
GDN_W = 384
HEAD = 64
D_IN = 3340
RET_IN = 4 * RET_W
LRU_IN = 2 * LRU_W
GDN_IN = 6 * GDN_W
ROPE_THETA = 10000.0
ALPHA = (2 * DEPTH) ** 0.25
LN_EPS = 1e-5
LRU_C = 8.0
N_CHIPS = 4
N_DEV = 8

ADAM_LR = 0.001
ADAM_B1 = 0.9
ADAM_B2 = 0.999
ADAM_EPS = 1e-08
ADAM_WD = 0.01
ADAM_STEP = 10

LANES = 128
VMEM_LIMIT = 56 * 1024 * 1024
ROW_TILE = 256
ROW_TILE_BWD = 512
SCAN_TILE = 256


def _params(*sem):
    return pltpu.CompilerParams(dimension_semantics=sem, vmem_limit_bytes=VMEM_LIMIT)


def _operand(a):
    return a.astype(BF16)


def _mm(a, b):
    return jnp.dot(_operand(a), _operand(b), preferred_element_type=F32)


def _mm_nt(a, b):
    return lax.dot_general(_operand(a), _operand(b), (((1,), (1,)), ((), ())), preferred_element_type=F32)


def _mm_tn(a, b):
    return lax.dot_general(_operand(a), _operand(b), (((0,), (0,)), ((), ())), preferred_element_type=F32)


def _split(a):
    hi = a.astype(BF16)
    lo = (a - hi.astype(F32)).astype(BF16)
    return hi, lo


def _mm3(a, b):
    ah, al = _split(a)
    bh, bl = _split(b)
    return _mm(ah, bh) + (_mm(ah, bl) + _mm(al, bh))


def _sigmoid(x):
    return jax.nn.sigmoid(x)


def _log1p(u):
    w = 1.0 + u
    return jnp.where(w == 1.0, u, jnp.log(w) * (u / jnp.where(w == 1.0, 1.0, w - 1.0)))


def _expm1(y):
    u = jnp.exp(y)
    um1 = u - 1.0
    safe = jnp.where((u == 1.0) | (um1 == -1.0), 1.0, jnp.log(jnp.where(u == 0.0, 1.0, u)))
    return jnp.where(u == 1.0, y, jnp.where(um1 == -1.0, -1.0, um1 * (y / safe)))


def _softplus(x):
    return jnp.maximum(x, 0.0) + _log1p(jnp.exp(-jnp.abs(x)))


_GELU_C = math.sqrt(2.0 / math.pi)


def _gelu(x):
    return 0.5 * x * (1.0 + jnp.tanh(_GELU_C * (x + 0.044715 * (x * x * x))))


def _gelu_grad(x):
    t = jnp.tanh(_GELU_C * (x + 0.044715 * (x * x * x)))
    return 0.5 * (1.0 + t) + 0.5 * x * (1.0 - t * t) * (_GELU_C * (1.0 + 3.0 * 0.044715 * (x * x)))


def _silu_and_grad(x):
    s = _sigmoid(x)
    return x * s, s * (1.0 + x * (1.0 - s))


def _group_sum_slab(x):
    lane = lax.broadcasted_iota(jnp.int32, x.shape, 1)
    low = jnp.sum(x[:, 0:LANES // 2], axis=1, keepdims=True)
    high = jnp.sum(x[:, LANES // 2:], axis=1, keepdims=True)
    return jnp.where(lane < LANES // 2, low, high)


def _group_sum(x):
    n = x.shape[1] // LANES
    if n == 1:
        return _group_sum_slab(x)
    return jnp.concatenate([_group_sum_slab(x[:, LANES * i:LANES * (i + 1)]) for i in range(n)], axis=1)


def _rows_prefix_sum(x):
    n = x.shape[0]
    row = lax.broadcasted_iota(jnp.int32, x.shape, 0)
    d = 1
    while d < n:
        x = x + jnp.where(row >= d, pltpu.roll(x, d, 0), 0.0)
        d *= 2
    return x


def _rows_suffix_sum(x):
    n = x.shape[0]
    row = lax.broadcasted_iota(jnp.int32, x.shape, 0)
    d = 1
    while d < n:
        x = x + jnp.where(row < n - d, pltpu.roll(x, n - d, 0), 0.0)
        d *= 2
    return x


def _shift_rows(cur, prev, j):
    row = lax.broadcasted_iota(jnp.int32, cur.shape, 0)
    return jnp.where(row < j, pltpu.roll(prev, j, 0), pltpu.roll(cur, j, 0))


def _shift_rows_up(cur, nxt, j):
    n = cur.shape[0]
    row = lax.broadcasted_iota(jnp.int32, cur.shape, 0)
    return jnp.where(row < n - j, pltpu.roll(cur, n - j, 0), pltpu.roll(nxt, n - j, 0))


def _layer_norm_stats(r):
    mu = jnp.mean(r, axis=-1, keepdims=True)
    d = r - mu
    var = jnp.mean(d * d, axis=-1, keepdims=True)
    rstd = lax.rsqrt(var + LN_EPS)
    return d * rstd, rstd


def _load_resident(step, pairs, sems):
    @pl.when(step == 0)
    def _():
        cps = [pltpu.make_async_copy(h, v, sems.at[i]) for i, (h, v) in enumerate(pairs)]
        for c in cps:
            c.start()
        for c in cps:
            c.wait()


def _row_spec(tile, width):
    return pl.BlockSpec((tile, width), lambda i: (i, 0))


def _full_spec(shape):
    nd = len(shape)
    return pl.BlockSpec(shape, lambda i: (0,) * nd)


_ANY = pl.BlockSpec(memory_space=pl.ANY)


def ffn_fwd(x, ln_g, ln_b, w_gate, w_up, w_down, ple=None, fused=None):
    s = x.shape[0]
    tm = ROW_TILE
    with_ple = ple is not None
    weights = [w_gate, w_up, w_down] + ([ple[1], ple[2]] if with_ple else [])

    def body(*refs):
        it = iter(refs)
        x_ref, g_ref, b_ref = next(it), next(it), next(it)
        p_ref = next(it) if with_ple else None
        w_hbm = [next(it) for _ in weights]
        hg_ref, hu_ref, r_ref, xn_ref, xnb_ref = next(it), next(it), next(it), next(it), next(it)
        pg_ref, pp_ref = (next(it), next(it)) if with_ple else (None, None)
        w_vm = [next(it) for _ in weights]
        sems = next(it)
        _load_resident(pl.program_id(0), list(zip(w_hbm, w_vm)), sems)
        xv = x_ref[...]
        xb = xv.astype(BF16)
        hg = _mm(xb, w_vm[0][...])
        hu = _mm(xb, w_vm[1][...])
        hg_ref[...] = hg
        hu_ref[...] = hu
        act = (hg * _sigmoid(hg)) * hu
        r = ALPHA * xv + 0.5 * _mm(act.astype(BF16), w_vm[2][...])
        if with_ple:
            pg = _mm(xb, w_vm[3][...])
            pp = _mm(p_ref[...].astype(BF16), w_vm[4][...])
            pg_ref[...] = pg
            pp_ref[...] = pp
            r = r + _sigmoid(pg) * pp
        r_ref[...] = r
        xhat, _ = _layer_norm_stats(r)
        xn = xhat * g_ref[...] + b_ref[...]
        xn_ref[...] = xn
        xnb_ref[...] = xn.astype(BF16)

    d, f = D_MODEL, D_FF
    in_specs = [_row_spec(tm, d), _full_spec((1, d)), _full_spec((1, d))]
    args = [x, ln_g, ln_b]
    if with_ple:
        in_specs.append(_row_spec(tm, PLE_DIM))
        args.append(ple[0])
    in_specs += [_ANY] * len(weights)
    args += weights
    out_shape = [jax.ShapeDtypeStruct((s, f), F32), jax.ShapeDtypeStruct((s, f), F32),
                 jax.ShapeDtypeStruct((s, d), F32), jax.ShapeDtypeStruct((s, d), F32), jax.ShapeDtypeStruct((s, d), BF16)]
    out_specs = [_row_spec(tm, f), _row_spec(tm, f), _row_spec(tm, d), _row_spec(tm, d), _row_spec(tm, d)]
    if with_ple:
        out_shape += [jax.ShapeDtypeStruct((s, d), F32)] * 2
        out_specs += [_row_spec(tm, d)] * 2
    scratch = [pltpu.VMEM(w.shape, w.dtype) for w in weights] + [pltpu.SemaphoreType.DMA((len(weights),))]
    f_in, f_out, f_scr, _ = fused if fused is not None else ([], [], [], None)
    n_out = len(out_shape)
    outs = pl.pallas_call(
        _fuse(body, len(args), n_out, len(scratch), s // tm, fused),
        name=("ffn_fwd_ple" if with_ple else "ffn_fwd") + ("_gather" if fused is not None else ""), grid=(s // tm,),
        in_specs=in_specs + [_ANY] * len(f_in), out_specs=out_specs + [_ANY] * len(f_out),
        out_shape=out_shape + list(f_out), scratch_shapes=scratch + list(f_scr), compiler_params=_params("arbitrary"),
    )(*args, *f_in)
    return tuple(outs[:n_out]) + ((list(outs[n_out:]),) if fused is not None else ())


def ffn_bwd(dxn, r, x, hg, hu, ln_g, w_gate, w_up, w_down, ple=None):
    s = x.shape[0]
    with_ple = ple is not None
    d, f = D_MODEL, D_FF
    suffix = "_ple" if with_ple else ""

    tm = ROW_TILE

    def body_a(*refs):
        it = iter(refs)
        dxn_ref, r_ref, hg_ref, hu_ref, g_ref = (next(it) for _ in range(5))
        pg_ref, pp_ref = (next(it), next(it)) if with_ple else (None, None)
        wd_hbm = next(it)
        dr_ref, act_ref, dhg_ref, dhu_ref, dy_ref, dg_ref, db_ref = (next(it) for _ in range(7))
        dpg_ref, dpp_ref = (next(it), next(it)) if with_ple else (None, None)
        wd_vm, sems = next(it), next(it)
        step = pl.program_id(0)
        _load_resident(step, [(wd_hbm, wd_vm)], sems)

        @pl.when(step == 0)
        def _():
            dg_ref[...] = jnp.zeros_like(dg_ref)
            db_ref[...] = jnp.zeros_like(db_ref)

        dxn_v = dxn_ref[...]
        xhat, rstd = _layer_norm_stats(r_ref[...])
        dg_ref[...] += jnp.sum(dxn_v * xhat, axis=0, keepdims=True)
        db_ref[...] += jnp.sum(dxn_v, axis=0, keepdims=True)
        dyh = dxn_v * g_ref[...]
        dr = rstd * (dyh - jnp.mean(dyh, axis=-1, keepdims=True) - xhat * jnp.mean(dyh * xhat, axis=-1, keepdims=True))
        dr_ref[...] = dr
        dy = (0.5 * dr).astype(BF16)
        dy_ref[...] = dy
        da = _mm_nt(dy, wd_vm[...])
        hg_v = hg_ref[...]
        hu_v = hu_ref[...]
        sil, dsil = _silu_and_grad(hg_v)
        act_ref[...] = (sil * hu_v).astype(BF16)
        dhu_ref[...] = (da * sil).astype(BF16)
        dhg_ref[...] = (da * hu_v * dsil).astype(BF16)
        if with_ple:
            sp = _sigmoid(pg_ref[...])
            dpp_ref[...] = (dr * sp).astype(BF16)
            dpg_ref[...] = (dr * pp_ref[...] * sp * (1.0 - sp)).astype(BF16)

    in_specs = [_row_spec(tm, d), _row_spec(tm, d), _row_spec(tm, f), _row_spec(tm, f), _full_spec((1, d))]
    args = [dxn, r, hg, hu, ln_g]
    if with_ple:
        in_specs += [_row_spec(tm, d), _row_spec(tm, d)]
        args += [ple[0], ple[1]]
    out_shape = [jax.ShapeDtypeStruct((s, d), F32), jax.ShapeDtypeStruct((s, f), BF16), jax.ShapeDtypeStruct((s, f), BF16),
                 jax.ShapeDtypeStruct((s, f), BF16), jax.ShapeDtypeStruct((s, d), BF16),
                 jax.ShapeDtypeStruct((1, d), F32), jax.ShapeDtypeStruct((1, d), F32)]
    out_specs = [_row_spec(tm, d), _row_spec(tm, f), _row_spec(tm, f), _row_spec(tm, f), _row_spec(tm, d),
                 _full_spec((1, d)), _full_spec((1, d))]
    if with_ple:
        out_shape += [jax.ShapeDtypeStruct((s, d), BF16)] * 2
        out_specs += [_row_spec(tm, d)] * 2
    first = pl.pallas_call(
        body_a, name="ffn_bwd_hidden" + suffix, grid=(s // tm,), in_specs=in_specs + [_ANY], out_specs=out_specs,
        out_shape=out_shape, scratch_shapes=[pltpu.VMEM(w_down.shape, w_down.dtype), pltpu.SemaphoreType.DMA((1,))],
        compiler_params=_params("arbitrary"),
    )(*args, w_down)
    dr, act, dhg, dhu, dy, dg, db = first[:7]

    tb = min(ROW_TILE_BWD, s)
    weights = [w_gate, w_up] + ([ple[2]] if with_ple else [])

    def body_b(*refs):
        it = iter(refs)
        dr_ref, dhg_ref, dhu_ref = next(it), next(it), next(it)
        dpg_ref = next(it) if with_ple else None
        w_hbm = [next(it) for _ in weights]
        dx_ref = next(it)
        w_vm = [next(it) for _ in weights]
        sems = next(it)
        _load_resident(pl.program_id(0), list(zip(w_hbm, w_vm)), sems)
        dx = ALPHA * dr_ref[...] + _mm_nt(dhg_ref[...], w_vm[0][...]) + _mm_nt(dhu_ref[...], w_vm[1][...])
        if with_ple:
            dx = dx + _mm_nt(dpg_ref[...], w_vm[2][...])
        dx_ref[...] = dx

    in_specs = [_row_spec(tb, d), _row_spec(tb, f), _row_spec(tb, f)] + ([_row_spec(tb, d)] if with_ple else [])
    args = [dr, dhg, dhu] + ([first[7]] if with_ple else [])
    dx = pl.pallas_call(
        body_b, name="ffn_bwd_input" + suffix, grid=(s // tb,), in_specs=in_specs + [_ANY] * len(weights),
        out_specs=_row_spec(tb, d), out_shape=jax.ShapeDtypeStruct((s, d), F32),
        scratch_shapes=[pltpu.VMEM(w.shape, w.dtype) for w in weights] + [pltpu.SemaphoreType.DMA((len(weights),))],
        compiler_params=_params("arbitrary"),
    )(*args, *weights)
    return (dx, act, dhg, dhu, dy, dg, db) + tuple(first[7:])


def win_fwd(x1, w_r, w_l, w_g):
    s = x1.shape[0]
    tm = min(ROW_TILE_BWD, s)
    weights = [w_r, w_l, w_g]

    def body(x_ref, wr_h, wl_h, wg_h, hr_ref, hl_ref, hgd_ref, wr_v, wl_v, wg_v, sems):
        _load_resident(pl.program_id(0), [(wr_h, wr_v), (wl_h, wl_v), (wg_h, wg_v)], sems)
        xb = x_ref[...].astype(BF16)
        hr_ref[...] = _mm(xb, wr_v[...])
        hl_ref[...] = _mm(xb, wl_v[...])
        hgd_ref[...] = _mm(xb, wg_v[...])

    return pl.pallas_call(
        body, name="win_fwd", grid=(s // tm,),
        in_specs=[_row_spec(tm, D_MODEL), _ANY, _ANY, _ANY],
        out_specs=[_row_spec(tm, RET_IN), _row_spec(tm, LRU_IN), _row_spec(tm, GDN_IN)],
        out_shape=[jax.ShapeDtypeStruct((s, RET_IN), F32), jax.ShapeDtypeStruct((s, LRU_IN), F32),
                   jax.ShapeDtypeStruct((s, GDN_IN), F32)],
        scratch_shapes=[pltpu.VMEM(w.shape, w.dtype) for w in weights] + [pltpu.SemaphoreType.DMA((3,))],
        compiler_params=_params("arbitrary"),
    )(x1, *weights)


def win_bwd(dr2, dhr, dhl, dhq, dab, w_r, w_l, w_g, w_ab):
    s = dr2.shape[0]
    tm = min(ROW_TILE_BWD, s)
    weights = [w_r, w_l, w_g, w_ab]
    nq = 4 * GDN_W

    def body(dr_ref, dhr_ref, dhl_ref, dhq_ref, dab_ref, wr_h, wl_h, wg_h, wab_h, dx_ref, wr_v, wl_v, wg_v, wab_v, sems):
        _load_resident(pl.program_id(0), [(wr_h, wr_v), (wl_h, wl_v), (wg_h, wg_v), (wab_h, wab_v)], sems)
        dx_ref[...] = (ALPHA * dr_ref[...] + _mm_nt(dhr_ref[...], wr_v[...]) + _mm_nt(dhl_ref[...], wl_v[...])
                       + _mm_nt(dhq_ref[...], wg_v[:, 0:nq]) + _mm_nt(dab_ref[...], wab_v[...]))

    return pl.pallas_call(
        body, name="win_bwd", grid=(s // tm,),
        in_specs=[_row_spec(tm, D_MODEL), _row_spec(tm, RET_IN), _row_spec(tm, LRU_IN), _row_spec(tm, nq), _row_spec(tm, LANES),
                  _ANY, _ANY, _ANY, _ANY],
        out_specs=_row_spec(tm, D_MODEL),
        out_shape=jax.ShapeDtypeStruct((s, D_MODEL), F32),
        scratch_shapes=[pltpu.VMEM(w.shape, w.dtype) for w in weights] + [pltpu.SemaphoreType.DMA((4,))],
        compiler_params=_params("arbitrary"),
    )(dr2, dhr, dhl, dhq, dab, *weights)


def out_fwd(o_r, o_l, o_g, x1, w_out, ln_g, ln_b):
    s = x1.shape[0]
    tm = ROW_TILE

    def body(or_ref, ol_ref, og_ref, x_ref, g_ref, b_ref, w_h, r_ref, xn_ref, xnb_ref, ocat_ref, w_v, sems):
        _load_resident(pl.program_id(0), [(w_h, w_v)], sems)
        ocat = jnp.concatenate([or_ref[...], ol_ref[...], og_ref[...]], axis=1).astype(BF16)
        ocat_ref[...] = ocat
        r = ALPHA * x_ref[...] + _mm(ocat, w_v[...])
        r_ref[...] = r
        xhat, _ = _layer_norm_stats(r)
        xn = xhat * g_ref[...] + b_ref[...]
        xn_ref[...] = xn
        xnb_ref[...] = xn.astype(BF16)

    d = D_MODEL
    return pl.pallas_call(
        body, name="out_fwd", grid=(s // tm,),
        in_specs=[_row_spec(tm, RET_W), _row_spec(tm, LRU_W), _row_spec(tm, GDN_W), _row_spec(tm, d),
                  _full_spec((1, d)), _full_spec((1, d)), _ANY],
        out_specs=[_row_spec(tm, d)] * 4,
        out_shape=[jax.ShapeDtypeStruct((s, d), F32)] * 2 + [jax.ShapeDtypeStruct((s, d), BF16)] * 2,
        scratch_shapes=[pltpu.VMEM(w_out.shape, w_out.dtype), pltpu.SemaphoreType.DMA((1,))],
        compiler_params=_params("arbitrary"),
    )(o_r, o_l, o_g, x1, ln_g, ln_b, w_out)


def out_bwd(dxn, r2, ln_g, w_out):
    s = dxn.shape[0]
    tm = ROW_TILE

    def body(dxn_ref, r_ref, g_ref, w_h, dr_ref, drb_ref, dor_ref, dol_ref, dog_ref, dg_ref, db_ref, w_v, sems):
        step = pl.program_id(0)
        _load_resident(step, [(w_h, w_v)], sems)

        @pl.when(step == 0)
        def _():
            dg_ref[...] = jnp.zeros_like(dg_ref)
            db_ref[...] = jnp.zeros_like(db_ref)

        dxn_v = dxn_ref[...]
        xhat, rstd = _layer_norm_stats(r_ref[...])
        dg_ref[...] += jnp.sum(dxn_v * xhat, axis=0, keepdims=True)
        db_ref[...] += jnp.sum(dxn_v, axis=0, keepdims=True)
        dyh = dxn_v * g_ref[...]
        dr = rstd * (dyh - jnp.mean(dyh, axis=-1, keepdims=True) - xhat * jnp.mean(dyh * xhat, axis=-1, keepdims=True))
        dr_ref[...] = dr
        drb = dr.astype(BF16)
        drb_ref[...] = drb
        dor_ref[...] = _mm_nt(drb, w_v[0:RET_W, :])
        dol_ref[...] = _mm_nt(drb, w_v[RET_W:RET_W + LRU_W, :])
        dog_ref[...] = _mm_nt(drb, w_v[RET_W + LRU_W:, :])

    d = D_MODEL
    return pl.pallas_call(
        body, name="out_bwd", grid=(s // tm,),
        in_specs=[_row_spec(tm, d), _row_spec(tm, d), _full_spec((1, d)), _ANY],
        out_specs=[_row_spec(tm, d), _row_spec(tm, d), _row_spec(tm, RET_W), _row_spec(tm, LRU_W), _row_spec(tm, GDN_W),
                   _full_spec((1, d)), _full_spec((1, d))],
        out_shape=[jax.ShapeDtypeStruct((s, d), F32), jax.ShapeDtypeStruct((s, d), BF16),
                   jax.ShapeDtypeStruct((s, RET_W), F32), jax.ShapeDtypeStruct((s, LRU_W), F32),
                   jax.ShapeDtypeStruct((s, GDN_W), F32), jax.ShapeDtypeStruct((1, d), F32), jax.ShapeDtypeStruct((1, d), F32)],
        scratch_shapes=[pltpu.VMEM(w_out.shape, w_out.dtype), pltpu.SemaphoreType.DMA((1,))],
        compiler_params=_params("arbitrary"),
    )(dxn, r2, ln_g, w_out)


def wgrad(a, b, name, out_dtype=BF16):
    s, m = a.shape
    n = b.shape[1]
    tk = 1024 if s % 1024 == 0 else s
    tm = next((c for c in (1408, 1024, 768, 512, 384, 256) if m % c == 0), m)
    tn = next((c for c in (1408, 1152, 1024, 768, 512) if n % c == 0), n)
    nk = s // tk

    def body(a_ref, b_ref, o_ref, acc_ref):
        k = pl.program_id(2)

        @pl.when(k == 0)
        def _():
            acc_ref[...] = jnp.zeros_like(acc_ref)

        acc_ref[...] += _mm_tn(a_ref[...].astype(BF16), b_ref[...].astype(BF16))

        @pl.when(k == nk - 1)
        def _():
            o_ref[...] = acc_ref[...].astype(o_ref.dtype)

    return pl.pallas_call(
        body, name=name, grid=(m // tm, n // tn, nk),
        in_specs=[pl.BlockSpec((tk, tm), lambda i, j, k: (k, i)), pl.BlockSpec((tk, tn), lambda i, j, k: (k, j))],
        out_specs=pl.BlockSpec((tm, tn), lambda i, j, k: (i, j)),
        out_shape=jax.ShapeDtypeStruct((m, n), out_dtype),
        scratch_shapes=[pltpu.VMEM((tm, tn), F32)],
        compiler_params=_params("arbitrary", "arbitrary", "arbitrary"),
    )(a, b)


def loss_and_grad(y, target):
    s, d = y.shape
    tm = ROW_TILE

    def body(y_ref, t_ref, dy_ref, l_ref):
        @pl.when(pl.program_id(0) == 0)
        def _():
            l_ref[...] = jnp.zeros_like(l_ref)

        err = y_ref[...] - t_ref[...]
        dy_ref[...] = err / d
        l_ref[...] += 0.5 * jnp.sum(jnp.mean(err * err, axis=-1, keepdims=True), axis=0, keepdims=True)

    return pl.pallas_call(
        body, name="loss_and_grad", grid=(s // tm,),
        in_specs=[_row_spec(tm, d), _row_spec(tm, d)],
        out_specs=[_row_spec(tm, d), _full_spec((1, 1))],
        out_shape=[jax.ShapeDtypeStruct((s, d), F32), jax.ShapeDtypeStruct((1, 1), F32)],
        compiler_params=_params("arbitrary"),
    )(y, target)


def _ret_consts():
    lg = np.log1p(-np.exp2(-5.0 - np.arange(RET_HEADS, dtype=np.float64)))
    idx = np.arange(CHUNK, dtype=np.float64)
    intra = np.exp(np.abs(idx[:, None] - idx[None, :])[None] * lg[:, None, None])
    cross = np.repeat(np.exp((idx + 1.0)[:, None] * lg[None, :]), HEAD, axis=1)
    tail = np.repeat(np.exp((CHUNK - 1.0 - idx)[:, None] * lg[None, :]), HEAD, axis=1)
    dec = np.repeat(np.exp(CHUNK * lg)[None, :], HEAD, axis=1)
    half = HEAD // 2
    inv_freq = (ROPE_THETA ** (-jnp.arange(half, dtype=F32) / half))
    invf = jnp.tile(inv_freq, 2 * LANES // HEAD)[None, :]
    sgn = np.tile(np.concatenate([-np.ones(half), np.ones(half)]), LANES // HEAD)[None, :]
    f = lambda a: jnp.asarray(a, F32)
    return dict(intra=f(intra), cross=f(cross), tail=f(tail), dec=f(dec), invf=invf, sgn=f(sgn))


def _swap_halves(t):
    lane = lax.broadcasted_iota(jnp.int32, t.shape, 1)
    return jnp.where((lane & 32) == 0, pltpu.roll(t, LANES - 32, 1), pltpu.roll(t, 32, 1))


def _rope(t, c, s):
    return t * c + _swap_halves(t) * s


def _rope_transposed(g, c, s):
    return g * c + _swap_halves(g * s)


def _head_mask(hd):
    lane = lax.broadcasted_iota(jnp.int32, (1, LANES), 1)
    return ((lane >= HEAD * hd) & (lane < HEAD * (hd + 1))).astype(F32)


def _block_diag_mask():
    r = lax.broadcasted_iota(jnp.int32, (LANES, LANES), 0)
    c = lax.broadcasted_iota(jnp.int32, (LANES, LANES), 1)
    return ((r >= HEAD) == (c >= HEAD)).astype(F32)


RET_STEP_CHUNKS = 4
RET_BWD_STEP_CHUNKS = 8


def _ret_specs(n_of, gch):
    cst = lambda shape: pl.BlockSpec(shape, lambda i: (0,) * len(shape))
    return [pl.BlockSpec((CHUNK * gch, RET_IN), lambda i: (n_of(i), 0)), pl.BlockSpec((CHUNK * gch, 1), lambda i: (n_of(i), 0)),
            cst((1, LANES)), cst((1, LANES)), cst((RET_HEADS, CHUNK, CHUNK)), cst((CHUNK, RET_W)), cst((CHUNK, RET_W)),
            cst((1, RET_W)), cst((1, RET_W))]


def ret_fwd(hr, pos, norm_g):
    s = hr.shape[0]
    n_chunks = s // CHUNK
    cs = _ret_consts()
    n_slab = RET_W // LANES

    gch = min(RET_STEP_CHUNKS, n_chunks)

    def body(hr_ref, pos_ref, invf_ref, sgn_ref, intra_ref, cross_ref, tail_ref, dec_ref, g_ref, o_ref, opre_ref, st_ref, state):
        @pl.when(pl.program_id(0) == 0)
        def _():
            state[...] = jnp.zeros_like(state)

        bd = _block_diag_mask()
        sts = [state[LANES * sl:LANES * (sl + 1), :] for sl in range(n_slab)]
        for c in range(gch):
            tok = slice(CHUNK * c, CHUNK * (c + 1))
            ang = pos_ref[tok, :].astype(F32) * invf_ref[...]
            cosv = jnp.cos(ang)
            sinv = jnp.sin(ang) * sgn_ref[...]
            for sl in range(n_slab):
                lanes = slice(LANES * sl, LANES * (sl + 1))
                q = hr_ref[tok, LANES * sl:LANES * (sl + 1)]
                k = hr_ref[tok, RET_W + LANES * sl:RET_W + LANES * (sl + 1)]
                v = hr_ref[tok, 2 * RET_W + LANES * sl:2 * RET_W + LANES * (sl + 1)]
                gate = hr_ref[tok, 3 * RET_W + LANES * sl:3 * RET_W + LANES * (sl + 1)]
                qt = _rope(q, cosv, sinv) * (HEAD ** -0.5)
                kt = _rope(k, cosv, sinv)
                st = sts[sl]
                st_ref[RET_W * c + LANES * sl:RET_W * c + LANES * (sl + 1), :] = st
                o = _mm(qt * cross_ref[:, lanes], st)
                for hd in range(2):
                    m = _head_mask(hd)
                    sc = _mm_nt(qt * m, kt) * intra_ref[2 * sl + hd]
                    o = o + _mm(sc, v) * m
                sts[sl] = st * dec_ref[:, lanes] + _mm_tn(kt, v * tail_ref[:, lanes]) * bd
                opre_ref[tok, lanes] = o
                mu = _group_sum_slab(o) * (1.0 / HEAD)
                dlt = o - mu
                var = _group_sum_slab(dlt * dlt) * (1.0 / HEAD)
                on = dlt * lax.rsqrt(var + 1e-5)
                o_ref[tok, lanes] = on * g_ref[:, lanes] * (gate * _sigmoid(gate))
        for sl in range(n_slab):
            state[LANES * sl:LANES * (sl + 1), :] = sts[sl]

    out_row = lambda w: pl.BlockSpec((CHUNK * gch, w), lambda i: (i, 0))
    return pl.pallas_call(
        body, name="ret_fwd", grid=(n_chunks // gch,),
        in_specs=_ret_specs(lambda i: i, gch),
        out_specs=[out_row(RET_W), out_row(RET_W), pl.BlockSpec((RET_W * gch, LANES), lambda i: (i, 0))],
        out_shape=[jax.ShapeDtypeStruct((s, RET_W), F32), jax.ShapeDtypeStruct((s, RET_W), F32),
                   jax.ShapeDtypeStruct((n_chunks * RET_W, LANES), F32)],
        scratch_shapes=[pltpu.VMEM((RET_W, LANES), F32)],
        compiler_params=_params("arbitrary"),
    )(hr, pos, cs["invf"], cs["sgn"], cs["intra"], cs["cross"], cs["tail"], cs["dec"], norm_g)


def ret_bwd(hr, pos, norm_g, opre, states, dout):
    s = hr.shape[0]
    n_chunks = s // CHUNK
    cs = _ret_consts()
    n_slab = RET_W // LANES
    gch = min(RET_BWD_STEP_CHUNKS, n_chunks)
    rev = lambda i: n_chunks // gch - 1 - i

    def body(hr_ref, pos_ref, invf_ref, sgn_ref, intra_ref, cross_ref, tail_ref, dec_ref, g_ref, opre_ref, st_ref, do_ref,
             dh_ref, dg_ref, gstate):
        @pl.when(pl.program_id(0) == 0)
        def _():
            gstate[...] = jnp.zeros_like(gstate)
            dg_ref[...] = jnp.zeros_like(dg_ref)

        bd = _block_diag_mask()
        gss = [gstate[LANES * sl:LANES * (sl + 1), :] for sl in range(n_slab)]
        dgs = [jnp.zeros((1, LANES), F32) for _ in range(n_slab)]
        for c in reversed(range(gch)):
            tok = slice(CHUNK * c, CHUNK * (c + 1))
            ang = pos_ref[tok, :].astype(F32) * invf_ref[...]
            cosv = jnp.cos(ang)
            sinv = jnp.sin(ang) * sgn_ref[...]
            for sl in range(n_slab):
                lanes = slice(LANES * sl, LANES * (sl + 1))
                q = hr_ref[tok, LANES * sl:LANES * (sl + 1)]
                k = hr_ref[tok, RET_W + LANES * sl:RET_W + LANES * (sl + 1)]
                v = hr_ref[tok, 2 * RET_W + LANES * sl:2 * RET_W + LANES * (sl + 1)]
                gate = hr_ref[tok, 3 * RET_W + LANES * sl:3 * RET_W + LANES * (sl + 1)]
                qt = _rope(q, cosv, sinv) * (HEAD ** -0.5)
                kt = _rope(k, cosv, sinv)
                o = opre_ref[tok, lanes]
                mu = _group_sum_slab(o) * (1.0 / HEAD)
                dlt = o - mu
                var = _group_sum_slab(dlt * dlt) * (1.0 / HEAD)
                rstd = lax.rsqrt(var + 1e-5)
                on = dlt * rstd
                sil, dsil = _silu_and_grad(gate)
                dout_v = do_ref[tok, lanes]
                gn = g_ref[:, lanes]
                dgs[sl] = dgs[sl] + jnp.sum(dout_v * on * sil, axis=0, keepdims=True)
                d_on = dout_v * gn * sil
                dgate = dout_v * on * gn * dsil
                d_o = rstd * (d_on - _group_sum_slab(d_on) * (1.0 / HEAD) - on * (_group_sum_slab(d_on * on) * (1.0 / HEAD)))
                st = st_ref[RET_W * c + LANES * sl:RET_W * c + LANES * (sl + 1), :]
                gs = gss[sl]
                cross = cross_ref[:, lanes]
                tail = tail_ref[:, lanes]
                dqt = _mm_nt(d_o, st) * cross
                ds_here = _mm_tn(qt * cross, d_o) * bd
                vt = v * tail
                dkt = _mm_nt(vt, gs)
                dv = _mm(kt, gs) * tail
                for hd in range(2):
                    m = _head_mask(hd)
                    qm = qt * m
                    dom = d_o * m
                    intra = intra_ref[2 * sl + hd]
                    sc = _mm_nt(qm, kt) * intra
                    dsc = _mm_nt(dom, v) * intra
                    dqt = dqt + _mm(dsc, kt) * m
                    dkt = dkt + _mm_tn(dsc, qm)
                    dv = dv + _mm_tn(sc, dom)
                gss[sl] = gs * dec_ref[:, lanes] + ds_here
                dh_ref[tok, LANES * sl:LANES * (sl + 1)] = _rope_transposed(dqt * (HEAD ** -0.5), cosv, sinv).astype(BF16)
                dh_ref[tok, RET_W + LANES * sl:RET_W + LANES * (sl + 1)] = _rope_transposed(dkt, cosv, sinv).astype(BF16)
                dh_ref[tok, 2 * RET_W + LANES * sl:2 * RET_W + LANES * (sl + 1)] = dv.astype(BF16)
                dh_ref[tok, 3 * RET_W + LANES * sl:3 * RET_W + LANES * (sl + 1)] = dgate.astype(BF16)
        for sl in range(n_slab):
            gstate[LANES * sl:LANES * (sl + 1), :] = gss[sl]
            dg_ref[:, LANES * sl:LANES * (sl + 1)] += dgs[sl]

    row = lambda w: pl.BlockSpec((CHUNK * gch, w), lambda i: (rev(i), 0))
    return pl.pallas_call(
        body, name="ret_bwd", grid=(n_chunks // gch,),
        in_specs=_ret_specs(rev, gch) + [row(RET_W), pl.BlockSpec((RET_W * gch, LANES), lambda i: (rev(i), 0)), row(RET_W)],
        out_specs=[row(RET_IN), pl.BlockSpec((1, RET_W), lambda i: (0, 0))],
        out_shape=[jax.ShapeDtypeStruct((s, RET_IN), BF16), jax.ShapeDtypeStruct((1, RET_W), F32)],
        scratch_shapes=[pltpu.VMEM((RET_W, LANES), F32)],
        compiler_params=_params("arbitrary"),
    )(hr, pos, cs["invf"], cs["sgn"], cs["intra"], cs["cross"], cs["tail"], cs["dec"], norm_g, opre, states, dout)


def _lru_gates(xc, wa_ref, ba_ref, wx_ref, bx_ref, lam_ref):
    xcb = xc.astype(BF16)
    r = _sigmoid(_mm(xcb, wa_ref[...].astype(BF16)) + ba_ref[...])
    ig = _sigmoid(_mm(xcb, wx_ref[...].astype(BF16)) + bx_ref[...])
    lam = lam_ref[...]
    ls = jnp.minimum(lam, 0.0) - _log1p(jnp.exp(-jnp.abs(lam)))
    la = (LRU_C * r) * ls
    a = jnp.exp(la)
    mult = jnp.sqrt(-_expm1(2.0 * la))
    return r, ig, ls, a, mult


def _lru_conv(x, xprev, w_ref, b_ref):
    xc = b_ref[...] + w_ref[3:4, :] * x
    for j in (1, 2, 3):
        xc = xc + w_ref[3 - j:4 - j, :] * _shift_rows(x, xprev, j)
    return xc


def lru_fwd(hl, conv_w, conv_b, w_a, b_a, w_x, b_x, lam):
    s = hl.shape[0]
    ts = SCAN_TILE
    w = LRU_W

    def body(hl_ref, hp_ref, cw_ref, cb_ref, wa_ref, ba_ref, wx_ref, bx_ref, lam_ref, o_ref, xc_ref, h_ref, carry):
        i = pl.program_id(0)

        @pl.when(i == 0)
        def _():
            carry[...] = jnp.zeros_like(carry)

        x = hl_ref[:, 0:w]
        gate = hl_ref[:, w:2 * w]
        xprev = hp_ref[...] * (i > 0).astype(F32)
        xc = _lru_conv(x, xprev, cw_ref, cb_ref)
        xc_ref[...] = xc
        _, ig, _, a, mult = _lru_gates(xc, wa_ref, ba_ref, wx_ref, bx_ref, lam_ref)
        b = mult * (ig * xc)
        row = lax.broadcasted_iota(jnp.int32, (ts, w), 0)
        d = 1
        while d < ts:
            ap = jnp.where(row >= d, pltpu.roll(a, d, 0), 1.0)
            bp = jnp.where(row >= d, pltpu.roll(b, d, 0), 0.0)
            b = a * bp + b
            a = a * ap
            d *= 2
        h = b + a * carry[0:1, :]
        h_ref[...] = h
        carry[0:1, :] = h[ts - 1:ts, :]
        o_ref[...] = h * _gelu(gate)

    cst = lambda shape: pl.BlockSpec(shape, lambda i: (0, 0))
    return pl.pallas_call(
        body, name="lru_fwd", grid=(s // ts,),
        in_specs=[_row_spec(ts, 2 * w), pl.BlockSpec((ts, w), lambda i: (jnp.maximum(i - 1, 0), 0)),
                  cst((4, w)), cst((1, w)), cst((w, w)), cst((1, w)), cst((w, w)), cst((1, w)), cst((1, w))],
        out_specs=[_row_spec(ts, w)] * 3,
        out_shape=[jax.ShapeDtypeStruct((s, w), F32)] * 3,
        scratch_shapes=[pltpu.VMEM((8, w), F32)],
        compiler_params=_params("arbitrary"),
    )(hl, hl, conv_w, conv_b, w_a, b_a, w_x, b_x, lam)


def lru_bwd(hl, conv_w, conv_b, w_a, b_a, w_x, b_x, lam, xc_saved, h_saved, dout):
    s = hl.shape[0]
    ts = SCAN_TILE
    w = LRU_W
    nb = s // ts
    rev = lambda i: nb - 1 - i

    def body(hl_ref, hp_ref, cw_ref, cb_ref, wa_ref, ba_ref, wx_ref, bx_ref, lam_ref, xc_ref, h_ref, hprev_ref, do_ref,
             dhl_ref, dcw_ref, dcb_ref, dwa_ref, dba_ref, dwx_ref, dbx_ref, dlam_ref, carry, dxc_next):
        i = pl.program_id(0)
        blk = nb - 1 - i

        @pl.when(i == 0)
        def _():
            carry[...] = jnp.zeros_like(carry)
            dxc_next[...] = jnp.zeros_like(dxc_next)
            for ref in (dcw_ref, dcb_ref, dwa_ref, dba_ref, dwx_ref, dbx_ref, dlam_ref):
                ref[...] = jnp.zeros_like(ref)

        first = (blk > 0).astype(F32)
        x = hl_ref[:, 0:w]
        gate = hl_ref[:, w:2 * w]
        xprev = hp_ref[...] * first
        xc = xc_ref[...]
        h = h_ref[...]
        hprev = hprev_ref[...] * first
        r, ig, ls, a, mult = _lru_gates(xc, wa_ref, ba_ref, wx_ref, bx_ref, lam_ref)
        do = do_ref[...]
        dh = do * _gelu(gate)
        dgate = do * h * _gelu_grad(gate)
        row = lax.broadcasted_iota(jnp.int32, (ts, w), 0)
        ca = jnp.where(row < ts - 1, pltpu.roll(a, ts - 1, 0), 1.0)
        cb = dh
        d = 1
        while d < ts:
            an = jnp.where(row < ts - d, pltpu.roll(ca, ts - d, 0), 1.0)
            bn = jnp.where(row < ts - d, pltpu.roll(cb, ts - d, 0), 0.0)
            cb = cb + ca * bn
            ca = ca * an
            d *= 2
        lamb = cb + ca * carry[0:1, :]
        carry[0:1, :] = a[0:1, :] * lamb[0:1, :]
        h_before = _shift_rows(h, hprev, 1)
        da = lamb * h_before
        ix = ig * xc
        dmult = lamb * ix
        dig = lamb * mult * xc
        dxc = lamb * mult * ig
        dla = (da - dmult * a / mult) * a
        dr = dla * LRU_C * ls
        dlam_ref[...] += jnp.sum(dla * LRU_C * r, axis=0, keepdims=True) * _sigmoid(-lam_ref[...])
        dpa = dr * r * (1.0 - r)
        dpx = dig * ig * (1.0 - ig)
        dba_ref[...] += jnp.sum(dpa, axis=0, keepdims=True)
        dbx_ref[...] += jnp.sum(dpx, axis=0, keepdims=True)
        dpab = dpa.astype(BF16)
        dpxb = dpx.astype(BF16)
        xcb = xc.astype(BF16)
        dxc = dxc + _mm_nt(dpab, wa_ref[...].astype(BF16)) + _mm_nt(dpxb, wx_ref[...].astype(BF16))
        dwa_ref[...] += _mm_tn(xcb, dpab)
        dwx_ref[...] += _mm_tn(xcb, dpxb)
        dcb_ref[...] += jnp.sum(dxc, axis=0, keepdims=True)
        nxt = dxc_next[...]
        dx = cw_ref[3:4, :] * dxc
        dcw_ref[3:4, :] += jnp.sum(dxc * x, axis=0, keepdims=True)
        for j in (1, 2, 3):
            dx = dx + cw_ref[3 - j:4 - j, :] * _shift_rows_up(dxc, nxt, j)
            dcw_ref[3 - j:4 - j, :] += jnp.sum(dxc * _shift_rows(x, xprev, j), axis=0, keepdims=True)
        dxc_next[...] = dxc
        dhl_ref[:, 0:w] = dx.astype(BF16)
        dhl_ref[:, w:2 * w] = dgate.astype(BF16)

    cst = lambda shape: pl.BlockSpec(shape, lambda i: (0, 0))
    rowr = lambda width: pl.BlockSpec((ts, width), lambda i: (rev(i), 0))
    prevr = lambda width: pl.BlockSpec((ts, width), lambda i: (jnp.maximum(rev(i) - 1, 0), 0))
    return pl.pallas_call(
        body, name="lru_bwd", grid=(nb,),
        in_specs=[rowr(2 * w), prevr(w), cst((4, w)), cst((1, w)), cst((w, w)), cst((1, w)), cst((w, w)), cst((1, w)), cst((1, w)),
                  rowr(w), rowr(w), prevr(w), rowr(w)],
        out_specs=[rowr(2 * w), cst((4, w)), cst((1, w)), cst((w, w)), cst((1, w)), cst((w, w)), cst((1, w)), cst((1, w))],
        out_shape=[jax.ShapeDtypeStruct((s, 2 * w), BF16), jax.ShapeDtypeStruct((4, w), F32), jax.ShapeDtypeStruct((1, w), F32),
                   jax.ShapeDtypeStruct((w, w), F32), jax.ShapeDtypeStruct((1, w), F32), jax.ShapeDtypeStruct((w, w), F32),
                   jax.ShapeDtypeStruct((1, w), F32), jax.ShapeDtypeStruct((1, w), F32)],
        scratch_shapes=[pltpu.VMEM((8, w), F32), pltpu.VMEM((ts, w), F32)],
        compiler_params=_params("arbitrary"),
    )(hl, hl, conv_w, conv_b, w_a, b_a, w_x, b_x, lam, xc_saved, h_saved, h_saved, dout)


GDN_QKV = 3 * GDN_W
GDN_STEP_CHUNKS = 8
GDN_BWD_STEP_CHUNKS = 2


def _tri_inverse_many(nms):
    r = lax.broadcasted_iota(jnp.int32, nms[0].shape, 0)
    c = lax.broadcasted_iota(jnp.int32, nms[0].shape, 1)
    eye = (r == c).astype(F32)
    ts = [eye - nm for nm in nms]
    ps = list(nms)
    for _ in range(5):
        ps = [_mm3(p, p) for p in ps]
        ts = [t + _mm3(t, p) for t, p in zip(ts, ps)]
    return ts


def _gdn_front(hx_ref, hprev, cw_ref, al_ref, dt_ref):
    w = GDN_W
    x = hx_ref[:, 0:GDN_QKV]
    y = cw_ref[3:4, :] * x
    for j in (1, 2, 3):
        y = y + cw_ref[3 - j:4 - j, :] * _shift_rows(x, hprev, j)
    qkv, dsil = _silu_and_grad(y)
    q, k, v = qkv[:, 0:w], qkv[:, w:2 * w], qkv[:, 2 * w:3 * w]
    rq = lax.rsqrt(_group_sum(q * q) + 1e-6)
    rk = lax.rsqrt(_group_sum(k * k) + 1e-6)
    beta = _sigmoid(hx_ref[:, 5 * w:6 * w])
    sp_in = hx_ref[:, 4 * w:5 * w] + dt_ref[...]
    neg_a = -jnp.exp(al_ref[...])
    g = neg_a * _softplus(sp_in)
    n_c = g.shape[0] // CHUNK
    gc = jnp.concatenate([_rows_prefix_sum(g[CHUNK * c:CHUNK * (c + 1)]) for c in range(n_c)], axis=0)
    return dict(x=x, dsil=dsil, qn=q * rq, kn=k * rk, v=v, rq=rq, rk=rk, beta=beta, sp_in=sp_in, neg_a=neg_a, g=g, gc=gc)


def _stack_heads(x):
    return jnp.concatenate([x * _head_mask(0), x * _head_mask(1)], axis=0)


def _unstack_heads(y):
    return y[0:CHUNK] + y[CHUNK:2 * CHUNK]


def _head_transpose(x):
    return jnp.concatenate([x[:, 0:HEAD].T, x[:, HEAD:2 * HEAD].T], axis=1)


def _head_total(x):
    cols = jnp.broadcast_to(jnp.sum(x, axis=0, keepdims=True), (8, LANES))
    return _group_sum_slab(cols)[0:1]


def _slab_tri_masks():
    r = lax.broadcasted_iota(jnp.int32, (CHUNK, LANES), 0)
    c = lax.broadcasted_iota(jnp.int32, (CHUNK, LANES), 1) & (HEAD - 1)
    return r >= c, r > c


def _gdn_slab(fr, c, sl, tri, transposed=False):
    lower, strict = tri
    ls = lambda a: a[CHUNK * c:CHUNK * (c + 1), LANES * sl:LANES * (sl + 1)]
    k = ls(fr["kn"])
    q = ls(fr["qn"]) * (HEAD ** -0.5)
    v = ls(fr["v"])
    beta = ls(fr["beta"])
    gc = ls(fr["gc"])
    e = jnp.exp(gc)
    gl = gc[CHUNK - 1:CHUNK, :]
    xt = jnp.exp(gl - gc)
    gc_t = _head_transpose(gc)
    dec = jnp.where(lower, jnp.exp(jnp.minimum(gc - gc_t, 0.0)), 0.0)
    kbd = _stack_heads(k)
    kk = _mm_nt(k, kbd)
    qkr = _mm_nt(q, kbd)
    out = dict(k=k, q=q, v=v, beta=beta, e=e, egl=jnp.exp(gl), xt=xt, dec=dec, kk=kk, qkr=qkr, kbd=kbd,
               nm=jnp.where(strict, beta * kk * dec, 0.0))
    if transposed:
        r = lax.broadcasted_iota(jnp.int32, (CHUNK, LANES), 0)
        col = lax.broadcasted_iota(jnp.int32, (CHUNK, LANES), 1) & (HEAD - 1)
        qbd = _stack_heads(q)
        out.update(dec_t=jnp.where(r <= col, jnp.exp(jnp.minimum(gc_t - gc, 0.0)), 0.0), beta_t=_head_transpose(beta),
                   qbd=qbd, kqr=_mm_nt(k, qbd), strict_t=r < col)
    return out


def gdn_fwd(hx, conv_w, a_log_e, dt_bias_e, norm_g_e, fused=None):
    s = hx.shape[0]
    n_chunks = s // CHUNK
    w = GDN_W
    n_slab = w // LANES
    gch = min(GDN_STEP_CHUNKS, n_chunks)

    def body(hx_ref, hp_ref, cw_ref, al_ref, dt_ref, ng_ref, o_ref, opre_ref, t_ref, st_ref, state):
        n = pl.program_id(0)

        @pl.when(n == 0)
        def _():
            state[...] = jnp.zeros_like(state)

        fr = _gdn_front(hx_ref, hp_ref[...] * (n > 0).astype(F32), cw_ref, al_ref, dt_ref)
        tri = _slab_tri_masks()
        bd = _block_diag_mask()
        sts = [state[LANES * sl:LANES * (sl + 1), :] for sl in range(n_slab)]
        slabs = [[_gdn_slab(fr, c, sl, tri) for sl in range(n_slab)] for c in range(gch)]
        tbd = _tri_inverse_many([_stack_heads(sq["nm"]) for row_ in slabs for sq in row_])
        o_rows = []
        for c in range(gch):
            ts, outs = [], []
            st_ref[w * c:w * (c + 1), :] = jnp.concatenate(sts, axis=0)
            for sl in range(n_slab):
                sq = slabs[c][sl]
                t = _unstack_heads(tbd[n_slab * c + sl])
                ts.append(t)
                u = _mm(t, _stack_heads(sq["v"] * sq["beta"]))
                wk = _mm(t, _stack_heads(sq["k"] * (sq["beta"] * sq["e"])))
                st = sts[sl]
                vnew = u - _mm(wk, st)
                outs.append(_mm(sq["q"] * sq["e"], st) + _mm(sq["qkr"] * sq["dec"], _stack_heads(vnew)))
                sts[sl] = st * sq["egl"] + _mm_tn(sq["k"] * sq["xt"], vnew) * bd
            t_ref[CHUNK * c:CHUNK * (c + 1), :] = jnp.concatenate(ts, axis=1)
            o_rows.append(jnp.concatenate(outs, axis=1))
        state[...] = jnp.concatenate(sts, axis=0)
        o = jnp.concatenate(o_rows, axis=0)
        opre_ref[...] = o
        rinv = lax.rsqrt(_group_sum(o * o) * (1.0 / HEAD) + 1e-6)
        z = hx_ref[:, 3 * w:4 * w]
        o_ref[...] = (o * rinv) * ng_ref[...] * (z * _sigmoid(z))

    cst = lambda shape: pl.BlockSpec(shape, lambda i: (0, 0))
    row = lambda width: pl.BlockSpec((CHUNK * gch, width), lambda i: (i, 0))
    f_in, f_out, f_scr, _ = fused if fused is not None else ([], [], [], None)
    outs = pl.pallas_call(
        _fuse(body, 6, 4, 1, n_chunks // gch, fused), name="gdn_fwd" + ("_gather" if fused is not None else ""),
        grid=(n_chunks // gch,),
        in_specs=[row(GDN_IN), pl.BlockSpec((CHUNK * gch, GDN_QKV), lambda i: (jnp.maximum(i - 1, 0), 0)),
                  cst((4, GDN_QKV)), cst((1, w)), cst((1, w)), cst((1, w))] + [_ANY] * len(f_in),
        out_specs=[row(w)] * 3 + [pl.BlockSpec((w * gch, LANES), lambda i: (i, 0))] + [_ANY] * len(f_out),
        out_shape=[jax.ShapeDtypeStruct((s, w), F32)] * 3 + [jax.ShapeDtypeStruct((n_chunks * w, LANES), F32)] + list(f_out),
        scratch_shapes=[pltpu.VMEM((w, LANES), F32)] + list(f_scr),
        compiler_params=_params("arbitrary"),
    )(hx, hx, conv_w, a_log_e, dt_bias_e, norm_g_e, *f_in)
    return tuple(outs[:4]) + ((list(outs[4:]),) if fused is not None else ())


def gdn_bwd(hx, conv_w, a_log_e, dt_bias_e, norm_g_e, opre, tmat, states, dout, fused=None):
    s = hx.shape[0]
    n_chunks = s // CHUNK
    w = GDN_W
    n_slab = w // LANES
    gch = min(GDN_BWD_STEP_CHUNKS, n_chunks)
    n_blocks = n_chunks // gch
    rev = lambda i: n_blocks - 1 - i

    def body(hx_ref, hp_ref, cw_ref, al_ref, dt_ref, ng_ref, opre_ref, t_ref, st_ref, do_ref,
             dhx_ref, dab_ref, dcw_ref, dal_ref, ddt_ref, dng_ref, dstate, dy_next):
        i = pl.program_id(0)
        n = n_blocks - 1 - i

        @pl.when(i == 0)
        def _():
            dstate[...] = jnp.zeros_like(dstate)
            dy_next[...] = jnp.zeros_like(dy_next)
            for ref in (dcw_ref, dal_ref, ddt_ref, dng_ref):
                ref[...] = jnp.zeros_like(ref)

        hprev = hp_ref[...] * (n > 0).astype(F32)
        fr = _gdn_front(hx_ref, hprev, cw_ref, al_ref, dt_ref)
        tri = _slab_tri_masks()
        lower, strict = tri
        o = opre_ref[...]
        rinv = lax.rsqrt(_group_sum(o * o) * (1.0 / HEAD) + 1e-6)
        yn = o * rinv
        z = hx_ref[:, 3 * w:4 * w]
        sil, dsil_z = _silu_and_grad(z)
        dout_v = do_ref[...]
        ng = ng_ref[...]
        dng_ref[...] += jnp.sum(dout_v * yn * sil, axis=0, keepdims=True)
        dz = dout_v * yn * ng * dsil_z
        dyn = dout_v * ng * sil
        d_o = rinv * (dyn - yn * (_group_sum(dyn * yn) * (1.0 / HEAD)))
        last_row = (lax.broadcasted_iota(jnp.int32, (CHUNK, LANES), 0) == CHUNK - 1).astype(F32)
        bd = _block_diag_mask()
        gsum = _group_sum_slab
        t_all, st_all = t_ref[...], st_ref[...]
        dsns = [dstate[LANES * sl:LANES * (sl + 1), :] for sl in range(n_slab)]
        per_chunk = {}
        order = [(c_, s_) for c_ in reversed(range(gch)) for s_ in range(n_slab)]
        chain = {}
        for c, sl in order:
            lanes = slice(LANES * sl, LANES * (sl + 1))
            tok = slice(CHUNK * c, CHUNK * (c + 1))
            sq = _gdn_slab(fr, c, sl, tri, transposed=True)
            t = t_all[tok, lanes]
            st = st_all[w * c + LANES * sl:w * c + LANES * (sl + 1), :]
            dsn = dsns[sl]
            do_s = d_o[tok, lanes]
            u = _mm(t, _stack_heads(sq["v"] * sq["beta"]))
            wk = _mm(t, _stack_heads(sq["k"] * (sq["beta"] * sq["e"])))
            kt = sq["k"] * sq["xt"]
            dobd = _stack_heads(do_s)
            dvnew = _mm(sq["kqr"] * sq["dec_t"], dobd) + _mm(kt, dsn)
            dsns[sl] = _mm_tn(sq["q"] * sq["e"], do_s) * bd + sq["egl"] * dsn - _mm_tn(wk, dvnew) * bd
            chain[(c, sl)] = (sq, t, st, dsn, do_s, dobd, u, wk, kt, dvnew)
        for c, sl in order:
            sq, t, st, dsn, do_s, dobd, u, wk, kt, dvnew = chain[(c, sl)]
            k, q, v, beta, e, xt, dec, kk, qkr, kbd = (sq[n_] for n_ in ("k", "q", "v", "beta", "e", "xt", "dec", "kk", "qkr", "kbd"))
            dec_t, beta_t, kqr, qbd = sq["dec_t"], sq["beta_t"], sq["kqr"], sq["qbd"]
            t_t = _head_transpose(t)
            vnew = u - _mm(wk, st)
            dqd = _mm_nt(do_s, st)
            dqk = _mm_nt(do_s, _stack_heads(vnew))
            dqk_t = _mm_nt(vnew, dobd)
            dkt = _mm_nt(vnew, dsn)
            dgl = _head_total(dsn * st) * sq["egl"]
            dwk = -_mm_nt(dvnew, st)
            drv = _mm(t_t, _stack_heads(dvnew))
            drk = _mm(t_t, _stack_heads(dwk))
            dnm = jnp.where(strict, -(_mm_nt(drv, _stack_heads(u)) + _mm_nt(drk, _stack_heads(wk))), 0.0)
            dnm_t = jnp.where(sq["strict_t"], -(_mm_nt(u, _stack_heads(drv)) + _mm_nt(wk, _stack_heads(drk))), 0.0)
            dbeta = gsum(dnm * kk * dec)
            dkk = dnm * beta * dec
            dkk_t = dnm_t * beta_t * dec_t
            ddec = dnm * beta * kk + dqk * qkr
            dd_t = (dnm_t * beta_t * kk + dqk_t * kqr) * dec_t
            dq = _mm(dqk * dec, kbd) + dqd * e
            dk = _mm(dqk_t * dec_t, qbd) + _mm(dkk + dkk_t, kbd) + drk * (beta * e) + dkt * xt
            rks = gsum(drk * k)
            dbeta = dbeta + gsum(drv * v) + rks * e
            de = rks * beta + gsum(dqd * q)
            dxt = gsum(dkt * k) * xt
            dgl = dgl + jnp.sum(dxt, axis=0, keepdims=True)
            dgc = de * e - dxt + gsum(ddec * dec) - gsum(dd_t) + last_row * dgl
            per_chunk[(c, sl)] = dict(dq=dq * (HEAD ** -0.5), dk=dk, dv=drv * beta, dbeta=dbeta, dgc=dgc)
        for sl in range(n_slab):
            dstate[LANES * sl:LANES * (sl + 1), :] = dsns[sl]

        def block_of(name, suffix_sum=False):
            rows = []
            for c in range(gch):
                r = jnp.concatenate([per_chunk[(c, sl)][name] for sl in range(n_slab)], axis=1)
                rows.append(_rows_suffix_sum(r) if suffix_sum else r)
            return jnp.concatenate(rows, axis=0)

        dg = block_of("dgc", suffix_sum=True)
        dal_ref[...] += jnp.sum(dg * fr["g"], axis=0, keepdims=True)
        da = dg * fr["neg_a"] * _sigmoid(fr["sp_in"])
        ddt_ref[...] += jnp.sum(da, axis=0, keepdims=True)
        beta_all = fr["beta"]
        db = block_of("dbeta") * beta_all * (1.0 - beta_all)
        lane = lax.broadcasted_iota(jnp.int32, (CHUNK * gch, LANES), 1)
        dab = jnp.zeros((CHUNK * gch, LANES), F32)
        for hd in range(GDN_HEADS):
            dab = jnp.where(lane == hd, da[:, HEAD * hd:HEAD * hd + 1], dab)
            dab = jnp.where(lane == GDN_HEADS + hd, db[:, HEAD * hd:HEAD * hd + 1], dab)
        dab_ref[...] = dab.astype(BF16)
        dqn = block_of("dq")
        dkn = block_of("dk")
        dq_raw = fr["rq"] * (dqn - fr["qn"] * _group_sum(dqn * fr["qn"]))
        dk_raw = fr["rk"] * (dkn - fr["kn"] * _group_sum(dkn * fr["kn"]))
        dy = jnp.concatenate([dq_raw, dk_raw, block_of("dv")], axis=1) * fr["dsil"]
        nxt = dy_next[...]
        x = fr["x"]
        dx = cw_ref[3:4, :] * dy
        dcw_ref[3:4, :] += jnp.sum(dy * x, axis=0, keepdims=True)
        for j in (1, 2, 3):
            dx = dx + cw_ref[3 - j:4 - j, :] * _shift_rows_up(dy, nxt, j)
            dcw_ref[3 - j:4 - j, :] += jnp.sum(dy * _shift_rows(x, hprev, j), axis=0, keepdims=True)
        dy_next[...] = dy
        dhx_ref[:, 0:GDN_QKV] = dx.astype(BF16)
        dhx_ref[:, 3 * w:4 * w] = dz.astype(BF16)

    cst = lambda shape: pl.BlockSpec(shape, lambda i: (0, 0))
    row = lambda width: pl.BlockSpec((CHUNK * gch, width), lambda i: (rev(i), 0))
    buf = lambda width: pltpu.VMEM((CHUNK * gch, width), F32)
    f_in, f_out, f_scr, _ = fused if fused is not None else ([], [], [], None)
    outs = pl.pallas_call(
        _fuse(body, 10, 6, 2, n_blocks, fused), name="gdn_bwd" + ("_exchange" if fused is not None else ""), grid=(n_blocks,),
        in_specs=[row(GDN_IN), pl.BlockSpec((CHUNK * gch, GDN_QKV), lambda i: (jnp.maximum(rev(i) - 1, 0), 0)),
                  cst((4, GDN_QKV)), cst((1, w)), cst((1, w)), cst((1, w)), row(w), row(w),
                  pl.BlockSpec((w * gch, LANES), lambda i: (rev(i), 0)), row(w)] + [_ANY] * len(f_in),
        out_specs=[row(4 * w), row(LANES), cst((4, GDN_QKV)), cst((1, w)), cst((1, w)), cst((1, w))] + [_ANY] * len(f_out),
        out_shape=[jax.ShapeDtypeStruct((s, 4 * w), BF16), jax.ShapeDtypeStruct((s, LANES), BF16),
                   jax.ShapeDtypeStruct((4, GDN_QKV), F32),
                   jax.ShapeDtypeStruct((1, w), F32), jax.ShapeDtypeStruct((1, w), F32), jax.ShapeDtypeStruct((1, w), F32)]
        + list(f_out),
        scratch_shapes=[pltpu.VMEM((w, LANES), F32), buf(GDN_QKV)] + list(f_scr),
        compiler_params=_params("arbitrary"),
    )(hx, hx, conv_w, a_log_e, dt_bias_e, norm_g_e, opre, tmat, states, dout, *f_in)
    return tuple(outs[:6]) + ((list(outs[6:]),) if fused is not None else ())


_MESH = pl.DeviceIdType.MESH


def all_gather8(x, name):
    m, n = x.shape

    def body(x_ref, out_ref, send_sems, recv_sems, local_sem):
        px, py, pc = lax.axis_index("x"), lax.axis_index("y"), lax.axis_index("c")
        me, sibling = (px, py, pc), (px, py, 1 - pc)
        chips = [(1 - px, py), (px, 1 - py), (1 - px, 1 - py)]

        def slot(dx, dy, dc):
            return out_ref.at[4 * dx + 2 * dy + dc]

        def copy(k, block, to, src=None):
            return pltpu.make_async_remote_copy(
                src_ref=slot(*block) if src is None else src, dst_ref=slot(*block),
                send_sem=send_sems.at[k], recv_sem=recv_sems.at[k], device_id=to, device_id_type=_MESH)

        mine = pltpu.make_async_copy(x_ref, slot(*me), local_sem)
        mine.start()
        first = [copy(0, me, sibling, src=x_ref)]
        first += [copy(1 + j, me, (*chip, pc), src=x_ref) for j, chip in enumerate(chips)]
        for cp in first:
            cp.start()
        passed = [copy(4 + j, (*chip, pc), sibling) for j, chip in enumerate(chips)]
        for j, chip in enumerate(chips):
            copy(1 + j, (*chip, pc), me).wait_recv()
            passed[j].start()
        copy(0, sibling, me).wait_recv()
        for j, chip in enumerate(chips):
            copy(4 + j, (*chip, 1 - pc), me).wait_recv()
        for cp in first + passed:
            cp.wait_send()
        mine.wait()

    return pl.pallas_call(
        body, name=name, out_shape=jax.ShapeDtypeStruct((N_DEV, m, n), x.dtype),
        in_specs=[_ANY], out_specs=_ANY,
        scratch_shapes=[pltpu.SemaphoreType.DMA((7,)), pltpu.SemaphoreType.DMA((7,)), pltpu.SemaphoreType.DMA],
    )(x)


def _weight_gather_steps(s0, s1, f0, f1, sems):
    n0 = len(s0)
    own_send, own_recv, ici_send, ici_recv, fwd_send, fwd_recv = sems
    px, py, pc = lax.axis_index("x"), lax.axis_index("y"), lax.axis_index("c")
    mine = 2 * px + py
    sibling = (px, py, 1 - pc)
    chips = [(1 - px, py), (px, 1 - py), (1 - px, 1 - py)]

    def copy(src, dst, sems_s, sems_r, k, to):
        return pltpu.make_async_remote_copy(src_ref=src, dst_ref=dst, send_sem=sems_s.at[k], recv_sem=sems_r.at[k],
                                            device_id=to, device_id_type=_MESH)

    def own_copies():
        return [copy(shards[i], full[i].at[mine], own_send, own_recv, base + i, sibling)
                for base, shards, full in ((0, s0, f0), (n0, s1, f1)) for i in range(len(shards))]

    def first_copies(my_shards, my_full):
        ici = [copy(my_shards[i], my_full[i].at[mine], ici_send, ici_recv, 3 * i + j, (cx, cy, pc))
               for i in range(len(my_shards)) for j, (cx, cy) in enumerate(chips)]
        return own_copies() + ici

    def begin(my_shards, my_full):
        for cp in first_copies(my_shards, my_full):
            cp.start()

    def forwards(my_shards, my_full):
        return [copy(my_full[i].at[2 * cx + cy], my_full[i].at[2 * cx + cy], fwd_send, fwd_recv, 3 * i + j, sibling)
                for i in range(len(my_shards)) for j, (cx, cy) in enumerate(chips)]

    def hand_on(my_shards, my_full):
        fwd = forwards(my_shards, my_full)
        for i in range(len(my_shards)):
            for j, (cx, cy) in enumerate(chips):
                copy(my_shards[i], my_full[i].at[2 * cx + cy], ici_send, ici_recv, 3 * i + j, (cx, cy, pc)).wait_recv()
                fwd[3 * i + j].start()

    def end(my_shards, my_full, other_full):
        for cp in own_copies():
            cp.wait_recv()
        for i in range(len(other_full)):
            for j, (cx, cy) in enumerate(chips):
                slot = other_full[i].at[2 * cx + cy]
                copy(slot, slot, fwd_send, fwd_recv, 3 * i + j, sibling).wait_recv()
        for cp in first_copies(my_shards, my_full) + forwards(my_shards, my_full):
            cp.wait_send()

    def on_core(fn0, fn1):
        def run():
            @pl.when(pc == 0)
            def _():
                fn0()

            @pl.when(pc == 1)
            def _():
                fn1()
        return run

    start = on_core(lambda: begin(s0, f0), lambda: begin(s1, f1))
    middle = on_core(lambda: hand_on(s0, f0), lambda: hand_on(s1, f1))
    finish = on_core(lambda: end(s0, f0, f1), lambda: end(s1, f1, f0))
    return start, middle, finish


def _weight_gather_operands(shards0, shards1):
    both, most = len(shards0) + len(shards1), max(len(shards0), len(shards1))
    full = [jax.ShapeDtypeStruct((N_CHIPS,) + v.shape, v.dtype) for v in list(shards0) + list(shards1)]
    dma = pltpu.SemaphoreType.DMA
    return full, [dma((both,)), dma((both,)), dma((3 * most,)), dma((3 * most,)), dma((3 * most,)), dma((3 * most,))]


def _weight_gather_fused(shards0, shards1):
    n0, both = len(shards0), len(shards0) + len(shards1)
    full, sems = _weight_gather_operands(shards0, shards1)
    steps = lambda ins, outs, scr: _weight_gather_steps(ins[:n0], ins[n0:both], outs[:n0], outs[n0:both], scr)
    return list(shards0) + list(shards1), full, sems, steps


def gather_layer_weights(shards0, shards1, name):
    ins, full, sems, steps = _weight_gather_fused(shards0, shards1)
    both = len(ins)

    def body(*refs):
        start, middle, finish = steps(refs[0:both], refs[both:2 * both], refs[2 * both:])
        start()
        middle()
        finish()

    return pl.pallas_call(
        body, name=name, out_shape=full, in_specs=[_ANY] * both, out_specs=[_ANY] * both, scratch_shapes=sems,
    )(*ins)


def _piece_offsets(pieces):
    offs = [0]
    for r, _ in pieces:
        offs.append(offs[-1] + r)
    return offs


def _chip_exchange_steps(srcs, q_ref, sems, pieces, owner):
    send_sems, recv_sems = sems
    offs = _piece_offsets(pieces)
    px, py, pc = lax.axis_index("x"), lax.axis_index("y"), lax.axis_index("c")
    mine = 2 * px + py
    chips = [(1 - px, py), (px, 1 - py), (1 - px, 1 - py)]

    def copies():
        sends = []
        for i, (r, stride) in enumerate(pieces):
            dst = pl.ds(offs[i], r)
            for j, (cx, cy) in enumerate(chips):
                sends.append(pltpu.make_async_remote_copy(
                    src_ref=srcs[i].at[pl.ds((2 * cx + cy) * stride, r)], dst_ref=q_ref.at[mine, dst],
                    send_sem=send_sems.at[3 * i + j], recv_sem=recv_sems.at[3 * i + j], device_id=(cx, cy, pc),
                    device_id_type=_MESH))
        return sends

    def start():
        @pl.when(pc == owner)
        def _():
            for cp in copies():
                cp.start()

    def finish():
        @pl.when(pc == owner)
        def _():
            for i, (r, stride) in enumerate(pieces):
                dst = pl.ds(offs[i], r)
                for j, (cx, cy) in enumerate(chips):
                    pltpu.make_async_remote_copy(
                        src_ref=srcs[i].at[pl.ds(mine * stride, r)], dst_ref=q_ref.at[2 * cx + cy, dst],
                        send_sem=send_sems.at[3 * i + j], recv_sem=recv_sems.at[3 * i + j], device_id=(cx, cy, pc),
                        device_id_type=_MESH).wait_recv()
            for cp in copies():
                cp.wait_send()

    return start, finish


def _chip_exchange_operands(arrays, pieces):
    n = len(pieces)
    dma = pltpu.SemaphoreType.DMA
    q = jax.ShapeDtypeStruct((N_CHIPS, _piece_offsets(pieces)[-1], arrays[0].shape[1]), arrays[0].dtype)
    return q, [dma((3 * n,)), dma((3 * n,))]


def _own_share(arrays, pieces, chip):
    return jnp.concatenate([lax.dynamic_slice_in_dim(arr, chip * stride, r, axis=0) for arr, (r, stride) in zip(arrays, pieces)],
                           axis=0)


def chip_exchange(arrays, pieces, owner, name):
    n = len(pieces)

    def body(*refs):
        start, finish = _chip_exchange_steps(refs[0:n], refs[n], refs[n + 1:], pieces, owner)
        start()
        finish()

    q, sems = _chip_exchange_operands(arrays, pieces)
    return pl.pallas_call(body, name=name, out_shape=q, in_specs=[_ANY] * n, out_specs=_ANY, scratch_shapes=sems)(*arrays)


def sibling_send(arrays, to_core, name):
    n = len(arrays)

    def body(*refs):
        srcs, outs = refs[0:n], refs[n:2 * n]
        send_sems, recv_sems = refs[2 * n:]
        px, py, pc = lax.axis_index("x"), lax.axis_index("y"), lax.axis_index("c")
        cps = [pltpu.make_async_remote_copy(
            src_ref=srcs[i], dst_ref=outs[i], send_sem=send_sems.at[i], recv_sem=recv_sems.at[i],
            device_id=(px, py, to_core), device_id_type=_MESH) for i in range(n)]

        @pl.when(pc != to_core)
        def _():
            for cp in cps:
                cp.start()
            for cp in cps:
                cp.wait_send()

        @pl.when(pc == to_core)
        def _():
            for cp in cps:
                cp.wait_recv()

    return pl.pallas_call(
        body, name=name, out_shape=[jax.ShapeDtypeStruct(v.shape, v.dtype) for v in arrays],
        in_specs=[_ANY] * n, out_specs=[_ANY] * n,
        scratch_shapes=[pltpu.SemaphoreType.DMA((n,)), pltpu.SemaphoreType.DMA((n,))],
    )(*arrays)


def _fuse(body, n_in, n_out, n_scratch, n_steps, fused):
    if fused is None:
        return body
    f_in, f_out, f_scr, steps = fused
    a, b, c = len(f_in), len(f_out), len(f_scr)
    late = (n_steps * FUSED_LATE_PERCENT) // 100

    def wrapped(*refs):
        ins, rest = refs[:n_in + a], refs[n_in + a:]
        outs, scr = rest[:n_out + b], rest[n_out + b:]
        start, middle, finish = steps(ins[n_in:], outs[n_out:], scr[n_scratch:])
        step = pl.program_id(0)

        @pl.when(step == 0)
        def _():
            start()

        body(*ins[:n_in], *outs[:n_out], *scr[:n_scratch])

        @pl.when(step == late)
        def _():
            middle()

        @pl.when(step == n_steps - 1)
        def _():
            finish()

    return wrapped


def sibling_swap(x, name):
    def body(x_ref, out_ref, send_sem, recv_sem):
        px, py, pc = lax.axis_index("x"), lax.axis_index("y"), lax.axis_index("c")
        cp = pltpu.make_async_remote_copy(
            src_ref=x_ref, dst_ref=out_ref, send_sem=send_sem, recv_sem=recv_sem,
            device_id=(px, py, 1 - pc), device_id_type=_MESH)
        cp.start()
        cp.wait()

    return pl.pallas_call(
        body, name=name, out_shape=jax.ShapeDtypeStruct(x.shape, x.dtype), in_specs=[_ANY], out_specs=_ANY,
        scratch_shapes=[pltpu.SemaphoreType.DMA, pltpu.SemaphoreType.DMA],
    )(x)


ELT_TILE = 128


def _elt_rows(m, big=False):
    for t in ((1728, 1408, 1024, 640) if big else ()) + (512, 256, ELT_TILE, 16, 8):
        if m % t == 0:
            return t
    return m


def pair_add(a, b, name):
    m, n = b.shape
    tm = _elt_rows(m, big=True)

    def body(a_ref, b_ref, o_ref):
        o_ref[...] = (a_ref[...].astype(F32) + b_ref[...].astype(F32)).astype(o_ref.dtype)

    return pl.pallas_call(
        body, name=name, grid=(m // tm,), in_specs=[_row_spec(tm, n)] * 2, out_specs=_row_spec(tm, n),
        out_shape=jax.ShapeDtypeStruct((m, n), b.dtype), compiler_params=_params("arbitrary"),
    )(a, b)


def sum_leading(q, name):
    kk, m, n = q.shape
    tm = _elt_rows(m)

    def body(q_ref, o_ref):
        acc = q_ref[0].astype(F32)
        for i in range(1, kk):
            acc = acc + q_ref[i].astype(F32)
        o_ref[...] = acc

    return pl.pallas_call(
        body, name=name, grid=(m // tm,), in_specs=[pl.BlockSpec((kk, tm, n), lambda i: (0, i, 0))],
        out_specs=_row_spec(tm, n), out_shape=jax.ShapeDtypeStruct((m, n), F32), compiler_params=_params("arbitrary"),
    )(q)


def sum_shares(q0, q1, own0, own1, name):
    _, m, n = q0.shape
    tm = next((t for t in (640,) if m % t == 0), _elt_rows(m))

    def body(q0_ref, q1_ref, o0_ref, o1_ref, out_ref):
        first = lax.axis_index("c") == 0
        mine = 2 * lax.axis_index("x") + lax.axis_index("y")
        own = jnp.where(first, o0_ref[...], o1_ref[...])
        acc = None
        for k in range(N_CHIPS):
            term = jnp.where(mine == k, own, jnp.where(first, q0_ref[k], q1_ref[k])).astype(F32)
            acc = term if acc is None else acc + term
        out_ref[...] = acc

    slots = pl.BlockSpec((N_CHIPS, tm, n), lambda i: (0, i, 0))
    return pl.pallas_call(
        body, name=name, grid=(m // tm,), in_specs=[slots, slots, _row_spec(tm, n), _row_spec(tm, n)],
        out_specs=_row_spec(tm, n), out_shape=jax.ShapeDtypeStruct((m, n), F32), compiler_params=_params("arbitrary"),
    )(q0, q1, own0, own1)


def adamw(w, g, m, v, name):
    rows, cols = w.shape
    tm = _elt_rows(rows)

    def body(w_ref, g_ref, m_ref, v_ref, d_ref, nm_ref, nv_ref):
        gv = g_ref[...]
        nm = ADAM_B1 * m_ref[...] + (1.0 - ADAM_B1) * gv
        nv = ADAM_B2 * v_ref[...] + (1.0 - ADAM_B2) * jnp.square(gv)
        nm_ref[...] = nm
        nv_ref[...] = nv
        m_hat = nm / (1.0 - ADAM_B1 ** ADAM_STEP)
        v_hat = nv / (1.0 - ADAM_B2 ** ADAM_STEP)
        d_ref[...] = -ADAM_LR * (m_hat / (jnp.sqrt(v_hat) + ADAM_EPS) + ADAM_WD * w_ref[...])

    spec = _row_spec(tm, cols)
    return pl.pallas_call(
        body, name=name, grid=(rows // tm,), in_specs=[spec] * 4, out_specs=[spec] * 3,
        out_shape=[jax.ShapeDtypeStruct((rows, cols), F32)] * 3, compiler_params=_params("arbitrary"),
    )(w, g, m, v)


def _block_diag_dense(w):
    g = w.shape[0]
    return jnp.einsum("gij,gh->gihj", w, jnp.eye(g, dtype=w.dtype)).reshape(g * w.shape[1], g * w.shape[2])


def _diag_blocks(m):
    return jnp.stack([m[HEAD * i:HEAD * (i + 1), HEAD * i:HEAD * (i + 1)] for i in range(LRU_BLOCKS)])


def _rep(v):
    return jnp.repeat(v, HEAD, axis=-1)


def _split_w_in(w_in):
    gdn0 = RET_IN + LRU_IN
    gdn1 = gdn0 + 4 * GDN_W
    w_r = w_in[:, 0:RET_IN]
    w_l = w_in[:, RET_IN:gdn0]
    w_g = jnp.concatenate([w_in[:, gdn0:gdn1], _rep(w_in[:, gdn1:gdn1 + GDN_HEADS]), _rep(w_in[:, gdn1 + GDN_HEADS:])], axis=1)
    w_ab = jnp.pad(w_in[:, gdn1:], ((0, 0), (0, LANES - 2 * GDN_HEADS)))
    return w_r, w_l, w_g, w_ab


WIN_SHARD = D_IN // N_CHIPS
WIN_STRIDE = 832
WIN_ROWS = 960
WIN_T_ROWS = WIN_STRIDE * (N_CHIPS - 1) + WIN_ROWS
AB_ROWS = 16

_GRAD_PIECES = (("ffn1_w_gate", 704, 704), ("ffn1_w_up", 704, 704), ("ffn1_w_down", 704, 704), ("w_in", WIN_ROWS, WIN_STRIDE),
                ("w_out", 256, 256), ("ffn2_w_gate", 704, 704), ("ffn2_w_up", 704, 704), ("ffn2_w_down", 704, 704),
                ("ple_w_gate", 256, 256), ("ple_w_proj", 64, 64))
_TRANSPOSED = ("ffn1_w_gate", "ffn1_w_up", "w_in", "ffn2_w_gate", "ffn2_w_up", "ple_w_proj")
FUSED_LATE_PERCENT = 88
N_EARLY = 5


def _local_step(x, p, pos, target, wt, mesh=None):
    row = lambda v: v[None, :]
    saved = []
    xb = x.astype(BF16)
    pieces = [(r, stride) for _, r, stride in _GRAD_PIECES]
    grad_names = [n for n, _, _ in _GRAD_PIECES]
    n_late = len(pieces) - N_EARLY
    for i in range(DEPTH):
        ffn1 = (wt["ffn1_w_gate"][i], wt["ffn1_w_up"][i], wt["ffn1_w_down"][i])
        if mesh is not None and i == 0:
            half0, half1, make = mesh["rest0"]
            hg1, hu1, r1, x1, x1b, gathered = ffn_fwd(x, row(wt["ln_ffn1_g"][i]), row(wt["ln_ffn1_b"][i]), *ffn1,
                                                      fused=_weight_gather_fused(half0, half1))
            wt = {**wt, **{n: [w0, None] for n, w0 in make(gathered).items()}}
        else:
            hg1, hu1, r1, x1, x1b = ffn_fwd(x, row(wt["ln_ffn1_g"][i]), row(wt["ln_ffn1_b"][i]), *ffn1)
        w_r, w_l, w_g, w_ab = _split_w_in(wt["w_in"][i])
        lw = dict(
            wg1=ffn1[0], wu1=ffn1[1], wd1=ffn1[2], w_r=w_r, w_l=w_l, w_g=w_g, w_ab=w_ab,
            w_out=wt["w_out"][i], wg2=wt["ffn2_w_gate"][i], wu2=wt["ffn2_w_up"][i], wd2=wt["ffn2_w_down"][i],
            wpg=wt["ple_w_gate"][i], wpp=wt["ple_w_proj"][i],
            wa=_block_diag_dense(wt["lru_w_a"][i]), wx=_block_diag_dense(wt["lru_w_x"][i]),
            al=row(_rep(wt["gdn_a_log"][i])), dt=row(_rep(wt["gdn_dt_bias"][i])), ng=row(jnp.tile(wt["gdn_norm_g"][i], GDN_HEADS)))
        hr, hl, hgd = win_fwd(x1, w_r, w_l, w_g)
        o_r, opre_r, st_r = ret_fwd(hr, pos, row(wt["ret_norm_g"][i]))
        o_l, xc, hs = lru_fwd(hl, wt["lru_conv_w"][i], row(wt["lru_conv_b"][i]), lw["wa"], row(wt["lru_b_a"][i]), lw["wx"],
                              row(wt["lru_b_x"][i]), row(wt["lru_lambda"][i]))
        if mesh is not None and i == 0:
            half0, half1, make = mesh["layer1"]
            o_g, opre_g, tmat, st_g, gathered = gdn_fwd(hgd, wt["gdn_conv_w"][i], lw["al"], lw["dt"], lw["ng"],
                                                        fused=_weight_gather_fused(half0, half1))
            wt = {**wt, **{n: [wt[n][0], w1] for n, w1 in make(gathered).items()}}
        else:
            o_g, opre_g, tmat, st_g = gdn_fwd(hgd, wt["gdn_conv_w"][i], lw["al"], lw["dt"], lw["ng"])
        r2, x2, x2b, ocat = out_fwd(o_r, o_l, o_g, x1, lw["w_out"], row(wt["ln_mix_g"][i]), row(wt["ln_mix_b"][i]))
        hg2, hu2, r3, x3, x3b, pg, pp = ffn_fwd(x2, row(wt["ln_ffn2_g"][i]), row(wt["ln_ffn2_b"][i]), lw["wg2"], lw["wu2"],
                                                lw["wd2"], ple=(p[i], lw["wpg"], lw["wpp"]))
        saved.append(dict(lw=lw, x0=xb, hg1=hg1, hu1=hu1, r1=r1, x1=x1b, hr=hr, hl=hl, hgd=hgd, ocat=ocat, opre_r=opre_r,
                          st_r=st_r, xc=xc, hs=hs, opre_g=opre_g, tmat=tmat, st_g=st_g, r2=r2, x2=x2b, hg2=hg2, hu2=hu2,
                          r3=r3, pg=pg, pp=pp))
        x, xb = x3, x3b

    dx, loss = loss_and_grad(x, target)
    grads = [None] * DEPTH
    big = [None] * DEPTH
    pair_sums = [None] * DEPTH
    for i in reversed(range(DEPTH)):
        sv = saved[i]
        lw = sv["lw"]
        tag = f"_l{i}"
        dx2, act2, dhg2, dhu2, dy2, dg3, db3, dpg, dpp = ffn_bwd(
            dx, sv["r3"], sv["x2"], sv["hg2"], sv["hu2"], row(wt["ln_ffn2_g"][i]), lw["wg2"], lw["wu2"], lw["wd2"],
            ple=(sv["pg"], sv["pp"], lw["wpg"]))
        g, bg = {}, {}
        bg["ffn2_w_gate"] = wgrad(dhg2, sv["x2"], "wgrad_gate2" + tag)
        bg["ffn2_w_up"] = wgrad(dhu2, sv["x2"], "wgrad_up2" + tag)
        bg["ffn2_w_down"] = wgrad(act2, dy2, "wgrad_down2" + tag)
        bg["ple_w_gate"] = wgrad(sv["x2"], dpg, "wgrad_pgate" + tag)
        bg["ple_w_proj"] = wgrad(dpp, p[i], "wgrad_pproj" + tag).reshape(PLE_DIM, D_MODEL)
        g["ln_ffn2_g"], g["ln_ffn2_b"] = dg3[0], db3[0]
        dr2, dr2b, do_r, do_l, do_g, dg2, db2 = out_bwd(dx2, sv["r2"], row(wt["ln_mix_g"][i]), lw["w_out"])
        g["ln_mix_g"], g["ln_mix_b"] = dg2[0], db2[0]
        bg["w_out"] = wgrad(sv["ocat"], dr2b, "wgrad_out" + tag)
        dhr, dgn = ret_bwd(sv["hr"], pos, row(wt["ret_norm_g"][i]), sv["opre_r"], sv["st_r"], do_r)
        g["ret_norm_g"] = dgn[0]
        dhl, dcw, dcb, dwa, dba, dwx, dbx, dlam = lru_bwd(
            sv["hl"], wt["lru_conv_w"][i], row(wt["lru_conv_b"][i]), lw["wa"], row(wt["lru_b_a"][i]), lw["wx"],
            row(wt["lru_b_x"][i]), row(wt["lru_lambda"][i]), sv["xc"], sv["hs"], do_l)
        g["lru_conv_w"], g["lru_conv_b"] = dcw, dcb[0]
        g["lru_w_a"], g["lru_b_a"], g["lru_w_x"], g["lru_b_x"], g["lru_lambda"] = _diag_blocks(dwa), dba[0], _diag_blocks(dwx), dbx[0], dlam[0]
        if mesh is not None and i == 0:
            early = [bg[n] for n in grad_names[n_late:]]
            early_sums = [pair_add(u, v, "reduce_pair_add_l0_" + n) for n, u, v in
                          zip(grad_names[n_late:], early, sibling_send(early, 0, "reduce_pair_send_l0_early"))]
            q1_shape, sems1 = _chip_exchange_operands(pair_sums[1], pieces)
            q0_shape, sems0 = _chip_exchange_operands(early_sums, pieces[n_late:])
            n1 = len(pieces)

            def steps(ins, outs, scr):
                start1, finish1 = _chip_exchange_steps(ins[:n1], outs[0], scr[:2], pieces, 1)
                start0, finish0 = _chip_exchange_steps(ins[n1:], outs[1], scr[2:], pieces[n_late:], 0)
                return (lambda: (start1(), start0())), (lambda: None), (lambda: (finish1(), finish0()))

            dhq, dab, dgcw, dal, ddt, dng, arrived = gdn_bwd(
                sv["hgd"], wt["gdn_conv_w"][i], lw["al"], lw["dt"], lw["ng"], sv["opre_g"], sv["tmat"], sv["st_g"], do_g,
                fused=(pair_sums[1] + early_sums, [q1_shape, q0_shape], sems1 + sems0, steps))
            big[1], early_arrived = arrived
        else:
            dhq, dab, dgcw, dal, ddt, dng = gdn_bwd(sv["hgd"], wt["gdn_conv_w"][i], lw["al"], lw["dt"], lw["ng"], sv["opre_g"],
                                                    sv["tmat"], sv["st_g"], do_g)
        g["gdn_conv_w"] = dgcw
        g["gdn_a_log"], g["gdn_dt_bias"] = dal[0, ::HEAD], ddt[0, ::HEAD]
        g["gdn_norm_g"] = dng[0].reshape(GDN_HEADS, HEAD).sum(0)
        dx1 = win_bwd(dr2, dhr, dhl, dhq, dab, lw["w_r"], lw["w_l"], lw["w_g"], lw["w_ab"])
        used = RET_IN + LRU_IN + 4 * GDN_W + AB_ROWS
        bg["w_in"] = jnp.concatenate(
            [wgrad(dhr, sv["x1"], "wgrad_in_r" + tag), wgrad(dhl, sv["x1"], "wgrad_in_l" + tag),
             wgrad(dhq, sv["x1"], "wgrad_in_q" + tag), wgrad(dab, sv["x1"], "wgrad_in_ab" + tag)[0:AB_ROWS],
             jnp.zeros((WIN_T_ROWS - used, D_MODEL), BF16)], axis=0)
        dx, act1, dhg1, dhu1, dy1, dg1, db1 = ffn_bwd(dx1, sv["r1"], sv["x0"], sv["hg1"], sv["hu1"], row(wt["ln_ffn1_g"][i]),
                                                      lw["wg1"], lw["wu1"], lw["wd1"])
        bg["ffn1_w_gate"] = wgrad(dhg1, sv["x0"], "wgrad_gate1" + tag)
        bg["ffn1_w_up"] = wgrad(dhu1, sv["x0"], "wgrad_up1" + tag)
        bg["ffn1_w_down"] = wgrad(act1, dy1, "wgrad_down1" + tag)
        g["ln_ffn1_g"], g["ln_ffn1_b"] = dg1[0], db1[0]
        grads[i] = g
        if mesh is None:
            big[i] = bg
        else:
            names = grad_names if i == 1 else grad_names[:n_late]
            mine = [bg[n] for n in names]
            theirs = sibling_send(mine, i, f"reduce_pair_send_l{i}")
            sums = [pair_add(u, v, f"reduce_pair_add_l{i}_" + n) for n, u, v in zip(names, mine, theirs)]
            if i == 1:
                pair_sums[1] = sums
            else:
                late_arrived = chip_exchange(sums, pieces[:n_late], 0, "reduce_chip_exchange_l0")
                pair_sums[0] = sums + early_sums
                big[0] = jnp.concatenate([late_arrived, early_arrived], axis=1)
    if mesh is not None:
        chip = 2 * lax.axis_index("x") + lax.axis_index("y")
        big = (big, [_own_share(pair_sums[layer], pieces, chip) for layer in range(DEPTH)])
    return loss, dx, {k: jnp.stack([grads[i][k] for i in range(DEPTH)]) for k in grads[0]}, big


def _natural_grad(name, rows):
    if name == "ple_w_proj":
        return rows.reshape(-1, PLE_DIM).T
    return rows.T if name in _TRANSPOSED else rows


_SPLIT = dict(ffn1_w_gate=2, ffn1_w_up=2, ffn1_w_down=1, w_in=2, w_out=1, ffn2_w_gate=2, ffn2_w_up=2, ffn2_w_down=1,
              ple_w_gate=1, ple_w_proj=2)
_CONV = ("lru_conv_w", "gdn_conv_w")
_WHOLE = ("ln_ffn1_g", "ln_ffn1_b", "ret_norm_g", "lru_conv_b", "lru_w_a", "lru_b_a", "lru_w_x", "lru_b_x", "lru_lambda",
          "gdn_a_log", "gdn_dt_bias", "gdn_norm_g", "ln_mix_g", "ln_mix_b", "ln_ffn2_g", "ln_ffn2_b")
_WEIGHTS = ("ln_ffn1_g", "ln_ffn1_b", "ffn1_w_gate", "ffn1_w_up", "ffn1_w_down", "w_in", "ret_norm_g", "lru_conv_w", "lru_conv_b",
            "lru_w_a", "lru_b_a", "lru_w_x", "lru_b_x", "lru_lambda", "gdn_conv_w", "gdn_a_log", "gdn_dt_bias", "gdn_norm_g",
            "w_out", "ln_mix_g", "ln_mix_b", "ffn2_w_gate", "ffn2_w_up", "ffn2_w_down", "ple_w_gate", "ple_w_proj",
            "ln_ffn2_g", "ln_ffn2_b")
_INPUTS = ("x", "p", "positions") + _WEIGHTS + ("loss_target",) + tuple("m_" + n for n in _WEIGHTS) + tuple("v_" + n for n in _WEIGHTS)

BIG_COLS = 1024
SMALL_COLS = LANES
SMALL_ROWS_MULT = 512


def _pack(arrays, dtype, cols, rows_mult):
    flat = jnp.concatenate([a.reshape(-1).astype(dtype) for a in arrays])
    rows = -(-flat.shape[0] // cols)
    rows = -(-rows // rows_mult) * rows_mult
    return jnp.pad(flat, (0, rows * cols - flat.shape[0])).reshape(rows, cols)


def _unpack(packed, shapes):
    flat = packed.reshape(-1)
    out, off = [], 0
    for shp in shapes:
        size = int(np.prod(shp))
        out.append(flat[off:off + size].reshape(shp))
        off += size
    return out


def _as2d(a):
    return a.reshape(-1, a.shape[-1])


def kernel(x, p, positions, ln_ffn1_g, ln_ffn1_b, ffn1_w_gate, ffn1_w_up, ffn1_w_down, w_in, ret_norm_g, lru_conv_w, lru_conv_b, lru_w_a, lru_b_a, lru_w_x, lru_b_x, lru_lambda, gdn_conv_w, gdn_a_log, gdn_dt_bias, gdn_norm_g, w_out, ln_mix_g, ln_mix_b, ffn2_w_gate, ffn2_w_up, ffn2_w_down, ple_w_gate, ple_w_proj, ln_ffn2_g, ln_ffn2_b, loss_target, m_ln_ffn1_g, m_ln_ffn1_b, m_ffn1_w_gate, m_ffn1_w_up, m_ffn1_w_down, m_w_in, m_ret_norm_g, m_lru_conv_w, m_lru_conv_b, m_lru_w_a, m_lru_b_a, m_lru_w_x, m_lru_b_x, m_lru_lambda, m_gdn_conv_w, m_gdn_a_log, m_gdn_dt_bias, m_gdn_norm_g, m_w_out, m_ln_mix_g, m_ln_mix_b, m_ffn2_w_gate, m_ffn2_w_up, m_ffn2_w_down, m_ple_w_gate, m_ple_w_proj, m_ln_ffn2_g, m_ln_ffn2_b, v_ln_ffn1_g, v_ln_ffn1_b, v_ffn1_w_gate, v_ffn1_w_up, v_ffn1_w_down, v_w_in, v_ret_norm_g, v_lru_conv_w, v_lru_conv_b, v_lru_w_a, v_lru_b_a, v_lru_w_x, v_lru_b_x, v_lru_lambda, v_gdn_conv_w, v_gdn_a_log, v_gdn_dt_bias, v_gdn_norm_g, v_w_out, v_ln_mix_g, v_ln_mix_b, v_ffn2_w_gate, v_ffn2_w_up, v_ffn2_w_down, v_ple_w_gate, v_ple_w_proj, v_ln_ffn2_g, v_ln_ffn2_b):
    a = dict(zip(_INPUTS, (x, p, positions, ln_ffn1_g, ln_ffn1_b, ffn1_w_gate, ffn1_w_up, ffn1_w_down, w_in, ret_norm_g, lru_conv_w, lru_conv_b, lru_w_a, lru_b_a, lru_w_x, lru_b_x, lru_lambda, gdn_conv_w, gdn_a_log, gdn_dt_bias, gdn_norm_g, w_out, ln_mix_g, ln_mix_b, ffn2_w_gate, ffn2_w_up, ffn2_w_down, ple_w_gate, ple_w_proj, ln_ffn2_g, ln_ffn2_b, loss_target, m_ln_ffn1_g, m_ln_ffn1_b, m_ffn1_w_gate, m_ffn1_w_up, m_ffn1_w_down, m_w_in, m_ret_norm_g, m_lru_conv_w, m_lru_conv_b, m_lru_w_a, m_lru_b_a, m_lru_w_x, m_lru_b_x, m_lru_lambda, m_gdn_conv_w, m_gdn_a_log, m_gdn_dt_bias, m_gdn_norm_g, m_w_out, m_ln_mix_g, m_ln_mix_b, m_ffn2_w_gate, m_ffn2_w_up, m_ffn2_w_down, m_ple_w_gate, m_ple_w_proj, m_ln_ffn2_g, m_ln_ffn2_b, v_ln_ffn1_g, v_ln_ffn1_b, v_ffn1_w_gate, v_ffn1_w_up, v_ffn1_w_down, v_w_in, v_ret_norm_g, v_lru_conv_w, v_lru_conv_b, v_lru_w_a, v_lru_b_a, v_lru_w_x, v_lru_b_x, v_lru_lambda, v_gdn_conv_w, v_gdn_a_log, v_gdn_dt_bias, v_gdn_norm_g, v_w_out, v_ln_mix_g, v_ln_mix_b, v_ffn2_w_gate, v_ffn2_w_up, v_ffn2_w_down, v_ple_w_gate, v_ple_w_proj, v_ln_ffn2_g, v_ln_ffn2_b)))
    assert len(a) == len(_INPUTS)
    core = lax.axis_index("c")
    chip = 2 * lax.axis_index("x") + lax.axis_index("y")
    big = list(_SPLIT)

    def group(layer, names, n_first):
        shards = [a[n][layer].astype(BF16) for n in names]

        def make(gathered):
            return {n: jnp.concatenate([gathered[i][k] for k in range(N_CHIPS)], axis=_SPLIT[n] - 1) for i, n in enumerate(names)}

        return shards[:n_first], shards[n_first:], make

    ffn1_0, ffn1_1, make_ffn1 = group(0, big[:3], 2)
    wt = {n: [w0, None] for n, w0 in make_ffn1(gather_layer_weights(ffn1_0, ffn1_1, "gather_weights_ffn1_l0")).items()}
    conv_g = all_gather8(_pack([a[n] for n in _CONV], F32, SMALL_COLS, 8), "gather_conv_weights")[0::2]
    conv_g = conv_g.reshape(N_CHIPS, -1)
    off = 0
    for n in _CONV:
        shp = a[n].shape
        size = int(np.prod(shp))
        parts = conv_g[:, off:off + size].reshape((N_CHIPS,) + shp)
        wt[n] = jnp.concatenate([parts[k] for k in range(N_CHIPS)], axis=2)
        off += size
    for n in _WHOLE:
        wt[n] = a[n]

    seq = a["x"].shape[1]
    mesh = dict(rest0=group(0, big[3:], 4), layer1=group(1, big, len(big) // 2))
    loss_part, dx, grads, (arrived, own) = _local_step(a["x"][0], a["p"][:, 0], a["positions"].reshape(seq, 1),
                                                       a["loss_target"][0], wt, mesh=mesh)
    loss = lax.psum(loss_part[0, 0], ("x", "y", "c"))

    my_layer_sum = sum_shares(arrived[0], arrived[1], own[0], own[1], "reduce_chip_sum")
    other_layer_sum = sibling_swap(my_layer_sum, "reduce_pair_share")
    reduced = [jnp.where(core == layer, my_layer_sum, other_layer_sum) for layer in range(DEPTH)]
    big_grads = {}
    off = 0
    for n, r, _ in _GRAD_PIECES:
        per_layer = []
        for layer in range(DEPTH):
            rows = reduced[layer][off:off + r]
            if n == "w_in":
                rows = lax.dynamic_slice_in_dim(rows, chip * (WIN_SHARD - WIN_STRIDE), WIN_SHARD, axis=0)
            per_layer.append(_natural_grad(n, rows))
        big_grads[n] = jnp.stack(per_layer)
        off += r

    small_names = list(_WHOLE) + list(_CONV)
    small_local = _pack([grads[n] for n in small_names], F32, SMALL_COLS, SMALL_ROWS_MULT)
    small_sum = sum_leading(all_gather8(small_local, "gather_small_grads"), "sum_small_grads")
    small_grads = dict(zip(small_names, _unpack(small_sum, [grads[n].shape for n in small_names])))
    for n in _CONV:
        width = a[n].shape[2]
        small_grads[n] = lax.dynamic_slice_in_dim(small_grads[n], chip * width, width, axis=2)

    new = {}
    for n in big:
        d, nm, nv = adamw(_as2d(a[n]), _as2d(big_grads[n]), _as2d(a["m_" + n]), _as2d(a["v_" + n]), "adamw_" + n)
        new[n] = tuple(t.reshape(a[n].shape) for t in (d, nm, nv))
    pk = lambda prefix: _pack([a[prefix + n] for n in small_names], F32, SMALL_COLS, SMALL_ROWS_MULT)
    pg = _pack([small_grads[n] for n in small_names], F32, SMALL_COLS, SMALL_ROWS_MULT)
    outs = adamw(pk(""), pg, pk("m_"), pk("v_"), "adamw_small")
    shapes = [a[n].shape for n in small_names]
    for n, d, nm, nv in zip(small_names, *[_unpack(o, shapes) for o in outs]):
        new[n] = (d, nm, nv)
    all_grads = {**big_grads, **small_grads}
    return (loss, dx[None], *[all_grads[n] for n in _WEIGHTS], *[new[n][0] for n in _WEIGHTS],
            *[new[n][1] for n in _WEIGHTS], *[new[n][2] for n in _WEIGHTS])
```
